```python
import math
import jax, jax.numpy as jnp
from jax import lax
import numpy as np

D_MODEL = 1024
BATCH = 8
SEQ = 2048
DEPTH = 2

MIX_WIDTH = D_MODEL
ATTN_WIDTH = MIX_WIDTH // 2
REC_WIDTH = MIX_WIDTH - ATTN_WIDTH
ATTN_HEAD_DIM = 64
ATTN_HEADS = ATTN_WIDTH // ATTN_HEAD_DIM
DILATED_PATTERNS = ((128, 1), (512, 4), (2048, 16))
ROPE_THETA = 10000.0
REC_EXPAND = 128
REC_HEADS = REC_WIDTH // REC_EXPAND
REC_HEAD_DIM = REC_WIDTH // REC_HEADS
REC_CHUNK = 64
IN_SPLIT_SIZES = (ATTN_WIDTH, ATTN_WIDTH, ATTN_WIDTH,
                  REC_HEADS * REC_EXPAND, REC_HEADS * REC_EXPAND, REC_HEADS * REC_EXPAND,
                  REC_WIDTH, REC_WIDTH)
IN_COLS = sum(IN_SPLIT_SIZES)
D_FF_DENSE = ((8 * D_MODEL // 3 + 63) // 64) * 64
N_EXPERTS = 8
TOP_K = 2
D_FF_EXPERT = 7 * D_MODEL // 2
N_DENSE = (DEPTH + 1) // 2
N_MOE = DEPTH // 2
DN_ALPHA = (2 * DEPTH) ** 0.25
DN_BETA = (8 * DEPTH) ** -0.25
LN_EPS = 1e-5
RMS_EPS = 1e-6
MASK_VALUE = -1e30
POS_OFFSET_RANGE = 4096

kernel_name = "hybrid_dilated_attn_hgrn2_moe_encoder"


def layer_norm(x, gain, bias):
    xf = x.astype(jnp.float32)
    mu = jnp.mean(xf, axis=-1, keepdims=True)
    var = jnp.mean(jnp.square(xf - mu), axis=-1, keepdims=True)
    return ((xf - mu) * lax.rsqrt(var + LN_EPS)).astype(x.dtype) * gain + bias


def rms_norm(x, gain):
    xf = x.astype(jnp.float32)
    y = xf * lax.rsqrt(jnp.mean(xf * xf, axis=-1, keepdims=True) + RMS_EPS)
    return y.astype(x.dtype) * gain


def rope(x, positions):
    half = x.shape[-1] // 2
    inv_freq = ROPE_THETA ** (-jnp.arange(half, dtype=jnp.float32) / half)
    ang = positions.astype(jnp.float32)[..., None] * inv_freq
    cos = jnp.cos(ang)[:, :, None, :]
    sin = jnp.sin(ang)[:, :, None, :]
    xf = x.astype(jnp.float32)
    x1, x2 = xf[..., :half], xf[..., half:]
    return jnp.concatenate([x1 * cos - x2 * sin, x2 * cos + x1 * sin], axis=-1)


def split_columns(proj):
    parts, start = [], 0
    for size in IN_SPLIT_SIZES:
        parts.append(proj[..., start:start + size])
        start += size
    return parts


def dilated_window_attention(q, k, v, dilation, radius):
    B, H, S, dh = q.shape
    L = S // dilation
    blk = radius
    nb = -(-L // blk)
    Lp = nb * blk

    def to_residue(t):
        t = t.reshape(B, H, L, dilation, dh).transpose(0, 1, 3, 2, 4)
        return jnp.pad(t, ((0, 0), (0, 0), (0, 0), (0, Lp - L), (0, 0)))

    def key_windows(t):
        t = jnp.pad(to_residue(t), ((0, 0), (0, 0), (0, 0), (blk, blk), (0, 0)))
        t = t.reshape(B, H, dilation, nb + 2, blk, dh)
        return jnp.concatenate([t[:, :, :, :-2], t[:, :, :, 1:-1], t[:, :, :, 2:]], axis=4)

    qb = to_residue(q).reshape(B, H, dilation, nb, blk, dh)
    kw = key_windows(k)
    vw = key_windows(v)
    scores = jnp.einsum('bhrnqd,bhrnkd->bhrnqk', qb, kw)
    q_idx = jnp.arange(nb)[:, None, None] * blk + jnp.arange(blk)[None, :, None]
    k_idx = (jnp.arange(nb)[:, None, None] - 1) * blk + jnp.arange(3 * blk)[None, None, :]
    valid = (jnp.abs(k_idx - q_idx) <= radius) & (k_idx >= 0) & (k_idx < L)
    scores = jnp.where(valid, scores, MASK_VALUE)
    m = jnp.max(scores, axis=-1, keepdims=True)
    p = jnp.exp(scores - m)
    denom = jnp.sum(p, axis=-1, keepdims=True)
    out = jnp.einsum('bhrnqk,bhrnkd->bhrnqd', p, vw) / denom
    lse = (m + jnp.log(denom))[..., 0]
    out = out.reshape(B, H, dilation, Lp, dh)[:, :, :, :L].transpose(0, 1, 3, 2, 4).reshape(B, H, S, dh)
    lse = lse.reshape(B, H, dilation, Lp)[:, :, :, :L].transpose(0, 1, 3, 2).reshape(B, H, S)
    return out, lse


def dilated_mixture_attention(q, k, v):
    outs, lses = [], []
    for window, dilation in DILATED_PATTERNS:
        o, l = dilated_window_attention(q, k, v, dilation, window // (2 * dilation))
        outs.append(o)
        lses.append(l)
    w = jax.nn.softmax(jnp.stack(lses, axis=0), axis=0)
    return jnp.einsum('pbhs,pbhsd->bhsd', w, jnp.stack(outs, axis=0))


def gla_chunk_scan(q, k, v, log_f):
    B, H, S, dk = q.shape
    dv = v.shape[-1]
    C = REC_CHUNK
    nc = S // C
    causal = jnp.tril(jnp.ones((C, C), dtype=bool))[:, :, None]

    def split(t):
        return t.reshape(B, H, nc, C, t.shape[-1]).transpose(2, 0, 1, 3, 4)

    def step(state, inp):
        qc, kc, vc, gc = inp
        b = jnp.cumsum(gc, axis=2)
        inter = jnp.einsum('bhtc,bhcv->bhtv', qc * jnp.exp(b), state)
        diff = b[:, :, :, None, :] - b[:, :, None, :, :]
        decay = jnp.exp(jnp.where(causal, diff, -jnp.inf))
        attn = jnp.einsum('bhtc,bhsc,bhtsc->bhts', qc, kc, decay)
        out = inter + jnp.einsum('bhts,bhsv->bhtv', attn, vc)
        b_last = b[:, :, -1, :]
        state = (jnp.exp(b_last)[..., None] * state
                 + jnp.einsum('bhsc,bhsv->bhcv', kc * jnp.exp(b_last[:, :, None, :] - b), vc))
        return state, out

    state0 = jnp.zeros((B, H, dk, dv), jnp.float32)
    _, out = lax.scan(step, state0, (split(q), split(k), split(v), split(log_f)))
    return out.transpose(1, 2, 0, 3, 4).reshape(B, H, S, dv)


def hgrn2_direction(q, f_logit, v, lb):
    lb = lb[None, :, None, :]
    log_f = jnp.logaddexp(jnp.log(lb), jnp.log1p(-lb) + jax.nn.log_sigmoid(f_logit))
    k = (1.0 - lb) * jax.nn.sigmoid(-f_logit)
    return gla_chunk_scan(q, k, v, log_f)


def lower_bounds(lb_logits):
    p = jax.nn.softmax(lb_logits.astype(jnp.float32), axis=0)
    cum = jnp.cumsum(p, axis=0)
    return cum - cum[0:1]


def hybrid_mixer(u, positions, w_in, w_out, attn_gain, rec_gain, lb):
    B, S, _ = u.shape
    proj = u @ w_in
    aq, ak, av, rq, rf_fwd, rf_bwd, ri, rg = split_columns(proj)

    q = rope(aq.reshape(B, S, ATTN_HEADS, ATTN_HEAD_DIM), positions) * (ATTN_HEAD_DIM ** -0.5)
    k = rope(ak.reshape(B, S, ATTN_HEADS, ATTN_HEAD_DIM), positions)
    v = av.reshape(B, S, ATTN_HEADS, ATTN_HEAD_DIM).astype(jnp.float32)
    q, k, v = (t.transpose(0, 2, 1, 3) for t in (q, k, v))
    attn = dilated_mixture_attention(q, k, v)
    attn = attn.transpose(0, 2, 1, 3).reshape(B, S, ATTN_WIDTH)
    attn = rms_norm(attn, attn_gain).astype(u.dtype)

    def rec_heads(t, d):
        return t.reshape(B, S, REC_HEADS, d).transpose(0, 2, 1, 3).astype(jnp.float32)

    def flip(t):
        return jnp.flip(t, axis=2)

    lb = lb.reshape(2, REC_HEADS, REC_EXPAND)
    rq_h = rec_heads(jax.nn.silu(rq), REC_EXPAND)
    ri_h = rec_heads(ri, REC_HEAD_DIM)
    o_fwd = hgrn2_direction(rq_h, rec_heads(rf_fwd, REC_EXPAND), ri_h, lb[0])
    o_bwd = flip(hgrn2_direction(flip(rq_h), flip(rec_heads(rf_bwd, REC_EXPAND)), flip(ri_h), lb[1]))
    rec = (o_fwd + o_bwd).transpose(0, 2, 1, 3)
    rec = rms_norm(rec, rec_gain.reshape(REC_HEADS, REC_HEAD_DIM)).reshape(B, S, REC_WIDTH)
    rec = rec.astype(u.dtype) * jax.nn.sigmoid(rg)

    return jnp.concatenate([attn, rec], axis=-1) @ w_out


def swiglu(u, w_gate, w_up, w_down):
    return (jax.nn.silu(u @ w_gate) * (u @ w_up)) @ w_down


def moe_swiglu(u, w_router, w_gate, w_up, w_down):
    B, S, D = u.shape
    t = u.reshape(B * S, D)
    logits = (t @ w_router).astype(jnp.float32)
    top_vals, top_idx = lax.top_k(logits, TOP_K)
    top_w = jax.nn.softmax(top_vals, axis=-1)
    combine = jnp.einsum('tk,tke->te', top_w, jax.nn.one_hot(top_idx, N_EXPERTS, dtype=jnp.float32))
    combine = combine.astype(u.dtype)
    out = jnp.zeros_like(t)
    for e in range(N_EXPERTS):
        out = out + combine[:, e:e + 1] * swiglu(t, w_gate[e], w_up[e], w_down[e])
    return out.reshape(B, S, D)


def ada_modulation(c_act, w_ada, b_ada):
    ada = (c_act @ w_ada + b_ada)[:, None, :]
    return ada[..., :D_MODEL], ada[..., D_MODEL:2 * D_MODEL], ada[..., 2 * D_MODEL:]


def setup_inputs(seed: int = 0) -> dict:
    key = jax.random.key(seed)
    ks = jax.random.split(key, 20)

    def nrm(k, shape, s):
        return jax.random.normal(k, shape, jnp.float32) * s

    x = nrm(ks[0], (BATCH, SEQ, D_MODEL), 1.0)
    c = nrm(ks[1], (BATCH, D_MODEL), 1.0)
    positions = (jax.random.randint(ks[2], (BATCH, 1), 0, POS_OFFSET_RANGE, jnp.int32)
                 + jnp.arange(SEQ, dtype=jnp.int32)[None, :])
    w_in = nrm(ks[3], (DEPTH, D_MODEL, IN_COLS), D_MODEL ** -0.5)
    w_out = nrm(ks[4], (DEPTH, MIX_WIDTH, D_MODEL), MIX_WIDTH ** -0.5 * DN_BETA)
    attn_norm_gain = 1.0 + nrm(ks[5], (DEPTH, ATTN_WIDTH), 0.02)
    rec_norm_gain = 1.0 + nrm(ks[6], (DEPTH, REC_WIDTH), 0.02)
    rec_lb_logits = nrm(ks[7], (DEPTH, 2, REC_WIDTH), 0.5)
    ada_w = nrm(ks[8], (DEPTH, 2, D_MODEL, 3 * D_MODEL), 0.01)
    ada_b = nrm(ks[9], (DEPTH, 2, 3 * D_MODEL), 0.02)
    ln_gain = 1.0 + nrm(ks[10], (DEPTH, 2, D_MODEL), 0.02)
    ln_bias = nrm(ks[11], (DEPTH, 2, D_MODEL), 0.02)
    ffn_w_gate = nrm(ks[12], (N_DENSE, D_MODEL, D_FF_DENSE), D_MODEL ** -0.5)
    ffn_w_up = nrm(ks[13], (N_DENSE, D_MODEL, D_FF_DENSE), D_MODEL ** -0.5)
    ffn_w_down = nrm(ks[14], (N_DENSE, D_FF_DENSE, D_MODEL), D_FF_DENSE ** -0.5 * DN_BETA)
    moe_router = nrm(ks[15], (N_MOE, D_MODEL, N_EXPERTS), D_MODEL ** -0.5)
    moe_w_gate = nrm(ks[16], (N_MOE, N_EXPERTS, D_MODEL, D_FF_EXPERT), D_MODEL ** -0.5)
    moe_w_up = nrm(ks[17], (N_MOE, N_EXPERTS, D_MODEL, D_FF_EXPERT), D_MODEL ** -0.5)
    moe_w_down = nrm(ks[18], (N_MOE, N_EXPERTS, D_FF_EXPERT, D_MODEL), D_FF_EXPERT ** -0.5 * DN_BETA)
    return {"x": x, "c": c, "positions": positions, "w_in": w_in, "w_out": w_out,
            "attn_norm_gain": attn_norm_gain, "rec_norm_gain": rec_norm_gain,
            "rec_lb_logits": rec_lb_logits, "ada_w": ada_w, "ada_b": ada_b,
            "ln_gain": ln_gain, "ln_bias": ln_bias,
            "ffn_w_gate": ffn_w_gate, "ffn_w_up": ffn_w_up, "ffn_w_down": ffn_w_down,
            "moe_router": moe_router, "moe_w_gate": moe_w_gate, "moe_w_up": moe_w_up,
            "moe_w_down": moe_w_down}


def reference(x, c, positions, w_in, w_out, attn_norm_gain, rec_norm_gain, rec_lb_logits,
              ada_w, ada_b, ln_gain, ln_bias, ffn_w_gate, ffn_w_up, ffn_w_down,
              moe_router, moe_w_gate, moe_w_up, moe_w_down):
    lb_all = lower_bounds(rec_lb_logits)
    c_act = jax.nn.silu(c)
    for layer in range(DEPTH):
        shift, scale, gate = ada_modulation(c_act, ada_w[layer, 0], ada_b[layer, 0])
        u = x * (1.0 + scale) + shift
        y = hybrid_mixer(u, positions, w_in[layer], w_out[layer],
                         attn_norm_gain[layer], rec_norm_gain[layer], lb_all[layer])
        x = layer_norm(DN_ALPHA * x + (1.0 + gate) * y, ln_gain[layer, 0], ln_bias[layer, 0])
        shift, scale, gate = ada_modulation(c_act, ada_w[layer, 1], ada_b[layer, 1])
        u = x * (1.0 + scale) + shift
        j = layer // 2
        if layer % 2 == 0:
            y = swiglu(u, ffn_w_gate[j], ffn_w_up[j], ffn_w_down[j])
        else:
            y = moe_swiglu(u, moe_router[j], moe_w_gate[j], moe_w_up[j], moe_w_down[j])
        x = layer_norm(DN_ALPHA * x + (1.0 + gate) * y, ln_gain[layer, 1], ln_bias[layer, 1])
    return x
```

```python
import functools

import jax
import jax.numpy as jnp
from jax import lax
from jax.experimental import pallas as pl
from jax.experimental.pallas import tpu as pltpu

F32 = jnp.float32
BF16 = jnp.bfloat16
I32 = jnp.int32

LANES = 128
V7X_VMEM_LIMIT_BYTES = 56 * 1024 * 1024

ATTN_WIDTH = 512
ATTN_HEAD_DIM = 64
DILATIONS = (1, 4, 16)
ATTN_RADIUS = 64
ROPE_THETA = 10000.0
REC_WIDTH = 512
REC_DIM = 128
REC_CHUNK = 64
TOP_K = 2
LN_EPS = 1e-5
RMS_EPS = 1e-6
MASK_VALUE = -1e30
DECAY_EXP_CLAMP = 80.0


def _params(semantics):
    return pltpu.CompilerParams(dimension_semantics=semantics,
                                vmem_limit_bytes=V7X_VMEM_LIMIT_BYTES)


def _sigmoid(x):
    return 1.0 / (1.0 + jnp.exp(-x))


def _layer_norm(z, gain, bias):
    mu = jnp.mean(z, axis=-1, keepdims=True)
    zc = z - mu
    var = jnp.mean(zc * zc, axis=-1, keepdims=True)
    return zc * lax.rsqrt(var + LN_EPS) * gain + bias


def _ada_kernel(c_ref, w_ref, b_ref, o_ref):
    c = c_ref[...]
    o_ref[...] = jnp.dot(c * _sigmoid(c), w_ref[...], precision=lax.Precision.HIGHEST,
                         preferred_element_type=F32) + b_ref[...]


def _ada_modulation(c, ada_w, ada_b):
    B, D = c.shape
    n = ada_w.shape[0] * ada_w.shape[1]
    n3 = ada_w.shape[-1]
    tn = 1024
    return pl.pallas_call(
        _ada_kernel,
        grid=(n, n3 // tn),
        in_specs=[pl.BlockSpec((B, D), lambda i, j: (0, 0)),
                  pl.BlockSpec((None, D, tn), lambda i, j: (i, 0, j)),
                  pl.BlockSpec((None, 1, tn), lambda i, j: (i, 0, j))],
        out_specs=pl.BlockSpec((None, B, tn), lambda i, j: (i, 0, j)),
        out_shape=jax.ShapeDtypeStruct((n, B, n3), F32),
        compiler_params=_params(("arbitrary", "arbitrary")),
        name="ada_modulation",
    )(c, ada_w.reshape(n, D, n3), ada_b.reshape(n, 1, n3))


def _inproj_kernel(x_ref, sc_ref, sh_ref, pos_ref, invf_ref, w_ref, o_ref, u_ref, *, tn):
    j = pl.program_id(2)

    @pl.when(j == 0)
    def _():
        u_ref[...] = (x_ref[...] * (1.0 + sc_ref[...]) + sh_ref[...]).astype(BF16)

    acc = jnp.dot(u_ref[...], w_ref[...], preferred_element_type=F32)

    @pl.when(j > 0)
    def _():
        o_ref[...] = acc

    @pl.when(j == 0)
    def _():
        ang = pos_ref[...] * invf_ref[...]
        cos = jnp.cos(ang)
        sin = jnp.sin(ang)
        lane = lax.broadcasted_iota(I32, (1, LANES), 1)
        first_half = (lane % ATTN_HEAD_DIM) < (ATTN_HEAD_DIM // 2)
        sin_signed = jnp.where(first_half, -sin, sin)
        q_scale = ATTN_HEAD_DIM ** -0.5
        for kk in range(tn // LANES):
            c = acc[:, kk * LANES:(kk + 1) * LANES]
            rot = jnp.where(first_half, pltpu.roll(c, LANES - 32, 1), pltpu.roll(c, 32, 1))
            r = c * cos + rot * sin_signed
            if kk * LANES < ATTN_WIDTH:
                r = r * q_scale
            o_ref[:, kk * LANES:(kk + 1) * LANES] = r


def _in_projection(x, scale, shift, pos_f, inv_freq_lanes, w_in_bf16, *, tm):
    B, S, D = x.shape
    n_cols = w_in_bf16.shape[1]
    tn = 2 * ATTN_WIDTH
    assert S % tm == 0 and n_cols % tn == 0
    return pl.pallas_call(
        functools.partial(_inproj_kernel, tn=tn),
        grid=(B, S // tm, n_cols // tn),
        in_specs=[pl.BlockSpec((None, tm, D), lambda b, i, j: (b, i, 0)),
                  pl.BlockSpec((None, 1, D), lambda b, i, j: (b, 0, 0)),
                  pl.BlockSpec((None, 1, D), lambda b, i, j: (b, 0, 0)),
                  pl.BlockSpec((None, tm, 1), lambda b, i, j: (b, i, 0)),
                  pl.BlockSpec((1, LANES), lambda b, i, j: (0, 0)),
                  pl.BlockSpec((D, tn), lambda b, i, j: (0, j))],
        out_specs=pl.BlockSpec((None, tm, tn), lambda b, i, j: (b, i, j)),
        out_shape=jax.ShapeDtypeStruct((B, S, n_cols), F32),
        scratch_shapes=[pltpu.VMEM((tm, D), BF16)],
        compiler_params=_params(("arbitrary", "arbitrary", "arbitrary")),
        name="in_projection",
    )(x, scale, shift, pos_f, inv_freq_lanes, w_in_bf16)


def _attn_kernel(q_ref, k_ref, v_ref, o_ref, qs, ks, vs, pm, plr, pa, nm, nl, na, *, S):
    lane = lax.broadcasted_iota(I32, (1, LANES), 1)
    head0 = lane < ATTN_HEAD_DIM

    for p, d in enumerate(DILATIONS):
        L = S // d
        tq = min(128, L)
        W = min(2 * tq, L)
        nb = L // tq

        for r in range(d):
            rows = pl.ds(r, L, stride=d) if d > 1 else pl.ds(0, L)
            qs[r * L:(r + 1) * L, :] = q_ref[rows, :].astype(BF16)
            ks[r * L:(r + 1) * L, :] = k_ref[rows, :].astype(BF16)
            vs[r * L:(r + 1) * L, :] = v_ref[rows, :].astype(BF16)

        def block(g, carry, L=L, tq=tq, W=W, nb=nb):
            r = g // nb
            n = g - r * nb
            base = r * L
            q0 = pl.multiple_of(base + n * tq, tq)
            ws = jnp.clip(n * tq - ATTN_RADIUS, 0, L - W)
            k0 = pl.multiple_of(base + ws, 16)
            qb = qs[pl.ds(q0, tq), :]
            kw = ks[pl.ds(k0, W), :]
            vw = vs[pl.ds(k0, W), :]
            qi = n * tq + lax.broadcasted_iota(I32, (tq, W), 0)
            kj = ws + lax.broadcasted_iota(I32, (tq, W), 1)
            valid = jnp.abs(qi - kj) <= ATTN_RADIUS
            stats = []
            for h in range(2):
                qh = jnp.where(head0 if h == 0 else jnp.logical_not(head0), qb, jnp.zeros_like(qb))
                s = lax.dot_general(qh, kw, (((1,), (1,)), ((), ())), preferred_element_type=F32)
                s = jnp.where(valid, s, MASK_VALUE)
                m = jnp.max(s, axis=1, keepdims=True)
                e = jnp.exp(s - m)
                l = jnp.sum(e, axis=1, keepdims=True)
                pv = jnp.dot(e.astype(BF16), vw, preferred_element_type=F32)
                stats.append((m, l, pv))
            (m0, l0, a0), (m1, l1, a1) = stats
            pm[pl.ds(q0, tq), :] = jnp.where(head0, m0, m1)
            plr[pl.ds(q0, tq), :] = jnp.where(head0, l0, l1)
            pa[pl.ds(q0, tq), :] = jnp.where(head0, a0, a1)
            return carry

        lax.fori_loop(0, d * nb, block, 0)

        for r in range(d):
            rows = pl.ds(r, L, stride=d) if d > 1 else pl.ds(0, L)
            nm[p, rows, :] = pm[r * L:(r + 1) * L, :]
            nl[p, rows, :] = plr[r * L:(r + 1) * L, :]
            na[p, rows, :] = pa[r * L:(r + 1) * L, :]

    m_all = jnp.maximum(jnp.maximum(nm[0], nm[1]), nm[2])
    num = jnp.zeros((S, LANES), F32)
    den = jnp.zeros((S, LANES), F32)
    for p in range(len(DILATIONS)):
        w = jnp.exp(nm[p] - m_all)
        num = num + w * na[p]
        den = den + w * nl[p]
    o_ref[...] = num / den


def _dilated_attention(proj):
    B, S, _ = proj.shape
    n_pairs = ATTN_WIDTH // LANES
    assert S % (16 * DILATIONS[-1]) == 0
    blk = lambda off: pl.BlockSpec((None, S, LANES), lambda b, h: (b, 0, off + h))
    return pl.pallas_call(
        functools.partial(_attn_kernel, S=S),
        grid=(B, n_pairs),
        in_specs=[blk(0), blk(n_pairs), blk(2 * n_pairs)],
        out_specs=pl.BlockSpec((None, S, LANES), lambda b, h: (b, 0, h)),
        out_shape=jax.ShapeDtypeStruct((B, S, ATTN_WIDTH), F32),
        scratch_shapes=[pltpu.VMEM((S, LANES), BF16)] * 3
                       + [pltpu.VMEM((S, LANES), F32)] * 3
                       + [pltpu.VMEM((len(DILATIONS), S, LANES), F32)] * 3,
        compiler_params=_params(("arbitrary", "arbitrary")),
        name="dilated_attention",
    )(proj, proj, proj)


REC_GROUP_CHUNKS = 4
REC_GROUP_ROWS = REC_GROUP_CHUNKS * REC_CHUNK


def _hgrn_kernel(rq_ref, zf_ref, zb_ref, ri_ref, rg_ref, lb_ref, gain_ref, o_ref,
                 q_s, vt_s, oi_s, qp_s, ut_s, dd_s, st_s, os_s, *, S):
    C, G, GR = REC_CHUNK, REC_GROUP_CHUNKS, REC_GROUP_ROWS
    nc, ng = S // C, S // GR

    row = lax.broadcasted_iota(I32, (GR, GR), 0)
    col = lax.broadcasted_iota(I32, (GR, GR), 1)
    same_chunk = (row // C) == (col // C)
    t_i = lax.broadcasted_iota(I32, (C, C), 0)
    s_i = lax.broadcasted_iota(I32, (C, C), 1)

    def prep(i, carry):
        r0 = pl.multiple_of(i * GR, GR)
        rq = rq_ref[pl.ds(r0, GR), :]
        q_s[pl.ds(r0, GR), :] = rq * _sigmoid(rq)
        v = ri_ref[pl.ds(r0, GR), :]
        for c in range(G):
            vt_s[i * G + c] = v[c * C:(c + 1) * C, :].T.astype(BF16)
        return carry

    lax.fori_loop(0, ng, prep, 0)

    for direction in range(2):
        fwd = direction == 0
        z_ref = zf_ref if fwd else zb_ref
        cum = jnp.where(same_chunk & ((col <= row) if fwd else (col >= row)), 1.0, 0.0).astype(BF16)
        keep = (s_i <= t_i) if fwd else (s_i >= t_i)
        mid_row = C // 2 - 1 if fwd else C // 2
        last_row = C - 1 if fwd else 0

        def phase_a(i, carry, fwd=fwd, z_ref=z_ref, cum=cum, keep=keep,
                    mid_row=mid_row, last_row=last_row, direction=direction):
            r0 = pl.multiple_of(i * GR, GR)
            z = z_ref[pl.ds(r0, GR), :]
            lb = lb_ref[direction:direction + 1, :]
            e = jnp.exp(-jnp.abs(z))
            log_sig = jnp.minimum(z, 0.0) - jnp.log1p(e)
            t = jnp.log1p(-lb) + log_sig
            a = jnp.log(lb)
            g = jnp.maximum(a, t) + jnp.log1p(jnp.exp(-jnp.abs(a - t)))
            kk = (1.0 - lb) * (jnp.where(z > 0, e, 1.0) / (1.0 + e))
            g1 = g.astype(BF16)
            r1 = g - g1.astype(F32)
            g2 = r1.astype(BF16)
            g3 = (r1 - g2.astype(F32)).astype(BF16)
            bc = (jnp.dot(cum, g1, preferred_element_type=F32)
                  + jnp.dot(cum, g2, preferred_element_type=F32)
                  + jnp.dot(cum, g3, preferred_element_type=F32))
            b3 = bc.reshape(G, C, REC_DIM)
            mid = b3[:, mid_row:mid_row + 1, :]
            last = b3[:, last_row:last_row + 1, :]
            e3 = b3 - mid
            q3 = q_s[pl.ds(r0, GR), :].reshape(G, C, REC_DIM)
            k3 = kk.reshape(G, C, REC_DIM)
            v3 = ri_ref[pl.ds(r0, GR), :].reshape(G, C, REC_DIM).astype(BF16)
            qt = (q3 * jnp.exp(jnp.minimum(e3, DECAY_EXP_CLAMP))).astype(BF16)
            kt = (k3 * jnp.exp(jnp.minimum(-e3, DECAY_EXP_CLAMP))).astype(BF16)
            att = jnp.einsum('gtc,gsc->gts', qt, kt, preferred_element_type=F32)
            att = jnp.where(keep[None], att, 0.0).astype(BF16)
            oi = jnp.einsum('gts,gsv->gtv', att, v3, preferred_element_type=F32)
            oi_s[pl.ds(r0, GR), :] = oi.reshape(GR, REC_DIM)
            qp_s[pl.ds(r0, GR), :] = (q3 * jnp.exp(b3)).astype(BF16).reshape(GR, REC_DIM)
            kp = (k3 * jnp.exp(last - b3)).astype(BF16)
            c0 = pl.multiple_of(i * G, G)
            ut_s[pl.ds(c0, G)] = jnp.einsum('gvs,gsc->gvc', vt_s[pl.ds(c0, G)], kp,
                                            preferred_element_type=F32)
            dd_s[pl.ds(c0, G)] = jnp.exp(last)
            return carry

        lax.fori_loop(0, ng, phase_a, 0)

        def scan(c, st, fwd=fwd):
            idx = c if fwd else nc - 1 - c
            st_s[idx] = st.astype(BF16)
            return dd_s[idx] * st + ut_s[idx]

        lax.fori_loop(0, nc, scan, jnp.zeros((REC_DIM, REC_DIM), F32))

        def phase_c(i, carry, fwd=fwd):
            r0 = pl.multiple_of(i * GR, GR)
            c0 = pl.multiple_of(i * G, G)
            qp3 = qp_s[pl.ds(r0, GR), :].reshape(G, C, REC_DIM)
            oo = jnp.einsum('gtc,gvc->gtv', qp3, st_s[pl.ds(c0, G)], preferred_element_type=F32)
            tot = oi_s[pl.ds(r0, GR), :] + oo.reshape(GR, REC_DIM)
            if fwd:
                os_s[pl.ds(r0, GR), :] = tot
            else:
                os_s[pl.ds(r0, GR), :] = os_s[pl.ds(r0, GR), :] + tot
            return carry

        lax.fori_loop(0, ng, phase_c, 0)

    o = os_s[...]
    o = o * lax.rsqrt(jnp.mean(o * o, axis=-1, keepdims=True) + RMS_EPS) * gain_ref[...]
    o_ref[...] = o * _sigmoid(rg_ref[...])


def _hgrn2(proj, lb, rec_gain):
    B, S, _ = proj.shape
    nh = REC_WIDTH // REC_DIM
    assert S % REC_GROUP_ROWS == 0
    col0 = 3 * ATTN_WIDTH // LANES
    blk = lambda k: pl.BlockSpec((None, S, REC_DIM), lambda b, h: (b, 0, col0 + k * nh + h))
    nc = S // REC_CHUNK
    return pl.pallas_call(
        functools.partial(_hgrn_kernel, S=S),
        grid=(B, nh),
        in_specs=[blk(0), blk(1), blk(2), blk(3), blk(4),
                  pl.BlockSpec((2, REC_DIM), lambda b, h: (0, h)),
                  pl.BlockSpec((1, REC_DIM), lambda b, h: (0, h))],
        out_specs=pl.BlockSpec((None, S, REC_DIM), lambda b, h: (b, 0, h)),
        out_shape=jax.ShapeDtypeStruct((B, S, REC_WIDTH), F32),
        scratch_shapes=[pltpu.VMEM((S, REC_DIM), F32),
                        pltpu.VMEM((nc, REC_DIM, REC_CHUNK), BF16),
                        pltpu.VMEM((S, REC_DIM), F32),
                        pltpu.VMEM((S, REC_DIM), BF16),
                        pltpu.VMEM((nc, REC_DIM, REC_DIM), F32),
                        pltpu.VMEM((nc, 1, REC_DIM), F32),
                        pltpu.VMEM((nc, REC_DIM, REC_DIM), BF16),
                        pltpu.VMEM((S, REC_DIM), F32)],
        compiler_params=_params(("arbitrary", "arbitrary")),
        name="hgrn2",
    )(proj, proj, proj, proj, proj, lb, rec_gain)


def _outproj_kernel(a_ref, r_ref, x_ref, gate_ref, ag_ref, w_ref, lg_ref, lb_ref, o_ref, *, alpha):
    a = a_ref[...]
    an = a * lax.rsqrt(jnp.mean(a * a, axis=-1, keepdims=True) + RMS_EPS) * ag_ref[...]
    y = (jnp.dot(an.astype(BF16), w_ref[:ATTN_WIDTH, :], preferred_element_type=F32)
         + jnp.dot(r_ref[...].astype(BF16), w_ref[ATTN_WIDTH:, :], preferred_element_type=F32))
    z = alpha * x_ref[...] + (1.0 + gate_ref[...]) * y
    o_ref[...] = _layer_norm(z, lg_ref[...], lb_ref[...])


def _out_projection(attn, rec, x, gate, attn_gain, w_out_bf16, ln_g, ln_b, *, alpha, tm):
    B, S, D = x.shape
    row = lambda w: pl.BlockSpec((None, tm, w), lambda b, i: (b, i, 0))
    vec = lambda w: pl.BlockSpec((1, w), lambda b, i: (0, 0))
    return pl.pallas_call(
        functools.partial(_outproj_kernel, alpha=alpha),
        grid=(B, S // tm),
        in_specs=[row(ATTN_WIDTH), row(REC_WIDTH), row(D),
                  pl.BlockSpec((None, 1, D), lambda b, i: (b, 0, 0)),
                  vec(ATTN_WIDTH),
                  pl.BlockSpec((ATTN_WIDTH + REC_WIDTH, D), lambda b, i: (0, 0)),
                  vec(D), vec(D)],
        out_specs=row(D),
        out_shape=jax.ShapeDtypeStruct((B, S, D), F32),
        compiler_params=_params(("arbitrary", "arbitrary")),
        name="out_projection",
    )(attn, rec, x, gate, attn_gain, w_out_bf16, ln_g, ln_b)


def _ffn_kernel(x_ref, sc_ref, sh_ref, gate_ref, wg_ref, wu_ref, wd_ref, lg_ref, lb_ref, o_ref,
                *, alpha, tc):
    x = x_ref[...]
    u = (x * (1.0 + sc_ref[...]) + sh_ref[...]).astype(BF16)
    acc = jnp.zeros(x.shape, F32)
    for c in range(wg_ref.shape[1] // tc):
        cols = slice(c * tc, (c + 1) * tc)
        g = jnp.dot(u, wg_ref[:, cols], preferred_element_type=F32)
        up = jnp.dot(u, wu_ref[:, cols], preferred_element_type=F32)
        h = (g * _sigmoid(g) * up).astype(BF16)
        acc = acc + jnp.dot(h, wd_ref[cols, :], preferred_element_type=F32)
    z = alpha * x + (1.0 + gate_ref[...]) * acc
    o_ref[...] = _layer_norm(z, lg_ref[...], lb_ref[...])


def _dense_ffn(x, scale, shift, gate, wg, wu, wd, ln_g, ln_b, *, alpha, tm, tc):
    B, S, D = x.shape
    ffp = wg.shape[1]
    assert ffp % tc == 0
    row = pl.BlockSpec((None, tm, D), lambda b, i: (b, i, 0))
    mod = pl.BlockSpec((None, 1, D), lambda b, i: (b, 0, 0))
    vec = pl.BlockSpec((1, D), lambda b, i: (0, 0))
    return pl.pallas_call(
        functools.partial(_ffn_kernel, alpha=alpha, tc=tc),
        grid=(B, S // tm),
        in_specs=[row, mod, mod, mod,
                  pl.BlockSpec((D, ffp), lambda b, i: (0, 0)),
                  pl.BlockSpec((D, ffp), lambda b, i: (0, 0)),
                  pl.BlockSpec((ffp, D), lambda b, i: (0, 0)),
                  vec, vec],
        out_specs=row,
        out_shape=jax.ShapeDtypeStruct((B, S, D), F32),
        compiler_params=_params(("arbitrary", "arbitrary")),
        name="dense_ffn",
    )(x, scale, shift, gate, wg, wu, wd, ln_g, ln_b)


META_E0, META_E1, META_W0, META_W1, META_R0, META_R1 = range(6)


def _router_kernel(x_ref, sc_ref, sh_ref, wr_ref, u_ref, meta_ref, cnt_ref, carry_ref, *, n_exp):
    tm = x_ref.shape[0]

    @pl.when((pl.program_id(0) == 0) & (pl.program_id(1) == 0))
    def _():
        carry_ref[...] = jnp.zeros_like(carry_ref)

    u = x_ref[...] * (1.0 + sc_ref[...]) + sh_ref[...]
    u_ref[...] = u
    logits = jnp.dot(u, wr_ref[...], precision=lax.Precision.HIGHEST, preferred_element_type=F32)
    lane = lax.broadcasted_iota(I32, (tm, LANES), 1)
    neg = -jnp.inf
    l1 = jnp.where(lane < n_exp, logits, neg)
    m1 = jnp.max(l1, axis=1, keepdims=True)
    i1 = jnp.min(jnp.where(l1 == m1, lane, LANES), axis=1, keepdims=True)
    l2 = jnp.where(lane == i1, neg, l1)
    m2 = jnp.max(l2, axis=1, keepdims=True)
    i2 = jnp.min(jnp.where(l2 == m2, lane, LANES), axis=1, keepdims=True)
    e = jnp.exp(m2 - m1)
    w1 = 1.0 / (1.0 + e)
    w2 = e * w1
    sel = jnp.where((lane == i1) | (lane == i2), 1.0, 0.0)
    before = (lax.broadcasted_iota(I32, (tm, tm), 1) < lax.broadcasted_iota(I32, (tm, tm), 0))
    ranks = jnp.dot(jnp.where(before, 1.0, 0.0).astype(BF16), sel.astype(BF16),
                    preferred_element_type=F32) + carry_ref[...]
    r1 = jnp.sum(jnp.where(lane == i1, ranks, 0.0), axis=1, keepdims=True)
    r2 = jnp.sum(jnp.where(lane == i2, ranks, 0.0), axis=1, keepdims=True)
    carry_ref[...] = carry_ref[...] + jnp.sum(sel, axis=0, keepdims=True)
    cnt_ref[...] = carry_ref[...]
    meta = jnp.zeros((tm, LANES), F32)
    for k, val in ((META_E0, i1.astype(F32)), (META_E1, i2.astype(F32)), (META_W0, w1),
                   (META_W1, w2), (META_R0, r1), (META_R1, r2)):
        meta = jnp.where(lane == k, val, meta)
    meta_ref[...] = meta


def _router(x, scale, shift, w_router_lanes, *, n_exp, tm):
    B, S, D = x.shape
    row = pl.BlockSpec((None, tm, D), lambda b, i: (b, i, 0))
    mod = pl.BlockSpec((None, 1, D), lambda b, i: (b, 0, 0))
    return pl.pallas_call(
        functools.partial(_router_kernel, n_exp=n_exp),
        grid=(B, S // tm),
        in_specs=[row, mod, mod, pl.BlockSpec((D, LANES), lambda b, i: (0, 0))],
        out_specs=[row,
                   pl.BlockSpec((None, tm, LANES), lambda b, i: (b, i, 0)),
                   pl.BlockSpec((1, LANES), lambda b, i: (0, 0))],
        out_shape=[jax.ShapeDtypeStruct((B, S, D), F32),
                   jax.ShapeDtypeStruct((B, S, LANES), F32),
                   jax.ShapeDtypeStruct((1, LANES), F32)],
        scratch_shapes=[pltpu.VMEM((1, LANES), F32)],
        compiler_params=_params(("arbitrary", "arbitrary")),
        name="moe_router",
    )(x, scale, shift, w_router_lanes)


def _row_copy(src_hbm, row, dst, dst_row, sem):
    return pltpu.make_async_copy(src_hbm.at[pl.ds(row, 1)], dst.at[pl.ds(dst_row, 1)], sem)


def _gather_kernel(src_ref, u_hbm, o_ref, sem):
    n = o_ref.shape[0]

    def issue(r, carry):
        _row_copy(u_hbm, src_ref[0, r], o_ref, r, sem).start()
        return carry

    def drain(r, carry):
        _row_copy(u_hbm, 0, o_ref, r, sem).wait()
        return carry

    lax.fori_loop(0, n, issue, 0)
    lax.fori_loop(0, n, drain, 0)


def _gather_rows(u_flat, src_tiles):
    n_tiles, _, rows = src_tiles.shape
    D = u_flat.shape[1]
    return pl.pallas_call(
        _gather_kernel,
        grid=(n_tiles,),
        in_specs=[pl.BlockSpec((None, 1, rows), lambda i: (i, 0, 0), memory_space=pltpu.SMEM),
                  pl.BlockSpec(memory_space=pl.ANY)],
        out_specs=pl.BlockSpec((rows, D), lambda i: (i, 0)),
        out_shape=jax.ShapeDtypeStruct((n_tiles * rows, D), F32),
        scratch_shapes=[pltpu.SemaphoreType.DMA(())],
        compiler_params=_params(("arbitrary",)),
        name="moe_gather",
    )(src_tiles, u_flat)


def _expert_kernel(te_ref, na_ref, xs_ref, wg_ref, wu_ref, wd_ref, o_ref, xb_ref, acc_ref):
    i = pl.program_id(0)
    j = pl.program_id(1)
    last = pl.num_programs(1) - 1
    active = i < na_ref[0]

    @pl.when(active & (j == 0))
    def _():
        xb_ref[...] = xs_ref[...].astype(BF16)
        acc_ref[...] = jnp.zeros_like(acc_ref)

    @pl.when(active)
    def _():
        xb = xb_ref[...]
        g = jnp.dot(xb, wg_ref[...].astype(BF16), preferred_element_type=F32)
        up = jnp.dot(xb, wu_ref[...].astype(BF16), preferred_element_type=F32)
        h = (g * _sigmoid(g) * up).astype(BF16)
        acc_ref[...] += jnp.dot(h, wd_ref[...].astype(BF16), preferred_element_type=F32)

    @pl.when(active & (j == last))
    def _():
        o_ref[...] = acc_ref[...]

    @pl.when(jnp.logical_not(active) & (j == last))
    def _():
        o_ref[...] = jnp.zeros_like(o_ref)


def _expert_ffn(xs, tile_expert, n_active, wg, wu, wd, *, tm, tf):
    n_rows, D = xs.shape
    ff = wg.shape[2]
    assert n_rows % tm == 0 and ff % tf == 0
    nj = ff // tf

    def jj(i, j, na):
        return jnp.where(i < na[0], j, nj - 1)

    grid_spec = pltpu.PrefetchScalarGridSpec(
        num_scalar_prefetch=2,
        grid=(n_rows // tm, nj),
        in_specs=[pl.BlockSpec((tm, D), lambda i, j, te, na: (i, 0)),
                  pl.BlockSpec((None, D, tf), lambda i, j, te, na: (te[i], 0, jj(i, j, na))),
                  pl.BlockSpec((None, D, tf), lambda i, j, te, na: (te[i], 0, jj(i, j, na))),
                  pl.BlockSpec((None, tf, D), lambda i, j, te, na: (te[i], jj(i, j, na), 0))],
        out_specs=pl.BlockSpec((tm, D), lambda i, j, te, na: (i, 0)),
        scratch_shapes=[pltpu.VMEM((tm, D), BF16), pltpu.VMEM((tm, D), F32)],
    )
    return pl.pallas_call(
        _expert_kernel,
        grid_spec=grid_spec,
        out_shape=jax.ShapeDtypeStruct((n_rows, D), F32),
        compiler_params=_params(("arbitrary", "arbitrary")),
        name="moe_experts",
    )(tile_expert, n_active, xs, wg, wu, wd)


def _combine_kernel(d0_ref, d1_ref, ys_hbm, meta_ref, x_ref, gate_ref, lg_ref, lb_ref, o_ref,
                    buf, sem, *, alpha):
    tm = x_ref.shape[0]

    def issue(r, carry):
        _row_copy(ys_hbm, d0_ref[0, r], buf.at[0], r, sem).start()
        _row_copy(ys_hbm, d1_ref[0, r], buf.at[1], r, sem).start()
        return carry

    def drain(r, carry):
        _row_copy(ys_hbm, 0, buf.at[0], r, sem).wait()
        _row_copy(ys_hbm, 0, buf.at[1], r, sem).wait()
        return carry

    lax.fori_loop(0, tm, issue, 0)
    lax.fori_loop(0, tm, drain, 0)
    meta = meta_ref[...]
    y = meta[:, META_W0:META_W0 + 1] * buf[0] + meta[:, META_W1:META_W1 + 1] * buf[1]
    z = alpha * x_ref[...] + (1.0 + gate_ref[...]) * y
    o_ref[...] = _layer_norm(z, lg_ref[...], lb_ref[...])


def _combine(ys, d0_tiles, d1_tiles, meta, x, gate, ln_g, ln_b, *, alpha, tm):
    B, S, D = x.shape
    nt = S // tm
    ids = pl.BlockSpec((None, 1, tm), lambda b, i: (b * nt + i, 0, 0), memory_space=pltpu.SMEM)
    row = pl.BlockSpec((None, tm, D), lambda b, i: (b, i, 0))
    vec = pl.BlockSpec((1, D), lambda b, i: (0, 0))
    return pl.pallas_call(
        functools.partial(_combine_kernel, alpha=alpha),
        grid=(B, nt),
        in_specs=[ids, ids, pl.BlockSpec(memory_space=pl.ANY),
                  pl.BlockSpec((None, tm, LANES), lambda b, i: (b, i, 0)),
                  row, pl.BlockSpec((None, 1, D), lambda b, i: (b, 0, 0)), vec, vec],
        out_specs=row,
        out_shape=jax.ShapeDtypeStruct((B, S, D), F32),
        scratch_shapes=[pltpu.VMEM((2, tm, D), F32), pltpu.SemaphoreType.DMA(())],
        compiler_params=_params(("arbitrary", "arbitrary")),
        name="moe_combine",
    )(d0_tiles, d1_tiles, ys, meta, x, gate, ln_g, ln_b)


def _moe_ffn(x, scale, shift, gate, w_router, wg, wu, wd, ln_g, ln_b, *, alpha, tm, tm_e, tf):
    B, S, D = x.shape
    T = B * S
    n_exp = w_router.shape[1]
    wr = jnp.zeros((D, LANES), F32).at[:, :n_exp].set(w_router)
    u, meta, counts = _router(x, scale, shift, wr, n_exp=n_exp, tm=tm)

    meta2 = meta.reshape(T, LANES)
    e0 = meta2[:, META_E0].astype(I32)
    e1 = meta2[:, META_E1].astype(I32)
    r0 = meta2[:, META_R0].astype(I32)
    r1 = meta2[:, META_R1].astype(I32)
    cnt = counts[0, :n_exp].astype(I32)
    sizes = ((cnt + tm_e - 1) // tm_e) * tm_e
    ends = jnp.cumsum(sizes)
    starts = ends - sizes
    d0 = starts[e0] + r0
    d1 = starts[e1] + r1
    n_tiles = (TOP_K * T) // tm_e + n_exp
    tok = jnp.arange(T, dtype=I32)
    src = jnp.zeros((n_tiles * tm_e,), I32).at[d0].set(tok).at[d1].set(tok)
    n_active = (ends[-1] // tm_e).astype(I32)
    tile_id = jnp.minimum(jnp.arange(n_tiles, dtype=I32), n_active - 1)
    tile_expert = jnp.searchsorted(ends, tile_id * tm_e, side='right').astype(I32)

    xs = _gather_rows(u.reshape(T, D), src.reshape(n_tiles, 1, tm_e))
    ys = _expert_ffn(xs, tile_expert, n_active.reshape(1), wg, wu, wd, tm=tm_e, tf=tf)
    nt = T // tm
    return _combine(ys, d0.reshape(nt, 1, tm), d1.reshape(nt, 1, tm), meta, x, gate, ln_g, ln_b,
                    alpha=alpha, tm=tm)


def kernel(x, c, positions, w_in, w_out, attn_norm_gain, rec_norm_gain, rec_lb_logits, ada_w, ada_b,
           ln_gain, ln_bias, ffn_w_gate, ffn_w_up, ffn_w_down, moe_router, moe_w_gate, moe_w_up,
           moe_w_down):
    B, S, D = x.shape
    depth = w_in.shape[0]
    alpha = (2 * depth) ** 0.25

    p = jax.nn.softmax(rec_lb_logits.astype(F32), axis=0)
    cum = jnp.cumsum(p, axis=0)
    lb_all = cum - cum[0:1]
    half = ATTN_HEAD_DIM // 2
    inv_freq = ROPE_THETA ** (-jnp.arange(half, dtype=F32) / half)
    inv_freq_lanes = jnp.tile(inv_freq, LANES // half).reshape(1, LANES)
    pos_f = positions.astype(F32).reshape(B, S, 1)
    w_in_b = w_in.astype(BF16)
    w_out_b = w_out.astype(BF16)
    ff = ffn_w_gate.shape[2]
    ffp = -(-ff // (2 * LANES)) * (2 * LANES)
    pad_c = lambda w: jnp.pad(w.astype(BF16), ((0, 0), (0, 0), (0, ffp - ff)))
    ffn_g, ffn_u = pad_c(ffn_w_gate), pad_c(ffn_w_up)
    ffn_d = jnp.pad(ffn_w_down.astype(BF16), ((0, 0), (0, ffp - ff), (0, 0)))

    mods = _ada_modulation(c, ada_w, ada_b)

    def mod(layer, sub):
        m = mods[layer * 2 + sub].reshape(B, 1, 3 * D)
        return m[..., :D], m[..., D:2 * D], m[..., 2 * D:]

    vec = lambda a: a.reshape(1, -1)
    for layer in range(depth):
        shift, scale, gate = mod(layer, 0)
        proj = _in_projection(x, scale, shift, pos_f, inv_freq_lanes, w_in_b[layer], tm=min(1024, S))
        attn = _dilated_attention(proj)
        rec = _hgrn2(proj, lb_all[layer], vec(rec_norm_gain[layer]))
        x = _out_projection(attn, rec, x, gate, vec(attn_norm_gain[layer]), w_out_b[layer],
                            vec(ln_gain[layer, 0]), vec(ln_bias[layer, 0]), alpha=alpha, tm=min(512, S))
        shift, scale, gate = mod(layer, 1)
        j = layer // 2
        if layer % 2 == 0:
            x = _dense_ffn(x, scale, shift, gate, ffn_g[j], ffn_u[j], ffn_d[j],
                           vec(ln_gain[layer, 1]), vec(ln_bias[layer, 1]),
                           alpha=alpha, tm=min(512, S), tc=2 * LANES)
        else:
            x = _moe_ffn(x, scale, shift, gate, moe_router[j], moe_w_gate[j], moe_w_up[j],
                         moe_w_down[j], vec(ln_gain[layer, 1]), vec(ln_bias[layer, 1]),
                         alpha=alpha, tm=min(512, S), tm_e=512, tf=512)
    return x
```

```python
import functools

import jax
import jax.numpy as jnp
from jax import lax
from jax.experimental import pallas as pl
from jax.experimental.pallas import tpu as pltpu

F32 = jnp.float32
BF16 = jnp.bfloat16
I32 = jnp.int32

LANES = 128
V7X_VMEM_LIMIT_BYTES = 56 * 1024 * 1024

ATTN_WIDTH = 512
ATTN_HEAD_DIM = 64
DILATIONS = (1, 4, 16)
ATTN_RADIUS = 64
ROPE_THETA = 10000.0
REC_WIDTH = 512
REC_DIM = 128
REC_CHUNK = 64
TOP_K = 2
LN_EPS = 1e-5
RMS_EPS = 1e-6
MASK_VALUE = -1e30
DECAY_EXP_CLAMP = 80.0


def _params(semantics):
    return pltpu.CompilerParams(dimension_semantics=semantics,
                                vmem_limit_bytes=V7X_VMEM_LIMIT_BYTES)


def _sigmoid(x):
    return 1.0 / (1.0 + jnp.exp(-x))


def _layer_norm(z, gain, bias):
    mu = jnp.mean(z, axis=-1, keepdims=True)
    zc = z - mu
    var = jnp.mean(zc * zc, axis=-1, keepdims=True)
    return zc * lax.rsqrt(var + LN_EPS) * gain + bias


def _ada_kernel(c_ref, w_ref, b_ref, o_ref):
    c = c_ref[...]
    o_ref[...] = jnp.dot(c * _sigmoid(c), w_ref[...], precision=lax.Precision.HIGHEST,
                         preferred_element_type=F32) + b_ref[...]


def _ada_modulation(c, ada_w, ada_b):
    B, D = c.shape
    n = ada_w.shape[0] * ada_w.shape[1]
    n3 = ada_w.shape[-1]
    tn = 1024
    return pl.pallas_call(
        _ada_kernel,
        grid=(n, n3 // tn),
        in_specs=[pl.BlockSpec((B, D), lambda i, j: (0, 0)),
                  pl.BlockSpec((None, D, tn), lambda i, j: (i, 0, j)),
                  pl.BlockSpec((None, 1, tn), lambda i, j: (i, 0, j))],
        out_specs=pl.BlockSpec((None, B, tn), lambda i, j: (i, 0, j)),
        out_shape=jax.ShapeDtypeStruct((n, B, n3), F32),
        compiler_params=_params(("arbitrary", "arbitrary")),
        name="ada_modulation",
    )(c, ada_w.reshape(n, D, n3), ada_b.reshape(n, 1, n3))


def _inproj_kernel(x_ref, sc_ref, sh_ref, pos_ref, invf_ref, w_ref, o_ref, u_ref, *, tn):
    j = pl.program_id(2)

    @pl.when(j == 0)
    def _():
        u_ref[...] = (x_ref[...] * (1.0 + sc_ref[...]) + sh_ref[...]).astype(BF16)

    acc = jnp.dot(u_ref[...], w_ref[...], preferred_element_type=F32)

    @pl.when(j > 0)
    def _():
        o_ref[...] = acc

    @pl.when(j == 0)
    def _():
        ang = pos_ref[...] * invf_ref[...]
        cos = jnp.cos(ang)
        sin = jnp.sin(ang)
        lane = lax.broadcasted_iota(I32, (1, LANES), 1)
        first_half = (lane % ATTN_HEAD_DIM) < (ATTN_HEAD_DIM // 2)
        sin_signed = jnp.where(first_half, -sin, sin)
        q_scale = ATTN_HEAD_DIM ** -0.5
        for kk in range(tn // LANES):
            c = acc[:, kk * LANES:(kk + 1) * LANES]
            rot = jnp.where(first_half, pltpu.roll(c, LANES - 32, 1), pltpu.roll(c, 32, 1))
            r = c * cos + rot * sin_signed
            if kk * LANES < ATTN_WIDTH:
                r = r * q_scale
            o_ref[:, kk * LANES:(kk + 1) * LANES] = r


def _in_projection(x, scale, shift, pos_f, inv_freq_lanes, w_in_bf16, *, tm):
    B, S, D = x.shape
    n_cols = w_in_bf16.shape[1]
    tn = 2 * ATTN_WIDTH
    assert S % tm == 0 and n_cols % tn == 0
    return pl.pallas_call(
        functools.partial(_inproj_kernel, tn=tn),
        grid=(B, S // tm, n_cols // tn),
        in_specs=[pl.BlockSpec((None, tm, D), lambda b, i, j: (b, i, 0)),
                  pl.BlockSpec((None, 1, D), lambda b, i, j: (b, 0, 0)),
                  pl.BlockSpec((None, 1, D), lambda b, i, j: (b, 0, 0)),
                  pl.BlockSpec((None, tm, 1), lambda b, i, j: (b, i, 0)),
                  pl.BlockSpec((1, LANES), lambda b, i, j: (0, 0)),
                  pl.BlockSpec((D, tn), lambda b, i, j: (0, j))],
        out_specs=pl.BlockSpec((None, tm, tn), lambda b, i, j: (b, i, j)),
        out_shape=jax.ShapeDtypeStruct((B, S, n_cols), F32),
        scratch_shapes=[pltpu.VMEM((tm, D), BF16)],
        compiler_params=_params(("arbitrary", "arbitrary", "arbitrary")),
        name="in_projection",
    )(x, scale, shift, pos_f, inv_freq_lanes, w_in_bf16)


ATTN_BLOCK_GROUP = 4


def _attn_kernel(q_ref, k_ref, v_ref, o_ref, qs, ks, vs, qf, kf, vf, pm, plr, pa, nm, nl, na, *, S):
    lane = lax.broadcasted_iota(I32, (1, LANES), 1)
    head0 = lane < ATTN_HEAD_DIM

    for p, d in enumerate(DILATIONS):
        L = S // d
        tq = min(128, L)
        W = min(2 * tq, L)
        nb = L // tq

        d_prev = DILATIONS[p - 1] if p else 1
        ratio, l_prev = d // d_prev, S // d_prev
        keep_f32 = 0 < p < len(DILATIONS) - 1
        for src, stage, dst in ((q_ref, qf, qs), (k_ref, kf, ks), (v_ref, vf, vs)):
            src = src if p <= 1 else stage
            for r in range(d):
                a, r_prev = divmod(r, d_prev)
                rows = pl.ds(r_prev * l_prev + a, L, stride=ratio) if ratio > 1 else pl.ds(0, L)
                val = src[rows, :]
                if keep_f32:
                    stage[r * L:(r + 1) * L, :] = val
                dst[r * L:(r + 1) * L, :] = val.astype(BF16)

        n_blocks = d * nb
        group = min(ATTN_BLOCK_GROUP, n_blocks)
        assert n_blocks % group == 0

        def block_group(gi, carry, L=L, tq=tq, W=W, nb=nb, group=group):
            q0s, k0s, scores, probs, outs, sums = [], [], [], [], [], []
            for u in range(group):
                g = gi * group + u
                r = g // nb
                n = g - r * nb
                base = r * L
                q0 = pl.multiple_of(base + n * tq, tq)
                ws = jnp.clip(n * tq - ATTN_RADIUS, 0, L - W)
                k0 = pl.multiple_of(base + ws, 16)
                qb = qs[pl.ds(q0, tq), :]
                zero = jnp.zeros_like(qb)
                q2 = jnp.concatenate([jnp.where(head0, qb, zero), jnp.where(head0, zero, qb)], axis=0)
                s = lax.dot_general(q2, ks[pl.ds(k0, W), :], (((1,), (1,)), ((), ())),
                                    preferred_element_type=F32)
                qi = n * tq + lax.broadcasted_iota(I32, (tq, W), 0)
                kj = ws + lax.broadcasted_iota(I32, (tq, W), 1)
                valid = jnp.abs(qi - kj) <= ATTN_RADIUS
                scores.append(jnp.where(jnp.concatenate([valid, valid], axis=0), s, MASK_VALUE))
                q0s.append(q0)
                k0s.append(k0)
            for u in range(group):
                s = scores[u]
                m = jnp.max(s, axis=1, keepdims=True)
                e = jnp.exp(s - m)
                probs.append((m, e.astype(BF16)))
                sums.append(jnp.sum(e, axis=1, keepdims=True))
            for u in range(group):
                outs.append(jnp.dot(probs[u][1], vs[pl.ds(k0s[u], W), :], preferred_element_type=F32))
            for u in range(group):
                m, l = probs[u][0], sums[u]
                rows = pl.ds(q0s[u], tq)
                pm[rows, :] = jnp.where(head0, m[:tq], m[tq:])
                plr[rows, :] = jnp.where(head0, l[:tq], l[tq:])
                pa[rows, :] = jnp.where(head0, outs[u][:tq], outs[u][tq:])
            return carry

        lax.fori_loop(0, n_blocks // group, block_group, 0)

        for r in range(d):
            rows = pl.ds(r, L, stride=d) if d > 1 else pl.ds(0, L)
            nm[p, rows, :] = pm[r * L:(r + 1) * L, :]
            nl[p, rows, :] = plr[r * L:(r + 1) * L, :]
            na[p, rows, :] = pa[r * L:(r + 1) * L, :]

    m_all = jnp.maximum(jnp.maximum(nm[0], nm[1]), nm[2])
    num = jnp.zeros((S, LANES), F32)
    den = jnp.zeros((S, LANES), F32)
    for p in range(len(DILATIONS)):
        w = jnp.exp(nm[p] - m_all)
        num = num + w * na[p]
        den = den + w * nl[p]
    o_ref[...] = num / den


def _dilated_attention(proj):
    B, S, _ = proj.shape
    n_pairs = ATTN_WIDTH // LANES
    assert S % (16 * DILATIONS[-1]) == 0
    blk = lambda off: pl.BlockSpec((None, S, LANES), lambda b, h: (b, 0, off + h))
    return pl.pallas_call(
        functools.partial(_attn_kernel, S=S),
        grid=(B, n_pairs),
        in_specs=[blk(0), blk(n_pairs), blk(2 * n_pairs)],
        out_specs=pl.BlockSpec((None, S, LANES), lambda b, h: (b, 0, h)),
        out_shape=jax.ShapeDtypeStruct((B, S, ATTN_WIDTH), F32),
        scratch_shapes=[pltpu.VMEM((S, LANES), BF16)] * 3
                       + [pltpu.VMEM((S, LANES), F32)] * 6
                       + [pltpu.VMEM((len(DILATIONS), S, LANES), F32)] * 3,
        compiler_params=_params(("arbitrary", "arbitrary")),
        name="dilated_attention",
    )(proj, proj, proj)


REC_GROUP_CHUNKS = 4
REC_GROUP_ROWS = REC_GROUP_CHUNKS * REC_CHUNK


def _hgrn_kernel(rq_ref, zf_ref, zb_ref, ri_ref, rg_ref, lb_ref, gain_ref, o_ref,
                 q_s, vt_s, oi_s, qp_s, ut_s, dd_s, st_s, os_s, *, S):
    C, G, GR = REC_CHUNK, REC_GROUP_CHUNKS, REC_GROUP_ROWS
    nc, ng = S // C, S // GR

    row = lax.broadcasted_iota(I32, (GR, GR), 0)
    col = lax.broadcasted_iota(I32, (GR, GR), 1)
    same_chunk = (row // C) == (col // C)
    t_i = lax.broadcasted_iota(I32, (C, C), 0)
    s_i = lax.broadcasted_iota(I32, (C, C), 1)

    def prep(i, carry):
        r0 = pl.multiple_of(i * GR, GR)
        rq = rq_ref[pl.ds(r0, GR), :]
        q_s[pl.ds(r0, GR), :] = rq * _sigmoid(rq)
        v = ri_ref[pl.ds(r0, GR), :]
        for c in range(G):
            vt_s[i * G + c] = v[c * C:(c + 1) * C, :].T.astype(BF16)
        return carry

    lax.fori_loop(0, ng, prep, 0)

    for direction in range(2):
        fwd = direction == 0
        z_ref = zf_ref if fwd else zb_ref
        cum = jnp.where(same_chunk & ((col <= row) if fwd else (col >= row)), 1.0, 0.0).astype(BF16)
        keep = (s_i <= t_i) if fwd else (s_i >= t_i)
        mid_row = C // 2 - 1 if fwd else C // 2
        last_row = C - 1 if fwd else 0

        def phase_a(i, carry, fwd=fwd, z_ref=z_ref, cum=cum, keep=keep,
                    mid_row=mid_row, last_row=last_row, direction=direction):
            r0 = pl.multiple_of(i * GR, GR)
            z = z_ref[pl.ds(r0, GR), :]
            lb = lb_ref[direction:direction + 1, :]
            e = jnp.exp(-jnp.abs(z))
            log_sig = jnp.minimum(z, 0.0) - jnp.log(1.0 + e)
            t = jnp.log1p(-lb) + log_sig
            a = jnp.log(lb)
            g = jnp.maximum(a, t) + jnp.log(1.0 + jnp.exp(-jnp.abs(a - t)))
            kk = (1.0 - lb) * (jnp.where(z > 0, e, 1.0) / (1.0 + e))
            g1 = g.astype(BF16)
            r1 = g - g1.astype(F32)
            g2 = r1.astype(BF16)
            g3 = (r1 - g2.astype(F32)).astype(BF16)
            bc = (jnp.dot(cum, g1, preferred_element_type=F32)
                  + jnp.dot(cum, g2, preferred_element_type=F32)
                  + jnp.dot(cum, g3, preferred_element_type=F32))
            b3 = bc.reshape(G, C, REC_DIM)
            mid = b3[:, mid_row:mid_row + 1, :]
            last = b3[:, last_row:last_row + 1, :]
            e3 = b3 - mid
            q3 = q_s[pl.ds(r0, GR), :].reshape(G, C, REC_DIM)
            k3 = kk.reshape(G, C, REC_DIM)
            v3 = ri_ref[pl.ds(r0, GR), :].reshape(G, C, REC_DIM).astype(BF16)
            qt = (q3 * jnp.exp(jnp.minimum(e3, DECAY_EXP_CLAMP))).astype(BF16)
            kt = (k3 * jnp.exp(jnp.minimum(-e3, DECAY_EXP_CLAMP))).astype(BF16)
            att = jnp.einsum('gtc,gsc->gts', qt, kt, preferred_element_type=F32)
            att = jnp.where(keep[None], att, 0.0).astype(BF16)
            oi = jnp.einsum('gts,gsv->gtv', att, v3, preferred_element_type=F32)
            oi_s[pl.ds(r0, GR), :] = oi.reshape(GR, REC_DIM)
            qp_s[pl.ds(r0, GR), :] = (q3 * jnp.exp(b3)).astype(BF16).reshape(GR, REC_DIM)
            kp = (k3 * jnp.exp(last - b3)).astype(BF16)
            c0 = pl.multiple_of(i * G, G)
            ut_s[pl.ds(c0, G)] = jnp.einsum('gvs,gsc->gvc', vt_s[pl.ds(c0, G)], kp,
                                            preferred_element_type=F32)
            dd_s[pl.ds(c0, G)] = jnp.exp(last)
            return carry

        lax.fori_loop(0, ng, phase_a, 0, unroll=2)

        def scan(c, st, fwd=fwd):
            idx = c if fwd else nc - 1 - c
            st_s[idx] = st.astype(BF16)
            return dd_s[idx] * st + ut_s[idx]

        lax.fori_loop(0, nc, scan, jnp.zeros((REC_DIM, REC_DIM), F32))

        def phase_c(i, carry, fwd=fwd):
            r0 = pl.multiple_of(i * GR, GR)
            c0 = pl.multiple_of(i * G, G)
            qp3 = qp_s[pl.ds(r0, GR), :].reshape(G, C, REC_DIM)
            oo = jnp.einsum('gtc,gvc->gtv', qp3, st_s[pl.ds(c0, G)], preferred_element_type=F32)
            tot = oi_s[pl.ds(r0, GR), :] + oo.reshape(GR, REC_DIM)
            if fwd:
                os_s[pl.ds(r0, GR), :] = tot
            else:
                os_s[pl.ds(r0, GR), :] = os_s[pl.ds(r0, GR), :] + tot
            return carry

        lax.fori_loop(0, ng, phase_c, 0, unroll=2)

    o = os_s[...]
    o = o * lax.rsqrt(jnp.mean(o * o, axis=-1, keepdims=True) + RMS_EPS) * gain_ref[...]
    o_ref[...] = o * _sigmoid(rg_ref[...])


def _hgrn2(proj, lb, rec_gain):
    B, S, _ = proj.shape
    nh = REC_WIDTH // REC_DIM
    assert S % REC_GROUP_ROWS == 0
    col0 = 3 * ATTN_WIDTH // LANES
    blk = lambda k: pl.BlockSpec((None, S, REC_DIM), lambda b, h: (b, 0, col0 + k * nh + h))
    nc = S // REC_CHUNK
    return pl.pallas_call(
        functools.partial(_hgrn_kernel, S=S),
        grid=(B, nh),
        in_specs=[blk(0), blk(1), blk(2), blk(3), blk(4),
                  pl.BlockSpec((2, REC_DIM), lambda b, h: (0, h)),
                  pl.BlockSpec((1, REC_DIM), lambda b, h: (0, h))],
        out_specs=pl.BlockSpec((None, S, REC_DIM), lambda b, h: (b, 0, h)),
        out_shape=jax.ShapeDtypeStruct((B, S, REC_WIDTH), F32),
        scratch_shapes=[pltpu.VMEM((S, REC_DIM), F32),
                        pltpu.VMEM((nc, REC_DIM, REC_CHUNK), BF16),
                        pltpu.VMEM((S, REC_DIM), F32),
                        pltpu.VMEM((S, REC_DIM), BF16),
                        pltpu.VMEM((nc, REC_DIM, REC_DIM), F32),
                        pltpu.VMEM((nc, 1, REC_DIM), F32),
                        pltpu.VMEM((nc, REC_DIM, REC_DIM), BF16),
                        pltpu.VMEM((S, REC_DIM), F32)],
        compiler_params=_params(("arbitrary", "arbitrary")),
        name="hgrn2",
    )(proj, proj, proj, proj, proj, lb, rec_gain)


def _outproj_kernel(a_ref, r_ref, x_ref, gate_ref, ag_ref, w_ref, lg_ref, lb_ref, o_ref, *, alpha):
    a = a_ref[...]
    an = a * lax.rsqrt(jnp.mean(a * a, axis=-1, keepdims=True) + RMS_EPS) * ag_ref[...]
    y = (jnp.dot(an.astype(BF16), w_ref[:ATTN_WIDTH, :], preferred_element_type=F32)
         + jnp.dot(r_ref[...].astype(BF16), w_ref[ATTN_WIDTH:, :], preferred_element_type=F32))
    z = alpha * x_ref[...] + (1.0 + gate_ref[...]) * y
    o_ref[...] = _layer_norm(z, lg_ref[...], lb_ref[...])


def _out_projection(attn, rec, x, gate, attn_gain, w_out_bf16, ln_g, ln_b, *, alpha, tm):
    B, S, D = x.shape
    row = lambda w: pl.BlockSpec((None, tm, w), lambda b, i: (b, i, 0))
    vec = lambda w: pl.BlockSpec((1, w), lambda b, i: (0, 0))
    return pl.pallas_call(
        functools.partial(_outproj_kernel, alpha=alpha),
        grid=(B, S // tm),
        in_specs=[row(ATTN_WIDTH), row(REC_WIDTH), row(D),
                  pl.BlockSpec((None, 1, D), lambda b, i: (b, 0, 0)),
                  vec(ATTN_WIDTH),
                  pl.BlockSpec((ATTN_WIDTH + REC_WIDTH, D), lambda b, i: (0, 0)),
                  vec(D), vec(D)],
        out_specs=row(D),
        out_shape=jax.ShapeDtypeStruct((B, S, D), F32),
        compiler_params=_params(("arbitrary", "arbitrary")),
        name="out_projection",
    )(attn, rec, x, gate, attn_gain, w_out_bf16, ln_g, ln_b)


def _ffn_kernel(x_ref, sc_ref, sh_ref, gate_ref, wg_ref, wu_ref, wd_ref, lg_ref, lb_ref, o_ref,
                *, alpha, tc):
    x = x_ref[...]
    u = (x * (1.0 + sc_ref[...]) + sh_ref[...]).astype(BF16)
    acc = jnp.zeros(x.shape, F32)
    for c in range(wg_ref.shape[1] // tc):
        cols = slice(c * tc, (c + 1) * tc)
        g = jnp.dot(u, wg_ref[:, cols], preferred_element_type=F32)
        up = jnp.dot(u, wu_ref[:, cols], preferred_element_type=F32)
        h = (g * _sigmoid(g) * up).astype(BF16)
        acc = acc + jnp.dot(h, wd_ref[cols, :], preferred_element_type=F32)
    z = alpha * x + (1.0 + gate_ref[...]) * acc
    o_ref[...] = _layer_norm(z, lg_ref[...], lb_ref[...])


def _dense_ffn(x, scale, shift, gate, wg, wu, wd, ln_g, ln_b, *, alpha, tm, tc):
    B, S, D = x.shape
    ffp = wg.shape[1]
    assert ffp % tc == 0
    row = pl.BlockSpec((None, tm, D), lambda b, i: (b, i, 0))
    mod = pl.BlockSpec((None, 1, D), lambda b, i: (b, 0, 0))
    vec = pl.BlockSpec((1, D), lambda b, i: (0, 0))
    return pl.pallas_call(
        functools.partial(_ffn_kernel, alpha=alpha, tc=tc),
        grid=(B, S // tm),
        in_specs=[row, mod, mod, mod,
                  pl.BlockSpec((D, ffp), lambda b, i: (0, 0)),
                  pl.BlockSpec((D, ffp), lambda b, i: (0, 0)),
                  pl.BlockSpec((ffp, D), lambda b, i: (0, 0)),
                  vec, vec],
        out_specs=row,
        out_shape=jax.ShapeDtypeStruct((B, S, D), F32),
        compiler_params=_params(("arbitrary", "arbitrary")),
        name="dense_ffn",
    )(x, scale, shift, gate, wg, wu, wd, ln_g, ln_b)


META_E0, META_E1, META_W0, META_W1, META_R0, META_R1 = range(6)
SUBLANES = 8
DMA_LOOP_UNROLL = 8


def _to_token_tiles(ref, val):
    n = val.shape[0]
    for k in range(SUBLANES):
        ref[pl.ds(k, n, stride=SUBLANES), :] = val[:, k * LANES:(k + 1) * LANES]


def _from_token_tiles(ref, n):
    return [ref[pl.ds(k, n, stride=SUBLANES), :] for k in range(SUBLANES)]


def _token_copy(src_hbm, token, dst, slot, sem):
    return pltpu.make_async_copy(
        src_hbm.at[pl.ds(pl.multiple_of(token * SUBLANES, SUBLANES), SUBLANES)],
        dst.at[pl.ds(pl.multiple_of(slot * SUBLANES, SUBLANES), SUBLANES)], sem)


def _router_kernel(x_ref, sc_ref, sh_ref, wr_ref, u_ref, meta_ref, cnt_ref, carry_ref, *, n_exp):
    tm = x_ref.shape[0]

    @pl.when((pl.program_id(0) == 0) & (pl.program_id(1) == 0))
    def _():
        carry_ref[...] = jnp.zeros_like(carry_ref)

    u = x_ref[...] * (1.0 + sc_ref[...]) + sh_ref[...]
    _to_token_tiles(u_ref, u)
    logits = jnp.dot(u, wr_ref[...], precision=lax.Precision.HIGHEST, preferred_element_type=F32)
    lane = lax.broadcasted_iota(I32, (tm, LANES), 1)
    neg = -jnp.inf
    l1 = jnp.where(lane < n_exp, logits, neg)
    m1 = jnp.max(l1, axis=1, keepdims=True)
    i1 = jnp.min(jnp.where(l1 == m1, lane, LANES), axis=1, keepdims=True)
    l2 = jnp.where(lane == i1, neg, l1)
    m2 = jnp.max(l2, axis=1, keepdims=True)
    i2 = jnp.min(jnp.where(l2 == m2, lane, LANES), axis=1, keepdims=True)
    e = jnp.exp(m2 - m1)
    w1 = 1.0 / (1.0 + e)
    w2 = e * w1
    sel = jnp.where((lane == i1) | (lane == i2), 1.0, 0.0)
    before = (lax.broadcasted_iota(I32, (tm, tm), 1) < lax.broadcasted_iota(I32, (tm, tm), 0))
    ranks = jnp.dot(jnp.where(before, 1.0, 0.0).astype(BF16), sel.astype(BF16),
                    preferred_element_type=F32) + carry_ref[...]
    r1 = jnp.sum(jnp.where(lane == i1, ranks, 0.0), axis=1, keepdims=True)
    r2 = jnp.sum(jnp.where(lane == i2, ranks, 0.0), axis=1, keepdims=True)
    carry_ref[...] = carry_ref[...] + jnp.sum(sel, axis=0, keepdims=True)
    cnt_ref[...] = carry_ref[...]
    meta = jnp.zeros((tm, LANES), F32)
    for k, val in ((META_E0, i1.astype(F32)), (META_E1, i2.astype(F32)), (META_W0, w1),
                   (META_W1, w2), (META_R0, r1), (META_R1, r2)):
        meta = jnp.where(lane == k, val, meta)
    meta_ref[...] = meta


def _router(x, scale, shift, w_router_lanes, *, n_exp, tm):
    B, S, D = x.shape
    assert D == SUBLANES * LANES
    nt = S // tm
    row = pl.BlockSpec((None, tm, D), lambda b, i: (b, i, 0))
    mod = pl.BlockSpec((None, 1, D), lambda b, i: (b, 0, 0))
    return pl.pallas_call(
        functools.partial(_router_kernel, n_exp=n_exp),
        grid=(B, nt),
        in_specs=[row, mod, mod, pl.BlockSpec((D, LANES), lambda b, i: (0, 0))],
        out_specs=[pl.BlockSpec((tm * SUBLANES, LANES), lambda b, i: (b * nt + i, 0)),
                   pl.BlockSpec((None, tm, LANES), lambda b, i: (b, i, 0)),
                   pl.BlockSpec((1, LANES), lambda b, i: (0, 0))],
        out_shape=[jax.ShapeDtypeStruct((B * S * SUBLANES, LANES), F32),
                   jax.ShapeDtypeStruct((B, S, LANES), F32),
                   jax.ShapeDtypeStruct((1, LANES), F32)],
        scratch_shapes=[pltpu.VMEM((1, LANES), F32)],
        compiler_params=_params(("arbitrary", "arbitrary")),
        name="moe_router",
    )(x, scale, shift, w_router_lanes)


def _gather_kernel(src_ref, u_hbm, o_ref, sem):
    n = o_ref.shape[0] // SUBLANES

    def issue(r, carry):
        _token_copy(u_hbm, src_ref[0, r], o_ref, r, sem).start()
        return carry

    def drain(r, carry):
        _token_copy(u_hbm, 0, o_ref, r, sem).wait()
        return carry

    lax.fori_loop(0, n, issue, 0, unroll=DMA_LOOP_UNROLL)
    lax.fori_loop(0, n, drain, 0, unroll=DMA_LOOP_UNROLL)


def _gather_rows(u_tiles, src_tiles):
    n_tiles, _, rows = src_tiles.shape
    return pl.pallas_call(
        _gather_kernel,
        grid=(n_tiles,),
        in_specs=[pl.BlockSpec((None, 1, rows), lambda i: (i, 0, 0), memory_space=pltpu.SMEM),
                  pl.BlockSpec(memory_space=pl.ANY)],
        out_specs=pl.BlockSpec((rows * SUBLANES, LANES), lambda i: (i, 0)),
        out_shape=jax.ShapeDtypeStruct((n_tiles * rows * SUBLANES, LANES), F32),
        scratch_shapes=[pltpu.SemaphoreType.DMA(())],
        compiler_params=_params(("arbitrary",)),
        name="moe_gather",
    )(src_tiles, u_tiles)


def _expert_kernel(te_ref, na_ref, xs_ref, wg_ref, wu_ref, wd_ref, o_ref, xb_ref, acc_ref):
    i = pl.program_id(0)
    j = pl.program_id(1)
    last = pl.num_programs(1) - 1
    active = i < na_ref[0]
    tm = xb_ref.shape[0]

    @pl.when(active & (j == 0))
    def _():
        for k, chunk in enumerate(_from_token_tiles(xs_ref, tm)):
            xb_ref[:, k * LANES:(k + 1) * LANES] = chunk.astype(BF16)
        acc_ref[...] = jnp.zeros_like(acc_ref)

    @pl.when(active)
    def _():
        xb = xb_ref[...]
        g = jnp.dot(xb, wg_ref[...].astype(BF16), preferred_element_type=F32)
        up = jnp.dot(xb, wu_ref[...].astype(BF16), preferred_element_type=F32)
        h = (g * _sigmoid(g) * up).astype(BF16)
        acc_ref[...] += jnp.dot(h, wd_ref[...].astype(BF16), preferred_element_type=F32)

    @pl.when(active & (j == last))
    def _():
        _to_token_tiles(o_ref, acc_ref[...])

    @pl.when(jnp.logical_not(active) & (j == last))
    def _():
        o_ref[...] = jnp.zeros_like(o_ref)


def _expert_ffn(xs, tile_expert, n_active, wg, wu, wd, *, tm, tf):
    D = wg.shape[1]
    n_rows = xs.shape[0] // SUBLANES
    ff = wg.shape[2]
    assert n_rows % tm == 0 and ff % tf == 0
    nj = ff // tf

    def jj(i, j, na):
        return jnp.where(i < na[0], j, nj - 1)

    tiles = pl.BlockSpec((tm * SUBLANES, LANES), lambda i, j, te, na: (i, 0))
    grid_spec = pltpu.PrefetchScalarGridSpec(
        num_scalar_prefetch=2,
        grid=(n_rows // tm, nj),
        in_specs=[tiles,
                  pl.BlockSpec((None, D, tf), lambda i, j, te, na: (te[i], 0, jj(i, j, na))),
                  pl.BlockSpec((None, D, tf), lambda i, j, te, na: (te[i], 0, jj(i, j, na))),
                  pl.BlockSpec((None, tf, D), lambda i, j, te, na: (te[i], jj(i, j, na), 0))],
        out_specs=tiles,
        scratch_shapes=[pltpu.VMEM((tm, D), BF16), pltpu.VMEM((tm, D), F32)],
    )
    return pl.pallas_call(
        _expert_kernel,
        grid_spec=grid_spec,
        out_shape=jax.ShapeDtypeStruct(xs.shape, F32),
        compiler_params=_params(("arbitrary", "arbitrary")),
        name="moe_experts",
    )(tile_expert, n_active, xs, wg, wu, wd)


def _combine_kernel(d0_ref, d1_ref, ys_hbm, meta_ref, x_ref, gate_ref, lg_ref, lb_ref, o_ref,
                    buf, sem, *, alpha):
    tm = x_ref.shape[0]

    def issue(r, carry):
        _token_copy(ys_hbm, d0_ref[0, r], buf.at[0], r, sem).start()
        _token_copy(ys_hbm, d1_ref[0, r], buf.at[1], r, sem).start()
        return carry

    def drain(r, carry):
        _token_copy(ys_hbm, 0, buf.at[0], r, sem).wait()
        _token_copy(ys_hbm, 0, buf.at[1], r, sem).wait()
        return carry

    lax.fori_loop(0, tm, issue, 0, unroll=DMA_LOOP_UNROLL)
    lax.fori_loop(0, tm, drain, 0, unroll=DMA_LOOP_UNROLL)
    meta = meta_ref[...]
    w0 = meta[:, META_W0:META_W0 + 1]
    w1 = meta[:, META_W1:META_W1 + 1]
    y = jnp.concatenate([w0 * a + w1 * b for a, b in zip(_from_token_tiles(buf.at[0], tm),
                                                         _from_token_tiles(buf.at[1], tm))], axis=1)
    z = alpha * x_ref[...] + (1.0 + gate_ref[...]) * y
    o_ref[...] = _layer_norm(z, lg_ref[...], lb_ref[...])


def _combine(ys, d0_tiles, d1_tiles, meta, x, gate, ln_g, ln_b, *, alpha, tm):
    B, S, D = x.shape
    nt = S // tm
    ids = pl.BlockSpec((None, 1, tm), lambda b, i: (b * nt + i, 0, 0), memory_space=pltpu.SMEM)
    row = pl.BlockSpec((None, tm, D), lambda b, i: (b, i, 0))
    vec = pl.BlockSpec((1, D), lambda b, i: (0, 0))
    return pl.pallas_call(
        functools.partial(_combine_kernel, alpha=alpha),
        grid=(B, nt),
        in_specs=[ids, ids, pl.BlockSpec(memory_space=pl.ANY),
                  pl.BlockSpec((None, tm, LANES), lambda b, i: (b, i, 0)),
                  row, pl.BlockSpec((None, 1, D), lambda b, i: (b, 0, 0)), vec, vec],
        out_specs=row,
        out_shape=jax.ShapeDtypeStruct((B, S, D), F32),
        scratch_shapes=[pltpu.VMEM((2, tm * SUBLANES, LANES), F32), pltpu.SemaphoreType.DMA(())],
        compiler_params=_params(("arbitrary", "arbitrary")),
        name="moe_combine",
    )(d0_tiles, d1_tiles, ys, meta, x, gate, ln_g, ln_b)


def _moe_ffn(x, scale, shift, gate, w_router, wg, wu, wd, ln_g, ln_b, *, alpha, tm, tm_e, tf):
    B, S, D = x.shape
    T = B * S
    n_exp = w_router.shape[1]
    wr = jnp.zeros((D, LANES), F32).at[:, :n_exp].set(w_router)
    u_tiles, meta, counts = _router(x, scale, shift, wr, n_exp=n_exp, tm=tm)

    meta2 = meta.reshape(T, LANES)
    e0 = meta2[:, META_E0].astype(I32)
    e1 = meta2[:, META_E1].astype(I32)
    r0 = meta2[:, META_R0].astype(I32)
    r1 = meta2[:, META_R1].astype(I32)
    cnt = counts[0, :n_exp].astype(I32)
    sizes = ((cnt + tm_e - 1) // tm_e) * tm_e
    ends = jnp.cumsum(sizes)
    starts = ends - sizes
    d0 = starts[e0] + r0
    d1 = starts[e1] + r1
    n_tiles = (TOP_K * T) // tm_e + n_exp
    tok = jnp.arange(T, dtype=I32)
    src = jnp.zeros((n_tiles * tm_e,), I32).at[d0].set(tok).at[d1].set(tok)
    n_active = (ends[-1] // tm_e).astype(I32)
    tile_id = jnp.minimum(jnp.arange(n_tiles, dtype=I32), n_active - 1)
    tile_expert = jnp.sum((ends[None, :] <= (tile_id * tm_e)[:, None]).astype(I32), axis=1)

    xs = _gather_rows(u_tiles, src.reshape(n_tiles, 1, tm_e))
    ys = _expert_ffn(xs, tile_expert, n_active.reshape(1), wg, wu, wd, tm=tm_e, tf=tf)
    nt = T // tm
    return _combine(ys, d0.reshape(nt, 1, tm), d1.reshape(nt, 1, tm), meta, x, gate, ln_g, ln_b,
                    alpha=alpha, tm=tm)


def kernel(x, c, positions, w_in, w_out, attn_norm_gain, rec_norm_gain, rec_lb_logits, ada_w, ada_b,
           ln_gain, ln_bias, ffn_w_gate, ffn_w_up, ffn_w_down, moe_router, moe_w_gate, moe_w_up,
           moe_w_down):
    B, S, D = x.shape
    depth = w_in.shape[0]
    alpha = (2 * depth) ** 0.25

    p = jax.nn.softmax(rec_lb_logits.astype(F32), axis=0)
    cum = jnp.cumsum(p, axis=0)
    lb_all = cum - cum[0:1]
    half = ATTN_HEAD_DIM // 2
    inv_freq = ROPE_THETA ** (-jnp.arange(half, dtype=F32) / half)
    inv_freq_lanes = jnp.tile(inv_freq, LANES // half).reshape(1, LANES)
    pos_f = positions.astype(F32).reshape(B, S, 1)
    w_in_b = w_in.astype(BF16)
    w_out_b = w_out.astype(BF16)
    ff = ffn_w_gate.shape[2]
    ffp = -(-ff // (2 * LANES)) * (2 * LANES)
    pad_c = lambda w: jnp.pad(w.astype(BF16), ((0, 0), (0, 0), (0, ffp - ff)))
    ffn_g, ffn_u = pad_c(ffn_w_gate), pad_c(ffn_w_up)
    ffn_d = jnp.pad(ffn_w_down.astype(BF16), ((0, 0), (0, ffp - ff), (0, 0)))

    mods = _ada_modulation(c, ada_w, ada_b)

    def mod(layer, sub):
        m = mods[layer * 2 + sub].reshape(B, 1, 3 * D)
        return m[..., :D], m[..., D:2 * D], m[..., 2 * D:]

    vec = lambda a: a.reshape(1, -1)
    for layer in range(depth):
        shift, scale, gate = mod(layer, 0)
        proj = _in_projection(x, scale, shift, pos_f, inv_freq_lanes, w_in_b[layer], tm=min(1024, S))
        attn = _dilated_attention(proj)
        rec = _hgrn2(proj, lb_all[layer], vec(rec_norm_gain[layer]))
        x = _out_projection(attn, rec, x, gate, vec(attn_norm_gain[layer]), w_out_b[layer],
                            vec(ln_gain[layer, 0]), vec(ln_bias[layer, 0]), alpha=alpha, tm=min(512, S))
        shift, scale, gate = mod(layer, 1)
        j = layer // 2
        if layer % 2 == 0:
            x = _dense_ffn(x, scale, shift, gate, ffn_g[j], ffn_u[j], ffn_d[j],
                           vec(ln_gain[layer, 1]), vec(ln_bias[layer, 1]),
                           alpha=alpha, tm=min(512, S), tc=2 * LANES)
        else:
            x = _moe_ffn(x, scale, shift, gate, moe_router[j], moe_w_gate[j], moe_w_up[j],
                         moe_w_down[j], vec(ln_gain[layer, 1]), vec(ln_bias[layer, 1]),
                         alpha=alpha, tm=min(512, S), tm_e=512, tf=512)
    return x
```

```python
import functools

import jax
import jax.numpy as jnp
from jax import lax
from jax.experimental import pallas as pl
from jax.experimental.pallas import tpu as pltpu

F32 = jnp.float32
BF16 = jnp.bfloat16
I32 = jnp.int32

LANES = 128
V7X_VMEM_LIMIT_BYTES = 56 * 1024 * 1024

ATTN_WIDTH = 512
ATTN_HEAD_DIM = 64
DILATIONS = (1, 4, 16)
ATTN_RADIUS = 64
ROPE_THETA = 10000.0
REC_WIDTH = 512
REC_DIM = 128
REC_CHUNK = 64
TOP_K = 2
LN_EPS = 1e-5
RMS_EPS = 1e-6
MASK_VALUE = -1e30
DECAY_EXP_CLAMP = 80.0


def _params(semantics):
    return pltpu.CompilerParams(dimension_semantics=semantics,
                                vmem_limit_bytes=V7X_VMEM_LIMIT_BYTES)


def _sigmoid(x):
    return 1.0 / (1.0 + jnp.exp(-x))


def _layer_norm(z, gain, bias):
    mu = jnp.mean(z, axis=-1, keepdims=True)
    zc = z - mu
    var = jnp.mean(zc * zc, axis=-1, keepdims=True)
    return zc * lax.rsqrt(var + LN_EPS) * gain + bias


def _ada_kernel(c_ref, w_ref, b_ref, o_ref):
    c = c_ref[...]
    o_ref[...] = jnp.dot(c * _sigmoid(c), w_ref[...], precision=lax.Precision.HIGHEST,
                         preferred_element_type=F32) + b_ref[...]


def _ada_modulation(c, ada_w, ada_b):
    B, D = c.shape
    n = ada_w.shape[0] * ada_w.shape[1]
    n3 = ada_w.shape[-1]
    tn = 1024
    return pl.pallas_call(
        _ada_kernel,
        grid=(n, n3 // tn),
        in_specs=[pl.BlockSpec((B, D), lambda i, j: (0, 0)),
                  pl.BlockSpec((None, D, tn), lambda i, j: (i, 0, j)),
                  pl.BlockSpec((None, 1, tn), lambda i, j: (i, 0, j))],
        out_specs=pl.BlockSpec((None, B, tn), lambda i, j: (i, 0, j)),
        out_shape=jax.ShapeDtypeStruct((n, B, n3), F32),
        compiler_params=_params(("arbitrary", "arbitrary")),
        name="ada_modulation",
    )(c, ada_w.reshape(n, D, n3), ada_b.reshape(n, 1, n3))


def _rope_kernel(pos_ref, invf_ref, cos_ref, sin_ref):
    ang = pos_ref[...] * invf_ref[...]
    lane = lax.broadcasted_iota(I32, (1, LANES), 1)
    first_half = (lane % ATTN_HEAD_DIM) < (ATTN_HEAD_DIM // 2)
    sin = jnp.sin(ang)
    cos_ref[...] = jnp.cos(ang)
    sin_ref[...] = jnp.where(first_half, -sin, sin)


def _rope_tables(pos_f, inv_freq_lanes, *, tm):
    B, S, _ = pos_f.shape
    tab = pl.BlockSpec((None, tm, LANES), lambda b, i: (b, i, 0))
    return pl.pallas_call(
        _rope_kernel,
        grid=(B, S // tm),
        in_specs=[pl.BlockSpec((None, tm, 1), lambda b, i: (b, i, 0)),
                  pl.BlockSpec((1, LANES), lambda b, i: (0, 0))],
        out_specs=[tab, tab],
        out_shape=[jax.ShapeDtypeStruct((B, S, LANES), F32)] * 2,
        compiler_params=_params(("arbitrary", "arbitrary")),
        name="rope_tables",
    )(pos_f, inv_freq_lanes)


def _inproj_kernel(x_ref, sc_ref, sh_ref, cos_ref, sin_ref, w_ref, o_ref, u_ref, *, tn):
    j = pl.program_id(2)

    @pl.when(j == 0)
    def _():
        u_ref[...] = (x_ref[...] * (1.0 + sc_ref[...]) + sh_ref[...]).astype(BF16)

    acc = jnp.dot(u_ref[...], w_ref[...], preferred_element_type=F32)

    @pl.when(j > 0)
    def _():
        o_ref[...] = acc

    @pl.when(j == 0)
    def _():
        cos = cos_ref[...]
        sin_signed = sin_ref[...]
        lane = lax.broadcasted_iota(I32, (1, LANES), 1)
        first_half = (lane % ATTN_HEAD_DIM) < (ATTN_HEAD_DIM // 2)
        q_scale = ATTN_HEAD_DIM ** -0.5
        for kk in range(tn // LANES):
            c = acc[:, kk * LANES:(kk + 1) * LANES]
            rot = jnp.where(first_half, pltpu.roll(c, LANES - 32, 1), pltpu.roll(c, 32, 1))
            r = c * cos + rot * sin_signed
            if kk * LANES < ATTN_WIDTH:
                r = r * q_scale
            o_ref[:, kk * LANES:(kk + 1) * LANES] = r


def _in_projection(x, scale, shift, rope_cos, rope_sin, w_in_bf16, *, tm):
    B, S, D = x.shape
    n_cols = w_in_bf16.shape[1]
    tn = 2 * ATTN_WIDTH
    assert S % tm == 0 and n_cols % tn == 0
    return pl.pallas_call(
        functools.partial(_inproj_kernel, tn=tn),
        grid=(B, S // tm, n_cols // tn),
        in_specs=[pl.BlockSpec((None, tm, D), lambda b, i, j: (b, i, 0)),
                  pl.BlockSpec((None, 1, D), lambda b, i, j: (b, 0, 0)),
                  pl.BlockSpec((None, 1, D), lambda b, i, j: (b, 0, 0)),
                  pl.BlockSpec((None, tm, LANES), lambda b, i, j: (b, i, 0)),
                  pl.BlockSpec((None, tm, LANES), lambda b, i, j: (b, i, 0)),
                  pl.BlockSpec((D, tn), lambda b, i, j: (0, j))],
        out_specs=pl.BlockSpec((None, tm, tn), lambda b, i, j: (b, i, j)),
        out_shape=jax.ShapeDtypeStruct((B, S, n_cols), F32),
        scratch_shapes=[pltpu.VMEM((tm, D), BF16)],
        compiler_params=_params(("arbitrary", "arbitrary", "arbitrary")),
        name="in_projection",
    )(x, scale, shift, rope_cos, rope_sin, w_in_bf16)


ATTN_BLOCK_GROUP = 4


def _attn_kernel(q_ref, k_ref, v_ref, o_ref, qs, ks, vs, qf, kf, vf, pm, plr, pa, nm, nl, na, *, S):
    lane = lax.broadcasted_iota(I32, (1, LANES), 1)
    head0 = lane < ATTN_HEAD_DIM

    for p, d in enumerate(DILATIONS):
        L = S // d
        tq = min(128, L)
        W = min(2 * tq, L)
        nb = L // tq

        d_prev = DILATIONS[p - 1] if p else 1
        ratio, l_prev = d // d_prev, S // d_prev
        keep_f32 = 0 < p < len(DILATIONS) - 1
        for src, stage, dst in ((q_ref, qf, qs), (k_ref, kf, ks), (v_ref, vf, vs)):
            src = src if p <= 1 else stage
            for r in range(d):
                a, r_prev = divmod(r, d_prev)
                rows = pl.ds(r_prev * l_prev + a, L, stride=ratio) if ratio > 1 else pl.ds(0, L)
                val = src[rows, :]
                if keep_f32:
                    stage[r * L:(r + 1) * L, :] = val
                dst[r * L:(r + 1) * L, :] = val.astype(BF16)

        n_blocks = d * nb
        group = min(ATTN_BLOCK_GROUP, n_blocks)
        assert n_blocks % group == 0

        def block_group(gi, carry, L=L, tq=tq, W=W, nb=nb, group=group):
            q0s, k0s, scores, probs, outs, sums = [], [], [], [], [], []
            for u in range(group):
                g = gi * group + u
                r = g // nb
                n = g - r * nb
                base = r * L
                q0 = pl.multiple_of(base + n * tq, tq)
                ws = jnp.clip(n * tq - ATTN_RADIUS, 0, L - W)
                k0 = pl.multiple_of(base + ws, 16)
                qb = qs[pl.ds(q0, tq), :]
                zero = jnp.zeros_like(qb)
                q2 = jnp.concatenate([jnp.where(head0, qb, zero), jnp.where(head0, zero, qb)], axis=0)
                s = lax.dot_general(q2, ks[pl.ds(k0, W), :], (((1,), (1,)), ((), ())),
                                    preferred_element_type=F32)
                qi = n * tq + lax.broadcasted_iota(I32, (tq, W), 0)
                kj = ws + lax.broadcasted_iota(I32, (tq, W), 1)
                valid = jnp.abs(qi - kj) <= ATTN_RADIUS
                scores.append(jnp.where(jnp.concatenate([valid, valid], axis=0), s, MASK_VALUE))
                q0s.append(q0)
                k0s.append(k0)
            for u in range(group):
                s = scores[u]
                m = jnp.max(s, axis=1, keepdims=True)
                e = jnp.exp(s - m)
                probs.append((m, e.astype(BF16)))
                sums.append(jnp.sum(e, axis=1, keepdims=True))
            for u in range(group):
                outs.append(jnp.dot(probs[u][1], vs[pl.ds(k0s[u], W), :], preferred_element_type=F32))
            for u in range(group):
                m, l = probs[u][0], sums[u]
                rows = pl.ds(q0s[u], tq)
                pm[rows, :] = jnp.where(head0, m[:tq], m[tq:])
                plr[rows, :] = jnp.where(head0, l[:tq], l[tq:])
                pa[rows, :] = jnp.where(head0, outs[u][:tq], outs[u][tq:])
            return carry

        lax.fori_loop(0, n_blocks // group, block_group, 0)

        for r in range(d):
            rows = pl.ds(r, L, stride=d) if d > 1 else pl.ds(0, L)
            nm[p, rows, :] = pm[r * L:(r + 1) * L, :]
            nl[p, rows, :] = plr[r * L:(r + 1) * L, :]
            na[p, rows, :] = pa[r * L:(r + 1) * L, :]

    m_all = jnp.maximum(jnp.maximum(nm[0], nm[1]), nm[2])
    num = jnp.zeros((S, LANES), F32)
    den = jnp.zeros((S, LANES), F32)
    for p in range(len(DILATIONS)):
        w = jnp.exp(nm[p] - m_all)
        num = num + w * na[p]
        den = den + w * nl[p]
    o_ref[...] = num / den


def _dilated_attention(proj):
    B, S, _ = proj.shape
    n_pairs = ATTN_WIDTH // LANES
    assert S % (16 * DILATIONS[-1]) == 0
    blk = lambda off: pl.BlockSpec((None, S, LANES), lambda b, h: (b, 0, off + h))
    return pl.pallas_call(
        functools.partial(_attn_kernel, S=S),
        grid=(B, n_pairs),
        in_specs=[blk(0), blk(n_pairs), blk(2 * n_pairs)],
        out_specs=pl.BlockSpec((None, S, LANES), lambda b, h: (b, 0, h)),
        out_shape=jax.ShapeDtypeStruct((B, S, ATTN_WIDTH), F32),
        scratch_shapes=[pltpu.VMEM((S, LANES), BF16)] * 3
                       + [pltpu.VMEM((S, LANES), F32)] * 6
                       + [pltpu.VMEM((len(DILATIONS), S, LANES), F32)] * 3,
        compiler_params=_params(("arbitrary", "arbitrary")),
        name="dilated_attention",
    )(proj, proj, proj)


REC_GROUP_CHUNKS = 4
REC_GROUP_ROWS = REC_GROUP_CHUNKS * REC_CHUNK
REC_STAGE_GROUPS = 2


def _hgrn_kernel(rq_ref, zf_ref, zb_ref, ri_ref, rg_ref, lb_ref, gain_ref, o_ref,
                 q_s, vt_s, oi_s, qp_s, ut_s, dd_s, st_s, os_s, *, S):
    C, G, GR = REC_CHUNK, REC_GROUP_CHUNKS, REC_GROUP_ROWS
    nc, ng = S // C, S // GR
    sg = min(REC_STAGE_GROUPS, ng)
    assert ng % sg == 0

    row = lax.broadcasted_iota(I32, (GR, GR), 0)
    col = lax.broadcasted_iota(I32, (GR, GR), 1)
    same_chunk = (row // C) == (col // C)
    t_i = lax.broadcasted_iota(I32, (C, C), 0)
    s_i = lax.broadcasted_iota(I32, (C, C), 1)

    def prep(i, carry):
        r0 = pl.multiple_of(i * GR, GR)
        rq = rq_ref[pl.ds(r0, GR), :]
        q_s[pl.ds(r0, GR), :] = rq * _sigmoid(rq)
        v = ri_ref[pl.ds(r0, GR), :]
        for c in range(G):
            vt_s[i * G + c] = v[c * C:(c + 1) * C, :].T.astype(BF16)
        return carry

    lax.fori_loop(0, ng, prep, 0)

    for direction in range(2):
        fwd = direction == 0
        z_ref = zf_ref if fwd else zb_ref
        cum = jnp.where(same_chunk & ((col <= row) if fwd else (col >= row)), 1.0, 0.0).astype(BF16)
        keep = (s_i <= t_i) if fwd else (s_i >= t_i)
        mid_row = C // 2 - 1 if fwd else C // 2
        last_row = C - 1 if fwd else 0

        def phase_a(it, carry, fwd=fwd, z_ref=z_ref, cum=cum, keep=keep,
                    mid_row=mid_row, last_row=last_row, direction=direction):
            lb = lb_ref[direction:direction + 1, :]
            log_lb = jnp.log(lb)
            log_1m_lb = jnp.log1p(-lb)
            r0s, c0s, k3s, bcs, atts, stage3 = [], [], [], [], [], []
            for u in range(sg):
                i = it * sg + u
                r0 = pl.multiple_of(i * GR, GR)
                z = z_ref[pl.ds(r0, GR), :]
                e = jnp.exp(-jnp.abs(z))
                t = log_1m_lb + (jnp.minimum(z, 0.0) - jnp.log(1.0 + e))
                g = jnp.maximum(log_lb, t) + jnp.log(1.0 + jnp.exp(-jnp.abs(log_lb - t)))
                kk = (1.0 - lb) * (jnp.where(z > 0, e, 1.0) / (1.0 + e))
                g1 = g.astype(BF16)
                r1 = g - g1.astype(F32)
                g2 = r1.astype(BF16)
                g3 = (r1 - g2.astype(F32)).astype(BF16)
                bcs.append(jnp.dot(cum, g1, preferred_element_type=F32)
                           + jnp.dot(cum, g2, preferred_element_type=F32)
                           + jnp.dot(cum, g3, preferred_element_type=F32))
                k3s.append(kk.reshape(G, C, REC_DIM))
                r0s.append(r0)
                c0s.append(pl.multiple_of(i * G, G))
            for u in range(sg):
                b3 = bcs[u].reshape(G, C, REC_DIM)
                mid = b3[:, mid_row:mid_row + 1, :]
                last = b3[:, last_row:last_row + 1, :]
                e3 = b3 - mid
                q3 = q_s[pl.ds(r0s[u], GR), :].reshape(G, C, REC_DIM)
                qt = (q3 * jnp.exp(jnp.minimum(e3, DECAY_EXP_CLAMP))).astype(BF16)
                kt = (k3s[u] * jnp.exp(jnp.minimum(-e3, DECAY_EXP_CLAMP))).astype(BF16)
                atts.append(jnp.einsum('gtc,gsc->gts', qt, kt, preferred_element_type=F32))
                qp_s[pl.ds(r0s[u], GR), :] = (q3 * jnp.exp(b3)).astype(BF16).reshape(GR, REC_DIM)
                stage3.append(((k3s[u] * jnp.exp(last - b3)).astype(BF16), jnp.exp(last)))
            for u in range(sg):
                att = jnp.where(keep[None], atts[u], 0.0).astype(BF16)
                v3 = ri_ref[pl.ds(r0s[u], GR), :].reshape(G, C, REC_DIM).astype(BF16)
                oi = jnp.einsum('gts,gsv->gtv', att, v3, preferred_element_type=F32)
                oi_s[pl.ds(r0s[u], GR), :] = oi.reshape(GR, REC_DIM)
                kp, decay = stage3[u]
                ut_s[pl.ds(c0s[u], G)] = jnp.einsum('gvs,gsc->gvc', vt_s[pl.ds(c0s[u], G)], kp,
                                                    preferred_element_type=F32)
                dd_s[pl.ds(c0s[u], G)] = decay
            return carry

        lax.fori_loop(0, ng // sg, phase_a, 0)

        def scan(c, st, fwd=fwd):
            idx = c if fwd else nc - 1 - c
            st_s[idx] = st.astype(BF16)
            return dd_s[idx] * st + ut_s[idx]

        lax.fori_loop(0, nc, scan, jnp.zeros((REC_DIM, REC_DIM), F32))

        def phase_c(i, carry, fwd=fwd):
            r0 = pl.multiple_of(i * GR, GR)
            c0 = pl.multiple_of(i * G, G)
            qp3 = qp_s[pl.ds(r0, GR), :].reshape(G, C, REC_DIM)
            oo = jnp.einsum('gtc,gvc->gtv', qp3, st_s[pl.ds(c0, G)], preferred_element_type=F32)
            tot = oi_s[pl.ds(r0, GR), :] + oo.reshape(GR, REC_DIM)
            if fwd:
                os_s[pl.ds(r0, GR), :] = tot
            else:
                os_s[pl.ds(r0, GR), :] = os_s[pl.ds(r0, GR), :] + tot
            return carry

        lax.fori_loop(0, ng, phase_c, 0, unroll=2)

    o = os_s[...]
    o = o * lax.rsqrt(jnp.mean(o * o, axis=-1, keepdims=True) + RMS_EPS) * gain_ref[...]
    o_ref[...] = o * _sigmoid(rg_ref[...])


def _hgrn2(proj, lb, rec_gain):
    B, S, _ = proj.shape
    nh = REC_WIDTH // REC_DIM
    assert S % REC_GROUP_ROWS == 0
    col0 = 3 * ATTN_WIDTH // LANES
    blk = lambda k: pl.BlockSpec((None, S, REC_DIM), lambda b, h: (b, 0, col0 + k * nh + h))
    nc = S // REC_CHUNK
    return pl.pallas_call(
        functools.partial(_hgrn_kernel, S=S),
        grid=(B, nh),
        in_specs=[blk(0), blk(1), blk(2), blk(3), blk(4),
                  pl.BlockSpec((2, REC_DIM), lambda b, h: (0, h)),
                  pl.BlockSpec((1, REC_DIM), lambda b, h: (0, h))],
        out_specs=pl.BlockSpec((None, S, REC_DIM), lambda b, h: (b, 0, h)),
        out_shape=jax.ShapeDtypeStruct((B, S, REC_WIDTH), F32),
        scratch_shapes=[pltpu.VMEM((S, REC_DIM), F32),
                        pltpu.VMEM((nc, REC_DIM, REC_CHUNK), BF16),
                        pltpu.VMEM((S, REC_DIM), F32),
                        pltpu.VMEM((S, REC_DIM), BF16),
                        pltpu.VMEM((nc, REC_DIM, REC_DIM), F32),
                        pltpu.VMEM((nc, 1, REC_DIM), F32),
                        pltpu.VMEM((nc, REC_DIM, REC_DIM), BF16),
                        pltpu.VMEM((S, REC_DIM), F32)],
        compiler_params=_params(("arbitrary", "arbitrary")),
        name="hgrn2",
    )(proj, proj, proj, proj, proj, lb, rec_gain)


def _outproj_kernel(a_ref, r_ref, x_ref, gate_ref, ag_ref, w_ref, lg_ref, lb_ref, o_ref, *, alpha):
    a = a_ref[...]
    an = a * lax.rsqrt(jnp.mean(a * a, axis=-1, keepdims=True) + RMS_EPS) * ag_ref[...]
    y = (jnp.dot(an.astype(BF16), w_ref[:ATTN_WIDTH, :], preferred_element_type=F32)
         + jnp.dot(r_ref[...].astype(BF16), w_ref[ATTN_WIDTH:, :], preferred_element_type=F32))
    z = alpha * x_ref[...] + (1.0 + gate_ref[...]) * y
    o_ref[...] = _layer_norm(z, lg_ref[...], lb_ref[...])


def _out_projection(attn, rec, x, gate, attn_gain, w_out_bf16, ln_g, ln_b, *, alpha, tm):
    B, S, D = x.shape
    row = lambda w: pl.BlockSpec((None, tm, w), lambda b, i: (b, i, 0))
    vec = lambda w: pl.BlockSpec((1, w), lambda b, i: (0, 0))
    return pl.pallas_call(
        functools.partial(_outproj_kernel, alpha=alpha),
        grid=(B, S // tm),
        in_specs=[row(ATTN_WIDTH), row(REC_WIDTH), row(D),
                  pl.BlockSpec((None, 1, D), lambda b, i: (b, 0, 0)),
                  vec(ATTN_WIDTH),
                  pl.BlockSpec((ATTN_WIDTH + REC_WIDTH, D), lambda b, i: (0, 0)),
                  vec(D), vec(D)],
        out_specs=row(D),
        out_shape=jax.ShapeDtypeStruct((B, S, D), F32),
        compiler_params=_params(("arbitrary", "arbitrary")),
        name="out_projection",
    )(attn, rec, x, gate, attn_gain, w_out_bf16, ln_g, ln_b)


def _ffn_kernel(x_ref, sc_ref, sh_ref, gate_ref, wg_ref, wu_ref, wd_ref, lg_ref, lb_ref, o_ref,
                *, alpha, tc):
    x = x_ref[...]
    u = (x * (1.0 + sc_ref[...]) + sh_ref[...]).astype(BF16)
    acc = jnp.zeros(x.shape, F32)
    for c in range(wg_ref.shape[1] // tc):
        cols = slice(c * tc, (c + 1) * tc)
        g = jnp.dot(u, wg_ref[:, cols], preferred_element_type=F32)
        up = jnp.dot(u, wu_ref[:, cols], preferred_element_type=F32)
        h = (g * _sigmoid(g) * up).astype(BF16)
        acc = acc + jnp.dot(h, wd_ref[cols, :], preferred_element_type=F32)
    z = alpha * x + (1.0 + gate_ref[...]) * acc
    o_ref[...] = _layer_norm(z, lg_ref[...], lb_ref[...])


def _dense_ffn(x, scale, shift, gate, wg, wu, wd, ln_g, ln_b, *, alpha, tm, tc):
    B, S, D = x.shape
    ffp = wg.shape[1]
    assert ffp % tc == 0
    row = pl.BlockSpec((None, tm, D), lambda b, i: (b, i, 0))
    mod = pl.BlockSpec((None, 1, D), lambda b, i: (b, 0, 0))
    vec = pl.BlockSpec((1, D), lambda b, i: (0, 0))
    return pl.pallas_call(
        functools.partial(_ffn_kernel, alpha=alpha, tc=tc),
        grid=(B, S // tm),
        in_specs=[row, mod, mod, mod,
                  pl.BlockSpec((D, ffp), lambda b, i: (0, 0)),
                  pl.BlockSpec((D, ffp), lambda b, i: (0, 0)),
                  pl.BlockSpec((ffp, D), lambda b, i: (0, 0)),
                  vec, vec],
        out_specs=row,
        out_shape=jax.ShapeDtypeStruct((B, S, D), F32),
        compiler_params=_params(("arbitrary", "arbitrary")),
        name="dense_ffn",
    )(x, scale, shift, gate, wg, wu, wd, ln_g, ln_b)


META_E0, META_E1, META_W0, META_W1, META_R0, META_R1 = range(6)
SUBLANES = 8
DMA_LOOP_UNROLL = 8


def _to_token_tiles(ref, val):
    n = val.shape[0]
    for k in range(SUBLANES):
        ref[pl.ds(k, n, stride=SUBLANES), :] = val[:, k * LANES:(k + 1) * LANES]


def _from_token_tiles(ref, n):
    return [ref[pl.ds(k, n, stride=SUBLANES), :] for k in range(SUBLANES)]


def _token_copy(src_hbm, token, dst, slot, sem):
    return pltpu.make_async_copy(
        src_hbm.at[pl.ds(pl.multiple_of(token * SUBLANES, SUBLANES), SUBLANES)],
        dst.at[pl.ds(pl.multiple_of(slot * SUBLANES, SUBLANES), SUBLANES)], sem)


def _router_kernel(x_ref, sc_ref, sh_ref, wr_ref, u_ref, meta_ref, cnt_ref, carry_ref, *, n_exp):
    tm = x_ref.shape[0]

    @pl.when((pl.program_id(0) == 0) & (pl.program_id(1) == 0))
    def _():
        carry_ref[...] = jnp.zeros_like(carry_ref)

    u = x_ref[...] * (1.0 + sc_ref[...]) + sh_ref[...]
    _to_token_tiles(u_ref, u)
    logits = jnp.dot(u, wr_ref[...], precision=lax.Precision.HIGHEST, preferred_element_type=F32)
    lane = lax.broadcasted_iota(I32, (tm, LANES), 1)
    neg = -jnp.inf
    l1 = jnp.where(lane < n_exp, logits, neg)
    m1 = jnp.max(l1, axis=1, keepdims=True)
    i1 = jnp.min(jnp.where(l1 == m1, lane, LANES), axis=1, keepdims=True)
    l2 = jnp.where(lane == i1, neg, l1)
    m2 = jnp.max(l2, axis=1, keepdims=True)
    i2 = jnp.min(jnp.where(l2 == m2, lane, LANES), axis=1, keepdims=True)
    e = jnp.exp(m2 - m1)
    w1 = 1.0 / (1.0 + e)
    w2 = e * w1
    sel = jnp.where((lane == i1) | (lane == i2), 1.0, 0.0)
    before = (lax.broadcasted_iota(I32, (tm, tm), 1) < lax.broadcasted_iota(I32, (tm, tm), 0))
    ranks = jnp.dot(jnp.where(before, 1.0, 0.0).astype(BF16), sel.astype(BF16),
                    preferred_element_type=F32) + carry_ref[...]
    r1 = jnp.sum(jnp.where(lane == i1, ranks, 0.0), axis=1, keepdims=True)
    r2 = jnp.sum(jnp.where(lane == i2, ranks, 0.0), axis=1, keepdims=True)
    carry_ref[...] = carry_ref[...] + jnp.sum(sel, axis=0, keepdims=True)
    cnt_ref[...] = carry_ref[...]
    meta = jnp.zeros((tm, LANES), F32)
    for k, val in ((META_E0, i1.astype(F32)), (META_E1, i2.astype(F32)), (META_W0, w1),
                   (META_W1, w2), (META_R0, r1), (META_R1, r2)):
        meta = jnp.where(lane == k, val, meta)
    meta_ref[...] = meta


def _router(x, scale, shift, w_router_lanes, *, n_exp, tm):
    B, S, D = x.shape
    assert D == SUBLANES * LANES
    nt = S // tm
    row = pl.BlockSpec((None, tm, D), lambda b, i: (b, i, 0))
    mod = pl.BlockSpec((None, 1, D), lambda b, i: (b, 0, 0))
    return pl.pallas_call(
        functools.partial(_router_kernel, n_exp=n_exp),
        grid=(B, nt),
        in_specs=[row, mod, mod, pl.BlockSpec((D, LANES), lambda b, i: (0, 0))],
        out_specs=[pl.BlockSpec((tm * SUBLANES, LANES), lambda b, i: (b * nt + i, 0)),
                   pl.BlockSpec((None, tm, LANES), lambda b, i: (b, i, 0)),
                   pl.BlockSpec((1, LANES), lambda b, i: (0, 0))],
        out_shape=[jax.ShapeDtypeStruct((B * S * SUBLANES, LANES), F32),
                   jax.ShapeDtypeStruct((B, S, LANES), F32),
                   jax.ShapeDtypeStruct((1, LANES), F32)],
        scratch_shapes=[pltpu.VMEM((1, LANES), F32)],
        compiler_params=_params(("arbitrary", "arbitrary")),
        name="moe_router",
    )(x, scale, shift, w_router_lanes)


def _gather_kernel(src_ref, u_hbm, o_ref, sem):
    n = o_ref.shape[0] // SUBLANES

    def issue(g, carry):
        for u in range(DMA_LOOP_UNROLL):
            r = g * DMA_LOOP_UNROLL + u
            _token_copy(u_hbm, src_ref[0, r], o_ref, r, sem).start(priority=u % 2)
        return carry

    def drain(r, carry):
        _token_copy(u_hbm, 0, o_ref, r, sem).wait()
        return carry

    lax.fori_loop(0, n // DMA_LOOP_UNROLL, issue, 0)
    lax.fori_loop(0, n, drain, 0, unroll=DMA_LOOP_UNROLL)


def _gather_rows(u_tiles, src_tiles):
    n_tiles, _, rows = src_tiles.shape
    return pl.pallas_call(
        _gather_kernel,
        grid=(n_tiles,),
        in_specs=[pl.BlockSpec((None, 1, rows), lambda i: (i, 0, 0), memory_space=pltpu.SMEM),
                  pl.BlockSpec(memory_space=pl.ANY)],
        out_specs=pl.BlockSpec((rows * SUBLANES, LANES), lambda i: (i, 0)),
        out_shape=jax.ShapeDtypeStruct((n_tiles * rows * SUBLANES, LANES), F32),
        scratch_shapes=[pltpu.SemaphoreType.DMA(())],
        compiler_params=_params(("arbitrary",)),
        name="moe_gather",
    )(src_tiles, u_tiles)


def _expert_kernel(te_ref, na_ref, xs_ref, wg_ref, wu_ref, wd_ref, o_ref, xb_ref, acc_ref):
    i = pl.program_id(0)
    j = pl.program_id(1)
    last = pl.num_programs(1) - 1
    active = i < na_ref[0]
    tm = xb_ref.shape[0]

    @pl.when(active & (j == 0))
    def _():
        for k, chunk in enumerate(_from_token_tiles(xs_ref, tm)):
            xb_ref[:, k * LANES:(k + 1) * LANES] = chunk.astype(BF16)
        acc_ref[...] = jnp.zeros_like(acc_ref)

    @pl.when(active)
    def _():
        xb = xb_ref[...]
        g = jnp.dot(xb, wg_ref[...].astype(BF16), preferred_element_type=F32)
        up = jnp.dot(xb, wu_ref[...].astype(BF16), preferred_element_type=F32)
        h = (g * _sigmoid(g) * up).astype(BF16)
        acc_ref[...] += jnp.dot(h, wd_ref[...].astype(BF16), preferred_element_type=F32)

    @pl.when(active & (j == last))
    def _():
        _to_token_tiles(o_ref, acc_ref[...])

    @pl.when(jnp.logical_not(active) & (j == last))
    def _():
        o_ref[...] = jnp.zeros_like(o_ref)


def _expert_ffn(xs, tile_expert, n_active, wg, wu, wd, *, tm, tf):
    D = wg.shape[1]
    n_rows = xs.shape[0] // SUBLANES
    ff = wg.shape[2]
    assert n_rows % tm == 0 and ff % tf == 0
    nj = ff // tf

    def jj(i, j, na):
        return jnp.where(i < na[0], j, nj - 1)

    tiles = pl.BlockSpec((tm * SUBLANES, LANES), lambda i, j, te, na: (i, 0))
    grid_spec = pltpu.PrefetchScalarGridSpec(
        num_scalar_prefetch=2,
        grid=(n_rows // tm, nj),
        in_specs=[tiles,
                  pl.BlockSpec((None, D, tf), lambda i, j, te, na: (te[i], 0, jj(i, j, na))),
                  pl.BlockSpec((None, D, tf), lambda i, j, te, na: (te[i], 0, jj(i, j, na))),
                  pl.BlockSpec((None, tf, D), lambda i, j, te, na: (te[i], jj(i, j, na), 0))],
        out_specs=tiles,
        scratch_shapes=[pltpu.VMEM((tm, D), BF16), pltpu.VMEM((tm, D), F32)],
    )
    return pl.pallas_call(
        _expert_kernel,
        grid_spec=grid_spec,
        out_shape=jax.ShapeDtypeStruct(xs.shape, F32),
        compiler_params=_params(("arbitrary", "arbitrary")),
        name="moe_experts",
    )(tile_expert, n_active, xs, wg, wu, wd)


def _combine_kernel(d0_ref, d1_ref, ys_hbm, meta_ref, x_ref, gate_ref, lg_ref, lb_ref, o_ref,
                    buf, sem, *, alpha):
    tm = x_ref.shape[0]

    def issue(g, carry):
        for u in range(DMA_LOOP_UNROLL):
            r = g * DMA_LOOP_UNROLL + u
            _token_copy(ys_hbm, d0_ref[0, r], buf.at[0], r, sem).start(priority=0)
            _token_copy(ys_hbm, d1_ref[0, r], buf.at[1], r, sem).start(priority=1)
        return carry

    def drain(r, carry):
        _token_copy(ys_hbm, 0, buf.at[0], r, sem).wait()
        _token_copy(ys_hbm, 0, buf.at[1], r, sem).wait()
        return carry

    lax.fori_loop(0, tm // DMA_LOOP_UNROLL, issue, 0)
    lax.fori_loop(0, tm, drain, 0, unroll=DMA_LOOP_UNROLL)
    meta = meta_ref[...]
    w0 = meta[:, META_W0:META_W0 + 1]
    w1 = meta[:, META_W1:META_W1 + 1]
    y = jnp.concatenate([w0 * a + w1 * b for a, b in zip(_from_token_tiles(buf.at[0], tm),
                                                         _from_token_tiles(buf.at[1], tm))], axis=1)
    z = alpha * x_ref[...] + (1.0 + gate_ref[...]) * y
    o_ref[...] = _layer_norm(z, lg_ref[...], lb_ref[...])


def _combine(ys, d0_tiles, d1_tiles, meta, x, gate, ln_g, ln_b, *, alpha, tm):
    B, S, D = x.shape
    nt = S // tm
    ids = pl.BlockSpec((None, 1, tm), lambda b, i: (b * nt + i, 0, 0), memory_space=pltpu.SMEM)
    row = pl.BlockSpec((None, tm, D), lambda b, i: (b, i, 0))
    vec = pl.BlockSpec((1, D), lambda b, i: (0, 0))
    return pl.pallas_call(
        functools.partial(_combine_kernel, alpha=alpha),
        grid=(B, nt),
        in_specs=[ids, ids, pl.BlockSpec(memory_space=pl.ANY),
                  pl.BlockSpec((None, tm, LANES), lambda b, i: (b, i, 0)),
                  row, pl.BlockSpec((None, 1, D), lambda b, i: (b, 0, 0)), vec, vec],
        out_specs=row,
        out_shape=jax.ShapeDtypeStruct((B, S, D), F32),
        scratch_shapes=[pltpu.VMEM((2, tm * SUBLANES, LANES), F32), pltpu.SemaphoreType.DMA(())],
        compiler_params=_params(("arbitrary", "arbitrary")),
        name="moe_combine",
    )(d0_tiles, d1_tiles, ys, meta, x, gate, ln_g, ln_b)


def _moe_ffn(x, scale, shift, gate, w_router, wg, wu, wd, ln_g, ln_b, *, alpha, tm, tm_e, tf):
    B, S, D = x.shape
    T = B * S
    n_exp = w_router.shape[1]
    wr = jnp.zeros((D, LANES), F32).at[:, :n_exp].set(w_router)
    u_tiles, meta, counts = _router(x, scale, shift, wr, n_exp=n_exp, tm=tm)

    meta2 = meta.reshape(T, LANES)
    e0 = meta2[:, META_E0].astype(I32)
    e1 = meta2[:, META_E1].astype(I32)
    r0 = meta2[:, META_R0].astype(I32)
    r1 = meta2[:, META_R1].astype(I32)
    cnt = counts[0, :n_exp].astype(I32)
    sizes = ((cnt + tm_e - 1) // tm_e) * tm_e
    ends = jnp.cumsum(sizes)
    starts = ends - sizes
    d0 = starts[e0] + r0
    d1 = starts[e1] + r1
    n_tiles = (TOP_K * T) // tm_e + n_exp
    tok = jnp.arange(T, dtype=I32)
    src = jnp.zeros((n_tiles * tm_e,), I32).at[d0].set(tok).at[d1].set(tok)
    n_active = (ends[-1] // tm_e).astype(I32)
    tile_id = jnp.minimum(jnp.arange(n_tiles, dtype=I32), n_active - 1)
    tile_expert = jnp.sum((ends[None, :] <= (tile_id * tm_e)[:, None]).astype(I32), axis=1)

    xs = _gather_rows(u_tiles, src.reshape(n_tiles, 1, tm_e))
    ys = _expert_ffn(xs, tile_expert, n_active.reshape(1), wg, wu, wd, tm=tm_e, tf=tf)
    nt = T // tm
    return _combine(ys, d0.reshape(nt, 1, tm), d1.reshape(nt, 1, tm), meta, x, gate, ln_g, ln_b,
                    alpha=alpha, tm=tm)


def kernel(x, c, positions, w_in, w_out, attn_norm_gain, rec_norm_gain, rec_lb_logits, ada_w, ada_b,
           ln_gain, ln_bias, ffn_w_gate, ffn_w_up, ffn_w_down, moe_router, moe_w_gate, moe_w_up,
           moe_w_down):
    B, S, D = x.shape
    depth = w_in.shape[0]
    alpha = (2 * depth) ** 0.25

    p = jax.nn.softmax(rec_lb_logits.astype(F32), axis=0)
    cum = jnp.cumsum(p, axis=0)
    lb_all = cum - cum[0:1]
    half = ATTN_HEAD_DIM // 2
    inv_freq = ROPE_THETA ** (-jnp.arange(half, dtype=F32) / half)
    inv_freq_lanes = jnp.tile(inv_freq, LANES // half).reshape(1, LANES)
    pos_f = positions.astype(F32).reshape(B, S, 1)
    w_in_b = w_in.astype(BF16)
    w_out_b = w_out.astype(BF16)
    ff = ffn_w_gate.shape[2]
    ffp = -(-ff // (2 * LANES)) * (2 * LANES)
    pad_c = lambda w: jnp.pad(w.astype(BF16), ((0, 0), (0, 0), (0, ffp - ff)))
    ffn_g, ffn_u = pad_c(ffn_w_gate), pad_c(ffn_w_up)
    ffn_d = jnp.pad(ffn_w_down.astype(BF16), ((0, 0), (0, ffp - ff), (0, 0)))

    mods = _ada_modulation(c, ada_w, ada_b)
    rope_cos, rope_sin = _rope_tables(pos_f, inv_freq_lanes, tm=min(1024, S))

    def mod(layer, sub):
        m = mods[layer * 2 + sub].reshape(B, 1, 3 * D)
        return m[..., :D], m[..., D:2 * D], m[..., 2 * D:]

    vec = lambda a: a.reshape(1, -1)
    for layer in range(depth):
        shift, scale, gate = mod(layer, 0)
        proj = _in_projection(x, scale, shift, rope_cos, rope_sin, w_in_b[layer], tm=min(1024, S))
        attn = _dilated_attention(proj)
        rec = _hgrn2(proj, lb_all[layer], vec(rec_norm_gain[layer]))
        x = _out_projection(attn, rec, x, gate, vec(attn_norm_gain[layer]), w_out_b[layer],
                            vec(ln_gain[layer, 0]), vec(ln_bias[layer, 0]), alpha=alpha, tm=min(512, S))
        shift, scale, gate = mod(layer, 1)
        j = layer // 2
        if layer % 2 == 0:
            x = _dense_ffn(x, scale, shift, gate, ffn_g[j], ffn_u[j], ffn_d[j],
                           vec(ln_gain[layer, 1]), vec(ln_bias[layer, 1]),
                           alpha=alpha, tm=min(512, S), tc=2 * LANES)
        else:
            x = _moe_ffn(x, scale, shift, gate, moe_router[j], moe_w_gate[j], moe_w_up[j],
                         moe_w_down[j], vec(ln_gain[layer, 1]), vec(ln_bias[layer, 1]),
                         alpha=alpha, tm=min(512, S), tm_e=min(1024, B * S), tf=512)
    return x
```

```python
import functools

import jax
import jax.numpy as jnp
from jax import lax
from jax.experimental import pallas as pl
from jax.experimental.pallas import tpu as pltpu

F32 = jnp.float32
BF16 = jnp.bfloat16
I32 = jnp.int32

LANES = 128
V7X_VMEM_LIMIT_BYTES = 56 * 1024 * 1024

ATTN_WIDTH = 512
ATTN_HEAD_DIM = 64
DILATIONS = (1, 4, 16)
ATTN_RADIUS = 64
ROPE_THETA = 10000.0
REC_WIDTH = 512
REC_DIM = 128
REC_CHUNK = 64
TOP_K = 2
LN_EPS = 1e-5
RMS_EPS = 1e-6
MASK_VALUE = -1e30
DECAY_EXP_CLAMP = 80.0


def _params(semantics):
    return pltpu.CompilerParams(dimension_semantics=semantics,
                                vmem_limit_bytes=V7X_VMEM_LIMIT_BYTES)


def _sigmoid(x):
    return 1.0 / (1.0 + jnp.exp(-x))


def _layer_norm(z, gain, bias):
    mu = jnp.mean(z, axis=-1, keepdims=True)
    zc = z - mu
    var = jnp.mean(zc * zc, axis=-1, keepdims=True)
    return zc * lax.rsqrt(var + LN_EPS) * gain + bias


def _ada_kernel(c_ref, w_ref, b_ref, o_ref):
    c = c_ref[...]
    o_ref[...] = jnp.dot(c * _sigmoid(c), w_ref[...], precision=lax.Precision.HIGHEST,
                         preferred_element_type=F32) + b_ref[...]


def _ada_modulation(c, ada_w, ada_b):
    B, D = c.shape
    n = ada_w.shape[0] * ada_w.shape[1]
    n3 = ada_w.shape[-1]
    tn = 1024
    return pl.pallas_call(
        _ada_kernel,
        grid=(n, n3 // tn),
        in_specs=[pl.BlockSpec((B, D), lambda i, j: (0, 0)),
                  pl.BlockSpec((None, D, tn), lambda i, j: (i, 0, j)),
                  pl.BlockSpec((None, 1, tn), lambda i, j: (i, 0, j))],
        out_specs=pl.BlockSpec((None, B, tn), lambda i, j: (i, 0, j)),
        out_shape=jax.ShapeDtypeStruct((n, B, n3), F32),
        compiler_params=_params(("arbitrary", "arbitrary")),
        name="ada_modulation",
    )(c, ada_w.reshape(n, D, n3), ada_b.reshape(n, 1, n3))


def _rope_kernel(pos_ref, invf_ref, cos_ref, sin_ref):
    ang = pos_ref[...] * invf_ref[...]
    lane = lax.broadcasted_iota(I32, (1, LANES), 1)
    first_half = (lane % ATTN_HEAD_DIM) < (ATTN_HEAD_DIM // 2)
    sin = jnp.sin(ang)
    cos_ref[...] = jnp.cos(ang)
    sin_ref[...] = jnp.where(first_half, -sin, sin)


def _rope_tables(pos_f, inv_freq_lanes, *, tm):
    B, S, _ = pos_f.shape
    tab = pl.BlockSpec((None, tm, LANES), lambda b, i: (b, i, 0))
    return pl.pallas_call(
        _rope_kernel,
        grid=(B, S // tm),
        in_specs=[pl.BlockSpec((None, tm, 1), lambda b, i: (b, i, 0)),
                  pl.BlockSpec((1, LANES), lambda b, i: (0, 0))],
        out_specs=[tab, tab],
        out_shape=[jax.ShapeDtypeStruct((B, S, LANES), F32)] * 2,
        compiler_params=_params(("arbitrary", "arbitrary")),
        name="rope_tables",
    )(pos_f, inv_freq_lanes)


def _inproj_kernel(x_ref, sc_ref, sh_ref, cos_ref, sin_ref, w_ref, o_ref, u_ref, *, tn):
    j = pl.program_id(2)

    @pl.when(j == 0)
    def _():
        u_ref[...] = (x_ref[...] * (1.0 + sc_ref[...]) + sh_ref[...]).astype(BF16)

    acc = jnp.dot(u_ref[...], w_ref[...], preferred_element_type=F32)

    @pl.when(j > 0)
    def _():
        o_ref[...] = acc

    @pl.when(j == 0)
    def _():
        cos = cos_ref[...]
        sin_signed = sin_ref[...]
        lane = lax.broadcasted_iota(I32, (1, LANES), 1)
        first_half = (lane % ATTN_HEAD_DIM) < (ATTN_HEAD_DIM // 2)
        q_scale = ATTN_HEAD_DIM ** -0.5
        for kk in range(tn // LANES):
            c = acc[:, kk * LANES:(kk + 1) * LANES]
            rot = jnp.where(first_half, pltpu.roll(c, LANES - 32, 1), pltpu.roll(c, 32, 1))
            r = c * cos + rot * sin_signed
            if kk * LANES < ATTN_WIDTH:
                r = r * q_scale
            o_ref[:, kk * LANES:(kk + 1) * LANES] = r


def _in_projection(x, scale, shift, rope_cos, rope_sin, w_in_bf16, *, tm):
    B, S, D = x.shape
    n_cols = w_in_bf16.shape[1]
    tn = 2 * ATTN_WIDTH
    assert S % tm == 0 and n_cols % tn == 0
    return pl.pallas_call(
        functools.partial(_inproj_kernel, tn=tn),
        grid=(B, S // tm, n_cols // tn),
        in_specs=[pl.BlockSpec((None, tm, D), lambda b, i, j: (b, i, 0)),
                  pl.BlockSpec((None, 1, D), lambda b, i, j: (b, 0, 0)),
                  pl.BlockSpec((None, 1, D), lambda b, i, j: (b, 0, 0)),
                  pl.BlockSpec((None, tm, LANES), lambda b, i, j: (b, i, 0)),
                  pl.BlockSpec((None, tm, LANES), lambda b, i, j: (b, i, 0)),
                  pl.BlockSpec((D, tn), lambda b, i, j: (0, j))],
        out_specs=pl.BlockSpec((None, tm, tn), lambda b, i, j: (b, i, j)),
        out_shape=jax.ShapeDtypeStruct((B, S, n_cols), F32),
        scratch_shapes=[pltpu.VMEM((tm, D), BF16)],
        compiler_params=_params(("arbitrary", "arbitrary", "arbitrary")),
        name="in_projection",
    )(x, scale, shift, rope_cos, rope_sin, w_in_bf16)


ATTN_BLOCK_GROUP = 8


def _attn_kernel(q_ref, k_ref, v_ref, o_ref, qs, ks, vs, qf, kf, vf, pm, plr, pa, nm, nl, na, bias_s,
                 *, S):
    lane = lax.broadcasted_iota(I32, (1, LANES), 1)
    head0 = lane < ATTN_HEAD_DIM

    for p, d in enumerate(DILATIONS):
        L = S // d
        tq = min(128, L)
        W = min(2 * tq, L)
        nb = L // tq

        d_prev = DILATIONS[p - 1] if p else 1
        ratio, l_prev = d // d_prev, S // d_prev
        keep_f32 = 0 < p < len(DILATIONS) - 1
        for src, stage, dst in ((q_ref, qf, qs), (k_ref, kf, ks), (v_ref, vf, vs)):
            src = src if p <= 1 else stage
            for r in range(d):
                a, r_prev = divmod(r, d_prev)
                rows = pl.ds(r_prev * l_prev + a, L, stride=ratio) if ratio > 1 else pl.ds(0, L)
                val = src[rows, :]
                if keep_f32:
                    stage[r * L:(r + 1) * L, :] = val
                dst[r * L:(r + 1) * L, :] = val.astype(BF16)

        n_blocks = d * nb
        rc = lax.broadcasted_iota(I32, (tq, W), 0) - lax.broadcasted_iota(I32, (tq, W), 1)
        for which, delta in enumerate((0, ATTN_RADIUS, tq)):
            bias_s[which, pl.ds(0, tq), pl.ds(0, W)] = jnp.where(
                jnp.abs(rc + delta) <= ATTN_RADIUS, 0.0, MASK_VALUE)
        group = min(ATTN_BLOCK_GROUP, n_blocks)
        assert n_blocks % group == 0

        def block_group(gi, carry, L=L, tq=tq, W=W, nb=nb, group=group):
            q0s, k0s, scores, probs, outs, sums = [], [], [], [], [], []
            for u in range(group):
                g = gi * group + u
                r = g // nb
                n = g - r * nb
                base = r * L
                q0 = pl.multiple_of(base + n * tq, tq)
                ws = jnp.clip(n * tq - ATTN_RADIUS, 0, L - W)
                k0 = pl.multiple_of(base + ws, 16)
                qb = qs[pl.ds(q0, tq), :]
                zero = jnp.zeros_like(qb)
                q2 = jnp.concatenate([jnp.where(head0, qb, zero), jnp.where(head0, zero, qb)], axis=0)
                s = lax.dot_general(q2, ks[pl.ds(k0, W), :], (((1,), (1,)), ((), ())),
                                    preferred_element_type=F32)
                delta = n * tq - ws
                which = jnp.where(delta == 0, 0, jnp.where(delta == ATTN_RADIUS, 1, 2))
                bias = bias_s[which, pl.ds(0, tq), pl.ds(0, W)]
                scores.append(s + jnp.concatenate([bias, bias], axis=0))
                q0s.append(q0)
                k0s.append(k0)
            for u in range(group):
                s = scores[u]
                m = jnp.max(s, axis=1, keepdims=True)
                e = jnp.exp(s - m)
                probs.append((m, e.astype(BF16)))
                sums.append(jnp.sum(e, axis=1, keepdims=True))
            for u in range(group):
                outs.append(jnp.dot(probs[u][1], vs[pl.ds(k0s[u], W), :], preferred_element_type=F32))
            for u in range(group):
                m, l = probs[u][0], sums[u]
                rows = pl.ds(q0s[u], tq)
                pm[rows, :] = jnp.where(head0, m[:tq], m[tq:])
                plr[rows, :] = jnp.where(head0, l[:tq], l[tq:])
                pa[rows, :] = jnp.where(head0, outs[u][:tq], outs[u][tq:])
            return carry

        lax.fori_loop(0, n_blocks // group, block_group, 0)

        for r in range(d):
            rows = pl.ds(r, L, stride=d) if d > 1 else pl.ds(0, L)
            nm[p, rows, :] = pm[r * L:(r + 1) * L, :]
            nl[p, rows, :] = plr[r * L:(r + 1) * L, :]
            na[p, rows, :] = pa[r * L:(r + 1) * L, :]

    m_all = jnp.maximum(jnp.maximum(nm[0], nm[1]), nm[2])
    num = jnp.zeros((S, LANES), F32)
    den = jnp.zeros((S, LANES), F32)
    for p in range(len(DILATIONS)):
        w = jnp.exp(nm[p] - m_all)
        num = num + w * na[p]
        den = den + w * nl[p]
    o_ref[...] = num / den


def _dilated_attention(proj):
    B, S, _ = proj.shape
    n_pairs = ATTN_WIDTH // LANES
    assert S % (16 * DILATIONS[-1]) == 0
    blk = lambda off: pl.BlockSpec((None, S, LANES), lambda b, h: (b, 0, off + h))
    return pl.pallas_call(
        functools.partial(_attn_kernel, S=S),
        grid=(B, n_pairs),
        in_specs=[blk(0), blk(n_pairs), blk(2 * n_pairs)],
        out_specs=pl.BlockSpec((None, S, LANES), lambda b, h: (b, 0, h)),
        out_shape=jax.ShapeDtypeStruct((B, S, ATTN_WIDTH), F32),
        scratch_shapes=[pltpu.VMEM((S, LANES), BF16)] * 3
                       + [pltpu.VMEM((S, LANES), F32)] * 6
                       + [pltpu.VMEM((len(DILATIONS), S, LANES), F32)] * 3
                       + [pltpu.VMEM((3, min(128, S), min(256, S)), F32)],
        compiler_params=_params(("arbitrary", "arbitrary")),
        name="dilated_attention",
    )(proj, proj, proj)


REC_GROUP_CHUNKS = 4
REC_GROUP_ROWS = REC_GROUP_CHUNKS * REC_CHUNK
REC_STAGE_GROUPS = 4


def _hgrn_kernel(rq_ref, zf_ref, zb_ref, ri_ref, rg_ref, lb_ref, gain_ref, o_ref,
                 q_s, vt_s, oi_s, qp_s, ut_s, dd_s, st_s, os_s, *, S):
    C, G, GR = REC_CHUNK, REC_GROUP_CHUNKS, REC_GROUP_ROWS
    nc, ng = S // C, S // GR
    sg = min(REC_STAGE_GROUPS, ng)
    assert ng % sg == 0

    row = lax.broadcasted_iota(I32, (GR, GR), 0)
    col = lax.broadcasted_iota(I32, (GR, GR), 1)
    same_chunk = (row // C) == (col // C)
    t_i = lax.broadcasted_iota(I32, (C, C), 0)
    s_i = lax.broadcasted_iota(I32, (C, C), 1)

    def prep(i, carry):
        r0 = pl.multiple_of(i * GR, GR)
        rq = rq_ref[pl.ds(r0, GR), :]
        q_s[pl.ds(r0, GR), :] = rq * _sigmoid(rq)
        v = ri_ref[pl.ds(r0, GR), :]
        for c in range(G):
            vt_s[i * G + c] = v[c * C:(c + 1) * C, :].T.astype(BF16)
        return carry

    lax.fori_loop(0, ng, prep, 0)

    for direction in range(2):
        fwd = direction == 0
        z_ref = zf_ref if fwd else zb_ref
        cum = jnp.where(same_chunk & ((col <= row) if fwd else (col >= row)), 1.0, 0.0).astype(BF16)
        keep = (s_i <= t_i) if fwd else (s_i >= t_i)
        mid_row = C // 2 - 1 if fwd else C // 2
        last_row = C - 1 if fwd else 0

        def phase_a(it, carry, fwd=fwd, z_ref=z_ref, cum=cum, keep=keep,
                    mid_row=mid_row, last_row=last_row, direction=direction):
            lb = lb_ref[direction:direction + 1, :]
            log_lb = jnp.log(lb)
            log_1m_lb = jnp.log1p(-lb)
            r0s, c0s, k3s, bcs, atts, stage3 = [], [], [], [], [], []
            for u in range(sg):
                i = it * sg + u
                r0 = pl.multiple_of(i * GR, GR)
                z = z_ref[pl.ds(r0, GR), :]
                e = jnp.exp(-jnp.abs(z))
                t = log_1m_lb + (jnp.minimum(z, 0.0) - jnp.log(1.0 + e))
                g = jnp.maximum(log_lb, t) + jnp.log(1.0 + jnp.exp(-jnp.abs(log_lb - t)))
                kk = (1.0 - lb) * (jnp.where(z > 0, e, 1.0) / (1.0 + e))
                g1 = g.astype(BF16)
                r1 = g - g1.astype(F32)
                g2 = r1.astype(BF16)
                g3 = (r1 - g2.astype(F32)).astype(BF16)
                bcs.append(jnp.dot(cum, g1, preferred_element_type=F32)
                           + jnp.dot(cum, g2, preferred_element_type=F32)
                           + jnp.dot(cum, g3, preferred_element_type=F32))
                k3s.append(kk.reshape(G, C, REC_DIM))
                r0s.append(r0)
                c0s.append(pl.multiple_of(i * G, G))
            for u in range(sg):
                b3 = bcs[u].reshape(G, C, REC_DIM)
                mid = b3[:, mid_row:mid_row + 1, :]
                last = b3[:, last_row:last_row + 1, :]
                e3 = b3 - mid
                q3 = q_s[pl.ds(r0s[u], GR), :].reshape(G, C, REC_DIM)
                qt = (q3 * jnp.exp(jnp.minimum(e3, DECAY_EXP_CLAMP))).astype(BF16)
                kt = (k3s[u] * jnp.exp(jnp.minimum(-e3, DECAY_EXP_CLAMP))).astype(BF16)
                atts.append(jnp.einsum('gtc,gsc->gts', qt, kt, preferred_element_type=F32))
                qp_s[pl.ds(r0s[u], GR), :] = (q3 * jnp.exp(b3)).astype(BF16).reshape(GR, REC_DIM)
                stage3.append(((k3s[u] * jnp.exp(last - b3)).astype(BF16), jnp.exp(last)))
            for u in range(sg):
                att = jnp.where(keep[None], atts[u], 0.0).astype(BF16)
                v3 = ri_ref[pl.ds(r0s[u], GR), :].reshape(G, C, REC_DIM).astype(BF16)
                oi = jnp.einsum('gts,gsv->gtv', att, v3, preferred_element_type=F32)
                oi_s[pl.ds(r0s[u], GR), :] = oi.reshape(GR, REC_DIM)
                kp, decay = stage3[u]
                ut_s[pl.ds(c0s[u], G)] = jnp.einsum('gvs,gsc->gvc', vt_s[pl.ds(c0s[u], G)], kp,
                                                    preferred_element_type=F32)
                dd_s[pl.ds(c0s[u], G)] = decay
            return carry

        lax.fori_loop(0, ng // sg, phase_a, 0)

        def scan(c, st, fwd=fwd):
            idx = c if fwd else nc - 1 - c
            st_s[idx] = st.astype(BF16)
            return dd_s[idx] * st + ut_s[idx]

        lax.fori_loop(0, nc, scan, jnp.zeros((REC_DIM, REC_DIM), F32))

        def phase_c(i, carry, fwd=fwd):
            r0 = pl.multiple_of(i * GR, GR)
            c0 = pl.multiple_of(i * G, G)
            qp3 = qp_s[pl.ds(r0, GR), :].reshape(G, C, REC_DIM)
            oo = jnp.einsum('gtc,gvc->gtv', qp3, st_s[pl.ds(c0, G)], preferred_element_type=F32)
            tot = oi_s[pl.ds(r0, GR), :] + oo.reshape(GR, REC_DIM)
            if fwd:
                os_s[pl.ds(r0, GR), :] = tot
            else:
                os_s[pl.ds(r0, GR), :] = os_s[pl.ds(r0, GR), :] + tot
            return carry

        lax.fori_loop(0, ng, phase_c, 0, unroll=2)

    o = os_s[...]
    o = o * lax.rsqrt(jnp.mean(o * o, axis=-1, keepdims=True) + RMS_EPS) * gain_ref[...]
    o_ref[...] = o * _sigmoid(rg_ref[...])


def _hgrn2(proj, lb, rec_gain):
    B, S, _ = proj.shape
    nh = REC_WIDTH // REC_DIM
    assert S % REC_GROUP_ROWS == 0
    col0 = 3 * ATTN_WIDTH // LANES
    blk = lambda k: pl.BlockSpec((None, S, REC_DIM), lambda b, h: (b, 0, col0 + k * nh + h))
    nc = S // REC_CHUNK
    return pl.pallas_call(
        functools.partial(_hgrn_kernel, S=S),
        grid=(B, nh),
        in_specs=[blk(0), blk(1), blk(2), blk(3), blk(4),
                  pl.BlockSpec((2, REC_DIM), lambda b, h: (0, h)),
                  pl.BlockSpec((1, REC_DIM), lambda b, h: (0, h))],
        out_specs=pl.BlockSpec((None, S, REC_DIM), lambda b, h: (b, 0, h)),
        out_shape=jax.ShapeDtypeStruct((B, S, REC_WIDTH), F32),
        scratch_shapes=[pltpu.VMEM((S, REC_DIM), F32),
                        pltpu.VMEM((nc, REC_DIM, REC_CHUNK), BF16),
                        pltpu.VMEM((S, REC_DIM), F32),
                        pltpu.VMEM((S, REC_DIM), BF16),
                        pltpu.VMEM((nc, REC_DIM, REC_DIM), F32),
                        pltpu.VMEM((nc, 1, REC_DIM), F32),
                        pltpu.VMEM((nc, REC_DIM, REC_DIM), BF16),
                        pltpu.VMEM((S, REC_DIM), F32)],
        compiler_params=_params(("arbitrary", "arbitrary")),
        name="hgrn2",
    )(proj, proj, proj, proj, proj, lb, rec_gain)


def _outproj_kernel(a_ref, r_ref, x_ref, gate_ref, ag_ref, w_ref, lg_ref, lb_ref, o_ref, *, alpha):
    a = a_ref[...]
    an = a * lax.rsqrt(jnp.mean(a * a, axis=-1, keepdims=True) + RMS_EPS) * ag_ref[...]
    y = (jnp.dot(an.astype(BF16), w_ref[:ATTN_WIDTH, :], preferred_element_type=F32)
         + jnp.dot(r_ref[...].astype(BF16), w_ref[ATTN_WIDTH:, :], preferred_element_type=F32))
    z = alpha * x_ref[...] + (1.0 + gate_ref[...]) * y
    o_ref[...] = _layer_norm(z, lg_ref[...], lb_ref[...])


def _out_projection(attn, rec, x, gate, attn_gain, w_out_bf16, ln_g, ln_b, *, alpha, tm):
    B, S, D = x.shape
    row = lambda w: pl.BlockSpec((None, tm, w), lambda b, i: (b, i, 0))
    vec = lambda w: pl.BlockSpec((1, w), lambda b, i: (0, 0))
    return pl.pallas_call(
        functools.partial(_outproj_kernel, alpha=alpha),
        grid=(B, S // tm),
        in_specs=[row(ATTN_WIDTH), row(REC_WIDTH), row(D),
                  pl.BlockSpec((None, 1, D), lambda b, i: (b, 0, 0)),
                  vec(ATTN_WIDTH),
                  pl.BlockSpec((ATTN_WIDTH + REC_WIDTH, D), lambda b, i: (0, 0)),
                  vec(D), vec(D)],
        out_specs=row(D),
        out_shape=jax.ShapeDtypeStruct((B, S, D), F32),
        compiler_params=_params(("arbitrary", "arbitrary")),
        name="out_projection",
    )(attn, rec, x, gate, attn_gain, w_out_bf16, ln_g, ln_b)


def _ffn_kernel(x_ref, sc_ref, sh_ref, gate_ref, wg_ref, wu_ref, wd_ref, lg_ref, lb_ref, o_ref,
                *, alpha, tc):
    x = x_ref[...]
    u = (x * (1.0 + sc_ref[...]) + sh_ref[...]).astype(BF16)
    acc = jnp.zeros(x.shape, F32)
    for c in range(wg_ref.shape[1] // tc):
        cols = slice(c * tc, (c + 1) * tc)
        g = jnp.dot(u, wg_ref[:, cols], preferred_element_type=F32)
        up = jnp.dot(u, wu_ref[:, cols], preferred_element_type=F32)
        h = (g * _sigmoid(g) * up).astype(BF16)
        acc = acc + jnp.dot(h, wd_ref[cols, :], preferred_element_type=F32)
    z = alpha * x + (1.0 + gate_ref[...]) * acc
    o_ref[...] = _layer_norm(z, lg_ref[...], lb_ref[...])


def _dense_ffn(x, scale, shift, gate, wg, wu, wd, ln_g, ln_b, *, alpha, tm, tc):
    B, S, D = x.shape
    ffp = wg.shape[1]
    assert ffp % tc == 0
    row = pl.BlockSpec((None, tm, D), lambda b, i: (b, i, 0))
    mod = pl.BlockSpec((None, 1, D), lambda b, i: (b, 0, 0))
    vec = pl.BlockSpec((1, D), lambda b, i: (0, 0))
    return pl.pallas_call(
        functools.partial(_ffn_kernel, alpha=alpha, tc=tc),
        grid=(B, S // tm),
        in_specs=[row, mod, mod, mod,
                  pl.BlockSpec((D, ffp), lambda b, i: (0, 0)),
                  pl.BlockSpec((D, ffp), lambda b, i: (0, 0)),
                  pl.BlockSpec((ffp, D), lambda b, i: (0, 0)),
                  vec, vec],
        out_specs=row,
        out_shape=jax.ShapeDtypeStruct((B, S, D), F32),
        compiler_params=_params(("arbitrary", "arbitrary")),
        name="dense_ffn",
    )(x, scale, shift, gate, wg, wu, wd, ln_g, ln_b)


META_E0, META_E1, META_W0, META_W1, META_R0, META_R1 = range(6)
SUBLANES = 8
DMA_LOOP_UNROLL = 8


def _to_token_tiles(ref, val):
    n = val.shape[0]
    for k in range(SUBLANES):
        ref[pl.ds(k, n, stride=SUBLANES), :] = val[:, k * LANES:(k + 1) * LANES]


def _from_token_tiles(ref, n):
    return [ref[pl.ds(k, n, stride=SUBLANES), :] for k in range(SUBLANES)]


def _token_copy(src_hbm, token, dst, slot, sem):
    return pltpu.make_async_copy(
        src_hbm.at[pl.ds(pl.multiple_of(token * SUBLANES, SUBLANES), SUBLANES)],
        dst.at[pl.ds(pl.multiple_of(slot * SUBLANES, SUBLANES), SUBLANES)], sem)


def _router_kernel(x_ref, sc_ref, sh_ref, wr_ref, u_ref, meta_ref, cnt_ref, carry_ref, *, n_exp):
    tm = x_ref.shape[0]

    @pl.when((pl.program_id(0) == 0) & (pl.program_id(1) == 0))
    def _():
        carry_ref[...] = jnp.zeros_like(carry_ref)

    u = x_ref[...] * (1.0 + sc_ref[...]) + sh_ref[...]
    _to_token_tiles(u_ref, u)
    logits = jnp.dot(u, wr_ref[...], precision=lax.Precision.HIGHEST, preferred_element_type=F32)
    lane = lax.broadcasted_iota(I32, (tm, LANES), 1)
    neg = -jnp.inf
    l1 = jnp.where(lane < n_exp, logits, neg)
    m1 = jnp.max(l1, axis=1, keepdims=True)
    i1 = jnp.min(jnp.where(l1 == m1, lane, LANES), axis=1, keepdims=True)
    l2 = jnp.where(lane == i1, neg, l1)
    m2 = jnp.max(l2, axis=1, keepdims=True)
    i2 = jnp.min(jnp.where(l2 == m2, lane, LANES), axis=1, keepdims=True)
    e = jnp.exp(m2 - m1)
    w1 = 1.0 / (1.0 + e)
    w2 = e * w1
    sel = jnp.where((lane == i1) | (lane == i2), 1.0, 0.0)
    before = (lax.broadcasted_iota(I32, (tm, tm), 1) < lax.broadcasted_iota(I32, (tm, tm), 0))
    ranks = jnp.dot(jnp.where(before, 1.0, 0.0).astype(BF16), sel.astype(BF16),
                    preferred_element_type=F32) + carry_ref[...]
    r1 = jnp.sum(jnp.where(lane == i1, ranks, 0.0), axis=1, keepdims=True)
    r2 = jnp.sum(jnp.where(lane == i2, ranks, 0.0), axis=1, keepdims=True)
    carry_ref[...] = carry_ref[...] + jnp.sum(sel, axis=0, keepdims=True)
    cnt_ref[...] = carry_ref[...]
    meta = jnp.zeros((tm, LANES), F32)
    for k, val in ((META_E0, i1.astype(F32)), (META_E1, i2.astype(F32)), (META_W0, w1),
                   (META_W1, w2), (META_R0, r1), (META_R1, r2)):
        meta = jnp.where(lane == k, val, meta)
    meta_ref[...] = meta


def _router(x, scale, shift, w_router_lanes, *, n_exp, tm):
    B, S, D = x.shape
    assert D == SUBLANES * LANES
    nt = S // tm
    row = pl.BlockSpec((None, tm, D), lambda b, i: (b, i, 0))
    mod = pl.BlockSpec((None, 1, D), lambda b, i: (b, 0, 0))
    return pl.pallas_call(
        functools.partial(_router_kernel, n_exp=n_exp),
        grid=(B, nt),
        in_specs=[row, mod, mod, pl.BlockSpec((D, LANES), lambda b, i: (0, 0))],
        out_specs=[pl.BlockSpec((tm * SUBLANES, LANES), lambda b, i: (b * nt + i, 0)),
                   pl.BlockSpec((None, tm, LANES), lambda b, i: (b, i, 0)),
                   pl.BlockSpec((1, LANES), lambda b, i: (0, 0))],
        out_shape=[jax.ShapeDtypeStruct((B * S * SUBLANES, LANES), F32),
                   jax.ShapeDtypeStruct((B, S, LANES), F32),
                   jax.ShapeDtypeStruct((1, LANES), F32)],
        scratch_shapes=[pltpu.VMEM((1, LANES), F32)],
        compiler_params=_params(("arbitrary", "arbitrary")),
        name="moe_router",
    )(x, scale, shift, w_router_lanes)


def _expert_kernel(te_ref, na_ref, src_cur, src_nxt, u_hbm, wg_ref, wu_ref, wd_ref, o_ref,
                   xs_buf, xb_ref, acc_ref, sems):
    i = pl.program_id(0)
    j = pl.program_id(1)
    last = pl.num_programs(1) - 1
    n_active = na_ref[0]
    active = i < n_active
    tm = xb_ref.shape[0]
    slot = i % 2

    def issue_gather(src_ref, dst_slot):
        def body(g, carry):
            for u in range(DMA_LOOP_UNROLL):
                r = g * DMA_LOOP_UNROLL + u
                _token_copy(u_hbm, src_ref[0, r], xs_buf.at[dst_slot], r, sems.at[dst_slot]).start()
            return carry
        lax.fori_loop(0, tm // DMA_LOOP_UNROLL, body, 0)

    @pl.when((i == 0) & (j == 0))
    def _():
        issue_gather(src_cur, 0)

    @pl.when(active & (j == 0))
    def _():
        def drain(r, carry):
            _token_copy(u_hbm, 0, xs_buf.at[slot], r, sems.at[slot]).wait()
            return carry
        lax.fori_loop(0, tm, drain, 0, unroll=DMA_LOOP_UNROLL)
        for k, chunk in enumerate(_from_token_tiles(xs_buf.at[slot], tm)):
            xb_ref[:, k * LANES:(k + 1) * LANES] = chunk.astype(BF16)
        acc_ref[...] = jnp.zeros_like(acc_ref)

    @pl.when((j == 1) & (i + 1 < n_active))
    def _():
        issue_gather(src_nxt, 1 - slot)

    @pl.when(active)
    def _():
        xb = xb_ref[...]
        g = jnp.dot(xb, wg_ref[...].astype(BF16), preferred_element_type=F32)
        up = jnp.dot(xb, wu_ref[...].astype(BF16), preferred_element_type=F32)
        h = (g * _sigmoid(g) * up).astype(BF16)
        acc_ref[...] += jnp.dot(h, wd_ref[...].astype(BF16), preferred_element_type=F32)

    @pl.when(active & (j == last))
    def _():
        _to_token_tiles(o_ref, acc_ref[...])

    @pl.when(jnp.logical_not(active) & (j == last))
    def _():
        o_ref[...] = jnp.zeros_like(o_ref)


def _expert_ffn(u_tiles, src_tiles, tile_expert, n_active, wg, wu, wd, *, tf):
    n_tiles, _, tm = src_tiles.shape
    D = wg.shape[1]
    ff = wg.shape[2]
    assert ff % tf == 0 and ff // tf >= 2 and tm % DMA_LOOP_UNROLL == 0
    nj = ff // tf

    def jj(i, j, na):
        return jnp.where(i < na[0], j, nj - 1)

    ids = lambda f: pl.BlockSpec((None, 1, tm), lambda i, j, te, na: (f(i), 0, 0),
                                 memory_space=pltpu.SMEM)
    grid_spec = pltpu.PrefetchScalarGridSpec(
        num_scalar_prefetch=2,
        grid=(n_tiles, nj),
        in_specs=[ids(lambda i: i), ids(lambda i: jnp.minimum(i + 1, n_tiles - 1)),
                  pl.BlockSpec(memory_space=pl.ANY),
                  pl.BlockSpec((None, D, tf), lambda i, j, te, na: (te[i], 0, jj(i, j, na))),
                  pl.BlockSpec((None, D, tf), lambda i, j, te, na: (te[i], 0, jj(i, j, na))),
                  pl.BlockSpec((None, tf, D), lambda i, j, te, na: (te[i], jj(i, j, na), 0))],
        out_specs=pl.BlockSpec((tm * SUBLANES, LANES), lambda i, j, te, na: (i, 0)),
        scratch_shapes=[pltpu.VMEM((2, tm * SUBLANES, LANES), F32),
                        pltpu.VMEM((tm, D), BF16), pltpu.VMEM((tm, D), F32),
                        pltpu.SemaphoreType.DMA((2,))],
    )
    return pl.pallas_call(
        _expert_kernel,
        grid_spec=grid_spec,
        out_shape=jax.ShapeDtypeStruct((n_tiles * tm * SUBLANES, LANES), F32),
        compiler_params=_params(("arbitrary", "arbitrary")),
        name="moe_experts",
    )(tile_expert, n_active, src_tiles, src_tiles, u_tiles, wg, wu, wd)


def _combine_kernel(d0_ref, d1_ref, ys_hbm, meta_ref, x_ref, gate_ref, lg_ref, lb_ref, o_ref,
                    buf, sem, *, alpha):
    tm = x_ref.shape[0]

    def issue(g, carry):
        for u in range(DMA_LOOP_UNROLL):
            r = g * DMA_LOOP_UNROLL + u
            _token_copy(ys_hbm, d0_ref[0, r], buf.at[0], r, sem).start(priority=0)
            _token_copy(ys_hbm, d1_ref[0, r], buf.at[1], r, sem).start(priority=1)
        return carry

    def drain(r, carry):
        _token_copy(ys_hbm, 0, buf.at[0], r, sem).wait()
        _token_copy(ys_hbm, 0, buf.at[1], r, sem).wait()
        return carry

    lax.fori_loop(0, tm // DMA_LOOP_UNROLL, issue, 0)
    lax.fori_loop(0, tm, drain, 0, unroll=DMA_LOOP_UNROLL)
    meta = meta_ref[...]
    w0 = meta[:, META_W0:META_W0 + 1]
    w1 = meta[:, META_W1:META_W1 + 1]
    y = jnp.concatenate([w0 * a + w1 * b for a, b in zip(_from_token_tiles(buf.at[0], tm),
                                                         _from_token_tiles(buf.at[1], tm))], axis=1)
    z = alpha * x_ref[...] + (1.0 + gate_ref[...]) * y
    o_ref[...] = _layer_norm(z, lg_ref[...], lb_ref[...])


def _combine(ys, d0_tiles, d1_tiles, meta, x, gate, ln_g, ln_b, *, alpha, tm):
    B, S, D = x.shape
    nt = S // tm
    ids = pl.BlockSpec((None, 1, tm), lambda b, i: (b * nt + i, 0, 0), memory_space=pltpu.SMEM)
    row = pl.BlockSpec((None, tm, D), lambda b, i: (b, i, 0))
    vec = pl.BlockSpec((1, D), lambda b, i: (0, 0))
    return pl.pallas_call(
        functools.partial(_combine_kernel, alpha=alpha),
        grid=(B, nt),
        in_specs=[ids, ids, pl.BlockSpec(memory_space=pl.ANY),
                  pl.BlockSpec((None, tm, LANES), lambda b, i: (b, i, 0)),
                  row, pl.BlockSpec((None, 1, D), lambda b, i: (b, 0, 0)), vec, vec],
        out_specs=row,
        out_shape=jax.ShapeDtypeStruct((B, S, D), F32),
        scratch_shapes=[pltpu.VMEM((2, tm * SUBLANES, LANES), F32), pltpu.SemaphoreType.DMA(())],
        compiler_params=_params(("arbitrary", "arbitrary")),
        name="moe_combine",
    )(d0_tiles, d1_tiles, ys, meta, x, gate, ln_g, ln_b)


def _moe_ffn(x, scale, shift, gate, w_router, wg, wu, wd, ln_g, ln_b, *, alpha, tm, tm_e, tf):
    B, S, D = x.shape
    T = B * S
    n_exp = w_router.shape[1]
    wr = jnp.zeros((D, LANES), F32).at[:, :n_exp].set(w_router)
    u_tiles, meta, counts = _router(x, scale, shift, wr, n_exp=n_exp, tm=tm)

    meta2 = meta.reshape(T, LANES)
    e0 = meta2[:, META_E0].astype(I32)
    e1 = meta2[:, META_E1].astype(I32)
    r0 = meta2[:, META_R0].astype(I32)
    r1 = meta2[:, META_R1].astype(I32)
    cnt = counts[0, :n_exp].astype(I32)
    sizes = ((cnt + tm_e - 1) // tm_e) * tm_e
    ends = jnp.cumsum(sizes)
    starts = ends - sizes
    d0 = starts[e0] + r0
    d1 = starts[e1] + r1
    n_tiles = (TOP_K * T) // tm_e + n_exp
    tok = jnp.arange(T, dtype=I32)
    src = jnp.zeros((n_tiles * tm_e,), I32).at[d0].set(tok).at[d1].set(tok)
    n_active = (ends[-1] // tm_e).astype(I32)
    tile_id = jnp.minimum(jnp.arange(n_tiles, dtype=I32), n_active - 1)
    tile_expert = jnp.sum((ends[None, :] <= (tile_id * tm_e)[:, None]).astype(I32), axis=1)

    ys = _expert_ffn(u_tiles, src.reshape(n_tiles, 1, tm_e), tile_expert, n_active.reshape(1),
                     wg, wu, wd, tf=tf)
    nt = T // tm
    return _combine(ys, d0.reshape(nt, 1, tm), d1.reshape(nt, 1, tm), meta, x, gate, ln_g, ln_b,
                    alpha=alpha, tm=tm)


def kernel(x, c, positions, w_in, w_out, attn_norm_gain, rec_norm_gain, rec_lb_logits, ada_w, ada_b,
           ln_gain, ln_bias, ffn_w_gate, ffn_w_up, ffn_w_down, moe_router, moe_w_gate, moe_w_up,
           moe_w_down):
    B, S, D = x.shape
    depth = w_in.shape[0]
    alpha = (2 * depth) ** 0.25

    p = jax.nn.softmax(rec_lb_logits.astype(F32), axis=0)
    cum = jnp.cumsum(p, axis=0)
    lb_all = cum - cum[0:1]
    half = ATTN_HEAD_DIM // 2
    inv_freq = ROPE_THETA ** (-jnp.arange(half, dtype=F32) / half)
    inv_freq_lanes = jnp.tile(inv_freq, LANES // half).reshape(1, LANES)
    pos_f = positions.astype(F32).reshape(B, S, 1)
    w_in_b = w_in.astype(BF16)
    w_out_b = w_out.astype(BF16)
    ff = ffn_w_gate.shape[2]
    ffp = -(-ff // (2 * LANES)) * (2 * LANES)
    pad_c = lambda w: jnp.pad(w.astype(BF16), ((0, 0), (0, 0), (0, ffp - ff)))
    ffn_g, ffn_u = pad_c(ffn_w_gate), pad_c(ffn_w_up)
    ffn_d = jnp.pad(ffn_w_down.astype(BF16), ((0, 0), (0, ffp - ff), (0, 0)))

    mods = _ada_modulation(c, ada_w, ada_b)
    rope_cos, rope_sin = _rope_tables(pos_f, inv_freq_lanes, tm=min(1024, S))

    def mod(layer, sub):
        m = mods[layer * 2 + sub].reshape(B, 1, 3 * D)
        return m[..., :D], m[..., D:2 * D], m[..., 2 * D:]

    vec = lambda a: a.reshape(1, -1)
    for layer in range(depth):
        shift, scale, gate = mod(layer, 0)
        proj = _in_projection(x, scale, shift, rope_cos, rope_sin, w_in_b[layer], tm=min(1024, S))
        attn = _dilated_attention(proj)
        rec = _hgrn2(proj, lb_all[layer], vec(rec_norm_gain[layer]))
        x = _out_projection(attn, rec, x, gate, vec(attn_norm_gain[layer]), w_out_b[layer],
                            vec(ln_gain[layer, 0]), vec(ln_bias[layer, 0]), alpha=alpha, tm=min(512, S))
        shift, scale, gate = mod(layer, 1)
        j = layer // 2
        if layer % 2 == 0:
            x = _dense_ffn(x, scale, shift, gate, ffn_g[j], ffn_u[j], ffn_d[j],
                           vec(ln_gain[layer, 1]), vec(ln_bias[layer, 1]),
                           alpha=alpha, tm=min(512, S), tc=2 * LANES)
        else:
            x = _moe_ffn(x, scale, shift, gate, moe_router[j], moe_w_gate[j], moe_w_up[j],
                         moe_w_down[j], vec(ln_gain[layer, 1]), vec(ln_bias[layer, 1]),
                         alpha=alpha, tm=min(512, S), tm_e=min(1024, B * S), tf=512)
    return x
```

```python
import functools

import jax
import jax.numpy as jnp
from jax import lax
from jax.experimental import pallas as pl
from jax.experimental.pallas import tpu as pltpu

F32 = jnp.float32
BF16 = jnp.bfloat16
I32 = jnp.int32

LANES = 128
V7X_VMEM_LIMIT_BYTES = 56 * 1024 * 1024

ATTN_WIDTH = 512
ATTN_HEAD_DIM = 64
DILATIONS = (1, 4, 16)
ATTN_RADIUS = 64
ROPE_THETA = 10000.0
REC_WIDTH = 512
REC_DIM = 128
REC_CHUNK = 64
TOP_K = 2
LN_EPS = 1e-5
RMS_EPS = 1e-6
MASK_VALUE = -1e30
DECAY_EXP_CLAMP = 80.0


def _params(semantics):
    return pltpu.CompilerParams(dimension_semantics=semantics,
                                vmem_limit_bytes=V7X_VMEM_LIMIT_BYTES)


def _sigmoid(x):
    return 1.0 / (1.0 + jnp.exp(-x))


def _layer_norm(z, gain, bias):
    mu = jnp.mean(z, axis=-1, keepdims=True)
    zc = z - mu
    var = jnp.mean(zc * zc, axis=-1, keepdims=True)
    return zc * lax.rsqrt(var + LN_EPS) * gain + bias


def _ada_kernel(c_ref, w_ref, b_ref, o_ref):
    c = c_ref[...]
    o_ref[...] = jnp.dot(c * _sigmoid(c), w_ref[...], precision=lax.Precision.HIGHEST,
                         preferred_element_type=F32) + b_ref[...]


def _ada_modulation(c, ada_w, ada_b):
    B, D = c.shape
    n = ada_w.shape[0] * ada_w.shape[1]
    n3 = ada_w.shape[-1]
    tn = 1024
    return pl.pallas_call(
        _ada_kernel,
        grid=(n, n3 // tn),
        in_specs=[pl.BlockSpec((B, D), lambda i, j: (0, 0)),
                  pl.BlockSpec((None, D, tn), lambda i, j: (i, 0, j)),
                  pl.BlockSpec((None, 1, tn), lambda i, j: (i, 0, j))],
        out_specs=pl.BlockSpec((None, B, tn), lambda i, j: (i, 0, j)),
        out_shape=jax.ShapeDtypeStruct((n, B, n3), F32),
        compiler_params=_params(("arbitrary", "arbitrary")),
        name="ada_modulation",
    )(c, ada_w.reshape(n, D, n3), ada_b.reshape(n, 1, n3))


def _rope_kernel(pos_ref, invf_ref, cos_ref, sin_ref):
    ang = pos_ref[...] * invf_ref[...]
    lane = lax.broadcasted_iota(I32, (1, LANES), 1)
    first_half = (lane % ATTN_HEAD_DIM) < (ATTN_HEAD_DIM // 2)
    sin = jnp.sin(ang)
    cos_ref[...] = jnp.cos(ang)
    sin_ref[...] = jnp.where(first_half, -sin, sin)


def _rope_tables(pos_f, inv_freq_lanes, *, tm):
    B, S, _ = pos_f.shape
    tab = pl.BlockSpec((None, tm, LANES), lambda b, i: (b, i, 0))
    return pl.pallas_call(
        _rope_kernel,
        grid=(B, S // tm),
        in_specs=[pl.BlockSpec((None, tm, 1), lambda b, i: (b, i, 0)),
                  pl.BlockSpec((1, LANES), lambda b, i: (0, 0))],
        out_specs=[tab, tab],
        out_shape=[jax.ShapeDtypeStruct((B, S, LANES), F32)] * 2,
        compiler_params=_params(("arbitrary", "arbitrary")),
        name="rope_tables",
    )(pos_f, inv_freq_lanes)


def _inproj_kernel(x_ref, sc_ref, sh_ref, cos_ref, sin_ref, w_ref, o_ref, u_ref, *, tn):
    j = pl.program_id(2)

    @pl.when(j == 0)
    def _():
        u_ref[...] = (x_ref[...] * (1.0 + sc_ref[...]) + sh_ref[...]).astype(BF16)

    cols = pl.ds(pl.multiple_of(j * tn, tn), tn)
    acc = jnp.dot(u_ref[...], w_ref[:, cols], preferred_element_type=F32)

    @pl.when(j > 0)
    def _():
        o_ref[...] = acc

    @pl.when(j == 0)
    def _():
        cos = cos_ref[...]
        sin_signed = sin_ref[...]
        lane = lax.broadcasted_iota(I32, (1, LANES), 1)
        first_half = (lane % ATTN_HEAD_DIM) < (ATTN_HEAD_DIM // 2)
        q_scale = ATTN_HEAD_DIM ** -0.5
        for kk in range(tn // LANES):
            c = acc[:, kk * LANES:(kk + 1) * LANES]
            rot = jnp.where(first_half, pltpu.roll(c, LANES - 32, 1), pltpu.roll(c, 32, 1))
            r = c * cos + rot * sin_signed
            if kk * LANES < ATTN_WIDTH:
                r = r * q_scale
            o_ref[:, kk * LANES:(kk + 1) * LANES] = r


def _in_projection(x, scale, shift, rope_cos, rope_sin, w_in_bf16, *, tm):
    B, S, D = x.shape
    n_cols = w_in_bf16.shape[1]
    tn = 2 * ATTN_WIDTH
    assert S % tm == 0 and n_cols % tn == 0
    return pl.pallas_call(
        functools.partial(_inproj_kernel, tn=tn),
        grid=(B, S // tm, n_cols // tn),
        in_specs=[pl.BlockSpec((None, tm, D), lambda b, i, j: (b, i, 0)),
                  pl.BlockSpec((None, 1, D), lambda b, i, j: (b, 0, 0)),
                  pl.BlockSpec((None, 1, D), lambda b, i, j: (b, 0, 0)),
                  pl.BlockSpec((None, tm, LANES), lambda b, i, j: (b, i, 0)),
                  pl.BlockSpec((None, tm, LANES), lambda b, i, j: (b, i, 0)),
                  pl.BlockSpec((D, n_cols), lambda b, i, j: (0, 0))],
        out_specs=pl.BlockSpec((None, tm, tn), lambda b, i, j: (b, i, j)),
        out_shape=jax.ShapeDtypeStruct((B, S, n_cols), F32),
        scratch_shapes=[pltpu.VMEM((tm, D), BF16)],
        compiler_params=_params(("arbitrary", "arbitrary", "arbitrary")),
        name="in_projection",
    )(x, scale, shift, rope_cos, rope_sin, w_in_bf16)


ATTN_BLOCK_GROUP = 8


def _attn_kernel(q_ref, k_ref, v_ref, o_ref, qs, ks, vs, qf, kf, vf, pm, plr, pa, nm, nl, na, bias_s,
                 *, S):
    lane = lax.broadcasted_iota(I32, (1, LANES), 1)
    head0 = lane < ATTN_HEAD_DIM

    for p, d in enumerate(DILATIONS):
        L = S // d
        tq = min(128, L)
        W = min(2 * tq, L)
        nb = L // tq

        d_prev = DILATIONS[p - 1] if p else 1
        ratio, l_prev = d // d_prev, S // d_prev
        keep_f32 = 0 < p < len(DILATIONS) - 1
        for src, stage, dst in ((q_ref, qf, qs), (k_ref, kf, ks), (v_ref, vf, vs)):
            src = src if p <= 1 else stage
            for r in range(d):
                a, r_prev = divmod(r, d_prev)
                rows = pl.ds(r_prev * l_prev + a, L, stride=ratio) if ratio > 1 else pl.ds(0, L)
                val = src[rows, :]
                if keep_f32:
                    stage[r * L:(r + 1) * L, :] = val
                dst[r * L:(r + 1) * L, :] = val.astype(BF16)

        n_blocks = d * nb
        rc = lax.broadcasted_iota(I32, (tq, W), 0) - lax.broadcasted_iota(I32, (tq, W), 1)
        for which, delta in enumerate((0, ATTN_RADIUS, tq)):
            bias_s[which, pl.ds(0, tq), pl.ds(0, W)] = jnp.where(
                jnp.abs(rc + delta) <= ATTN_RADIUS, 0.0, MASK_VALUE)
        group = min(ATTN_BLOCK_GROUP, n_blocks)
        assert n_blocks % group == 0

        def block_group(gi, carry, L=L, tq=tq, W=W, nb=nb, group=group):
            q0s, k0s, scores, probs, outs, sums = [], [], [], [], [], []
            for u in range(group):
                g = gi * group + u
                r = g // nb
                n = g - r * nb
                base = r * L
                q0 = pl.multiple_of(base + n * tq, tq)
                ws = jnp.clip(n * tq - ATTN_RADIUS, 0, L - W)
                k0 = pl.multiple_of(base + ws, 16)
                qb = qs[pl.ds(q0, tq), :]
                zero = jnp.zeros_like(qb)
                q2 = jnp.concatenate([jnp.where(head0, qb, zero), jnp.where(head0, zero, qb)], axis=0)
                s = lax.dot_general(q2, ks[pl.ds(k0, W), :], (((1,), (1,)), ((), ())),
                                    preferred_element_type=F32)
                delta = n * tq - ws
                which = jnp.where(delta == 0, 0, jnp.where(delta == ATTN_RADIUS, 1, 2))
                bias = bias_s[which, pl.ds(0, tq), pl.ds(0, W)]
                scores.append(s + jnp.concatenate([bias, bias], axis=0))
                q0s.append(q0)
                k0s.append(k0)
            for u in range(group):
                s = scores[u]
                m = jnp.max(s, axis=1, keepdims=True)
                e = jnp.exp(s - m)
                probs.append((m, e.astype(BF16)))
                sums.append(jnp.sum(e, axis=1, keepdims=True))
            for u in range(group):
                outs.append(jnp.dot(probs[u][1], vs[pl.ds(k0s[u], W), :], preferred_element_type=F32))
            for u in range(group):
                m, l = probs[u][0], sums[u]
                rows = pl.ds(q0s[u], tq)
                pm[rows, :] = jnp.where(head0, m[:tq], m[tq:])
                plr[rows, :] = jnp.where(head0, l[:tq], l[tq:])
                pa[rows, :] = jnp.where(head0, outs[u][:tq], outs[u][tq:])
            return carry

        lax.fori_loop(0, n_blocks // group, block_group, 0)

        for r in range(d):
            rows = pl.ds(r, L, stride=d) if d > 1 else pl.ds(0, L)
            nm[p, rows, :] = pm[r * L:(r + 1) * L, :]
            nl[p, rows, :] = plr[r * L:(r + 1) * L, :]
            na[p, rows, :] = pa[r * L:(r + 1) * L, :]

    m_all = jnp.maximum(jnp.maximum(nm[0], nm[1]), nm[2])
    num = jnp.zeros((S, LANES), F32)
    den = jnp.zeros((S, LANES), F32)
    for p in range(len(DILATIONS)):
        w = jnp.exp(nm[p] - m_all)
        num = num + w * na[p]
        den = den + w * nl[p]
    o_ref[...] = num / den


def _dilated_attention(proj):
    B, S, _ = proj.shape
    n_pairs = ATTN_WIDTH // LANES
    assert S % (16 * DILATIONS[-1]) == 0
    blk = lambda off: pl.BlockSpec((None, S, LANES), lambda b, h: (b, 0, off + h))
    return pl.pallas_call(
        functools.partial(_attn_kernel, S=S),
        grid=(B, n_pairs),
        in_specs=[blk(0), blk(n_pairs), blk(2 * n_pairs)],
        out_specs=pl.BlockSpec((None, S, LANES), lambda b, h: (b, 0, h)),
        out_shape=jax.ShapeDtypeStruct((B, S, ATTN_WIDTH), F32),
        scratch_shapes=[pltpu.VMEM((S, LANES), BF16)] * 3
                       + [pltpu.VMEM((S, LANES), F32)] * 6
                       + [pltpu.VMEM((len(DILATIONS), S, LANES), F32)] * 3
                       + [pltpu.VMEM((3, min(128, S), min(256, S)), F32)],
        compiler_params=_params(("arbitrary", "arbitrary")),
        name="dilated_attention",
    )(proj, proj, proj)


REC_GROUP_CHUNKS = 4
REC_GROUP_ROWS = REC_GROUP_CHUNKS * REC_CHUNK
REC_STAGE_GROUPS = 4


def _hgrn_kernel(rq_ref, zf_ref, zb_ref, ri_ref, rg_ref, lb_ref, gain_ref, o_ref,
                 q_s, vt_s, oi_s, qp_s, ut_s, dd_s, st_s, os_s, *, S):
    C, G, GR = REC_CHUNK, REC_GROUP_CHUNKS, REC_GROUP_ROWS
    nc, ng = S // C, S // GR
    sg = min(REC_STAGE_GROUPS, ng)
    assert ng % sg == 0

    row = lax.broadcasted_iota(I32, (GR, GR), 0)
    col = lax.broadcasted_iota(I32, (GR, GR), 1)
    same_chunk = (row // C) == (col // C)
    t_i = lax.broadcasted_iota(I32, (C, C), 0)
    s_i = lax.broadcasted_iota(I32, (C, C), 1)

    def prep(i, carry):
        r0 = pl.multiple_of(i * GR, GR)
        rq = rq_ref[pl.ds(r0, GR), :]
        q_s[pl.ds(r0, GR), :] = rq * _sigmoid(rq)
        v = ri_ref[pl.ds(r0, GR), :]
        for c in range(G):
            vt_s[i * G + c] = v[c * C:(c + 1) * C, :].T.astype(BF16)
        return carry

    lax.fori_loop(0, ng, prep, 0)

    for direction in range(2):
        fwd = direction == 0
        z_ref = zf_ref if fwd else zb_ref
        cum = jnp.where(same_chunk & ((col <= row) if fwd else (col >= row)), 1.0, 0.0).astype(BF16)
        keep = (s_i <= t_i) if fwd else (s_i >= t_i)
        mid_row = C // 2 - 1 if fwd else C // 2
        last_row = C - 1 if fwd else 0

        def phase_a(it, carry, fwd=fwd, z_ref=z_ref, cum=cum, keep=keep,
                    mid_row=mid_row, last_row=last_row, direction=direction):
            lb = lb_ref[direction:direction + 1, :]
            log_lb = jnp.log(lb)
            log_1m_lb = jnp.log1p(-lb)
            r0s, c0s, k3s, bcs, atts, stage3 = [], [], [], [], [], []
            for u in range(sg):
                i = it * sg + u
                r0 = pl.multiple_of(i * GR, GR)
                z = z_ref[pl.ds(r0, GR), :]
                e = jnp.exp(-jnp.abs(z))
                t = log_1m_lb + (jnp.minimum(z, 0.0) - jnp.log(1.0 + e))
                g = jnp.maximum(log_lb, t) + jnp.log(1.0 + jnp.exp(-jnp.abs(log_lb - t)))
                kk = (1.0 - lb) * (jnp.where(z > 0, e, 1.0) / (1.0 + e))
                g1 = g.astype(BF16)
                r1 = g - g1.astype(F32)
                g2 = r1.astype(BF16)
                g3 = (r1 - g2.astype(F32)).astype(BF16)
                bcs.append(jnp.dot(cum, g1, preferred_element_type=F32)
                           + jnp.dot(cum, g2, preferred_element_type=F32)
                           + jnp.dot(cum, g3, preferred_element_type=F32))
                k3s.append(kk.reshape(G, C, REC_DIM))
                r0s.append(r0)
                c0s.append(pl.multiple_of(i * G, G))
            for u in range(sg):
                b3 = bcs[u].reshape(G, C, REC_DIM)
                mid = b3[:, mid_row:mid_row + 1, :]
                last = b3[:, last_row:last_row + 1, :]
                e3 = b3 - mid
                q3 = q_s[pl.ds(r0s[u], GR), :].reshape(G, C, REC_DIM)
                qt = (q3 * jnp.exp(jnp.minimum(e3, DECAY_EXP_CLAMP))).astype(BF16)
                kt = (k3s[u] * jnp.exp(jnp.minimum(-e3, DECAY_EXP_CLAMP))).astype(BF16)
                atts.append(jnp.einsum('gtc,gsc->gts', qt, kt, preferred_element_type=F32))
                qp_s[pl.ds(r0s[u], GR), :] = (q3 * jnp.exp(b3)).astype(BF16).reshape(GR, REC_DIM)
                stage3.append(((k3s[u] * jnp.exp(last - b3)).astype(BF16), jnp.exp(last)))
            for u in range(sg):
                att = jnp.where(keep[None], atts[u], 0.0).astype(BF16)
                v3 = ri_ref[pl.ds(r0s[u], GR), :].reshape(G, C, REC_DIM).astype(BF16)
                oi = jnp.einsum('gts,gsv->gtv', att, v3, preferred_element_type=F32)
                oi_s[pl.ds(r0s[u], GR), :] = oi.reshape(GR, REC_DIM)
                kp, decay = stage3[u]
                ut_s[pl.ds(c0s[u], G)] = jnp.einsum('gvs,gsc->gvc', vt_s[pl.ds(c0s[u], G)], kp,
                                                    preferred_element_type=F32)
                dd_s[pl.ds(c0s[u], G)] = decay
            return carry

        lax.fori_loop(0, ng // sg, phase_a, 0)

        def scan(c, st, fwd=fwd):
            idx = c if fwd else nc - 1 - c
            st_s[idx] = st.astype(BF16)
            return dd_s[idx] * st + ut_s[idx]

        lax.fori_loop(0, nc, scan, jnp.zeros((REC_DIM, REC_DIM), F32))

        def phase_c(i, carry, fwd=fwd):
            r0 = pl.multiple_of(i * GR, GR)
            c0 = pl.multiple_of(i * G, G)
            qp3 = qp_s[pl.ds(r0, GR), :].reshape(G, C, REC_DIM)
            oo = jnp.einsum('gtc,gvc->gtv', qp3, st_s[pl.ds(c0, G)], preferred_element_type=F32)
            tot = oi_s[pl.ds(r0, GR), :] + oo.reshape(GR, REC_DIM)
            if fwd:
                os_s[pl.ds(r0, GR), :] = tot
            else:
                os_s[pl.ds(r0, GR), :] = os_s[pl.ds(r0, GR), :] + tot
            return carry

        lax.fori_loop(0, ng, phase_c, 0, unroll=2)

    o = os_s[...]
    o = o * lax.rsqrt(jnp.mean(o * o, axis=-1, keepdims=True) + RMS_EPS) * gain_ref[...]
    o_ref[...] = o * _sigmoid(rg_ref[...])


def _hgrn2(proj, lb, rec_gain):
    B, S, _ = proj.shape
    nh = REC_WIDTH // REC_DIM
    assert S % REC_GROUP_ROWS == 0
    col0 = 3 * ATTN_WIDTH // LANES
    blk = lambda k: pl.BlockSpec((None, S, REC_DIM), lambda b, h: (b, 0, col0 + k * nh + h))
    nc = S // REC_CHUNK
    return pl.pallas_call(
        functools.partial(_hgrn_kernel, S=S),
        grid=(B, nh),
        in_specs=[blk(0), blk(1), blk(2), blk(3), blk(4),
                  pl.BlockSpec((2, REC_DIM), lambda b, h: (0, h)),
                  pl.BlockSpec((1, REC_DIM), lambda b, h: (0, h))],
        out_specs=pl.BlockSpec((None, S, REC_DIM), lambda b, h: (b, 0, h)),
        out_shape=jax.ShapeDtypeStruct((B, S, REC_WIDTH), F32),
        scratch_shapes=[pltpu.VMEM((S, REC_DIM), F32),
                        pltpu.VMEM((nc, REC_DIM, REC_CHUNK), BF16),
                        pltpu.VMEM((S, REC_DIM), F32),
                        pltpu.VMEM((S, REC_DIM), BF16),
                        pltpu.VMEM((nc, REC_DIM, REC_DIM), F32),
                        pltpu.VMEM((nc, 1, REC_DIM), F32),
                        pltpu.VMEM((nc, REC_DIM, REC_DIM), BF16),
                        pltpu.VMEM((S, REC_DIM), F32)],
        compiler_params=_params(("arbitrary", "arbitrary")),
        name="hgrn2",
    )(proj, proj, proj, proj, proj, lb, rec_gain)


def _outproj_kernel(a_ref, r_ref, x_ref, gate_ref, ag_ref, w_ref, lg_ref, lb_ref, o_ref, *, alpha):
    a = a_ref[...]
    an = a * lax.rsqrt(jnp.mean(a * a, axis=-1, keepdims=True) + RMS_EPS) * ag_ref[...]
    y = (jnp.dot(an.astype(BF16), w_ref[:ATTN_WIDTH, :], preferred_element_type=F32)
         + jnp.dot(r_ref[...].astype(BF16), w_ref[ATTN_WIDTH:, :], preferred_element_type=F32))
    z = alpha * x_ref[...] + (1.0 + gate_ref[...]) * y
    o_ref[...] = _layer_norm(z, lg_ref[...], lb_ref[...])


def _out_projection(attn, rec, x, gate, attn_gain, w_out_bf16, ln_g, ln_b, *, alpha, tm):
    B, S, D = x.shape
    row = lambda w: pl.BlockSpec((None, tm, w), lambda b, i: (b, i, 0))
    vec = lambda w: pl.BlockSpec((1, w), lambda b, i: (0, 0))
    return pl.pallas_call(
        functools.partial(_outproj_kernel, alpha=alpha),
        grid=(B, S // tm),
        in_specs=[row(ATTN_WIDTH), row(REC_WIDTH), row(D),
                  pl.BlockSpec((None, 1, D), lambda b, i: (b, 0, 0)),
                  vec(ATTN_WIDTH),
                  pl.BlockSpec((ATTN_WIDTH + REC_WIDTH, D), lambda b, i: (0, 0)),
                  vec(D), vec(D)],
        out_specs=row(D),
        out_shape=jax.ShapeDtypeStruct((B, S, D), F32),
        compiler_params=_params(("arbitrary", "arbitrary")),
        name="out_projection",
    )(attn, rec, x, gate, attn_gain, w_out_bf16, ln_g, ln_b)


def _ffn_kernel(x_ref, sc_ref, sh_ref, gate_ref, wg_ref, wu_ref, wd_ref, lg_ref, lb_ref, o_ref,
                *, alpha, tc):
    x = x_ref[...]
    u = (x * (1.0 + sc_ref[...]) + sh_ref[...]).astype(BF16)
    acc = jnp.zeros(x.shape, F32)
    for c in range(wg_ref.shape[1] // tc):
        cols = slice(c * tc, (c + 1) * tc)
        g = jnp.dot(u, wg_ref[:, cols], preferred_element_type=F32)
        up = jnp.dot(u, wu_ref[:, cols], preferred_element_type=F32)
        h = (g * _sigmoid(g) * up).astype(BF16)
        acc = acc + jnp.dot(h, wd_ref[cols, :], preferred_element_type=F32)
    z = alpha * x + (1.0 + gate_ref[...]) * acc
    o_ref[...] = _layer_norm(z, lg_ref[...], lb_ref[...])


def _dense_ffn(x, scale, shift, gate, wg, wu, wd, ln_g, ln_b, *, alpha, tm, tc):
    B, S, D = x.shape
    ffp = wg.shape[1]
    assert ffp % tc == 0
    row = pl.BlockSpec((None, tm, D), lambda b, i: (b, i, 0))
    mod = pl.BlockSpec((None, 1, D), lambda b, i: (b, 0, 0))
    vec = pl.BlockSpec((1, D), lambda b, i: (0, 0))
    return pl.pallas_call(
        functools.partial(_ffn_kernel, alpha=alpha, tc=tc),
        grid=(B, S // tm),
        in_specs=[row, mod, mod, mod,
                  pl.BlockSpec((D, ffp), lambda b, i: (0, 0)),
                  pl.BlockSpec((D, ffp), lambda b, i: (0, 0)),
                  pl.BlockSpec((ffp, D), lambda b, i: (0, 0)),
                  vec, vec],
        out_specs=row,
        out_shape=jax.ShapeDtypeStruct((B, S, D), F32),
        compiler_params=_params(("arbitrary", "arbitrary")),
        name="dense_ffn",
    )(x, scale, shift, gate, wg, wu, wd, ln_g, ln_b)


META_E0, META_E1, META_W0, META_W1, META_R0, META_R1 = range(6)
SUBLANES = 8
DMA_LOOP_UNROLL = 8


def _to_token_tiles(ref, val):
    n = val.shape[0]
    for k in range(SUBLANES):
        ref[pl.ds(k, n, stride=SUBLANES), :] = val[:, k * LANES:(k + 1) * LANES]


def _from_token_tiles(ref, n):
    return [ref[pl.ds(k, n, stride=SUBLANES), :] for k in range(SUBLANES)]


def _token_copy(src_hbm, token, dst, slot, sem):
    return pltpu.make_async_copy(
        src_hbm.at[pl.ds(pl.multiple_of(token * SUBLANES, SUBLANES), SUBLANES)],
        dst.at[pl.ds(pl.multiple_of(slot * SUBLANES, SUBLANES), SUBLANES)], sem)


def _router_kernel(x_ref, sc_ref, sh_ref, wr_ref, u_ref, meta_ref, cnt_ref, carry_ref, *, n_exp):
    tm = x_ref.shape[0]

    @pl.when((pl.program_id(0) == 0) & (pl.program_id(1) == 0))
    def _():
        carry_ref[...] = jnp.zeros_like(carry_ref)

    u = x_ref[...] * (1.0 + sc_ref[...]) + sh_ref[...]
    _to_token_tiles(u_ref, u)
    u_hi = u.astype(BF16)
    u_lo = (u - u_hi.astype(F32)).astype(BF16)
    logits = (jnp.dot(u_hi, wr_ref[0], preferred_element_type=F32)
              + jnp.dot(u_lo, wr_ref[0], preferred_element_type=F32)
              + jnp.dot(u_hi, wr_ref[1], preferred_element_type=F32))
    lane = lax.broadcasted_iota(I32, (tm, LANES), 1)
    neg = -jnp.inf
    l1 = jnp.where(lane < n_exp, logits, neg)
    m1 = jnp.max(l1, axis=1, keepdims=True)
    i1 = jnp.min(jnp.where(l1 == m1, lane, LANES), axis=1, keepdims=True)
    l2 = jnp.where(lane == i1, neg, l1)
    m2 = jnp.max(l2, axis=1, keepdims=True)
    i2 = jnp.min(jnp.where(l2 == m2, lane, LANES), axis=1, keepdims=True)
    e = jnp.exp(m2 - m1)
    w1 = 1.0 / (1.0 + e)
    w2 = e * w1
    sel = jnp.where((lane == i1) | (lane == i2), 1.0, 0.0)
    before = (lax.broadcasted_iota(I32, (tm, tm), 1) < lax.broadcasted_iota(I32, (tm, tm), 0))
    ranks = jnp.dot(jnp.where(before, 1.0, 0.0).astype(BF16), sel.astype(BF16),
                    preferred_element_type=F32) + carry_ref[...]
    r1 = jnp.sum(jnp.where(lane == i1, ranks, 0.0), axis=1, keepdims=True)
    r2 = jnp.sum(jnp.where(lane == i2, ranks, 0.0), axis=1, keepdims=True)
    carry_ref[...] = carry_ref[...] + jnp.sum(sel, axis=0, keepdims=True)
    cnt_ref[...] = carry_ref[...]
    meta = jnp.zeros((tm, LANES), F32)
    for k, val in ((META_E0, i1.astype(F32)), (META_E1, i2.astype(F32)), (META_W0, w1),
                   (META_W1, w2), (META_R0, r1), (META_R1, r2)):
        meta = jnp.where(lane == k, val, meta)
    meta_ref[...] = meta


def _router(x, scale, shift, w_router_lanes, *, n_exp, tm):
    B, S, D = x.shape
    assert D == SUBLANES * LANES
    nt = S // tm
    row = pl.BlockSpec((None, tm, D), lambda b, i: (b, i, 0))
    mod = pl.BlockSpec((None, 1, D), lambda b, i: (b, 0, 0))
    return pl.pallas_call(
        functools.partial(_router_kernel, n_exp=n_exp),
        grid=(B, nt),
        in_specs=[row, mod, mod, pl.BlockSpec((2, D, LANES), lambda b, i: (0, 0, 0))],
        out_specs=[pl.BlockSpec((tm * SUBLANES, LANES), lambda b, i: (b * nt + i, 0)),
                   pl.BlockSpec((None, tm, LANES), lambda b, i: (b, i, 0)),
                   pl.BlockSpec((1, LANES), lambda b, i: (0, 0))],
        out_shape=[jax.ShapeDtypeStruct((B * S * SUBLANES, LANES), F32),
                   jax.ShapeDtypeStruct((B, S, LANES), F32),
                   jax.ShapeDtypeStruct((1, LANES), F32)],
        scratch_shapes=[pltpu.VMEM((1, LANES), F32)],
        compiler_params=_params(("arbitrary", "arbitrary")),
        name="moe_router",
    )(x, scale, shift, w_router_lanes)


def _expert_kernel(te_ref, na_ref, src_cur, src_nxt, u_hbm, wg_ref, wu_ref, wd_ref, o_ref,
                   xs_buf, xb_ref, acc_ref, sems):
    i = pl.program_id(0)
    j = pl.program_id(1)
    last = pl.num_programs(1) - 1
    n_active = na_ref[0]
    active = i < n_active
    tm = xb_ref.shape[0]
    slot = i % 2

    def issue_gather(src_ref, dst_slot):
        def body(g, carry):
            for u in range(DMA_LOOP_UNROLL):
                r = g * DMA_LOOP_UNROLL + u
                _token_copy(u_hbm, src_ref[0, r], xs_buf.at[dst_slot], r,
                            sems.at[dst_slot]).start(priority=1)
            return carry
        lax.fori_loop(0, tm // DMA_LOOP_UNROLL, body, 0)

    @pl.when((i == 0) & (j == 0))
    def _():
        issue_gather(src_cur, 0)

    @pl.when(active & (j == 0))
    def _():
        def drain(r, carry):
            _token_copy(u_hbm, 0, xs_buf.at[slot], r, sems.at[slot]).wait()
            return carry
        lax.fori_loop(0, tm, drain, 0, unroll=DMA_LOOP_UNROLL)
        for k, chunk in enumerate(_from_token_tiles(xs_buf.at[slot], tm)):
            xb_ref[:, k * LANES:(k + 1) * LANES] = chunk.astype(BF16)
        acc_ref[...] = jnp.zeros_like(acc_ref)

    @pl.when((j == 1) & (i + 1 < n_active))
    def _():
        issue_gather(src_nxt, 1 - slot)

    @pl.when(active)
    def _():
        xb = xb_ref[...]
        g = jnp.dot(xb, wg_ref[...].astype(BF16), preferred_element_type=F32)
        up = jnp.dot(xb, wu_ref[...].astype(BF16), preferred_element_type=F32)
        h = (g * _sigmoid(g) * up).astype(BF16)
        acc_ref[...] += jnp.dot(h, wd_ref[...].astype(BF16), preferred_element_type=F32)

    @pl.when(active & (j == last))
    def _():
        _to_token_tiles(o_ref, acc_ref[...])

    @pl.when(jnp.logical_not(active) & (j == last))
    def _():
        o_ref[...] = jnp.zeros_like(o_ref)


def _expert_ffn(u_tiles, src_tiles, tile_expert, n_active, wg, wu, wd, *, tf):
    n_tiles, _, tm = src_tiles.shape
    D = wg.shape[1]
    ff = wg.shape[2]
    assert ff % tf == 0 and ff // tf >= 2 and tm % DMA_LOOP_UNROLL == 0
    nj = ff // tf

    def jj(i, j, na):
        return jnp.where(i < na[0], j, nj - 1)

    ids = lambda f: pl.BlockSpec((None, 1, tm), lambda i, j, te, na: (f(i), 0, 0),
                                 memory_space=pltpu.SMEM)
    grid_spec = pltpu.PrefetchScalarGridSpec(
        num_scalar_prefetch=2,
        grid=(n_tiles, nj),
        in_specs=[ids(lambda i: i), ids(lambda i: jnp.minimum(i + 1, n_tiles - 1)),
                  pl.BlockSpec(memory_space=pl.ANY),
                  pl.BlockSpec((None, D, tf), lambda i, j, te, na: (te[i], 0, jj(i, j, na))),
                  pl.BlockSpec((None, D, tf), lambda i, j, te, na: (te[i], 0, jj(i, j, na))),
                  pl.BlockSpec((None, tf, D), lambda i, j, te, na: (te[i], jj(i, j, na), 0))],
        out_specs=pl.BlockSpec((tm * SUBLANES, LANES), lambda i, j, te, na: (i, 0)),
        scratch_shapes=[pltpu.VMEM((2, tm * SUBLANES, LANES), F32),
                        pltpu.VMEM((tm, D), BF16), pltpu.VMEM((tm, D), F32),
                        pltpu.SemaphoreType.DMA((2,))],
    )
    return pl.pallas_call(
        _expert_kernel,
        grid_spec=grid_spec,
        out_shape=jax.ShapeDtypeStruct((n_tiles * tm * SUBLANES, LANES), F32),
        compiler_params=_params(("arbitrary", "arbitrary")),
        name="moe_experts",
    )(tile_expert, n_active, src_tiles, src_tiles, u_tiles, wg, wu, wd)


def _combine_kernel(d0_ref, d1_ref, ys_hbm, meta_ref, x_ref, gate_ref, lg_ref, lb_ref, o_ref,
                    buf, sem, *, alpha):
    tm = x_ref.shape[0]

    def issue(g, carry):
        for u in range(DMA_LOOP_UNROLL):
            r = g * DMA_LOOP_UNROLL + u
            _token_copy(ys_hbm, d0_ref[0, r], buf.at[0], r, sem).start(priority=0)
            _token_copy(ys_hbm, d1_ref[0, r], buf.at[1], r, sem).start(priority=1)
        return carry

    def drain(r, carry):
        _token_copy(ys_hbm, 0, buf.at[0], r, sem).wait()
        _token_copy(ys_hbm, 0, buf.at[1], r, sem).wait()
        return carry

    lax.fori_loop(0, tm // DMA_LOOP_UNROLL, issue, 0)
    lax.fori_loop(0, tm, drain, 0, unroll=DMA_LOOP_UNROLL)
    meta = meta_ref[...]
    w0 = meta[:, META_W0:META_W0 + 1]
    w1 = meta[:, META_W1:META_W1 + 1]
    y = jnp.concatenate([w0 * a + w1 * b for a, b in zip(_from_token_tiles(buf.at[0], tm),
                                                         _from_token_tiles(buf.at[1], tm))], axis=1)
    z = alpha * x_ref[...] + (1.0 + gate_ref[...]) * y
    o_ref[...] = _layer_norm(z, lg_ref[...], lb_ref[...])


def _combine(ys, d0_tiles, d1_tiles, meta, x, gate, ln_g, ln_b, *, alpha, tm):
    B, S, D = x.shape
    nt = S // tm
    ids = pl.BlockSpec((None, 1, tm), lambda b, i: (b * nt + i, 0, 0), memory_space=pltpu.SMEM)
    row = pl.BlockSpec((None, tm, D), lambda b, i: (b, i, 0))
    vec = pl.BlockSpec((1, D), lambda b, i: (0, 0))
    return pl.pallas_call(
        functools.partial(_combine_kernel, alpha=alpha),
        grid=(B, nt),
        in_specs=[ids, ids, pl.BlockSpec(memory_space=pl.ANY),
                  pl.BlockSpec((None, tm, LANES), lambda b, i: (b, i, 0)),
                  row, pl.BlockSpec((None, 1, D), lambda b, i: (b, 0, 0)), vec, vec],
        out_specs=row,
        out_shape=jax.ShapeDtypeStruct((B, S, D), F32),
        scratch_shapes=[pltpu.VMEM((2, tm * SUBLANES, LANES), F32), pltpu.SemaphoreType.DMA(())],
        compiler_params=_params(("arbitrary", "arbitrary")),
        name="moe_combine",
    )(d0_tiles, d1_tiles, ys, meta, x, gate, ln_g, ln_b)


def _moe_ffn(x, scale, shift, gate, w_router, wg, wu, wd, ln_g, ln_b, *, alpha, tm, tm_e, tf):
    B, S, D = x.shape
    T = B * S
    n_exp = w_router.shape[1]
    wr = jnp.zeros((D, LANES), F32).at[:, :n_exp].set(w_router)
    wr_hi = wr.astype(BF16)
    wr = jnp.stack([wr_hi, (wr - wr_hi.astype(F32)).astype(BF16)])
    u_tiles, meta, counts = _router(x, scale, shift, wr, n_exp=n_exp, tm=tm)

    meta2 = meta.reshape(T, LANES)
    e0 = meta2[:, META_E0].astype(I32)
    e1 = meta2[:, META_E1].astype(I32)
    r0 = meta2[:, META_R0].astype(I32)
    r1 = meta2[:, META_R1].astype(I32)
    cnt = counts[0, :n_exp].astype(I32)
    sizes = ((cnt + tm_e - 1) // tm_e) * tm_e
    ends = jnp.cumsum(sizes)
    starts = ends - sizes
    d0 = starts[e0] + r0
    d1 = starts[e1] + r1
    n_tiles = (TOP_K * T) // tm_e + n_exp
    tok = jnp.arange(T, dtype=I32)
    src = jnp.zeros((n_tiles * tm_e,), I32).at[jnp.concatenate([d0, d1])].set(
        jnp.concatenate([tok, tok]), unique_indices=True)
    n_active = (ends[-1] // tm_e).astype(I32)
    tile_id = jnp.minimum(jnp.arange(n_tiles, dtype=I32), n_active - 1)
    tile_expert = jnp.sum((ends[None, :] <= (tile_id * tm_e)[:, None]).astype(I32), axis=1)

    ys = _expert_ffn(u_tiles, src.reshape(n_tiles, 1, tm_e), tile_expert, n_active.reshape(1),
                     wg, wu, wd, tf=tf)
    nt = T // tm
    return _combine(ys, d0.reshape(nt, 1, tm), d1.reshape(nt, 1, tm), meta, x, gate, ln_g, ln_b,
                    alpha=alpha, tm=tm)


def kernel(x, c, positions, w_in, w_out, attn_norm_gain, rec_norm_gain, rec_lb_logits, ada_w, ada_b,
           ln_gain, ln_bias, ffn_w_gate, ffn_w_up, ffn_w_down, moe_router, moe_w_gate, moe_w_up,
           moe_w_down):
    B, S, D = x.shape
    depth = w_in.shape[0]
    alpha = (2 * depth) ** 0.25

    p = jax.nn.softmax(rec_lb_logits.astype(F32), axis=0)
    cum = jnp.cumsum(p, axis=0)
    lb_all = cum - cum[0:1]
    half = ATTN_HEAD_DIM // 2
    inv_freq = ROPE_THETA ** (-jnp.arange(half, dtype=F32) / half)
    inv_freq_lanes = jnp.tile(inv_freq, LANES // half).reshape(1, LANES)
    pos_f = positions.astype(F32).reshape(B, S, 1)
    w_in_b = w_in.astype(BF16)
    w_out_b = w_out.astype(BF16)
    ff = ffn_w_gate.shape[2]
    ffp = -(-ff // (2 * LANES)) * (2 * LANES)
    pad_c = lambda w: jnp.pad(w.astype(BF16), ((0, 0), (0, 0), (0, ffp - ff)))
    ffn_g, ffn_u = pad_c(ffn_w_gate), pad_c(ffn_w_up)
    ffn_d = jnp.pad(ffn_w_down.astype(BF16), ((0, 0), (0, ffp - ff), (0, 0)))

    mods = _ada_modulation(c, ada_w, ada_b)
    rope_cos, rope_sin = _rope_tables(pos_f, inv_freq_lanes, tm=min(1024, S))

    def mod(layer, sub):
        m = mods[layer * 2 + sub].reshape(B, 1, 3 * D)
        return m[..., :D], m[..., D:2 * D], m[..., 2 * D:]

    vec = lambda a: a.reshape(1, -1)
    for layer in range(depth):
        shift, scale, gate = mod(layer, 0)
        proj = _in_projection(x, scale, shift, rope_cos, rope_sin, w_in_b[layer], tm=min(1024, S))
        attn = _dilated_attention(proj)
        rec = _hgrn2(proj, lb_all[layer], vec(rec_norm_gain[layer]))
        x = _out_projection(attn, rec, x, gate, vec(attn_norm_gain[layer]), w_out_b[layer],
                            vec(ln_gain[layer, 0]), vec(ln_bias[layer, 0]), alpha=alpha, tm=min(512, S))
        shift, scale, gate = mod(layer, 1)
        j = layer // 2
        if layer % 2 == 0:
            x = _dense_ffn(x, scale, shift, gate, ffn_g[j], ffn_u[j], ffn_d[j],
                           vec(ln_gain[layer, 1]), vec(ln_bias[layer, 1]),
                           alpha=alpha, tm=min(512, S), tc=2 * LANES)
        else:
            x = _moe_ffn(x, scale, shift, gate, moe_router[j], moe_w_gate[j], moe_w_up[j],
                         moe_w_down[j], vec(ln_gain[layer, 1]), vec(ln_bias[layer, 1]),
                         alpha=alpha, tm=min(512, S), tm_e=min(1024, B * S), tf=512)
    return x
```

```python
import functools

import jax
import jax.numpy as jnp
from jax import lax
from jax.experimental import pallas as pl
from jax.experimental.pallas import tpu as pltpu

F32 = jnp.float32
BF16 = jnp.bfloat16
I32 = jnp.int32

LANES = 128
V7X_VMEM_LIMIT_BYTES = 56 * 1024 * 1024

ATTN_WIDTH = 512
ATTN_HEAD_DIM = 64
DILATIONS = (1, 4, 16)
ATTN_RADIUS = 64
ROPE_THETA = 10000.0
REC_WIDTH = 512
REC_DIM = 128
REC_CHUNK = 64
TOP_K = 2
LN_EPS = 1e-5
RMS_EPS = 1e-6
MASK_VALUE = -1e30
DECAY_EXP_CLAMP = 80.0


def _params(semantics):
    return pltpu.CompilerParams(dimension_semantics=semantics,
                                vmem_limit_bytes=V7X_VMEM_LIMIT_BYTES)


def _sigmoid(x):
    return 1.0 / (1.0 + jnp.exp(-x))


def _layer_norm(z, gain, bias):
    mu = jnp.mean(z, axis=-1, keepdims=True)
    zc = z - mu
    var = jnp.mean(zc * zc, axis=-1, keepdims=True)
    return zc * lax.rsqrt(var + LN_EPS) * gain + bias


def _ada_kernel(c_ref, w_ref, b_ref, o_ref):
    c = c_ref[...]
    o_ref[...] = jnp.dot(c * _sigmoid(c), w_ref[...], precision=lax.Precision.HIGHEST,
                         preferred_element_type=F32) + b_ref[...]


def _ada_modulation(c, ada_w, ada_b):
    B, D = c.shape
    n = ada_w.shape[0] * ada_w.shape[1]
    n3 = ada_w.shape[-1]
    tn = 1024
    return pl.pallas_call(
        _ada_kernel,
        grid=(n, n3 // tn),
        in_specs=[pl.BlockSpec((B, D), lambda i, j: (0, 0)),
                  pl.BlockSpec((None, D, tn), lambda i, j: (i, 0, j)),
                  pl.BlockSpec((None, 1, tn), lambda i, j: (i, 0, j))],
        out_specs=pl.BlockSpec((None, B, tn), lambda i, j: (i, 0, j)),
        out_shape=jax.ShapeDtypeStruct((n, B, n3), F32),
        compiler_params=_params(("arbitrary", "arbitrary")),
        name="ada_modulation",
    )(c, ada_w.reshape(n, D, n3), ada_b.reshape(n, 1, n3))


def _rope_kernel(pos_ref, invf_ref, cos_ref, sin_ref):
    ang = pos_ref[...] * invf_ref[...]
    lane = lax.broadcasted_iota(I32, (1, LANES), 1)
    first_half = (lane % ATTN_HEAD_DIM) < (ATTN_HEAD_DIM // 2)
    sin = jnp.sin(ang)
    cos_ref[...] = jnp.cos(ang)
    sin_ref[...] = jnp.where(first_half, -sin, sin)


def _rope_tables(pos_f, inv_freq_lanes, *, tm):
    B, S, _ = pos_f.shape
    tab = pl.BlockSpec((None, tm, LANES), lambda b, i: (b, i, 0))
    return pl.pallas_call(
        _rope_kernel,
        grid=(B, S // tm),
        in_specs=[pl.BlockSpec((None, tm, 1), lambda b, i: (b, i, 0)),
                  pl.BlockSpec((1, LANES), lambda b, i: (0, 0))],
        out_specs=[tab, tab],
        out_shape=[jax.ShapeDtypeStruct((B, S, LANES), F32)] * 2,
        compiler_params=_params(("arbitrary", "arbitrary")),
        name="rope_tables",
    )(pos_f, inv_freq_lanes)


def _inproj_kernel(x_ref, sc_ref, sh_ref, cos_ref, sin_ref, w_ref, o_ref, u_ref, *, tn):
    j = pl.program_id(2)

    @pl.when(j == 0)
    def _():
        u_ref[...] = (x_ref[...] * (1.0 + sc_ref[...]) + sh_ref[...]).astype(BF16)

    cols = pl.ds(pl.multiple_of(j * tn, tn), tn)
    acc = jnp.dot(u_ref[...], w_ref[:, cols], preferred_element_type=F32)

    @pl.when(j > 0)
    def _():
        o_ref[...] = acc

    @pl.when(j == 0)
    def _():
        cos = cos_ref[...]
        sin_signed = sin_ref[...]
        lane = lax.broadcasted_iota(I32, (1, LANES), 1)
        first_half = (lane % ATTN_HEAD_DIM) < (ATTN_HEAD_DIM // 2)
        q_scale = ATTN_HEAD_DIM ** -0.5
        for kk in range(tn // LANES):
            c = acc[:, kk * LANES:(kk + 1) * LANES]
            rot = jnp.where(first_half, pltpu.roll(c, LANES - 32, 1), pltpu.roll(c, 32, 1))
            r = c * cos + rot * sin_signed
            if kk * LANES < ATTN_WIDTH:
                r = r * q_scale
            o_ref[:, kk * LANES:(kk + 1) * LANES] = r


def _in_projection(x, scale, shift, rope_cos, rope_sin, w_in_bf16, *, tm):
    B, S, D = x.shape
    n_cols = w_in_bf16.shape[1]
    tn = 2 * ATTN_WIDTH
    assert S % tm == 0 and n_cols % tn == 0
    return pl.pallas_call(
        functools.partial(_inproj_kernel, tn=tn),
        grid=(B, S // tm, n_cols // tn),
        in_specs=[pl.BlockSpec((None, tm, D), lambda b, i, j: (b, i, 0)),
                  pl.BlockSpec((None, 1, D), lambda b, i, j: (b, 0, 0)),
                  pl.BlockSpec((None, 1, D), lambda b, i, j: (b, 0, 0)),
                  pl.BlockSpec((None, tm, LANES), lambda b, i, j: (b, i, 0)),
                  pl.BlockSpec((None, tm, LANES), lambda b, i, j: (b, i, 0)),
                  pl.BlockSpec((D, n_cols), lambda b, i, j: (0, 0))],
        out_specs=pl.BlockSpec((None, tm, tn), lambda b, i, j: (b, i, j)),
        out_shape=jax.ShapeDtypeStruct((B, S, n_cols), F32),
        scratch_shapes=[pltpu.VMEM((tm, D), BF16)],
        compiler_params=_params(("arbitrary", "arbitrary", "arbitrary")),
        name="in_projection",
    )(x, scale, shift, rope_cos, rope_sin, w_in_bf16)


ATTN_BLOCK_GROUP = 8


def _attn_kernel(q_ref, k_ref, v_ref, o_ref, qs, ks, vs, qf, kf, vf, pm, plr, pa, nm, nl, na, bias_s,
                 *, S):
    lane = lax.broadcasted_iota(I32, (1, LANES), 1)
    head0 = lane < ATTN_HEAD_DIM

    for p, d in enumerate(DILATIONS):
        L = S // d
        tq = min(128, L)
        W = min(2 * tq, L)
        nb = L // tq

        d_prev = DILATIONS[p - 1] if p else 1
        ratio, l_prev = d // d_prev, S // d_prev
        keep_f32 = 0 < p < len(DILATIONS) - 1
        for src, stage, dst in ((q_ref, qf, qs), (k_ref, kf, ks), (v_ref, vf, vs)):
            src = src if p <= 1 else stage
            for r in range(d):
                a, r_prev = divmod(r, d_prev)
                rows = pl.ds(r_prev * l_prev + a, L, stride=ratio) if ratio > 1 else pl.ds(0, L)
                val = src[rows, :]
                if keep_f32:
                    stage[r * L:(r + 1) * L, :] = val
                dst[r * L:(r + 1) * L, :] = val.astype(BF16)

        n_blocks = d * nb
        rc = lax.broadcasted_iota(I32, (tq, W), 0) - lax.broadcasted_iota(I32, (tq, W), 1)
        for which, delta in enumerate((0, ATTN_RADIUS, tq)):
            bias_s[which, pl.ds(0, tq), pl.ds(0, W)] = jnp.where(
                jnp.abs(rc + delta) <= ATTN_RADIUS, 0.0, MASK_VALUE)
        group = min(ATTN_BLOCK_GROUP, n_blocks)
        assert n_blocks % group == 0

        def block_group(gi, carry, L=L, tq=tq, W=W, nb=nb, group=group):
            q0s, k0s, scores, probs, outs, sums = [], [], [], [], [], []
            for u in range(group):
                g = gi * group + u
                r = g // nb
                n = g - r * nb
                base = r * L
                q0 = pl.multiple_of(base + n * tq, tq)
                ws = jnp.clip(n * tq - ATTN_RADIUS, 0, L - W)
                k0 = pl.multiple_of(base + ws, 16)
                qb = qs[pl.ds(q0, tq), :]
                zero = jnp.zeros_like(qb)
                q2 = jnp.concatenate([jnp.where(head0, qb, zero), jnp.where(head0, zero, qb)], axis=0)
                s = lax.dot_general(q2, ks[pl.ds(k0, W), :], (((1,), (1,)), ((), ())),
                                    preferred_element_type=F32)
                delta = n * tq - ws
                which = jnp.where(delta == 0, 0, jnp.where(delta == ATTN_RADIUS, 1, 2))
                bias = bias_s[which, pl.ds(0, tq), pl.ds(0, W)]
                scores.append(s + jnp.concatenate([bias, bias], axis=0))
                q0s.append(q0)
                k0s.append(k0)
            for u in range(group):
                s = scores[u]
                m = jnp.max(s, axis=1, keepdims=True)
                e = jnp.exp(s - m)
                probs.append((m, e.astype(BF16)))
                sums.append(jnp.sum(e, axis=1, keepdims=True))
            for u in range(group):
                outs.append(jnp.dot(probs[u][1], vs[pl.ds(k0s[u], W), :], preferred_element_type=F32))
            for u in range(group):
                m, l = probs[u][0], sums[u]
                rows = pl.ds(q0s[u], tq)
                pm[rows, :] = jnp.where(head0, m[:tq], m[tq:])
                plr[rows, :] = jnp.where(head0, l[:tq], l[tq:])
                pa[rows, :] = jnp.where(head0, outs[u][:tq], outs[u][tq:])
            return carry

        lax.fori_loop(0, n_blocks // group, block_group, 0)

        for r in range(d):
            rows = pl.ds(r, L, stride=d) if d > 1 else pl.ds(0, L)
            nm[p, rows, :] = pm[r * L:(r + 1) * L, :]
            nl[p, rows, :] = plr[r * L:(r + 1) * L, :]
            na[p, rows, :] = pa[r * L:(r + 1) * L, :]

    m_all = jnp.maximum(jnp.maximum(nm[0], nm[1]), nm[2])
    num = jnp.zeros((S, LANES), F32)
    den = jnp.zeros((S, LANES), F32)
    for p in range(len(DILATIONS)):
        w = jnp.exp(nm[p] - m_all)
        num = num + w * na[p]
        den = den + w * nl[p]
    o_ref[...] = num / den


def _dilated_attention(proj):
    B, S, _ = proj.shape
    n_pairs = ATTN_WIDTH // LANES
    assert S % (16 * DILATIONS[-1]) == 0
    blk = lambda off: pl.BlockSpec((None, S, LANES), lambda b, h: (b, 0, off + h))
    return pl.pallas_call(
        functools.partial(_attn_kernel, S=S),
        grid=(B, n_pairs),
        in_specs=[blk(0), blk(n_pairs), blk(2 * n_pairs)],
        out_specs=pl.BlockSpec((None, S, LANES), lambda b, h: (b, 0, h)),
        out_shape=jax.ShapeDtypeStruct((B, S, ATTN_WIDTH), F32),
        scratch_shapes=[pltpu.VMEM((S, LANES), BF16)] * 3
                       + [pltpu.VMEM((S, LANES), F32)] * 6
                       + [pltpu.VMEM((len(DILATIONS), S, LANES), F32)] * 3
                       + [pltpu.VMEM((3, min(128, S), min(256, S)), F32)],
        compiler_params=_params(("arbitrary", "arbitrary")),
        name="dilated_attention",
    )(proj, proj, proj)


REC_GROUP_CHUNKS = 4
REC_GROUP_ROWS = REC_GROUP_CHUNKS * REC_CHUNK
REC_STAGE_GROUPS = 4


def _hgrn_kernel(rq_ref, zf_ref, zb_ref, ri_ref, rg_ref, lb_ref, gain_ref, o_ref,
                 q_s, vt_s, oi_s, qp_s, ut_s, dd_s, st_s, os_s, *, S):
    C, G, GR = REC_CHUNK, REC_GROUP_CHUNKS, REC_GROUP_ROWS
    nc, ng = S // C, S // GR
    sg = min(REC_STAGE_GROUPS, ng)
    assert ng % sg == 0

    row = lax.broadcasted_iota(I32, (GR, GR), 0)
    col = lax.broadcasted_iota(I32, (GR, GR), 1)
    same_chunk = (row // C) == (col // C)
    t_i = lax.broadcasted_iota(I32, (C, C), 0)
    s_i = lax.broadcasted_iota(I32, (C, C), 1)

    def prep(i, carry):
        r0 = pl.multiple_of(i * GR, GR)
        rq = rq_ref[pl.ds(r0, GR), :]
        q_s[pl.ds(r0, GR), :] = rq * _sigmoid(rq)
        v = ri_ref[pl.ds(r0, GR), :]
        for c in range(G):
            vt_s[i * G + c] = v[c * C:(c + 1) * C, :].T.astype(BF16)
        return carry

    lax.fori_loop(0, ng, prep, 0)

    for direction in range(2):
        fwd = direction == 0
        z_ref = zf_ref if fwd else zb_ref
        cum = jnp.where(same_chunk & ((col <= row) if fwd else (col >= row)), 1.0, 0.0).astype(BF16)
        keep = (s_i <= t_i) if fwd else (s_i >= t_i)
        mid_row = C // 2 - 1 if fwd else C // 2
        last_row = C - 1 if fwd else 0

        def phase_a(it, carry, fwd=fwd, z_ref=z_ref, cum=cum, keep=keep,
                    mid_row=mid_row, last_row=last_row, direction=direction):
            lb = lb_ref[direction:direction + 1, :]
            log_lb = jnp.log(lb)
            log_1m_lb = jnp.log1p(-lb)
            r0s, c0s, k3s, bcs, atts, stage3 = [], [], [], [], [], []
            for u in range(sg):
                i = it * sg + u
                r0 = pl.multiple_of(i * GR, GR)
                z = z_ref[pl.ds(r0, GR), :]
                e = jnp.exp(-jnp.abs(z))
                t = log_1m_lb + (jnp.minimum(z, 0.0) - jnp.log(1.0 + e))
                g = jnp.maximum(log_lb, t) + jnp.log(1.0 + jnp.exp(-jnp.abs(log_lb - t)))
                kk = (1.0 - lb) * (jnp.where(z > 0, e, 1.0) / (1.0 + e))
                g1 = g.astype(BF16)
                r1 = g - g1.astype(F32)
                g2 = r1.astype(BF16)
                g3 = (r1 - g2.astype(F32)).astype(BF16)
                bcs.append(jnp.dot(cum, g1, preferred_element_type=F32)
                           + jnp.dot(cum, g2, preferred_element_type=F32)
                           + jnp.dot(cum, g3, preferred_element_type=F32))
                k3s.append(kk.reshape(G, C, REC_DIM))
                r0s.append(r0)
                c0s.append(pl.multiple_of(i * G, G))
            for u in range(sg):
                b3 = bcs[u].reshape(G, C, REC_DIM)
                mid = b3[:, mid_row:mid_row + 1, :]
                last = b3[:, last_row:last_row + 1, :]
                e3 = b3 - mid
                q3 = q_s[pl.ds(r0s[u], GR), :].reshape(G, C, REC_DIM)
                qt = (q3 * jnp.exp(jnp.minimum(e3, DECAY_EXP_CLAMP))).astype(BF16)
                kt = (k3s[u] * jnp.exp(jnp.minimum(-e3, DECAY_EXP_CLAMP))).astype(BF16)
                atts.append(jnp.einsum('gtc,gsc->gts', qt, kt, preferred_element_type=F32))
                qp_s[pl.ds(r0s[u], GR), :] = (q3 * jnp.exp(b3)).astype(BF16).reshape(GR, REC_DIM)
                stage3.append(((k3s[u] * jnp.exp(last - b3)).astype(BF16), jnp.exp(last)))
            for u in range(sg):
                att = jnp.where(keep[None], atts[u], 0.0).astype(BF16)
                v3 = ri_ref[pl.ds(r0s[u], GR), :].reshape(G, C, REC_DIM).astype(BF16)
                oi = jnp.einsum('gts,gsv->gtv', att, v3, preferred_element_type=F32)
                oi_s[pl.ds(r0s[u], GR), :] = oi.reshape(GR, REC_DIM)
                kp, decay = stage3[u]
                ut_s[pl.ds(c0s[u], G)] = jnp.einsum('gvs,gsc->gvc', vt_s[pl.ds(c0s[u], G)], kp,
                                                    preferred_element_type=F32)
                dd_s[pl.ds(c0s[u], G)] = decay
            return carry

        lax.fori_loop(0, ng // sg, phase_a, 0)

        def scan(c, st, fwd=fwd):
            idx = c if fwd else nc - 1 - c
            st_s[idx] = st.astype(BF16)
            return dd_s[idx] * st + ut_s[idx]

        lax.fori_loop(0, nc, scan, jnp.zeros((REC_DIM, REC_DIM), F32))

        def phase_c(i, carry, fwd=fwd):
            r0 = pl.multiple_of(i * GR, GR)
            c0 = pl.multiple_of(i * G, G)
            qp3 = qp_s[pl.ds(r0, GR), :].reshape(G, C, REC_DIM)
            oo = jnp.einsum('gtc,gvc->gtv', qp3, st_s[pl.ds(c0, G)], preferred_element_type=F32)
            tot = oi_s[pl.ds(r0, GR), :] + oo.reshape(GR, REC_DIM)
            if fwd:
                os_s[pl.ds(r0, GR), :] = tot
            else:
                os_s[pl.ds(r0, GR), :] = os_s[pl.ds(r0, GR), :] + tot
            return carry

        lax.fori_loop(0, ng, phase_c, 0, unroll=2)

    o = os_s[...]
    o = o * lax.rsqrt(jnp.mean(o * o, axis=-1, keepdims=True) + RMS_EPS) * gain_ref[...]
    o_ref[...] = o * _sigmoid(rg_ref[...])


def _hgrn2(proj, lb, rec_gain):
    B, S, _ = proj.shape
    nh = REC_WIDTH // REC_DIM
    assert S % REC_GROUP_ROWS == 0
    col0 = 3 * ATTN_WIDTH // LANES
    blk = lambda k: pl.BlockSpec((None, S, REC_DIM), lambda b, h: (b, 0, col0 + k * nh + h))
    nc = S // REC_CHUNK
    return pl.pallas_call(
        functools.partial(_hgrn_kernel, S=S),
        grid=(B, nh),
        in_specs=[blk(0), blk(1), blk(2), blk(3), blk(4),
                  pl.BlockSpec((2, REC_DIM), lambda b, h: (0, h)),
                  pl.BlockSpec((1, REC_DIM), lambda b, h: (0, h))],
        out_specs=pl.BlockSpec((None, S, REC_DIM), lambda b, h: (b, 0, h)),
        out_shape=jax.ShapeDtypeStruct((B, S, REC_WIDTH), F32),
        scratch_shapes=[pltpu.VMEM((S, REC_DIM), F32),
                        pltpu.VMEM((nc, REC_DIM, REC_CHUNK), BF16),
                        pltpu.VMEM((S, REC_DIM), F32),
                        pltpu.VMEM((S, REC_DIM), BF16),
                        pltpu.VMEM((nc, REC_DIM, REC_DIM), F32),
                        pltpu.VMEM((nc, 1, REC_DIM), F32),
                        pltpu.VMEM((nc, REC_DIM, REC_DIM), BF16),
                        pltpu.VMEM((S, REC_DIM), F32)],
        compiler_params=_params(("arbitrary", "arbitrary")),
        name="hgrn2",
    )(proj, proj, proj, proj, proj, lb, rec_gain)


def _outproj_kernel(a_ref, r_ref, x_ref, gate_ref, ag_ref, w_ref, lg_ref, lb_ref, o_ref, *, alpha):
    a = a_ref[...]
    an = a * lax.rsqrt(jnp.mean(a * a, axis=-1, keepdims=True) + RMS_EPS) * ag_ref[...]
    y = (jnp.dot(an.astype(BF16), w_ref[:ATTN_WIDTH, :], preferred_element_type=F32)
         + jnp.dot(r_ref[...].astype(BF16), w_ref[ATTN_WIDTH:, :], preferred_element_type=F32))
    z = alpha * x_ref[...] + (1.0 + gate_ref[...]) * y
    o_ref[...] = _layer_norm(z, lg_ref[...], lb_ref[...])


def _out_projection(attn, rec, x, gate, attn_gain, w_out_bf16, ln_g, ln_b, *, alpha, tm):
    B, S, D = x.shape
    row = lambda w: pl.BlockSpec((None, tm, w), lambda b, i: (b, i, 0))
    vec = lambda w: pl.BlockSpec((1, w), lambda b, i: (0, 0))
    return pl.pallas_call(
        functools.partial(_outproj_kernel, alpha=alpha),
        grid=(B, S // tm),
        in_specs=[row(ATTN_WIDTH), row(REC_WIDTH), row(D),
                  pl.BlockSpec((None, 1, D), lambda b, i: (b, 0, 0)),
                  vec(ATTN_WIDTH),
                  pl.BlockSpec((ATTN_WIDTH + REC_WIDTH, D), lambda b, i: (0, 0)),
                  vec(D), vec(D)],
        out_specs=row(D),
        out_shape=jax.ShapeDtypeStruct((B, S, D), F32),
        compiler_params=_params(("arbitrary", "arbitrary")),
        name="out_projection",
    )(attn, rec, x, gate, attn_gain, w_out_bf16, ln_g, ln_b)


def _ffn_kernel(x_ref, sc_ref, sh_ref, gate_ref, wg_ref, wu_ref, wd_ref, lg_ref, lb_ref, o_ref,
                *, alpha, tc):
    x = x_ref[...]
    u = (x * (1.0 + sc_ref[...]) + sh_ref[...]).astype(BF16)
    acc = jnp.zeros(x.shape, F32)
    for c in range(wg_ref.shape[1] // tc):
        cols = slice(c * tc, (c + 1) * tc)
        g = jnp.dot(u, wg_ref[:, cols], preferred_element_type=F32)
        up = jnp.dot(u, wu_ref[:, cols], preferred_element_type=F32)
        h = (g * _sigmoid(g) * up).astype(BF16)
        acc = acc + jnp.dot(h, wd_ref[cols, :], preferred_element_type=F32)
    z = alpha * x + (1.0 + gate_ref[...]) * acc
    o_ref[...] = _layer_norm(z, lg_ref[...], lb_ref[...])


def _dense_ffn(x, scale, shift, gate, wg, wu, wd, ln_g, ln_b, *, alpha, tm, tc):
    B, S, D = x.shape
    ffp = wg.shape[1]
    assert ffp % tc == 0
    row = pl.BlockSpec((None, tm, D), lambda b, i: (b, i, 0))
    mod = pl.BlockSpec((None, 1, D), lambda b, i: (b, 0, 0))
    vec = pl.BlockSpec((1, D), lambda b, i: (0, 0))
    return pl.pallas_call(
        functools.partial(_ffn_kernel, alpha=alpha, tc=tc),
        grid=(B, S // tm),
        in_specs=[row, mod, mod, mod,
                  pl.BlockSpec((D, ffp), lambda b, i: (0, 0)),
                  pl.BlockSpec((D, ffp), lambda b, i: (0, 0)),
                  pl.BlockSpec((ffp, D), lambda b, i: (0, 0)),
                  vec, vec],
        out_specs=row,
        out_shape=jax.ShapeDtypeStruct((B, S, D), F32),
        compiler_params=_params(("arbitrary", "arbitrary")),
        name="dense_ffn",
    )(x, scale, shift, gate, wg, wu, wd, ln_g, ln_b)


META_E0, META_E1, META_W0, META_W1, META_R0, META_R1 = range(6)
SUBLANES = 8
DMA_LOOP_UNROLL = 8


def _to_token_tiles(ref, val):
    n = val.shape[0]
    for k in range(SUBLANES):
        ref[pl.ds(k, n, stride=SUBLANES), :] = val[:, k * LANES:(k + 1) * LANES]


def _from_token_tiles(ref, n):
    return [ref[pl.ds(k, n, stride=SUBLANES), :] for k in range(SUBLANES)]


def _token_copy(src_hbm, token, dst, slot, sem):
    return pltpu.make_async_copy(
        src_hbm.at[pl.ds(pl.multiple_of(token * SUBLANES, SUBLANES), SUBLANES)],
        dst.at[pl.ds(pl.multiple_of(slot * SUBLANES, SUBLANES), SUBLANES)], sem)


def _router_kernel(x_ref, sc_ref, sh_ref, wr_ref, u_ref, meta_ref, cnt_ref, carry_ref, *, n_exp):
    tm = x_ref.shape[0]

    @pl.when((pl.program_id(0) == 0) & (pl.program_id(1) == 0))
    def _():
        carry_ref[...] = jnp.zeros_like(carry_ref)

    u = x_ref[...] * (1.0 + sc_ref[...]) + sh_ref[...]
    u_ref[...] = u.astype(BF16)
    u_hi = u.astype(BF16)
    u_lo = (u - u_hi.astype(F32)).astype(BF16)
    logits = (jnp.dot(u_hi, wr_ref[0], preferred_element_type=F32)
              + jnp.dot(u_lo, wr_ref[0], preferred_element_type=F32)
              + jnp.dot(u_hi, wr_ref[1], preferred_element_type=F32))
    lane = lax.broadcasted_iota(I32, (tm, LANES), 1)
    neg = -jnp.inf
    l1 = jnp.where(lane < n_exp, logits, neg)
    m1 = jnp.max(l1, axis=1, keepdims=True)
    i1 = jnp.min(jnp.where(l1 == m1, lane, LANES), axis=1, keepdims=True)
    l2 = jnp.where(lane == i1, neg, l1)
    m2 = jnp.max(l2, axis=1, keepdims=True)
    i2 = jnp.min(jnp.where(l2 == m2, lane, LANES), axis=1, keepdims=True)
    e = jnp.exp(m2 - m1)
    w1 = 1.0 / (1.0 + e)
    w2 = e * w1
    sel = jnp.where((lane == i1) | (lane == i2), 1.0, 0.0)
    before = (lax.broadcasted_iota(I32, (tm, tm), 1) < lax.broadcasted_iota(I32, (tm, tm), 0))
    ranks = jnp.dot(jnp.where(before, 1.0, 0.0).astype(BF16), sel.astype(BF16),
                    preferred_element_type=F32) + carry_ref[...]
    r1 = jnp.sum(jnp.where(lane == i1, ranks, 0.0), axis=1, keepdims=True)
    r2 = jnp.sum(jnp.where(lane == i2, ranks, 0.0), axis=1, keepdims=True)
    carry_ref[...] = carry_ref[...] + jnp.sum(sel, axis=0, keepdims=True)
    cnt_ref[...] = carry_ref[...]
    meta = jnp.zeros((tm, LANES), F32)
    for k, val in ((META_E0, i1.astype(F32)), (META_E1, i2.astype(F32)), (META_W0, w1),
                   (META_W1, w2), (META_R0, r1), (META_R1, r2)):
        meta = jnp.where(lane == k, val, meta)
    meta_ref[...] = meta


def _router(x, scale, shift, w_router_lanes, *, n_exp, tm):
    B, S, D = x.shape
    assert D == SUBLANES * LANES
    nt = S // tm
    row = pl.BlockSpec((None, tm, D), lambda b, i: (b, i, 0))
    mod = pl.BlockSpec((None, 1, D), lambda b, i: (b, 0, 0))
    return pl.pallas_call(
        functools.partial(_router_kernel, n_exp=n_exp),
        grid=(B, nt),
        in_specs=[row, mod, mod, pl.BlockSpec((2, D, LANES), lambda b, i: (0, 0, 0))],
        out_specs=[row,
                   pl.BlockSpec((None, tm, LANES), lambda b, i: (b, i, 0)),
                   pl.BlockSpec((1, LANES), lambda b, i: (0, 0))],
        out_shape=[jax.ShapeDtypeStruct((B, S, D), BF16),
                   jax.ShapeDtypeStruct((B, S, LANES), F32),
                   jax.ShapeDtypeStruct((1, LANES), F32)],
        scratch_shapes=[pltpu.VMEM((1, LANES), F32)],
        compiler_params=_params(("arbitrary", "arbitrary")),
        name="moe_router",
    )(x, scale, shift, w_router_lanes)


DISPATCH_ROWS = 128


def _dispatch_kernel(base_ref, cnt_ref, padlo_ref, padn_ref, u_ref, meta_ref, xs_hbm,
                     stage, zeros_buf, sem, *, n_exp):
    t = pl.program_id(0)
    tm = u_ref.shape[0]
    R = DISPATCH_ROWS
    nblk = tm // R
    u = u_ref[...]
    e0, e1 = meta_ref[0:1, :], meta_ref[1:2, :]
    d0, d1 = meta_ref[2:3, :], meta_ref[3:4, :]
    srow = lax.broadcasted_iota(I32, (R, tm), 0)

    def slots_copy(src, first_slot, n_slots):
        dst = xs_hbm.at[pl.ds(pl.multiple_of(first_slot * SUBLANES, SUBLANES), n_slots * SUBLANES)]
        return pltpu.make_async_copy(src, dst, sem)

    def per_block(fn):
        for e in range(n_exp):
            base = base_ref[t * n_exp + e]
            n = cnt_ref[t * n_exp + e]
            for blk in range(nblk):
                call = functools.partial(fn, e, blk, base)
                if blk == 0:
                    call()
                else:
                    pl.when(n > blk * R)(call)

    def emit(e, blk, base):
        rel = jnp.where(e0 == e, d0, jnp.where(e1 == e, d1, -1))
        rel = jnp.where(rel >= 0, rel - base, -1)
        onehot = jnp.where(srow + blk * R == rel, 1.0, 0.0).astype(BF16)
        rows = jnp.dot(onehot, u, preferred_element_type=F32)
        _to_token_tiles(stage.at[e * nblk + blk], rows)
        slots_copy(stage.at[e * nblk + blk], base + blk * R, R).start()

    def finish(e, blk, base):
        slots_copy(stage.at[e * nblk + blk], base + blk * R, R).wait()

    per_block(emit)
    per_block(finish)

    @pl.when(t == pl.num_programs(0) - 1)
    def _():
        zeros_buf[...] = jnp.zeros_like(zeros_buf)
        bits = [1 << b for b in range(R.bit_length() - 1)]

        def pad_copies(fn):
            for e in range(padlo_ref.shape[0]):
                lo, n = padlo_ref[e], padn_ref[e]
                whole = n // R

                def body(k, carry, lo=lo):
                    fn(slots_copy(zeros_buf, lo + k * R, R))
                    return carry
                lax.fori_loop(0, whole, body, 0)
                rem = n - whole * R
                for sz in bits:
                    off = lo + whole * R + (rem // (2 * sz)) * (2 * sz)
                    pl.when((rem // sz) % 2 == 1)(functools.partial(
                        lambda off, sz: fn(slots_copy(zeros_buf.at[pl.ds(0, sz * SUBLANES)], off, sz)),
                        off, sz))

        pad_copies(lambda cp: cp.start())
        pad_copies(lambda cp: cp.wait())


def _dispatch(u, meta_t, base_te, cnt_te, pad_lo, pad_n, *, n_slots, tm):
    T, D = u.shape
    n_exp = pad_lo.shape[0] - 1
    assert tm % DISPATCH_ROWS == 0 and D == SUBLANES * LANES
    grid_spec = pltpu.PrefetchScalarGridSpec(
        num_scalar_prefetch=4,
        grid=(T // tm,),
        in_specs=[pl.BlockSpec((tm, D), lambda t, *_: (t, 0)),
                  pl.BlockSpec((SUBLANES, tm), lambda t, *_: (0, t))],
        out_specs=pl.BlockSpec(memory_space=pl.ANY),
        scratch_shapes=[pltpu.VMEM((n_exp * (tm // DISPATCH_ROWS), DISPATCH_ROWS * SUBLANES, LANES), F32),
                        pltpu.VMEM((DISPATCH_ROWS * SUBLANES, LANES), F32),
                        pltpu.SemaphoreType.DMA(())],
    )
    return pl.pallas_call(
        functools.partial(_dispatch_kernel, n_exp=n_exp),
        grid_spec=grid_spec,
        out_shape=jax.ShapeDtypeStruct((n_slots * SUBLANES, LANES), F32),
        compiler_params=_params(("arbitrary",)),
        name="moe_dispatch",
    )(base_te, cnt_te, pad_lo, pad_n, u, meta_t)


def _expert_kernel(te_ref, na_ref, xs_ref, wg_ref, wu_ref, wd_ref, o_ref, xb_ref, acc_ref):
    i = pl.program_id(0)
    j = pl.program_id(1)
    last = pl.num_programs(1) - 1
    active = i < na_ref[0]
    tm = xb_ref.shape[0]

    @pl.when(active & (j == 0))
    def _():
        for k, chunk in enumerate(_from_token_tiles(xs_ref, tm)):
            xb_ref[:, k * LANES:(k + 1) * LANES] = chunk.astype(BF16)
        acc_ref[...] = jnp.zeros_like(acc_ref)

    @pl.when(active)
    def _():
        xb = xb_ref[...]
        g = jnp.dot(xb, wg_ref[...].astype(BF16), preferred_element_type=F32)
        up = jnp.dot(xb, wu_ref[...].astype(BF16), preferred_element_type=F32)
        h = (g * _sigmoid(g) * up).astype(BF16)
        acc_ref[...] += jnp.dot(h, wd_ref[...].astype(BF16), preferred_element_type=F32)

    @pl.when(active & (j == last))
    def _():
        _to_token_tiles(o_ref, acc_ref[...])

    @pl.when(jnp.logical_not(active) & (j == last))
    def _():
        o_ref[...] = jnp.zeros_like(o_ref)


def _expert_ffn(xs, tile_expert, n_active, wg, wu, wd, *, tm, tf):
    D = wg.shape[1]
    n_tiles = xs.shape[0] // (tm * SUBLANES)
    ff = wg.shape[2]
    assert ff % tf == 0
    nj = ff // tf

    def jj(i, j, na):
        return jnp.where(i < na[0], j, nj - 1)

    def ii(i, na):
        return jnp.minimum(i, na[0] - 1)

    grid_spec = pltpu.PrefetchScalarGridSpec(
        num_scalar_prefetch=2,
        grid=(n_tiles, nj),
        in_specs=[pl.BlockSpec((tm * SUBLANES, LANES), lambda i, j, te, na: (ii(i, na), 0)),
                  pl.BlockSpec((None, D, tf), lambda i, j, te, na: (te[i], 0, jj(i, j, na))),
                  pl.BlockSpec((None, D, tf), lambda i, j, te, na: (te[i], 0, jj(i, j, na))),
                  pl.BlockSpec((None, tf, D), lambda i, j, te, na: (te[i], jj(i, j, na), 0))],
        out_specs=pl.BlockSpec((tm * SUBLANES, LANES), lambda i, j, te, na: (i, 0)),
        scratch_shapes=[pltpu.VMEM((tm, D), BF16), pltpu.VMEM((tm, D), F32)],
    )
    return pl.pallas_call(
        _expert_kernel,
        grid_spec=grid_spec,
        out_shape=jax.ShapeDtypeStruct(xs.shape, F32),
        compiler_params=_params(("arbitrary", "arbitrary")),
        name="moe_experts",
    )(tile_expert, n_active, xs, wg, wu, wd)


def _combine_kernel(d0_ref, d1_ref, ys_hbm, meta_ref, x_ref, gate_ref, lg_ref, lb_ref, o_ref,
                    buf, sem, *, alpha):
    tm = x_ref.shape[0]

    def issue(g, carry):
        for u in range(DMA_LOOP_UNROLL):
            r = g * DMA_LOOP_UNROLL + u
            _token_copy(ys_hbm, d0_ref[0, r], buf.at[0], r, sem).start(priority=0)
            _token_copy(ys_hbm, d1_ref[0, r], buf.at[1], r, sem).start(priority=1)
        return carry

    def drain(r, carry):
        _token_copy(ys_hbm, 0, buf.at[0], r, sem).wait()
        _token_copy(ys_hbm, 0, buf.at[1], r, sem).wait()
        return carry

    lax.fori_loop(0, tm // DMA_LOOP_UNROLL, issue, 0)
    lax.fori_loop(0, tm, drain, 0, unroll=DMA_LOOP_UNROLL)
    meta = meta_ref[...]
    w0 = meta[:, META_W0:META_W0 + 1]
    w1 = meta[:, META_W1:META_W1 + 1]
    y = jnp.concatenate([w0 * a + w1 * b for a, b in zip(_from_token_tiles(buf.at[0], tm),
                                                         _from_token_tiles(buf.at[1], tm))], axis=1)
    z = alpha * x_ref[...] + (1.0 + gate_ref[...]) * y
    o_ref[...] = _layer_norm(z, lg_ref[...], lb_ref[...])


def _combine(ys, d0_tiles, d1_tiles, meta, x, gate, ln_g, ln_b, *, alpha, tm):
    B, S, D = x.shape
    nt = S // tm
    ids = pl.BlockSpec((None, 1, tm), lambda b, i: (b * nt + i, 0, 0), memory_space=pltpu.SMEM)
    row = pl.BlockSpec((None, tm, D), lambda b, i: (b, i, 0))
    vec = pl.BlockSpec((1, D), lambda b, i: (0, 0))
    return pl.pallas_call(
        functools.partial(_combine_kernel, alpha=alpha),
        grid=(B, nt),
        in_specs=[ids, ids, pl.BlockSpec(memory_space=pl.ANY),
                  pl.BlockSpec((None, tm, LANES), lambda b, i: (b, i, 0)),
                  row, pl.BlockSpec((None, 1, D), lambda b, i: (b, 0, 0)), vec, vec],
        out_specs=row,
        out_shape=jax.ShapeDtypeStruct((B, S, D), F32),
        scratch_shapes=[pltpu.VMEM((2, tm * SUBLANES, LANES), F32), pltpu.SemaphoreType.DMA(())],
        compiler_params=_params(("arbitrary", "arbitrary")),
        name="moe_combine",
    )(d0_tiles, d1_tiles, ys, meta, x, gate, ln_g, ln_b)


def _moe_ffn(x, scale, shift, gate, w_router, wg, wu, wd, ln_g, ln_b, *, alpha, tm, tm_e, tf):
    B, S, D = x.shape
    T = B * S
    n_exp = w_router.shape[1]
    wr = jnp.zeros((D, LANES), F32).at[:, :n_exp].set(w_router)
    wr_hi = wr.astype(BF16)
    wr = jnp.stack([wr_hi, (wr - wr_hi.astype(F32)).astype(BF16)])
    u, meta, counts = _router(x, scale, shift, wr, n_exp=n_exp, tm=tm)

    nt = T // tm
    meta2 = meta.reshape(T, LANES)
    e0 = meta2[:, META_E0].astype(I32)
    e1 = meta2[:, META_E1].astype(I32)
    r0 = meta2[:, META_R0].astype(I32)
    r1 = meta2[:, META_R1].astype(I32)
    cnt = counts[0, :n_exp].astype(I32)
    sizes = ((cnt + DISPATCH_ROWS + tm_e - 1) // tm_e) * tm_e
    ends = jnp.cumsum(sizes)
    starts = ends - sizes
    d0 = starts[e0] + r0
    d1 = starts[e1] + r1
    n_tiles = -(-(TOP_K * T + n_exp * DISPATCH_ROWS) // tm_e) + n_exp
    n_active = (ends[-1] // tm_e).astype(I32)
    tile_id = jnp.minimum(jnp.arange(n_tiles, dtype=I32), n_active - 1)
    tile_expert = jnp.sum((ends[None, :] <= (tile_id * tm_e)[:, None]).astype(I32), axis=1)
    experts = jnp.arange(n_exp, dtype=I32)
    chosen = ((e0[:, None] == experts) | (e1[:, None] == experts)).astype(I32)
    cnt_te = chosen.reshape(nt, tm, n_exp).sum(axis=1)
    base_te = starts[None, :] + jnp.cumsum(cnt_te, axis=0) - cnt_te
    meta_t = jnp.zeros((SUBLANES, T), I32).at[0].set(e0).at[1].set(e1).at[2].set(d0).at[3].set(d1)

    xs = _dispatch(u.reshape(T, D), meta_t, base_te.reshape(-1), cnt_te.reshape(-1),
                   jnp.concatenate([starts + cnt, ends[-1:]]),
                   jnp.concatenate([sizes - cnt, n_tiles * tm_e - ends[-1:]]),
                   n_slots=n_tiles * tm_e, tm=tm)
    ys = _expert_ffn(xs, tile_expert, n_active.reshape(1), wg, wu, wd, tm=tm_e, tf=tf)
    return _combine(ys, d0.reshape(nt, 1, tm), d1.reshape(nt, 1, tm), meta, x, gate, ln_g, ln_b,
                    alpha=alpha, tm=tm)


def kernel(x, c, positions, w_in, w_out, attn_norm_gain, rec_norm_gain, rec_lb_logits, ada_w, ada_b,
           ln_gain, ln_bias, ffn_w_gate, ffn_w_up, ffn_w_down, moe_router, moe_w_gate, moe_w_up,
           moe_w_down):
    B, S, D = x.shape
    depth = w_in.shape[0]
    alpha = (2 * depth) ** 0.25

    p = jax.nn.softmax(rec_lb_logits.astype(F32), axis=0)
    cum = jnp.cumsum(p, axis=0)
    lb_all = cum - cum[0:1]
    half = ATTN_HEAD_DIM // 2
    inv_freq = ROPE_THETA ** (-jnp.arange(half, dtype=F32) / half)
    inv_freq_lanes = jnp.tile(inv_freq, LANES // half).reshape(1, LANES)
    pos_f = positions.astype(F32).reshape(B, S, 1)
    w_in_b = w_in.astype(BF16)
    w_out_b = w_out.astype(BF16)
    ff = ffn_w_gate.shape[2]
    ffp = -(-ff // (2 * LANES)) * (2 * LANES)
    pad_c = lambda w: jnp.pad(w.astype(BF16), ((0, 0), (0, 0), (0, ffp - ff)))
    ffn_g, ffn_u = pad_c(ffn_w_gate), pad_c(ffn_w_up)
    ffn_d = jnp.pad(ffn_w_down.astype(BF16), ((0, 0), (0, ffp - ff), (0, 0)))

    mods = _ada_modulation(c, ada_w, ada_b)
    rope_cos, rope_sin = _rope_tables(pos_f, inv_freq_lanes, tm=min(1024, S))

    def mod(layer, sub):
        m = mods[layer * 2 + sub].reshape(B, 1, 3 * D)
        return m[..., :D], m[..., D:2 * D], m[..., 2 * D:]

    vec = lambda a: a.reshape(1, -1)
    for layer in range(depth):
        shift, scale, gate = mod(layer, 0)
        proj = _in_projection(x, scale, shift, rope_cos, rope_sin, w_in_b[layer], tm=min(1024, S))
        attn = _dilated_attention(proj)
        rec = _hgrn2(proj, lb_all[layer], vec(rec_norm_gain[layer]))
        x = _out_projection(attn, rec, x, gate, vec(attn_norm_gain[layer]), w_out_b[layer],
                            vec(ln_gain[layer, 0]), vec(ln_bias[layer, 0]), alpha=alpha, tm=min(512, S))
        shift, scale, gate = mod(layer, 1)
        j = layer // 2
        if layer % 2 == 0:
            x = _dense_ffn(x, scale, shift, gate, ffn_g[j], ffn_u[j], ffn_d[j],
                           vec(ln_gain[layer, 1]), vec(ln_bias[layer, 1]),
                           alpha=alpha, tm=min(512, S), tc=2 * LANES)
        else:
            x = _moe_ffn(x, scale, shift, gate, moe_router[j], moe_w_gate[j], moe_w_up[j],
                         moe_w_down[j], vec(ln_gain[layer, 1]), vec(ln_bias[layer, 1]),
                         alpha=alpha, tm=min(512, S), tm_e=min(1024, B * S), tf=512)
    return x
```

```python
import functools

import jax
import jax.numpy as jnp
from jax import lax
from jax.experimental import pallas as pl
from jax.experimental.pallas import tpu as pltpu

F32 = jnp.float32
BF16 = jnp.bfloat16
I32 = jnp.int32

LANES = 128
V7X_VMEM_LIMIT_BYTES = 56 * 1024 * 1024

ATTN_WIDTH = 512
ATTN_HEAD_DIM = 64
DILATIONS = (1, 4, 16)
ATTN_RADIUS = 64
ROPE_THETA = 10000.0
REC_WIDTH = 512
REC_DIM = 128
REC_CHUNK = 64
TOP_K = 2
LN_EPS = 1e-5
RMS_EPS = 1e-6
MASK_VALUE = -1e30
DECAY_EXP_CLAMP = 80.0


def _params(semantics):
    return pltpu.CompilerParams(dimension_semantics=semantics,
                                vmem_limit_bytes=V7X_VMEM_LIMIT_BYTES)


def _sigmoid(x):
    return 1.0 / (1.0 + jnp.exp(-x))


def _layer_norm(z, gain, bias):
    mu = jnp.mean(z, axis=-1, keepdims=True)
    zc = z - mu
    var = jnp.mean(zc * zc, axis=-1, keepdims=True)
    return zc * lax.rsqrt(var + LN_EPS) * gain + bias


def _ada_kernel(c_ref, w_ref, b_ref, o_ref):
    c = c_ref[...]
    o_ref[...] = jnp.dot(c * _sigmoid(c), w_ref[...], precision=lax.Precision.HIGHEST,
                         preferred_element_type=F32) + b_ref[...]


def _ada_modulation(c, ada_w, ada_b):
    B, D = c.shape
    n = ada_w.shape[0] * ada_w.shape[1]
    n3 = ada_w.shape[-1]
    tn = 1024
    return pl.pallas_call(
        _ada_kernel,
        grid=(n, n3 // tn),
        in_specs=[pl.BlockSpec((B, D), lambda i, j: (0, 0)),
                  pl.BlockSpec((None, D, tn), lambda i, j: (i, 0, j)),
                  pl.BlockSpec((None, 1, tn), lambda i, j: (i, 0, j))],
        out_specs=pl.BlockSpec((None, B, tn), lambda i, j: (i, 0, j)),
        out_shape=jax.ShapeDtypeStruct((n, B, n3), F32),
        compiler_params=_params(("arbitrary", "arbitrary")),
        name="ada_modulation",
    )(c, ada_w.reshape(n, D, n3), ada_b.reshape(n, 1, n3))


def _rope_kernel(pos_ref, invf_ref, cos_ref, sin_ref):
    ang = pos_ref[...] * invf_ref[...]
    lane = lax.broadcasted_iota(I32, (1, LANES), 1)
    first_half = (lane % ATTN_HEAD_DIM) < (ATTN_HEAD_DIM // 2)
    sin = jnp.sin(ang)
    cos_ref[...] = jnp.cos(ang)
    sin_ref[...] = jnp.where(first_half, -sin, sin)


def _rope_tables(pos_f, inv_freq_lanes, *, tm):
    B, S, _ = pos_f.shape
    tab = pl.BlockSpec((None, tm, LANES), lambda b, i: (b, i, 0))
    return pl.pallas_call(
        _rope_kernel,
        grid=(B, S // tm),
        in_specs=[pl.BlockSpec((None, tm, 1), lambda b, i: (b, i, 0)),
                  pl.BlockSpec((1, LANES), lambda b, i: (0, 0))],
        out_specs=[tab, tab],
        out_shape=[jax.ShapeDtypeStruct((B, S, LANES), F32)] * 2,
        compiler_params=_params(("arbitrary", "arbitrary")),
        name="rope_tables",
    )(pos_f, inv_freq_lanes)


def _inproj_kernel(x_ref, sc_ref, sh_ref, cos_ref, sin_ref, w_ref, o_ref, u_ref, *, tn):
    j = pl.program_id(2)

    @pl.when(j == 0)
    def _():
        u_ref[...] = (x_ref[...] * (1.0 + sc_ref[...]) + sh_ref[...]).astype(BF16)

    cols = pl.ds(pl.multiple_of(j * tn, tn), tn)
    acc = jnp.dot(u_ref[...], w_ref[:, cols], preferred_element_type=F32)

    @pl.when(j > 0)
    def _():
        o_ref[...] = acc

    @pl.when(j == 0)
    def _():
        cos = cos_ref[...]
        sin_signed = sin_ref[...]
        lane = lax.broadcasted_iota(I32, (1, LANES), 1)
        first_half = (lane % ATTN_HEAD_DIM) < (ATTN_HEAD_DIM // 2)
        q_scale = ATTN_HEAD_DIM ** -0.5
        for kk in range(tn // LANES):
            c = acc[:, kk * LANES:(kk + 1) * LANES]
            rot = jnp.where(first_half, pltpu.roll(c, LANES - 32, 1), pltpu.roll(c, 32, 1))
            r = c * cos + rot * sin_signed
            if kk * LANES < ATTN_WIDTH:
                r = r * q_scale
            o_ref[:, kk * LANES:(kk + 1) * LANES] = r


def _in_projection(x, scale, shift, rope_cos, rope_sin, w_in_bf16, *, tm):
    B, S, D = x.shape
    n_cols = w_in_bf16.shape[1]
    tn = 2 * ATTN_WIDTH
    assert S % tm == 0 and n_cols % tn == 0
    return pl.pallas_call(
        functools.partial(_inproj_kernel, tn=tn),
        grid=(B, S // tm, n_cols // tn),
        in_specs=[pl.BlockSpec((None, tm, D), lambda b, i, j: (b, i, 0)),
                  pl.BlockSpec((None, 1, D), lambda b, i, j: (b, 0, 0)),
                  pl.BlockSpec((None, 1, D), lambda b, i, j: (b, 0, 0)),
                  pl.BlockSpec((None, tm, LANES), lambda b, i, j: (b, i, 0)),
                  pl.BlockSpec((None, tm, LANES), lambda b, i, j: (b, i, 0)),
                  pl.BlockSpec((D, n_cols), lambda b, i, j: (0, 0))],
        out_specs=pl.BlockSpec((None, tm, tn), lambda b, i, j: (b, i, j)),
        out_shape=jax.ShapeDtypeStruct((B, S, n_cols), F32),
        scratch_shapes=[pltpu.VMEM((tm, D), BF16)],
        compiler_params=_params(("arbitrary", "arbitrary", "arbitrary")),
        name="in_projection",
    )(x, scale, shift, rope_cos, rope_sin, w_in_bf16)


ATTN_BLOCK_GROUP = 8


def _attn_kernel(q_ref, k_ref, v_ref, o_ref, qs, ks, vs, qf, kf, vf, pm, plr, pa, nm, nl, na, bias_s,
                 *, S):
    lane = lax.broadcasted_iota(I32, (1, LANES), 1)
    head0 = lane < ATTN_HEAD_DIM

    for p, d in enumerate(DILATIONS):
        L = S // d
        tq = min(128, L)
        W = min(2 * tq, L)
        nb = L // tq

        d_prev = DILATIONS[p - 1] if p else 1
        ratio, l_prev = d // d_prev, S // d_prev
        keep_f32 = 0 < p < len(DILATIONS) - 1
        for src, stage, dst in ((q_ref, qf, qs), (k_ref, kf, ks), (v_ref, vf, vs)):
            src = src if p <= 1 else stage
            for r in range(d):
                a, r_prev = divmod(r, d_prev)
                rows = pl.ds(r_prev * l_prev + a, L, stride=ratio) if ratio > 1 else pl.ds(0, L)
                val = src[rows, :]
                if keep_f32:
                    stage[r * L:(r + 1) * L, :] = val
                dst[r * L:(r + 1) * L, :] = val.astype(BF16)

        n_blocks = d * nb
        rc = lax.broadcasted_iota(I32, (tq, W), 0) - lax.broadcasted_iota(I32, (tq, W), 1)
        for which, delta in enumerate((0, ATTN_RADIUS, tq)):
            bias_s[which, pl.ds(0, tq), pl.ds(0, W)] = jnp.where(
                jnp.abs(rc + delta) <= ATTN_RADIUS, 0.0, MASK_VALUE)
        group = min(ATTN_BLOCK_GROUP, n_blocks)
        assert n_blocks % group == 0

        def block_group(gi, carry, L=L, tq=tq, W=W, nb=nb, group=group):
            q0s, k0s, scores, probs, outs, sums = [], [], [], [], [], []
            for u in range(group):
                g = gi * group + u
                r = g // nb
                n = g - r * nb
                base = r * L
                q0 = pl.multiple_of(base + n * tq, tq)
                ws = jnp.clip(n * tq - ATTN_RADIUS, 0, L - W)
                k0 = pl.multiple_of(base + ws, 16)
                qb = qs[pl.ds(q0, tq), :]
                zero = jnp.zeros_like(qb)
                q2 = jnp.concatenate([jnp.where(head0, qb, zero), jnp.where(head0, zero, qb)], axis=0)
                s = lax.dot_general(q2, ks[pl.ds(k0, W), :], (((1,), (1,)), ((), ())),
                                    preferred_element_type=F32)
                delta = n * tq - ws
                which = jnp.where(delta == 0, 0, jnp.where(delta == ATTN_RADIUS, 1, 2))
                bias = bias_s[which, pl.ds(0, tq), pl.ds(0, W)]
                scores.append(s + jnp.concatenate([bias, bias], axis=0))
                q0s.append(q0)
                k0s.append(k0)
            for u in range(group):
                s = scores[u]
                m = jnp.max(s, axis=1, keepdims=True)
                e = jnp.exp(s - m)
                probs.append((m, e.astype(BF16)))
                sums.append(jnp.sum(e, axis=1, keepdims=True))
            for u in range(group):
                outs.append(jnp.dot(probs[u][1], vs[pl.ds(k0s[u], W), :], preferred_element_type=F32))
            for u in range(group):
                m, l = probs[u][0], sums[u]
                rows = pl.ds(q0s[u], tq)
                pm[rows, :] = jnp.where(head0, m[:tq], m[tq:])
                plr[rows, :] = jnp.where(head0, l[:tq], l[tq:])
                pa[rows, :] = jnp.where(head0, outs[u][:tq], outs[u][tq:])
            return carry

        lax.fori_loop(0, n_blocks // group, block_group, 0)

        for r in range(d):
            rows = pl.ds(r, L, stride=d) if d > 1 else pl.ds(0, L)
            nm[p, rows, :] = pm[r * L:(r + 1) * L, :]
            nl[p, rows, :] = plr[r * L:(r + 1) * L, :]
            na[p, rows, :] = pa[r * L:(r + 1) * L, :]

    m_all = jnp.maximum(jnp.maximum(nm[0], nm[1]), nm[2])
    num = jnp.zeros((S, LANES), F32)
    den = jnp.zeros((S, LANES), F32)
    for p in range(len(DILATIONS)):
        w = jnp.exp(nm[p] - m_all)
        num = num + w * na[p]
        den = den + w * nl[p]
    o_ref[...] = num / den


def _dilated_attention(proj):
    B, S, _ = proj.shape
    n_pairs = ATTN_WIDTH // LANES
    assert S % (16 * DILATIONS[-1]) == 0
    blk = lambda off: pl.BlockSpec((None, S, LANES), lambda b, h: (b, 0, off + h))
    return pl.pallas_call(
        functools.partial(_attn_kernel, S=S),
        grid=(B, n_pairs),
        in_specs=[blk(0), blk(n_pairs), blk(2 * n_pairs)],
        out_specs=pl.BlockSpec((None, S, LANES), lambda b, h: (b, 0, h)),
        out_shape=jax.ShapeDtypeStruct((B, S, ATTN_WIDTH), F32),
        scratch_shapes=[pltpu.VMEM((S, LANES), BF16)] * 3
                       + [pltpu.VMEM((S, LANES), F32)] * 6
                       + [pltpu.VMEM((len(DILATIONS), S, LANES), F32)] * 3
                       + [pltpu.VMEM((3, min(128, S), min(256, S)), F32)],
        compiler_params=_params(("arbitrary", "arbitrary")),
        name="dilated_attention",
    )(proj, proj, proj)


REC_GROUP_CHUNKS = 4
REC_GROUP_ROWS = REC_GROUP_CHUNKS * REC_CHUNK
REC_STAGE_GROUPS = 4


def _hgrn_kernel(rq_ref, zf_ref, zb_ref, ri_ref, rg_ref, lb_ref, gain_ref, o_ref,
                 q_s, vt_s, oi_s, qp_s, ut_s, dd_s, st_s, os_s, *, S):
    C, G, GR = REC_CHUNK, REC_GROUP_CHUNKS, REC_GROUP_ROWS
    nc, ng = S // C, S // GR
    sg = min(REC_STAGE_GROUPS, ng)
    assert ng % sg == 0

    row = lax.broadcasted_iota(I32, (GR, GR), 0)
    col = lax.broadcasted_iota(I32, (GR, GR), 1)
    same_chunk = (row // C) == (col // C)
    t_i = lax.broadcasted_iota(I32, (C, C), 0)
    s_i = lax.broadcasted_iota(I32, (C, C), 1)

    def prep(i, carry):
        r0 = pl.multiple_of(i * GR, GR)
        rq = rq_ref[pl.ds(r0, GR), :]
        q_s[pl.ds(r0, GR), :] = rq * _sigmoid(rq)
        v = ri_ref[pl.ds(r0, GR), :]
        for c in range(G):
            vt_s[i * G + c] = v[c * C:(c + 1) * C, :].T.astype(BF16)
        return carry

    lax.fori_loop(0, ng, prep, 0)

    for direction in range(2):
        fwd = direction == 0
        z_ref = zf_ref if fwd else zb_ref
        cum = jnp.where(same_chunk & ((col <= row) if fwd else (col >= row)), 1.0, 0.0).astype(BF16)
        keep = (s_i <= t_i) if fwd else (s_i >= t_i)
        mid_row = C // 2 - 1 if fwd else C // 2
        last_row = C - 1 if fwd else 0

        def phase_a(it, carry, fwd=fwd, z_ref=z_ref, cum=cum, keep=keep,
                    mid_row=mid_row, last_row=last_row, direction=direction):
            lb = lb_ref[direction:direction + 1, :]
            log_lb = jnp.log(lb)
            log_1m_lb = jnp.log1p(-lb)
            r0s, c0s, k3s, bcs, atts, stage3 = [], [], [], [], [], []
            for u in range(sg):
                i = it * sg + u
                r0 = pl.multiple_of(i * GR, GR)
                z = z_ref[pl.ds(r0, GR), :]
                e = jnp.exp(-jnp.abs(z))
                t = log_1m_lb + (jnp.minimum(z, 0.0) - jnp.log(1.0 + e))
                g = jnp.maximum(log_lb, t) + jnp.log(1.0 + jnp.exp(-jnp.abs(log_lb - t)))
                kk = (1.0 - lb) * (jnp.where(z > 0, e, 1.0) / (1.0 + e))
                g1 = g.astype(BF16)
                r1 = g - g1.astype(F32)
                g2 = r1.astype(BF16)
                g3 = (r1 - g2.astype(F32)).astype(BF16)
                bcs.append(jnp.dot(cum, g1, preferred_element_type=F32)
                           + jnp.dot(cum, g2, preferred_element_type=F32)
                           + jnp.dot(cum, g3, preferred_element_type=F32))
                k3s.append(kk.reshape(G, C, REC_DIM))
                r0s.append(r0)
                c0s.append(pl.multiple_of(i * G, G))
            for u in range(sg):
                b3 = bcs[u].reshape(G, C, REC_DIM)
                mid = b3[:, mid_row:mid_row + 1, :]
                last = b3[:, last_row:last_row + 1, :]
                e3 = b3 - mid
                q3 = q_s[pl.ds(r0s[u], GR), :].reshape(G, C, REC_DIM)
                qt = (q3 * jnp.exp(jnp.minimum(e3, DECAY_EXP_CLAMP))).astype(BF16)
                kt = (k3s[u] * jnp.exp(jnp.minimum(-e3, DECAY_EXP_CLAMP))).astype(BF16)
                atts.append(jnp.einsum('gtc,gsc->gts', qt, kt, preferred_element_type=F32))
                qp_s[pl.ds(r0s[u], GR), :] = (q3 * jnp.exp(b3)).astype(BF16).reshape(GR, REC_DIM)
                stage3.append(((k3s[u] * jnp.exp(last - b3)).astype(BF16), jnp.exp(last)))
            for u in range(sg):
                att = jnp.where(keep[None], atts[u], 0.0).astype(BF16)
                v3 = ri_ref[pl.ds(r0s[u], GR), :].reshape(G, C, REC_DIM).astype(BF16)
                oi = jnp.einsum('gts,gsv->gtv', att, v3, preferred_element_type=F32)
                oi_s[pl.ds(r0s[u], GR), :] = oi.reshape(GR, REC_DIM)
                kp, decay = stage3[u]
                ut_s[pl.ds(c0s[u], G)] = jnp.einsum('gvs,gsc->gvc', vt_s[pl.ds(c0s[u], G)], kp,
                                                    preferred_element_type=F32)
                dd_s[pl.ds(c0s[u], G)] = decay
            return carry

        lax.fori_loop(0, ng // sg, phase_a, 0)

        def scan(c, st, fwd=fwd):
            idx = c if fwd else nc - 1 - c
            st_s[idx] = st.astype(BF16)
            return dd_s[idx] * st + ut_s[idx]

        lax.fori_loop(0, nc, scan, jnp.zeros((REC_DIM, REC_DIM), F32))

        def phase_c(i, carry, fwd=fwd):
            r0 = pl.multiple_of(i * GR, GR)
            c0 = pl.multiple_of(i * G, G)
            qp3 = qp_s[pl.ds(r0, GR), :].reshape(G, C, REC_DIM)
            oo = jnp.einsum('gtc,gvc->gtv', qp3, st_s[pl.ds(c0, G)], preferred_element_type=F32)
            tot = oi_s[pl.ds(r0, GR), :] + oo.reshape(GR, REC_DIM)
            if fwd:
                os_s[pl.ds(r0, GR), :] = tot
            else:
                os_s[pl.ds(r0, GR), :] = os_s[pl.ds(r0, GR), :] + tot
            return carry

        lax.fori_loop(0, ng, phase_c, 0, unroll=2)

    o = os_s[...]
    o = o * lax.rsqrt(jnp.mean(o * o, axis=-1, keepdims=True) + RMS_EPS) * gain_ref[...]
    o_ref[...] = o * _sigmoid(rg_ref[...])


def _hgrn2(proj, lb, rec_gain):
    B, S, _ = proj.shape
    nh = REC_WIDTH // REC_DIM
    assert S % REC_GROUP_ROWS == 0
    col0 = 3 * ATTN_WIDTH // LANES
    blk = lambda k: pl.BlockSpec((None, S, REC_DIM), lambda b, h: (b, 0, col0 + k * nh + h))
    nc = S // REC_CHUNK
    return pl.pallas_call(
        functools.partial(_hgrn_kernel, S=S),
        grid=(B, nh),
        in_specs=[blk(0), blk(1), blk(2), blk(3), blk(4),
                  pl.BlockSpec((2, REC_DIM), lambda b, h: (0, h)),
                  pl.BlockSpec((1, REC_DIM), lambda b, h: (0, h))],
        out_specs=pl.BlockSpec((None, S, REC_DIM), lambda b, h: (b, 0, h)),
        out_shape=jax.ShapeDtypeStruct((B, S, REC_WIDTH), F32),
        scratch_shapes=[pltpu.VMEM((S, REC_DIM), F32),
                        pltpu.VMEM((nc, REC_DIM, REC_CHUNK), BF16),
                        pltpu.VMEM((S, REC_DIM), F32),
                        pltpu.VMEM((S, REC_DIM), BF16),
                        pltpu.VMEM((nc, REC_DIM, REC_DIM), F32),
                        pltpu.VMEM((nc, 1, REC_DIM), F32),
                        pltpu.VMEM((nc, REC_DIM, REC_DIM), BF16),
                        pltpu.VMEM((S, REC_DIM), F32)],
        compiler_params=_params(("arbitrary", "arbitrary")),
        name="hgrn2",
    )(proj, proj, proj, proj, proj, lb, rec_gain)


def _outproj_kernel(a_ref, r_ref, x_ref, gate_ref, ag_ref, w_ref, lg_ref, lb_ref, o_ref, *, alpha):
    a = a_ref[...]
    an = a * lax.rsqrt(jnp.mean(a * a, axis=-1, keepdims=True) + RMS_EPS) * ag_ref[...]
    y = (jnp.dot(an.astype(BF16), w_ref[:ATTN_WIDTH, :], preferred_element_type=F32)
         + jnp.dot(r_ref[...].astype(BF16), w_ref[ATTN_WIDTH:, :], preferred_element_type=F32))
    z = alpha * x_ref[...] + (1.0 + gate_ref[...]) * y
    o_ref[...] = _layer_norm(z, lg_ref[...], lb_ref[...])


def _out_projection(attn, rec, x, gate, attn_gain, w_out_bf16, ln_g, ln_b, *, alpha, tm):
    B, S, D = x.shape
    row = lambda w: pl.BlockSpec((None, tm, w), lambda b, i: (b, i, 0))
    vec = lambda w: pl.BlockSpec((1, w), lambda b, i: (0, 0))
    return pl.pallas_call(
        functools.partial(_outproj_kernel, alpha=alpha),
        grid=(B, S // tm),
        in_specs=[row(ATTN_WIDTH), row(REC_WIDTH), row(D),
                  pl.BlockSpec((None, 1, D), lambda b, i: (b, 0, 0)),
                  vec(ATTN_WIDTH),
                  pl.BlockSpec((ATTN_WIDTH + REC_WIDTH, D), lambda b, i: (0, 0)),
                  vec(D), vec(D)],
        out_specs=row(D),
        out_shape=jax.ShapeDtypeStruct((B, S, D), F32),
        compiler_params=_params(("arbitrary", "arbitrary")),
        name="out_projection",
    )(attn, rec, x, gate, attn_gain, w_out_bf16, ln_g, ln_b)


def _ffn_kernel(x_ref, sc_ref, sh_ref, gate_ref, wg_ref, wu_ref, wd_ref, lg_ref, lb_ref, o_ref,
                *, alpha, tc):
    x = x_ref[...]
    u = (x * (1.0 + sc_ref[...]) + sh_ref[...]).astype(BF16)
    acc = jnp.zeros(x.shape, F32)
    for c in range(wg_ref.shape[1] // tc):
        cols = slice(c * tc, (c + 1) * tc)
        g = jnp.dot(u, wg_ref[:, cols], preferred_element_type=F32)
        up = jnp.dot(u, wu_ref[:, cols], preferred_element_type=F32)
        h = (g * _sigmoid(g) * up).astype(BF16)
        acc = acc + jnp.dot(h, wd_ref[cols, :], preferred_element_type=F32)
    z = alpha * x + (1.0 + gate_ref[...]) * acc
    o_ref[...] = _layer_norm(z, lg_ref[...], lb_ref[...])


def _dense_ffn(x, scale, shift, gate, wg, wu, wd, ln_g, ln_b, *, alpha, tm, tc):
    B, S, D = x.shape
    ffp = wg.shape[1]
    assert ffp % tc == 0
    row = pl.BlockSpec((None, tm, D), lambda b, i: (b, i, 0))
    mod = pl.BlockSpec((None, 1, D), lambda b, i: (b, 0, 0))
    vec = pl.BlockSpec((1, D), lambda b, i: (0, 0))
    return pl.pallas_call(
        functools.partial(_ffn_kernel, alpha=alpha, tc=tc),
        grid=(B, S // tm),
        in_specs=[row, mod, mod, mod,
                  pl.BlockSpec((D, ffp), lambda b, i: (0, 0)),
                  pl.BlockSpec((D, ffp), lambda b, i: (0, 0)),
                  pl.BlockSpec((ffp, D), lambda b, i: (0, 0)),
                  vec, vec],
        out_specs=row,
        out_shape=jax.ShapeDtypeStruct((B, S, D), F32),
        compiler_params=_params(("arbitrary", "arbitrary")),
        name="dense_ffn",
    )(x, scale, shift, gate, wg, wu, wd, ln_g, ln_b)


META_E0, META_E1, META_W0, META_W1, META_R0, META_R1 = range(6)
SUBLANES = 8


def _to_token_tiles(ref, val):
    n = val.shape[0]
    for k in range(SUBLANES):
        ref[pl.ds(k, n, stride=SUBLANES), :] = val[:, k * LANES:(k + 1) * LANES]


def _from_token_tiles(ref, n):
    return [ref[pl.ds(k, n, stride=SUBLANES), :] for k in range(SUBLANES)]


def _router_kernel(x_ref, sc_ref, sh_ref, wr_ref, u_ref, meta_ref, cnt_ref, carry_ref, *, n_exp):
    tm = x_ref.shape[0]

    @pl.when((pl.program_id(0) == 0) & (pl.program_id(1) == 0))
    def _():
        carry_ref[...] = jnp.zeros_like(carry_ref)

    u = x_ref[...] * (1.0 + sc_ref[...]) + sh_ref[...]
    u_ref[...] = u.astype(BF16)
    u_hi = u.astype(BF16)
    u_lo = (u - u_hi.astype(F32)).astype(BF16)
    logits = (jnp.dot(u_hi, wr_ref[0], preferred_element_type=F32)
              + jnp.dot(u_lo, wr_ref[0], preferred_element_type=F32)
              + jnp.dot(u_hi, wr_ref[1], preferred_element_type=F32))
    lane = lax.broadcasted_iota(I32, (tm, LANES), 1)
    neg = -jnp.inf
    l1 = jnp.where(lane < n_exp, logits, neg)
    m1 = jnp.max(l1, axis=1, keepdims=True)
    i1 = jnp.min(jnp.where(l1 == m1, lane, LANES), axis=1, keepdims=True)
    l2 = jnp.where(lane == i1, neg, l1)
    m2 = jnp.max(l2, axis=1, keepdims=True)
    i2 = jnp.min(jnp.where(l2 == m2, lane, LANES), axis=1, keepdims=True)
    e = jnp.exp(m2 - m1)
    w1 = 1.0 / (1.0 + e)
    w2 = e * w1
    sel = jnp.where((lane == i1) | (lane == i2), 1.0, 0.0)
    before = (lax.broadcasted_iota(I32, (tm, tm), 1) < lax.broadcasted_iota(I32, (tm, tm), 0))
    ranks = jnp.dot(jnp.where(before, 1.0, 0.0).astype(BF16), sel.astype(BF16),
                    preferred_element_type=F32) + carry_ref[...]
    r1 = jnp.sum(jnp.where(lane == i1, ranks, 0.0), axis=1, keepdims=True)
    r2 = jnp.sum(jnp.where(lane == i2, ranks, 0.0), axis=1, keepdims=True)
    carry_ref[...] = carry_ref[...] + jnp.sum(sel, axis=0, keepdims=True)
    cnt_ref[...] = carry_ref[...]
    meta = jnp.zeros((tm, LANES), F32)
    for k, val in ((META_E0, i1.astype(F32)), (META_E1, i2.astype(F32)), (META_W0, w1),
                   (META_W1, w2), (META_R0, r1), (META_R1, r2)):
        meta = jnp.where(lane == k, val, meta)
    meta_ref[...] = meta


def _router(x, scale, shift, w_router_lanes, *, n_exp, tm):
    B, S, D = x.shape
    assert D == SUBLANES * LANES
    nt = S // tm
    row = pl.BlockSpec((None, tm, D), lambda b, i: (b, i, 0))
    mod = pl.BlockSpec((None, 1, D), lambda b, i: (b, 0, 0))
    return pl.pallas_call(
        functools.partial(_router_kernel, n_exp=n_exp),
        grid=(B, nt),
        in_specs=[row, mod, mod, pl.BlockSpec((2, D, LANES), lambda b, i: (0, 0, 0))],
        out_specs=[row,
                   pl.BlockSpec((None, tm, LANES), lambda b, i: (b, i, 0)),
                   pl.BlockSpec((1, LANES), lambda b, i: (0, 0))],
        out_shape=[jax.ShapeDtypeStruct((B, S, D), BF16),
                   jax.ShapeDtypeStruct((B, S, LANES), F32),
                   jax.ShapeDtypeStruct((1, LANES), F32)],
        scratch_shapes=[pltpu.VMEM((1, LANES), F32)],
        compiler_params=_params(("arbitrary", "arbitrary")),
        name="moe_router",
    )(x, scale, shift, w_router_lanes)


DISPATCH_ROWS = 128


def _dispatch_kernel(base_ref, cnt_ref, padlo_ref, padn_ref, u_ref, meta_ref, xs_hbm,
                     stage, extra, zeros_buf, sems, extra_sem, *, n_exp):
    t = pl.program_id(0)
    last = pl.num_programs(0) - 1
    tm = u_ref.shape[0]
    R = DISPATCH_ROWS
    slot = t % 2
    u = u_ref[...]
    e0, e1 = meta_ref[0:1, :], meta_ref[1:2, :]
    d0, d1 = meta_ref[2:3, :], meta_ref[3:4, :]
    srow = lax.broadcasted_iota(I32, (R, tm), 0)

    def to_hbm(src, first_slot, sem):
        dst = xs_hbm.at[pl.ds(pl.multiple_of(first_slot * SUBLANES, SUBLANES), src.shape[0])]
        return pltpu.make_async_copy(src, dst, sem)

    def main_copy(step, e, buf):
        rows = pl.ds(e * R * SUBLANES, R * SUBLANES)
        return to_hbm(stage.at[buf, rows], base_ref[step * n_exp + e], sems.at[buf])

    rels = []
    for e in range(n_exp):
        rel = jnp.where(e0 == e, d0, jnp.where(e1 == e, d1, -1))
        rels.append(jnp.where(rel >= 0, rel - base_ref[t * n_exp + e], -1))
    onehot = jnp.concatenate([jnp.where(srow == rel, 1.0, 0.0) for rel in rels], axis=0).astype(BF16)
    rows = jnp.dot(onehot, u, preferred_element_type=F32)
    _to_token_tiles(stage.at[slot], rows)

    @pl.when(t > 0)
    def _():
        for e in range(n_exp):
            main_copy(t - 1, e, 1 - slot).wait()
    for e in range(n_exp):
        main_copy(t, e, slot).start()

    for e in range(n_exp):
        for blk in range(1, tm // R):
            @pl.when(cnt_ref[t * n_exp + e] > blk * R)
            def _(e=e, blk=blk):
                more = jnp.where(srow + blk * R == rels[e], 1.0, 0.0).astype(BF16)
                _to_token_tiles(extra, jnp.dot(more, u, preferred_element_type=F32))
                cp = to_hbm(extra, base_ref[t * n_exp + e] + blk * R, extra_sem)
                cp.start()
                cp.wait()

    @pl.when(t == last)
    def _():
        for e in range(n_exp):
            main_copy(t, e, slot).wait()
        zeros_buf[...] = jnp.zeros_like(zeros_buf)
        bits = [1 << b for b in range(R.bit_length() - 1)]

        def pad_copies(fn):
            for p in range(padlo_ref.shape[0]):
                lo, n = padlo_ref[p], padn_ref[p]
                whole = n // R

                def body(k, carry, lo=lo):
                    fn(to_hbm(zeros_buf, lo + k * R, extra_sem))
                    return carry
                lax.fori_loop(0, whole, body, 0)
                rem = n - whole * R
                for sz in bits:
                    off = lo + whole * R + (rem // (2 * sz)) * (2 * sz)
                    pl.when((rem // sz) % 2 == 1)(functools.partial(
                        lambda off, sz: fn(to_hbm(zeros_buf.at[pl.ds(0, sz * SUBLANES)], off, extra_sem)),
                        off, sz))

        pad_copies(lambda cp: cp.start())
        pad_copies(lambda cp: cp.wait())


def _dispatch(u, meta_t, base_te, cnt_te, pad_lo, pad_n, *, n_slots, tm):
    T, D = u.shape
    n_exp = pad_lo.shape[0] - 1
    assert tm % DISPATCH_ROWS == 0 and D == SUBLANES * LANES
    block = DISPATCH_ROWS * SUBLANES
    grid_spec = pltpu.PrefetchScalarGridSpec(
        num_scalar_prefetch=4,
        grid=(T // tm,),
        in_specs=[pl.BlockSpec((tm, D), lambda t, *_: (t, 0)),
                  pl.BlockSpec((SUBLANES, tm), lambda t, *_: (0, t))],
        out_specs=pl.BlockSpec(memory_space=pl.ANY),
        scratch_shapes=[pltpu.VMEM((2, n_exp * block, LANES), F32),
                        pltpu.VMEM((block, LANES), F32),
                        pltpu.VMEM((block, LANES), F32),
                        pltpu.SemaphoreType.DMA((2,)),
                        pltpu.SemaphoreType.DMA(())],
    )
    return pl.pallas_call(
        functools.partial(_dispatch_kernel, n_exp=n_exp),
        grid_spec=grid_spec,
        out_shape=jax.ShapeDtypeStruct((n_slots * SUBLANES, LANES), F32),
        compiler_params=_params(("arbitrary",)),
        name="moe_dispatch",
    )(base_te, cnt_te, pad_lo, pad_n, u, meta_t)


def _expert_kernel(te_ref, na_ref, xs_ref, wg_ref, wu_ref, wd_ref, o_ref, xb_ref, acc_ref):
    i = pl.program_id(0)
    j = pl.program_id(1)
    last = pl.num_programs(1) - 1
    active = i < na_ref[0]
    tm = xb_ref.shape[0]

    @pl.when(active & (j == 0))
    def _():
        for k, chunk in enumerate(_from_token_tiles(xs_ref, tm)):
            xb_ref[:, k * LANES:(k + 1) * LANES] = chunk.astype(BF16)
        acc_ref[...] = jnp.zeros_like(acc_ref)

    @pl.when(active)
    def _():
        xb = xb_ref[...]
        g = jnp.dot(xb, wg_ref[...].astype(BF16), preferred_element_type=F32)
        up = jnp.dot(xb, wu_ref[...].astype(BF16), preferred_element_type=F32)
        h = (g * _sigmoid(g) * up).astype(BF16)
        acc_ref[...] += jnp.dot(h, wd_ref[...].astype(BF16), preferred_element_type=F32)

    @pl.when(active & (j == last))
    def _():
        _to_token_tiles(o_ref, acc_ref[...])

    @pl.when(jnp.logical_not(active) & (j == last))
    def _():
        o_ref[...] = jnp.zeros_like(o_ref)


def _expert_ffn(xs, tile_expert, n_active, wg, wu, wd, *, tm, tf):
    D = wg.shape[1]
    n_tiles = xs.shape[0] // (tm * SUBLANES)
    ff = wg.shape[2]
    assert ff % tf == 0
    nj = ff // tf

    def jj(i, j, na):
        return jnp.where(i < na[0], j, nj - 1)

    def ii(i, na):
        return jnp.minimum(i, na[0] - 1)

    grid_spec = pltpu.PrefetchScalarGridSpec(
        num_scalar_prefetch=2,
        grid=(n_tiles, nj),
        in_specs=[pl.BlockSpec((tm * SUBLANES, LANES), lambda i, j, te, na: (ii(i, na), 0)),
                  pl.BlockSpec((None, D, tf), lambda i, j, te, na: (te[i], 0, jj(i, j, na))),
                  pl.BlockSpec((None, D, tf), lambda i, j, te, na: (te[i], 0, jj(i, j, na))),
                  pl.BlockSpec((None, tf, D), lambda i, j, te, na: (te[i], jj(i, j, na), 0))],
        out_specs=pl.BlockSpec((tm * SUBLANES, LANES), lambda i, j, te, na: (i, 0)),
        scratch_shapes=[pltpu.VMEM((tm, D), BF16), pltpu.VMEM((tm, D), F32)],
    )
    return pl.pallas_call(
        _expert_kernel,
        grid_spec=grid_spec,
        out_shape=jax.ShapeDtypeStruct(xs.shape, F32),
        compiler_params=_params(("arbitrary", "arbitrary")),
        name="moe_experts",
    )(tile_expert, n_active, xs, wg, wu, wd)


def _combine_kernel(base_ref, off_ref, cnt_ref, ys_hbm, meta_ref, x_ref, gate_ref, lg_ref, lb_ref, o_ref,
                    stage, extra, acc_ref, sems, extra_sem, *, alpha, n_exp):
    t = pl.program_id(0) * pl.num_programs(1) + pl.program_id(1)
    n_steps = pl.num_programs(0) * pl.num_programs(1)
    tm = x_ref.shape[0]
    R = DISPATCH_ROWS
    slot = t % 2

    def from_hbm(first_slot, dst, sem):
        src = ys_hbm.at[pl.ds(pl.multiple_of(first_slot * SUBLANES, SUBLANES), R * SUBLANES)]
        return pltpu.make_async_copy(src, dst, sem)

    def main_copy(step, e, buf):
        return from_hbm(base_ref[step * n_exp + e], stage.at[buf, e], sems.at[buf, e])

    @pl.when(t == 0)
    def _():
        for e in range(n_exp):
            main_copy(0, e, 0).start()

    @pl.when(t + 1 < n_steps)
    def _():
        for e in range(n_exp):
            main_copy(t + 1, e, 1 - slot).start()

    meta = meta_ref[...]
    col = lambda k: meta[:, k:k + 1]
    e0, e1 = col(META_E0).astype(I32), col(META_E1).astype(I32)
    r0, r1 = col(META_R0).astype(I32), col(META_R1).astype(I32)
    w0, w1 = col(META_W0), col(META_W1)
    lane = lax.broadcasted_iota(I32, (tm, R), 1)

    def rows_bf16(ref):
        return jnp.concatenate([c.astype(BF16) for c in _from_token_tiles(ref, R)], axis=1)

    acc_ref[...] = jnp.zeros_like(acc_ref)
    for e in range(n_exp):
        rel = jnp.where(e0 == e, r0, jnp.where(e1 == e, r1, -1))
        rel = jnp.where(rel >= 0, rel - off_ref[t * n_exp + e], -1)
        w = jnp.where(e0 == e, w0, jnp.where(e1 == e, w1, 0.0))
        main_copy(t, e, slot).wait()
        onehot = jnp.where(lane == rel, 1.0, 0.0).astype(BF16)
        acc_ref[...] += w * jnp.dot(onehot, rows_bf16(stage.at[slot, e]), preferred_element_type=F32)
        for blk in range(1, tm // R):
            @pl.when(cnt_ref[t * n_exp + e] > blk * R)
            def _(e=e, blk=blk, rel=rel, w=w):
                cp = from_hbm(base_ref[t * n_exp + e] + blk * R, extra, extra_sem)
                cp.start()
                cp.wait()
                more = jnp.where(lane + blk * R == rel, 1.0, 0.0).astype(BF16)
                acc_ref[...] += w * jnp.dot(more, rows_bf16(extra), preferred_element_type=F32)

    z = alpha * x_ref[...] + (1.0 + gate_ref[...]) * acc_ref[...]
    o_ref[...] = _layer_norm(z, lg_ref[...], lb_ref[...])


def _combine(ys, base_te, off_te, cnt_te, meta, x, gate, ln_g, ln_b, *, alpha, tm):
    B, S, D = x.shape
    nt = S // tm
    n_exp = base_te.shape[0] // (B * nt)
    block = DISPATCH_ROWS * SUBLANES
    row = pl.BlockSpec((None, tm, D), lambda b, i, *_: (b, i, 0))
    vec = pl.BlockSpec((1, D), lambda b, i, *_: (0, 0))
    grid_spec = pltpu.PrefetchScalarGridSpec(
        num_scalar_prefetch=3,
        grid=(B, nt),
        in_specs=[pl.BlockSpec(memory_space=pl.ANY),
                  pl.BlockSpec((None, tm, LANES), lambda b, i, *_: (b, i, 0)),
                  row, pl.BlockSpec((None, 1, D), lambda b, i, *_: (b, 0, 0)), vec, vec],
        out_specs=row,
        scratch_shapes=[pltpu.VMEM((2, n_exp, block, LANES), F32),
                        pltpu.VMEM((block, LANES), F32),
                        pltpu.VMEM((tm, D), F32),
                        pltpu.SemaphoreType.DMA((2, n_exp)),
                        pltpu.SemaphoreType.DMA(())],
    )
    return pl.pallas_call(
        functools.partial(_combine_kernel, alpha=alpha, n_exp=n_exp),
        grid_spec=grid_spec,
        out_shape=jax.ShapeDtypeStruct((B, S, D), F32),
        compiler_params=_params(("arbitrary", "arbitrary")),
        name="moe_combine",
    )(base_te, off_te, cnt_te, ys, meta, x, gate, ln_g, ln_b)


def _moe_ffn(x, scale, shift, gate, w_router, wg, wu, wd, ln_g, ln_b, *, alpha, tm, tm_e, tf):
    B, S, D = x.shape
    T = B * S
    n_exp = w_router.shape[1]
    wr = jnp.zeros((D, LANES), F32).at[:, :n_exp].set(w_router)
    wr_hi = wr.astype(BF16)
    wr = jnp.stack([wr_hi, (wr - wr_hi.astype(F32)).astype(BF16)])
    u, meta, counts = _router(x, scale, shift, wr, n_exp=n_exp, tm=tm)

    nt = T // tm
    meta2 = meta.reshape(T, LANES)
    e0 = meta2[:, META_E0].astype(I32)
    e1 = meta2[:, META_E1].astype(I32)
    r0 = meta2[:, META_R0].astype(I32)
    r1 = meta2[:, META_R1].astype(I32)
    cnt = counts[0, :n_exp].astype(I32)
    sizes = ((cnt + DISPATCH_ROWS + tm_e - 1) // tm_e) * tm_e
    ends = jnp.cumsum(sizes)
    starts = ends - sizes
    d0 = starts[e0] + r0
    d1 = starts[e1] + r1
    n_tiles = -(-(TOP_K * T + n_exp * DISPATCH_ROWS) // tm_e) + n_exp
    n_active = (ends[-1] // tm_e).astype(I32)
    tile_id = jnp.minimum(jnp.arange(n_tiles, dtype=I32), n_active - 1)
    tile_expert = jnp.sum((ends[None, :] <= (tile_id * tm_e)[:, None]).astype(I32), axis=1)
    experts = jnp.arange(n_exp, dtype=I32)
    chosen = ((e0[:, None] == experts) | (e1[:, None] == experts)).astype(I32)
    cnt_te = chosen.reshape(nt, tm, n_exp).sum(axis=1)
    base_te = starts[None, :] + jnp.cumsum(cnt_te, axis=0) - cnt_te
    meta_t = jnp.zeros((SUBLANES, T), I32).at[0].set(e0).at[1].set(e1).at[2].set(d0).at[3].set(d1)

    xs = _dispatch(u.reshape(T, D), meta_t, base_te.reshape(-1), cnt_te.reshape(-1),
                   jnp.concatenate([starts + cnt, ends[-1:]]),
                   jnp.concatenate([sizes - cnt, n_tiles * tm_e - ends[-1:]]),
                   n_slots=n_tiles * tm_e, tm=tm)
    ys = _expert_ffn(xs, tile_expert, n_active.reshape(1), wg, wu, wd, tm=tm_e, tf=tf)
    return _combine(ys, base_te.reshape(-1), (base_te - starts[None, :]).reshape(-1), cnt_te.reshape(-1),
                    meta, x, gate, ln_g, ln_b, alpha=alpha, tm=tm)


def kernel(x, c, positions, w_in, w_out, attn_norm_gain, rec_norm_gain, rec_lb_logits, ada_w, ada_b,
           ln_gain, ln_bias, ffn_w_gate, ffn_w_up, ffn_w_down, moe_router, moe_w_gate, moe_w_up,
           moe_w_down):
    B, S, D = x.shape
    depth = w_in.shape[0]
    alpha = (2 * depth) ** 0.25

    p = jax.nn.softmax(rec_lb_logits.astype(F32), axis=0)
    cum = jnp.cumsum(p, axis=0)
    lb_all = cum - cum[0:1]
    half = ATTN_HEAD_DIM // 2
    inv_freq = ROPE_THETA ** (-jnp.arange(half, dtype=F32) / half)
    inv_freq_lanes = jnp.tile(inv_freq, LANES // half).reshape(1, LANES)
    pos_f = positions.astype(F32).reshape(B, S, 1)
    w_in_b = w_in.astype(BF16)
    w_out_b = w_out.astype(BF16)
    ff = ffn_w_gate.shape[2]
    ffp = -(-ff // (2 * LANES)) * (2 * LANES)
    pad_c = lambda w: jnp.pad(w.astype(BF16), ((0, 0), (0, 0), (0, ffp - ff)))
    ffn_g, ffn_u = pad_c(ffn_w_gate), pad_c(ffn_w_up)
    ffn_d = jnp.pad(ffn_w_down.astype(BF16), ((0, 0), (0, ffp - ff), (0, 0)))

    mods = _ada_modulation(c, ada_w, ada_b)
    rope_cos, rope_sin = _rope_tables(pos_f, inv_freq_lanes, tm=min(1024, S))

    def mod(layer, sub):
        m = mods[layer * 2 + sub].reshape(B, 1, 3 * D)
        return m[..., :D], m[..., D:2 * D], m[..., 2 * D:]

    vec = lambda a: a.reshape(1, -1)
    for layer in range(depth):
        shift, scale, gate = mod(layer, 0)
        proj = _in_projection(x, scale, shift, rope_cos, rope_sin, w_in_b[layer], tm=min(1024, S))
        attn = _dilated_attention(proj)
        rec = _hgrn2(proj, lb_all[layer], vec(rec_norm_gain[layer]))
        x = _out_projection(attn, rec, x, gate, vec(attn_norm_gain[layer]), w_out_b[layer],
                            vec(ln_gain[layer, 0]), vec(ln_bias[layer, 0]), alpha=alpha, tm=min(512, S))
        shift, scale, gate = mod(layer, 1)
        j = layer // 2
        if layer % 2 == 0:
            x = _dense_ffn(x, scale, shift, gate, ffn_g[j], ffn_u[j], ffn_d[j],
                           vec(ln_gain[layer, 1]), vec(ln_bias[layer, 1]),
                           alpha=alpha, tm=min(512, S), tc=2 * LANES)
        else:
            x = _moe_ffn(x, scale, shift, gate, moe_router[j], moe_w_gate[j], moe_w_up[j],
                         moe_w_down[j], vec(ln_gain[layer, 1]), vec(ln_bias[layer, 1]),
                         alpha=alpha, tm=min(512, S), tm_e=min(1024, B * S), tf=512)
    return x
```

```python
import functools

import jax
import jax.numpy as jnp
from jax import lax
from jax.experimental import pallas as pl
from jax.experimental.pallas import tpu as pltpu

F32 = jnp.float32
BF16 = jnp.bfloat16
I32 = jnp.int32

LANES = 128
V7X_VMEM_LIMIT_BYTES = 56 * 1024 * 1024

ATTN_WIDTH = 512
ATTN_HEAD_DIM = 64
DILATIONS = (1, 4, 16)
ATTN_RADIUS = 64
ROPE_THETA = 10000.0
REC_WIDTH = 512
REC_DIM = 128
REC_CHUNK = 64
TOP_K = 2
LN_EPS = 1e-5
RMS_EPS = 1e-6
MASK_VALUE = -1e30
DECAY_EXP_CLAMP = 80.0


def _params(semantics):
    return pltpu.CompilerParams(dimension_semantics=semantics,
                                vmem_limit_bytes=V7X_VMEM_LIMIT_BYTES)


def _sigmoid(x):
    return 1.0 / (1.0 + jnp.exp(-x))


def _layer_norm(z, gain, bias):
    mu = jnp.mean(z, axis=-1, keepdims=True)
    zc = z - mu
    var = jnp.mean(zc * zc, axis=-1, keepdims=True)
    return zc * lax.rsqrt(var + LN_EPS) * gain + bias


def _ada_kernel(c_ref, w_ref, b_ref, o_ref):
    c = c_ref[...]
    o_ref[...] = jnp.dot(c * _sigmoid(c), w_ref[...], precision=lax.Precision.HIGHEST,
                         preferred_element_type=F32) + b_ref[...]


def _ada_modulation(c, ada_w, ada_b):
    B, D = c.shape
    n = ada_w.shape[0] * ada_w.shape[1]
    n3 = ada_w.shape[-1]
    tn = 1024
    return pl.pallas_call(
        _ada_kernel,
        grid=(n, n3 // tn),
        in_specs=[pl.BlockSpec((B, D), lambda i, j: (0, 0)),
                  pl.BlockSpec((None, D, tn), lambda i, j: (i, 0, j)),
                  pl.BlockSpec((None, 1, tn), lambda i, j: (i, 0, j))],
        out_specs=pl.BlockSpec((None, B, tn), lambda i, j: (i, 0, j)),
        out_shape=jax.ShapeDtypeStruct((n, B, n3), F32),
        compiler_params=_params(("arbitrary", "arbitrary")),
        name="ada_modulation",
    )(c, ada_w.reshape(n, D, n3), ada_b.reshape(n, 1, n3))


def _rope_kernel(pos_ref, invf_ref, cos_ref, sin_ref):
    ang = pos_ref[...] * invf_ref[...]
    lane = lax.broadcasted_iota(I32, (1, LANES), 1)
    first_half = (lane % ATTN_HEAD_DIM) < (ATTN_HEAD_DIM // 2)
    sin = jnp.sin(ang)
    cos_ref[...] = jnp.cos(ang)
    sin_ref[...] = jnp.where(first_half, -sin, sin)


def _rope_tables(pos_f, inv_freq_lanes, *, tm):
    B, S, _ = pos_f.shape
    tab = pl.BlockSpec((None, tm, LANES), lambda b, i: (b, i, 0))
    return pl.pallas_call(
        _rope_kernel,
        grid=(B, S // tm),
        in_specs=[pl.BlockSpec((None, tm, 1), lambda b, i: (b, i, 0)),
                  pl.BlockSpec((1, LANES), lambda b, i: (0, 0))],
        out_specs=[tab, tab],
        out_shape=[jax.ShapeDtypeStruct((B, S, LANES), F32)] * 2,
        compiler_params=_params(("arbitrary", "arbitrary")),
        name="rope_tables",
    )(pos_f, inv_freq_lanes)


def _inproj_kernel(x_ref, sc_ref, sh_ref, cos_ref, sin_ref, w_ref, o_ref, u_ref, *, tn):
    j = pl.program_id(2)

    @pl.when(j == 0)
    def _():
        u_ref[...] = (x_ref[...] * (1.0 + sc_ref[...]) + sh_ref[...]).astype(BF16)

    cols = pl.ds(pl.multiple_of(j * tn, tn), tn)
    acc = jnp.dot(u_ref[...], w_ref[:, cols], preferred_element_type=F32)

    @pl.when(j > 0)
    def _():
        o_ref[...] = acc

    @pl.when(j == 0)
    def _():
        cos = cos_ref[...]
        sin_signed = sin_ref[...]
        lane = lax.broadcasted_iota(I32, (1, LANES), 1)
        first_half = (lane % ATTN_HEAD_DIM) < (ATTN_HEAD_DIM // 2)
        q_scale = ATTN_HEAD_DIM ** -0.5
        for kk in range(tn // LANES):
            c = acc[:, kk * LANES:(kk + 1) * LANES]
            rot = jnp.where(first_half, pltpu.roll(c, LANES - 32, 1), pltpu.roll(c, 32, 1))
            r = c * cos + rot * sin_signed
            if kk * LANES < ATTN_WIDTH:
                r = r * q_scale
            o_ref[:, kk * LANES:(kk + 1) * LANES] = r


def _in_projection(x, scale, shift, rope_cos, rope_sin, w_in_bf16, *, tm):
    B, S, D = x.shape
    n_cols = w_in_bf16.shape[1]
    tn = 2 * ATTN_WIDTH
    assert S % tm == 0 and n_cols % tn == 0
    return pl.pallas_call(
        functools.partial(_inproj_kernel, tn=tn),
        grid=(B, S // tm, n_cols // tn),
        in_specs=[pl.BlockSpec((None, tm, D), lambda b, i, j: (b, i, 0)),
                  pl.BlockSpec((None, 1, D), lambda b, i, j: (b, 0, 0)),
                  pl.BlockSpec((None, 1, D), lambda b, i, j: (b, 0, 0)),
                  pl.BlockSpec((None, tm, LANES), lambda b, i, j: (b, i, 0)),
                  pl.BlockSpec((None, tm, LANES), lambda b, i, j: (b, i, 0)),
                  pl.BlockSpec((D, n_cols), lambda b, i, j: (0, 0))],
        out_specs=pl.BlockSpec((None, tm, tn), lambda b, i, j: (b, i, j)),
        out_shape=jax.ShapeDtypeStruct((B, S, n_cols), F32),
        scratch_shapes=[pltpu.VMEM((tm, D), BF16)],
        compiler_params=_params(("arbitrary", "arbitrary", "arbitrary")),
        name="in_projection",
    )(x, scale, shift, rope_cos, rope_sin, w_in_bf16)


ATTN_BLOCK_GROUP = 8


def _attn_kernel(q_ref, k_ref, v_ref, o_ref, qs, ks, vs, qf, kf, vf, pm, plr, pa, nm, nl, na, bias_s,
                 *, S):
    lane = lax.broadcasted_iota(I32, (1, LANES), 1)
    head0 = lane < ATTN_HEAD_DIM

    for p, d in enumerate(DILATIONS):
        L = S // d
        tq = min(128, L)
        W = min(2 * tq, L)
        nb = L // tq

        d_prev = DILATIONS[p - 1] if p else 1
        ratio, l_prev = d // d_prev, S // d_prev
        keep_f32 = 0 < p < len(DILATIONS) - 1
        for src, stage, dst in ((q_ref, qf, qs), (k_ref, kf, ks), (v_ref, vf, vs)):
            src = src if p <= 1 else stage
            for r in range(d):
                a, r_prev = divmod(r, d_prev)
                rows = pl.ds(r_prev * l_prev + a, L, stride=ratio) if ratio > 1 else pl.ds(0, L)
                val = src[rows, :]
                if keep_f32:
                    stage[r * L:(r + 1) * L, :] = val
                dst[r * L:(r + 1) * L, :] = val.astype(BF16)

        n_blocks = d * nb
        rc = lax.broadcasted_iota(I32, (tq, W), 0) - lax.broadcasted_iota(I32, (tq, W), 1)
        for which, delta in enumerate((0, ATTN_RADIUS, tq)):
            bias_s[which, pl.ds(0, tq), pl.ds(0, W)] = jnp.where(
                jnp.abs(rc + delta) <= ATTN_RADIUS, 0.0, MASK_VALUE)
        group = min(ATTN_BLOCK_GROUP, n_blocks)
        assert n_blocks % group == 0

        def block_group(gi, carry, L=L, tq=tq, W=W, nb=nb, group=group):
            q0s, k0s, scores, probs, outs, sums = [], [], [], [], [], []
            for u in range(group):
                g = gi * group + u
                r = g // nb
                n = g - r * nb
                base = r * L
                q0 = pl.multiple_of(base + n * tq, tq)
                ws = jnp.clip(n * tq - ATTN_RADIUS, 0, L - W)
                k0 = pl.multiple_of(base + ws, 16)
                qb = qs[pl.ds(q0, tq), :]
                zero = jnp.zeros_like(qb)
                q2 = jnp.concatenate([jnp.where(head0, qb, zero), jnp.where(head0, zero, qb)], axis=0)
                s = lax.dot_general(q2, ks[pl.ds(k0, W), :], (((1,), (1,)), ((), ())),
                                    preferred_element_type=F32)
                delta = n * tq - ws
                which = jnp.where(delta == 0, 0, jnp.where(delta == ATTN_RADIUS, 1, 2))
                bias = bias_s[which, pl.ds(0, tq), pl.ds(0, W)]
                scores.append(s + jnp.concatenate([bias, bias], axis=0))
                q0s.append(q0)
                k0s.append(k0)
            for u in range(group):
                s = scores[u]
                m = jnp.max(s, axis=1, keepdims=True)
                e = jnp.exp(s - m)
                probs.append((m, e.astype(BF16)))
                sums.append(jnp.sum(e, axis=1, keepdims=True))
            for u in range(group):
                outs.append(jnp.dot(probs[u][1], vs[pl.ds(k0s[u], W), :], preferred_element_type=F32))
            for u in range(group):
                m, l = probs[u][0], sums[u]
                rows = pl.ds(q0s[u], tq)
                pm[rows, :] = jnp.where(head0, m[:tq], m[tq:])
                plr[rows, :] = jnp.where(head0, l[:tq], l[tq:])
                pa[rows, :] = jnp.where(head0, outs[u][:tq], outs[u][tq:])
            return carry

        lax.fori_loop(0, n_blocks // group, block_group, 0)

        for r in range(d):
            rows = pl.ds(r, L, stride=d) if d > 1 else pl.ds(0, L)
            nm[p, rows, :] = pm[r * L:(r + 1) * L, :]
            nl[p, rows, :] = plr[r * L:(r + 1) * L, :]
            na[p, rows, :] = pa[r * L:(r + 1) * L, :]

    m_all = jnp.maximum(jnp.maximum(nm[0], nm[1]), nm[2])
    num = jnp.zeros((S, LANES), F32)
    den = jnp.zeros((S, LANES), F32)
    for p in range(len(DILATIONS)):
        w = jnp.exp(nm[p] - m_all)
        num = num + w * na[p]
        den = den + w * nl[p]
    o_ref[...] = num / den


def _dilated_attention(proj):
    B, S, _ = proj.shape
    n_pairs = ATTN_WIDTH // LANES
    assert S % (16 * DILATIONS[-1]) == 0
    blk = lambda off: pl.BlockSpec((None, S, LANES), lambda b, h: (b, 0, off + h))
    return pl.pallas_call(
        functools.partial(_attn_kernel, S=S),
        grid=(B, n_pairs),
        in_specs=[blk(0), blk(n_pairs), blk(2 * n_pairs)],
        out_specs=pl.BlockSpec((None, S, LANES), lambda b, h: (b, 0, h)),
        out_shape=jax.ShapeDtypeStruct((B, S, ATTN_WIDTH), F32),
        scratch_shapes=[pltpu.VMEM((S, LANES), BF16)] * 3
                       + [pltpu.VMEM((S, LANES), F32)] * 6
                       + [pltpu.VMEM((len(DILATIONS), S, LANES), F32)] * 3
                       + [pltpu.VMEM((3, min(128, S), min(256, S)), F32)],
        compiler_params=_params(("arbitrary", "arbitrary")),
        name="dilated_attention",
    )(proj, proj, proj)


REC_GROUP_CHUNKS = 4
REC_GROUP_ROWS = REC_GROUP_CHUNKS * REC_CHUNK
REC_STAGE_GROUPS = 4


def _hgrn_kernel(rq_ref, zf_ref, zb_ref, ri_ref, rg_ref, lb_ref, gain_ref, o_ref,
                 q_s, vt_s, oi_s, qp_s, ut_s, dd_s, st_s, os_s, *, S):
    C, G, GR = REC_CHUNK, REC_GROUP_CHUNKS, REC_GROUP_ROWS
    nc, ng = S // C, S // GR
    sg = min(REC_STAGE_GROUPS, ng)
    assert ng % sg == 0

    row = lax.broadcasted_iota(I32, (GR, GR), 0)
    col = lax.broadcasted_iota(I32, (GR, GR), 1)
    same_chunk = (row // C) == (col // C)
    t_i = lax.broadcasted_iota(I32, (C, C), 0)
    s_i = lax.broadcasted_iota(I32, (C, C), 1)

    def prep(i, carry):
        r0 = pl.multiple_of(i * GR, GR)
        rq = rq_ref[pl.ds(r0, GR), :]
        q_s[pl.ds(r0, GR), :] = rq * _sigmoid(rq)
        v = ri_ref[pl.ds(r0, GR), :]
        for c in range(G):
            vt_s[i * G + c] = v[c * C:(c + 1) * C, :].T.astype(BF16)
        return carry

    lax.fori_loop(0, ng, prep, 0)

    for direction in range(2):
        fwd = direction == 0
        z_ref = zf_ref if fwd else zb_ref
        cum = jnp.where(same_chunk & ((col <= row) if fwd else (col >= row)), 1.0, 0.0).astype(BF16)
        keep = (s_i <= t_i) if fwd else (s_i >= t_i)
        mid_row = C // 2 - 1 if fwd else C // 2
        last_row = C - 1 if fwd else 0

        def phase_a(it, carry, fwd=fwd, z_ref=z_ref, cum=cum, keep=keep,
                    mid_row=mid_row, last_row=last_row, direction=direction):
            lb = lb_ref[direction:direction + 1, :]
            log_lb = jnp.log(lb)
            log_1m_lb = jnp.log1p(-lb)
            r0s, c0s, k3s, bcs, atts, stage3 = [], [], [], [], [], []
            for u in range(sg):
                i = it * sg + u
                r0 = pl.multiple_of(i * GR, GR)
                z = z_ref[pl.ds(r0, GR), :]
                e = jnp.exp(-jnp.abs(z))
                t = log_1m_lb + (jnp.minimum(z, 0.0) - jnp.log(1.0 + e))
                g = jnp.maximum(log_lb, t) + jnp.log(1.0 + jnp.exp(-jnp.abs(log_lb - t)))
                kk = (1.0 - lb) * (jnp.where(z > 0, e, 1.0) / (1.0 + e))
                g1 = g.astype(BF16)
                r1 = g - g1.astype(F32)
                g2 = r1.astype(BF16)
                g3 = (r1 - g2.astype(F32)).astype(BF16)
                bcs.append(jnp.dot(cum, g1, preferred_element_type=F32)
                           + jnp.dot(cum, g2, preferred_element_type=F32)
                           + jnp.dot(cum, g3, preferred_element_type=F32))
                k3s.append(kk.reshape(G, C, REC_DIM))
                r0s.append(r0)
                c0s.append(pl.multiple_of(i * G, G))
            for u in range(sg):
                b3 = bcs[u].reshape(G, C, REC_DIM)
                mid = b3[:, mid_row:mid_row + 1, :]
                last = b3[:, last_row:last_row + 1, :]
                e3 = b3 - mid
                q3 = q_s[pl.ds(r0s[u], GR), :].reshape(G, C, REC_DIM)
                qt = (q3 * jnp.exp(jnp.minimum(e3, DECAY_EXP_CLAMP))).astype(BF16)
                kt = (k3s[u] * jnp.exp(jnp.minimum(-e3, DECAY_EXP_CLAMP))).astype(BF16)
                atts.append(jnp.einsum('gtc,gsc->gts', qt, kt, preferred_element_type=F32))
                qp_s[pl.ds(r0s[u], GR), :] = (q3 * jnp.exp(b3)).astype(BF16).reshape(GR, REC_DIM)
                stage3.append(((k3s[u] * jnp.exp(last - b3)).astype(BF16), jnp.exp(last)))
            for u in range(sg):
                att = jnp.where(keep[None], atts[u], 0.0).astype(BF16)
                v3 = ri_ref[pl.ds(r0s[u], GR), :].reshape(G, C, REC_DIM).astype(BF16)
                oi = jnp.einsum('gts,gsv->gtv', att, v3, preferred_element_type=F32)
                oi_s[pl.ds(r0s[u], GR), :] = oi.reshape(GR, REC_DIM)
                kp, decay = stage3[u]
                ut_s[pl.ds(c0s[u], G)] = jnp.einsum('gvs,gsc->gvc', vt_s[pl.ds(c0s[u], G)], kp,
                                                    preferred_element_type=F32)
                dd_s[pl.ds(c0s[u], G)] = decay
            return carry

        lax.fori_loop(0, ng // sg, phase_a, 0)

        def scan(c, st, fwd=fwd):
            idx = c if fwd else nc - 1 - c
            st_s[idx] = st.astype(BF16)
            return dd_s[idx] * st + ut_s[idx]

        lax.fori_loop(0, nc, scan, jnp.zeros((REC_DIM, REC_DIM), F32))

        def phase_c(i, carry, fwd=fwd):
            r0 = pl.multiple_of(i * GR, GR)
            c0 = pl.multiple_of(i * G, G)
            qp3 = qp_s[pl.ds(r0, GR), :].reshape(G, C, REC_DIM)
            oo = jnp.einsum('gtc,gvc->gtv', qp3, st_s[pl.ds(c0, G)], preferred_element_type=F32)
            tot = oi_s[pl.ds(r0, GR), :] + oo.reshape(GR, REC_DIM)
            if fwd:
                os_s[pl.ds(r0, GR), :] = tot
            else:
                os_s[pl.ds(r0, GR), :] = os_s[pl.ds(r0, GR), :] + tot
            return carry

        lax.fori_loop(0, ng, phase_c, 0, unroll=2)

    o = os_s[...]
    o = o * lax.rsqrt(jnp.mean(o * o, axis=-1, keepdims=True) + RMS_EPS) * gain_ref[...]
    o_ref[...] = o * _sigmoid(rg_ref[...])


def _hgrn2(proj, lb, rec_gain):
    B, S, _ = proj.shape
    nh = REC_WIDTH // REC_DIM
    assert S % REC_GROUP_ROWS == 0
    col0 = 3 * ATTN_WIDTH // LANES
    blk = lambda k: pl.BlockSpec((None, S, REC_DIM), lambda b, h: (b, 0, col0 + k * nh + h))
    nc = S // REC_CHUNK
    return pl.pallas_call(
        functools.partial(_hgrn_kernel, S=S),
        grid=(B, nh),
        in_specs=[blk(0), blk(1), blk(2), blk(3), blk(4),
                  pl.BlockSpec((2, REC_DIM), lambda b, h: (0, h)),
                  pl.BlockSpec((1, REC_DIM), lambda b, h: (0, h))],
        out_specs=pl.BlockSpec((None, S, REC_DIM), lambda b, h: (b, 0, h)),
        out_shape=jax.ShapeDtypeStruct((B, S, REC_WIDTH), F32),
        scratch_shapes=[pltpu.VMEM((S, REC_DIM), F32),
                        pltpu.VMEM((nc, REC_DIM, REC_CHUNK), BF16),
                        pltpu.VMEM((S, REC_DIM), F32),
                        pltpu.VMEM((S, REC_DIM), BF16),
                        pltpu.VMEM((nc, REC_DIM, REC_DIM), F32),
                        pltpu.VMEM((nc, 1, REC_DIM), F32),
                        pltpu.VMEM((nc, REC_DIM, REC_DIM), BF16),
                        pltpu.VMEM((S, REC_DIM), F32)],
        compiler_params=_params(("arbitrary", "arbitrary")),
        name="hgrn2",
    )(proj, proj, proj, proj, proj, lb, rec_gain)


def _outproj_kernel(a_ref, r_ref, x_ref, gate_ref, ag_ref, w_ref, lg_ref, lb_ref, o_ref, *, alpha):
    a = a_ref[...]
    an = a * lax.rsqrt(jnp.mean(a * a, axis=-1, keepdims=True) + RMS_EPS) * ag_ref[...]
    y = (jnp.dot(an.astype(BF16), w_ref[:ATTN_WIDTH, :], preferred_element_type=F32)
         + jnp.dot(r_ref[...].astype(BF16), w_ref[ATTN_WIDTH:, :], preferred_element_type=F32))
    z = alpha * x_ref[...] + (1.0 + gate_ref[...]) * y
    o_ref[...] = _layer_norm(z, lg_ref[...], lb_ref[...])


def _out_projection(attn, rec, x, gate, attn_gain, w_out_bf16, ln_g, ln_b, *, alpha, tm):
    B, S, D = x.shape
    row = lambda w: pl.BlockSpec((None, tm, w), lambda b, i: (b, i, 0))
    vec = lambda w: pl.BlockSpec((1, w), lambda b, i: (0, 0))
    return pl.pallas_call(
        functools.partial(_outproj_kernel, alpha=alpha),
        grid=(B, S // tm),
        in_specs=[row(ATTN_WIDTH), row(REC_WIDTH), row(D),
                  pl.BlockSpec((None, 1, D), lambda b, i: (b, 0, 0)),
                  vec(ATTN_WIDTH),
                  pl.BlockSpec((ATTN_WIDTH + REC_WIDTH, D), lambda b, i: (0, 0)),
                  vec(D), vec(D)],
        out_specs=row(D),
        out_shape=jax.ShapeDtypeStruct((B, S, D), F32),
        compiler_params=_params(("arbitrary", "arbitrary")),
        name="out_projection",
    )(attn, rec, x, gate, attn_gain, w_out_bf16, ln_g, ln_b)


def _ffn_kernel(x_ref, sc_ref, sh_ref, gate_ref, wg_ref, wu_ref, wd_ref, lg_ref, lb_ref, o_ref,
                *, alpha, tc):
    x = x_ref[...]
    u = (x * (1.0 + sc_ref[...]) + sh_ref[...]).astype(BF16)
    acc = jnp.zeros(x.shape, F32)
    for c in range(wg_ref.shape[1] // tc):
        cols = slice(c * tc, (c + 1) * tc)
        g = jnp.dot(u, wg_ref[:, cols], preferred_element_type=F32)
        up = jnp.dot(u, wu_ref[:, cols], preferred_element_type=F32)
        h = (g * _sigmoid(g) * up).astype(BF16)
        acc = acc + jnp.dot(h, wd_ref[cols, :], preferred_element_type=F32)
    z = alpha * x + (1.0 + gate_ref[...]) * acc
    o_ref[...] = _layer_norm(z, lg_ref[...], lb_ref[...])


def _dense_ffn(x, scale, shift, gate, wg, wu, wd, ln_g, ln_b, *, alpha, tm, tc):
    B, S, D = x.shape
    ffp = wg.shape[1]
    assert ffp % tc == 0
    row = pl.BlockSpec((None, tm, D), lambda b, i: (b, i, 0))
    mod = pl.BlockSpec((None, 1, D), lambda b, i: (b, 0, 0))
    vec = pl.BlockSpec((1, D), lambda b, i: (0, 0))
    return pl.pallas_call(
        functools.partial(_ffn_kernel, alpha=alpha, tc=tc),
        grid=(B, S // tm),
        in_specs=[row, mod, mod, mod,
                  pl.BlockSpec((D, ffp), lambda b, i: (0, 0)),
                  pl.BlockSpec((D, ffp), lambda b, i: (0, 0)),
                  pl.BlockSpec((ffp, D), lambda b, i: (0, 0)),
                  vec, vec],
        out_specs=row,
        out_shape=jax.ShapeDtypeStruct((B, S, D), F32),
        compiler_params=_params(("arbitrary", "arbitrary")),
        name="dense_ffn",
    )(x, scale, shift, gate, wg, wu, wd, ln_g, ln_b)


META_E0, META_E1, META_W0, META_W1, META_R0, META_R1 = range(6)
SEGMENT_ALIGN = 16
DISPATCH_ROWS = 256
DISPATCH_HALF = DISPATCH_ROWS // 2
DISPATCH_SLACK = DISPATCH_HALF


def _router_kernel(x_ref, sc_ref, sh_ref, wr_ref, u_ref, meta_ref, cnt_ref, carry_ref, *, n_exp):
    tm = x_ref.shape[0]

    @pl.when((pl.program_id(0) == 0) & (pl.program_id(1) == 0))
    def _():
        carry_ref[...] = jnp.zeros_like(carry_ref)

    u = x_ref[...] * (1.0 + sc_ref[...]) + sh_ref[...]
    u_ref[...] = u.astype(BF16)
    u_hi = u.astype(BF16)
    u_lo = (u - u_hi.astype(F32)).astype(BF16)
    logits = (jnp.dot(u_hi, wr_ref[0], preferred_element_type=F32)
              + jnp.dot(u_lo, wr_ref[0], preferred_element_type=F32)
              + jnp.dot(u_hi, wr_ref[1], preferred_element_type=F32))
    lane = lax.broadcasted_iota(I32, (tm, LANES), 1)
    neg = -jnp.inf
    l1 = jnp.where(lane < n_exp, logits, neg)
    m1 = jnp.max(l1, axis=1, keepdims=True)
    i1 = jnp.min(jnp.where(l1 == m1, lane, LANES), axis=1, keepdims=True)
    l2 = jnp.where(lane == i1, neg, l1)
    m2 = jnp.max(l2, axis=1, keepdims=True)
    i2 = jnp.min(jnp.where(l2 == m2, lane, LANES), axis=1, keepdims=True)
    e = jnp.exp(m2 - m1)
    w1 = 1.0 / (1.0 + e)
    w2 = e * w1
    sel = jnp.where((lane == i1) | (lane == i2), 1.0, 0.0)
    before = (lax.broadcasted_iota(I32, (tm, tm), 1) < lax.broadcasted_iota(I32, (tm, tm), 0))
    ranks = jnp.dot(jnp.where(before, 1.0, 0.0).astype(BF16), sel.astype(BF16),
                    preferred_element_type=F32) + carry_ref[...]
    r1 = jnp.sum(jnp.where(lane == i1, ranks, 0.0), axis=1, keepdims=True)
    r2 = jnp.sum(jnp.where(lane == i2, ranks, 0.0), axis=1, keepdims=True)
    carry_ref[...] = carry_ref[...] + jnp.sum(sel, axis=0, keepdims=True)
    cnt_ref[...] = carry_ref[...]
    meta = jnp.zeros((tm, LANES), F32)
    for k, val in ((META_E0, i1.astype(F32)), (META_E1, i2.astype(F32)), (META_W0, w1),
                   (META_W1, w2), (META_R0, r1), (META_R1, r2)):
        meta = jnp.where(lane == k, val, meta)
    meta_ref[...] = meta


def _router(x, scale, shift, w_router_lanes, *, n_exp, tm):
    B, S, D = x.shape
    nt = S // tm
    row = pl.BlockSpec((None, tm, D), lambda b, i: (b, i, 0))
    mod = pl.BlockSpec((None, 1, D), lambda b, i: (b, 0, 0))
    return pl.pallas_call(
        functools.partial(_router_kernel, n_exp=n_exp),
        grid=(B, nt),
        in_specs=[row, mod, mod, pl.BlockSpec((2, D, LANES), lambda b, i: (0, 0, 0))],
        out_specs=[row,
                   pl.BlockSpec((None, tm, LANES), lambda b, i: (b, i, 0)),
                   pl.BlockSpec((1, LANES), lambda b, i: (0, 0))],
        out_shape=[jax.ShapeDtypeStruct((B, S, D), BF16),
                   jax.ShapeDtypeStruct((B, S, LANES), F32),
                   jax.ShapeDtypeStruct((1, LANES), F32)],
        scratch_shapes=[pltpu.VMEM((1, LANES), F32)],
        compiler_params=_params(("arbitrary", "arbitrary")),
        name="moe_router",
    )(x, scale, shift, w_router_lanes)


def _rows_at(hbm, first_row, n_rows):
    return hbm.at[pl.ds(pl.multiple_of(first_row, SEGMENT_ALIGN), n_rows)]


def _dispatch_kernel(base_ref, cnt_ref, padlo_ref, padn_ref, u_ref, meta_ref, xs_hbm,
                     stage, extra, zeros_buf, sems, extra_sem, *, n_exp):
    t = pl.program_id(0)
    last = pl.num_programs(0) - 1
    tm = u_ref.shape[0]
    R, H = DISPATCH_ROWS, DISPATCH_HALF
    slot = t % 2
    u = u_ref[...]
    e0, e1 = meta_ref[0:1, :], meta_ref[1:2, :]
    d0, d1 = meta_ref[2:3, :], meta_ref[3:4, :]
    srow = lax.broadcasted_iota(I32, (R, tm), 0)

    def main_copies(step, buf, fn):
        for e in range(n_exp):
            base = base_ref[step * n_exp + e]
            fn(pltpu.make_async_copy(stage.at[buf, pl.ds(e * R, H)], _rows_at(xs_hbm, base, H),
                                     sems.at[buf]))

            @pl.when(cnt_ref[step * n_exp + e] > H)
            def _(e=e, base=base):
                fn(pltpu.make_async_copy(stage.at[buf, pl.ds(e * R + H, H)],
                                         _rows_at(xs_hbm, base + H, H), sems.at[buf]))

    rels = []
    for e in range(n_exp):
        rel = jnp.where(e0 == e, d0, jnp.where(e1 == e, d1, -1))
        rels.append(jnp.where(rel >= 0, rel - base_ref[t * n_exp + e], -1))
    onehot = jnp.concatenate([jnp.where(srow == rel, 1.0, 0.0) for rel in rels], axis=0).astype(BF16)
    stage[slot] = jnp.dot(onehot, u, preferred_element_type=F32).astype(BF16)

    @pl.when(t > 0)
    def _():
        main_copies(t - 1, 1 - slot, lambda cp: cp.wait())
    main_copies(t, slot, lambda cp: cp.start())

    for e in range(n_exp):
        for blk in range(1, tm // R):
            @pl.when(cnt_ref[t * n_exp + e] > blk * R)
            def _(e=e, blk=blk):
                more = jnp.where(srow + blk * R == rels[e], 1.0, 0.0).astype(BF16)
                extra[...] = jnp.dot(more, u, preferred_element_type=F32).astype(BF16)
                cp = pltpu.make_async_copy(extra, _rows_at(xs_hbm, base_ref[t * n_exp + e] + blk * R, R),
                                           extra_sem)
                cp.start()
                cp.wait()

    @pl.when(t == last)
    def _():
        main_copies(t, slot, lambda cp: cp.wait())
        zeros_buf[...] = jnp.zeros_like(zeros_buf)
        pieces = [SEGMENT_ALIGN << b for b in range((R // SEGMENT_ALIGN).bit_length() - 1)]

        def pad_copies(fn):
            for p in range(padlo_ref.shape[0]):
                lo, n = padlo_ref[p], padn_ref[p]
                whole = n // R

                def body(k, carry, lo=lo):
                    fn(pltpu.make_async_copy(zeros_buf, _rows_at(xs_hbm, lo + k * R, R), extra_sem))
                    return carry
                lax.fori_loop(0, whole, body, 0)
                rem = n - whole * R
                for sz in pieces:
                    off = lo + whole * R + (rem // (2 * sz)) * (2 * sz)
                    pl.when((rem // sz) % 2 == 1)(functools.partial(
                        lambda off, sz: fn(pltpu.make_async_copy(
                            zeros_buf.at[pl.ds(0, sz)], _rows_at(xs_hbm, off, sz), extra_sem)), off, sz))

        pad_copies(lambda cp: cp.start())
        pad_copies(lambda cp: cp.wait())


def _dispatch(u, meta_t, base_te, cnt_te, pad_lo, pad_n, *, n_rows, tm):
    T, D = u.shape
    n_exp = pad_lo.shape[0] - 1
    assert tm % DISPATCH_ROWS == 0
    grid_spec = pltpu.PrefetchScalarGridSpec(
        num_scalar_prefetch=4,
        grid=(T // tm,),
        in_specs=[pl.BlockSpec((tm, D), lambda t, *_: (t, 0)),
                  pl.BlockSpec((meta_t.shape[0], tm), lambda t, *_: (0, t))],
        out_specs=pl.BlockSpec(memory_space=pl.ANY),
        scratch_shapes=[pltpu.VMEM((2, n_exp * DISPATCH_ROWS, D), BF16),
                        pltpu.VMEM((DISPATCH_ROWS, D), BF16),
                        pltpu.VMEM((DISPATCH_ROWS, D), BF16),
                        pltpu.SemaphoreType.DMA((2,)),
                        pltpu.SemaphoreType.DMA(())],
    )
    return pl.pallas_call(
        functools.partial(_dispatch_kernel, n_exp=n_exp),
        grid_spec=grid_spec,
        out_shape=jax.ShapeDtypeStruct((n_rows, D), BF16),
        compiler_params=_params(("arbitrary",)),
        name="moe_dispatch",
    )(base_te, cnt_te, pad_lo, pad_n, u, meta_t)


def _expert_kernel(te_ref, na_ref, xs_ref, wg_ref, wu_ref, wd_ref, o_ref):
    i = pl.program_id(0)
    j = pl.program_id(1)
    active = i < na_ref[0]

    @pl.when(jnp.logical_not(active) | (j == 0))
    def _():
        o_ref[...] = jnp.zeros_like(o_ref)

    @pl.when(active)
    def _():
        xb = xs_ref[...]
        g = jnp.dot(xb, wg_ref[...].astype(BF16), preferred_element_type=F32)
        up = jnp.dot(xb, wu_ref[...].astype(BF16), preferred_element_type=F32)
        h = (g * _sigmoid(g) * up).astype(BF16)
        o_ref[...] += jnp.dot(h, wd_ref[...].astype(BF16), preferred_element_type=F32)


def _expert_ffn(xs, tile_expert, n_active, wg, wu, wd, *, tm, tf):
    n_rows, D = xs.shape
    ff = wg.shape[2]
    assert n_rows % tm == 0 and ff % tf == 0
    nj = ff // tf

    def jj(i, j, na):
        return jnp.where(i < na[0], j, nj - 1)

    def ii(i, na):
        return jnp.minimum(i, na[0] - 1)

    grid_spec = pltpu.PrefetchScalarGridSpec(
        num_scalar_prefetch=2,
        grid=(n_rows // tm, nj),
        in_specs=[pl.BlockSpec((tm, D), lambda i, j, te, na: (ii(i, na), 0)),
                  pl.BlockSpec((None, D, tf), lambda i, j, te, na: (te[i], 0, jj(i, j, na))),
                  pl.BlockSpec((None, D, tf), lambda i, j, te, na: (te[i], 0, jj(i, j, na))),
                  pl.BlockSpec((None, tf, D), lambda i, j, te, na: (te[i], jj(i, j, na), 0))],
        out_specs=pl.BlockSpec((tm, D), lambda i, j, te, na: (i, 0)),
    )
    return pl.pallas_call(
        _expert_kernel,
        grid_spec=grid_spec,
        out_shape=jax.ShapeDtypeStruct((n_rows, D), F32),
        compiler_params=_params(("arbitrary", "arbitrary")),
        name="moe_experts",
    )(tile_expert, n_active, xs, wg, wu, wd)


def _combine_kernel(base_ref, off_ref, cnt_ref, ys_hbm, meta_ref, x_ref, gate_ref, lg_ref, lb_ref, o_ref,
                    stage, extra, acc_ref, sems, extra_sem, *, alpha, n_exp):
    t = pl.program_id(0) * pl.num_programs(1) + pl.program_id(1)
    n_steps = pl.num_programs(0) * pl.num_programs(1)
    tm = x_ref.shape[0]
    R = DISPATCH_ROWS
    slot = t % 2

    def main_copy(step, e, buf):
        return pltpu.make_async_copy(_rows_at(ys_hbm, base_ref[step * n_exp + e], R),
                                     stage.at[buf, e], sems.at[buf, e])

    @pl.when(t == 0)
    def _():
        for e in range(n_exp):
            main_copy(0, e, 0).start()

    @pl.when(t + 1 < n_steps)
    def _():
        for e in range(n_exp):
            main_copy(t + 1, e, 1 - slot).start()

    meta = meta_ref[...]
    col = lambda k: meta[:, k:k + 1]
    e0, e1 = col(META_E0).astype(I32), col(META_E1).astype(I32)
    r0, r1 = col(META_R0).astype(I32), col(META_R1).astype(I32)
    w0, w1 = col(META_W0), col(META_W1)
    lane = lax.broadcasted_iota(I32, (tm, R), 1)

    acc_ref[...] = jnp.zeros_like(acc_ref)
    for e in range(n_exp):
        rel = jnp.where(e0 == e, r0, jnp.where(e1 == e, r1, -1))
        rel = jnp.where(rel >= 0, rel - off_ref[t * n_exp + e], -1)
        w = jnp.where(e0 == e, w0, jnp.where(e1 == e, w1, 0.0))
        main_copy(t, e, slot).wait()
        onehot = jnp.where(lane == rel, 1.0, 0.0).astype(BF16)
        acc_ref[...] += w * jnp.dot(onehot, stage[slot, e].astype(BF16), preferred_element_type=F32)
        for blk in range(1, tm // R):
            @pl.when(cnt_ref[t * n_exp + e] > blk * R)
            def _(e=e, blk=blk, rel=rel, w=w):
                cp = pltpu.make_async_copy(_rows_at(ys_hbm, base_ref[t * n_exp + e] + blk * R, R),
                                           extra, extra_sem)
                cp.start()
                cp.wait()
                more = jnp.where(lane + blk * R == rel, 1.0, 0.0).astype(BF16)
                acc_ref[...] += w * jnp.dot(more, extra[...].astype(BF16), preferred_element_type=F32)

    z = alpha * x_ref[...] + (1.0 + gate_ref[...]) * acc_ref[...]
    o_ref[...] = _layer_norm(z, lg_ref[...], lb_ref[...])


def _combine(ys, base_te, off_te, cnt_te, meta, x, gate, ln_g, ln_b, *, alpha, tm):
    B, S, D = x.shape
    nt = S // tm
    n_exp = base_te.shape[0] // (B * nt)
    row = pl.BlockSpec((None, tm, D), lambda b, i, *_: (b, i, 0))
    vec = pl.BlockSpec((1, D), lambda b, i, *_: (0, 0))
    grid_spec = pltpu.PrefetchScalarGridSpec(
        num_scalar_prefetch=3,
        grid=(B, nt),
        in_specs=[pl.BlockSpec(memory_space=pl.ANY),
                  pl.BlockSpec((None, tm, LANES), lambda b, i, *_: (b, i, 0)),
                  row, pl.BlockSpec((None, 1, D), lambda b, i, *_: (b, 0, 0)), vec, vec],
        out_specs=row,
        scratch_shapes=[pltpu.VMEM((2, n_exp, DISPATCH_ROWS, D), F32),
                        pltpu.VMEM((DISPATCH_ROWS, D), F32),
                        pltpu.VMEM((tm, D), F32),
                        pltpu.SemaphoreType.DMA((2, n_exp)),
                        pltpu.SemaphoreType.DMA(())],
    )
    return pl.pallas_call(
        functools.partial(_combine_kernel, alpha=alpha, n_exp=n_exp),
        grid_spec=grid_spec,
        out_shape=jax.ShapeDtypeStruct((B, S, D), F32),
        compiler_params=_params(("arbitrary", "arbitrary")),
        name="moe_combine",
    )(base_te, off_te, cnt_te, ys, meta, x, gate, ln_g, ln_b)


def _moe_ffn(x, scale, shift, gate, w_router, wg, wu, wd, ln_g, ln_b, *, alpha, tm, tm_e, tf):
    B, S, D = x.shape
    T = B * S
    nt = T // tm
    n_exp = w_router.shape[1]
    wr = jnp.zeros((D, LANES), F32).at[:, :n_exp].set(w_router)
    wr_hi = wr.astype(BF16)
    wr = jnp.stack([wr_hi, (wr - wr_hi.astype(F32)).astype(BF16)])
    u, meta, counts = _router(x, scale, shift, wr, n_exp=n_exp, tm=tm)

    meta2 = meta.reshape(T, LANES)
    e0 = meta2[:, META_E0].astype(I32)
    e1 = meta2[:, META_E1].astype(I32)
    r0 = meta2[:, META_R0].astype(I32)
    r1 = meta2[:, META_R1].astype(I32)
    experts = jnp.arange(n_exp, dtype=I32)
    chosen = ((e0[:, None] == experts) | (e1[:, None] == experts)).astype(I32)
    cnt_te = chosen.reshape(nt, tm, n_exp).sum(axis=1)
    seg_te = -(-cnt_te // SEGMENT_ALIGN) * SEGMENT_ALIGN
    off_te = jnp.cumsum(cnt_te, axis=0) - cnt_te
    filled = seg_te.sum(axis=0)
    sizes = -(-(filled + DISPATCH_SLACK) // tm_e) * tm_e
    ends = jnp.cumsum(sizes)
    starts = ends - sizes
    base_te = starts[None, :] + jnp.cumsum(seg_te, axis=0) - seg_te
    tile_of = jnp.arange(T, dtype=I32) // tm
    d0 = base_te[tile_of, e0] + r0 - off_te[tile_of, e0]
    d1 = base_te[tile_of, e1] + r1 - off_te[tile_of, e1]
    bound = TOP_K * T + n_exp * ((SEGMENT_ALIGN - 1) * nt + DISPATCH_SLACK)
    n_tiles = -(-bound // tm_e) + n_exp + 1
    n_active = (ends[-1] // tm_e).astype(I32)
    tile_id = jnp.minimum(jnp.arange(n_tiles, dtype=I32), n_active - 1)
    tile_expert = jnp.sum((ends[None, :] <= (tile_id * tm_e)[:, None]).astype(I32), axis=1)
    meta_t = jnp.stack([e0, e1, d0, d1] + [jnp.zeros_like(e0)] * 4)

    xs = _dispatch(u.reshape(T, D), meta_t, base_te.reshape(-1), cnt_te.reshape(-1),
                   jnp.concatenate([starts + filled, ends[-1:]]),
                   jnp.concatenate([sizes - filled, n_tiles * tm_e - ends[-1:]]),
                   n_rows=n_tiles * tm_e, tm=tm)
    ys = _expert_ffn(xs, tile_expert, n_active.reshape(1), wg, wu, wd, tm=tm_e, tf=tf)
    return _combine(ys, base_te.reshape(-1), off_te.reshape(-1), cnt_te.reshape(-1),
                    meta, x, gate, ln_g, ln_b, alpha=alpha, tm=tm)


def kernel(x, c, positions, w_in, w_out, attn_norm_gain, rec_norm_gain, rec_lb_logits, ada_w, ada_b,
           ln_gain, ln_bias, ffn_w_gate, ffn_w_up, ffn_w_down, moe_router, moe_w_gate, moe_w_up,
           moe_w_down):
    B, S, D = x.shape
    depth = w_in.shape[0]
    alpha = (2 * depth) ** 0.25

    p = jax.nn.softmax(rec_lb_logits.astype(F32), axis=0)
    cum = jnp.cumsum(p, axis=0)
    lb_all = cum - cum[0:1]
    half = ATTN_HEAD_DIM // 2
    inv_freq = ROPE_THETA ** (-jnp.arange(half, dtype=F32) / half)
    inv_freq_lanes = jnp.tile(inv_freq, LANES // half).reshape(1, LANES)
    pos_f = positions.astype(F32).reshape(B, S, 1)
    w_in_b = w_in.astype(BF16)
    w_out_b = w_out.astype(BF16)
    ff = ffn_w_gate.shape[2]
    ffp = -(-ff // (2 * LANES)) * (2 * LANES)
    pad_c = lambda w: jnp.pad(w.astype(BF16), ((0, 0), (0, 0), (0, ffp - ff)))
    ffn_g, ffn_u = pad_c(ffn_w_gate), pad_c(ffn_w_up)
    ffn_d = jnp.pad(ffn_w_down.astype(BF16), ((0, 0), (0, ffp - ff), (0, 0)))

    mods = _ada_modulation(c, ada_w, ada_b)
    rope_cos, rope_sin = _rope_tables(pos_f, inv_freq_lanes, tm=min(1024, S))

    def mod(layer, sub):
        m = mods[layer * 2 + sub].reshape(B, 1, 3 * D)
        return m[..., :D], m[..., D:2 * D], m[..., 2 * D:]

    vec = lambda a: a.reshape(1, -1)
    for layer in range(depth):
        shift, scale, gate = mod(layer, 0)
        proj = _in_projection(x, scale, shift, rope_cos, rope_sin, w_in_b[layer], tm=min(1024, S))
        attn = _dilated_attention(proj)
        rec = _hgrn2(proj, lb_all[layer], vec(rec_norm_gain[layer]))
        x = _out_projection(attn, rec, x, gate, vec(attn_norm_gain[layer]), w_out_b[layer],
                            vec(ln_gain[layer, 0]), vec(ln_bias[layer, 0]), alpha=alpha, tm=min(512, S))
        shift, scale, gate = mod(layer, 1)
        j = layer // 2
        if layer % 2 == 0:
            x = _dense_ffn(x, scale, shift, gate, ffn_g[j], ffn_u[j], ffn_d[j],
                           vec(ln_gain[layer, 1]), vec(ln_bias[layer, 1]),
                           alpha=alpha, tm=min(512, S), tc=2 * LANES)
        else:
            x = _moe_ffn(x, scale, shift, gate, moe_router[j], moe_w_gate[j], moe_w_up[j],
                         moe_w_down[j], vec(ln_gain[layer, 1]), vec(ln_bias[layer, 1]),
                         alpha=alpha, tm=min(512, S), tm_e=min(1024, B * S), tf=512)
    return x
```

```python
import functools

import jax
import jax.numpy as jnp
from jax import lax
from jax.experimental import pallas as pl
from jax.experimental.pallas import tpu as pltpu

F32 = jnp.float32
BF16 = jnp.bfloat16
I32 = jnp.int32

LANES = 128
V7X_VMEM_LIMIT_BYTES = 56 * 1024 * 1024

ATTN_WIDTH = 512
ATTN_HEAD_DIM = 64
DILATIONS = (1, 4, 16)
ATTN_RADIUS = 64
ROPE_THETA = 10000.0
REC_WIDTH = 512
REC_DIM = 128
REC_CHUNK = 64
TOP_K = 2
LN_EPS = 1e-5
RMS_EPS = 1e-6
MASK_VALUE = -1e30
DECAY_EXP_CLAMP = 80.0


def _params(semantics):
    return pltpu.CompilerParams(dimension_semantics=semantics,
                                vmem_limit_bytes=V7X_VMEM_LIMIT_BYTES)


def _sigmoid(x):
    return 1.0 / (1.0 + jnp.exp(-x))


def _layer_norm(z, gain, bias):
    mu = jnp.mean(z, axis=-1, keepdims=True)
    zc = z - mu
    var = jnp.mean(zc * zc, axis=-1, keepdims=True)
    return zc * lax.rsqrt(var + LN_EPS) * gain + bias


def _ada_kernel(c_ref, w_ref, b_ref, o_ref):
    c = c_ref[...]
    o_ref[...] = jnp.dot(c * _sigmoid(c), w_ref[...], precision=lax.Precision.HIGHEST,
                         preferred_element_type=F32) + b_ref[...]


def _ada_modulation(c, ada_w, ada_b):
    B, D = c.shape
    n = ada_w.shape[0] * ada_w.shape[1]
    n3 = ada_w.shape[-1]
    tn = 1024
    return pl.pallas_call(
        _ada_kernel,
        grid=(n, n3 // tn),
        in_specs=[pl.BlockSpec((B, D), lambda i, j: (0, 0)),
                  pl.BlockSpec((None, D, tn), lambda i, j: (i, 0, j)),
                  pl.BlockSpec((None, 1, tn), lambda i, j: (i, 0, j))],
        out_specs=pl.BlockSpec((None, B, tn), lambda i, j: (i, 0, j)),
        out_shape=jax.ShapeDtypeStruct((n, B, n3), F32),
        compiler_params=_params(("arbitrary", "arbitrary")),
        name="ada_modulation",
    )(c, ada_w.reshape(n, D, n3), ada_b.reshape(n, 1, n3))


def _rope_kernel(pos_ref, invf_ref, cos_ref, sin_ref):
    ang = pos_ref[...] * invf_ref[...]
    lane = lax.broadcasted_iota(I32, (1, LANES), 1)
    first_half = (lane % ATTN_HEAD_DIM) < (ATTN_HEAD_DIM // 2)
    sin = jnp.sin(ang)
    cos_ref[...] = jnp.cos(ang)
    sin_ref[...] = jnp.where(first_half, -sin, sin)


def _rope_tables(pos_f, inv_freq_lanes, *, tm):
    B, S, _ = pos_f.shape
    tab = pl.BlockSpec((None, tm, LANES), lambda b, i: (b, i, 0))
    return pl.pallas_call(
        _rope_kernel,
        grid=(B, S // tm),
        in_specs=[pl.BlockSpec((None, tm, 1), lambda b, i: (b, i, 0)),
                  pl.BlockSpec((1, LANES), lambda b, i: (0, 0))],
        out_specs=[tab, tab],
        out_shape=[jax.ShapeDtypeStruct((B, S, LANES), F32)] * 2,
        compiler_params=_params(("arbitrary", "arbitrary")),
        name="rope_tables",
    )(pos_f, inv_freq_lanes)


def _inproj_kernel(x_ref, sc_ref, sh_ref, cos_ref, sin_ref, w_ref, o_ref, u_ref, *, tn):
    j = pl.program_id(2)

    @pl.when(j == 0)
    def _():
        u_ref[...] = (x_ref[...] * (1.0 + sc_ref[...]) + sh_ref[...]).astype(BF16)

    cols = pl.ds(pl.multiple_of(j * tn, tn), tn)
    acc = jnp.dot(u_ref[...], w_ref[:, cols], preferred_element_type=F32)

    @pl.when(j > 0)
    def _():
        o_ref[...] = acc

    @pl.when(j == 0)
    def _():
        cos = cos_ref[...]
        sin_signed = sin_ref[...]
        lane = lax.broadcasted_iota(I32, (1, LANES), 1)
        first_half = (lane % ATTN_HEAD_DIM) < (ATTN_HEAD_DIM // 2)
        q_scale = ATTN_HEAD_DIM ** -0.5
        for kk in range(tn // LANES):
            c = acc[:, kk * LANES:(kk + 1) * LANES]
            rot = jnp.where(first_half, pltpu.roll(c, LANES - 32, 1), pltpu.roll(c, 32, 1))
            r = c * cos + rot * sin_signed
            if kk * LANES < ATTN_WIDTH:
                r = r * q_scale
            o_ref[:, kk * LANES:(kk + 1) * LANES] = r


def _in_projection(x, scale, shift, rope_cos, rope_sin, w_in_bf16, *, tm):
    B, S, D = x.shape
    n_cols = w_in_bf16.shape[1]
    tn = 2 * ATTN_WIDTH
    assert S % tm == 0 and n_cols % tn == 0
    return pl.pallas_call(
        functools.partial(_inproj_kernel, tn=tn),
        grid=(B, S // tm, n_cols // tn),
        in_specs=[pl.BlockSpec((None, tm, D), lambda b, i, j: (b, i, 0)),
                  pl.BlockSpec((None, 1, D), lambda b, i, j: (b, 0, 0)),
                  pl.BlockSpec((None, 1, D), lambda b, i, j: (b, 0, 0)),
                  pl.BlockSpec((None, tm, LANES), lambda b, i, j: (b, i, 0)),
                  pl.BlockSpec((None, tm, LANES), lambda b, i, j: (b, i, 0)),
                  pl.BlockSpec((D, n_cols), lambda b, i, j: (0, 0))],
        out_specs=pl.BlockSpec((None, tm, tn), lambda b, i, j: (b, i, j)),
        out_shape=jax.ShapeDtypeStruct((B, S, n_cols), F32),
        scratch_shapes=[pltpu.VMEM((tm, D), BF16)],
        compiler_params=_params(("arbitrary", "arbitrary", "arbitrary")),
        name="in_projection",
    )(x, scale, shift, rope_cos, rope_sin, w_in_bf16)


ATTN_BLOCK_GROUP = 8


def _attn_kernel(q_ref, k_ref, v_ref, o_ref, qs, ks, vs, qf, kf, vf, pm, plr, pa, nm, nl, na, bias_s,
                 *, S):
    lane = lax.broadcasted_iota(I32, (1, LANES), 1)
    head0 = lane < ATTN_HEAD_DIM

    for p, d in enumerate(DILATIONS):
        L = S // d
        tq = min(128, L)
        W = min(2 * tq, L)
        nb = L // tq

        d_prev = DILATIONS[p - 1] if p else 1
        ratio, l_prev = d // d_prev, S // d_prev
        keep_f32 = 0 < p < len(DILATIONS) - 1
        for src, stage, dst in ((q_ref, qf, qs), (k_ref, kf, ks), (v_ref, vf, vs)):
            src = src if p <= 1 else stage
            for r in range(d):
                a, r_prev = divmod(r, d_prev)
                rows = pl.ds(r_prev * l_prev + a, L, stride=ratio) if ratio > 1 else pl.ds(0, L)
                val = src[rows, :]
                if keep_f32:
                    stage[r * L:(r + 1) * L, :] = val
                dst[r * L:(r + 1) * L, :] = val.astype(BF16)

        n_blocks = d * nb
        rc = lax.broadcasted_iota(I32, (tq, W), 0) - lax.broadcasted_iota(I32, (tq, W), 1)
        for which, delta in enumerate((0, ATTN_RADIUS, tq)):
            bias_s[which, pl.ds(0, tq), pl.ds(0, W)] = jnp.where(
                jnp.abs(rc + delta) <= ATTN_RADIUS, 0.0, MASK_VALUE)
        group = min(ATTN_BLOCK_GROUP, n_blocks)
        assert n_blocks % group == 0

        def block_group(gi, carry, L=L, tq=tq, W=W, nb=nb, group=group):
            q0s, k0s, scores, probs, outs, sums = [], [], [], [], [], []
            for u in range(group):
                g = gi * group + u
                r = g // nb
                n = g - r * nb
                base = r * L
                q0 = pl.multiple_of(base + n * tq, tq)
                ws = jnp.clip(n * tq - ATTN_RADIUS, 0, L - W)
                k0 = pl.multiple_of(base + ws, 16)
                qb = qs[pl.ds(q0, tq), :]
                zero = jnp.zeros_like(qb)
                q2 = jnp.concatenate([jnp.where(head0, qb, zero), jnp.where(head0, zero, qb)], axis=0)
                s = lax.dot_general(q2, ks[pl.ds(k0, W), :], (((1,), (1,)), ((), ())),
                                    preferred_element_type=F32)
                delta = n * tq - ws
                which = jnp.where(delta == 0, 0, jnp.where(delta == ATTN_RADIUS, 1, 2))
                bias = bias_s[which, pl.ds(0, tq), pl.ds(0, W)]
                scores.append(s + jnp.concatenate([bias, bias], axis=0))
                q0s.append(q0)
                k0s.append(k0)
            for u in range(group):
                s = scores[u]
                m = jnp.max(s, axis=1, keepdims=True)
                e = jnp.exp(s - m)
                probs.append((m, e.astype(BF16)))
                sums.append(jnp.sum(e, axis=1, keepdims=True))
            for u in range(group):
                outs.append(jnp.dot(probs[u][1], vs[pl.ds(k0s[u], W), :], preferred_element_type=F32))
            for u in range(group):
                m, l = probs[u][0], sums[u]
                rows = pl.ds(q0s[u], tq)
                pm[rows, :] = jnp.where(head0, m[:tq], m[tq:])
                plr[rows, :] = jnp.where(head0, l[:tq], l[tq:])
                pa[rows, :] = jnp.where(head0, outs[u][:tq], outs[u][tq:])
            return carry

        lax.fori_loop(0, n_blocks // group, block_group, 0)

        for r in range(d):
            rows = pl.ds(r, L, stride=d) if d > 1 else pl.ds(0, L)
            nm[p, rows, :] = pm[r * L:(r + 1) * L, :]
            nl[p, rows, :] = plr[r * L:(r + 1) * L, :]
            na[p, rows, :] = pa[r * L:(r + 1) * L, :]

    m_all = jnp.maximum(jnp.maximum(nm[0], nm[1]), nm[2])
    num = jnp.zeros((S, LANES), F32)
    den = jnp.zeros((S, LANES), F32)
    for p in range(len(DILATIONS)):
        w = jnp.exp(nm[p] - m_all)
        num = num + w * na[p]
        den = den + w * nl[p]
    o_ref[...] = num / den


def _dilated_attention(proj):
    B, S, _ = proj.shape
    n_pairs = ATTN_WIDTH // LANES
    assert S % (16 * DILATIONS[-1]) == 0
    blk = lambda off: pl.BlockSpec((None, S, LANES), lambda b, h: (b, 0, off + h))
    return pl.pallas_call(
        functools.partial(_attn_kernel, S=S),
        grid=(B, n_pairs),
        in_specs=[blk(0), blk(n_pairs), blk(2 * n_pairs)],
        out_specs=pl.BlockSpec((None, S, LANES), lambda b, h: (b, 0, h)),
        out_shape=jax.ShapeDtypeStruct((B, S, ATTN_WIDTH), F32),
        scratch_shapes=[pltpu.VMEM((S, LANES), BF16)] * 3
                       + [pltpu.VMEM((S, LANES), F32)] * 6
                       + [pltpu.VMEM((len(DILATIONS), S, LANES), F32)] * 3
                       + [pltpu.VMEM((3, min(128, S), min(256, S)), F32)],
        compiler_params=_params(("arbitrary", "arbitrary")),
        name="dilated_attention",
    )(proj, proj, proj)


REC_GROUP_CHUNKS = 4
REC_GROUP_ROWS = REC_GROUP_CHUNKS * REC_CHUNK
REC_STAGE_GROUPS = 4


def _hgrn_kernel(rq_ref, zf_ref, zb_ref, ri_ref, rg_ref, lb_ref, gain_ref, o_ref,
                 q_s, vt_s, oi_s, qp_s, ut_s, dd_s, st_s, os_s, *, S):
    C, G, GR = REC_CHUNK, REC_GROUP_CHUNKS, REC_GROUP_ROWS
    nc, ng = S // C, S // GR
    sg = min(REC_STAGE_GROUPS, ng)
    assert ng % sg == 0

    row = lax.broadcasted_iota(I32, (GR, GR), 0)
    col = lax.broadcasted_iota(I32, (GR, GR), 1)
    same_chunk = (row // C) == (col // C)
    t_i = lax.broadcasted_iota(I32, (C, C), 0)
    s_i = lax.broadcasted_iota(I32, (C, C), 1)

    def prep(i, carry):
        r0 = pl.multiple_of(i * GR, GR)
        rq = rq_ref[pl.ds(r0, GR), :]
        q_s[pl.ds(r0, GR), :] = rq * _sigmoid(rq)
        v = ri_ref[pl.ds(r0, GR), :]
        for c in range(G):
            vt_s[i * G + c] = v[c * C:(c + 1) * C, :].T.astype(BF16)
        return carry

    lax.fori_loop(0, ng, prep, 0)

    for direction in range(2):
        fwd = direction == 0
        z_ref = zf_ref if fwd else zb_ref
        cum = jnp.where(same_chunk & ((col <= row) if fwd else (col >= row)), 1.0, 0.0).astype(BF16)
        keep = (s_i <= t_i) if fwd else (s_i >= t_i)
        mid_row = C // 2 - 1 if fwd else C // 2
        last_row = C - 1 if fwd else 0

        def phase_a(it, carry, fwd=fwd, z_ref=z_ref, cum=cum, keep=keep,
                    mid_row=mid_row, last_row=last_row, direction=direction):
            lb = lb_ref[direction:direction + 1, :]
            log_lb = jnp.log(lb)
            log_1m_lb = jnp.log1p(-lb)
            r0s, c0s, k3s, bcs, atts, stage3 = [], [], [], [], [], []
            for u in range(sg):
                i = it * sg + u
                r0 = pl.multiple_of(i * GR, GR)
                z = z_ref[pl.ds(r0, GR), :]
                e = jnp.exp(-jnp.abs(z))
                t = log_1m_lb + (jnp.minimum(z, 0.0) - jnp.log(1.0 + e))
                g = jnp.maximum(log_lb, t) + jnp.log(1.0 + jnp.exp(-jnp.abs(log_lb - t)))
                kk = (1.0 - lb) * (jnp.where(z > 0, e, 1.0) / (1.0 + e))
                g1 = g.astype(BF16)
                r1 = g - g1.astype(F32)
                g2 = r1.astype(BF16)
                g3 = (r1 - g2.astype(F32)).astype(BF16)
                bcs.append(jnp.dot(cum, g1, preferred_element_type=F32)
                           + jnp.dot(cum, g2, preferred_element_type=F32)
                           + jnp.dot(cum, g3, preferred_element_type=F32))
                k3s.append(kk.reshape(G, C, REC_DIM))
                r0s.append(r0)
                c0s.append(pl.multiple_of(i * G, G))
            for u in range(sg):
                b3 = bcs[u].reshape(G, C, REC_DIM)
                mid = b3[:, mid_row:mid_row + 1, :]
                last = b3[:, last_row:last_row + 1, :]
                e3 = b3 - mid
                q3 = q_s[pl.ds(r0s[u], GR), :].reshape(G, C, REC_DIM)
                qt = (q3 * jnp.exp(jnp.minimum(e3, DECAY_EXP_CLAMP))).astype(BF16)
                kt = (k3s[u] * jnp.exp(jnp.minimum(-e3, DECAY_EXP_CLAMP))).astype(BF16)
                atts.append(jnp.einsum('gtc,gsc->gts', qt, kt, preferred_element_type=F32))
                qp_s[pl.ds(r0s[u], GR), :] = (q3 * jnp.exp(b3)).astype(BF16).reshape(GR, REC_DIM)
                stage3.append(((k3s[u] * jnp.exp(last - b3)).astype(BF16), jnp.exp(last)))
            for u in range(sg):
                att = jnp.where(keep[None], atts[u], 0.0).astype(BF16)
                v3 = ri_ref[pl.ds(r0s[u], GR), :].reshape(G, C, REC_DIM).astype(BF16)
                oi = jnp.einsum('gts,gsv->gtv', att, v3, preferred_element_type=F32)
                oi_s[pl.ds(r0s[u], GR), :] = oi.reshape(GR, REC_DIM)
                kp, decay = stage3[u]
                ut_s[pl.ds(c0s[u], G)] = jnp.einsum('gvs,gsc->gvc', vt_s[pl.ds(c0s[u], G)], kp,
                                                    preferred_element_type=F32)
                dd_s[pl.ds(c0s[u], G)] = decay
            return carry

        lax.fori_loop(0, ng // sg, phase_a, 0)

        def scan(c, st, fwd=fwd):
            idx = c if fwd else nc - 1 - c
            st_s[idx] = st.astype(BF16)
            return dd_s[idx] * st + ut_s[idx]

        lax.fori_loop(0, nc, scan, jnp.zeros((REC_DIM, REC_DIM), F32))

        def phase_c(i, carry, fwd=fwd):
            r0 = pl.multiple_of(i * GR, GR)
            c0 = pl.multiple_of(i * G, G)
            qp3 = qp_s[pl.ds(r0, GR), :].reshape(G, C, REC_DIM)
            oo = jnp.einsum('gtc,gvc->gtv', qp3, st_s[pl.ds(c0, G)], preferred_element_type=F32)
            tot = oi_s[pl.ds(r0, GR), :] + oo.reshape(GR, REC_DIM)
            if fwd:
                os_s[pl.ds(r0, GR), :] = tot
            else:
                os_s[pl.ds(r0, GR), :] = os_s[pl.ds(r0, GR), :] + tot
            return carry

        lax.fori_loop(0, ng, phase_c, 0, unroll=2)

    o = os_s[...]
    o = o * lax.rsqrt(jnp.mean(o * o, axis=-1, keepdims=True) + RMS_EPS) * gain_ref[...]
    o_ref[...] = o * _sigmoid(rg_ref[...])


def _hgrn2(proj, lb, rec_gain):
    B, S, _ = proj.shape
    nh = REC_WIDTH // REC_DIM
    assert S % REC_GROUP_ROWS == 0
    col0 = 3 * ATTN_WIDTH // LANES
    blk = lambda k: pl.BlockSpec((None, S, REC_DIM), lambda b, h: (b, 0, col0 + k * nh + h))
    nc = S // REC_CHUNK
    return pl.pallas_call(
        functools.partial(_hgrn_kernel, S=S),
        grid=(B, nh),
        in_specs=[blk(0), blk(1), blk(2), blk(3), blk(4),
                  pl.BlockSpec((2, REC_DIM), lambda b, h: (0, h)),
                  pl.BlockSpec((1, REC_DIM), lambda b, h: (0, h))],
        out_specs=pl.BlockSpec((None, S, REC_DIM), lambda b, h: (b, 0, h)),
        out_shape=jax.ShapeDtypeStruct((B, S, REC_WIDTH), F32),
        scratch_shapes=[pltpu.VMEM((S, REC_DIM), F32),
                        pltpu.VMEM((nc, REC_DIM, REC_CHUNK), BF16),
                        pltpu.VMEM((S, REC_DIM), F32),
                        pltpu.VMEM((S, REC_DIM), BF16),
                        pltpu.VMEM((nc, REC_DIM, REC_DIM), F32),
                        pltpu.VMEM((nc, 1, REC_DIM), F32),
                        pltpu.VMEM((nc, REC_DIM, REC_DIM), BF16),
                        pltpu.VMEM((S, REC_DIM), F32)],
        compiler_params=_params(("arbitrary", "arbitrary")),
        name="hgrn2",
    )(proj, proj, proj, proj, proj, lb, rec_gain)


def _outproj_kernel(a_ref, r_ref, x_ref, gate_ref, ag_ref, w_ref, lg_ref, lb_ref, o_ref, *, alpha):
    a = a_ref[...]
    an = a * lax.rsqrt(jnp.mean(a * a, axis=-1, keepdims=True) + RMS_EPS) * ag_ref[...]
    y = (jnp.dot(an.astype(BF16), w_ref[:ATTN_WIDTH, :], preferred_element_type=F32)
         + jnp.dot(r_ref[...].astype(BF16), w_ref[ATTN_WIDTH:, :], preferred_element_type=F32))
    z = alpha * x_ref[...] + (1.0 + gate_ref[...]) * y
    o_ref[...] = _layer_norm(z, lg_ref[...], lb_ref[...])


def _out_projection(attn, rec, x, gate, attn_gain, w_out_bf16, ln_g, ln_b, *, alpha, tm):
    B, S, D = x.shape
    row = lambda w: pl.BlockSpec((None, tm, w), lambda b, i: (b, i, 0))
    vec = lambda w: pl.BlockSpec((1, w), lambda b, i: (0, 0))
    return pl.pallas_call(
        functools.partial(_outproj_kernel, alpha=alpha),
        grid=(B, S // tm),
        in_specs=[row(ATTN_WIDTH), row(REC_WIDTH), row(D),
                  pl.BlockSpec((None, 1, D), lambda b, i: (b, 0, 0)),
                  vec(ATTN_WIDTH),
                  pl.BlockSpec((ATTN_WIDTH + REC_WIDTH, D), lambda b, i: (0, 0)),
                  vec(D), vec(D)],
        out_specs=row(D),
        out_shape=jax.ShapeDtypeStruct((B, S, D), F32),
        compiler_params=_params(("arbitrary", "arbitrary")),
        name="out_projection",
    )(attn, rec, x, gate, attn_gain, w_out_bf16, ln_g, ln_b)


def _ffn_kernel(x_ref, sc_ref, sh_ref, gate_ref, wg_ref, wu_ref, wd_ref, lg_ref, lb_ref, o_ref,
                *, alpha, tc):
    x = x_ref[...]
    u = (x * (1.0 + sc_ref[...]) + sh_ref[...]).astype(BF16)
    acc = jnp.zeros(x.shape, F32)
    for c in range(wg_ref.shape[1] // tc):
        cols = slice(c * tc, (c + 1) * tc)
        g = jnp.dot(u, wg_ref[:, cols], preferred_element_type=F32)
        up = jnp.dot(u, wu_ref[:, cols], preferred_element_type=F32)
        h = (g * _sigmoid(g) * up).astype(BF16)
        acc = acc + jnp.dot(h, wd_ref[cols, :], preferred_element_type=F32)
    z = alpha * x + (1.0 + gate_ref[...]) * acc
    o_ref[...] = _layer_norm(z, lg_ref[...], lb_ref[...])


def _dense_ffn(x, scale, shift, gate, wg, wu, wd, ln_g, ln_b, *, alpha, tm, tc):
    B, S, D = x.shape
    ffp = wg.shape[1]
    assert ffp % tc == 0
    row = pl.BlockSpec((None, tm, D), lambda b, i: (b, i, 0))
    mod = pl.BlockSpec((None, 1, D), lambda b, i: (b, 0, 0))
    vec = pl.BlockSpec((1, D), lambda b, i: (0, 0))
    return pl.pallas_call(
        functools.partial(_ffn_kernel, alpha=alpha, tc=tc),
        grid=(B, S // tm),
        in_specs=[row, mod, mod, mod,
                  pl.BlockSpec((D, ffp), lambda b, i: (0, 0)),
                  pl.BlockSpec((D, ffp), lambda b, i: (0, 0)),
                  pl.BlockSpec((ffp, D), lambda b, i: (0, 0)),
                  vec, vec],
        out_specs=row,
        out_shape=jax.ShapeDtypeStruct((B, S, D), F32),
        compiler_params=_params(("arbitrary", "arbitrary")),
        name="dense_ffn",
    )(x, scale, shift, gate, wg, wu, wd, ln_g, ln_b)


META_E0, META_E1, META_W0, META_W1, META_R0, META_R1 = range(6)
SEGMENT_ALIGN = 16
DISPATCH_ROWS = 256
DISPATCH_HALF = DISPATCH_ROWS // 2
DISPATCH_SLACK = DISPATCH_HALF


def _router_kernel(x_ref, sc_ref, sh_ref, wr_ref, u_ref, meta_ref, cnt_ref, carry_ref, *, n_exp):
    tm = x_ref.shape[0]

    @pl.when((pl.program_id(0) == 0) & (pl.program_id(1) == 0))
    def _():
        carry_ref[...] = jnp.zeros_like(carry_ref)

    u = x_ref[...] * (1.0 + sc_ref[...]) + sh_ref[...]
    u_ref[...] = u.astype(BF16)
    u_hi = u.astype(BF16)
    u_lo = (u - u_hi.astype(F32)).astype(BF16)
    logits = (jnp.dot(u_hi, wr_ref[0], preferred_element_type=F32)
              + jnp.dot(u_lo, wr_ref[0], preferred_element_type=F32)
              + jnp.dot(u_hi, wr_ref[1], preferred_element_type=F32))
    lane = lax.broadcasted_iota(I32, (tm, LANES), 1)
    neg = -jnp.inf
    l1 = jnp.where(lane < n_exp, logits, neg)
    m1 = jnp.max(l1, axis=1, keepdims=True)
    i1 = jnp.min(jnp.where(l1 == m1, lane, LANES), axis=1, keepdims=True)
    l2 = jnp.where(lane == i1, neg, l1)
    m2 = jnp.max(l2, axis=1, keepdims=True)
    i2 = jnp.min(jnp.where(l2 == m2, lane, LANES), axis=1, keepdims=True)
    e = jnp.exp(m2 - m1)
    w1 = 1.0 / (1.0 + e)
    w2 = e * w1
    sel = jnp.where((lane == i1) | (lane == i2), 1.0, 0.0)
    before = (lax.broadcasted_iota(I32, (tm, tm), 1) < lax.broadcasted_iota(I32, (tm, tm), 0))
    ranks = jnp.dot(jnp.where(before, 1.0, 0.0).astype(BF16), sel.astype(BF16),
                    preferred_element_type=F32) + carry_ref[...]
    r1 = jnp.sum(jnp.where(lane == i1, ranks, 0.0), axis=1, keepdims=True)
    r2 = jnp.sum(jnp.where(lane == i2, ranks, 0.0), axis=1, keepdims=True)
    carry_ref[...] = carry_ref[...] + jnp.sum(sel, axis=0, keepdims=True)
    cnt_ref[...] = carry_ref[...]
    meta = jnp.zeros((tm, LANES), F32)
    for k, val in ((META_E0, i1.astype(F32)), (META_E1, i2.astype(F32)), (META_W0, w1),
                   (META_W1, w2), (META_R0, r1), (META_R1, r2)):
        meta = jnp.where(lane == k, val, meta)
    meta_ref[...] = meta


def _router(x, scale, shift, w_router_lanes, *, n_exp, tm):
    B, S, D = x.shape
    nt = S // tm
    row = pl.BlockSpec((None, tm, D), lambda b, i: (b, i, 0))
    mod = pl.BlockSpec((None, 1, D), lambda b, i: (b, 0, 0))
    return pl.pallas_call(
        functools.partial(_router_kernel, n_exp=n_exp),
        grid=(B, nt),
        in_specs=[row, mod, mod, pl.BlockSpec((2, D, LANES), lambda b, i: (0, 0, 0))],
        out_specs=[row,
                   pl.BlockSpec((None, tm, LANES), lambda b, i: (b, i, 0)),
                   pl.BlockSpec((1, LANES), lambda b, i: (0, 0))],
        out_shape=[jax.ShapeDtypeStruct((B, S, D), BF16),
                   jax.ShapeDtypeStruct((B, S, LANES), F32),
                   jax.ShapeDtypeStruct((1, LANES), F32)],
        scratch_shapes=[pltpu.VMEM((1, LANES), F32)],
        compiler_params=_params(("arbitrary", "arbitrary")),
        name="moe_router",
    )(x, scale, shift, w_router_lanes)


def _rows_at(hbm, first_row, n_rows):
    return hbm.at[pl.ds(pl.multiple_of(first_row, SEGMENT_ALIGN), n_rows)]


def _dispatch_kernel(base_ref, cnt_ref, padlo_ref, padn_ref, u_ref, meta_ref, xs_hbm,
                     stage, extra, zeros_buf, sems, extra_sem, *, n_exp):
    t = pl.program_id(0)
    last = pl.num_programs(0) - 1
    tm = u_ref.shape[0]
    R, H = DISPATCH_ROWS, DISPATCH_HALF
    slot = t % 2
    u = u_ref[...]
    e0, e1 = meta_ref[0:1, :], meta_ref[1:2, :]
    d0, d1 = meta_ref[2:3, :], meta_ref[3:4, :]
    srow = lax.broadcasted_iota(I32, (R, tm), 0)

    def main_copies(step, buf, fn):
        for e in range(n_exp):
            base = base_ref[step * n_exp + e]
            fn(pltpu.make_async_copy(stage.at[buf, pl.ds(e * R, H)], _rows_at(xs_hbm, base, H),
                                     sems.at[buf]))

            @pl.when(cnt_ref[step * n_exp + e] > H)
            def _(e=e, base=base):
                fn(pltpu.make_async_copy(stage.at[buf, pl.ds(e * R + H, H)],
                                         _rows_at(xs_hbm, base + H, H), sems.at[buf]))

    rels = []
    for e in range(n_exp):
        rel = jnp.where(e0 == e, d0, jnp.where(e1 == e, d1, -1))
        rels.append(jnp.where(rel >= 0, rel - base_ref[t * n_exp + e], -1))
    onehot = jnp.concatenate([jnp.where(srow == rel, 1.0, 0.0) for rel in rels], axis=0).astype(BF16)
    stage[slot] = jnp.dot(onehot, u, preferred_element_type=F32).astype(BF16)

    @pl.when(t > 0)
    def _():
        main_copies(t - 1, 1 - slot, lambda cp: cp.wait())
    main_copies(t, slot, lambda cp: cp.start())

    for e in range(n_exp):
        for blk in range(1, tm // R):
            @pl.when(cnt_ref[t * n_exp + e] > blk * R)
            def _(e=e, blk=blk):
                more = jnp.where(srow + blk * R == rels[e], 1.0, 0.0).astype(BF16)
                extra[...] = jnp.dot(more, u, preferred_element_type=F32).astype(BF16)
                cp = pltpu.make_async_copy(extra, _rows_at(xs_hbm, base_ref[t * n_exp + e] + blk * R, R),
                                           extra_sem)
                cp.start()
                cp.wait()

    @pl.when(t == last)
    def _():
        main_copies(t, slot, lambda cp: cp.wait())
        zeros_buf[...] = jnp.zeros_like(zeros_buf)
        pieces = [SEGMENT_ALIGN << b for b in range((R // SEGMENT_ALIGN).bit_length() - 1)]

        def pad_copies(fn):
            for p in range(padlo_ref.shape[0]):
                lo, n = padlo_ref[p], padn_ref[p]
                whole = n // R

                def body(k, carry, lo=lo):
                    fn(pltpu.make_async_copy(zeros_buf, _rows_at(xs_hbm, lo + k * R, R), extra_sem))
                    return carry
                lax.fori_loop(0, whole, body, 0)
                rem = n - whole * R
                for sz in pieces:
                    off = lo + whole * R + (rem // (2 * sz)) * (2 * sz)
                    pl.when((rem // sz) % 2 == 1)(functools.partial(
                        lambda off, sz: fn(pltpu.make_async_copy(
                            zeros_buf.at[pl.ds(0, sz)], _rows_at(xs_hbm, off, sz), extra_sem)), off, sz))

        pad_copies(lambda cp: cp.start())
        pad_copies(lambda cp: cp.wait())


def _dispatch(u, meta_t, base_te, cnt_te, pad_lo, pad_n, *, n_rows, tm):
    T, D = u.shape
    n_exp = pad_lo.shape[0] - 1
    assert tm % DISPATCH_ROWS == 0
    grid_spec = pltpu.PrefetchScalarGridSpec(
        num_scalar_prefetch=4,
        grid=(T // tm,),
        in_specs=[pl.BlockSpec((tm, D), lambda t, *_: (t, 0)),
                  pl.BlockSpec((meta_t.shape[0], tm), lambda t, *_: (0, t))],
        out_specs=pl.BlockSpec(memory_space=pl.ANY),
        scratch_shapes=[pltpu.VMEM((2, n_exp * DISPATCH_ROWS, D), BF16),
                        pltpu.VMEM((DISPATCH_ROWS, D), BF16),
                        pltpu.VMEM((DISPATCH_ROWS, D), BF16),
                        pltpu.SemaphoreType.DMA((2,)),
                        pltpu.SemaphoreType.DMA(())],
    )
    return pl.pallas_call(
        functools.partial(_dispatch_kernel, n_exp=n_exp),
        grid_spec=grid_spec,
        out_shape=jax.ShapeDtypeStruct((n_rows, D), BF16),
        compiler_params=_params(("arbitrary",)),
        name="moe_dispatch",
    )(base_te, cnt_te, pad_lo, pad_n, u, meta_t)


def _expert_kernel(te_ref, na_ref, xs_ref, wg_ref, wu_ref, wd_ref, o_ref, acc_ref):
    i = pl.program_id(0)
    j = pl.program_id(1)
    last = pl.num_programs(1) - 1
    active = i < na_ref[0]

    @pl.when(active & (j == 0))
    def _():
        acc_ref[...] = jnp.zeros_like(acc_ref)

    @pl.when(active)
    def _():
        xb = xs_ref[...]
        g = jnp.dot(xb, wg_ref[...].astype(BF16), preferred_element_type=F32)
        up = jnp.dot(xb, wu_ref[...].astype(BF16), preferred_element_type=F32)
        h = (g * _sigmoid(g) * up).astype(BF16)
        acc_ref[...] += jnp.dot(h, wd_ref[...].astype(BF16), preferred_element_type=F32)

    @pl.when(active & (j == last))
    def _():
        o_ref[...] = acc_ref[...].astype(BF16)

    @pl.when(jnp.logical_not(active) & (j == last))
    def _():
        o_ref[...] = jnp.zeros_like(o_ref)


def _expert_ffn(xs, tile_expert, n_active, wg, wu, wd, *, tm, tf):
    n_rows, D = xs.shape
    ff = wg.shape[2]
    assert n_rows % tm == 0 and ff % tf == 0
    nj = ff // tf

    def jj(i, j, na):
        return jnp.where(i < na[0], j, nj - 1)

    def ii(i, na):
        return jnp.minimum(i, na[0] - 1)

    grid_spec = pltpu.PrefetchScalarGridSpec(
        num_scalar_prefetch=2,
        grid=(n_rows // tm, nj),
        in_specs=[pl.BlockSpec((tm, D), lambda i, j, te, na: (ii(i, na), 0)),
                  pl.BlockSpec((None, D, tf), lambda i, j, te, na: (te[i], 0, jj(i, j, na))),
                  pl.BlockSpec((None, D, tf), lambda i, j, te, na: (te[i], 0, jj(i, j, na))),
                  pl.BlockSpec((None, tf, D), lambda i, j, te, na: (te[i], jj(i, j, na), 0))],
        out_specs=pl.BlockSpec((tm, D), lambda i, j, te, na: (i, 0)),
        scratch_shapes=[pltpu.VMEM((tm, D), F32)],
    )
    return pl.pallas_call(
        _expert_kernel,
        grid_spec=grid_spec,
        out_shape=jax.ShapeDtypeStruct((n_rows, D), BF16),
        compiler_params=_params(("arbitrary", "arbitrary")),
        name="moe_experts",
    )(tile_expert, n_active, xs, wg, wu, wd)


def _combine_kernel(base_ref, off_ref, cnt_ref, ys_hbm, meta_ref, x_ref, gate_ref, lg_ref, lb_ref, o_ref,
                    stage, extra, acc_ref, sems, extra_sem, *, alpha, n_exp):
    t = pl.program_id(0) * pl.num_programs(1) + pl.program_id(1)
    n_steps = pl.num_programs(0) * pl.num_programs(1)
    tm = x_ref.shape[0]
    R = DISPATCH_ROWS
    slot = t % 2

    def main_copies(step, buf, fn):
        for e in range(n_exp):
            fn(pltpu.make_async_copy(_rows_at(ys_hbm, base_ref[step * n_exp + e], R),
                                     stage.at[buf, pl.ds(e * R, R)], sems.at[buf]))

    @pl.when(t == 0)
    def _():
        main_copies(0, 0, lambda cp: cp.start())

    @pl.when(t + 1 < n_steps)
    def _():
        main_copies(t + 1, 1 - slot, lambda cp: cp.start())

    meta = meta_ref[...]
    col = lambda k: meta[:, k:k + 1]
    e0, e1 = col(META_E0).astype(I32), col(META_E1).astype(I32)
    r0, r1 = col(META_R0).astype(I32), col(META_R1).astype(I32)
    w0, w1 = col(META_W0), col(META_W1)
    lane = lax.broadcasted_iota(I32, (tm, R), 1)

    rels, ws, picks = [], [], []
    for e in range(n_exp):
        rel = jnp.where(e0 == e, r0, jnp.where(e1 == e, r1, -1))
        rels.append(jnp.where(rel >= 0, rel - off_ref[t * n_exp + e], -1))
        ws.append(jnp.where(e0 == e, w0, jnp.where(e1 == e, w1, 0.0)))
        picks.append(jnp.where(lane == rels[e], ws[e], 0.0))
    weighted = jnp.concatenate(picks, axis=1).astype(BF16)
    main_copies(t, slot, lambda cp: cp.wait())
    acc_ref[...] = jnp.dot(weighted, stage[slot], preferred_element_type=F32)

    for e in range(n_exp):
        for blk in range(1, tm // R):
            @pl.when(cnt_ref[t * n_exp + e] > blk * R)
            def _(e=e, blk=blk):
                cp = pltpu.make_async_copy(_rows_at(ys_hbm, base_ref[t * n_exp + e] + blk * R, R),
                                           extra, extra_sem)
                cp.start()
                cp.wait()
                more = jnp.where(lane + blk * R == rels[e], ws[e], 0.0).astype(BF16)
                acc_ref[...] += jnp.dot(more, extra[...], preferred_element_type=F32)

    z = alpha * x_ref[...] + (1.0 + gate_ref[...]) * acc_ref[...]
    o_ref[...] = _layer_norm(z, lg_ref[...], lb_ref[...])


def _combine(ys, base_te, off_te, cnt_te, meta, x, gate, ln_g, ln_b, *, alpha, tm):
    B, S, D = x.shape
    nt = S // tm
    n_exp = base_te.shape[0] // (B * nt)
    row = pl.BlockSpec((None, tm, D), lambda b, i, *_: (b, i, 0))
    vec = pl.BlockSpec((1, D), lambda b, i, *_: (0, 0))
    grid_spec = pltpu.PrefetchScalarGridSpec(
        num_scalar_prefetch=3,
        grid=(B, nt),
        in_specs=[pl.BlockSpec(memory_space=pl.ANY),
                  pl.BlockSpec((None, tm, LANES), lambda b, i, *_: (b, i, 0)),
                  row, pl.BlockSpec((None, 1, D), lambda b, i, *_: (b, 0, 0)), vec, vec],
        out_specs=row,
        scratch_shapes=[pltpu.VMEM((2, n_exp * DISPATCH_ROWS, D), BF16),
                        pltpu.VMEM((DISPATCH_ROWS, D), BF16),
                        pltpu.VMEM((tm, D), F32),
                        pltpu.SemaphoreType.DMA((2,)),
                        pltpu.SemaphoreType.DMA(())],
    )
    return pl.pallas_call(
        functools.partial(_combine_kernel, alpha=alpha, n_exp=n_exp),
        grid_spec=grid_spec,
        out_shape=jax.ShapeDtypeStruct((B, S, D), F32),
        compiler_params=_params(("arbitrary", "arbitrary")),
        name="moe_combine",
    )(base_te, off_te, cnt_te, ys, meta, x, gate, ln_g, ln_b)


def _moe_ffn(x, scale, shift, gate, w_router, wg, wu, wd, ln_g, ln_b, *, alpha, tm, tm_e, tf):
    B, S, D = x.shape
    T = B * S
    nt = T // tm
    n_exp = w_router.shape[1]
    wr = jnp.zeros((D, LANES), F32).at[:, :n_exp].set(w_router)
    wr_hi = wr.astype(BF16)
    wr = jnp.stack([wr_hi, (wr - wr_hi.astype(F32)).astype(BF16)])
    u, meta, counts = _router(x, scale, shift, wr, n_exp=n_exp, tm=tm)

    meta2 = meta.reshape(T, LANES)
    e0 = meta2[:, META_E0].astype(I32)
    e1 = meta2[:, META_E1].astype(I32)
    r0 = meta2[:, META_R0].astype(I32)
    r1 = meta2[:, META_R1].astype(I32)
    experts = jnp.arange(n_exp, dtype=I32)
    chosen = ((e0[:, None] == experts) | (e1[:, None] == experts)).astype(I32)
    cnt_te = chosen.reshape(nt, tm, n_exp).sum(axis=1)
    seg_te = -(-cnt_te // SEGMENT_ALIGN) * SEGMENT_ALIGN
    off_te = jnp.cumsum(cnt_te, axis=0) - cnt_te
    filled = seg_te.sum(axis=0)
    sizes = -(-(filled + DISPATCH_SLACK) // tm_e) * tm_e
    ends = jnp.cumsum(sizes)
    starts = ends - sizes
    base_te = starts[None, :] + jnp.cumsum(seg_te, axis=0) - seg_te
    shift_te = (base_te - off_te)[:, None, :]

    def rows_of(e, r):
        pick = (e[:, None] == experts).astype(I32).reshape(nt, tm, n_exp)
        return r + (pick * shift_te).sum(axis=-1).reshape(T)

    d0, d1 = rows_of(e0, r0), rows_of(e1, r1)
    bound = TOP_K * T + n_exp * ((SEGMENT_ALIGN - 1) * nt + DISPATCH_SLACK)
    n_tiles = -(-bound // tm_e) + n_exp + 1
    n_active = (ends[-1] // tm_e).astype(I32)
    tile_id = jnp.minimum(jnp.arange(n_tiles, dtype=I32), n_active - 1)
    tile_expert = jnp.sum((ends[None, :] <= (tile_id * tm_e)[:, None]).astype(I32), axis=1)
    meta_t = jnp.stack([e0, e1, d0, d1] + [jnp.zeros_like(e0)] * 4)

    xs = _dispatch(u.reshape(T, D), meta_t, base_te.reshape(-1), cnt_te.reshape(-1),
                   jnp.concatenate([starts + filled, ends[-1:]]),
                   jnp.concatenate([sizes - filled, n_tiles * tm_e - ends[-1:]]),
                   n_rows=n_tiles * tm_e, tm=tm)
    ys = _expert_ffn(xs, tile_expert, n_active.reshape(1), wg, wu, wd, tm=tm_e, tf=tf)
    return _combine(ys, base_te.reshape(-1), off_te.reshape(-1), cnt_te.reshape(-1),
                    meta, x, gate, ln_g, ln_b, alpha=alpha, tm=tm)


def kernel(x, c, positions, w_in, w_out, attn_norm_gain, rec_norm_gain, rec_lb_logits, ada_w, ada_b,
           ln_gain, ln_bias, ffn_w_gate, ffn_w_up, ffn_w_down, moe_router, moe_w_gate, moe_w_up,
           moe_w_down):
    B, S, D = x.shape
    depth = w_in.shape[0]
    alpha = (2 * depth) ** 0.25

    p = jax.nn.softmax(rec_lb_logits.astype(F32), axis=0)
    cum = jnp.cumsum(p, axis=0)
    lb_all = cum - cum[0:1]
    half = ATTN_HEAD_DIM // 2
    inv_freq = ROPE_THETA ** (-jnp.arange(half, dtype=F32) / half)
    inv_freq_lanes = jnp.tile(inv_freq, LANES // half).reshape(1, LANES)
    pos_f = positions.astype(F32).reshape(B, S, 1)
    w_in_b = w_in.astype(BF16)
    w_out_b = w_out.astype(BF16)
    ff = ffn_w_gate.shape[2]
    ffp = -(-ff // (2 * LANES)) * (2 * LANES)
    pad_c = lambda w: jnp.pad(w.astype(BF16), ((0, 0), (0, 0), (0, ffp - ff)))
    ffn_g, ffn_u = pad_c(ffn_w_gate), pad_c(ffn_w_up)
    ffn_d = jnp.pad(ffn_w_down.astype(BF16), ((0, 0), (0, ffp - ff), (0, 0)))

    mods = _ada_modulation(c, ada_w, ada_b)
    rope_cos, rope_sin = _rope_tables(pos_f, inv_freq_lanes, tm=min(1024, S))

    def mod(layer, sub):
        m = mods[layer * 2 + sub].reshape(B, 1, 3 * D)
        return m[..., :D], m[..., D:2 * D], m[..., 2 * D:]

    vec = lambda a: a.reshape(1, -1)
    for layer in range(depth):
        shift, scale, gate = mod(layer, 0)
        proj = _in_projection(x, scale, shift, rope_cos, rope_sin, w_in_b[layer], tm=min(1024, S))
        attn = _dilated_attention(proj)
        rec = _hgrn2(proj, lb_all[layer], vec(rec_norm_gain[layer]))
        x = _out_projection(attn, rec, x, gate, vec(attn_norm_gain[layer]), w_out_b[layer],
                            vec(ln_gain[layer, 0]), vec(ln_bias[layer, 0]), alpha=alpha, tm=min(512, S))
        shift, scale, gate = mod(layer, 1)
        j = layer // 2
        if layer % 2 == 0:
            x = _dense_ffn(x, scale, shift, gate, ffn_g[j], ffn_u[j], ffn_d[j],
                           vec(ln_gain[layer, 1]), vec(ln_bias[layer, 1]),
                           alpha=alpha, tm=min(512, S), tc=2 * LANES)
        else:
            x = _moe_ffn(x, scale, shift, gate, moe_router[j], moe_w_gate[j], moe_w_up[j],
                         moe_w_down[j], vec(ln_gain[layer, 1]), vec(ln_bias[layer, 1]),
                         alpha=alpha, tm=min(512, S), tm_e=min(1024, B * S), tf=512)
    return x
```

```python
import functools

import jax
import jax.numpy as jnp
from jax import lax
from jax.experimental import pallas as pl
from jax.experimental.pallas import tpu as pltpu

F32 = jnp.float32
BF16 = jnp.bfloat16
I32 = jnp.int32

LANES = 128
V7X_VMEM_LIMIT_BYTES = 56 * 1024 * 1024

ATTN_WIDTH = 512
ATTN_HEAD_DIM = 64
DILATIONS = (1, 4, 16)
ATTN_RADIUS = 64
ROPE_THETA = 10000.0
REC_WIDTH = 512
REC_DIM = 128
REC_CHUNK = 64
TOP_K = 2
LN_EPS = 1e-5
RMS_EPS = 1e-6
MASK_VALUE = -1e30
DECAY_EXP_CLAMP = 80.0


def _params(semantics):
    return pltpu.CompilerParams(dimension_semantics=semantics,
                                vmem_limit_bytes=V7X_VMEM_LIMIT_BYTES)


def _sigmoid(x):
    return 1.0 / (1.0 + jnp.exp(-x))


def _layer_norm(z, gain, bias):
    mu = jnp.mean(z, axis=-1, keepdims=True)
    zc = z - mu
    var = jnp.mean(zc * zc, axis=-1, keepdims=True)
    return zc * lax.rsqrt(var + LN_EPS) * gain + bias


def _ada_kernel(c_ref, w_ref, b_ref, o_ref):
    c = c_ref[...]
    o_ref[...] = jnp.dot(c * _sigmoid(c), w_ref[...], precision=lax.Precision.HIGHEST,
                         preferred_element_type=F32) + b_ref[...]


def _ada_modulation(c, ada_w, ada_b):
    B, D = c.shape
    n = ada_w.shape[0] * ada_w.shape[1]
    n3 = ada_w.shape[-1]
    tn = 1024
    return pl.pallas_call(
        _ada_kernel,
        grid=(n, n3 // tn),
        in_specs=[pl.BlockSpec((B, D), lambda i, j: (0, 0)),
                  pl.BlockSpec((None, D, tn), lambda i, j: (i, 0, j)),
                  pl.BlockSpec((None, 1, tn), lambda i, j: (i, 0, j))],
        out_specs=pl.BlockSpec((None, B, tn), lambda i, j: (i, 0, j)),
        out_shape=jax.ShapeDtypeStruct((n, B, n3), F32),
        compiler_params=_params(("arbitrary", "arbitrary")),
        name="ada_modulation",
    )(c, ada_w.reshape(n, D, n3), ada_b.reshape(n, 1, n3))


ROPE_HALF = ATTN_HEAD_DIM // 2


def _qk_lane_order(w_cols):
    lead = w_cols.shape[:-1]
    w = w_cols.reshape(*lead, -1, 2, 2, ROPE_HALF)
    return jnp.swapaxes(w, -3, -2).reshape(*lead, -1)


def _rope_kernel(pos_ref, invf_ref, cos_ref, sin_ref):
    ang = pos_ref[...] * invf_ref[...]
    lane = lax.broadcasted_iota(I32, (1, LANES), 1)
    sin = jnp.sin(ang)
    cos_ref[...] = jnp.cos(ang)
    sin_ref[...] = jnp.where(lane < LANES // 2, -sin, sin)


def _rope_tables(pos_f, inv_freq_lanes, *, tm):
    B, S, _ = pos_f.shape
    tab = pl.BlockSpec((None, tm, LANES), lambda b, i: (b, i, 0))
    return pl.pallas_call(
        _rope_kernel,
        grid=(B, S // tm),
        in_specs=[pl.BlockSpec((None, tm, 1), lambda b, i: (b, i, 0)),
                  pl.BlockSpec((1, LANES), lambda b, i: (0, 0))],
        out_specs=[tab, tab],
        out_shape=[jax.ShapeDtypeStruct((B, S, LANES), F32)] * 2,
        compiler_params=_params(("arbitrary", "arbitrary")),
        name="rope_tables",
    )(pos_f, inv_freq_lanes)


def _inproj_kernel(x_ref, sc_ref, sh_ref, cos_ref, sin_ref, w_ref, o_ref, u_ref, *, tn):
    j = pl.program_id(2)

    @pl.when(j == 0)
    def _():
        u_ref[...] = (x_ref[...] * (1.0 + sc_ref[...]) + sh_ref[...]).astype(BF16)

    cols = pl.ds(pl.multiple_of(j * tn, tn), tn)
    acc = jnp.dot(u_ref[...], w_ref[:, cols], preferred_element_type=F32)

    @pl.when(j > 0)
    def _():
        o_ref[...] = acc

    @pl.when(j == 0)
    def _():
        cos = cos_ref[...]
        sin_signed = sin_ref[...]
        q_scale = ATTN_HEAD_DIM ** -0.5
        for kk in range(tn // LANES):
            c = acc[:, kk * LANES:(kk + 1) * LANES]
            r = c * cos + pltpu.roll(c, LANES // 2, 1) * sin_signed
            if kk * LANES < ATTN_WIDTH:
                r = r * q_scale
            o_ref[:, kk * LANES:(kk + 1) * LANES] = r


def _in_projection(x, scale, shift, rope_cos, rope_sin, w_in_bf16, *, tm):
    B, S, D = x.shape
    n_cols = w_in_bf16.shape[1]
    tn = 2 * ATTN_WIDTH
    assert S % tm == 0 and n_cols % tn == 0
    return pl.pallas_call(
        functools.partial(_inproj_kernel, tn=tn),
        grid=(B, S // tm, n_cols // tn),
        in_specs=[pl.BlockSpec((None, tm, D), lambda b, i, j: (b, i, 0)),
                  pl.BlockSpec((None, 1, D), lambda b, i, j: (b, 0, 0)),
                  pl.BlockSpec((None, 1, D), lambda b, i, j: (b, 0, 0)),
                  pl.BlockSpec((None, tm, LANES), lambda b, i, j: (b, i, 0)),
                  pl.BlockSpec((None, tm, LANES), lambda b, i, j: (b, i, 0)),
                  pl.BlockSpec((D, n_cols), lambda b, i, j: (0, 0))],
        out_specs=pl.BlockSpec((None, tm, tn), lambda b, i, j: (b, i, j)),
        out_shape=jax.ShapeDtypeStruct((B, S, n_cols), F32),
        scratch_shapes=[pltpu.VMEM((tm, D), BF16)],
        compiler_params=_params(("arbitrary", "arbitrary", "arbitrary")),
        name="in_projection",
    )(x, scale, shift, rope_cos, rope_sin, w_in_bf16)


ATTN_BLOCK_GROUP = 8


def _attn_kernel(q_ref, k_ref, v_ref, o_ref, qs, ks, vs, qf, kf, vf, pm, plr, pa, nm, nl, na, bias_s,
                 *, S):
    lane = lax.broadcasted_iota(I32, (1, LANES), 1)
    head0 = lane < ATTN_HEAD_DIM
    head0_qk = (lane % ATTN_HEAD_DIM) < ROPE_HALF

    for p, d in enumerate(DILATIONS):
        L = S // d
        tq = min(128, L)
        W = min(2 * tq, L)
        nb = L // tq

        d_prev = DILATIONS[p - 1] if p else 1
        ratio, l_prev = d // d_prev, S // d_prev
        keep_f32 = 0 < p < len(DILATIONS) - 1
        for src, stage, dst in ((q_ref, qf, qs), (k_ref, kf, ks), (v_ref, vf, vs)):
            src = src if p <= 1 else stage
            for r in range(d):
                a, r_prev = divmod(r, d_prev)
                rows = pl.ds(r_prev * l_prev + a, L, stride=ratio) if ratio > 1 else pl.ds(0, L)
                val = src[rows, :]
                if keep_f32:
                    stage[r * L:(r + 1) * L, :] = val
                dst[r * L:(r + 1) * L, :] = val.astype(BF16)

        n_blocks = d * nb
        rc = lax.broadcasted_iota(I32, (tq, W), 0) - lax.broadcasted_iota(I32, (tq, W), 1)
        for which, delta in enumerate((0, ATTN_RADIUS, tq)):
            bias_s[which, pl.ds(0, tq), pl.ds(0, W)] = jnp.where(
                jnp.abs(rc + delta) <= ATTN_RADIUS, 0.0, MASK_VALUE)
        group = min(ATTN_BLOCK_GROUP, n_blocks)
        assert n_blocks % group == 0

        def block_group(gi, carry, L=L, tq=tq, W=W, nb=nb, group=group):
            q0s, k0s, scores, probs, outs, sums = [], [], [], [], [], []
            for u in range(group):
                g = gi * group + u
                r = g // nb
                n = g - r * nb
                base = r * L
                q0 = pl.multiple_of(base + n * tq, tq)
                ws = jnp.clip(n * tq - ATTN_RADIUS, 0, L - W)
                k0 = pl.multiple_of(base + ws, 16)
                qb = qs[pl.ds(q0, tq), :]
                zero = jnp.zeros_like(qb)
                q2 = jnp.concatenate([jnp.where(head0_qk, qb, zero), jnp.where(head0_qk, zero, qb)], axis=0)
                s = lax.dot_general(q2, ks[pl.ds(k0, W), :], (((1,), (1,)), ((), ())),
                                    preferred_element_type=F32)
                delta = n * tq - ws
                which = jnp.where(delta == 0, 0, jnp.where(delta == ATTN_RADIUS, 1, 2))
                bias = bias_s[which, pl.ds(0, tq), pl.ds(0, W)]
                scores.append(s + jnp.concatenate([bias, bias], axis=0))
                q0s.append(q0)
                k0s.append(k0)
            for u in range(group):
                s = scores[u]
                m = jnp.max(s, axis=1, keepdims=True)
                e = jnp.exp(s - m)
                probs.append((m, e.astype(BF16)))
                sums.append(jnp.sum(e, axis=1, keepdims=True))
            for u in range(group):
                outs.append(jnp.dot(probs[u][1], vs[pl.ds(k0s[u], W), :], preferred_element_type=F32))
            for u in range(group):
                m, l = probs[u][0], sums[u]
                rows = pl.ds(q0s[u], tq)
                out_m[rows, :] = jnp.where(head0, m[:tq], m[tq:])
                out_l[rows, :] = jnp.where(head0, l[:tq], l[tq:])
                out_a[rows, :] = jnp.where(head0, outs[u][:tq], outs[u][tq:])
            return carry

        out_m, out_l, out_a = (nm.at[p], nl.at[p], na.at[p]) if d == 1 else (pm, plr, pa)
        lax.fori_loop(0, n_blocks // group, block_group, 0)

        for r in range(d if d > 1 else 0):
            rows = pl.ds(r, L, stride=d)
            nm[p, rows, :] = pm[r * L:(r + 1) * L, :]
            nl[p, rows, :] = plr[r * L:(r + 1) * L, :]
            na[p, rows, :] = pa[r * L:(r + 1) * L, :]

    m_all = jnp.maximum(jnp.maximum(nm[0], nm[1]), nm[2])
    num = jnp.zeros((S, LANES), F32)
    den = jnp.zeros((S, LANES), F32)
    for p in range(len(DILATIONS)):
        w = jnp.exp(nm[p] - m_all)
        num = num + w * na[p]
        den = den + w * nl[p]
    o_ref[...] = num / den


def _dilated_attention(proj):
    B, S, _ = proj.shape
    n_pairs = ATTN_WIDTH // LANES
    assert S % (16 * DILATIONS[-1]) == 0
    blk = lambda off: pl.BlockSpec((None, S, LANES), lambda b, h: (b, 0, off + h))
    return pl.pallas_call(
        functools.partial(_attn_kernel, S=S),
        grid=(B, n_pairs),
        in_specs=[blk(0), blk(n_pairs), blk(2 * n_pairs)],
        out_specs=pl.BlockSpec((None, S, LANES), lambda b, h: (b, 0, h)),
        out_shape=jax.ShapeDtypeStruct((B, S, ATTN_WIDTH), F32),
        scratch_shapes=[pltpu.VMEM((S, LANES), BF16)] * 3
                       + [pltpu.VMEM((S, LANES), F32)] * 6
                       + [pltpu.VMEM((len(DILATIONS), S, LANES), F32)] * 3
                       + [pltpu.VMEM((3, min(128, S), min(256, S)), F32)],
        compiler_params=_params(("arbitrary", "arbitrary")),
        name="dilated_attention",
    )(proj, proj, proj)


REC_GROUP_CHUNKS = 4
REC_GROUP_ROWS = REC_GROUP_CHUNKS * REC_CHUNK
REC_STAGE_GROUPS = 4


def _hgrn_kernel(rq_ref, zf_ref, zb_ref, ri_ref, rg_ref, lb_ref, gain_ref, o_ref,
                 q_s, vt_s, oi_s, qp_s, ut_s, dd_s, st_s, os_s, *, S):
    C, G, GR = REC_CHUNK, REC_GROUP_CHUNKS, REC_GROUP_ROWS
    nc, ng = S // C, S // GR
    sg = min(REC_STAGE_GROUPS, ng)
    assert ng % sg == 0

    row = lax.broadcasted_iota(I32, (GR, GR), 0)
    col = lax.broadcasted_iota(I32, (GR, GR), 1)
    same_chunk = (row // C) == (col // C)
    t_i = lax.broadcasted_iota(I32, (C, C), 0)
    s_i = lax.broadcasted_iota(I32, (C, C), 1)

    def prep(i, carry):
        r0 = pl.multiple_of(i * GR, GR)
        rq = rq_ref[pl.ds(r0, GR), :]
        q_s[pl.ds(r0, GR), :] = rq * _sigmoid(rq)
        v = ri_ref[pl.ds(r0, GR), :]
        for c in range(G):
            vt_s[i * G + c] = v[c * C:(c + 1) * C, :].T.astype(BF16)
        return carry

    lax.fori_loop(0, ng, prep, 0)

    for direction in range(2):
        fwd = direction == 0
        z_ref = zf_ref if fwd else zb_ref
        cum = jnp.where(same_chunk & ((col <= row) if fwd else (col >= row)), 1.0, 0.0).astype(BF16)
        keep = (s_i <= t_i) if fwd else (s_i >= t_i)
        mid_row = C // 2 - 1 if fwd else C // 2
        last_row = C - 1 if fwd else 0

        def phase_a(it, carry, fwd=fwd, z_ref=z_ref, cum=cum, keep=keep,
                    mid_row=mid_row, last_row=last_row, direction=direction):
            lb = lb_ref[direction:direction + 1, :]
            log_lb = jnp.log(lb)
            log_1m_lb = jnp.log1p(-lb)
            r0s, c0s, k3s, bcs, atts, stage3 = [], [], [], [], [], []
            for u in range(sg):
                i = it * sg + u
                r0 = pl.multiple_of(i * GR, GR)
                z = z_ref[pl.ds(r0, GR), :]
                e = jnp.exp(-jnp.abs(z))
                t = log_1m_lb + (jnp.minimum(z, 0.0) - jnp.log(1.0 + e))
                g = jnp.maximum(log_lb, t) + jnp.log(1.0 + jnp.exp(-jnp.abs(log_lb - t)))
                kk = (1.0 - lb) * (jnp.where(z > 0, e, 1.0) / (1.0 + e))
                g1 = g.astype(BF16)
                r1 = g - g1.astype(F32)
                g2 = r1.astype(BF16)
                g3 = (r1 - g2.astype(F32)).astype(BF16)
                bcs.append(jnp.dot(cum, g1, preferred_element_type=F32)
                           + jnp.dot(cum, g2, preferred_element_type=F32)
                           + jnp.dot(cum, g3, preferred_element_type=F32))
                k3s.append(kk.reshape(G, C, REC_DIM))
                r0s.append(r0)
                c0s.append(pl.multiple_of(i * G, G))
            for u in range(sg):
                b3 = bcs[u].reshape(G, C, REC_DIM)
                mid = b3[:, mid_row:mid_row + 1, :]
                last = b3[:, last_row:last_row + 1, :]
                e3 = b3 - mid
                q3 = q_s[pl.ds(r0s[u], GR), :].reshape(G, C, REC_DIM)
                qt = (q3 * jnp.exp(jnp.minimum(e3, DECAY_EXP_CLAMP))).astype(BF16)
                kt = (k3s[u] * jnp.exp(jnp.minimum(-e3, DECAY_EXP_CLAMP))).astype(BF16)
                atts.append(jnp.einsum('gtc,gsc->gts', qt, kt, preferred_element_type=F32))
                qp_s[pl.ds(r0s[u], GR), :] = (q3 * jnp.exp(b3)).astype(BF16).reshape(GR, REC_DIM)
                stage3.append(((k3s[u] * jnp.exp(last - b3)).astype(BF16), jnp.exp(last)))
            for u in range(sg):
                att = jnp.where(keep[None], atts[u], 0.0).astype(BF16)
                v3 = ri_ref[pl.ds(r0s[u], GR), :].reshape(G, C, REC_DIM).astype(BF16)
                oi = jnp.einsum('gts,gsv->gtv', att, v3, preferred_element_type=F32)
                oi_s[pl.ds(r0s[u], GR), :] = oi.reshape(GR, REC_DIM)
                kp, decay = stage3[u]
                ut_s[pl.ds(c0s[u], G)] = jnp.einsum('gvs,gsc->gvc', vt_s[pl.ds(c0s[u], G)], kp,
                                                    preferred_element_type=F32)
                dd_s[pl.ds(c0s[u], G)] = decay
            return carry

        lax.fori_loop(0, ng // sg, phase_a, 0)

        def scan(c, st, fwd=fwd):
            idx = c if fwd else nc - 1 - c
            st_s[idx] = st.astype(BF16)
            return dd_s[idx] * st + ut_s[idx]

        lax.fori_loop(0, nc, scan, jnp.zeros((REC_DIM, REC_DIM), F32))

        def phase_c(it, carry, fwd=fwd):
            r0s = [pl.multiple_of((it * sg + u) * GR, GR) for u in range(sg)]
            oos = []
            for u in range(sg):
                c0 = pl.multiple_of((it * sg + u) * G, G)
                qp3 = qp_s[pl.ds(r0s[u], GR), :].reshape(G, C, REC_DIM)
                oos.append(jnp.einsum('gtc,gvc->gtv', qp3, st_s[pl.ds(c0, G)],
                                      preferred_element_type=F32))
            for u in range(sg):
                rows = pl.ds(r0s[u], GR)
                tot = oi_s[rows, :] + oos[u].reshape(GR, REC_DIM)
                os_s[rows, :] = tot if fwd else os_s[rows, :] + tot
            return carry

        lax.fori_loop(0, ng // sg, phase_c, 0)

    o = os_s[...]
    o = o * lax.rsqrt(jnp.mean(o * o, axis=-1, keepdims=True) + RMS_EPS) * gain_ref[...]
    o_ref[...] = o * _sigmoid(rg_ref[...])


def _hgrn2(proj, lb, rec_gain):
    B, S, _ = proj.shape
    nh = REC_WIDTH // REC_DIM
    assert S % REC_GROUP_ROWS == 0
    col0 = 3 * ATTN_WIDTH // LANES
    blk = lambda k: pl.BlockSpec((None, S, REC_DIM), lambda b, h: (b, 0, col0 + k * nh + h))
    nc = S // REC_CHUNK
    return pl.pallas_call(
        functools.partial(_hgrn_kernel, S=S),
        grid=(B, nh),
        in_specs=[blk(0), blk(1), blk(2), blk(3), blk(4),
                  pl.BlockSpec((2, REC_DIM), lambda b, h: (0, h)),
                  pl.BlockSpec((1, REC_DIM), lambda b, h: (0, h))],
        out_specs=pl.BlockSpec((None, S, REC_DIM), lambda b, h: (b, 0, h)),
        out_shape=jax.ShapeDtypeStruct((B, S, REC_WIDTH), F32),
        scratch_shapes=[pltpu.VMEM((S, REC_DIM), F32),
                        pltpu.VMEM((nc, REC_DIM, REC_CHUNK), BF16),
                        pltpu.VMEM((S, REC_DIM), F32),
                        pltpu.VMEM((S, REC_DIM), BF16),
                        pltpu.VMEM((nc, REC_DIM, REC_DIM), F32),
                        pltpu.VMEM((nc, 1, REC_DIM), F32),
                        pltpu.VMEM((nc, REC_DIM, REC_DIM), BF16),
                        pltpu.VMEM((S, REC_DIM), F32)],
        compiler_params=_params(("arbitrary", "arbitrary")),
        name="hgrn2",
    )(proj, proj, proj, proj, proj, lb, rec_gain)


def _outproj_kernel(a_ref, r_ref, x_ref, gate_ref, ag_ref, w_ref, lg_ref, lb_ref, o_ref, *, alpha):
    a = a_ref[...]
    an = a * lax.rsqrt(jnp.mean(a * a, axis=-1, keepdims=True) + RMS_EPS) * ag_ref[...]
    y = (jnp.dot(an.astype(BF16), w_ref[:ATTN_WIDTH, :], preferred_element_type=F32)
         + jnp.dot(r_ref[...].astype(BF16), w_ref[ATTN_WIDTH:, :], preferred_element_type=F32))
    z = alpha * x_ref[...] + (1.0 + gate_ref[...]) * y
    o_ref[...] = _layer_norm(z, lg_ref[...], lb_ref[...])


def _out_projection(attn, rec, x, gate, attn_gain, w_out_bf16, ln_g, ln_b, *, alpha, tm):
    B, S, D = x.shape
    row = lambda w: pl.BlockSpec((None, tm, w), lambda b, i: (b, i, 0))
    vec = lambda w: pl.BlockSpec((1, w), lambda b, i: (0, 0))
    return pl.pallas_call(
        functools.partial(_outproj_kernel, alpha=alpha),
        grid=(B, S // tm),
        in_specs=[row(ATTN_WIDTH), row(REC_WIDTH), row(D),
                  pl.BlockSpec((None, 1, D), lambda b, i: (b, 0, 0)),
                  vec(ATTN_WIDTH),
                  pl.BlockSpec((ATTN_WIDTH + REC_WIDTH, D), lambda b, i: (0, 0)),
                  vec(D), vec(D)],
        out_specs=row(D),
        out_shape=jax.ShapeDtypeStruct((B, S, D), F32),
        compiler_params=_params(("arbitrary", "arbitrary")),
        name="out_projection",
    )(attn, rec, x, gate, attn_gain, w_out_bf16, ln_g, ln_b)


def _ffn_kernel(x_ref, sc_ref, sh_ref, gate_ref, wg_ref, wu_ref, wd_ref, lg_ref, lb_ref, o_ref,
                *, alpha, tc):
    x = x_ref[...]
    u = (x * (1.0 + sc_ref[...]) + sh_ref[...]).astype(BF16)
    acc = jnp.zeros(x.shape, F32)
    for c in range(wg_ref.shape[1] // tc):
        cols = slice(c * tc, (c + 1) * tc)
        g = jnp.dot(u, wg_ref[:, cols], preferred_element_type=F32)
        up = jnp.dot(u, wu_ref[:, cols], preferred_element_type=F32)
        h = (g * _sigmoid(g) * up).astype(BF16)
        acc = acc + jnp.dot(h, wd_ref[cols, :], preferred_element_type=F32)
    z = alpha * x + (1.0 + gate_ref[...]) * acc
    o_ref[...] = _layer_norm(z, lg_ref[...], lb_ref[...])


def _dense_ffn(x, scale, shift, gate, wg, wu, wd, ln_g, ln_b, *, alpha, tm, tc):
    B, S, D = x.shape
    ffp = wg.shape[1]
    assert ffp % tc == 0
    row = pl.BlockSpec((None, tm, D), lambda b, i: (b, i, 0))
    mod = pl.BlockSpec((None, 1, D), lambda b, i: (b, 0, 0))
    vec = pl.BlockSpec((1, D), lambda b, i: (0, 0))
    return pl.pallas_call(
        functools.partial(_ffn_kernel, alpha=alpha, tc=tc),
        grid=(B, S // tm),
        in_specs=[row, mod, mod, mod,
                  pl.BlockSpec((D, ffp), lambda b, i: (0, 0)),
                  pl.BlockSpec((D, ffp), lambda b, i: (0, 0)),
                  pl.BlockSpec((ffp, D), lambda b, i: (0, 0)),
                  vec, vec],
        out_specs=row,
        out_shape=jax.ShapeDtypeStruct((B, S, D), F32),
        compiler_params=_params(("arbitrary", "arbitrary")),
        name="dense_ffn",
    )(x, scale, shift, gate, wg, wu, wd, ln_g, ln_b)


META_E0, META_E1, META_W0, META_W1, META_R0, META_R1 = range(6)
SEGMENT_ALIGN = 16
DISPATCH_ROWS = 256
DISPATCH_HALF = DISPATCH_ROWS // 2
DISPATCH_SLACK = DISPATCH_HALF


def _router_kernel(x_ref, sc_ref, sh_ref, wr_ref, u_ref, meta_ref, cnt_ref, carry_ref, *, n_exp):
    tm = x_ref.shape[0]

    @pl.when((pl.program_id(0) == 0) & (pl.program_id(1) == 0))
    def _():
        carry_ref[...] = jnp.zeros_like(carry_ref)

    u = x_ref[...] * (1.0 + sc_ref[...]) + sh_ref[...]
    u_ref[...] = u.astype(BF16)
    u_hi = u.astype(BF16)
    u_lo = (u - u_hi.astype(F32)).astype(BF16)
    logits = (jnp.dot(u_hi, wr_ref[0], preferred_element_type=F32)
              + jnp.dot(u_lo, wr_ref[0], preferred_element_type=F32)
              + jnp.dot(u_hi, wr_ref[1], preferred_element_type=F32))
    lane = lax.broadcasted_iota(I32, (tm, LANES), 1)
    neg = -jnp.inf
    l1 = jnp.where(lane < n_exp, logits, neg)
    m1 = jnp.max(l1, axis=1, keepdims=True)
    i1 = jnp.min(jnp.where(l1 == m1, lane, LANES), axis=1, keepdims=True)
    l2 = jnp.where(lane == i1, neg, l1)
    m2 = jnp.max(l2, axis=1, keepdims=True)
    i2 = jnp.min(jnp.where(l2 == m2, lane, LANES), axis=1, keepdims=True)
    e = jnp.exp(m2 - m1)
    w1 = 1.0 / (1.0 + e)
    w2 = e * w1
    sel = jnp.where((lane == i1) | (lane == i2), 1.0, 0.0)
    before = (lax.broadcasted_iota(I32, (tm, tm), 1) < lax.broadcasted_iota(I32, (tm, tm), 0))
    ranks = jnp.dot(jnp.where(before, 1.0, 0.0).astype(BF16), sel.astype(BF16),
                    preferred_element_type=F32) + carry_ref[...]
    r1 = jnp.sum(jnp.where(lane == i1, ranks, 0.0), axis=1, keepdims=True)
    r2 = jnp.sum(jnp.where(lane == i2, ranks, 0.0), axis=1, keepdims=True)
    carry_ref[...] = carry_ref[...] + jnp.sum(sel, axis=0, keepdims=True)
    cnt_ref[...] = carry_ref[...]
    meta = jnp.zeros((tm, LANES), F32)
    for k, val in ((META_E0, i1.astype(F32)), (META_E1, i2.astype(F32)), (META_W0, w1),
                   (META_W1, w2), (META_R0, r1), (META_R1, r2)):
        meta = jnp.where(lane == k, val, meta)
    meta_ref[...] = meta


def _router(x, scale, shift, w_router_lanes, *, n_exp, tm):
    B, S, D = x.shape
    nt = S // tm
    row = pl.BlockSpec((None, tm, D), lambda b, i: (b, i, 0))
    mod = pl.BlockSpec((None, 1, D), lambda b, i: (b, 0, 0))
    return pl.pallas_call(
        functools.partial(_router_kernel, n_exp=n_exp),
        grid=(B, nt),
        in_specs=[row, mod, mod, pl.BlockSpec((2, D, LANES), lambda b, i: (0, 0, 0))],
        out_specs=[row,
                   pl.BlockSpec((None, tm, LANES), lambda b, i: (b, i, 0)),
                   pl.BlockSpec((1, LANES), lambda b, i: (0, 0))],
        out_shape=[jax.ShapeDtypeStruct((B, S, D), BF16),
                   jax.ShapeDtypeStruct((B, S, LANES), F32),
                   jax.ShapeDtypeStruct((1, LANES), F32)],
        scratch_shapes=[pltpu.VMEM((1, LANES), F32)],
        compiler_params=_params(("arbitrary", "arbitrary")),
        name="moe_router",
    )(x, scale, shift, w_router_lanes)


def _rows_at(hbm, first_row, n_rows):
    return hbm.at[pl.ds(pl.multiple_of(first_row, SEGMENT_ALIGN), n_rows)]


def _dispatch_kernel(base_ref, cnt_ref, padlo_ref, padn_ref, u_ref, meta_ref, xs_hbm,
                     stage, extra, zeros_buf, sems, extra_sem, *, n_exp):
    t = pl.program_id(0)
    last = pl.num_programs(0) - 1
    tm = u_ref.shape[0]
    R, H = DISPATCH_ROWS, DISPATCH_HALF
    slot = t % 2
    u = u_ref[...]
    e0, e1 = meta_ref[0:1, :], meta_ref[1:2, :]
    d0, d1 = meta_ref[2:3, :], meta_ref[3:4, :]
    srow = lax.broadcasted_iota(I32, (R, tm), 0)

    def main_copies(step, buf, fn):
        for e in range(n_exp):
            base = base_ref[step * n_exp + e]
            fn(pltpu.make_async_copy(stage.at[buf, pl.ds(e * R, H)], _rows_at(xs_hbm, base, H),
                                     sems.at[buf]))

            @pl.when(cnt_ref[step * n_exp + e] > H)
            def _(e=e, base=base):
                fn(pltpu.make_async_copy(stage.at[buf, pl.ds(e * R + H, H)],
                                         _rows_at(xs_hbm, base + H, H), sems.at[buf]))

    rels = []
    for e in range(n_exp):
        rel = jnp.where(e0 == e, d0, jnp.where(e1 == e, d1, -1))
        rels.append(jnp.where(rel >= 0, rel - base_ref[t * n_exp + e], -1))
    onehot = jnp.concatenate([jnp.where(srow == rel, 1.0, 0.0) for rel in rels], axis=0).astype(BF16)
    stage[slot] = jnp.dot(onehot, u, preferred_element_type=F32).astype(BF16)

    @pl.when(t > 0)
    def _():
        main_copies(t - 1, 1 - slot, lambda cp: cp.wait())
    main_copies(t, slot, lambda cp: cp.start())

    for e in range(n_exp):
        for blk in range(1, tm // R):
            @pl.when(cnt_ref[t * n_exp + e] > blk * R)
            def _(e=e, blk=blk):
                more = jnp.where(srow + blk * R == rels[e], 1.0, 0.0).astype(BF16)
                extra[...] = jnp.dot(more, u, preferred_element_type=F32).astype(BF16)
                cp = pltpu.make_async_copy(extra, _rows_at(xs_hbm, base_ref[t * n_exp + e] + blk * R, R),
                                           extra_sem)
                cp.start()
                cp.wait()

    @pl.when(t == last)
    def _():
        main_copies(t, slot, lambda cp: cp.wait())
        zeros_buf[...] = jnp.zeros_like(zeros_buf)
        pieces = [SEGMENT_ALIGN << b for b in range((R // SEGMENT_ALIGN).bit_length() - 1)]

        def pad_copies(fn):
            for p in range(padlo_ref.shape[0]):
                lo, n = padlo_ref[p], padn_ref[p]
                whole = n // R

                def body(k, carry, lo=lo):
                    fn(pltpu.make_async_copy(zeros_buf, _rows_at(xs_hbm, lo + k * R, R), extra_sem))
                    return carry
                lax.fori_loop(0, whole, body, 0)
                rem = n - whole * R
                for sz in pieces:
                    off = lo + whole * R + (rem // (2 * sz)) * (2 * sz)
                    pl.when((rem // sz) % 2 == 1)(functools.partial(
                        lambda off, sz: fn(pltpu.make_async_copy(
                            zeros_buf.at[pl.ds(0, sz)], _rows_at(xs_hbm, off, sz), extra_sem)), off, sz))

        pad_copies(lambda cp: cp.start())
        pad_copies(lambda cp: cp.wait())


def _dispatch(u, meta_t, base_te, cnt_te, pad_lo, pad_n, *, n_rows, tm):
    T, D = u.shape
    n_exp = pad_lo.shape[0] - 1
    assert tm % DISPATCH_ROWS == 0
    grid_spec = pltpu.PrefetchScalarGridSpec(
        num_scalar_prefetch=4,
        grid=(T // tm,),
        in_specs=[pl.BlockSpec((tm, D), lambda t, *_: (t, 0)),
                  pl.BlockSpec((meta_t.shape[0], tm), lambda t, *_: (0, t))],
        out_specs=pl.BlockSpec(memory_space=pl.ANY),
        scratch_shapes=[pltpu.VMEM((2, n_exp * DISPATCH_ROWS, D), BF16),
                        pltpu.VMEM((DISPATCH_ROWS, D), BF16),
                        pltpu.VMEM((DISPATCH_ROWS, D), BF16),
                        pltpu.SemaphoreType.DMA((2,)),
                        pltpu.SemaphoreType.DMA(())],
    )
    return pl.pallas_call(
        functools.partial(_dispatch_kernel, n_exp=n_exp),
        grid_spec=grid_spec,
        out_shape=jax.ShapeDtypeStruct((n_rows, D), BF16),
        compiler_params=_params(("arbitrary",)),
        name="moe_dispatch",
    )(base_te, cnt_te, pad_lo, pad_n, u, meta_t)


def _expert_kernel(te_ref, na_ref, xs_ref, wg_ref, wu_ref, wd_ref, o_ref, acc_ref):
    i = pl.program_id(0)
    j = pl.program_id(1)
    last = pl.num_programs(1) - 1
    active = i < na_ref[0]

    @pl.when(active & (j == 0))
    def _():
        acc_ref[...] = jnp.zeros_like(acc_ref)

    @pl.when(active)
    def _():
        xb = xs_ref[...]
        g = jnp.dot(xb, wg_ref[...].astype(BF16), preferred_element_type=F32)
        up = jnp.dot(xb, wu_ref[...].astype(BF16), preferred_element_type=F32)
        h = (g * _sigmoid(g) * up).astype(BF16)
        acc_ref[...] += jnp.dot(h, wd_ref[...].astype(BF16), preferred_element_type=F32)

    @pl.when(active & (j == last))
    def _():
        o_ref[...] = acc_ref[...].astype(BF16)

    @pl.when(jnp.logical_not(active) & (j == last))
    def _():
        o_ref[...] = jnp.zeros_like(o_ref)


def _expert_ffn(xs, tile_expert, n_active, wg, wu, wd, *, tm, tf):
    n_rows, D = xs.shape
    ff = wg.shape[2]
    assert n_rows % tm == 0 and ff % tf == 0
    nj = ff // tf

    def jj(i, j, na):
        return jnp.where(i < na[0], j, nj - 1)

    def ii(i, na):
        return jnp.minimum(i, na[0] - 1)

    grid_spec = pltpu.PrefetchScalarGridSpec(
        num_scalar_prefetch=2,
        grid=(n_rows // tm, nj),
        in_specs=[pl.BlockSpec((tm, D), lambda i, j, te, na: (ii(i, na), 0)),
                  pl.BlockSpec((None, D, tf), lambda i, j, te, na: (te[i], 0, jj(i, j, na))),
                  pl.BlockSpec((None, D, tf), lambda i, j, te, na: (te[i], 0, jj(i, j, na))),
                  pl.BlockSpec((None, tf, D), lambda i, j, te, na: (te[i], jj(i, j, na), 0))],
        out_specs=pl.BlockSpec((tm, D), lambda i, j, te, na: (i, 0)),
        scratch_shapes=[pltpu.VMEM((tm, D), F32)],
    )
    return pl.pallas_call(
        _expert_kernel,
        grid_spec=grid_spec,
        out_shape=jax.ShapeDtypeStruct((n_rows, D), BF16),
        compiler_params=_params(("arbitrary", "arbitrary")),
        name="moe_experts",
    )(tile_expert, n_active, xs, wg, wu, wd)


def _combine_kernel(base_ref, off_ref, cnt_ref, ys_hbm, meta_ref, x_ref, gate_ref, lg_ref, lb_ref, o_ref,
                    stage, extra, acc_ref, sems, extra_sem, *, alpha, n_exp):
    t = pl.program_id(0) * pl.num_programs(1) + pl.program_id(1)
    n_steps = pl.num_programs(0) * pl.num_programs(1)
    tm = x_ref.shape[0]
    R = DISPATCH_ROWS
    slot = t % 2

    def main_copies(step, buf, fn):
        for e in range(n_exp):
            fn(pltpu.make_async_copy(_rows_at(ys_hbm, base_ref[step * n_exp + e], R),
                                     stage.at[buf, pl.ds(e * R, R)], sems.at[buf]))

    @pl.when(t == 0)
    def _():
        main_copies(0, 0, lambda cp: cp.start())

    @pl.when(t + 1 < n_steps)
    def _():
        main_copies(t + 1, 1 - slot, lambda cp: cp.start())

    meta = meta_ref[...]
    col = lambda k: meta[:, k:k + 1]
    e0, e1 = col(META_E0).astype(I32), col(META_E1).astype(I32)
    r0, r1 = col(META_R0).astype(I32), col(META_R1).astype(I32)
    w0, w1 = col(META_W0), col(META_W1)
    lane = lax.broadcasted_iota(I32, (tm, R), 1)

    rels, ws, picks = [], [], []
    for e in range(n_exp):
        rel = jnp.where(e0 == e, r0, jnp.where(e1 == e, r1, -1))
        rels.append(jnp.where(rel >= 0, rel - off_ref[t * n_exp + e], -1))
        ws.append(jnp.where(e0 == e, w0, jnp.where(e1 == e, w1, 0.0)))
        picks.append(jnp.where(lane == rels[e], ws[e], 0.0))
    weighted = jnp.concatenate(picks, axis=1).astype(BF16)
    main_copies(t, slot, lambda cp: cp.wait())
    acc_ref[...] = jnp.dot(weighted, stage[slot], preferred_element_type=F32)

    for e in range(n_exp):
        for blk in range(1, tm // R):
            @pl.when(cnt_ref[t * n_exp + e] > blk * R)
            def _(e=e, blk=blk):
                cp = pltpu.make_async_copy(_rows_at(ys_hbm, base_ref[t * n_exp + e] + blk * R, R),
                                           extra, extra_sem)
                cp.start()
                cp.wait()
                more = jnp.where(lane + blk * R == rels[e], ws[e], 0.0).astype(BF16)
                acc_ref[...] += jnp.dot(more, extra[...], preferred_element_type=F32)

    z = alpha * x_ref[...] + (1.0 + gate_ref[...]) * acc_ref[...]
    o_ref[...] = _layer_norm(z, lg_ref[...], lb_ref[...])


def _combine(ys, base_te, off_te, cnt_te, meta, x, gate, ln_g, ln_b, *, alpha, tm):
    B, S, D = x.shape
    nt = S // tm
    n_exp = base_te.shape[0] // (B * nt)
    row = pl.BlockSpec((None, tm, D), lambda b, i, *_: (b, i, 0))
    vec = pl.BlockSpec((1, D), lambda b, i, *_: (0, 0))
    grid_spec = pltpu.PrefetchScalarGridSpec(
        num_scalar_prefetch=3,
        grid=(B, nt),
        in_specs=[pl.BlockSpec(memory_space=pl.ANY),
                  pl.BlockSpec((None, tm, LANES), lambda b, i, *_: (b, i, 0)),
                  row, pl.BlockSpec((None, 1, D), lambda b, i, *_: (b, 0, 0)), vec, vec],
        out_specs=row,
        scratch_shapes=[pltpu.VMEM((2, n_exp * DISPATCH_ROWS, D), BF16),
                        pltpu.VMEM((DISPATCH_ROWS, D), BF16),
                        pltpu.VMEM((tm, D), F32),
                        pltpu.SemaphoreType.DMA((2,)),
                        pltpu.SemaphoreType.DMA(())],
    )
    return pl.pallas_call(
        functools.partial(_combine_kernel, alpha=alpha, n_exp=n_exp),
        grid_spec=grid_spec,
        out_shape=jax.ShapeDtypeStruct((B, S, D), F32),
        compiler_params=_params(("arbitrary", "arbitrary")),
        name="moe_combine",
    )(base_te, off_te, cnt_te, ys, meta, x, gate, ln_g, ln_b)


def _moe_ffn(x, scale, shift, gate, w_router, wg, wu, wd, ln_g, ln_b, *, alpha, tm, tm_e, tf):
    B, S, D = x.shape
    T = B * S
    nt = T // tm
    n_exp = w_router.shape[1]
    wr = jnp.zeros((D, LANES), F32).at[:, :n_exp].set(w_router)
    wr_hi = wr.astype(BF16)
    wr = jnp.stack([wr_hi, (wr - wr_hi.astype(F32)).astype(BF16)])
    u, meta, counts = _router(x, scale, shift, wr, n_exp=n_exp, tm=tm)

    meta2 = meta.reshape(T, LANES)
    e0 = meta2[:, META_E0].astype(I32)
    e1 = meta2[:, META_E1].astype(I32)
    r0 = meta2[:, META_R0].astype(I32)
    r1 = meta2[:, META_R1].astype(I32)
    experts = jnp.arange(n_exp, dtype=I32)
    chosen = ((e0[:, None] == experts) | (e1[:, None] == experts)).astype(I32)
    cnt_te = chosen.reshape(nt, tm, n_exp).sum(axis=1)
    seg_te = -(-cnt_te // SEGMENT_ALIGN) * SEGMENT_ALIGN
    off_te = jnp.cumsum(cnt_te, axis=0) - cnt_te
    filled = seg_te.sum(axis=0)
    sizes = -(-(filled + DISPATCH_SLACK) // tm_e) * tm_e
    ends = jnp.cumsum(sizes)
    starts = ends - sizes
    base_te = starts[None, :] + jnp.cumsum(seg_te, axis=0) - seg_te
    shift_te = (base_te - off_te)[:, None, :]

    def rows_of(e, r):
        pick = (e[:, None] == experts).astype(I32).reshape(nt, tm, n_exp)
        return r + (pick * shift_te).sum(axis=-1).reshape(T)

    d0, d1 = rows_of(e0, r0), rows_of(e1, r1)
    bound = TOP_K * T + n_exp * ((SEGMENT_ALIGN - 1) * nt + DISPATCH_SLACK)
    n_tiles = -(-bound // tm_e) + n_exp + 1
    n_active = (ends[-1] // tm_e).astype(I32)
    tile_id = jnp.minimum(jnp.arange(n_tiles, dtype=I32), n_active - 1)
    tile_expert = jnp.sum((ends[None, :] <= (tile_id * tm_e)[:, None]).astype(I32), axis=1)
    meta_t = jnp.stack([e0, e1, d0, d1] + [jnp.zeros_like(e0)] * 4)

    xs = _dispatch(u.reshape(T, D), meta_t, base_te.reshape(-1), cnt_te.reshape(-1),
                   jnp.concatenate([starts + filled, ends[-1:]]),
                   jnp.concatenate([sizes - filled, n_tiles * tm_e - ends[-1:]]),
                   n_rows=n_tiles * tm_e, tm=tm)
    ys = _expert_ffn(xs, tile_expert, n_active.reshape(1), wg, wu, wd, tm=tm_e, tf=tf)
    return _combine(ys, base_te.reshape(-1), off_te.reshape(-1), cnt_te.reshape(-1),
                    meta, x, gate, ln_g, ln_b, alpha=alpha, tm=tm)


def kernel(x, c, positions, w_in, w_out, attn_norm_gain, rec_norm_gain, rec_lb_logits, ada_w, ada_b,
           ln_gain, ln_bias, ffn_w_gate, ffn_w_up, ffn_w_down, moe_router, moe_w_gate, moe_w_up,
           moe_w_down):
    B, S, D = x.shape
    depth = w_in.shape[0]
    alpha = (2 * depth) ** 0.25

    p = jax.nn.softmax(rec_lb_logits.astype(F32), axis=0)
    cum = jnp.cumsum(p, axis=0)
    lb_all = cum - cum[0:1]
    half = ATTN_HEAD_DIM // 2
    inv_freq = ROPE_THETA ** (-jnp.arange(half, dtype=F32) / half)
    inv_freq_lanes = jnp.tile(inv_freq, LANES // half).reshape(1, LANES)
    pos_f = positions.astype(F32).reshape(B, S, 1)
    n_qk = 2 * ATTN_WIDTH
    w_in_b = jnp.concatenate([_qk_lane_order(w_in[..., :n_qk]), w_in[..., n_qk:]], axis=-1).astype(BF16)
    w_out_b = w_out.astype(BF16)
    ff = ffn_w_gate.shape[2]
    ffp = -(-ff // (2 * LANES)) * (2 * LANES)
    pad_c = lambda w: jnp.pad(w.astype(BF16), ((0, 0), (0, 0), (0, ffp - ff)))
    ffn_g, ffn_u = pad_c(ffn_w_gate), pad_c(ffn_w_up)
    ffn_d = jnp.pad(ffn_w_down.astype(BF16), ((0, 0), (0, ffp - ff), (0, 0)))

    mods = _ada_modulation(c, ada_w, ada_b)
    rope_cos, rope_sin = _rope_tables(pos_f, inv_freq_lanes, tm=min(1024, S))

    def mod(layer, sub):
        m = mods[layer * 2 + sub].reshape(B, 1, 3 * D)
        return m[..., :D], m[..., D:2 * D], m[..., 2 * D:]

    vec = lambda a: a.reshape(1, -1)
    for layer in range(depth):
        shift, scale, gate = mod(layer, 0)
        proj = _in_projection(x, scale, shift, rope_cos, rope_sin, w_in_b[layer], tm=min(1024, S))
        attn = _dilated_attention(proj)
        rec = _hgrn2(proj, lb_all[layer], vec(rec_norm_gain[layer]))
        x = _out_projection(attn, rec, x, gate, vec(attn_norm_gain[layer]), w_out_b[layer],
                            vec(ln_gain[layer, 0]), vec(ln_bias[layer, 0]), alpha=alpha, tm=min(512, S))
        shift, scale, gate = mod(layer, 1)
        j = layer // 2
        if layer % 2 == 0:
            x = _dense_ffn(x, scale, shift, gate, ffn_g[j], ffn_u[j], ffn_d[j],
                           vec(ln_gain[layer, 1]), vec(ln_bias[layer, 1]),
                           alpha=alpha, tm=min(512, S), tc=2 * LANES)
        else:
            x = _moe_ffn(x, scale, shift, gate, moe_router[j], moe_w_gate[j], moe_w_up[j],
                         moe_w_down[j], vec(ln_gain[layer, 1]), vec(ln_bias[layer, 1]),
                         alpha=alpha, tm=min(512, S), tm_e=min(768, B * S), tf=512)
    return x
```

```python
import functools

import jax
import jax.numpy as jnp
from jax import lax
from jax.experimental import pallas as pl
from jax.experimental.pallas import tpu as pltpu

F32 = jnp.float32
BF16 = jnp.bfloat16
I32 = jnp.int32

LANES = 128
V7X_VMEM_LIMIT_BYTES = 56 * 1024 * 1024

ATTN_WIDTH = 512
ATTN_HEAD_DIM = 64
DILATIONS = (1, 4, 16)
ATTN_RADIUS = 64
ROPE_THETA = 10000.0
REC_WIDTH = 512
REC_DIM = 128
REC_CHUNK = 64
TOP_K = 2
LN_EPS = 1e-5
RMS_EPS = 1e-6
MASK_VALUE = -1e30
DECAY_EXP_CLAMP = 80.0
LOG2_E = 1.4426950408889634


def _params(semantics):
    return pltpu.CompilerParams(dimension_semantics=semantics,
                                vmem_limit_bytes=V7X_VMEM_LIMIT_BYTES)


def _sigmoid(x):
    return 1.0 / (1.0 + jnp.exp(-x))


def _layer_norm(z, gain, bias):
    mu = jnp.mean(z, axis=-1, keepdims=True)
    zc = z - mu
    var = jnp.mean(zc * zc, axis=-1, keepdims=True)
    return zc * lax.rsqrt(var + LN_EPS) * gain + bias


def _ada_kernel(c_ref, w_ref, b_ref, o_ref):
    c = c_ref[...]
    o_ref[...] = jnp.dot(c * _sigmoid(c), w_ref[...], precision=lax.Precision.HIGHEST,
                         preferred_element_type=F32) + b_ref[...]


def _ada_modulation(c, ada_w, ada_b):
    B, D = c.shape
    n = ada_w.shape[0] * ada_w.shape[1]
    n3 = ada_w.shape[-1]
    tn = 1024
    return pl.pallas_call(
        _ada_kernel,
        grid=(n, n3 // tn),
        in_specs=[pl.BlockSpec((B, D), lambda i, j: (0, 0)),
                  pl.BlockSpec((None, D, tn), lambda i, j: (i, 0, j)),
                  pl.BlockSpec((None, 1, tn), lambda i, j: (i, 0, j))],
        out_specs=pl.BlockSpec((None, B, tn), lambda i, j: (i, 0, j)),
        out_shape=jax.ShapeDtypeStruct((n, B, n3), F32),
        compiler_params=_params(("arbitrary", "arbitrary")),
        name="ada_modulation",
    )(c, ada_w.reshape(n, D, n3), ada_b.reshape(n, 1, n3))


ROPE_HALF = ATTN_HEAD_DIM // 2


def _qk_lane_order(w_cols):
    lead = w_cols.shape[:-1]
    w = w_cols.reshape(*lead, -1, 2, 2, ROPE_HALF)
    return jnp.swapaxes(w, -3, -2).reshape(*lead, -1)


def _rope_kernel(pos_ref, invf_ref, cos_ref, sin_ref):
    ang = pos_ref[...] * invf_ref[...]
    lane = lax.broadcasted_iota(I32, (1, LANES), 1)
    sin = jnp.sin(ang)
    cos_ref[...] = jnp.cos(ang)
    sin_ref[...] = jnp.where(lane < LANES // 2, -sin, sin)


def _rope_tables(pos_f, inv_freq_lanes, *, tm):
    B, S, _ = pos_f.shape
    tab = pl.BlockSpec((None, tm, LANES), lambda b, i: (b, i, 0))
    return pl.pallas_call(
        _rope_kernel,
        grid=(B, S // tm),
        in_specs=[pl.BlockSpec((None, tm, 1), lambda b, i: (b, i, 0)),
                  pl.BlockSpec((1, LANES), lambda b, i: (0, 0))],
        out_specs=[tab, tab],
        out_shape=[jax.ShapeDtypeStruct((B, S, LANES), F32)] * 2,
        compiler_params=_params(("arbitrary", "arbitrary")),
        name="rope_tables",
    )(pos_f, inv_freq_lanes)


def _inproj_kernel(x_ref, sc_ref, sh_ref, cos_ref, sin_ref, w_ref, o_ref, u_ref, *, tn):
    j = pl.program_id(2)

    @pl.when(j == 0)
    def _():
        u_ref[...] = (x_ref[...] * (1.0 + sc_ref[...]) + sh_ref[...]).astype(BF16)

    cols = pl.ds(pl.multiple_of(j * tn, tn), tn)
    acc = jnp.dot(u_ref[...], w_ref[:, cols], preferred_element_type=F32)

    @pl.when(j > 0)
    def _():
        o_ref[...] = acc

    @pl.when(j == 0)
    def _():
        cos = cos_ref[...]
        sin_signed = sin_ref[...]
        q_scale = ATTN_HEAD_DIM ** -0.5
        for kk in range(tn // LANES):
            c = acc[:, kk * LANES:(kk + 1) * LANES]
            r = c * cos + pltpu.roll(c, LANES // 2, 1) * sin_signed
            if kk * LANES < ATTN_WIDTH:
                r = r * q_scale
            o_ref[:, kk * LANES:(kk + 1) * LANES] = r


def _in_projection(x, scale, shift, rope_cos, rope_sin, w_in_bf16, *, tm):
    B, S, D = x.shape
    n_cols = w_in_bf16.shape[1]
    tn = 2 * ATTN_WIDTH
    assert S % tm == 0 and n_cols % tn == 0
    return pl.pallas_call(
        functools.partial(_inproj_kernel, tn=tn),
        grid=(B, S // tm, n_cols // tn),
        in_specs=[pl.BlockSpec((None, tm, D), lambda b, i, j: (b, i, 0)),
                  pl.BlockSpec((None, 1, D), lambda b, i, j: (b, 0, 0)),
                  pl.BlockSpec((None, 1, D), lambda b, i, j: (b, 0, 0)),
                  pl.BlockSpec((None, tm, LANES), lambda b, i, j: (b, i, 0)),
                  pl.BlockSpec((None, tm, LANES), lambda b, i, j: (b, i, 0)),
                  pl.BlockSpec((D, n_cols), lambda b, i, j: (0, 0))],
        out_specs=pl.BlockSpec((None, tm, tn), lambda b, i, j: (b, i, j)),
        out_shape=jax.ShapeDtypeStruct((B, S, n_cols), F32),
        scratch_shapes=[pltpu.VMEM((tm, D), BF16)],
        compiler_params=_params(("arbitrary", "arbitrary", "arbitrary")),
        name="in_projection",
    )(x, scale, shift, rope_cos, rope_sin, w_in_bf16)


ATTN_BLOCK_GROUP = 8


def _attn_kernel(q_ref, k_ref, v_ref, o_ref, qs, ks, vs, qf, kf, vf, pm, plr, pa, nm, nl, na, bias_s,
                 *, S):
    lane = lax.broadcasted_iota(I32, (1, LANES), 1)
    head0 = lane < ATTN_HEAD_DIM
    head0_qk = (lane % ATTN_HEAD_DIM) < ROPE_HALF

    for p, d in enumerate(DILATIONS):
        L = S // d
        tq = min(128, L)
        W = min(2 * tq, L)
        nb = L // tq

        d_prev = DILATIONS[p - 1] if p else 1
        ratio, l_prev = d // d_prev, S // d_prev
        keep_f32 = 0 < p < len(DILATIONS) - 1
        for src, stage, dst in ((q_ref, qf, qs), (k_ref, kf, ks), (v_ref, vf, vs)):
            src = src if p <= 1 else stage
            for r in range(d):
                a, r_prev = divmod(r, d_prev)
                rows = pl.ds(r_prev * l_prev + a, L, stride=ratio) if ratio > 1 else pl.ds(0, L)
                val = src[rows, :]
                if keep_f32:
                    stage[r * L:(r + 1) * L, :] = val
                dst[r * L:(r + 1) * L, :] = val.astype(BF16)

        n_blocks = d * nb
        rc = lax.broadcasted_iota(I32, (tq, W), 0) - lax.broadcasted_iota(I32, (tq, W), 1)
        for which, delta in enumerate((0, ATTN_RADIUS, tq)):
            bias_s[which, pl.ds(0, tq), pl.ds(0, W)] = jnp.where(
                jnp.abs(rc + delta) <= ATTN_RADIUS, 0.0, MASK_VALUE)
        group = min(ATTN_BLOCK_GROUP, n_blocks)
        assert n_blocks % group == 0

        def block_group(gi, carry, L=L, tq=tq, W=W, nb=nb, group=group):
            q0s, k0s, scores, probs, outs, sums = [], [], [], [], [], []
            for u in range(group):
                g = gi * group + u
                r = g // nb
                n = g - r * nb
                base = r * L
                q0 = pl.multiple_of(base + n * tq, tq)
                ws = jnp.clip(n * tq - ATTN_RADIUS, 0, L - W)
                k0 = pl.multiple_of(base + ws, 16)
                qb = qs[pl.ds(q0, tq), :]
                zero = jnp.zeros_like(qb)
                q2 = jnp.concatenate([jnp.where(head0_qk, qb, zero), jnp.where(head0_qk, zero, qb)], axis=0)
                s = lax.dot_general(q2, ks[pl.ds(k0, W), :], (((1,), (1,)), ((), ())),
                                    preferred_element_type=F32)
                delta = n * tq - ws
                which = jnp.where(delta == 0, 0, jnp.where(delta == ATTN_RADIUS, 1, 2))
                bias = bias_s[which, pl.ds(0, tq), pl.ds(0, W)]
                scores.append(s + jnp.concatenate([bias, bias], axis=0))
                q0s.append(q0)
                k0s.append(k0)
            for u in range(group):
                s = scores[u]
                m = jnp.max(s, axis=1, keepdims=True)
                e = jnp.exp(s - m)
                probs.append((m, e.astype(BF16)))
                sums.append(jnp.sum(e, axis=1, keepdims=True))
            for u in range(group):
                outs.append(jnp.dot(probs[u][1], vs[pl.ds(k0s[u], W), :], preferred_element_type=F32))
            for u in range(group):
                m, l = probs[u][0], sums[u]
                rows = pl.ds(q0s[u], tq)
                out_m[rows, :] = jnp.where(head0, m[:tq], m[tq:])
                out_l[rows, :] = jnp.where(head0, l[:tq], l[tq:])
                out_a[rows, :] = jnp.where(head0, outs[u][:tq], outs[u][tq:])
            return carry

        out_m, out_l, out_a = (nm.at[p], nl.at[p], na.at[p]) if d == 1 else (pm, plr, pa)
        lax.fori_loop(0, n_blocks // group, block_group, 0)

        for r in range(d if d > 1 else 0):
            rows = pl.ds(r, L, stride=d)
            nm[p, rows, :] = pm[r * L:(r + 1) * L, :]
            nl[p, rows, :] = plr[r * L:(r + 1) * L, :]
            na[p, rows, :] = pa[r * L:(r + 1) * L, :]

    m_all = jnp.maximum(jnp.maximum(nm[0], nm[1]), nm[2])
    num = jnp.zeros((S, LANES), F32)
    den = jnp.zeros((S, LANES), F32)
    for p in range(len(DILATIONS)):
        w = jnp.exp(nm[p] - m_all)
        num = num + w * na[p]
        den = den + w * nl[p]
    o_ref[...] = (num / den).astype(o_ref.dtype)


def _dilated_attention(proj):
    B, S, _ = proj.shape
    n_pairs = ATTN_WIDTH // LANES
    assert S % (16 * DILATIONS[-1]) == 0
    blk = lambda off: pl.BlockSpec((None, S, LANES), lambda b, h: (b, 0, off + h))
    return pl.pallas_call(
        functools.partial(_attn_kernel, S=S),
        grid=(B, n_pairs),
        in_specs=[blk(0), blk(n_pairs), blk(2 * n_pairs)],
        out_specs=pl.BlockSpec((None, S, LANES), lambda b, h: (b, 0, h)),
        out_shape=jax.ShapeDtypeStruct((B, S, ATTN_WIDTH), BF16),
        scratch_shapes=[pltpu.VMEM((S, LANES), BF16)] * 3
                       + [pltpu.VMEM((S, LANES), F32)] * 6
                       + [pltpu.VMEM((len(DILATIONS), S, LANES), F32)] * 3
                       + [pltpu.VMEM((3, min(128, S), min(256, S)), F32)],
        compiler_params=_params(("arbitrary", "arbitrary")),
        name="dilated_attention",
    )(proj, proj, proj)


REC_GROUP_CHUNKS = 4
REC_GROUP_ROWS = REC_GROUP_CHUNKS * REC_CHUNK
REC_STAGE_GROUPS = 4


def _hgrn_kernel(rq_ref, zf_ref, zb_ref, ri_ref, rg_ref, lb_ref, gain_ref, o_ref,
                 q_s, vt_s, oi_s, qp_s, ut_s, dd_s, st_s, os_s, *, S):
    C, G, GR = REC_CHUNK, REC_GROUP_CHUNKS, REC_GROUP_ROWS
    nc, ng = S // C, S // GR
    sg = min(REC_STAGE_GROUPS, ng)
    assert ng % sg == 0

    row = lax.broadcasted_iota(I32, (GR, GR), 0)
    col = lax.broadcasted_iota(I32, (GR, GR), 1)
    same_chunk = (row // C) == (col // C)
    t_i = lax.broadcasted_iota(I32, (C, C), 0)
    s_i = lax.broadcasted_iota(I32, (C, C), 1)

    def prep(i, carry):
        r0 = pl.multiple_of(i * GR, GR)
        rq = rq_ref[pl.ds(r0, GR), :]
        q_s[pl.ds(r0, GR), :] = rq * _sigmoid(rq)
        v = ri_ref[pl.ds(r0, GR), :]
        for c in range(G):
            vt_s[i * G + c] = v[c * C:(c + 1) * C, :].T.astype(BF16)
        return carry

    lax.fori_loop(0, ng, prep, 0)

    for direction in range(2):
        fwd = direction == 0
        z_ref = zf_ref if fwd else zb_ref
        cum = jnp.where(same_chunk & ((col <= row) if fwd else (col >= row)), 1.0, 0.0).astype(BF16)
        keep = (s_i <= t_i) if fwd else (s_i >= t_i)
        mid_row = C // 2 - 1 if fwd else C // 2
        last_row = C - 1 if fwd else 0

        def phase_a(it, carry, fwd=fwd, z_ref=z_ref, cum=cum, keep=keep,
                    mid_row=mid_row, last_row=last_row, direction=direction):
            lb = lb_ref[direction:direction + 1, :]
            log_lb = jnp.log(lb)
            log_1m_lb = jnp.log1p(-lb)
            r0s, c0s, k3s, bcs, atts, stage3 = [], [], [], [], [], []
            for u in range(sg):
                i = it * sg + u
                r0 = pl.multiple_of(i * GR, GR)
                z = z_ref[pl.ds(r0, GR), :]
                e = jnp.exp(-jnp.abs(z))
                t = log_1m_lb + (jnp.minimum(z, 0.0) - jnp.log(1.0 + e))
                g = jnp.maximum(log_lb, t) + jnp.log(1.0 + jnp.exp(-jnp.abs(log_lb - t)))
                kk = (1.0 - lb) * (jnp.where(z > 0, e, 1.0) / (1.0 + e))
                g = g * LOG2_E
                g1 = g.astype(BF16)
                r1 = g - g1.astype(F32)
                g2 = r1.astype(BF16)
                g3 = (r1 - g2.astype(F32)).astype(BF16)
                bcs.append(jnp.dot(cum, g1, preferred_element_type=F32)
                           + jnp.dot(cum, g2, preferred_element_type=F32)
                           + jnp.dot(cum, g3, preferred_element_type=F32))
                k3s.append(kk.reshape(G, C, REC_DIM))
                r0s.append(r0)
                c0s.append(pl.multiple_of(i * G, G))
            for u in range(sg):
                b3 = bcs[u].reshape(G, C, REC_DIM)
                mid = b3[:, mid_row:mid_row + 1, :]
                last = b3[:, last_row:last_row + 1, :]
                e3 = b3 - mid
                q3 = q_s[pl.ds(r0s[u], GR), :].reshape(G, C, REC_DIM)
                clamp = DECAY_EXP_CLAMP * LOG2_E
                qt = (q3 * jnp.exp2(jnp.minimum(e3, clamp))).astype(BF16)
                kt = (k3s[u] * jnp.exp2(jnp.minimum(-e3, clamp))).astype(BF16)
                atts.append(jnp.einsum('gtc,gsc->gts', qt, kt, preferred_element_type=F32))
                qp_s[pl.ds(r0s[u], GR), :] = (q3 * jnp.exp2(b3)).astype(BF16).reshape(GR, REC_DIM)
                stage3.append(((k3s[u] * jnp.exp2(last - b3)).astype(BF16), jnp.exp2(last)))
            for u in range(sg):
                att = jnp.where(keep[None], atts[u], 0.0).astype(BF16)
                v3 = ri_ref[pl.ds(r0s[u], GR), :].reshape(G, C, REC_DIM).astype(BF16)
                oi = jnp.einsum('gts,gsv->gtv', att, v3, preferred_element_type=F32)
                oi_s[pl.ds(r0s[u], GR), :] = oi.reshape(GR, REC_DIM)
                kp, decay = stage3[u]
                ut_s[pl.ds(c0s[u], G)] = jnp.einsum('gvs,gsc->gvc', vt_s[pl.ds(c0s[u], G)], kp,
                                                    preferred_element_type=F32)
                dd_s[pl.ds(c0s[u], G)] = decay
            return carry

        lax.fori_loop(0, ng // sg, phase_a, 0)

        def scan(c, st, fwd=fwd):
            idx = c if fwd else nc - 1 - c
            st_s[idx] = st.astype(BF16)
            return dd_s[idx] * st + ut_s[idx]

        lax.fori_loop(0, nc, scan, jnp.zeros((REC_DIM, REC_DIM), F32))

        def phase_c(it, carry, fwd=fwd):
            r0s = [pl.multiple_of((it * sg + u) * GR, GR) for u in range(sg)]
            oos = []
            for u in range(sg):
                c0 = pl.multiple_of((it * sg + u) * G, G)
                qp3 = qp_s[pl.ds(r0s[u], GR), :].reshape(G, C, REC_DIM)
                oos.append(jnp.einsum('gtc,gvc->gtv', qp3, st_s[pl.ds(c0, G)],
                                      preferred_element_type=F32))
            for u in range(sg):
                rows = pl.ds(r0s[u], GR)
                tot = oi_s[rows, :] + oos[u].reshape(GR, REC_DIM)
                os_s[rows, :] = tot if fwd else os_s[rows, :] + tot
            return carry

        lax.fori_loop(0, ng // sg, phase_c, 0)

    o = os_s[...]
    o = o * lax.rsqrt(jnp.mean(o * o, axis=-1, keepdims=True) + RMS_EPS) * gain_ref[...]
    o_ref[...] = (o * _sigmoid(rg_ref[...])).astype(o_ref.dtype)


def _hgrn2(proj, lb, rec_gain):
    B, S, _ = proj.shape
    nh = REC_WIDTH // REC_DIM
    assert S % REC_GROUP_ROWS == 0
    col0 = 3 * ATTN_WIDTH // LANES
    blk = lambda k: pl.BlockSpec((None, S, REC_DIM), lambda b, h: (b, 0, col0 + k * nh + h))
    nc = S // REC_CHUNK
    return pl.pallas_call(
        functools.partial(_hgrn_kernel, S=S),
        grid=(B, nh),
        in_specs=[blk(0), blk(1), blk(2), blk(3), blk(4),
                  pl.BlockSpec((2, REC_DIM), lambda b, h: (0, h)),
                  pl.BlockSpec((1, REC_DIM), lambda b, h: (0, h))],
        out_specs=pl.BlockSpec((None, S, REC_DIM), lambda b, h: (b, 0, h)),
        out_shape=jax.ShapeDtypeStruct((B, S, REC_WIDTH), BF16),
        scratch_shapes=[pltpu.VMEM((S, REC_DIM), F32),
                        pltpu.VMEM((nc, REC_DIM, REC_CHUNK), BF16),
                        pltpu.VMEM((S, REC_DIM), F32),
                        pltpu.VMEM((S, REC_DIM), BF16),
                        pltpu.VMEM((nc, REC_DIM, REC_DIM), F32),
                        pltpu.VMEM((nc, 1, REC_DIM), F32),
                        pltpu.VMEM((nc, REC_DIM, REC_DIM), BF16),
                        pltpu.VMEM((S, REC_DIM), F32)],
        compiler_params=_params(("arbitrary", "arbitrary")),
        name="hgrn2",
    )(proj, proj, proj, proj, proj, lb, rec_gain)


def _outproj_kernel(a_ref, r_ref, x_ref, gate_ref, ag_ref, w_ref, lg_ref, lb_ref, o_ref, *, alpha):
    a = a_ref[...].astype(F32)
    an = a * lax.rsqrt(jnp.mean(a * a, axis=-1, keepdims=True) + RMS_EPS) * ag_ref[...]
    y = (jnp.dot(an.astype(BF16), w_ref[:ATTN_WIDTH, :], preferred_element_type=F32)
         + jnp.dot(r_ref[...], w_ref[ATTN_WIDTH:, :], preferred_element_type=F32))
    z = alpha * x_ref[...] + (1.0 + gate_ref[...]) * y
    o_ref[...] = _layer_norm(z, lg_ref[...], lb_ref[...])


def _out_projection(attn, rec, x, gate, attn_gain, w_out_bf16, ln_g, ln_b, *, alpha, tm):
    B, S, D = x.shape
    row = lambda w: pl.BlockSpec((None, tm, w), lambda b, i: (b, i, 0))
    vec = lambda w: pl.BlockSpec((1, w), lambda b, i: (0, 0))
    return pl.pallas_call(
        functools.partial(_outproj_kernel, alpha=alpha),
        grid=(B, S // tm),
        in_specs=[row(ATTN_WIDTH), row(REC_WIDTH), row(D),
                  pl.BlockSpec((None, 1, D), lambda b, i: (b, 0, 0)),
                  vec(ATTN_WIDTH),
                  pl.BlockSpec((ATTN_WIDTH + REC_WIDTH, D), lambda b, i: (0, 0)),
                  vec(D), vec(D)],
        out_specs=row(D),
        out_shape=jax.ShapeDtypeStruct((B, S, D), F32),
        compiler_params=_params(("arbitrary", "arbitrary")),
        name="out_projection",
    )(attn, rec, x, gate, attn_gain, w_out_bf16, ln_g, ln_b)


def _ffn_kernel(x_ref, sc_ref, sh_ref, gate_ref, wg_ref, wu_ref, wd_ref, lg_ref, lb_ref, o_ref,
                *, alpha, tc):
    x = x_ref[...]
    u = (x * (1.0 + sc_ref[...]) + sh_ref[...]).astype(BF16)
    acc = jnp.zeros(x.shape, F32)
    for c in range(wg_ref.shape[1] // tc):
        cols = slice(c * tc, (c + 1) * tc)
        g = jnp.dot(u, wg_ref[:, cols], preferred_element_type=F32)
        up = jnp.dot(u, wu_ref[:, cols], preferred_element_type=F32)
        h = (g * _sigmoid(g) * up).astype(BF16)
        acc = acc + jnp.dot(h, wd_ref[cols, :], preferred_element_type=F32)
    z = alpha * x + (1.0 + gate_ref[...]) * acc
    o_ref[...] = _layer_norm(z, lg_ref[...], lb_ref[...])


def _dense_ffn(x, scale, shift, gate, wg, wu, wd, ln_g, ln_b, *, alpha, tm, tc):
    B, S, D = x.shape
    ffp = wg.shape[1]
    assert ffp % tc == 0
    row = pl.BlockSpec((None, tm, D), lambda b, i: (b, i, 0))
    mod = pl.BlockSpec((None, 1, D), lambda b, i: (b, 0, 0))
    vec = pl.BlockSpec((1, D), lambda b, i: (0, 0))
    return pl.pallas_call(
        functools.partial(_ffn_kernel, alpha=alpha, tc=tc),
        grid=(B, S // tm),
        in_specs=[row, mod, mod, mod,
                  pl.BlockSpec((D, ffp), lambda b, i: (0, 0)),
                  pl.BlockSpec((D, ffp), lambda b, i: (0, 0)),
                  pl.BlockSpec((ffp, D), lambda b, i: (0, 0)),
                  vec, vec],
        out_specs=row,
        out_shape=jax.ShapeDtypeStruct((B, S, D), F32),
        compiler_params=_params(("arbitrary", "arbitrary")),
        name="dense_ffn",
    )(x, scale, shift, gate, wg, wu, wd, ln_g, ln_b)


META_E0, META_E1, META_W0, META_W1, META_R0, META_R1 = range(6)
SEGMENT_ALIGN = 16
DISPATCH_ROWS = 256
DISPATCH_HALF = DISPATCH_ROWS // 2
DISPATCH_SLACK = DISPATCH_HALF


def _router_kernel(x_ref, sc_ref, sh_ref, wr_ref, u_ref, meta_ref, cnt_ref, carry_ref, *, n_exp):
    tm = x_ref.shape[0]

    @pl.when((pl.program_id(0) == 0) & (pl.program_id(1) == 0))
    def _():
        carry_ref[...] = jnp.zeros_like(carry_ref)

    u = x_ref[...] * (1.0 + sc_ref[...]) + sh_ref[...]
    u_ref[...] = u.astype(BF16)
    u_hi = u.astype(BF16)
    u_lo = (u - u_hi.astype(F32)).astype(BF16)
    logits = (jnp.dot(u_hi, wr_ref[0], preferred_element_type=F32)
              + jnp.dot(u_lo, wr_ref[0], preferred_element_type=F32)
              + jnp.dot(u_hi, wr_ref[1], preferred_element_type=F32))
    lane = lax.broadcasted_iota(I32, (tm, LANES), 1)
    neg = -jnp.inf
    l1 = jnp.where(lane < n_exp, logits, neg)
    m1 = jnp.max(l1, axis=1, keepdims=True)
    i1 = jnp.min(jnp.where(l1 == m1, lane, LANES), axis=1, keepdims=True)
    l2 = jnp.where(lane == i1, neg, l1)
    m2 = jnp.max(l2, axis=1, keepdims=True)
    i2 = jnp.min(jnp.where(l2 == m2, lane, LANES), axis=1, keepdims=True)
    e = jnp.exp(m2 - m1)
    w1 = 1.0 / (1.0 + e)
    w2 = e * w1
    sel = jnp.where((lane == i1) | (lane == i2), 1.0, 0.0)
    before = (lax.broadcasted_iota(I32, (tm, tm), 1) < lax.broadcasted_iota(I32, (tm, tm), 0))
    ranks = jnp.dot(jnp.where(before, 1.0, 0.0).astype(BF16), sel.astype(BF16),
                    preferred_element_type=F32) + carry_ref[...]
    r1 = jnp.sum(jnp.where(lane == i1, ranks, 0.0), axis=1, keepdims=True)
    r2 = jnp.sum(jnp.where(lane == i2, ranks, 0.0), axis=1, keepdims=True)
    carry_ref[...] = carry_ref[...] + jnp.sum(sel, axis=0, keepdims=True)
    cnt_ref[...] = carry_ref[...]
    meta = jnp.zeros((tm, LANES), F32)
    for k, val in ((META_E0, i1.astype(F32)), (META_E1, i2.astype(F32)), (META_W0, w1),
                   (META_W1, w2), (META_R0, r1), (META_R1, r2)):
        meta = jnp.where(lane == k, val, meta)
    meta_ref[...] = meta


def _router(x, scale, shift, w_router_lanes, *, n_exp, tm):
    B, S, D = x.shape
    nt = S // tm
    row = pl.BlockSpec((None, tm, D), lambda b, i: (b, i, 0))
    mod = pl.BlockSpec((None, 1, D), lambda b, i: (b, 0, 0))
    return pl.pallas_call(
        functools.partial(_router_kernel, n_exp=n_exp),
        grid=(B, nt),
        in_specs=[row, mod, mod, pl.BlockSpec((2, D, LANES), lambda b, i: (0, 0, 0))],
        out_specs=[row,
                   pl.BlockSpec((None, tm, LANES), lambda b, i: (b, i, 0)),
                   pl.BlockSpec((1, LANES), lambda b, i: (0, 0))],
        out_shape=[jax.ShapeDtypeStruct((B, S, D), BF16),
                   jax.ShapeDtypeStruct((B, S, LANES), F32),
                   jax.ShapeDtypeStruct((1, LANES), F32)],
        scratch_shapes=[pltpu.VMEM((1, LANES), F32)],
        compiler_params=_params(("arbitrary", "arbitrary")),
        name="moe_router",
    )(x, scale, shift, w_router_lanes)


def _rows_at(hbm, first_row, n_rows):
    return hbm.at[pl.ds(pl.multiple_of(first_row, SEGMENT_ALIGN), n_rows)]


def _dispatch_kernel(base_ref, cnt_ref, padlo_ref, padn_ref, u_ref, meta_ref, xs_hbm,
                     stage, extra, zeros_buf, sems, extra_sem, *, n_exp):
    t = pl.program_id(0)
    last = pl.num_programs(0) - 1
    tm = u_ref.shape[0]
    R, H = DISPATCH_ROWS, DISPATCH_HALF
    slot = t % 2
    u = u_ref[...]
    e0, e1 = meta_ref[0:1, :], meta_ref[1:2, :]
    d0, d1 = meta_ref[2:3, :], meta_ref[3:4, :]
    srow = lax.broadcasted_iota(I32, (R, tm), 0)

    def main_copies(step, buf, fn):
        for e in range(n_exp):
            base = base_ref[step * n_exp + e]
            fn(pltpu.make_async_copy(stage.at[buf, pl.ds(e * R, H)], _rows_at(xs_hbm, base, H),
                                     sems.at[buf]))

            @pl.when(cnt_ref[step * n_exp + e] > H)
            def _(e=e, base=base):
                fn(pltpu.make_async_copy(stage.at[buf, pl.ds(e * R + H, H)],
                                         _rows_at(xs_hbm, base + H, H), sems.at[buf]))

    rels = []
    for e in range(n_exp):
        rel = jnp.where(e0 == e, d0, jnp.where(e1 == e, d1, -1))
        rels.append(jnp.where(rel >= 0, rel - base_ref[t * n_exp + e], -1))
    onehot = jnp.concatenate([jnp.where(srow == rel, 1.0, 0.0) for rel in rels], axis=0).astype(BF16)
    stage[slot] = jnp.dot(onehot, u, preferred_element_type=F32).astype(BF16)

    @pl.when(t > 0)
    def _():
        main_copies(t - 1, 1 - slot, lambda cp: cp.wait())
    main_copies(t, slot, lambda cp: cp.start())

    for e in range(n_exp):
        for blk in range(1, tm // R):
            @pl.when(cnt_ref[t * n_exp + e] > blk * R)
            def _(e=e, blk=blk):
                more = jnp.where(srow + blk * R == rels[e], 1.0, 0.0).astype(BF16)
                extra[...] = jnp.dot(more, u, preferred_element_type=F32).astype(BF16)
                cp = pltpu.make_async_copy(extra, _rows_at(xs_hbm, base_ref[t * n_exp + e] + blk * R, R),
                                           extra_sem)
                cp.start()
                cp.wait()

    @pl.when(t == last)
    def _():
        main_copies(t, slot, lambda cp: cp.wait())
        zeros_buf[...] = jnp.zeros_like(zeros_buf)
        pieces = [SEGMENT_ALIGN << b for b in range((R // SEGMENT_ALIGN).bit_length() - 1)]

        def pad_copies(fn):
            for p in range(padlo_ref.shape[0]):
                lo, n = padlo_ref[p], padn_ref[p]
                whole = n // R

                def body(k, carry, lo=lo):
                    fn(pltpu.make_async_copy(zeros_buf, _rows_at(xs_hbm, lo + k * R, R), extra_sem))
                    return carry
                lax.fori_loop(0, whole, body, 0)
                rem = n - whole * R
                for sz in pieces:
                    off = lo + whole * R + (rem // (2 * sz)) * (2 * sz)
                    pl.when((rem // sz) % 2 == 1)(functools.partial(
                        lambda off, sz: fn(pltpu.make_async_copy(
                            zeros_buf.at[pl.ds(0, sz)], _rows_at(xs_hbm, off, sz), extra_sem)), off, sz))

        pad_copies(lambda cp: cp.start())
        pad_copies(lambda cp: cp.wait())


def _dispatch(u, meta_t, base_te, cnt_te, pad_lo, pad_n, *, n_rows, tm):
    T, D = u.shape
    n_exp = pad_lo.shape[0] - 1
    assert tm % DISPATCH_ROWS == 0
    grid_spec = pltpu.PrefetchScalarGridSpec(
        num_scalar_prefetch=4,
        grid=(T // tm,),
        in_specs=[pl.BlockSpec((tm, D), lambda t, *_: (t, 0)),
                  pl.BlockSpec((meta_t.shape[0], tm), lambda t, *_: (0, t))],
        out_specs=pl.BlockSpec(memory_space=pl.ANY),
        scratch_shapes=[pltpu.VMEM((2, n_exp * DISPATCH_ROWS, D), BF16),
                        pltpu.VMEM((DISPATCH_ROWS, D), BF16),
                        pltpu.VMEM((DISPATCH_ROWS, D), BF16),
                        pltpu.SemaphoreType.DMA((2,)),
                        pltpu.SemaphoreType.DMA(())],
    )
    return pl.pallas_call(
        functools.partial(_dispatch_kernel, n_exp=n_exp),
        grid_spec=grid_spec,
        out_shape=jax.ShapeDtypeStruct((n_rows, D), BF16),
        compiler_params=_params(("arbitrary",)),
        name="moe_dispatch",
    )(base_te, cnt_te, pad_lo, pad_n, u, meta_t)


def _expert_kernel(te_ref, na_ref, xs_ref, wg_ref, wu_ref, wd_ref, o_ref, acc_ref):
    i = pl.program_id(0)
    j = pl.program_id(1)
    last = pl.num_programs(1) - 1
    active = i < na_ref[0]

    @pl.when(active & (j == 0))
    def _():
        acc_ref[...] = jnp.zeros_like(acc_ref)

    @pl.when(active)
    def _():
        xb = xs_ref[...]
        g = jnp.dot(xb, wg_ref[...].astype(BF16), preferred_element_type=F32)
        up = jnp.dot(xb, wu_ref[...].astype(BF16), preferred_element_type=F32)
        h = (g * _sigmoid(g) * up).astype(BF16)
        acc_ref[...] += jnp.dot(h, wd_ref[...].astype(BF16), preferred_element_type=F32)

    @pl.when(active & (j == last))
    def _():
        o_ref[...] = acc_ref[...].astype(BF16)

    @pl.when(jnp.logical_not(active) & (j == last))
    def _():
        o_ref[...] = jnp.zeros_like(o_ref)


def _expert_ffn(xs, tile_expert, n_active, wg, wu, wd, *, tm, tf):
    n_rows, D = xs.shape
    ff = wg.shape[2]
    assert n_rows % tm == 0 and ff % tf == 0
    nj = ff // tf

    def jj(i, j, na):
        return jnp.where(i < na[0], j, nj - 1)

    def ii(i, na):
        return jnp.minimum(i, na[0] - 1)

    grid_spec = pltpu.PrefetchScalarGridSpec(
        num_scalar_prefetch=2,
        grid=(n_rows // tm, nj),
        in_specs=[pl.BlockSpec((tm, D), lambda i, j, te, na: (ii(i, na), 0)),
                  pl.BlockSpec((None, D, tf), lambda i, j, te, na: (te[i], 0, jj(i, j, na))),
                  pl.BlockSpec((None, D, tf), lambda i, j, te, na: (te[i], 0, jj(i, j, na))),
                  pl.BlockSpec((None, tf, D), lambda i, j, te, na: (te[i], jj(i, j, na), 0))],
        out_specs=pl.BlockSpec((tm, D), lambda i, j, te, na: (i, 0)),
        scratch_shapes=[pltpu.VMEM((tm, D), F32)],
    )
    return pl.pallas_call(
        _expert_kernel,
        grid_spec=grid_spec,
        out_shape=jax.ShapeDtypeStruct((n_rows, D), BF16),
        compiler_params=_params(("arbitrary", "arbitrary")),
        name="moe_experts",
    )(tile_expert, n_active, xs, wg, wu, wd)


def _combine_kernel(base_ref, off_ref, cnt_ref, ys_hbm, meta_ref, x_ref, gate_ref, lg_ref, lb_ref, o_ref,
                    stage, extra, acc_ref, sems, extra_sem, *, alpha, n_exp):
    t = pl.program_id(0) * pl.num_programs(1) + pl.program_id(1)
    n_steps = pl.num_programs(0) * pl.num_programs(1)
    tm = x_ref.shape[0]
    R = DISPATCH_ROWS
    slot = t % 2

    def main_copies(step, buf, fn):
        for e in range(n_exp):
            fn(pltpu.make_async_copy(_rows_at(ys_hbm, base_ref[step * n_exp + e], R),
                                     stage.at[buf, pl.ds(e * R, R)], sems.at[buf]))

    @pl.when(t == 0)
    def _():
        main_copies(0, 0, lambda cp: cp.start())

    @pl.when(t + 1 < n_steps)
    def _():
        main_copies(t + 1, 1 - slot, lambda cp: cp.start())

    meta = meta_ref[...]
    col = lambda k: meta[:, k:k + 1]
    e0, e1 = col(META_E0).astype(I32), col(META_E1).astype(I32)
    r0, r1 = col(META_R0).astype(I32), col(META_R1).astype(I32)
    w0, w1 = col(META_W0), col(META_W1)
    lane = lax.broadcasted_iota(I32, (tm, R), 1)

    rels, ws, picks = [], [], []
    for e in range(n_exp):
        rel = jnp.where(e0 == e, r0, jnp.where(e1 == e, r1, -1))
        rels.append(jnp.where(rel >= 0, rel - off_ref[t * n_exp + e], -1))
        ws.append(jnp.where(e0 == e, w0, jnp.where(e1 == e, w1, 0.0)))
        picks.append(jnp.where(lane == rels[e], ws[e], 0.0))
    weighted = jnp.concatenate(picks, axis=1).astype(BF16)
    main_copies(t, slot, lambda cp: cp.wait())
    acc_ref[...] = jnp.dot(weighted, stage[slot], preferred_element_type=F32)

    for e in range(n_exp):
        for blk in range(1, tm // R):
            @pl.when(cnt_ref[t * n_exp + e] > blk * R)
            def _(e=e, blk=blk):
                cp = pltpu.make_async_copy(_rows_at(ys_hbm, base_ref[t * n_exp + e] + blk * R, R),
                                           extra, extra_sem)
                cp.start()
                cp.wait()
                more = jnp.where(lane + blk * R == rels[e], ws[e], 0.0).astype(BF16)
                acc_ref[...] += jnp.dot(more, extra[...], preferred_element_type=F32)

    z = alpha * x_ref[...] + (1.0 + gate_ref[...]) * acc_ref[...]
    o_ref[...] = _layer_norm(z, lg_ref[...], lb_ref[...])


def _combine(ys, base_te, off_te, cnt_te, meta, x, gate, ln_g, ln_b, *, alpha, tm):
    B, S, D = x.shape
    nt = S // tm
    n_exp = base_te.shape[0] // (B * nt)
    row = pl.BlockSpec((None, tm, D), lambda b, i, *_: (b, i, 0))
    vec = pl.BlockSpec((1, D), lambda b, i, *_: (0, 0))
    grid_spec = pltpu.PrefetchScalarGridSpec(
        num_scalar_prefetch=3,
        grid=(B, nt),
        in_specs=[pl.BlockSpec(memory_space=pl.ANY),
                  pl.BlockSpec((None, tm, LANES), lambda b, i, *_: (b, i, 0)),
                  row, pl.BlockSpec((None, 1, D), lambda b, i, *_: (b, 0, 0)), vec, vec],
        out_specs=row,
        scratch_shapes=[pltpu.VMEM((2, n_exp * DISPATCH_ROWS, D), BF16),
                        pltpu.VMEM((DISPATCH_ROWS, D), BF16),
                        pltpu.VMEM((tm, D), F32),
                        pltpu.SemaphoreType.DMA((2,)),
                        pltpu.SemaphoreType.DMA(())],
    )
    return pl.pallas_call(
        functools.partial(_combine_kernel, alpha=alpha, n_exp=n_exp),
        grid_spec=grid_spec,
        out_shape=jax.ShapeDtypeStruct((B, S, D), F32),
        compiler_params=_params(("arbitrary", "arbitrary")),
        name="moe_combine",
    )(base_te, off_te, cnt_te, ys, meta, x, gate, ln_g, ln_b)


def _moe_ffn(x, scale, shift, gate, w_router, wg, wu, wd, ln_g, ln_b, *, alpha, tm, tm_e, tf):
    B, S, D = x.shape
    T = B * S
    nt = T // tm
    n_exp = w_router.shape[1]
    wr = jnp.zeros((D, LANES), F32).at[:, :n_exp].set(w_router)
    wr_hi = wr.astype(BF16)
    wr = jnp.stack([wr_hi, (wr - wr_hi.astype(F32)).astype(BF16)])
    u, meta, counts = _router(x, scale, shift, wr, n_exp=n_exp, tm=tm)

    meta2 = meta.reshape(T, LANES)
    e0 = meta2[:, META_E0].astype(I32)
    e1 = meta2[:, META_E1].astype(I32)
    r0 = meta2[:, META_R0].astype(I32)
    r1 = meta2[:, META_R1].astype(I32)
    experts = jnp.arange(n_exp, dtype=I32)
    chosen = ((e0[:, None] == experts) | (e1[:, None] == experts)).astype(I32)
    cnt_te = chosen.reshape(nt, tm, n_exp).sum(axis=1)
    seg_te = -(-cnt_te // SEGMENT_ALIGN) * SEGMENT_ALIGN
    off_te = jnp.cumsum(cnt_te, axis=0) - cnt_te
    filled = seg_te.sum(axis=0)
    sizes = -(-(filled + DISPATCH_SLACK) // tm_e) * tm_e
    ends = jnp.cumsum(sizes)
    starts = ends - sizes
    base_te = starts[None, :] + jnp.cumsum(seg_te, axis=0) - seg_te
    shift_te = (base_te - off_te)[:, None, :]

    def rows_of(e, r):
        pick = (e[:, None] == experts).astype(I32).reshape(nt, tm, n_exp)
        return r + (pick * shift_te).sum(axis=-1).reshape(T)

    d0, d1 = rows_of(e0, r0), rows_of(e1, r1)
    bound = TOP_K * T + n_exp * ((SEGMENT_ALIGN - 1) * nt + DISPATCH_SLACK)
    n_tiles = -(-bound // tm_e) + n_exp + 1
    n_active = (ends[-1] // tm_e).astype(I32)
    tile_id = jnp.minimum(jnp.arange(n_tiles, dtype=I32), n_active - 1)
    tile_expert = jnp.sum((ends[None, :] <= (tile_id * tm_e)[:, None]).astype(I32), axis=1)
    meta_t = jnp.stack([e0, e1, d0, d1] + [jnp.zeros_like(e0)] * 4)

    xs = _dispatch(u.reshape(T, D), meta_t, base_te.reshape(-1), cnt_te.reshape(-1),
                   jnp.concatenate([starts + filled, ends[-1:]]),
                   jnp.concatenate([sizes - filled, n_tiles * tm_e - ends[-1:]]),
                   n_rows=n_tiles * tm_e, tm=tm)
    ys = _expert_ffn(xs, tile_expert, n_active.reshape(1), wg, wu, wd, tm=tm_e, tf=tf)
    return _combine(ys, base_te.reshape(-1), off_te.reshape(-1), cnt_te.reshape(-1),
                    meta, x, gate, ln_g, ln_b, alpha=alpha, tm=tm)


def kernel(x, c, positions, w_in, w_out, attn_norm_gain, rec_norm_gain, rec_lb_logits, ada_w, ada_b,
           ln_gain, ln_bias, ffn_w_gate, ffn_w_up, ffn_w_down, moe_router, moe_w_gate, moe_w_up,
           moe_w_down):
    B, S, D = x.shape
    depth = w_in.shape[0]
    alpha = (2 * depth) ** 0.25

    p = jax.nn.softmax(rec_lb_logits.astype(F32), axis=0)
    cum = jnp.cumsum(p, axis=0)
    lb_all = cum - cum[0:1]
    half = ATTN_HEAD_DIM // 2
    inv_freq = ROPE_THETA ** (-jnp.arange(half, dtype=F32) / half)
    inv_freq_lanes = jnp.tile(inv_freq, LANES // half).reshape(1, LANES)
    pos_f = positions.astype(F32).reshape(B, S, 1)
    n_qk = 2 * ATTN_WIDTH
    w_in_b = jnp.concatenate([_qk_lane_order(w_in[..., :n_qk]), w_in[..., n_qk:]], axis=-1).astype(BF16)
    w_out_b = w_out.astype(BF16)
    ff = ffn_w_gate.shape[2]
    ffp = -(-ff // (2 * LANES)) * (2 * LANES)
    pad_c = lambda w: jnp.pad(w.astype(BF16), ((0, 0), (0, 0), (0, ffp - ff)))
    ffn_g, ffn_u = pad_c(ffn_w_gate), pad_c(ffn_w_up)
    ffn_d = jnp.pad(ffn_w_down.astype(BF16), ((0, 0), (0, ffp - ff), (0, 0)))

    mods = _ada_modulation(c, ada_w, ada_b)
    rope_cos, rope_sin = _rope_tables(pos_f, inv_freq_lanes, tm=min(1024, S))

    def mod(layer, sub):
        m = mods[layer * 2 + sub].reshape(B, 1, 3 * D)
        return m[..., :D], m[..., D:2 * D], m[..., 2 * D:]

    vec = lambda a: a.reshape(1, -1)
    for layer in range(depth):
        shift, scale, gate = mod(layer, 0)
        proj = _in_projection(x, scale, shift, rope_cos, rope_sin, w_in_b[layer], tm=min(1024, S))
        attn = _dilated_attention(proj)
        rec = _hgrn2(proj, lb_all[layer], vec(rec_norm_gain[layer]))
        x = _out_projection(attn, rec, x, gate, vec(attn_norm_gain[layer]), w_out_b[layer],
                            vec(ln_gain[layer, 0]), vec(ln_bias[layer, 0]), alpha=alpha, tm=min(512, S))
        shift, scale, gate = mod(layer, 1)
        j = layer // 2
        if layer % 2 == 0:
            x = _dense_ffn(x, scale, shift, gate, ffn_g[j], ffn_u[j], ffn_d[j],
                           vec(ln_gain[layer, 1]), vec(ln_bias[layer, 1]),
                           alpha=alpha, tm=min(512, S), tc=2 * LANES)
        else:
            x = _moe_ffn(x, scale, shift, gate, moe_router[j], moe_w_gate[j], moe_w_up[j],
                         moe_w_down[j], vec(ln_gain[layer, 1]), vec(ln_bias[layer, 1]),
                         alpha=alpha, tm=min(512, S), tm_e=min(1152, B * S), tf=512)
    return x
```

```python
import functools

import jax
import jax.numpy as jnp
from jax import lax
from jax.experimental import pallas as pl
from jax.experimental.pallas import tpu as pltpu

F32 = jnp.float32
BF16 = jnp.bfloat16
I32 = jnp.int32

LANES = 128
V7X_VMEM_LIMIT_BYTES = 56 * 1024 * 1024

ATTN_WIDTH = 512
ATTN_HEAD_DIM = 64
DILATIONS = (1, 4, 16)
ATTN_RADIUS = 64
ROPE_THETA = 10000.0
REC_WIDTH = 512
REC_DIM = 128
REC_CHUNK = 64
TOP_K = 2
LN_EPS = 1e-5
RMS_EPS = 1e-6
MASK_VALUE = -1e30
DECAY_EXP_CLAMP = 80.0
LOG2_E = 1.4426950408889634


def _params(semantics):
    return pltpu.CompilerParams(dimension_semantics=semantics,
                                vmem_limit_bytes=V7X_VMEM_LIMIT_BYTES)


def _sigmoid(x):
    return 1.0 / (1.0 + jnp.exp(-x))


def _layer_norm(z, gain, bias):
    mu = jnp.mean(z, axis=-1, keepdims=True)
    zc = z - mu
    var = jnp.mean(zc * zc, axis=-1, keepdims=True)
    return zc * lax.rsqrt(var + LN_EPS) * gain + bias


def _ada_kernel(c_ref, w_ref, b_ref, o_ref):
    c = c_ref[...]
    o_ref[...] = jnp.dot(c * _sigmoid(c), w_ref[...], precision=lax.Precision.HIGHEST,
                         preferred_element_type=F32) + b_ref[...]


def _ada_modulation(c, ada_w, ada_b):
    B, D = c.shape
    n = ada_w.shape[0] * ada_w.shape[1]
    n3 = ada_w.shape[-1]
    tn = 1024
    return pl.pallas_call(
        _ada_kernel,
        grid=(n, n3 // tn),
        in_specs=[pl.BlockSpec((B, D), lambda i, j: (0, 0)),
                  pl.BlockSpec((None, D, tn), lambda i, j: (i, 0, j)),
                  pl.BlockSpec((None, 1, tn), lambda i, j: (i, 0, j))],
        out_specs=pl.BlockSpec((None, B, tn), lambda i, j: (i, 0, j)),
        out_shape=jax.ShapeDtypeStruct((n, B, n3), F32),
        compiler_params=_params(("arbitrary", "arbitrary")),
        name="ada_modulation",
    )(c, ada_w.reshape(n, D, n3), ada_b.reshape(n, 1, n3))


ROPE_HALF = ATTN_HEAD_DIM // 2


def _qk_lane_order(w_cols):
    lead = w_cols.shape[:-1]
    w = w_cols.reshape(*lead, -1, 2, 2, ROPE_HALF)
    return jnp.swapaxes(w, -3, -2).reshape(*lead, -1)


def _rope_kernel(pos_ref, invf_ref, cos_ref, sin_ref):
    ang = pos_ref[...] * invf_ref[...]
    lane = lax.broadcasted_iota(I32, (1, LANES), 1)
    sin = jnp.sin(ang)
    cos_ref[...] = jnp.cos(ang)
    sin_ref[...] = jnp.where(lane < LANES // 2, -sin, sin)


def _rope_tables(pos_f, inv_freq_lanes, *, tm):
    B, S, _ = pos_f.shape
    tab = pl.BlockSpec((None, tm, LANES), lambda b, i: (b, i, 0))
    return pl.pallas_call(
        _rope_kernel,
        grid=(B, S // tm),
        in_specs=[pl.BlockSpec((None, tm, 1), lambda b, i: (b, i, 0)),
                  pl.BlockSpec((1, LANES), lambda b, i: (0, 0))],
        out_specs=[tab, tab],
        out_shape=[jax.ShapeDtypeStruct((B, S, LANES), F32)] * 2,
        compiler_params=_params(("arbitrary", "arbitrary")),
        name="rope_tables",
    )(pos_f, inv_freq_lanes)


def _inproj_kernel(x_ref, sc_ref, sh_ref, cos_ref, sin_ref, w_ref, o_ref, u_ref, *, tn):
    j = pl.program_id(2)

    @pl.when(j == 0)
    def _():
        u_ref[...] = (x_ref[...] * (1.0 + sc_ref[...]) + sh_ref[...]).astype(BF16)

    cols = pl.ds(pl.multiple_of(j * tn, tn), tn)
    acc = jnp.dot(u_ref[...], w_ref[:, cols], preferred_element_type=F32)

    @pl.when(j > 0)
    def _():
        o_ref[...] = acc

    @pl.when(j == 0)
    def _():
        cos = cos_ref[...]
        sin_signed = sin_ref[...]
        q_scale = ATTN_HEAD_DIM ** -0.5
        for kk in range(tn // LANES):
            c = acc[:, kk * LANES:(kk + 1) * LANES]
            r = c * cos + pltpu.roll(c, LANES // 2, 1) * sin_signed
            if kk * LANES < ATTN_WIDTH:
                r = r * q_scale
            o_ref[:, kk * LANES:(kk + 1) * LANES] = r


def _in_projection(x, scale, shift, rope_cos, rope_sin, w_in_bf16, *, tm):
    B, S, D = x.shape
    n_cols = w_in_bf16.shape[1]
    tn = 2 * ATTN_WIDTH
    assert S % tm == 0 and n_cols % tn == 0
    return pl.pallas_call(
        functools.partial(_inproj_kernel, tn=tn),
        grid=(B, S // tm, n_cols // tn),
        in_specs=[pl.BlockSpec((None, tm, D), lambda b, i, j: (b, i, 0)),
                  pl.BlockSpec((None, 1, D), lambda b, i, j: (b, 0, 0)),
                  pl.BlockSpec((None, 1, D), lambda b, i, j: (b, 0, 0)),
                  pl.BlockSpec((None, tm, LANES), lambda b, i, j: (b, i, 0)),
                  pl.BlockSpec((None, tm, LANES), lambda b, i, j: (b, i, 0)),
                  pl.BlockSpec((D, n_cols), lambda b, i, j: (0, 0))],
        out_specs=pl.BlockSpec((None, tm, tn), lambda b, i, j: (b, i, j)),
        out_shape=jax.ShapeDtypeStruct((B, S, n_cols), F32),
        scratch_shapes=[pltpu.VMEM((tm, D), BF16)],
        compiler_params=_params(("arbitrary", "arbitrary", "arbitrary")),
        name="in_projection",
    )(x, scale, shift, rope_cos, rope_sin, w_in_bf16)


ATTN_BLOCK_GROUP = 8


def _attn_kernel(q_ref, k_ref, v_ref, o_ref, qs, ks, vs, qf, kf, vf, pm, plr, pa, nm, nl, na, bias_s,
                 *, S):
    lane = lax.broadcasted_iota(I32, (1, LANES), 1)
    head0 = lane < ATTN_HEAD_DIM
    head0_qk = (lane % ATTN_HEAD_DIM) < ROPE_HALF

    for p, d in enumerate(DILATIONS):
        L = S // d
        tq = min(128, L)
        W = min(2 * tq, L)
        nb = L // tq

        d_prev = DILATIONS[p - 1] if p else 1
        ratio, l_prev = d // d_prev, S // d_prev
        keep_f32 = 0 < p < len(DILATIONS) - 1
        for src, stage, dst in ((q_ref, qf, qs), (k_ref, kf, ks), (v_ref, vf, vs)):
            src = src if p <= 1 else stage
            for r in range(d):
                a, r_prev = divmod(r, d_prev)
                rows = pl.ds(r_prev * l_prev + a, L, stride=ratio) if ratio > 1 else pl.ds(0, L)
                val = src[rows, :]
                if keep_f32:
                    stage[r * L:(r + 1) * L, :] = val
                dst[r * L:(r + 1) * L, :] = val.astype(BF16)

        n_blocks = d * nb
        rc = lax.broadcasted_iota(I32, (tq, W), 0) - lax.broadcasted_iota(I32, (tq, W), 1)
        for which, delta in enumerate((0, ATTN_RADIUS, tq)):
            bias_s[which, pl.ds(0, tq), pl.ds(0, W)] = jnp.where(
                jnp.abs(rc + delta) <= ATTN_RADIUS, 0.0, MASK_VALUE)
        group = min(ATTN_BLOCK_GROUP, n_blocks)
        assert n_blocks % group == 0

        def block_group(gi, carry, L=L, tq=tq, W=W, nb=nb, group=group):
            q0s, k0s, scores, probs, outs, sums = [], [], [], [], [], []
            for u in range(group):
                g = gi * group + u
                r = g // nb
                n = g - r * nb
                base = r * L
                q0 = pl.multiple_of(base + n * tq, tq)
                ws = jnp.clip(n * tq - ATTN_RADIUS, 0, L - W)
                k0 = pl.multiple_of(base + ws, 16)
                qb = qs[pl.ds(q0, tq), :]
                zero = jnp.zeros_like(qb)
                q2 = jnp.concatenate([jnp.where(head0_qk, qb, zero), jnp.where(head0_qk, zero, qb)], axis=0)
                s = lax.dot_general(q2, ks[pl.ds(k0, W), :], (((1,), (1,)), ((), ())),
                                    preferred_element_type=F32)
                delta = n * tq - ws
                which = jnp.where(delta == 0, 0, jnp.where(delta == ATTN_RADIUS, 1, 2))
                bias = bias_s[which, pl.ds(0, tq), pl.ds(0, W)]
                scores.append(s + jnp.concatenate([bias, bias], axis=0))
                q0s.append(q0)
                k0s.append(k0)
            for u in range(group):
                s = scores[u]
                m = jnp.max(s, axis=1, keepdims=True)
                e = jnp.exp(s - m)
                probs.append((m, e.astype(BF16)))
                sums.append(jnp.sum(e, axis=1, keepdims=True))
            for u in range(group):
                outs.append(jnp.dot(probs[u][1], vs[pl.ds(k0s[u], W), :], preferred_element_type=F32))
            for u in range(group):
                m, l = probs[u][0], sums[u]
                rows = pl.ds(q0s[u], tq)
                out_m[rows, :] = jnp.where(head0, m[:tq], m[tq:])
                out_l[rows, :] = jnp.where(head0, l[:tq], l[tq:])
                out_a[rows, :] = jnp.where(head0, outs[u][:tq], outs[u][tq:])
            return carry

        out_m, out_l, out_a = (nm.at[p], nl.at[p], na.at[p]) if d == 1 else (pm, plr, pa)
        lax.fori_loop(0, n_blocks // group, block_group, 0)

        for r in range(d if d > 1 else 0):
            rows = pl.ds(r, L, stride=d)
            nm[p, rows, :] = pm[r * L:(r + 1) * L, :]
            nl[p, rows, :] = plr[r * L:(r + 1) * L, :]
            na[p, rows, :] = pa[r * L:(r + 1) * L, :]

    m_all = jnp.maximum(jnp.maximum(nm[0], nm[1]), nm[2])
    num = jnp.zeros((S, LANES), F32)
    den = jnp.zeros((S, LANES), F32)
    for p in range(len(DILATIONS)):
        w = jnp.exp(nm[p] - m_all)
        num = num + w * na[p]
        den = den + w * nl[p]
    o_ref[...] = (num / den).astype(o_ref.dtype)


def _dilated_attention(proj):
    B, S, _ = proj.shape
    n_pairs = ATTN_WIDTH // LANES
    assert S % (16 * DILATIONS[-1]) == 0
    blk = lambda off: pl.BlockSpec((None, S, LANES), lambda b, h: (b, 0, off + h))
    return pl.pallas_call(
        functools.partial(_attn_kernel, S=S),
        grid=(B, n_pairs),
        in_specs=[blk(0), blk(n_pairs), blk(2 * n_pairs)],
        out_specs=pl.BlockSpec((None, S, LANES), lambda b, h: (b, 0, h)),
        out_shape=jax.ShapeDtypeStruct((B, S, ATTN_WIDTH), BF16),
        scratch_shapes=[pltpu.VMEM((S, LANES), BF16)] * 3
                       + [pltpu.VMEM((S, LANES), F32)] * 6
                       + [pltpu.VMEM((len(DILATIONS), S, LANES), F32)] * 3
                       + [pltpu.VMEM((3, min(128, S), min(256, S)), F32)],
        compiler_params=_params(("arbitrary", "arbitrary")),
        name="dilated_attention",
    )(proj, proj, proj)


REC_GROUP_CHUNKS = 4
REC_GROUP_ROWS = REC_GROUP_CHUNKS * REC_CHUNK
REC_STAGE_GROUPS = 4


def _hgrn_kernel(rq_ref, zf_ref, zb_ref, ri_ref, rg_ref, lb_ref, gain_ref, o_ref,
                 q_s, vt_s, oi_s, qp_s, ut_s, dd_s, st_s, os_s, *, S):
    C, G, GR = REC_CHUNK, REC_GROUP_CHUNKS, REC_GROUP_ROWS
    nc, ng = S // C, S // GR
    sg = min(REC_STAGE_GROUPS, ng)
    assert ng % sg == 0

    row = lax.broadcasted_iota(I32, (GR, GR), 0)
    col = lax.broadcasted_iota(I32, (GR, GR), 1)
    same_chunk = (row // C) == (col // C)
    t_i = lax.broadcasted_iota(I32, (C, C), 0)
    s_i = lax.broadcasted_iota(I32, (C, C), 1)

    def prep(i, carry):
        r0 = pl.multiple_of(i * GR, GR)
        rq = rq_ref[pl.ds(r0, GR), :]
        q_s[pl.ds(r0, GR), :] = rq * _sigmoid(rq)
        v = ri_ref[pl.ds(r0, GR), :]
        for c in range(G):
            vt_s[i * G + c] = v[c * C:(c + 1) * C, :].T.astype(BF16)
        return carry

    lax.fori_loop(0, ng, prep, 0)

    for direction in range(2):
        fwd = direction == 0
        z_ref = zf_ref if fwd else zb_ref
        cum = jnp.where(same_chunk & ((col <= row) if fwd else (col >= row)), 1.0, 0.0).astype(BF16)
        keep = (s_i <= t_i) if fwd else (s_i >= t_i)
        mid_row = C // 2 - 1 if fwd else C // 2
        last_row = C - 1 if fwd else 0

        def phase_a(it, carry, fwd=fwd, z_ref=z_ref, cum=cum, keep=keep,
                    mid_row=mid_row, last_row=last_row, direction=direction):
            lb = lb_ref[direction:direction + 1, :]
            log_lb = jnp.log(lb)
            log_1m_lb = jnp.log1p(-lb)
            r0s, c0s, k3s, bcs, atts, stage3 = [], [], [], [], [], []
            for u in range(sg):
                i = it * sg + u
                r0 = pl.multiple_of(i * GR, GR)
                z = z_ref[pl.ds(r0, GR), :]
                e = jnp.exp(-jnp.abs(z))
                t = log_1m_lb + (jnp.minimum(z, 0.0) - jnp.log(1.0 + e))
                g = jnp.maximum(log_lb, t) + jnp.log(1.0 + jnp.exp(-jnp.abs(log_lb - t)))
                kk = (1.0 - lb) * (jnp.where(z > 0, e, 1.0) / (1.0 + e))
                g = g * LOG2_E
                g1 = g.astype(BF16)
                r1 = g - g1.astype(F32)
                g2 = r1.astype(BF16)
                g3 = (r1 - g2.astype(F32)).astype(BF16)
                bcs.append(jnp.dot(cum, g1, preferred_element_type=F32)
                           + jnp.dot(cum, g2, preferred_element_type=F32)
                           + jnp.dot(cum, g3, preferred_element_type=F32))
                k3s.append(kk.reshape(G, C, REC_DIM))
                r0s.append(r0)
                c0s.append(pl.multiple_of(i * G, G))
            for u in range(sg):
                b3 = bcs[u].reshape(G, C, REC_DIM)
                mid = b3[:, mid_row:mid_row + 1, :]
                last = b3[:, last_row:last_row + 1, :]
                e3 = b3 - mid
                q3 = q_s[pl.ds(r0s[u], GR), :].reshape(G, C, REC_DIM)
                clamp = DECAY_EXP_CLAMP * LOG2_E
                qt = (q3 * jnp.exp2(jnp.minimum(e3, clamp))).astype(BF16)
                kt = (k3s[u] * jnp.exp2(jnp.minimum(-e3, clamp))).astype(BF16)
                atts.append(jnp.einsum('gtc,gsc->gts', qt, kt, preferred_element_type=F32))
                qp_s[pl.ds(r0s[u], GR), :] = (q3 * jnp.exp2(b3)).astype(BF16).reshape(GR, REC_DIM)
                stage3.append(((k3s[u] * jnp.exp2(last - b3)).astype(BF16), jnp.exp2(last)))
            for u in range(sg):
                att = jnp.where(keep[None], atts[u], 0.0).astype(BF16)
                v3 = ri_ref[pl.ds(r0s[u], GR), :].reshape(G, C, REC_DIM).astype(BF16)
                oi = jnp.einsum('gts,gsv->gtv', att, v3, preferred_element_type=F32)
                oi_s[pl.ds(r0s[u], GR), :] = oi.reshape(GR, REC_DIM)
                kp, decay = stage3[u]
                ut_s[pl.ds(c0s[u], G)] = jnp.einsum('gvs,gsc->gvc', vt_s[pl.ds(c0s[u], G)], kp,
                                                    preferred_element_type=F32)
                dd_s[pl.ds(c0s[u], G)] = decay
            return carry

        lax.fori_loop(0, ng // sg, phase_a, 0)

        def scan(c, st, fwd=fwd):
            idx = c if fwd else nc - 1 - c
            st_s[idx] = st.astype(BF16)
            return dd_s[idx] * st + ut_s[idx]

        lax.fori_loop(0, nc, scan, jnp.zeros((REC_DIM, REC_DIM), F32))

        def phase_c(it, carry, fwd=fwd):
            r0s = [pl.multiple_of((it * sg + u) * GR, GR) for u in range(sg)]
            oos = []
            for u in range(sg):
                c0 = pl.multiple_of((it * sg + u) * G, G)
                qp3 = qp_s[pl.ds(r0s[u], GR), :].reshape(G, C, REC_DIM)
                oos.append(jnp.einsum('gtc,gvc->gtv', qp3, st_s[pl.ds(c0, G)],
                                      preferred_element_type=F32))
            for u in range(sg):
                rows = pl.ds(r0s[u], GR)
                tot = oi_s[rows, :] + oos[u].reshape(GR, REC_DIM)
                os_s[rows, :] = tot if fwd else os_s[rows, :] + tot
            return carry

        lax.fori_loop(0, ng // sg, phase_c, 0)

    o = os_s[...]
    o = o * lax.rsqrt(jnp.mean(o * o, axis=-1, keepdims=True) + RMS_EPS) * gain_ref[...]
    o_ref[...] = (o * _sigmoid(rg_ref[...])).astype(o_ref.dtype)


def _hgrn2(proj, lb, rec_gain):
    B, S, _ = proj.shape
    nh = REC_WIDTH // REC_DIM
    assert S % REC_GROUP_ROWS == 0
    col0 = 3 * ATTN_WIDTH // LANES
    blk = lambda k: pl.BlockSpec((None, S, REC_DIM), lambda b, h: (b, 0, col0 + k * nh + h))
    nc = S // REC_CHUNK
    return pl.pallas_call(
        functools.partial(_hgrn_kernel, S=S),
        grid=(B, nh),
        in_specs=[blk(0), blk(1), blk(2), blk(3), blk(4),
                  pl.BlockSpec((2, REC_DIM), lambda b, h: (0, h)),
                  pl.BlockSpec((1, REC_DIM), lambda b, h: (0, h))],
        out_specs=pl.BlockSpec((None, S, REC_DIM), lambda b, h: (b, 0, h)),
        out_shape=jax.ShapeDtypeStruct((B, S, REC_WIDTH), BF16),
        scratch_shapes=[pltpu.VMEM((S, REC_DIM), F32),
                        pltpu.VMEM((nc, REC_DIM, REC_CHUNK), BF16),
                        pltpu.VMEM((S, REC_DIM), F32),
                        pltpu.VMEM((S, REC_DIM), BF16),
                        pltpu.VMEM((nc, REC_DIM, REC_DIM), F32),
                        pltpu.VMEM((nc, 1, REC_DIM), F32),
                        pltpu.VMEM((nc, REC_DIM, REC_DIM), BF16),
                        pltpu.VMEM((S, REC_DIM), F32)],
        compiler_params=_params(("arbitrary", "arbitrary")),
        name="hgrn2",
    )(proj, proj, proj, proj, proj, lb, rec_gain)


def _outproj_kernel(a_ref, r_ref, x_ref, gate_ref, ag_ref, w_ref, lg_ref, lb_ref, o_ref, *, alpha):
    a = a_ref[...].astype(F32)
    an = a * lax.rsqrt(jnp.mean(a * a, axis=-1, keepdims=True) + RMS_EPS) * ag_ref[...]
    y = (jnp.dot(an.astype(BF16), w_ref[:ATTN_WIDTH, :], preferred_element_type=F32)
         + jnp.dot(r_ref[...], w_ref[ATTN_WIDTH:, :], preferred_element_type=F32))
    z = alpha * x_ref[...] + (1.0 + gate_ref[...]) * y
    o_ref[...] = _layer_norm(z, lg_ref[...], lb_ref[...])


def _out_projection(attn, rec, x, gate, attn_gain, w_out_bf16, ln_g, ln_b, *, alpha, tm):
    B, S, D = x.shape
    row = lambda w: pl.BlockSpec((None, tm, w), lambda b, i: (b, i, 0))
    vec = lambda w: pl.BlockSpec((1, w), lambda b, i: (0, 0))
    return pl.pallas_call(
        functools.partial(_outproj_kernel, alpha=alpha),
        grid=(B, S // tm),
        in_specs=[row(ATTN_WIDTH), row(REC_WIDTH), row(D),
                  pl.BlockSpec((None, 1, D), lambda b, i: (b, 0, 0)),
                  vec(ATTN_WIDTH),
                  pl.BlockSpec((ATTN_WIDTH + REC_WIDTH, D), lambda b, i: (0, 0)),
                  vec(D), vec(D)],
        out_specs=row(D),
        out_shape=jax.ShapeDtypeStruct((B, S, D), F32),
        compiler_params=_params(("arbitrary", "arbitrary")),
        name="out_projection",
    )(attn, rec, x, gate, attn_gain, w_out_bf16, ln_g, ln_b)


def _ffn_kernel(x_ref, sc_ref, sh_ref, gate_ref, wg_ref, wu_ref, wd_ref, lg_ref, lb_ref, o_ref,
                *, alpha, tc):
    x = x_ref[...]
    u = (x * (1.0 + sc_ref[...]) + sh_ref[...]).astype(BF16)
    acc = jnp.zeros(x.shape, F32)
    for c in range(wg_ref.shape[1] // tc):
        cols = slice(c * tc, (c + 1) * tc)
        g = jnp.dot(u, wg_ref[:, cols], preferred_element_type=F32)
        up = jnp.dot(u, wu_ref[:, cols], preferred_element_type=F32)
        h = (g * _sigmoid(g) * up).astype(BF16)
        acc = acc + jnp.dot(h, wd_ref[cols, :], preferred_element_type=F32)
    z = alpha * x + (1.0 + gate_ref[...]) * acc
    o_ref[...] = _layer_norm(z, lg_ref[...], lb_ref[...])


def _dense_ffn(x, scale, shift, gate, wg, wu, wd, ln_g, ln_b, *, alpha, tm, tc):
    B, S, D = x.shape
    ffp = wg.shape[1]
    assert ffp % tc == 0
    row = pl.BlockSpec((None, tm, D), lambda b, i: (b, i, 0))
    mod = pl.BlockSpec((None, 1, D), lambda b, i: (b, 0, 0))
    vec = pl.BlockSpec((1, D), lambda b, i: (0, 0))
    return pl.pallas_call(
        functools.partial(_ffn_kernel, alpha=alpha, tc=tc),
        grid=(B, S // tm),
        in_specs=[row, mod, mod, mod,
                  pl.BlockSpec((D, ffp), lambda b, i: (0, 0)),
                  pl.BlockSpec((D, ffp), lambda b, i: (0, 0)),
                  pl.BlockSpec((ffp, D), lambda b, i: (0, 0)),
                  vec, vec],
        out_specs=row,
        out_shape=jax.ShapeDtypeStruct((B, S, D), F32),
        compiler_params=_params(("arbitrary", "arbitrary")),
        name="dense_ffn",
    )(x, scale, shift, gate, wg, wu, wd, ln_g, ln_b)


META_E0, META_E1, META_W0, META_W1, META_R0, META_R1 = range(6)
SEGMENT_ALIGN = 16
DISPATCH_ROWS = 256
DISPATCH_HALF = DISPATCH_ROWS // 2
DISPATCH_SLACK = DISPATCH_HALF


def _router_kernel(x_ref, sc_ref, sh_ref, wr_ref, u_ref, meta_ref, cnt_ref, carry_ref, *, n_exp):
    tm = x_ref.shape[0]

    @pl.when((pl.program_id(0) == 0) & (pl.program_id(1) == 0))
    def _():
        carry_ref[...] = jnp.zeros_like(carry_ref)

    u = x_ref[...] * (1.0 + sc_ref[...]) + sh_ref[...]
    u_ref[...] = u.astype(BF16)
    u_hi = u.astype(BF16)
    u_lo = (u - u_hi.astype(F32)).astype(BF16)
    logits = (jnp.dot(u_hi, wr_ref[0], preferred_element_type=F32)
              + jnp.dot(u_lo, wr_ref[0], preferred_element_type=F32)
              + jnp.dot(u_hi, wr_ref[1], preferred_element_type=F32))
    lane = lax.broadcasted_iota(I32, (tm, LANES), 1)
    neg = -jnp.inf
    l1 = jnp.where(lane < n_exp, logits, neg)
    m1 = jnp.max(l1, axis=1, keepdims=True)
    i1 = jnp.min(jnp.where(l1 == m1, lane, LANES), axis=1, keepdims=True)
    l2 = jnp.where(lane == i1, neg, l1)
    m2 = jnp.max(l2, axis=1, keepdims=True)
    i2 = jnp.min(jnp.where(l2 == m2, lane, LANES), axis=1, keepdims=True)
    e = jnp.exp(m2 - m1)
    w1 = 1.0 / (1.0 + e)
    w2 = e * w1
    sel = jnp.where((lane == i1) | (lane == i2), 1.0, 0.0)
    before = (lax.broadcasted_iota(I32, (tm, tm), 1) < lax.broadcasted_iota(I32, (tm, tm), 0))
    ranks = jnp.dot(jnp.where(before, 1.0, 0.0).astype(BF16), sel.astype(BF16),
                    preferred_element_type=F32) + carry_ref[...]
    r1 = jnp.sum(jnp.where(lane == i1, ranks, 0.0), axis=1, keepdims=True)
    r2 = jnp.sum(jnp.where(lane == i2, ranks, 0.0), axis=1, keepdims=True)
    carry_ref[...] = carry_ref[...] + jnp.sum(sel, axis=0, keepdims=True)
    cnt_ref[...] = carry_ref[...]
    meta = jnp.zeros((tm, LANES), F32)
    for k, val in ((META_E0, i1.astype(F32)), (META_E1, i2.astype(F32)), (META_W0, w1),
                   (META_W1, w2), (META_R0, r1), (META_R1, r2)):
        meta = jnp.where(lane == k, val, meta)
    meta_ref[...] = meta


def _router(x, scale, shift, w_router_lanes, *, n_exp, tm):
    B, S, D = x.shape
    nt = S // tm
    row = pl.BlockSpec((None, tm, D), lambda b, i: (b, i, 0))
    mod = pl.BlockSpec((None, 1, D), lambda b, i: (b, 0, 0))
    return pl.pallas_call(
        functools.partial(_router_kernel, n_exp=n_exp),
        grid=(B, nt),
        in_specs=[row, mod, mod, pl.BlockSpec((2, D, LANES), lambda b, i: (0, 0, 0))],
        out_specs=[row,
                   pl.BlockSpec((None, tm, LANES), lambda b, i: (b, i, 0)),
                   pl.BlockSpec((1, LANES), lambda b, i: (0, 0))],
        out_shape=[jax.ShapeDtypeStruct((B, S, D), BF16),
                   jax.ShapeDtypeStruct((B, S, LANES), F32),
                   jax.ShapeDtypeStruct((1, LANES), F32)],
        scratch_shapes=[pltpu.VMEM((1, LANES), F32)],
        compiler_params=_params(("arbitrary", "arbitrary")),
        name="moe_router",
    )(x, scale, shift, w_router_lanes)


def _rows_at(hbm, first_row, n_rows):
    return hbm.at[pl.ds(pl.multiple_of(first_row, SEGMENT_ALIGN), n_rows)]


def _dispatch_kernel(base_ref, cnt_ref, padlo_ref, padn_ref, u_ref, meta_ref, xs_hbm,
                     stage, extra, zeros_buf, sems, extra_sem, *, n_exp):
    t = pl.program_id(0)
    last = pl.num_programs(0) - 1
    tm = u_ref.shape[0]
    R, H = DISPATCH_ROWS, DISPATCH_HALF
    slot = t % 2
    u = u_ref[...]
    e0, e1 = meta_ref[0:1, :], meta_ref[1:2, :]
    d0, d1 = meta_ref[2:3, :], meta_ref[3:4, :]
    srow = lax.broadcasted_iota(I32, (R, tm), 0)

    def main_copies(step, buf, fn):
        for e in range(n_exp):
            base = base_ref[step * n_exp + e]
            fn(pltpu.make_async_copy(stage.at[buf, pl.ds(e * R, H)], _rows_at(xs_hbm, base, H),
                                     sems.at[buf]))

            @pl.when(cnt_ref[step * n_exp + e] > H)
            def _(e=e, base=base):
                fn(pltpu.make_async_copy(stage.at[buf, pl.ds(e * R + H, H)],
                                         _rows_at(xs_hbm, base + H, H), sems.at[buf]))

    rels = []
    for e in range(n_exp):
        rel = jnp.where(e0 == e, d0, jnp.where(e1 == e, d1, -1))
        rels.append(jnp.where(rel >= 0, rel - base_ref[t * n_exp + e], -1))
    onehot = jnp.concatenate([jnp.where(srow == rel, 1.0, 0.0) for rel in rels], axis=0).astype(BF16)
    stage[slot] = jnp.dot(onehot, u, preferred_element_type=F32).astype(BF16)

    @pl.when(t > 0)
    def _():
        main_copies(t - 1, 1 - slot, lambda cp: cp.wait())
    main_copies(t, slot, lambda cp: cp.start())

    for e in range(n_exp):
        for blk in range(1, tm // R):
            @pl.when(cnt_ref[t * n_exp + e] > blk * R)
            def _(e=e, blk=blk):
                more = jnp.where(srow + blk * R == rels[e], 1.0, 0.0).astype(BF16)
                extra[...] = jnp.dot(more, u, preferred_element_type=F32).astype(BF16)
                cp = pltpu.make_async_copy(extra, _rows_at(xs_hbm, base_ref[t * n_exp + e] + blk * R, R),
                                           extra_sem)
                cp.start()
                cp.wait()

    @pl.when(t == last)
    def _():
        main_copies(t, slot, lambda cp: cp.wait())
        zeros_buf[...] = jnp.zeros_like(zeros_buf)
        pieces = [SEGMENT_ALIGN << b for b in range((R // SEGMENT_ALIGN).bit_length() - 1)]

        def pad_copies(fn):
            for p in range(padlo_ref.shape[0]):
                lo, n = padlo_ref[p], padn_ref[p]
                whole = n // R

                def body(k, carry, lo=lo):
                    fn(pltpu.make_async_copy(zeros_buf, _rows_at(xs_hbm, lo + k * R, R), extra_sem))
                    return carry
                lax.fori_loop(0, whole, body, 0)
                rem = n - whole * R
                for sz in pieces:
                    off = lo + whole * R + (rem // (2 * sz)) * (2 * sz)
                    pl.when((rem // sz) % 2 == 1)(functools.partial(
                        lambda off, sz: fn(pltpu.make_async_copy(
                            zeros_buf.at[pl.ds(0, sz)], _rows_at(xs_hbm, off, sz), extra_sem)), off, sz))

        pad_copies(lambda cp: cp.start())
        pad_copies(lambda cp: cp.wait())


def _dispatch(u, meta_t, base_te, cnt_te, pad_lo, pad_n, *, n_rows, tm):
    T, D = u.shape
    n_exp = pad_lo.shape[0] - 1
    assert tm % DISPATCH_ROWS == 0
    grid_spec = pltpu.PrefetchScalarGridSpec(
        num_scalar_prefetch=4,
        grid=(T // tm,),
        in_specs=[pl.BlockSpec((tm, D), lambda t, *_: (t, 0)),
                  pl.BlockSpec((meta_t.shape[0], tm), lambda t, *_: (0, t))],
        out_specs=pl.BlockSpec(memory_space=pl.ANY),
        scratch_shapes=[pltpu.VMEM((2, n_exp * DISPATCH_ROWS, D), BF16),
                        pltpu.VMEM((DISPATCH_ROWS, D), BF16),
                        pltpu.VMEM((DISPATCH_ROWS, D), BF16),
                        pltpu.SemaphoreType.DMA((2,)),
                        pltpu.SemaphoreType.DMA(())],
    )
    return pl.pallas_call(
        functools.partial(_dispatch_kernel, n_exp=n_exp),
        grid_spec=grid_spec,
        out_shape=jax.ShapeDtypeStruct((n_rows, D), BF16),
        compiler_params=_params(("arbitrary",)),
        name="moe_dispatch",
    )(base_te, cnt_te, pad_lo, pad_n, u, meta_t)


TILE_FULL, TILE_HALF, TILE_EMPTY = 0, 1, 2


def _expert_kernel(te_ref, na_ref, fill_ref, xs_ref, wg_ref, wu_ref, wd_ref, o_ref, acc_ref):
    i = pl.program_id(0)
    j = pl.program_id(1)
    last = pl.num_programs(1) - 1
    active = i < na_ref[0]
    fill = fill_ref[i]
    tm = acc_ref.shape[0]

    @pl.when(active & (j == 0))
    def _():
        acc_ref[...] = jnp.zeros_like(acc_ref)

    def swiglu(rows):
        xb = xs_ref[rows, :]
        g = jnp.dot(xb, wg_ref[...].astype(BF16), preferred_element_type=F32)
        up = jnp.dot(xb, wu_ref[...].astype(BF16), preferred_element_type=F32)
        h = (g * _sigmoid(g) * up).astype(BF16)
        acc_ref[rows, :] += jnp.dot(h, wd_ref[...].astype(BF16), preferred_element_type=F32)

    pl.when(active & (fill == TILE_FULL))(functools.partial(swiglu, pl.ds(0, tm)))
    pl.when(active & (fill == TILE_HALF))(functools.partial(swiglu, pl.ds(0, tm // 2)))

    @pl.when(active & (j == last))
    def _():
        o_ref[...] = acc_ref[...].astype(BF16)

    @pl.when(jnp.logical_not(active) & (j == last))
    def _():
        o_ref[...] = jnp.zeros_like(o_ref)


def _expert_ffn(xs, tile_expert, n_active, tile_fill, wg, wu, wd, *, tm, tf):
    n_rows, D = xs.shape
    ff = wg.shape[2]
    assert n_rows % tm == 0 and ff % tf == 0
    nj = ff // tf

    def jj(i, j, na):
        return jnp.where(i < na[0], j, nj - 1)

    def ii(i, na):
        return jnp.minimum(i, na[0] - 1)

    grid_spec = pltpu.PrefetchScalarGridSpec(
        num_scalar_prefetch=3,
        grid=(n_rows // tm, nj),
        in_specs=[pl.BlockSpec((tm, D), lambda i, j, te, na, tf_: (ii(i, na), 0)),
                  pl.BlockSpec((None, D, tf), lambda i, j, te, na, tf_: (te[i], 0, jj(i, j, na))),
                  pl.BlockSpec((None, D, tf), lambda i, j, te, na, tf_: (te[i], 0, jj(i, j, na))),
                  pl.BlockSpec((None, tf, D), lambda i, j, te, na, tf_: (te[i], jj(i, j, na), 0))],
        out_specs=pl.BlockSpec((tm, D), lambda i, j, te, na, tf_: (i, 0)),
        scratch_shapes=[pltpu.VMEM((tm, D), F32)],
    )
    return pl.pallas_call(
        _expert_kernel,
        grid_spec=grid_spec,
        out_shape=jax.ShapeDtypeStruct((n_rows, D), BF16),
        compiler_params=_params(("arbitrary", "arbitrary")),
        name="moe_experts",
    )(tile_expert, n_active, tile_fill, xs, wg, wu, wd)


def _combine_kernel(base_ref, off_ref, cnt_ref, ys_hbm, meta_ref, x_ref, gate_ref, lg_ref, lb_ref, o_ref,
                    stage, extra, acc_ref, sems, extra_sem, *, alpha, n_exp):
    t = pl.program_id(0) * pl.num_programs(1) + pl.program_id(1)
    n_steps = pl.num_programs(0) * pl.num_programs(1)
    tm = x_ref.shape[0]
    R = DISPATCH_ROWS
    slot = t % 2

    def main_copies(step, buf, fn):
        for e in range(n_exp):
            fn(pltpu.make_async_copy(_rows_at(ys_hbm, base_ref[step * n_exp + e], R),
                                     stage.at[buf, pl.ds(e * R, R)], sems.at[buf]))

    @pl.when(t == 0)
    def _():
        main_copies(0, 0, lambda cp: cp.start())

    @pl.when(t + 1 < n_steps)
    def _():
        main_copies(t + 1, 1 - slot, lambda cp: cp.start())

    meta = meta_ref[...]
    col = lambda k: meta[:, k:k + 1]
    e0, e1 = col(META_E0).astype(I32), col(META_E1).astype(I32)
    r0, r1 = col(META_R0).astype(I32), col(META_R1).astype(I32)
    w0, w1 = col(META_W0), col(META_W1)
    lane = lax.broadcasted_iota(I32, (tm, R), 1)

    rels, ws, picks = [], [], []
    for e in range(n_exp):
        rel = jnp.where(e0 == e, r0, jnp.where(e1 == e, r1, -1))
        rels.append(jnp.where(rel >= 0, rel - off_ref[t * n_exp + e], -1))
        ws.append(jnp.where(e0 == e, w0, jnp.where(e1 == e, w1, 0.0)))
        picks.append(jnp.where(lane == rels[e], ws[e], 0.0))
    weighted = jnp.concatenate(picks, axis=1).astype(BF16)
    main_copies(t, slot, lambda cp: cp.wait())
    acc_ref[...] = jnp.dot(weighted, stage[slot], preferred_element_type=F32)

    for e in range(n_exp):
        for blk in range(1, tm // R):
            @pl.when(cnt_ref[t * n_exp + e] > blk * R)
            def _(e=e, blk=blk):
                cp = pltpu.make_async_copy(_rows_at(ys_hbm, base_ref[t * n_exp + e] + blk * R, R),
                                           extra, extra_sem)
                cp.start()
                cp.wait()
                more = jnp.where(lane + blk * R == rels[e], ws[e], 0.0).astype(BF16)
                acc_ref[...] += jnp.dot(more, extra[...], preferred_element_type=F32)

    z = alpha * x_ref[...] + (1.0 + gate_ref[...]) * acc_ref[...]
    o_ref[...] = _layer_norm(z, lg_ref[...], lb_ref[...])


def _combine(ys, base_te, off_te, cnt_te, meta, x, gate, ln_g, ln_b, *, alpha, tm):
    B, S, D = x.shape
    nt = S // tm
    n_exp = base_te.shape[0] // (B * nt)
    row = pl.BlockSpec((None, tm, D), lambda b, i, *_: (b, i, 0))
    vec = pl.BlockSpec((1, D), lambda b, i, *_: (0, 0))
    grid_spec = pltpu.PrefetchScalarGridSpec(
        num_scalar_prefetch=3,
        grid=(B, nt),
        in_specs=[pl.BlockSpec(memory_space=pl.ANY),
                  pl.BlockSpec((None, tm, LANES), lambda b, i, *_: (b, i, 0)),
                  row, pl.BlockSpec((None, 1, D), lambda b, i, *_: (b, 0, 0)), vec, vec],
        out_specs=row,
        scratch_shapes=[pltpu.VMEM((2, n_exp * DISPATCH_ROWS, D), BF16),
                        pltpu.VMEM((DISPATCH_ROWS, D), BF16),
                        pltpu.VMEM((tm, D), F32),
                        pltpu.SemaphoreType.DMA((2,)),
                        pltpu.SemaphoreType.DMA(())],
    )
    return pl.pallas_call(
        functools.partial(_combine_kernel, alpha=alpha, n_exp=n_exp),
        grid_spec=grid_spec,
        out_shape=jax.ShapeDtypeStruct((B, S, D), F32),
        compiler_params=_params(("arbitrary", "arbitrary")),
        name="moe_combine",
    )(base_te, off_te, cnt_te, ys, meta, x, gate, ln_g, ln_b)


def _moe_ffn(x, scale, shift, gate, w_router, wg, wu, wd, ln_g, ln_b, *, alpha, tm, tm_e, tf):
    B, S, D = x.shape
    T = B * S
    nt = T // tm
    n_exp = w_router.shape[1]
    wr = jnp.zeros((D, LANES), F32).at[:, :n_exp].set(w_router)
    wr_hi = wr.astype(BF16)
    wr = jnp.stack([wr_hi, (wr - wr_hi.astype(F32)).astype(BF16)])
    u, meta, counts = _router(x, scale, shift, wr, n_exp=n_exp, tm=tm)

    meta2 = meta.reshape(T, LANES)
    e0 = meta2[:, META_E0].astype(I32)
    e1 = meta2[:, META_E1].astype(I32)
    r0 = meta2[:, META_R0].astype(I32)
    r1 = meta2[:, META_R1].astype(I32)
    experts = jnp.arange(n_exp, dtype=I32)
    chosen = ((e0[:, None] == experts) | (e1[:, None] == experts)).astype(I32)
    cnt_te = chosen.reshape(nt, tm, n_exp).sum(axis=1)
    seg_te = -(-cnt_te // SEGMENT_ALIGN) * SEGMENT_ALIGN
    off_te = jnp.cumsum(cnt_te, axis=0) - cnt_te
    filled = seg_te.sum(axis=0)
    sizes = -(-(filled + DISPATCH_SLACK) // tm_e) * tm_e
    ends = jnp.cumsum(sizes)
    starts = ends - sizes
    base_te = starts[None, :] + jnp.cumsum(seg_te, axis=0) - seg_te
    shift_te = (base_te - off_te)[:, None, :]

    def rows_of(e, r):
        pick = (e[:, None] == experts).astype(I32).reshape(nt, tm, n_exp)
        return r + (pick * shift_te).sum(axis=-1).reshape(T)

    d0, d1 = rows_of(e0, r0), rows_of(e1, r1)
    bound = TOP_K * T + n_exp * ((SEGMENT_ALIGN - 1) * nt + DISPATCH_SLACK)
    n_tiles = -(-bound // tm_e) + n_exp + 1
    n_active = (ends[-1] // tm_e).astype(I32)
    tile_id = jnp.minimum(jnp.arange(n_tiles, dtype=I32), n_active - 1)
    tile_expert = jnp.sum((ends[None, :] <= (tile_id * tm_e)[:, None]).astype(I32), axis=1)
    data_end = jnp.sum((tile_expert[:, None] == experts) * (starts + filled)[None, :], axis=1)
    left = data_end - tile_id * tm_e
    tile_fill = jnp.where(left <= 0, TILE_EMPTY, jnp.where(left <= tm_e // 2, TILE_HALF, TILE_FULL)).astype(I32)
    meta_t = jnp.stack([e0, e1, d0, d1] + [jnp.zeros_like(e0)] * 4)

    xs = _dispatch(u.reshape(T, D), meta_t, base_te.reshape(-1), cnt_te.reshape(-1),
                   jnp.concatenate([starts + filled, ends[-1:]]),
                   jnp.concatenate([sizes - filled, n_tiles * tm_e - ends[-1:]]),
                   n_rows=n_tiles * tm_e, tm=tm)
    ys = _expert_ffn(xs, tile_expert, n_active.reshape(1), tile_fill, wg, wu, wd, tm=tm_e, tf=tf)
    return _combine(ys, base_te.reshape(-1), off_te.reshape(-1), cnt_te.reshape(-1),
                    meta, x, gate, ln_g, ln_b, alpha=alpha, tm=tm)


def kernel(x, c, positions, w_in, w_out, attn_norm_gain, rec_norm_gain, rec_lb_logits, ada_w, ada_b,
           ln_gain, ln_bias, ffn_w_gate, ffn_w_up, ffn_w_down, moe_router, moe_w_gate, moe_w_up,
           moe_w_down):
    B, S, D = x.shape
    depth = w_in.shape[0]
    alpha = (2 * depth) ** 0.25

    p = jax.nn.softmax(rec_lb_logits.astype(F32), axis=0)
    cum = jnp.cumsum(p, axis=0)
    lb_all = cum - cum[0:1]
    half = ATTN_HEAD_DIM // 2
    inv_freq = ROPE_THETA ** (-jnp.arange(half, dtype=F32) / half)
    inv_freq_lanes = jnp.tile(inv_freq, LANES // half).reshape(1, LANES)
    pos_f = positions.astype(F32).reshape(B, S, 1)
    n_qk = 2 * ATTN_WIDTH
    w_in_b = jnp.concatenate([_qk_lane_order(w_in[..., :n_qk]), w_in[..., n_qk:]], axis=-1).astype(BF16)
    w_out_b = w_out.astype(BF16)
    ff = ffn_w_gate.shape[2]
    ffp = -(-ff // (2 * LANES)) * (2 * LANES)
    pad_c = lambda w: jnp.pad(w.astype(BF16), ((0, 0), (0, 0), (0, ffp - ff)))
    ffn_g, ffn_u = pad_c(ffn_w_gate), pad_c(ffn_w_up)
    ffn_d = jnp.pad(ffn_w_down.astype(BF16), ((0, 0), (0, ffp - ff), (0, 0)))

    mods = _ada_modulation(c, ada_w, ada_b)
    rope_cos, rope_sin = _rope_tables(pos_f, inv_freq_lanes, tm=min(1024, S))

    def mod(layer, sub):
        m = mods[layer * 2 + sub].reshape(B, 1, 3 * D)
        return m[..., :D], m[..., D:2 * D], m[..., 2 * D:]

    vec = lambda a: a.reshape(1, -1)
    for layer in range(depth):
        shift, scale, gate = mod(layer, 0)
        proj = _in_projection(x, scale, shift, rope_cos, rope_sin, w_in_b[layer], tm=min(1024, S))
        attn = _dilated_attention(proj)
        rec = _hgrn2(proj, lb_all[layer], vec(rec_norm_gain[layer]))
        x = _out_projection(attn, rec, x, gate, vec(attn_norm_gain[layer]), w_out_b[layer],
                            vec(ln_gain[layer, 0]), vec(ln_bias[layer, 0]), alpha=alpha, tm=min(512, S))
        shift, scale, gate = mod(layer, 1)
        j = layer // 2
        if layer % 2 == 0:
            x = _dense_ffn(x, scale, shift, gate, ffn_g[j], ffn_u[j], ffn_d[j],
                           vec(ln_gain[layer, 1]), vec(ln_bias[layer, 1]),
                           alpha=alpha, tm=min(512, S), tc=2 * LANES)
        else:
            x = _moe_ffn(x, scale, shift, gate, moe_router[j], moe_w_gate[j], moe_w_up[j],
                         moe_w_down[j], vec(ln_gain[layer, 1]), vec(ln_bias[layer, 1]),
                         alpha=alpha, tm=min(512, S), tm_e=min(1024, B * S), tf=512)
    return x
```

```python
import functools

import jax
import jax.numpy as jnp
from jax import lax
from jax.experimental import pallas as pl
from jax.experimental.pallas import tpu as pltpu

F32 = jnp.float32
BF16 = jnp.bfloat16
I32 = jnp.int32

LANES = 128
V7X_VMEM_LIMIT_BYTES = 56 * 1024 * 1024

ATTN_WIDTH = 512
ATTN_HEAD_DIM = 64
DILATIONS = (1, 4, 16)
ATTN_RADIUS = 64
ROPE_THETA = 10000.0
REC_WIDTH = 512
REC_DIM = 128
REC_CHUNK = 64
TOP_K = 2
LN_EPS = 1e-5
RMS_EPS = 1e-6
MASK_VALUE = -1e30
DECAY_EXP_CLAMP = 80.0
LOG2_E = 1.4426950408889634


def _params(semantics):
    return pltpu.CompilerParams(dimension_semantics=semantics,
                                vmem_limit_bytes=V7X_VMEM_LIMIT_BYTES)


def _sigmoid(x):
    return 1.0 / (1.0 + jnp.exp(-x))


def _layer_norm(z, gain, bias):
    mu = jnp.mean(z, axis=-1, keepdims=True)
    zc = z - mu
    var = jnp.mean(zc * zc, axis=-1, keepdims=True)
    return zc * lax.rsqrt(var + LN_EPS) * gain + bias


def _ada_kernel(c_ref, w_ref, b_ref, o_ref):
    c = c_ref[...]
    o_ref[...] = jnp.dot(c * _sigmoid(c), w_ref[...], precision=lax.Precision.HIGHEST,
                         preferred_element_type=F32) + b_ref[...]


def _ada_modulation(c, ada_w, ada_b):
    B, D = c.shape
    n = ada_w.shape[0] * ada_w.shape[1]
    n3 = ada_w.shape[-1]
    tn = 1024
    return pl.pallas_call(
        _ada_kernel,
        grid=(n, n3 // tn),
        in_specs=[pl.BlockSpec((B, D), lambda i, j: (0, 0)),
                  pl.BlockSpec((None, D, tn), lambda i, j: (i, 0, j)),
                  pl.BlockSpec((None, 1, tn), lambda i, j: (i, 0, j))],
        out_specs=pl.BlockSpec((None, B, tn), lambda i, j: (i, 0, j)),
        out_shape=jax.ShapeDtypeStruct((n, B, n3), F32),
        compiler_params=_params(("arbitrary", "arbitrary")),
        name="ada_modulation",
    )(c, ada_w.reshape(n, D, n3), ada_b.reshape(n, 1, n3))


ROPE_HALF = ATTN_HEAD_DIM // 2


def _qk_lane_order(w_cols):
    lead = w_cols.shape[:-1]
    w = w_cols.reshape(*lead, -1, 2, 2, ROPE_HALF)
    return jnp.swapaxes(w, -3, -2).reshape(*lead, -1)


def _rope_kernel(pos_ref, invf_ref, cos_ref, sin_ref):
    ang = pos_ref[...] * invf_ref[...]
    lane = lax.broadcasted_iota(I32, (1, LANES), 1)
    sin = jnp.sin(ang)
    cos_ref[...] = jnp.cos(ang)
    sin_ref[...] = jnp.where(lane < LANES // 2, -sin, sin)


def _rope_tables(pos_f, inv_freq_lanes, *, tm):
    B, S, _ = pos_f.shape
    tab = pl.BlockSpec((None, tm, LANES), lambda b, i: (b, i, 0))
    return pl.pallas_call(
        _rope_kernel,
        grid=(B, S // tm),
        in_specs=[pl.BlockSpec((None, tm, 1), lambda b, i: (b, i, 0)),
                  pl.BlockSpec((1, LANES), lambda b, i: (0, 0))],
        out_specs=[tab, tab],
        out_shape=[jax.ShapeDtypeStruct((B, S, LANES), F32)] * 2,
        compiler_params=_params(("arbitrary", "arbitrary")),
        name="rope_tables",
    )(pos_f, inv_freq_lanes)


def _inproj_kernel(x_ref, sc_ref, sh_ref, cos_ref, sin_ref, w_ref, o_ref, u_ref, *, tn):
    j = pl.program_id(2)

    @pl.when(j == 0)
    def _():
        u_ref[...] = (x_ref[...] * (1.0 + sc_ref[...]) + sh_ref[...]).astype(BF16)

    cols = pl.ds(pl.multiple_of(j * tn, tn), tn)
    acc = jnp.dot(u_ref[...], w_ref[:, cols], preferred_element_type=F32)

    @pl.when(j > 0)
    def _():
        o_ref[...] = acc

    @pl.when(j == 0)
    def _():
        cos = cos_ref[...]
        sin_signed = sin_ref[...]
        q_scale = ATTN_HEAD_DIM ** -0.5
        for kk in range(tn // LANES):
            c = acc[:, kk * LANES:(kk + 1) * LANES]
            r = c * cos + pltpu.roll(c, LANES // 2, 1) * sin_signed
            if kk * LANES < ATTN_WIDTH:
                r = r * q_scale
            o_ref[:, kk * LANES:(kk + 1) * LANES] = r


def _in_projection(x, scale, shift, rope_cos, rope_sin, w_in_bf16, *, tm):
    B, S, D = x.shape
    n_cols = w_in_bf16.shape[1]
    tn = 2 * ATTN_WIDTH
    assert S % tm == 0 and n_cols % tn == 0
    return pl.pallas_call(
        functools.partial(_inproj_kernel, tn=tn),
        grid=(B, S // tm, n_cols // tn),
        in_specs=[pl.BlockSpec((None, tm, D), lambda b, i, j: (b, i, 0)),
                  pl.BlockSpec((None, 1, D), lambda b, i, j: (b, 0, 0)),
                  pl.BlockSpec((None, 1, D), lambda b, i, j: (b, 0, 0)),
                  pl.BlockSpec((None, tm, LANES), lambda b, i, j: (b, i, 0)),
                  pl.BlockSpec((None, tm, LANES), lambda b, i, j: (b, i, 0)),
                  pl.BlockSpec((D, n_cols), lambda b, i, j: (0, 0))],
        out_specs=pl.BlockSpec((None, tm, tn), lambda b, i, j: (b, i, j)),
        out_shape=jax.ShapeDtypeStruct((B, S, n_cols), F32),
        scratch_shapes=[pltpu.VMEM((tm, D), BF16)],
        compiler_params=_params(("arbitrary", "arbitrary", "arbitrary")),
        name="in_projection",
    )(x, scale, shift, rope_cos, rope_sin, w_in_bf16)


ATTN_BLOCK_GROUP = 8


def _attn_kernel(q_ref, k_ref, v_ref, o_ref, qs, ks, vs, qf, kf, vf, pm, plr, pa, nm, nl, na, bias_s,
                 *, S):
    lane = lax.broadcasted_iota(I32, (1, LANES), 1)
    head0 = lane < ATTN_HEAD_DIM
    head0_qk = (lane % ATTN_HEAD_DIM) < ROPE_HALF

    for p, d in enumerate(DILATIONS):
        L = S // d
        tq = min(128, L)
        W = min(2 * tq, L)
        nb = L // tq

        d_prev = DILATIONS[p - 1] if p else 1
        ratio, l_prev = d // d_prev, S // d_prev
        keep_f32 = 0 < p < len(DILATIONS) - 1
        for src, stage, dst in ((q_ref, qf, qs), (k_ref, kf, ks), (v_ref, vf, vs)):
            src = src if p <= 1 else stage
            for r in range(d):
                a, r_prev = divmod(r, d_prev)
                rows = pl.ds(r_prev * l_prev + a, L, stride=ratio) if ratio > 1 else pl.ds(0, L)
                val = src[rows, :]
                if keep_f32:
                    stage[r * L:(r + 1) * L, :] = val
                dst[r * L:(r + 1) * L, :] = val.astype(BF16)

        n_blocks = d * nb
        rc = lax.broadcasted_iota(I32, (tq, W), 0) - lax.broadcasted_iota(I32, (tq, W), 1)
        for which, delta in enumerate((0, ATTN_RADIUS, tq)):
            bias_s[which, pl.ds(0, tq), pl.ds(0, W)] = jnp.where(
                jnp.abs(rc + delta) <= ATTN_RADIUS, 0.0, MASK_VALUE)
        group = min(ATTN_BLOCK_GROUP, n_blocks)
        assert n_blocks % group == 0

        def block_group(gi, carry, L=L, tq=tq, W=W, nb=nb, group=group):
            q0s, k0s, scores, probs, outs, sums = [], [], [], [], [], []
            for u in range(group):
                g = gi * group + u
                r = g // nb
                n = g - r * nb
                base = r * L
                q0 = pl.multiple_of(base + n * tq, tq)
                ws = jnp.clip(n * tq - ATTN_RADIUS, 0, L - W)
                k0 = pl.multiple_of(base + ws, 16)
                qb = qs[pl.ds(q0, tq), :]
                zero = jnp.zeros_like(qb)
                q2 = jnp.concatenate([jnp.where(head0_qk, qb, zero), jnp.where(head0_qk, zero, qb)], axis=0)
                s = lax.dot_general(q2, ks[pl.ds(k0, W), :], (((1,), (1,)), ((), ())),
                                    preferred_element_type=F32)
                delta = n * tq - ws
                which = jnp.where(delta == 0, 0, jnp.where(delta == ATTN_RADIUS, 1, 2))
                bias = bias_s[which, pl.ds(0, tq), pl.ds(0, W)]
                scores.append(s + jnp.concatenate([bias, bias], axis=0))
                q0s.append(q0)
                k0s.append(k0)
            for u in range(group):
                s = scores[u]
                m = jnp.max(s, axis=1, keepdims=True)
                e = jnp.exp(s - m)
                probs.append((m, e.astype(BF16)))
                sums.append(jnp.sum(e, axis=1, keepdims=True))
            for u in range(group):
                outs.append(jnp.dot(probs[u][1], vs[pl.ds(k0s[u], W), :], preferred_element_type=F32))
            for u in range(group):
                m, l = probs[u][0], sums[u]
                rows = pl.ds(q0s[u], tq)
                out_m[rows, :] = jnp.where(head0, m[:tq], m[tq:])
                out_l[rows, :] = jnp.where(head0, l[:tq], l[tq:])
                out_a[rows, :] = jnp.where(head0, outs[u][:tq], outs[u][tq:])
            return carry

        out_m, out_l, out_a = (nm.at[p], nl.at[p], na.at[p]) if d == 1 else (pm, plr, pa)
        lax.fori_loop(0, n_blocks // group, block_group, 0)

        for r in range(d if d > 1 else 0):
            rows = pl.ds(r, L, stride=d)
            nm[p, rows, :] = pm[r * L:(r + 1) * L, :]
            nl[p, rows, :] = plr[r * L:(r + 1) * L, :]
            na[p, rows, :] = pa[r * L:(r + 1) * L, :]

    m_all = jnp.maximum(jnp.maximum(nm[0], nm[1]), nm[2])
    num = jnp.zeros((S, LANES), F32)
    den = jnp.zeros((S, LANES), F32)
    for p in range(len(DILATIONS)):
        w = jnp.exp(nm[p] - m_all)
        num = num + w * na[p]
        den = den + w * nl[p]
    o_ref[...] = (num / den).astype(o_ref.dtype)


def _dilated_attention(proj):
    B, S, _ = proj.shape
    n_pairs = ATTN_WIDTH // LANES
    assert S % (16 * DILATIONS[-1]) == 0
    blk = lambda off: pl.BlockSpec((None, S, LANES), lambda b, h: (b, 0, off + h))
    return pl.pallas_call(
        functools.partial(_attn_kernel, S=S),
        grid=(B, n_pairs),
        in_specs=[blk(0), blk(n_pairs), blk(2 * n_pairs)],
        out_specs=pl.BlockSpec((None, S, LANES), lambda b, h: (b, 0, h)),
        out_shape=jax.ShapeDtypeStruct((B, S, ATTN_WIDTH), BF16),
        scratch_shapes=[pltpu.VMEM((S, LANES), BF16)] * 3
                       + [pltpu.VMEM((S, LANES), F32)] * 6
                       + [pltpu.VMEM((len(DILATIONS), S, LANES), F32)] * 3
                       + [pltpu.VMEM((3, min(128, S), min(256, S)), F32)],
        compiler_params=_params(("arbitrary", "arbitrary")),
        name="dilated_attention",
    )(proj, proj, proj)


REC_GROUP_CHUNKS = 4
REC_GROUP_ROWS = REC_GROUP_CHUNKS * REC_CHUNK
REC_STAGE_GROUPS = 4


def _hgrn_kernel(rq_ref, zf_ref, zb_ref, ri_ref, rg_ref, lb_ref, gain_ref, o_ref,
                 q_s, vt_s, oi_s, qp_s, ut_s, dd_s, st_s, os_s, *, S):
    C, G, GR = REC_CHUNK, REC_GROUP_CHUNKS, REC_GROUP_ROWS
    nc, ng = S // C, S // GR
    sg = min(REC_STAGE_GROUPS, ng)
    assert ng % sg == 0

    row = lax.broadcasted_iota(I32, (GR, GR), 0)
    col = lax.broadcasted_iota(I32, (GR, GR), 1)
    same_chunk = (row // C) == (col // C)
    t_i = lax.broadcasted_iota(I32, (C, C), 0)
    s_i = lax.broadcasted_iota(I32, (C, C), 1)

    def prep(i, carry):
        r0 = pl.multiple_of(i * GR, GR)
        rq = rq_ref[pl.ds(r0, GR), :]
        q_s[pl.ds(r0, GR), :] = rq * _sigmoid(rq)
        v = ri_ref[pl.ds(r0, GR), :]
        for c in range(G):
            vt_s[i * G + c] = v[c * C:(c + 1) * C, :].T.astype(BF16)
        return carry

    lax.fori_loop(0, ng, prep, 0)

    for direction in range(2):
        fwd = direction == 0
        z_ref = zf_ref if fwd else zb_ref
        cum = jnp.where(same_chunk & ((col <= row) if fwd else (col >= row)), 1.0, 0.0).astype(BF16)
        keep = (s_i <= t_i) if fwd else (s_i >= t_i)
        mid_row = C // 2 - 1 if fwd else C // 2
        last_row = C - 1 if fwd else 0

        def phase_a(it, carry, fwd=fwd, z_ref=z_ref, cum=cum, keep=keep,
                    mid_row=mid_row, last_row=last_row, direction=direction):
            lb = lb_ref[direction:direction + 1, :]
            log_lb = jnp.log(lb)
            log_1m_lb = jnp.log1p(-lb)
            r0s, c0s, k3s, bcs, atts, stage3 = [], [], [], [], [], []
            for u in range(sg):
                i = it * sg + u
                r0 = pl.multiple_of(i * GR, GR)
                z = z_ref[pl.ds(r0, GR), :]
                e = jnp.exp(-jnp.abs(z))
                t = log_1m_lb + (jnp.minimum(z, 0.0) - jnp.log(1.0 + e))
                g = jnp.maximum(log_lb, t) + jnp.log(1.0 + jnp.exp(-jnp.abs(log_lb - t)))
                kk = (1.0 - lb) * (jnp.where(z > 0, e, 1.0) / (1.0 + e))
                g = g * LOG2_E
                g1 = g.astype(BF16)
                r1 = g - g1.astype(F32)
                g2 = r1.astype(BF16)
                bcs.append(jnp.dot(cum, g1, preferred_element_type=F32)
                           + jnp.dot(cum, g2, preferred_element_type=F32))
                k3s.append(kk.reshape(G, C, REC_DIM))
                r0s.append(r0)
                c0s.append(pl.multiple_of(i * G, G))
            for u in range(sg):
                b3 = bcs[u].reshape(G, C, REC_DIM)
                mid = b3[:, mid_row:mid_row + 1, :]
                last = b3[:, last_row:last_row + 1, :]
                e3 = b3 - mid
                q3 = q_s[pl.ds(r0s[u], GR), :].reshape(G, C, REC_DIM)
                clamp = DECAY_EXP_CLAMP * LOG2_E
                qt = (q3 * jnp.exp2(jnp.minimum(e3, clamp))).astype(BF16)
                kt = (k3s[u] * jnp.exp2(jnp.minimum(-e3, clamp))).astype(BF16)
                atts.append(jnp.einsum('gtc,gsc->gts', qt, kt, preferred_element_type=F32))
                qp_s[pl.ds(r0s[u], GR), :] = (q3 * jnp.exp2(b3)).astype(BF16).reshape(GR, REC_DIM)
                stage3.append(((k3s[u] * jnp.exp2(last - b3)).astype(BF16), jnp.exp2(last)))
            for u in range(sg):
                att = jnp.where(keep[None], atts[u], 0.0).astype(BF16)
                v3 = ri_ref[pl.ds(r0s[u], GR), :].reshape(G, C, REC_DIM).astype(BF16)
                oi = jnp.einsum('gts,gsv->gtv', att, v3, preferred_element_type=F32)
                oi_s[pl.ds(r0s[u], GR), :] = oi.reshape(GR, REC_DIM)
                kp, decay = stage3[u]
                ut_s[pl.ds(c0s[u], G)] = jnp.einsum('gvs,gsc->gvc', vt_s[pl.ds(c0s[u], G)], kp,
                                                    preferred_element_type=F32)
                dd_s[pl.ds(c0s[u], G)] = decay
            return carry

        lax.fori_loop(0, ng // sg, phase_a, 0)

        def scan(c, st, fwd=fwd):
            idx = c if fwd else nc - 1 - c
            st_s[idx] = st.astype(BF16)
            return dd_s[idx] * st + ut_s[idx]

        lax.fori_loop(0, nc, scan, jnp.zeros((REC_DIM, REC_DIM), F32))

        def phase_c(it, carry, fwd=fwd):
            r0s = [pl.multiple_of((it * sg + u) * GR, GR) for u in range(sg)]
            oos = []
            for u in range(sg):
                c0 = pl.multiple_of((it * sg + u) * G, G)
                qp3 = qp_s[pl.ds(r0s[u], GR), :].reshape(G, C, REC_DIM)
                oos.append(jnp.einsum('gtc,gvc->gtv', qp3, st_s[pl.ds(c0, G)],
                                      preferred_element_type=F32))
            for u in range(sg):
                rows = pl.ds(r0s[u], GR)
                tot = oi_s[rows, :] + oos[u].reshape(GR, REC_DIM)
                os_s[rows, :] = tot if fwd else os_s[rows, :] + tot
            return carry

        lax.fori_loop(0, ng // sg, phase_c, 0)

    o = os_s[...]
    o = o * lax.rsqrt(jnp.mean(o * o, axis=-1, keepdims=True) + RMS_EPS) * gain_ref[...]
    o_ref[...] = (o * _sigmoid(rg_ref[...])).astype(o_ref.dtype)


def _hgrn2(proj, lb, rec_gain):
    B, S, _ = proj.shape
    nh = REC_WIDTH // REC_DIM
    assert S % REC_GROUP_ROWS == 0
    col0 = 3 * ATTN_WIDTH // LANES
    blk = lambda k: pl.BlockSpec((None, S, REC_DIM), lambda b, h: (b, 0, col0 + k * nh + h))
    nc = S // REC_CHUNK
    return pl.pallas_call(
        functools.partial(_hgrn_kernel, S=S),
        grid=(B, nh),
        in_specs=[blk(0), blk(1), blk(2), blk(3), blk(4),
                  pl.BlockSpec((2, REC_DIM), lambda b, h: (0, h)),
                  pl.BlockSpec((1, REC_DIM), lambda b, h: (0, h))],
        out_specs=pl.BlockSpec((None, S, REC_DIM), lambda b, h: (b, 0, h)),
        out_shape=jax.ShapeDtypeStruct((B, S, REC_WIDTH), BF16),
        scratch_shapes=[pltpu.VMEM((S, REC_DIM), F32),
                        pltpu.VMEM((nc, REC_DIM, REC_CHUNK), BF16),
                        pltpu.VMEM((S, REC_DIM), F32),
                        pltpu.VMEM((S, REC_DIM), BF16),
                        pltpu.VMEM((nc, REC_DIM, REC_DIM), F32),
                        pltpu.VMEM((nc, 1, REC_DIM), F32),
                        pltpu.VMEM((nc, REC_DIM, REC_DIM), BF16),
                        pltpu.VMEM((S, REC_DIM), F32)],
        compiler_params=_params(("arbitrary", "arbitrary")),
        name="hgrn2",
    )(proj, proj, proj, proj, proj, lb, rec_gain)


def _outproj_kernel(a_ref, r_ref, x_ref, gate_ref, ag_ref, w_ref, lg_ref, lb_ref, o_ref, *, alpha):
    a = a_ref[...].astype(F32)
    an = a * lax.rsqrt(jnp.mean(a * a, axis=-1, keepdims=True) + RMS_EPS) * ag_ref[...]
    y = (jnp.dot(an.astype(BF16), w_ref[:ATTN_WIDTH, :], preferred_element_type=F32)
         + jnp.dot(r_ref[...], w_ref[ATTN_WIDTH:, :], preferred_element_type=F32))
    z = alpha * x_ref[...] + (1.0 + gate_ref[...]) * y
    o_ref[...] = _layer_norm(z, lg_ref[...], lb_ref[...])


def _out_projection(attn, rec, x, gate, attn_gain, w_out_bf16, ln_g, ln_b, *, alpha, tm):
    B, S, D = x.shape
    row = lambda w: pl.BlockSpec((None, tm, w), lambda b, i: (b, i, 0))
    vec = lambda w: pl.BlockSpec((1, w), lambda b, i: (0, 0))
    return pl.pallas_call(
        functools.partial(_outproj_kernel, alpha=alpha),
        grid=(B, S // tm),
        in_specs=[row(ATTN_WIDTH), row(REC_WIDTH), row(D),
                  pl.BlockSpec((None, 1, D), lambda b, i: (b, 0, 0)),
                  vec(ATTN_WIDTH),
                  pl.BlockSpec((ATTN_WIDTH + REC_WIDTH, D), lambda b, i: (0, 0)),
                  vec(D), vec(D)],
        out_specs=row(D),
        out_shape=jax.ShapeDtypeStruct((B, S, D), F32),
        compiler_params=_params(("arbitrary", "arbitrary")),
        name="out_projection",
    )(attn, rec, x, gate, attn_gain, w_out_bf16, ln_g, ln_b)


def _ffn_kernel(x_ref, sc_ref, sh_ref, gate_ref, wg_ref, wu_ref, wd_ref, lg_ref, lb_ref, o_ref,
                *, alpha, tc):
    x = x_ref[...]
    u = (x * (1.0 + sc_ref[...]) + sh_ref[...]).astype(BF16)
    acc = jnp.zeros(x.shape, F32)
    for c in range(wg_ref.shape[1] // tc):
        cols = slice(c * tc, (c + 1) * tc)
        g = jnp.dot(u, wg_ref[:, cols], preferred_element_type=F32)
        up = jnp.dot(u, wu_ref[:, cols], preferred_element_type=F32)
        h = (g * _sigmoid(g) * up).astype(BF16)
        acc = acc + jnp.dot(h, wd_ref[cols, :], preferred_element_type=F32)
    z = alpha * x + (1.0 + gate_ref[...]) * acc
    o_ref[...] = _layer_norm(z, lg_ref[...], lb_ref[...])


def _dense_ffn(x, scale, shift, gate, wg, wu, wd, ln_g, ln_b, *, alpha, tm, tc):
    B, S, D = x.shape
    ffp = wg.shape[1]
    assert ffp % tc == 0
    row = pl.BlockSpec((None, tm, D), lambda b, i: (b, i, 0))
    mod = pl.BlockSpec((None, 1, D), lambda b, i: (b, 0, 0))
    vec = pl.BlockSpec((1, D), lambda b, i: (0, 0))
    return pl.pallas_call(
        functools.partial(_ffn_kernel, alpha=alpha, tc=tc),
        grid=(B, S // tm),
        in_specs=[row, mod, mod, mod,
                  pl.BlockSpec((D, ffp), lambda b, i: (0, 0)),
                  pl.BlockSpec((D, ffp), lambda b, i: (0, 0)),
                  pl.BlockSpec((ffp, D), lambda b, i: (0, 0)),
                  vec, vec],
        out_specs=row,
        out_shape=jax.ShapeDtypeStruct((B, S, D), F32),
        compiler_params=_params(("arbitrary", "arbitrary")),
        name="dense_ffn",
    )(x, scale, shift, gate, wg, wu, wd, ln_g, ln_b)


META_E0, META_E1, META_W0, META_W1, META_R0, META_R1 = range(6)
SEGMENT_ALIGN = 16
DISPATCH_ROWS = 256
DISPATCH_HALF = DISPATCH_ROWS // 2
DISPATCH_SLACK = DISPATCH_HALF


def _router_kernel(x_ref, sc_ref, sh_ref, wr_ref, u_ref, meta_ref, cnt_ref, carry_ref, *, n_exp):
    tm = x_ref.shape[0]

    @pl.when((pl.program_id(0) == 0) & (pl.program_id(1) == 0))
    def _():
        carry_ref[...] = jnp.zeros_like(carry_ref)

    u = x_ref[...] * (1.0 + sc_ref[...]) + sh_ref[...]
    u_ref[...] = u.astype(BF16)
    u_hi = u.astype(BF16)
    u_lo = (u - u_hi.astype(F32)).astype(BF16)
    logits = (jnp.dot(u_hi, wr_ref[0], preferred_element_type=F32)
              + jnp.dot(u_lo, wr_ref[0], preferred_element_type=F32)
              + jnp.dot(u_hi, wr_ref[1], preferred_element_type=F32))
    lane = lax.broadcasted_iota(I32, (tm, LANES), 1)
    neg = -jnp.inf
    l1 = jnp.where(lane < n_exp, logits, neg)
    m1 = jnp.max(l1, axis=1, keepdims=True)
    i1 = jnp.min(jnp.where(l1 == m1, lane, LANES), axis=1, keepdims=True)
    l2 = jnp.where(lane == i1, neg, l1)
    m2 = jnp.max(l2, axis=1, keepdims=True)
    i2 = jnp.min(jnp.where(l2 == m2, lane, LANES), axis=1, keepdims=True)
    e = jnp.exp(m2 - m1)
    w1 = 1.0 / (1.0 + e)
    w2 = e * w1
    sel = jnp.where((lane == i1) | (lane == i2), 1.0, 0.0)
    before = (lax.broadcasted_iota(I32, (tm, tm), 1) < lax.broadcasted_iota(I32, (tm, tm), 0))
    ranks = jnp.dot(jnp.where(before, 1.0, 0.0).astype(BF16), sel.astype(BF16),
                    preferred_element_type=F32) + carry_ref[...]
    r1 = jnp.sum(jnp.where(lane == i1, ranks, 0.0), axis=1, keepdims=True)
    r2 = jnp.sum(jnp.where(lane == i2, ranks, 0.0), axis=1, keepdims=True)
    carry_ref[...] = carry_ref[...] + jnp.sum(sel, axis=0, keepdims=True)
    cnt_ref[...] = carry_ref[...]
    meta = jnp.zeros((tm, LANES), F32)
    for k, val in ((META_E0, i1.astype(F32)), (META_E1, i2.astype(F32)), (META_W0, w1),
                   (META_W1, w2), (META_R0, r1), (META_R1, r2)):
        meta = jnp.where(lane == k, val, meta)
    meta_ref[...] = meta


def _router(x, scale, shift, w_router_lanes, *, n_exp, tm):
    B, S, D = x.shape
    nt = S // tm
    row = pl.BlockSpec((None, tm, D), lambda b, i: (b, i, 0))
    mod = pl.BlockSpec((None, 1, D), lambda b, i: (b, 0, 0))
    return pl.pallas_call(
        functools.partial(_router_kernel, n_exp=n_exp),
        grid=(B, nt),
        in_specs=[row, mod, mod, pl.BlockSpec((2, D, LANES), lambda b, i: (0, 0, 0))],
        out_specs=[row,
                   pl.BlockSpec((None, tm, LANES), lambda b, i: (b, i, 0)),
                   pl.BlockSpec((1, LANES), lambda b, i: (0, 0))],
        out_shape=[jax.ShapeDtypeStruct((B, S, D), BF16),
                   jax.ShapeDtypeStruct((B, S, LANES), F32),
                   jax.ShapeDtypeStruct((1, LANES), F32)],
        scratch_shapes=[pltpu.VMEM((1, LANES), F32)],
        compiler_params=_params(("arbitrary", "arbitrary")),
        name="moe_router",
    )(x, scale, shift, w_router_lanes)


def _rows_at(hbm, first_row, n_rows):
    return hbm.at[pl.ds(pl.multiple_of(first_row, SEGMENT_ALIGN), n_rows)]


def _dispatch_kernel(base_ref, cnt_ref, padlo_ref, padn_ref, u_ref, meta_ref, xs_hbm,
                     stage, extra, zeros_buf, sems, extra_sem, *, n_exp):
    t = pl.program_id(0)
    last = pl.num_programs(0) - 1
    tm = u_ref.shape[0]
    R, H = DISPATCH_ROWS, DISPATCH_HALF
    slot = t % 2
    u = u_ref[...]
    e0, e1 = meta_ref[0:1, :], meta_ref[1:2, :]
    d0, d1 = meta_ref[2:3, :], meta_ref[3:4, :]
    srow = lax.broadcasted_iota(I32, (R, tm), 0)

    def main_copies(step, buf, fn):
        for e in range(n_exp):
            base = base_ref[step * n_exp + e]
            fn(pltpu.make_async_copy(stage.at[buf, pl.ds(e * R, H)], _rows_at(xs_hbm, base, H),
                                     sems.at[buf]))

            @pl.when(cnt_ref[step * n_exp + e] > H)
            def _(e=e, base=base):
                fn(pltpu.make_async_copy(stage.at[buf, pl.ds(e * R + H, H)],
                                         _rows_at(xs_hbm, base + H, H), sems.at[buf]))

    rels = []
    for e in range(n_exp):
        rel = jnp.where(e0 == e, d0, jnp.where(e1 == e, d1, -1))
        rels.append(jnp.where(rel >= 0, rel - base_ref[t * n_exp + e], -1))
    onehot = jnp.concatenate([jnp.where(srow == rel, 1.0, 0.0) for rel in rels], axis=0).astype(BF16)
    stage[slot] = jnp.dot(onehot, u, preferred_element_type=F32).astype(BF16)

    @pl.when(t > 0)
    def _():
        main_copies(t - 1, 1 - slot, lambda cp: cp.wait())
    main_copies(t, slot, lambda cp: cp.start())

    for e in range(n_exp):
        for blk in range(1, tm // R):
            @pl.when(cnt_ref[t * n_exp + e] > blk * R)
            def _(e=e, blk=blk):
                more = jnp.where(srow + blk * R == rels[e], 1.0, 0.0).astype(BF16)
                extra[...] = jnp.dot(more, u, preferred_element_type=F32).astype(BF16)
                cp = pltpu.make_async_copy(extra, _rows_at(xs_hbm, base_ref[t * n_exp + e] + blk * R, R),
                                           extra_sem)
                cp.start()
                cp.wait()

    @pl.when(t == last)
    def _():
        main_copies(t, slot, lambda cp: cp.wait())
        zeros_buf[...] = jnp.zeros_like(zeros_buf)
        pieces = [SEGMENT_ALIGN << b for b in range((R // SEGMENT_ALIGN).bit_length() - 1)]

        def pad_copies(fn):
            for p in range(padlo_ref.shape[0]):
                lo, n = padlo_ref[p], padn_ref[p]
                whole = n // R

                def body(k, carry, lo=lo):
                    fn(pltpu.make_async_copy(zeros_buf, _rows_at(xs_hbm, lo + k * R, R), extra_sem))
                    return carry
                lax.fori_loop(0, whole, body, 0)
                rem = n - whole * R
                for sz in pieces:
                    off = lo + whole * R + (rem // (2 * sz)) * (2 * sz)
                    pl.when((rem // sz) % 2 == 1)(functools.partial(
                        lambda off, sz: fn(pltpu.make_async_copy(
                            zeros_buf.at[pl.ds(0, sz)], _rows_at(xs_hbm, off, sz), extra_sem)), off, sz))

        pad_copies(lambda cp: cp.start())
        pad_copies(lambda cp: cp.wait())


def _dispatch(u, meta_t, base_te, cnt_te, pad_lo, pad_n, *, n_rows, tm):
    T, D = u.shape
    n_exp = pad_lo.shape[0] - 1
    assert tm % DISPATCH_ROWS == 0
    grid_spec = pltpu.PrefetchScalarGridSpec(
        num_scalar_prefetch=4,
        grid=(T // tm,),
        in_specs=[pl.BlockSpec((tm, D), lambda t, *_: (t, 0)),
                  pl.BlockSpec((meta_t.shape[0], tm), lambda t, *_: (0, t))],
        out_specs=pl.BlockSpec(memory_space=pl.ANY),
        scratch_shapes=[pltpu.VMEM((2, n_exp * DISPATCH_ROWS, D), BF16),
                        pltpu.VMEM((DISPATCH_ROWS, D), BF16),
                        pltpu.VMEM((DISPATCH_ROWS, D), BF16),
                        pltpu.SemaphoreType.DMA((2,)),
                        pltpu.SemaphoreType.DMA(())],
    )
    return pl.pallas_call(
        functools.partial(_dispatch_kernel, n_exp=n_exp),
        grid_spec=grid_spec,
        out_shape=jax.ShapeDtypeStruct((n_rows, D), BF16),
        compiler_params=_params(("arbitrary",)),
        name="moe_dispatch",
    )(base_te, cnt_te, pad_lo, pad_n, u, meta_t)


TILE_FULL, TILE_HALF, TILE_EMPTY = 0, 1, 2


def _expert_kernel(te_ref, na_ref, fill_ref, xs_ref, wg_ref, wu_ref, wd_ref, o_ref, acc_ref):
    i = pl.program_id(0)
    j = pl.program_id(1)
    last = pl.num_programs(1) - 1
    active = i < na_ref[0]
    fill = fill_ref[i]
    tm = acc_ref.shape[0]

    @pl.when(active & (j == 0))
    def _():
        acc_ref[...] = jnp.zeros_like(acc_ref)

    def swiglu(rows):
        xb = xs_ref[rows, :]
        g = jnp.dot(xb, wg_ref[...].astype(BF16), preferred_element_type=F32)
        up = jnp.dot(xb, wu_ref[...].astype(BF16), preferred_element_type=F32)
        h = (g * _sigmoid(g) * up).astype(BF16)
        acc_ref[rows, :] += jnp.dot(h, wd_ref[...].astype(BF16), preferred_element_type=F32)

    pl.when(active & (fill == TILE_FULL))(functools.partial(swiglu, pl.ds(0, tm)))
    pl.when(active & (fill == TILE_HALF))(functools.partial(swiglu, pl.ds(0, tm // 2)))

    @pl.when(active & (j == last))
    def _():
        o_ref[...] = acc_ref[...].astype(BF16)

    @pl.when(jnp.logical_not(active) & (j == last))
    def _():
        o_ref[...] = jnp.zeros_like(o_ref)


def _expert_ffn(xs, tile_expert, n_active, tile_fill, wg, wu, wd, *, tm, tf):
    n_rows, D = xs.shape
    ff = wg.shape[2]
    assert n_rows % tm == 0 and ff % tf == 0
    nj = ff // tf

    def jj(i, j, na):
        return jnp.where(i < na[0], j, nj - 1)

    def ii(i, na):
        return jnp.minimum(i, na[0] - 1)

    grid_spec = pltpu.PrefetchScalarGridSpec(
        num_scalar_prefetch=3,
        grid=(n_rows // tm, nj),
        in_specs=[pl.BlockSpec((tm, D), lambda i, j, te, na, tf_: (ii(i, na), 0)),
                  pl.BlockSpec((None, D, tf), lambda i, j, te, na, tf_: (te[i], 0, jj(i, j, na))),
                  pl.BlockSpec((None, D, tf), lambda i, j, te, na, tf_: (te[i], 0, jj(i, j, na))),
                  pl.BlockSpec((None, tf, D), lambda i, j, te, na, tf_: (te[i], jj(i, j, na), 0))],
        out_specs=pl.BlockSpec((tm, D), lambda i, j, te, na, tf_: (i, 0)),
        scratch_shapes=[pltpu.VMEM((tm, D), F32)],
    )
    return pl.pallas_call(
        _expert_kernel,
        grid_spec=grid_spec,
        out_shape=jax.ShapeDtypeStruct((n_rows, D), BF16),
        compiler_params=_params(("arbitrary", "arbitrary")),
        name="moe_experts",
    )(tile_expert, n_active, tile_fill, xs, wg, wu, wd)


def _combine_kernel(base_ref, off_ref, cnt_ref, ys_hbm, meta_ref, x_ref, gate_ref, lg_ref, lb_ref, o_ref,
                    stage, extra, acc_ref, sems, extra_sem, *, alpha, n_exp):
    t = pl.program_id(0) * pl.num_programs(1) + pl.program_id(1)
    n_steps = pl.num_programs(0) * pl.num_programs(1)
    tm = x_ref.shape[0]
    R = DISPATCH_ROWS
    slot = t % 2

    def main_copies(step, buf, fn):
        for e in range(n_exp):
            fn(pltpu.make_async_copy(_rows_at(ys_hbm, base_ref[step * n_exp + e], R),
                                     stage.at[buf, pl.ds(e * R, R)], sems.at[buf]))

    @pl.when(t == 0)
    def _():
        main_copies(0, 0, lambda cp: cp.start())

    @pl.when(t + 1 < n_steps)
    def _():
        main_copies(t + 1, 1 - slot, lambda cp: cp.start())

    meta = meta_ref[...]
    col = lambda k: meta[:, k:k + 1]
    e0, e1 = col(META_E0).astype(I32), col(META_E1).astype(I32)
    r0, r1 = col(META_R0).astype(I32), col(META_R1).astype(I32)
    w0, w1 = col(META_W0), col(META_W1)
    lane = lax.broadcasted_iota(I32, (tm, R), 1)

    rels, ws, picks = [], [], []
    for e in range(n_exp):
        rel = jnp.where(e0 == e, r0, jnp.where(e1 == e, r1, -1))
        rels.append(jnp.where(rel >= 0, rel - off_ref[t * n_exp + e], -1))
        ws.append(jnp.where(e0 == e, w0, jnp.where(e1 == e, w1, 0.0)))
        picks.append(jnp.where(lane == rels[e], ws[e], 0.0))
    weighted = jnp.concatenate(picks, axis=1).astype(BF16)
    main_copies(t, slot, lambda cp: cp.wait())
    acc_ref[...] = jnp.dot(weighted, stage[slot], preferred_element_type=F32)

    for e in range(n_exp):
        for blk in range(1, tm // R):
            @pl.when(cnt_ref[t * n_exp + e] > blk * R)
            def _(e=e, blk=blk):
                cp = pltpu.make_async_copy(_rows_at(ys_hbm, base_ref[t * n_exp + e] + blk * R, R),
                                           extra, extra_sem)
                cp.start()
                cp.wait()
                more = jnp.where(lane + blk * R == rels[e], ws[e], 0.0).astype(BF16)
                acc_ref[...] += jnp.dot(more, extra[...], preferred_element_type=F32)

    z = alpha * x_ref[...] + (1.0 + gate_ref[...]) * acc_ref[...]
    o_ref[...] = _layer_norm(z, lg_ref[...], lb_ref[...])


def _combine(ys, base_te, off_te, cnt_te, meta, x, gate, ln_g, ln_b, *, alpha, tm):
    B, S, D = x.shape
    nt = S // tm
    n_exp = base_te.shape[0] // (B * nt)
    row = pl.BlockSpec((None, tm, D), lambda b, i, *_: (b, i, 0))
    vec = pl.BlockSpec((1, D), lambda b, i, *_: (0, 0))
    grid_spec = pltpu.PrefetchScalarGridSpec(
        num_scalar_prefetch=3,
        grid=(B, nt),
        in_specs=[pl.BlockSpec(memory_space=pl.ANY),
                  pl.BlockSpec((None, tm, LANES), lambda b, i, *_: (b, i, 0)),
                  row, pl.BlockSpec((None, 1, D), lambda b, i, *_: (b, 0, 0)), vec, vec],
        out_specs=row,
        scratch_shapes=[pltpu.VMEM((2, n_exp * DISPATCH_ROWS, D), BF16),
                        pltpu.VMEM((DISPATCH_ROWS, D), BF16),
                        pltpu.VMEM((tm, D), F32),
                        pltpu.SemaphoreType.DMA((2,)),
                        pltpu.SemaphoreType.DMA(())],
    )
    return pl.pallas_call(
        functools.partial(_combine_kernel, alpha=alpha, n_exp=n_exp),
        grid_spec=grid_spec,
        out_shape=jax.ShapeDtypeStruct((B, S, D), F32),
        compiler_params=_params(("arbitrary", "arbitrary")),
        name="moe_combine",
    )(base_te, off_te, cnt_te, ys, meta, x, gate, ln_g, ln_b)


def _moe_ffn(x, scale, shift, gate, w_router, wg, wu, wd, ln_g, ln_b, *, alpha, tm, tm_e, tf):
    B, S, D = x.shape
    T = B * S
    nt = T // tm
    n_exp = w_router.shape[1]
    wr = jnp.zeros((D, LANES), F32).at[:, :n_exp].set(w_router)
    wr_hi = wr.astype(BF16)
    wr = jnp.stack([wr_hi, (wr - wr_hi.astype(F32)).astype(BF16)])
    u, meta, counts = _router(x, scale, shift, wr, n_exp=n_exp, tm=tm)

    meta2 = meta.reshape(T, LANES)
    e0 = meta2[:, META_E0].astype(I32)
    e1 = meta2[:, META_E1].astype(I32)
    r0 = meta2[:, META_R0].astype(I32)
    r1 = meta2[:, META_R1].astype(I32)
    experts = jnp.arange(n_exp, dtype=I32)
    chosen = ((e0[:, None] == experts) | (e1[:, None] == experts)).astype(I32)
    cnt_te = chosen.reshape(nt, tm, n_exp).sum(axis=1)
    seg_te = -(-cnt_te // SEGMENT_ALIGN) * SEGMENT_ALIGN
    off_te = jnp.cumsum(cnt_te, axis=0) - cnt_te
    filled = seg_te.sum(axis=0)
    sizes = -(-(filled + DISPATCH_SLACK) // tm_e) * tm_e
    ends = jnp.cumsum(sizes)
    starts = ends - sizes
    base_te = starts[None, :] + jnp.cumsum(seg_te, axis=0) - seg_te
    shift_te = (base_te - off_te)[:, None, :]

    def rows_of(e, r):
        pick = (e[:, None] == experts).astype(I32).reshape(nt, tm, n_exp)
        return r + (pick * shift_te).sum(axis=-1).reshape(T)

    d0, d1 = rows_of(e0, r0), rows_of(e1, r1)
    bound = TOP_K * T + n_exp * ((SEGMENT_ALIGN - 1) * nt + DISPATCH_SLACK)
    n_tiles = -(-bound // tm_e) + n_exp + 1
    n_active = (ends[-1] // tm_e).astype(I32)
    tile_id = jnp.minimum(jnp.arange(n_tiles, dtype=I32), n_active - 1)
    tile_expert = jnp.sum((ends[None, :] <= (tile_id * tm_e)[:, None]).astype(I32), axis=1)
    data_end = jnp.sum((tile_expert[:, None] == experts) * (starts + filled)[None, :], axis=1)
    left = data_end - tile_id * tm_e
    tile_fill = jnp.where(left <= 0, TILE_EMPTY, jnp.where(left <= tm_e // 2, TILE_HALF, TILE_FULL)).astype(I32)
    meta_t = jnp.stack([e0, e1, d0, d1] + [jnp.zeros_like(e0)] * 4)

    xs = _dispatch(u.reshape(T, D), meta_t, base_te.reshape(-1), cnt_te.reshape(-1),
                   jnp.concatenate([starts + filled, ends[-1:]]),
                   jnp.concatenate([sizes - filled, n_tiles * tm_e - ends[-1:]]),
                   n_rows=n_tiles * tm_e, tm=tm)
    ys = _expert_ffn(xs, tile_expert, n_active.reshape(1), tile_fill, wg, wu, wd, tm=tm_e, tf=tf)
    return _combine(ys, base_te.reshape(-1), off_te.reshape(-1), cnt_te.reshape(-1),
                    meta, x, gate, ln_g, ln_b, alpha=alpha, tm=tm)


def kernel(x, c, positions, w_in, w_out, attn_norm_gain, rec_norm_gain, rec_lb_logits, ada_w, ada_b,
           ln_gain, ln_bias, ffn_w_gate, ffn_w_up, ffn_w_down, moe_router, moe_w_gate, moe_w_up,
           moe_w_down):
    B, S, D = x.shape
    depth = w_in.shape[0]
    alpha = (2 * depth) ** 0.25

    p = jax.nn.softmax(rec_lb_logits.astype(F32), axis=0)
    cum = jnp.cumsum(p, axis=0)
    lb_all = cum - cum[0:1]
    half = ATTN_HEAD_DIM // 2
    inv_freq = ROPE_THETA ** (-jnp.arange(half, dtype=F32) / half)
    inv_freq_lanes = jnp.tile(inv_freq, LANES // half).reshape(1, LANES)
    pos_f = positions.astype(F32).reshape(B, S, 1)
    n_qk = 2 * ATTN_WIDTH
    w_in_b = jnp.concatenate([_qk_lane_order(w_in[..., :n_qk]), w_in[..., n_qk:]], axis=-1).astype(BF16)
    w_out_b = w_out.astype(BF16)
    ff = ffn_w_gate.shape[2]
    ffp = -(-ff // (2 * LANES)) * (2 * LANES)
    pad_c = lambda w: jnp.pad(w.astype(BF16), ((0, 0), (0, 0), (0, ffp - ff)))
    ffn_g, ffn_u = pad_c(ffn_w_gate), pad_c(ffn_w_up)
    ffn_d = jnp.pad(ffn_w_down.astype(BF16), ((0, 0), (0, ffp - ff), (0, 0)))

    mods = _ada_modulation(c, ada_w, ada_b)
    rope_cos, rope_sin = _rope_tables(pos_f, inv_freq_lanes, tm=min(1024, S))

    def mod(layer, sub):
        m = mods[layer * 2 + sub].reshape(B, 1, 3 * D)
        return m[..., :D], m[..., D:2 * D], m[..., 2 * D:]

    vec = lambda a: a.reshape(1, -1)
    for layer in range(depth):
        shift, scale, gate = mod(layer, 0)
        proj = _in_projection(x, scale, shift, rope_cos, rope_sin, w_in_b[layer], tm=min(1024, S))
        attn = _dilated_attention(proj)
        rec = _hgrn2(proj, lb_all[layer], vec(rec_norm_gain[layer]))
        x = _out_projection(attn, rec, x, gate, vec(attn_norm_gain[layer]), w_out_b[layer],
                            vec(ln_gain[layer, 0]), vec(ln_bias[layer, 0]), alpha=alpha, tm=min(512, S))
        shift, scale, gate = mod(layer, 1)
        j = layer // 2
        if layer % 2 == 0:
            x = _dense_ffn(x, scale, shift, gate, ffn_g[j], ffn_u[j], ffn_d[j],
                           vec(ln_gain[layer, 1]), vec(ln_bias[layer, 1]),
                           alpha=alpha, tm=min(512, S), tc=2 * LANES)
        else:
            x = _moe_ffn(x, scale, shift, gate, moe_router[j], moe_w_gate[j], moe_w_up[j],
                         moe_w_down[j], vec(ln_gain[layer, 1]), vec(ln_bias[layer, 1]),
                         alpha=alpha, tm=min(512, S), tm_e=min(1024, B * S), tf=512)
    return x
```

```python
import functools

import jax
import jax.numpy as jnp
from jax import lax
from jax.experimental import pallas as pl
from jax.experimental.pallas import tpu as pltpu

F32 = jnp.float32
BF16 = jnp.bfloat16
I32 = jnp.int32

LANES = 128
V7X_VMEM_LIMIT_BYTES = 56 * 1024 * 1024

ATTN_WIDTH = 512
ATTN_HEAD_DIM = 64
DILATIONS = (1, 4, 16)
ATTN_RADIUS = 64
ROPE_THETA = 10000.0
REC_WIDTH = 512
REC_DIM = 128
REC_CHUNK = 64
TOP_K = 2
LN_EPS = 1e-5
RMS_EPS = 1e-6
MASK_VALUE = -1e30
DECAY_EXP_CLAMP = 80.0
LOG2_E = 1.4426950408889634


def _params(semantics):
    return pltpu.CompilerParams(dimension_semantics=semantics,
                                vmem_limit_bytes=V7X_VMEM_LIMIT_BYTES)


def _sigmoid(x):
    return 1.0 / (1.0 + jnp.exp(-x))


def _layer_norm(z, gain, bias):
    mu = jnp.mean(z, axis=-1, keepdims=True)
    zc = z - mu
    var = jnp.mean(zc * zc, axis=-1, keepdims=True)
    return zc * lax.rsqrt(var + LN_EPS) * gain + bias


def _ada_kernel(c_ref, w_ref, b_ref, o_ref):
    c = c_ref[...]
    o_ref[...] = jnp.dot(c * _sigmoid(c), w_ref[...], precision=lax.Precision.HIGHEST,
                         preferred_element_type=F32) + b_ref[...]


def _ada_modulation(c, ada_w, ada_b):
    B, D = c.shape
    n = ada_w.shape[0] * ada_w.shape[1]
    n3 = ada_w.shape[-1]
    tn = 1024
    return pl.pallas_call(
        _ada_kernel,
        grid=(n, n3 // tn),
        in_specs=[pl.BlockSpec((B, D), lambda i, j: (0, 0)),
                  pl.BlockSpec((None, D, tn), lambda i, j: (i, 0, j)),
                  pl.BlockSpec((None, 1, tn), lambda i, j: (i, 0, j))],
        out_specs=pl.BlockSpec((None, B, tn), lambda i, j: (i, 0, j)),
        out_shape=jax.ShapeDtypeStruct((n, B, n3), F32),
        compiler_params=_params(("arbitrary", "arbitrary")),
        name="ada_modulation",
    )(c, ada_w.reshape(n, D, n3), ada_b.reshape(n, 1, n3))


ROPE_HALF = ATTN_HEAD_DIM // 2


def _qk_lane_order(w_cols):
    lead = w_cols.shape[:-1]
    w = w_cols.reshape(*lead, -1, 2, 2, ROPE_HALF)
    return jnp.swapaxes(w, -3, -2).reshape(*lead, -1)


def _rope_kernel(pos_ref, invf_ref, cos_ref, sin_ref):
    ang = pos_ref[...] * invf_ref[...]
    lane = lax.broadcasted_iota(I32, (1, LANES), 1)
    sin = jnp.sin(ang)
    cos_ref[...] = jnp.cos(ang)
    sin_ref[...] = jnp.where(lane < LANES // 2, -sin, sin)


def _rope_tables(pos_f, inv_freq_lanes, *, tm):
    B, S, _ = pos_f.shape
    tab = pl.BlockSpec((None, tm, LANES), lambda b, i: (b, i, 0))
    return pl.pallas_call(
        _rope_kernel,
        grid=(B, S // tm),
        in_specs=[pl.BlockSpec((None, tm, 1), lambda b, i: (b, i, 0)),
                  pl.BlockSpec((1, LANES), lambda b, i: (0, 0))],
        out_specs=[tab, tab],
        out_shape=[jax.ShapeDtypeStruct((B, S, LANES), F32)] * 2,
        compiler_params=_params(("arbitrary", "arbitrary")),
        name="rope_tables",
    )(pos_f, inv_freq_lanes)


def _inproj_kernel(x_ref, sc_ref, sh_ref, cos_ref, sin_ref, w_ref, o_ref, u_ref, *, tn):
    j = pl.program_id(2)

    @pl.when(j == 0)
    def _():
        u_ref[...] = (x_ref[...] * (1.0 + sc_ref[...]) + sh_ref[...]).astype(BF16)

    cols = pl.ds(pl.multiple_of(j * tn, tn), tn)
    acc = jnp.dot(u_ref[...], w_ref[:, cols], preferred_element_type=F32)

    @pl.when(j > 0)
    def _():
        for kk in range(tn // LANES):
            o_ref[kk] = acc[:, kk * LANES:(kk + 1) * LANES]

    @pl.when(j == 0)
    def _():
        cos = cos_ref[...]
        sin_signed = sin_ref[...]
        q_scale = ATTN_HEAD_DIM ** -0.5
        for kk in range(tn // LANES):
            c = acc[:, kk * LANES:(kk + 1) * LANES]
            r = c * cos + pltpu.roll(c, LANES // 2, 1) * sin_signed
            if kk * LANES < ATTN_WIDTH:
                r = r * q_scale
            o_ref[kk] = r


def _in_projection(x, scale, shift, rope_cos, rope_sin, w_in_bf16, *, tm):
    B, S, D = x.shape
    n_cols = w_in_bf16.shape[1]
    tn = 2 * ATTN_WIDTH
    assert S % tm == 0 and n_cols % tn == 0
    return pl.pallas_call(
        functools.partial(_inproj_kernel, tn=tn),
        grid=(B, S // tm, n_cols // tn),
        in_specs=[pl.BlockSpec((None, tm, D), lambda b, i, j: (b, i, 0)),
                  pl.BlockSpec((None, 1, D), lambda b, i, j: (b, 0, 0)),
                  pl.BlockSpec((None, 1, D), lambda b, i, j: (b, 0, 0)),
                  pl.BlockSpec((None, tm, LANES), lambda b, i, j: (b, i, 0)),
                  pl.BlockSpec((None, tm, LANES), lambda b, i, j: (b, i, 0)),
                  pl.BlockSpec((D, n_cols), lambda b, i, j: (0, 0))],
        out_specs=pl.BlockSpec((None, tn // LANES, tm, LANES), lambda b, i, j: (b, j, i, 0)),
        out_shape=jax.ShapeDtypeStruct((B, n_cols // LANES, S, LANES), F32),
        scratch_shapes=[pltpu.VMEM((tm, D), BF16)],
        compiler_params=_params(("arbitrary", "arbitrary", "arbitrary")),
        name="in_projection",
    )(x, scale, shift, rope_cos, rope_sin, w_in_bf16)


ATTN_BLOCK_GROUP = 8


def _attn_kernel(q_ref, k_ref, v_ref, o_ref, qs, ks, vs, qf, kf, vf, pm, plr, pa, nm, nl, na, bias_s,
                 *, S):
    lane = lax.broadcasted_iota(I32, (1, LANES), 1)
    head0 = lane < ATTN_HEAD_DIM
    head0_qk = (lane % ATTN_HEAD_DIM) < ROPE_HALF

    for p, d in enumerate(DILATIONS):
        L = S // d
        tq = min(128, L)
        W = min(2 * tq, L)
        nb = L // tq

        d_prev = DILATIONS[p - 1] if p else 1
        ratio, l_prev = d // d_prev, S // d_prev
        keep_f32 = 0 < p < len(DILATIONS) - 1
        for src, stage, dst in ((q_ref, qf, qs), (k_ref, kf, ks), (v_ref, vf, vs)):
            src = src if p <= 1 else stage
            for r in range(d):
                a, r_prev = divmod(r, d_prev)
                rows = pl.ds(r_prev * l_prev + a, L, stride=ratio) if ratio > 1 else pl.ds(0, L)
                val = src[rows, :]
                if keep_f32:
                    stage[r * L:(r + 1) * L, :] = val
                dst[r * L:(r + 1) * L, :] = val.astype(BF16)

        n_blocks = d * nb
        rc = lax.broadcasted_iota(I32, (tq, W), 0) - lax.broadcasted_iota(I32, (tq, W), 1)
        for which, delta in enumerate((0, ATTN_RADIUS, tq)):
            bias_s[which, pl.ds(0, tq), pl.ds(0, W)] = jnp.where(
                jnp.abs(rc + delta) <= ATTN_RADIUS, 0.0, MASK_VALUE)
        group = min(ATTN_BLOCK_GROUP, n_blocks)
        assert n_blocks % group == 0

        def block_group(gi, carry, L=L, tq=tq, W=W, nb=nb, group=group):
            q0s, k0s, scores, probs, outs, sums = [], [], [], [], [], []
            for u in range(group):
                g = gi * group + u
                r = g // nb
                n = g - r * nb
                base = r * L
                q0 = pl.multiple_of(base + n * tq, tq)
                ws = jnp.clip(n * tq - ATTN_RADIUS, 0, L - W)
                k0 = pl.multiple_of(base + ws, 16)
                qb = qs[pl.ds(q0, tq), :]
                zero = jnp.zeros_like(qb)
                q2 = jnp.concatenate([jnp.where(head0_qk, qb, zero), jnp.where(head0_qk, zero, qb)], axis=0)
                s = lax.dot_general(q2, ks[pl.ds(k0, W), :], (((1,), (1,)), ((), ())),
                                    preferred_element_type=F32)
                delta = n * tq - ws
                which = jnp.where(delta == 0, 0, jnp.where(delta == ATTN_RADIUS, 1, 2))
                bias = bias_s[which, pl.ds(0, tq), pl.ds(0, W)]
                scores.append(s + jnp.concatenate([bias, bias], axis=0))
                q0s.append(q0)
                k0s.append(k0)
            for u in range(group):
                s = scores[u]
                m = jnp.max(s, axis=1, keepdims=True)
                e = jnp.exp(s - m)
                probs.append((m, e.astype(BF16)))
                sums.append(jnp.sum(e, axis=1, keepdims=True))
            for u in range(group):
                outs.append(jnp.dot(probs[u][1], vs[pl.ds(k0s[u], W), :], preferred_element_type=F32))
            for u in range(group):
                m, l = probs[u][0], sums[u]
                rows = pl.ds(q0s[u], tq)
                out_m[rows, :] = jnp.where(head0, m[:tq], m[tq:])
                out_l[rows, :] = jnp.where(head0, l[:tq], l[tq:])
                out_a[rows, :] = jnp.where(head0, outs[u][:tq], outs[u][tq:])
            return carry

        out_m, out_l, out_a = (nm.at[p], nl.at[p], na.at[p]) if d == 1 else (pm, plr, pa)
        lax.fori_loop(0, n_blocks // group, block_group, 0)

        for r in range(d if d > 1 else 0):
            rows = pl.ds(r, L, stride=d)
            nm[p, rows, :] = pm[r * L:(r + 1) * L, :]
            nl[p, rows, :] = plr[r * L:(r + 1) * L, :]
            na[p, rows, :] = pa[r * L:(r + 1) * L, :]

    m_all = jnp.maximum(jnp.maximum(nm[0], nm[1]), nm[2])
    num = jnp.zeros((S, LANES), F32)
    den = jnp.zeros((S, LANES), F32)
    for p in range(len(DILATIONS)):
        w = jnp.exp(nm[p] - m_all)
        num = num + w * na[p]
        den = den + w * nl[p]
    o_ref[...] = (num / den).astype(o_ref.dtype)


def _dilated_attention(proj):
    B, _, S, _ = proj.shape
    n_pairs = ATTN_WIDTH // LANES
    assert S % (16 * DILATIONS[-1]) == 0
    blk = lambda off: pl.BlockSpec((None, None, S, LANES), lambda b, h: (b, off + h, 0, 0))
    return pl.pallas_call(
        functools.partial(_attn_kernel, S=S),
        grid=(B, n_pairs),
        in_specs=[blk(0), blk(n_pairs), blk(2 * n_pairs)],
        out_specs=pl.BlockSpec((None, S, LANES), lambda b, h: (b, 0, h)),
        out_shape=jax.ShapeDtypeStruct((B, S, ATTN_WIDTH), BF16),
        scratch_shapes=[pltpu.VMEM((S, LANES), BF16)] * 3
                       + [pltpu.VMEM((S, LANES), F32)] * 6
                       + [pltpu.VMEM((len(DILATIONS), S, LANES), F32)] * 3
                       + [pltpu.VMEM((3, min(128, S), min(256, S)), F32)],
        compiler_params=_params(("arbitrary", "arbitrary")),
        name="dilated_attention",
    )(proj, proj, proj)


REC_GROUP_CHUNKS = 4
REC_GROUP_ROWS = REC_GROUP_CHUNKS * REC_CHUNK
REC_STAGE_GROUPS = 4


def _hgrn_kernel(rq_ref, zf_ref, zb_ref, ri_ref, rg_ref, lb_ref, gain_ref, o_ref,
                 q_s, vt_s, oi_s, qp_s, ut_s, dd_s, st_s, os_s, *, S):
    C, G, GR = REC_CHUNK, REC_GROUP_CHUNKS, REC_GROUP_ROWS
    nc, ng = S // C, S // GR
    sg = min(REC_STAGE_GROUPS, ng)
    assert ng % sg == 0

    row = lax.broadcasted_iota(I32, (GR, GR), 0)
    col = lax.broadcasted_iota(I32, (GR, GR), 1)
    same_chunk = (row // C) == (col // C)
    t_i = lax.broadcasted_iota(I32, (C, C), 0)
    s_i = lax.broadcasted_iota(I32, (C, C), 1)

    def prep(i, carry):
        r0 = pl.multiple_of(i * GR, GR)
        rq = rq_ref[pl.ds(r0, GR), :]
        q_s[pl.ds(r0, GR), :] = rq * _sigmoid(rq)
        v = ri_ref[pl.ds(r0, GR), :]
        for c in range(G):
            vt_s[i * G + c] = v[c * C:(c + 1) * C, :].T.astype(BF16)
        return carry

    lax.fori_loop(0, ng, prep, 0)

    for direction in range(2):
        fwd = direction == 0
        z_ref = zf_ref if fwd else zb_ref
        cum = jnp.where(same_chunk & ((col <= row) if fwd else (col >= row)), 1.0, 0.0).astype(BF16)
        keep = (s_i <= t_i) if fwd else (s_i >= t_i)
        mid_row = C // 2 - 1 if fwd else C // 2
        last_row = C - 1 if fwd else 0

        def phase_a(it, carry, fwd=fwd, z_ref=z_ref, cum=cum, keep=keep,
                    mid_row=mid_row, last_row=last_row, direction=direction):
            lb = lb_ref[direction:direction + 1, :]
            log_lb = jnp.log(lb)
            log_1m_lb = jnp.log1p(-lb)
            r0s, c0s, k3s, bcs, atts, stage3 = [], [], [], [], [], []
            for u in range(sg):
                i = it * sg + u
                r0 = pl.multiple_of(i * GR, GR)
                z = z_ref[pl.ds(r0, GR), :]
                e = jnp.exp(-jnp.abs(z))
                t = log_1m_lb + (jnp.minimum(z, 0.0) - jnp.log(1.0 + e))
                g = jnp.maximum(log_lb, t) + jnp.log(1.0 + jnp.exp(-jnp.abs(log_lb - t)))
                kk = (1.0 - lb) * (jnp.where(z > 0, e, 1.0) / (1.0 + e))
                g = g * LOG2_E
                g1 = g.astype(BF16)
                r1 = g - g1.astype(F32)
                g2 = r1.astype(BF16)
                g3 = (r1 - g2.astype(F32)).astype(BF16)
                bcs.append(jnp.dot(cum, g1, preferred_element_type=F32)
                           + jnp.dot(cum, g2, preferred_element_type=F32)
                           + jnp.dot(cum, g3, preferred_element_type=F32))
                k3s.append(kk.reshape(G, C, REC_DIM))
                r0s.append(r0)
                c0s.append(pl.multiple_of(i * G, G))
            for u in range(sg):
                b3 = bcs[u].reshape(G, C, REC_DIM)
                mid = b3[:, mid_row:mid_row + 1, :]
                last = b3[:, last_row:last_row + 1, :]
                e3 = b3 - mid
                q3 = q_s[pl.ds(r0s[u], GR), :].reshape(G, C, REC_DIM)
                clamp = DECAY_EXP_CLAMP * LOG2_E
                qt = (q3 * jnp.exp2(jnp.minimum(e3, clamp))).astype(BF16)
                kt = (k3s[u] * jnp.exp2(jnp.minimum(-e3, clamp))).astype(BF16)
                atts.append(jnp.einsum('gtc,gsc->gts', qt, kt, preferred_element_type=F32))
                qp_s[pl.ds(r0s[u], GR), :] = (q3 * jnp.exp2(b3)).astype(BF16).reshape(GR, REC_DIM)
                stage3.append(((k3s[u] * jnp.exp2(last - b3)).astype(BF16), jnp.exp2(last)))
            for u in range(sg):
                att = jnp.where(keep[None], atts[u], 0.0).astype(BF16)
                v3 = ri_ref[pl.ds(r0s[u], GR), :].reshape(G, C, REC_DIM).astype(BF16)
                oi = jnp.einsum('gts,gsv->gtv', att, v3, preferred_element_type=F32)
                oi_s[pl.ds(r0s[u], GR), :] = oi.reshape(GR, REC_DIM)
                kp, decay = stage3[u]
                ut_s[pl.ds(c0s[u], G)] = jnp.einsum('gvs,gsc->gvc', vt_s[pl.ds(c0s[u], G)], kp,
                                                    preferred_element_type=F32)
                dd_s[pl.ds(c0s[u], G)] = decay
            return carry

        lax.fori_loop(0, ng // sg, phase_a, 0)

        def scan(c, st, fwd=fwd):
            idx = c if fwd else nc - 1 - c
            st_s[idx] = st.astype(BF16)
            return dd_s[idx] * st + ut_s[idx]

        lax.fori_loop(0, nc, scan, jnp.zeros((REC_DIM, REC_DIM), F32))

        def phase_c(it, carry, fwd=fwd):
            r0s = [pl.multiple_of((it * sg + u) * GR, GR) for u in range(sg)]
            oos = []
            for u in range(sg):
                c0 = pl.multiple_of((it * sg + u) * G, G)
                qp3 = qp_s[pl.ds(r0s[u], GR), :].reshape(G, C, REC_DIM)
                oos.append(jnp.einsum('gtc,gvc->gtv', qp3, st_s[pl.ds(c0, G)],
                                      preferred_element_type=F32))
            for u in range(sg):
                rows = pl.ds(r0s[u], GR)
                tot = oi_s[rows, :] + oos[u].reshape(GR, REC_DIM)
                os_s[rows, :] = tot if fwd else os_s[rows, :] + tot
            return carry

        lax.fori_loop(0, ng // sg, phase_c, 0)

    o = os_s[...]
    o = o * lax.rsqrt(jnp.mean(o * o, axis=-1, keepdims=True) + RMS_EPS) * gain_ref[...]
    o_ref[...] = (o * _sigmoid(rg_ref[...])).astype(o_ref.dtype)


def _hgrn2(proj, lb, rec_gain):
    B, _, S, _ = proj.shape
    nh = REC_WIDTH // REC_DIM
    assert S % REC_GROUP_ROWS == 0 and REC_DIM == LANES
    col0 = 3 * ATTN_WIDTH // LANES
    blk = lambda k: pl.BlockSpec((None, None, S, REC_DIM), lambda b, h: (b, col0 + k * nh + h, 0, 0))
    nc = S // REC_CHUNK
    return pl.pallas_call(
        functools.partial(_hgrn_kernel, S=S),
        grid=(B, nh),
        in_specs=[blk(0), blk(1), blk(2), blk(3), blk(4),
                  pl.BlockSpec((2, REC_DIM), lambda b, h: (0, h)),
                  pl.BlockSpec((1, REC_DIM), lambda b, h: (0, h))],
        out_specs=pl.BlockSpec((None, S, REC_DIM), lambda b, h: (b, 0, h)),
        out_shape=jax.ShapeDtypeStruct((B, S, REC_WIDTH), BF16),
        scratch_shapes=[pltpu.VMEM((S, REC_DIM), F32),
                        pltpu.VMEM((nc, REC_DIM, REC_CHUNK), BF16),
                        pltpu.VMEM((S, REC_DIM), F32),
                        pltpu.VMEM((S, REC_DIM), BF16),
                        pltpu.VMEM((nc, REC_DIM, REC_DIM), F32),
                        pltpu.VMEM((nc, 1, REC_DIM), F32),
                        pltpu.VMEM((nc, REC_DIM, REC_DIM), BF16),
                        pltpu.VMEM((S, REC_DIM), F32)],
        compiler_params=_params(("arbitrary", "arbitrary")),
        name="hgrn2",
    )(proj, proj, proj, proj, proj, lb, rec_gain)


def _outproj_kernel(a_ref, r_ref, x_ref, gate_ref, ag_ref, w_ref, lg_ref, lb_ref, o_ref, *, alpha):
    a = a_ref[...].astype(F32)
    an = a * lax.rsqrt(jnp.mean(a * a, axis=-1, keepdims=True) + RMS_EPS) * ag_ref[...]
    y = (jnp.dot(an.astype(BF16), w_ref[:ATTN_WIDTH, :], preferred_element_type=F32)
         + jnp.dot(r_ref[...], w_ref[ATTN_WIDTH:, :], preferred_element_type=F32))
    z = alpha * x_ref[...] + (1.0 + gate_ref[...]) * y
    o_ref[...] = _layer_norm(z, lg_ref[...], lb_ref[...])


def _out_projection(attn, rec, x, gate, attn_gain, w_out_bf16, ln_g, ln_b, *, alpha, tm):
    B, S, D = x.shape
    row = lambda w: pl.BlockSpec((None, tm, w), lambda b, i: (b, i, 0))
    vec = lambda w: pl.BlockSpec((1, w), lambda b, i: (0, 0))
    return pl.pallas_call(
        functools.partial(_outproj_kernel, alpha=alpha),
        grid=(B, S // tm),
        in_specs=[row(ATTN_WIDTH), row(REC_WIDTH), row(D),
                  pl.BlockSpec((None, 1, D), lambda b, i: (b, 0, 0)),
                  vec(ATTN_WIDTH),
                  pl.BlockSpec((ATTN_WIDTH + REC_WIDTH, D), lambda b, i: (0, 0)),
                  vec(D), vec(D)],
        out_specs=row(D),
        out_shape=jax.ShapeDtypeStruct((B, S, D), F32),
        compiler_params=_params(("arbitrary", "arbitrary")),
        name="out_projection",
    )(attn, rec, x, gate, attn_gain, w_out_bf16, ln_g, ln_b)


def _ffn_kernel(x_ref, sc_ref, sh_ref, gate_ref, wg_ref, wu_ref, wd_ref, lg_ref, lb_ref, o_ref,
                *, alpha, tc):
    x = x_ref[...]
    u = (x * (1.0 + sc_ref[...]) + sh_ref[...]).astype(BF16)
    acc = jnp.zeros(x.shape, F32)
    for c in range(wg_ref.shape[1] // tc):
        cols = slice(c * tc, (c + 1) * tc)
        g = jnp.dot(u, wg_ref[:, cols], preferred_element_type=F32)
        up = jnp.dot(u, wu_ref[:, cols], preferred_element_type=F32)
        h = (g * _sigmoid(g) * up).astype(BF16)
        acc = acc + jnp.dot(h, wd_ref[cols, :], preferred_element_type=F32)
    z = alpha * x + (1.0 + gate_ref[...]) * acc
    o_ref[...] = _layer_norm(z, lg_ref[...], lb_ref[...])


def _dense_ffn(x, scale, shift, gate, wg, wu, wd, ln_g, ln_b, *, alpha, tm, tc):
    B, S, D = x.shape
    ffp = wg.shape[1]
    assert ffp % tc == 0
    row = pl.BlockSpec((None, tm, D), lambda b, i: (b, i, 0))
    mod = pl.BlockSpec((None, 1, D), lambda b, i: (b, 0, 0))
    vec = pl.BlockSpec((1, D), lambda b, i: (0, 0))
    return pl.pallas_call(
        functools.partial(_ffn_kernel, alpha=alpha, tc=tc),
        grid=(B, S // tm),
        in_specs=[row, mod, mod, mod,
                  pl.BlockSpec((D, ffp), lambda b, i: (0, 0)),
                  pl.BlockSpec((D, ffp), lambda b, i: (0, 0)),
                  pl.BlockSpec((ffp, D), lambda b, i: (0, 0)),
                  vec, vec],
        out_specs=row,
        out_shape=jax.ShapeDtypeStruct((B, S, D), F32),
        compiler_params=_params(("arbitrary", "arbitrary")),
        name="dense_ffn",
    )(x, scale, shift, gate, wg, wu, wd, ln_g, ln_b)


META_E0, META_E1, META_W0, META_W1, META_R0, META_R1 = range(6)
SEGMENT_ALIGN = 16
DISPATCH_ROWS = 256
DISPATCH_HALF = DISPATCH_ROWS // 2
DISPATCH_SLACK = DISPATCH_HALF


def _router_kernel(x_ref, sc_ref, sh_ref, wr_ref, u_ref, meta_ref, cnt_ref, carry_ref, *, n_exp):
    tm = x_ref.shape[0]

    @pl.when((pl.program_id(0) == 0) & (pl.program_id(1) == 0))
    def _():
        carry_ref[...] = jnp.zeros_like(carry_ref)

    u = x_ref[...] * (1.0 + sc_ref[...]) + sh_ref[...]
    u_ref[...] = u.astype(BF16)
    u_hi = u.astype(BF16)
    u_lo = (u - u_hi.astype(F32)).astype(BF16)
    logits = (jnp.dot(u_hi, wr_ref[0], preferred_element_type=F32)
              + jnp.dot(u_lo, wr_ref[0], preferred_element_type=F32)
              + jnp.dot(u_hi, wr_ref[1], preferred_element_type=F32))
    lane = lax.broadcasted_iota(I32, (tm, LANES), 1)
    neg = -jnp.inf
    l1 = jnp.where(lane < n_exp, logits, neg)
    m1 = jnp.max(l1, axis=1, keepdims=True)
    i1 = jnp.min(jnp.where(l1 == m1, lane, LANES), axis=1, keepdims=True)
    l2 = jnp.where(lane == i1, neg, l1)
    m2 = jnp.max(l2, axis=1, keepdims=True)
    i2 = jnp.min(jnp.where(l2 == m2, lane, LANES), axis=1, keepdims=True)
    e = jnp.exp(m2 - m1)
    w1 = 1.0 / (1.0 + e)
    w2 = e * w1
    sel = jnp.where((lane == i1) | (lane == i2), 1.0, 0.0)
    before = (lax.broadcasted_iota(I32, (tm, tm), 1) < lax.broadcasted_iota(I32, (tm, tm), 0))
    ranks = jnp.dot(jnp.where(before, 1.0, 0.0).astype(BF16), sel.astype(BF16),
                    preferred_element_type=F32) + carry_ref[...]
    r1 = jnp.sum(jnp.where(lane == i1, ranks, 0.0), axis=1, keepdims=True)
    r2 = jnp.sum(jnp.where(lane == i2, ranks, 0.0), axis=1, keepdims=True)
    carry_ref[...] = carry_ref[...] + jnp.sum(sel, axis=0, keepdims=True)
    cnt_ref[...] = carry_ref[...]
    meta = jnp.zeros((tm, LANES), F32)
    for k, val in ((META_E0, i1.astype(F32)), (META_E1, i2.astype(F32)), (META_W0, w1),
                   (META_W1, w2), (META_R0, r1), (META_R1, r2)):
        meta = jnp.where(lane == k, val, meta)
    meta_ref[...] = meta


def _router(x, scale, shift, w_router_lanes, *, n_exp, tm):
    B, S, D = x.shape
    nt = S // tm
    row = pl.BlockSpec((None, tm, D), lambda b, i: (b, i, 0))
    mod = pl.BlockSpec((None, 1, D), lambda b, i: (b, 0, 0))
    return pl.pallas_call(
        functools.partial(_router_kernel, n_exp=n_exp),
        grid=(B, nt),
        in_specs=[row, mod, mod, pl.BlockSpec((2, D, LANES), lambda b, i: (0, 0, 0))],
        out_specs=[row,
                   pl.BlockSpec((None, tm, LANES), lambda b, i: (b, i, 0)),
                   pl.BlockSpec((1, LANES), lambda b, i: (0, 0))],
        out_shape=[jax.ShapeDtypeStruct((B, S, D), BF16),
                   jax.ShapeDtypeStruct((B, S, LANES), F32),
                   jax.ShapeDtypeStruct((1, LANES), F32)],
        scratch_shapes=[pltpu.VMEM((1, LANES), F32)],
        compiler_params=_params(("arbitrary", "arbitrary")),
        name="moe_router",
    )(x, scale, shift, w_router_lanes)


def _rows_at(hbm, first_row, n_rows):
    return hbm.at[pl.ds(pl.multiple_of(first_row, SEGMENT_ALIGN), n_rows)]


def _dispatch_kernel(base_ref, cnt_ref, padlo_ref, padn_ref, u_ref, meta_ref, xs_hbm,
                     stage, extra, zeros_buf, sems, extra_sem, *, n_exp):
    t = pl.program_id(0)
    last = pl.num_programs(0) - 1
    tm = u_ref.shape[0]
    R, H = DISPATCH_ROWS, DISPATCH_HALF
    slot = t % 2
    u = u_ref[...]
    e0, e1 = meta_ref[0:1, :], meta_ref[1:2, :]
    d0, d1 = meta_ref[2:3, :], meta_ref[3:4, :]
    srow = lax.broadcasted_iota(I32, (R, tm), 0)

    def main_copies(step, buf, fn):
        for e in range(n_exp):
            base = base_ref[step * n_exp + e]
            fn(pltpu.make_async_copy(stage.at[buf, pl.ds(e * R, H)], _rows_at(xs_hbm, base, H),
                                     sems.at[buf]))

            @pl.when(cnt_ref[step * n_exp + e] > H)
            def _(e=e, base=base):
                fn(pltpu.make_async_copy(stage.at[buf, pl.ds(e * R + H, H)],
                                         _rows_at(xs_hbm, base + H, H), sems.at[buf]))

    rels = []
    for e in range(n_exp):
        rel = jnp.where(e0 == e, d0, jnp.where(e1 == e, d1, -1))
        rels.append(jnp.where(rel >= 0, rel - base_ref[t * n_exp + e], -1))
    onehot = jnp.concatenate([jnp.where(srow == rel, 1.0, 0.0) for rel in rels], axis=0).astype(BF16)
    stage[slot] = jnp.dot(onehot, u, preferred_element_type=F32).astype(BF16)

    @pl.when(t > 0)
    def _():
        main_copies(t - 1, 1 - slot, lambda cp: cp.wait())
    main_copies(t, slot, lambda cp: cp.start())

    for e in range(n_exp):
        for blk in range(1, tm // R):
            @pl.when(cnt_ref[t * n_exp + e] > blk * R)
            def _(e=e, blk=blk):
                more = jnp.where(srow + blk * R == rels[e], 1.0, 0.0).astype(BF16)
                extra[...] = jnp.dot(more, u, preferred_element_type=F32).astype(BF16)
                cp = pltpu.make_async_copy(extra, _rows_at(xs_hbm, base_ref[t * n_exp + e] + blk * R, R),
                                           extra_sem)
                cp.start()
                cp.wait()

    @pl.when(t == last)
    def _():
        main_copies(t, slot, lambda cp: cp.wait())
        zeros_buf[...] = jnp.zeros_like(zeros_buf)
        pieces = [SEGMENT_ALIGN << b for b in range((R // SEGMENT_ALIGN).bit_length() - 1)]

        def pad_copies(fn):
            for p in range(padlo_ref.shape[0]):
                lo, n = padlo_ref[p], padn_ref[p]
                whole = n // R

                def body(k, carry, lo=lo):
                    fn(pltpu.make_async_copy(zeros_buf, _rows_at(xs_hbm, lo + k * R, R), extra_sem))
                    return carry
                lax.fori_loop(0, whole, body, 0)
                rem = n - whole * R
                for sz in pieces:
                    off = lo + whole * R + (rem // (2 * sz)) * (2 * sz)
                    pl.when((rem // sz) % 2 == 1)(functools.partial(
                        lambda off, sz: fn(pltpu.make_async_copy(
                            zeros_buf.at[pl.ds(0, sz)], _rows_at(xs_hbm, off, sz), extra_sem)), off, sz))

        pad_copies(lambda cp: cp.start())
        pad_copies(lambda cp: cp.wait())


def _dispatch(u, meta_t, base_te, cnt_te, pad_lo, pad_n, *, n_rows, tm):
    T, D = u.shape
    n_exp = pad_lo.shape[0] - 1
    assert tm % DISPATCH_ROWS == 0
    grid_spec = pltpu.PrefetchScalarGridSpec(
        num_scalar_prefetch=4,
        grid=(T // tm,),
        in_specs=[pl.BlockSpec((tm, D), lambda t, *_: (t, 0)),
                  pl.BlockSpec((meta_t.shape[0], tm), lambda t, *_: (0, t))],
        out_specs=pl.BlockSpec(memory_space=pl.ANY),
        scratch_shapes=[pltpu.VMEM((2, n_exp * DISPATCH_ROWS, D), BF16),
                        pltpu.VMEM((DISPATCH_ROWS, D), BF16),
                        pltpu.VMEM((DISPATCH_ROWS, D), BF16),
                        pltpu.SemaphoreType.DMA((2,)),
                        pltpu.SemaphoreType.DMA(())],
    )
    return pl.pallas_call(
        functools.partial(_dispatch_kernel, n_exp=n_exp),
        grid_spec=grid_spec,
        out_shape=jax.ShapeDtypeStruct((n_rows, D), BF16),
        compiler_params=_params(("arbitrary",)),
        name="moe_dispatch",
    )(base_te, cnt_te, pad_lo, pad_n, u, meta_t)


TILE_FULL, TILE_HALF, TILE_EMPTY = 0, 1, 2


def _expert_kernel(te_ref, na_ref, fill_ref, xs_ref, wg_ref, wu_ref, wd_ref, o_ref, acc_ref):
    i = pl.program_id(0)
    j = pl.program_id(1)
    last = pl.num_programs(1) - 1
    active = i < na_ref[0]
    fill = fill_ref[i]
    tm = acc_ref.shape[0]

    @pl.when(active & (j == 0))
    def _():
        acc_ref[...] = jnp.zeros_like(acc_ref)

    def swiglu(rows):
        xb = xs_ref[rows, :]
        g = jnp.dot(xb, wg_ref[...].astype(BF16), preferred_element_type=F32)
        up = jnp.dot(xb, wu_ref[...].astype(BF16), preferred_element_type=F32)
        h = (g * _sigmoid(g) * up).astype(BF16)
        acc_ref[rows, :] += jnp.dot(h, wd_ref[...].astype(BF16), preferred_element_type=F32)

    pl.when(active & (fill == TILE_FULL))(functools.partial(swiglu, pl.ds(0, tm)))
    pl.when(active & (fill == TILE_HALF))(functools.partial(swiglu, pl.ds(0, tm // 2)))

    @pl.when(active & (j == last))
    def _():
        o_ref[...] = acc_ref[...].astype(BF16)

    @pl.when(jnp.logical_not(active) & (j == last))
    def _():
        o_ref[...] = jnp.zeros_like(o_ref)


def _expert_ffn(xs, tile_expert, n_active, tile_fill, wg, wu, wd, *, tm, tf):
    n_rows, D = xs.shape
    ff = wg.shape[2]
    assert n_rows % tm == 0 and ff % tf == 0
    nj = ff // tf

    def jj(i, j, na):
        return jnp.where(i < na[0], j, nj - 1)

    def ii(i, na):
        return jnp.minimum(i, na[0] - 1)

    grid_spec = pltpu.PrefetchScalarGridSpec(
        num_scalar_prefetch=3,
        grid=(n_rows // tm, nj),
        in_specs=[pl.BlockSpec((tm, D), lambda i, j, te, na, tf_: (ii(i, na), 0)),
                  pl.BlockSpec((None, D, tf), lambda i, j, te, na, tf_: (te[i], 0, jj(i, j, na))),
                  pl.BlockSpec((None, D, tf), lambda i, j, te, na, tf_: (te[i], 0, jj(i, j, na))),
                  pl.BlockSpec((None, tf, D), lambda i, j, te, na, tf_: (te[i], jj(i, j, na), 0))],
        out_specs=pl.BlockSpec((tm, D), lambda i, j, te, na, tf_: (i, 0)),
        scratch_shapes=[pltpu.VMEM((tm, D), F32)],
    )
    return pl.pallas_call(
        _expert_kernel,
        grid_spec=grid_spec,
        out_shape=jax.ShapeDtypeStruct((n_rows, D), BF16),
        compiler_params=_params(("arbitrary", "arbitrary")),
        name="moe_experts",
    )(tile_expert, n_active, tile_fill, xs, wg, wu, wd)


def _combine_kernel(base_ref, off_ref, cnt_ref, ys_hbm, meta_ref, x_ref, gate_ref, lg_ref, lb_ref, o_ref,
                    stage, extra, acc_ref, sems, extra_sem, *, alpha, n_exp):
    t = pl.program_id(0) * pl.num_programs(1) + pl.program_id(1)
    n_steps = pl.num_programs(0) * pl.num_programs(1)
    tm = x_ref.shape[0]
    R = DISPATCH_ROWS
    slot = t % 2

    def main_copies(step, buf, fn):
        for e in range(n_exp):
            fn(pltpu.make_async_copy(_rows_at(ys_hbm, base_ref[step * n_exp + e], R),
                                     stage.at[buf, pl.ds(e * R, R)], sems.at[buf]))

    @pl.when(t == 0)
    def _():
        main_copies(0, 0, lambda cp: cp.start())

    @pl.when(t + 1 < n_steps)
    def _():
        main_copies(t + 1, 1 - slot, lambda cp: cp.start())

    meta = meta_ref[...]
    col = lambda k: meta[:, k:k + 1]
    e0, e1 = col(META_E0).astype(I32), col(META_E1).astype(I32)
    r0, r1 = col(META_R0).astype(I32), col(META_R1).astype(I32)
    w0, w1 = col(META_W0), col(META_W1)
    lane = lax.broadcasted_iota(I32, (tm, R), 1)

    rels, ws, picks = [], [], []
    for e in range(n_exp):
        rel = jnp.where(e0 == e, r0, jnp.where(e1 == e, r1, -1))
        rels.append(jnp.where(rel >= 0, rel - off_ref[t * n_exp + e], -1))
        ws.append(jnp.where(e0 == e, w0, jnp.where(e1 == e, w1, 0.0)))
        picks.append(jnp.where(lane == rels[e], ws[e], 0.0))
    weighted = jnp.concatenate(picks, axis=1).astype(BF16)
    main_copies(t, slot, lambda cp: cp.wait())
    acc_ref[...] = jnp.dot(weighted, stage[slot], preferred_element_type=F32)

    for e in range(n_exp):
        for blk in range(1, tm // R):
            @pl.when(cnt_ref[t * n_exp + e] > blk * R)
            def _(e=e, blk=blk):
                cp = pltpu.make_async_copy(_rows_at(ys_hbm, base_ref[t * n_exp + e] + blk * R, R),
                                           extra, extra_sem)
                cp.start()
                cp.wait()
                more = jnp.where(lane + blk * R == rels[e], ws[e], 0.0).astype(BF16)
                acc_ref[...] += jnp.dot(more, extra[...], preferred_element_type=F32)

    z = alpha * x_ref[...] + (1.0 + gate_ref[...]) * acc_ref[...]
    o_ref[...] = _layer_norm(z, lg_ref[...], lb_ref[...])


def _combine(ys, base_te, off_te, cnt_te, meta, x, gate, ln_g, ln_b, *, alpha, tm):
    B, S, D = x.shape
    nt = S // tm
    n_exp = base_te.shape[0] // (B * nt)
    row = pl.BlockSpec((None, tm, D), lambda b, i, *_: (b, i, 0))
    vec = pl.BlockSpec((1, D), lambda b, i, *_: (0, 0))
    grid_spec = pltpu.PrefetchScalarGridSpec(
        num_scalar_prefetch=3,
        grid=(B, nt),
        in_specs=[pl.BlockSpec(memory_space=pl.ANY),
                  pl.BlockSpec((None, tm, LANES), lambda b, i, *_: (b, i, 0)),
                  row, pl.BlockSpec((None, 1, D), lambda b, i, *_: (b, 0, 0)), vec, vec],
        out_specs=row,
        scratch_shapes=[pltpu.VMEM((2, n_exp * DISPATCH_ROWS, D), BF16),
                        pltpu.VMEM((DISPATCH_ROWS, D), BF16),
                        pltpu.VMEM((tm, D), F32),
                        pltpu.SemaphoreType.DMA((2,)),
                        pltpu.SemaphoreType.DMA(())],
    )
    return pl.pallas_call(
        functools.partial(_combine_kernel, alpha=alpha, n_exp=n_exp),
        grid_spec=grid_spec,
        out_shape=jax.ShapeDtypeStruct((B, S, D), F32),
        compiler_params=_params(("arbitrary", "arbitrary")),
        name="moe_combine",
    )(base_te, off_te, cnt_te, ys, meta, x, gate, ln_g, ln_b)


def _moe_ffn(x, scale, shift, gate, w_router, wg, wu, wd, ln_g, ln_b, *, alpha, tm, tm_e, tf):
    B, S, D = x.shape
    T = B * S
    nt = T // tm
    n_exp = w_router.shape[1]
    wr = jnp.zeros((D, LANES), F32).at[:, :n_exp].set(w_router)
    wr_hi = wr.astype(BF16)
    wr = jnp.stack([wr_hi, (wr - wr_hi.astype(F32)).astype(BF16)])
    u, meta, counts = _router(x, scale, shift, wr, n_exp=n_exp, tm=tm)

    meta2 = meta.reshape(T, LANES)
    e0 = meta2[:, META_E0].astype(I32)
    e1 = meta2[:, META_E1].astype(I32)
    r0 = meta2[:, META_R0].astype(I32)
    r1 = meta2[:, META_R1].astype(I32)
    experts = jnp.arange(n_exp, dtype=I32)
    chosen = ((e0[:, None] == experts) | (e1[:, None] == experts)).astype(I32)
    cnt_te = chosen.reshape(nt, tm, n_exp).sum(axis=1)
    seg_te = -(-cnt_te // SEGMENT_ALIGN) * SEGMENT_ALIGN
    off_te = jnp.cumsum(cnt_te, axis=0) - cnt_te
    filled = seg_te.sum(axis=0)
    sizes = -(-(filled + DISPATCH_SLACK) // tm_e) * tm_e
    ends = jnp.cumsum(sizes)
    starts = ends - sizes
    base_te = starts[None, :] + jnp.cumsum(seg_te, axis=0) - seg_te
    shift_te = (base_te - off_te)[:, None, :]

    def rows_of(e, r):
        pick = (e[:, None] == experts).astype(I32).reshape(nt, tm, n_exp)
        return r + (pick * shift_te).sum(axis=-1).reshape(T)

    d0, d1 = rows_of(e0, r0), rows_of(e1, r1)
    bound = TOP_K * T + n_exp * ((SEGMENT_ALIGN - 1) * nt + DISPATCH_SLACK)
    n_tiles = -(-bound // tm_e) + n_exp + 1
    n_active = (ends[-1] // tm_e).astype(I32)
    tile_id = jnp.minimum(jnp.arange(n_tiles, dtype=I32), n_active - 1)
    tile_expert = jnp.sum((ends[None, :] <= (tile_id * tm_e)[:, None]).astype(I32), axis=1)
    data_end = jnp.sum((tile_expert[:, None] == experts) * (starts + filled)[None, :], axis=1)
    left = data_end - tile_id * tm_e
    tile_fill = jnp.where(left <= 0, TILE_EMPTY, jnp.where(left <= tm_e // 2, TILE_HALF, TILE_FULL)).astype(I32)
    meta_t = jnp.stack([e0, e1, d0, d1] + [jnp.zeros_like(e0)] * 4)

    xs = _dispatch(u.reshape(T, D), meta_t, base_te.reshape(-1), cnt_te.reshape(-1),
                   jnp.concatenate([starts + filled, ends[-1:]]),
                   jnp.concatenate([sizes - filled, n_tiles * tm_e - ends[-1:]]),
                   n_rows=n_tiles * tm_e, tm=tm)
    ys = _expert_ffn(xs, tile_expert, n_active.reshape(1), tile_fill, wg, wu, wd, tm=tm_e, tf=tf)
    return _combine(ys, base_te.reshape(-1), off_te.reshape(-1), cnt_te.reshape(-1),
                    meta, x, gate, ln_g, ln_b, alpha=alpha, tm=tm)


def kernel(x, c, positions, w_in, w_out, attn_norm_gain, rec_norm_gain, rec_lb_logits, ada_w, ada_b,
           ln_gain, ln_bias, ffn_w_gate, ffn_w_up, ffn_w_down, moe_router, moe_w_gate, moe_w_up,
           moe_w_down):
    B, S, D = x.shape
    depth = w_in.shape[0]
    alpha = (2 * depth) ** 0.25

    p = jax.nn.softmax(rec_lb_logits.astype(F32), axis=0)
    cum = jnp.cumsum(p, axis=0)
    lb_all = cum - cum[0:1]
    half = ATTN_HEAD_DIM // 2
    inv_freq = ROPE_THETA ** (-jnp.arange(half, dtype=F32) / half)
    inv_freq_lanes = jnp.tile(inv_freq, LANES // half).reshape(1, LANES)
    pos_f = positions.astype(F32).reshape(B, S, 1)
    n_qk = 2 * ATTN_WIDTH
    w_in_b = jnp.concatenate([_qk_lane_order(w_in[..., :n_qk]), w_in[..., n_qk:]], axis=-1).astype(BF16)
    w_out_b = w_out.astype(BF16)
    ff = ffn_w_gate.shape[2]
    ffp = -(-ff // (2 * LANES)) * (2 * LANES)
    pad_c = lambda w: jnp.pad(w.astype(BF16), ((0, 0), (0, 0), (0, ffp - ff)))
    ffn_g, ffn_u = pad_c(ffn_w_gate), pad_c(ffn_w_up)
    ffn_d = jnp.pad(ffn_w_down.astype(BF16), ((0, 0), (0, ffp - ff), (0, 0)))

    mods = _ada_modulation(c, ada_w, ada_b)
    rope_cos, rope_sin = _rope_tables(pos_f, inv_freq_lanes, tm=min(1024, S))

    def mod(layer, sub):
        m = mods[layer * 2 + sub].reshape(B, 1, 3 * D)
        return m[..., :D], m[..., D:2 * D], m[..., 2 * D:]

    vec = lambda a: a.reshape(1, -1)
    for layer in range(depth):
        shift, scale, gate = mod(layer, 0)
        proj = _in_projection(x, scale, shift, rope_cos, rope_sin, w_in_b[layer], tm=min(1024, S))
        attn = _dilated_attention(proj)
        rec = _hgrn2(proj, lb_all[layer], vec(rec_norm_gain[layer]))
        x = _out_projection(attn, rec, x, gate, vec(attn_norm_gain[layer]), w_out_b[layer],
                            vec(ln_gain[layer, 0]), vec(ln_bias[layer, 0]), alpha=alpha, tm=min(512, S))
        shift, scale, gate = mod(layer, 1)
        j = layer // 2
        if layer % 2 == 0:
            x = _dense_ffn(x, scale, shift, gate, ffn_g[j], ffn_u[j], ffn_d[j],
                           vec(ln_gain[layer, 1]), vec(ln_bias[layer, 1]),
                           alpha=alpha, tm=min(512, S), tc=2 * LANES)
        else:
            x = _moe_ffn(x, scale, shift, gate, moe_router[j], moe_w_gate[j], moe_w_up[j],
                         moe_w_down[j], vec(ln_gain[layer, 1]), vec(ln_bias[layer, 1]),
                         alpha=alpha, tm=min(512, S), tm_e=min(1024, B * S), tf=512)
    return x
```

```python
import functools

import jax
import jax.numpy as jnp
from jax import lax
from jax.experimental import pallas as pl
from jax.experimental.pallas import tpu as pltpu

F32 = jnp.float32
BF16 = jnp.bfloat16
I32 = jnp.int32

LANES = 128
V7X_VMEM_LIMIT_BYTES = 56 * 1024 * 1024

ATTN_WIDTH = 512
ATTN_HEAD_DIM = 64
DILATIONS = (1, 4, 16)
ATTN_RADIUS = 64
ROPE_THETA = 10000.0
REC_WIDTH = 512
REC_DIM = 128
REC_CHUNK = 64
TOP_K = 2
LN_EPS = 1e-5
RMS_EPS = 1e-6
MASK_VALUE = -1e30
DECAY_EXP_CLAMP = 80.0
LOG2_E = 1.4426950408889634


def _params(semantics):
    return pltpu.CompilerParams(dimension_semantics=semantics,
                                vmem_limit_bytes=V7X_VMEM_LIMIT_BYTES)


def _sigmoid(x):
    return 1.0 / (1.0 + jnp.exp(-x))


def _layer_norm(z, gain, bias):
    mu = jnp.mean(z, axis=-1, keepdims=True)
    zc = z - mu
    var = jnp.mean(zc * zc, axis=-1, keepdims=True)
    return zc * lax.rsqrt(var + LN_EPS) * gain + bias


def _ada_kernel(c_ref, w_ref, b_ref, o_ref):
    c = c_ref[...]
    o_ref[...] = jnp.dot(c * _sigmoid(c), w_ref[...], precision=lax.Precision.HIGHEST,
                         preferred_element_type=F32) + b_ref[...]


def _ada_modulation(c, ada_w, ada_b):
    B, D = c.shape
    n = ada_w.shape[0] * ada_w.shape[1]
    n3 = ada_w.shape[-1]
    tn = 1024
    return pl.pallas_call(
        _ada_kernel,
        grid=(n, n3 // tn),
        in_specs=[pl.BlockSpec((B, D), lambda i, j: (0, 0)),
                  pl.BlockSpec((None, D, tn), lambda i, j: (i, 0, j)),
                  pl.BlockSpec((None, 1, tn), lambda i, j: (i, 0, j))],
        out_specs=pl.BlockSpec((None, B, tn), lambda i, j: (i, 0, j)),
        out_shape=jax.ShapeDtypeStruct((n, B, n3), F32),
        compiler_params=_params(("arbitrary", "arbitrary")),
        name="ada_modulation",
    )(c, ada_w.reshape(n, D, n3), ada_b.reshape(n, 1, n3))


ROPE_HALF = ATTN_HEAD_DIM // 2


def _qk_lane_order(w_cols):
    lead = w_cols.shape[:-1]
    w = w_cols.reshape(*lead, -1, 2, 2, ROPE_HALF)
    return jnp.swapaxes(w, -3, -2).reshape(*lead, -1)


def _rope_kernel(pos_ref, invf_ref, cos_ref, sin_ref):
    ang = pos_ref[...] * invf_ref[...]
    lane = lax.broadcasted_iota(I32, (1, LANES), 1)
    sin = jnp.sin(ang)
    cos_ref[...] = jnp.cos(ang)
    sin_ref[...] = jnp.where(lane < LANES // 2, -sin, sin)


def _rope_tables(pos_f, inv_freq_lanes, *, tm):
    B, S, _ = pos_f.shape
    tab = pl.BlockSpec((None, tm, LANES), lambda b, i: (b, i, 0))
    return pl.pallas_call(
        _rope_kernel,
        grid=(B, S // tm),
        in_specs=[pl.BlockSpec((None, tm, 1), lambda b, i: (b, i, 0)),
                  pl.BlockSpec((1, LANES), lambda b, i: (0, 0))],
        out_specs=[tab, tab],
        out_shape=[jax.ShapeDtypeStruct((B, S, LANES), F32)] * 2,
        compiler_params=_params(("arbitrary", "arbitrary")),
        name="rope_tables",
    )(pos_f, inv_freq_lanes)


def _inproj_kernel(x_ref, sc_ref, sh_ref, cos_ref, sin_ref, w_ref, o_ref, *, tn):
    u = (x_ref[...] * (1.0 + sc_ref[...]) + sh_ref[...]).astype(BF16)
    cos = cos_ref[...]
    sin_signed = sin_ref[...]
    q_scale = ATTN_HEAD_DIM ** -0.5
    for j in range(w_ref.shape[1] // tn):
        acc = jnp.dot(u, w_ref[:, j * tn:(j + 1) * tn], preferred_element_type=F32)
        if j > 0:
            o_ref[:, j * tn:(j + 1) * tn] = acc
            continue
        for kk in range(tn // LANES):
            c = acc[:, kk * LANES:(kk + 1) * LANES]
            r = c * cos + pltpu.roll(c, LANES // 2, 1) * sin_signed
            if kk * LANES < ATTN_WIDTH:
                r = r * q_scale
            o_ref[:, kk * LANES:(kk + 1) * LANES] = r


def _in_projection(x, scale, shift, rope_cos, rope_sin, w_in_bf16, *, tm):
    B, S, D = x.shape
    n_cols = w_in_bf16.shape[1]
    tn = 2 * ATTN_WIDTH
    assert S % tm == 0 and n_cols % tn == 0
    return pl.pallas_call(
        functools.partial(_inproj_kernel, tn=tn),
        grid=(B, S // tm),
        in_specs=[pl.BlockSpec((None, tm, D), lambda b, i: (b, i, 0)),
                  pl.BlockSpec((None, 1, D), lambda b, i: (b, 0, 0)),
                  pl.BlockSpec((None, 1, D), lambda b, i: (b, 0, 0)),
                  pl.BlockSpec((None, tm, LANES), lambda b, i: (b, i, 0)),
                  pl.BlockSpec((None, tm, LANES), lambda b, i: (b, i, 0)),
                  pl.BlockSpec((D, n_cols), lambda b, i: (0, 0))],
        out_specs=pl.BlockSpec((None, tm, n_cols), lambda b, i: (b, i, 0)),
        out_shape=jax.ShapeDtypeStruct((B, S, n_cols), F32),
        compiler_params=_params(("arbitrary", "arbitrary")),
        name="in_projection",
    )(x, scale, shift, rope_cos, rope_sin, w_in_bf16)


ATTN_BLOCK_GROUP = 16


def _attn_kernel(q_ref, k_ref, v_ref, o_ref, qs, ks, vs, qf, kf, vf, pm, plr, pa, nm, nl, na, bias_s,
                 *, S):
    lane = lax.broadcasted_iota(I32, (1, LANES), 1)
    head0 = lane < ATTN_HEAD_DIM
    head0_qk = (lane % ATTN_HEAD_DIM) < ROPE_HALF

    for p, d in enumerate(DILATIONS):
        L = S // d
        tq = min(128, L)
        W = min(2 * tq, L)
        nb = L // tq

        d_prev = DILATIONS[p - 1] if p else 1
        ratio, l_prev = d // d_prev, S // d_prev
        keep_f32 = 0 < p < len(DILATIONS) - 1
        for src, stage, dst in ((q_ref, qf, qs), (k_ref, kf, ks), (v_ref, vf, vs)):
            src = src if p <= 1 else stage
            for r in range(d):
                a, r_prev = divmod(r, d_prev)
                rows = pl.ds(r_prev * l_prev + a, L, stride=ratio) if ratio > 1 else pl.ds(0, L)
                val = src[rows, :]
                if keep_f32:
                    stage[r * L:(r + 1) * L, :] = val
                dst[r * L:(r + 1) * L, :] = val.astype(BF16)

        n_blocks = d * nb
        rc = lax.broadcasted_iota(I32, (tq, W), 0) - lax.broadcasted_iota(I32, (tq, W), 1)
        for which, delta in enumerate((0, ATTN_RADIUS, tq)):
            bias_s[which, pl.ds(0, tq), pl.ds(0, W)] = jnp.where(
                jnp.abs(rc + delta) <= ATTN_RADIUS, 0.0, MASK_VALUE)
        group = min(ATTN_BLOCK_GROUP, n_blocks)
        assert n_blocks % group == 0

        def block_group(gi, carry, L=L, tq=tq, W=W, nb=nb, group=group):
            q0s, k0s, scores, probs, outs, sums = [], [], [], [], [], []
            for u in range(group):
                g = gi * group + u
                r = g // nb
                n = g - r * nb
                base = r * L
                q0 = pl.multiple_of(base + n * tq, tq)
                ws = jnp.clip(n * tq - ATTN_RADIUS, 0, L - W)
                k0 = pl.multiple_of(base + ws, 16)
                qb = qs[pl.ds(q0, tq), :]
                zero = jnp.zeros_like(qb)
                q2 = jnp.concatenate([jnp.where(head0_qk, qb, zero), jnp.where(head0_qk, zero, qb)], axis=0)
                s = lax.dot_general(q2, ks[pl.ds(k0, W), :], (((1,), (1,)), ((), ())),
                                    preferred_element_type=F32)
                delta = n * tq - ws
                which = jnp.where(delta == 0, 0, jnp.where(delta == ATTN_RADIUS, 1, 2))
                bias = bias_s[which, pl.ds(0, tq), pl.ds(0, W)]
                scores.append(s + jnp.concatenate([bias, bias], axis=0))
                q0s.append(q0)
                k0s.append(k0)
            for u in range(group):
                s = scores[u]
                m = jnp.max(s, axis=1, keepdims=True)
                e = jnp.exp(s - m)
                probs.append((m, e.astype(BF16)))
                sums.append(jnp.sum(e, axis=1, keepdims=True))
            for u in range(group):
                outs.append(jnp.dot(probs[u][1], vs[pl.ds(k0s[u], W), :], preferred_element_type=F32))
            for u in range(group):
                m, l = probs[u][0], sums[u]
                rows = pl.ds(q0s[u], tq)
                out_m[rows, :] = jnp.where(head0, m[:tq], m[tq:])
                out_l[rows, :] = jnp.where(head0, l[:tq], l[tq:])
                out_a[rows, :] = jnp.where(head0, outs[u][:tq], outs[u][tq:])
            return carry

        out_m, out_l, out_a = (nm.at[p], nl.at[p], na.at[p]) if d == 1 else (pm, plr, pa)
        lax.fori_loop(0, n_blocks // group, block_group, 0)

        for r in range(d if d > 1 else 0):
            rows = pl.ds(r, L, stride=d)
            nm[p, rows, :] = pm[r * L:(r + 1) * L, :]
            nl[p, rows, :] = plr[r * L:(r + 1) * L, :]
            na[p, rows, :] = pa[r * L:(r + 1) * L, :]

    m_all = jnp.maximum(jnp.maximum(nm[0], nm[1]), nm[2])
    num = jnp.zeros((S, LANES), F32)
    den = jnp.zeros((S, LANES), F32)
    for p in range(len(DILATIONS)):
        w = jnp.exp(nm[p] - m_all)
        num = num + w * na[p]
        den = den + w * nl[p]
    o_ref[...] = (num / den).astype(o_ref.dtype)


def _dilated_attention(proj):
    B, S, _ = proj.shape
    n_pairs = ATTN_WIDTH // LANES
    assert S % (16 * DILATIONS[-1]) == 0
    blk = lambda off: pl.BlockSpec((None, S, LANES), lambda b, h: (b, 0, off + h))
    return pl.pallas_call(
        functools.partial(_attn_kernel, S=S),
        grid=(B, n_pairs),
        in_specs=[blk(0), blk(n_pairs), blk(2 * n_pairs)],
        out_specs=pl.BlockSpec((None, S, LANES), lambda b, h: (b, 0, h)),
        out_shape=jax.ShapeDtypeStruct((B, S, ATTN_WIDTH), BF16),
        scratch_shapes=[pltpu.VMEM((S, LANES), BF16)] * 3
                       + [pltpu.VMEM((S, LANES), F32)] * 6
                       + [pltpu.VMEM((len(DILATIONS), S, LANES), F32)] * 3
                       + [pltpu.VMEM((3, min(128, S), min(256, S)), F32)],
        compiler_params=_params(("arbitrary", "arbitrary")),
        name="dilated_attention",
    )(proj, proj, proj)


REC_GROUP_CHUNKS = 4
REC_GROUP_ROWS = REC_GROUP_CHUNKS * REC_CHUNK
REC_STAGE_GROUPS = 4


def _hgrn_kernel(rq_ref, zf_ref, zb_ref, ri_ref, rg_ref, lb_ref, gain_ref, o_ref,
                 q_s, vt_s, oi_s, qp_s, ut_s, dd_s, st_s, os_s, *, S):
    C, G, GR = REC_CHUNK, REC_GROUP_CHUNKS, REC_GROUP_ROWS
    nc, ng = S // C, S // GR
    sg = min(REC_STAGE_GROUPS, ng)
    assert ng % sg == 0

    row = lax.broadcasted_iota(I32, (GR, GR), 0)
    col = lax.broadcasted_iota(I32, (GR, GR), 1)
    same_chunk = (row // C) == (col // C)
    t_i = lax.broadcasted_iota(I32, (C, C), 0)
    s_i = lax.broadcasted_iota(I32, (C, C), 1)

    def prep(i, carry):
        r0 = pl.multiple_of(i * GR, GR)
        rq = rq_ref[pl.ds(r0, GR), :]
        q_s[pl.ds(r0, GR), :] = rq * _sigmoid(rq)
        v = ri_ref[pl.ds(r0, GR), :]
        for c in range(G):
            vt_s[i * G + c] = v[c * C:(c + 1) * C, :].T.astype(BF16)
        return carry

    lax.fori_loop(0, ng, prep, 0)

    for direction in range(2):
        fwd = direction == 0
        z_ref = zf_ref if fwd else zb_ref
        cum = jnp.where(same_chunk & ((col <= row) if fwd else (col >= row)), 1.0, 0.0).astype(BF16)
        keep = (s_i <= t_i) if fwd else (s_i >= t_i)
        mid_row = C // 2 - 1 if fwd else C // 2
        last_row = C - 1 if fwd else 0

        def phase_a(it, carry, fwd=fwd, z_ref=z_ref, cum=cum, keep=keep,
                    mid_row=mid_row, last_row=last_row, direction=direction):
            lb = lb_ref[direction:direction + 1, :]
            log_lb = jnp.log(lb)
            log_1m_lb = jnp.log1p(-lb)
            r0s, c0s, k3s, bcs, atts, stage3 = [], [], [], [], [], []
            for u in range(sg):
                i = it * sg + u
                r0 = pl.multiple_of(i * GR, GR)
                z = z_ref[pl.ds(r0, GR), :]
                e = jnp.exp(-jnp.abs(z))
                t = log_1m_lb + (jnp.minimum(z, 0.0) - jnp.log(1.0 + e))
                g = jnp.maximum(log_lb, t) + jnp.log(1.0 + jnp.exp(-jnp.abs(log_lb - t)))
                kk = (1.0 - lb) * (jnp.where(z > 0, e, 1.0) / (1.0 + e))
                g = g * LOG2_E
                g1 = g.astype(BF16)
                r1 = g - g1.astype(F32)
                g2 = r1.astype(BF16)
                g3 = (r1 - g2.astype(F32)).astype(BF16)
                bcs.append(jnp.dot(cum, g1, preferred_element_type=F32)
                           + jnp.dot(cum, g2, preferred_element_type=F32)
                           + jnp.dot(cum, g3, preferred_element_type=F32))
                k3s.append(kk.reshape(G, C, REC_DIM))
                r0s.append(r0)
                c0s.append(pl.multiple_of(i * G, G))
            for u in range(sg):
                b3 = bcs[u].reshape(G, C, REC_DIM)
                mid = b3[:, mid_row:mid_row + 1, :]
                last = b3[:, last_row:last_row + 1, :]
                e3 = b3 - mid
                q3 = q_s[pl.ds(r0s[u], GR), :].reshape(G, C, REC_DIM)
                clamp = DECAY_EXP_CLAMP * LOG2_E
                qt = (q3 * jnp.exp2(jnp.minimum(e3, clamp))).astype(BF16)
                kt = (k3s[u] * jnp.exp2(jnp.minimum(-e3, clamp))).astype(BF16)
                atts.append(jnp.einsum('gtc,gsc->gts', qt, kt, preferred_element_type=F32))
                qp_s[pl.ds(r0s[u], GR), :] = (q3 * jnp.exp2(b3)).astype(BF16).reshape(GR, REC_DIM)
                stage3.append(((k3s[u] * jnp.exp2(last - b3)).astype(BF16), jnp.exp2(last)))
            for u in range(sg):
                att = jnp.where(keep[None], atts[u], 0.0).astype(BF16)
                v3 = ri_ref[pl.ds(r0s[u], GR), :].reshape(G, C, REC_DIM).astype(BF16)
                oi = jnp.einsum('gts,gsv->gtv', att, v3, preferred_element_type=F32)
                oi_s[pl.ds(r0s[u], GR), :] = oi.reshape(GR, REC_DIM)
                kp, decay = stage3[u]
                ut_s[pl.ds(c0s[u], G)] = jnp.einsum('gvs,gsc->gvc', vt_s[pl.ds(c0s[u], G)], kp,
                                                    preferred_element_type=F32)
                dd_s[pl.ds(c0s[u], G)] = decay
            return carry

        lax.fori_loop(0, ng // sg, phase_a, 0)

        def scan(c, st, fwd=fwd):
            idx = c if fwd else nc - 1 - c
            st_s[idx] = st.astype(BF16)
            return dd_s[idx] * st + ut_s[idx]

        lax.fori_loop(0, nc, scan, jnp.zeros((REC_DIM, REC_DIM), F32))

        def phase_c(it, carry, fwd=fwd):
            r0s = [pl.multiple_of((it * sg + u) * GR, GR) for u in range(sg)]
            oos = []
            for u in range(sg):
                c0 = pl.multiple_of((it * sg + u) * G, G)
                qp3 = qp_s[pl.ds(r0s[u], GR), :].reshape(G, C, REC_DIM)
                oos.append(jnp.einsum('gtc,gvc->gtv', qp3, st_s[pl.ds(c0, G)],
                                      preferred_element_type=F32))
            for u in range(sg):
                rows = pl.ds(r0s[u], GR)
                tot = oi_s[rows, :] + oos[u].reshape(GR, REC_DIM)
                os_s[rows, :] = tot if fwd else os_s[rows, :] + tot
            return carry

        lax.fori_loop(0, ng // sg, phase_c, 0)

    o = os_s[...]
    o = o * lax.rsqrt(jnp.mean(o * o, axis=-1, keepdims=True) + RMS_EPS) * gain_ref[...]
    o_ref[...] = (o * _sigmoid(rg_ref[...])).astype(o_ref.dtype)


def _hgrn2(proj, lb, rec_gain):
    B, S, _ = proj.shape
    nh = REC_WIDTH // REC_DIM
    assert S % REC_GROUP_ROWS == 0
    col0 = 3 * ATTN_WIDTH // LANES
    blk = lambda k: pl.BlockSpec((None, S, REC_DIM), lambda b, h: (b, 0, col0 + k * nh + h))
    nc = S // REC_CHUNK
    return pl.pallas_call(
        functools.partial(_hgrn_kernel, S=S),
        grid=(B, nh),
        in_specs=[blk(0), blk(1), blk(2), blk(3), blk(4),
                  pl.BlockSpec((2, REC_DIM), lambda b, h: (0, h)),
                  pl.BlockSpec((1, REC_DIM), lambda b, h: (0, h))],
        out_specs=pl.BlockSpec((None, S, REC_DIM), lambda b, h: (b, 0, h)),
        out_shape=jax.ShapeDtypeStruct((B, S, REC_WIDTH), BF16),
        scratch_shapes=[pltpu.VMEM((S, REC_DIM), F32),
                        pltpu.VMEM((nc, REC_DIM, REC_CHUNK), BF16),
                        pltpu.VMEM((S, REC_DIM), F32),
                        pltpu.VMEM((S, REC_DIM), BF16),
                        pltpu.VMEM((nc, REC_DIM, REC_DIM), F32),
                        pltpu.VMEM((nc, 1, REC_DIM), F32),
                        pltpu.VMEM((nc, REC_DIM, REC_DIM), BF16),
                        pltpu.VMEM((S, REC_DIM), F32)],
        compiler_params=_params(("arbitrary", "arbitrary")),
        name="hgrn2",
    )(proj, proj, proj, proj, proj, lb, rec_gain)


def _outproj_kernel(a_ref, r_ref, x_ref, gate_ref, ag_ref, w_ref, lg_ref, lb_ref, o_ref, *, alpha):
    a = a_ref[...].astype(F32)
    an = a * lax.rsqrt(jnp.mean(a * a, axis=-1, keepdims=True) + RMS_EPS) * ag_ref[...]
    y = (jnp.dot(an.astype(BF16), w_ref[:ATTN_WIDTH, :], preferred_element_type=F32)
         + jnp.dot(r_ref[...], w_ref[ATTN_WIDTH:, :], preferred_element_type=F32))
    z = alpha * x_ref[...] + (1.0 + gate_ref[...]) * y
    o_ref[...] = _layer_norm(z, lg_ref[...], lb_ref[...])


def _out_projection(attn, rec, x, gate, attn_gain, w_out_bf16, ln_g, ln_b, *, alpha, tm):
    B, S, D = x.shape
    row = lambda w: pl.BlockSpec((None, tm, w), lambda b, i: (b, i, 0))
    vec = lambda w: pl.BlockSpec((1, w), lambda b, i: (0, 0))
    return pl.pallas_call(
        functools.partial(_outproj_kernel, alpha=alpha),
        grid=(B, S // tm),
        in_specs=[row(ATTN_WIDTH), row(REC_WIDTH), row(D),
                  pl.BlockSpec((None, 1, D), lambda b, i: (b, 0, 0)),
                  vec(ATTN_WIDTH),
                  pl.BlockSpec((ATTN_WIDTH + REC_WIDTH, D), lambda b, i: (0, 0)),
                  vec(D), vec(D)],
        out_specs=row(D),
        out_shape=jax.ShapeDtypeStruct((B, S, D), F32),
        compiler_params=_params(("arbitrary", "arbitrary")),
        name="out_projection",
    )(attn, rec, x, gate, attn_gain, w_out_bf16, ln_g, ln_b)


def _ffn_kernel(x_ref, sc_ref, sh_ref, gate_ref, wg_ref, wu_ref, wd_ref, lg_ref, lb_ref, o_ref,
                *, alpha, tc):
    x = x_ref[...]
    u = (x * (1.0 + sc_ref[...]) + sh_ref[...]).astype(BF16)
    acc = jnp.zeros(x.shape, F32)
    for c in range(wg_ref.shape[1] // tc):
        cols = slice(c * tc, (c + 1) * tc)
        g = jnp.dot(u, wg_ref[:, cols], preferred_element_type=F32)
        up = jnp.dot(u, wu_ref[:, cols], preferred_element_type=F32)
        h = (g * _sigmoid(g) * up).astype(BF16)
        acc = acc + jnp.dot(h, wd_ref[cols, :], preferred_element_type=F32)
    z = alpha * x + (1.0 + gate_ref[...]) * acc
    o_ref[...] = _layer_norm(z, lg_ref[...], lb_ref[...])


def _dense_ffn(x, scale, shift, gate, wg, wu, wd, ln_g, ln_b, *, alpha, tm, tc):
    B, S, D = x.shape
    ffp = wg.shape[1]
    assert ffp % tc == 0
    row = pl.BlockSpec((None, tm, D), lambda b, i: (b, i, 0))
    mod = pl.BlockSpec((None, 1, D), lambda b, i: (b, 0, 0))
    vec = pl.BlockSpec((1, D), lambda b, i: (0, 0))
    return pl.pallas_call(
        functools.partial(_ffn_kernel, alpha=alpha, tc=tc),
        grid=(B, S // tm),
        in_specs=[row, mod, mod, mod,
                  pl.BlockSpec((D, ffp), lambda b, i: (0, 0)),
                  pl.BlockSpec((D, ffp), lambda b, i: (0, 0)),
                  pl.BlockSpec((ffp, D), lambda b, i: (0, 0)),
                  vec, vec],
        out_specs=row,
        out_shape=jax.ShapeDtypeStruct((B, S, D), F32),
        compiler_params=_params(("arbitrary", "arbitrary")),
        name="dense_ffn",
    )(x, scale, shift, gate, wg, wu, wd, ln_g, ln_b)


META_E0, META_E1, META_W0, META_W1, META_R0, META_R1 = range(6)
SEGMENT_ALIGN = 16
DISPATCH_ROWS = 256
DISPATCH_HALF = DISPATCH_ROWS // 2
DISPATCH_SLACK = DISPATCH_HALF


def _router_kernel(x_ref, sc_ref, sh_ref, wr_ref, u_ref, meta_ref, cnt_ref, carry_ref, *, n_exp):
    tm = x_ref.shape[0]

    @pl.when((pl.program_id(0) == 0) & (pl.program_id(1) == 0))
    def _():
        carry_ref[...] = jnp.zeros_like(carry_ref)

    u = x_ref[...] * (1.0 + sc_ref[...]) + sh_ref[...]
    u_ref[...] = u.astype(BF16)
    u_hi = u.astype(BF16)
    u_lo = (u - u_hi.astype(F32)).astype(BF16)
    logits = (jnp.dot(u_hi, wr_ref[0], preferred_element_type=F32)
              + jnp.dot(u_lo, wr_ref[0], preferred_element_type=F32)
              + jnp.dot(u_hi, wr_ref[1], preferred_element_type=F32))
    lane = lax.broadcasted_iota(I32, (tm, LANES), 1)
    neg = -jnp.inf
    l1 = jnp.where(lane < n_exp, logits, neg)
    m1 = jnp.max(l1, axis=1, keepdims=True)
    i1 = jnp.min(jnp.where(l1 == m1, lane, LANES), axis=1, keepdims=True)
    l2 = jnp.where(lane == i1, neg, l1)
    m2 = jnp.max(l2, axis=1, keepdims=True)
    i2 = jnp.min(jnp.where(l2 == m2, lane, LANES), axis=1, keepdims=True)
    e = jnp.exp(m2 - m1)
    w1 = 1.0 / (1.0 + e)
    w2 = e * w1
    sel = jnp.where((lane == i1) | (lane == i2), 1.0, 0.0)
    before = (lax.broadcasted_iota(I32, (tm, tm), 1) < lax.broadcasted_iota(I32, (tm, tm), 0))
    ranks = jnp.dot(jnp.where(before, 1.0, 0.0).astype(BF16), sel.astype(BF16),
                    preferred_element_type=F32) + carry_ref[...]
    r1 = jnp.sum(jnp.where(lane == i1, ranks, 0.0), axis=1, keepdims=True)
    r2 = jnp.sum(jnp.where(lane == i2, ranks, 0.0), axis=1, keepdims=True)
    carry_ref[...] = carry_ref[...] + jnp.sum(sel, axis=0, keepdims=True)
    cnt_ref[...] = carry_ref[...]
    meta = jnp.zeros((tm, LANES), F32)
    for k, val in ((META_E0, i1.astype(F32)), (META_E1, i2.astype(F32)), (META_W0, w1),
                   (META_W1, w2), (META_R0, r1), (META_R1, r2)):
        meta = jnp.where(lane == k, val, meta)
    meta_ref[...] = meta


def _router(x, scale, shift, w_router_lanes, *, n_exp, tm):
    B, S, D = x.shape
    nt = S // tm
    row = pl.BlockSpec((None, tm, D), lambda b, i: (b, i, 0))
    mod = pl.BlockSpec((None, 1, D), lambda b, i: (b, 0, 0))
    return pl.pallas_call(
        functools.partial(_router_kernel, n_exp=n_exp),
        grid=(B, nt),
        in_specs=[row, mod, mod, pl.BlockSpec((2, D, LANES), lambda b, i: (0, 0, 0))],
        out_specs=[row,
                   pl.BlockSpec((None, tm, LANES), lambda b, i: (b, i, 0)),
                   pl.BlockSpec((1, LANES), lambda b, i: (0, 0))],
        out_shape=[jax.ShapeDtypeStruct((B, S, D), BF16),
                   jax.ShapeDtypeStruct((B, S, LANES), F32),
                   jax.ShapeDtypeStruct((1, LANES), F32)],
        scratch_shapes=[pltpu.VMEM((1, LANES), F32)],
        compiler_params=_params(("arbitrary", "arbitrary")),
        name="moe_router",
    )(x, scale, shift, w_router_lanes)


def _rows_at(hbm, first_row, n_rows):
    return hbm.at[pl.ds(pl.multiple_of(first_row, SEGMENT_ALIGN), n_rows)]


def _dispatch_kernel(base_ref, cnt_ref, padlo_ref, padn_ref, u_ref, meta_ref, xs_hbm,
                     stage, extra, zeros_buf, sems, extra_sem, *, n_exp):
    t = pl.program_id(0)
    last = pl.num_programs(0) - 1
    tm = u_ref.shape[0]
    R, H = DISPATCH_ROWS, DISPATCH_HALF
    slot = t % 2
    u = u_ref[...]
    e0, e1 = meta_ref[0:1, :], meta_ref[1:2, :]
    d0, d1 = meta_ref[2:3, :], meta_ref[3:4, :]
    srow = lax.broadcasted_iota(I32, (R, tm), 0)

    def main_copies(step, buf, fn):
        for e in range(n_exp):
            base = base_ref[step * n_exp + e]
            fn(pltpu.make_async_copy(stage.at[buf, pl.ds(e * R, H)], _rows_at(xs_hbm, base, H),
                                     sems.at[buf]))

            @pl.when(cnt_ref[step * n_exp + e] > H)
            def _(e=e, base=base):
                fn(pltpu.make_async_copy(stage.at[buf, pl.ds(e * R + H, H)],
                                         _rows_at(xs_hbm, base + H, H), sems.at[buf]))

    rels = []
    for e in range(n_exp):
        rel = jnp.where(e0 == e, d0, jnp.where(e1 == e, d1, -1))
        rels.append(jnp.where(rel >= 0, rel - base_ref[t * n_exp + e], -1))
    onehot = jnp.concatenate([jnp.where(srow == rel, 1.0, 0.0) for rel in rels], axis=0).astype(BF16)
    stage[slot] = jnp.dot(onehot, u, preferred_element_type=F32).astype(BF16)

    @pl.when(t > 0)
    def _():
        main_copies(t - 1, 1 - slot, lambda cp: cp.wait())
    main_copies(t, slot, lambda cp: cp.start())

    for e in range(n_exp):
        for blk in range(1, tm // R):
            @pl.when(cnt_ref[t * n_exp + e] > blk * R)
            def _(e=e, blk=blk):
                more = jnp.where(srow + blk * R == rels[e], 1.0, 0.0).astype(BF16)
                extra[...] = jnp.dot(more, u, preferred_element_type=F32).astype(BF16)
                cp = pltpu.make_async_copy(extra, _rows_at(xs_hbm, base_ref[t * n_exp + e] + blk * R, R),
                                           extra_sem)
                cp.start()
                cp.wait()

    @pl.when(t == last)
    def _():
        main_copies(t, slot, lambda cp: cp.wait())
        zeros_buf[...] = jnp.zeros_like(zeros_buf)
        pieces = [SEGMENT_ALIGN << b for b in range((R // SEGMENT_ALIGN).bit_length() - 1)]

        def pad_copies(fn):
            for p in range(padlo_ref.shape[0]):
                lo, n = padlo_ref[p], padn_ref[p]
                whole = n // R

                def body(k, carry, lo=lo):
                    fn(pltpu.make_async_copy(zeros_buf, _rows_at(xs_hbm, lo + k * R, R), extra_sem))
                    return carry
                lax.fori_loop(0, whole, body, 0)
                rem = n - whole * R
                for sz in pieces:
                    off = lo + whole * R + (rem // (2 * sz)) * (2 * sz)
                    pl.when((rem // sz) % 2 == 1)(functools.partial(
                        lambda off, sz: fn(pltpu.make_async_copy(
                            zeros_buf.at[pl.ds(0, sz)], _rows_at(xs_hbm, off, sz), extra_sem)), off, sz))

        pad_copies(lambda cp: cp.start())
        pad_copies(lambda cp: cp.wait())


def _dispatch(u, meta_t, base_te, cnt_te, pad_lo, pad_n, *, n_rows, tm):
    T, D = u.shape
    n_exp = pad_lo.shape[0] - 1
    assert tm % DISPATCH_ROWS == 0
    grid_spec = pltpu.PrefetchScalarGridSpec(
        num_scalar_prefetch=4,
        grid=(T // tm,),
        in_specs=[pl.BlockSpec((tm, D), lambda t, *_: (t, 0)),
                  pl.BlockSpec((meta_t.shape[0], tm), lambda t, *_: (0, t))],
        out_specs=pl.BlockSpec(memory_space=pl.ANY),
        scratch_shapes=[pltpu.VMEM((2, n_exp * DISPATCH_ROWS, D), BF16),
                        pltpu.VMEM((DISPATCH_ROWS, D), BF16),
                        pltpu.VMEM((DISPATCH_ROWS, D), BF16),
                        pltpu.SemaphoreType.DMA((2,)),
                        pltpu.SemaphoreType.DMA(())],
    )
    return pl.pallas_call(
        functools.partial(_dispatch_kernel, n_exp=n_exp),
        grid_spec=grid_spec,
        out_shape=jax.ShapeDtypeStruct((n_rows, D), BF16),
        compiler_params=_params(("arbitrary",)),
        name="moe_dispatch",
    )(base_te, cnt_te, pad_lo, pad_n, u, meta_t)


TILE_FULL, TILE_HALF, TILE_EMPTY = 0, 1, 2


def _expert_kernel(te_ref, na_ref, fill_ref, xs_ref, wg_ref, wu_ref, wd_ref, o_ref, acc_ref):
    i = pl.program_id(0)
    j = pl.program_id(1)
    last = pl.num_programs(1) - 1
    active = i < na_ref[0]
    fill = fill_ref[i]
    tm = acc_ref.shape[0]

    @pl.when(active & (j == 0))
    def _():
        acc_ref[...] = jnp.zeros_like(acc_ref)

    def swiglu(rows):
        xb = xs_ref[rows, :]
        g = jnp.dot(xb, wg_ref[...].astype(BF16), preferred_element_type=F32)
        up = jnp.dot(xb, wu_ref[...].astype(BF16), preferred_element_type=F32)
        h = (g * _sigmoid(g) * up).astype(BF16)
        acc_ref[rows, :] += jnp.dot(h, wd_ref[...].astype(BF16), preferred_element_type=F32)

    pl.when(active & (fill == TILE_FULL))(functools.partial(swiglu, pl.ds(0, tm)))
    pl.when(active & (fill == TILE_HALF))(functools.partial(swiglu, pl.ds(0, tm // 2)))

    @pl.when(active & (j == last))
    def _():
        o_ref[...] = acc_ref[...].astype(BF16)

    @pl.when(jnp.logical_not(active) & (j == last))
    def _():
        o_ref[...] = jnp.zeros_like(o_ref)


def _expert_ffn(xs, tile_expert, n_active, tile_fill, wg, wu, wd, *, tm, tf):
    n_rows, D = xs.shape
    ff = wg.shape[2]
    assert n_rows % tm == 0 and ff % tf == 0
    nj = ff // tf

    def jj(i, j, na):
        return jnp.where(i < na[0], j, nj - 1)

    def ii(i, na):
        return jnp.minimum(i, na[0] - 1)

    grid_spec = pltpu.PrefetchScalarGridSpec(
        num_scalar_prefetch=3,
        grid=(n_rows // tm, nj),
        in_specs=[pl.BlockSpec((tm, D), lambda i, j, te, na, tf_: (ii(i, na), 0)),
                  pl.BlockSpec((None, D, tf), lambda i, j, te, na, tf_: (te[i], 0, jj(i, j, na))),
                  pl.BlockSpec((None, D, tf), lambda i, j, te, na, tf_: (te[i], 0, jj(i, j, na))),
                  pl.BlockSpec((None, tf, D), lambda i, j, te, na, tf_: (te[i], jj(i, j, na), 0))],
        out_specs=pl.BlockSpec((tm, D), lambda i, j, te, na, tf_: (i, 0)),
        scratch_shapes=[pltpu.VMEM((tm, D), F32)],
    )
    return pl.pallas_call(
        _expert_kernel,
        grid_spec=grid_spec,
        out_shape=jax.ShapeDtypeStruct((n_rows, D), BF16),
        compiler_params=_params(("arbitrary", "arbitrary")),
        name="moe_experts",
    )(tile_expert, n_active, tile_fill, xs, wg, wu, wd)


def _combine_kernel(base_ref, off_ref, cnt_ref, ys_hbm, meta_ref, x_ref, gate_ref, lg_ref, lb_ref, o_ref,
                    stage, extra, acc_ref, sems, extra_sem, *, alpha, n_exp):
    t = pl.program_id(0) * pl.num_programs(1) + pl.program_id(1)
    n_steps = pl.num_programs(0) * pl.num_programs(1)
    tm = x_ref.shape[0]
    R = DISPATCH_ROWS
    slot = t % 2

    def main_copies(step, buf, fn):
        for e in range(n_exp):
            fn(pltpu.make_async_copy(_rows_at(ys_hbm, base_ref[step * n_exp + e], R),
                                     stage.at[buf, pl.ds(e * R, R)], sems.at[buf]))

    @pl.when(t == 0)
    def _():
        main_copies(0, 0, lambda cp: cp.start())

    @pl.when(t + 1 < n_steps)
    def _():
        main_copies(t + 1, 1 - slot, lambda cp: cp.start())

    meta = meta_ref[...]
    col = lambda k: meta[:, k:k + 1]
    e0, e1 = col(META_E0).astype(I32), col(META_E1).astype(I32)
    r0, r1 = col(META_R0).astype(I32), col(META_R1).astype(I32)
    w0, w1 = col(META_W0), col(META_W1)
    lane = lax.broadcasted_iota(I32, (tm, R), 1)

    rels, ws, picks = [], [], []
    for e in range(n_exp):
        rel = jnp.where(e0 == e, r0, jnp.where(e1 == e, r1, -1))
        rels.append(jnp.where(rel >= 0, rel - off_ref[t * n_exp + e], -1))
        ws.append(jnp.where(e0 == e, w0, jnp.where(e1 == e, w1, 0.0)))
        picks.append(jnp.where(lane == rels[e], ws[e], 0.0))
    weighted = jnp.concatenate(picks, axis=1).astype(BF16)
    main_copies(t, slot, lambda cp: cp.wait())
    acc_ref[...] = jnp.dot(weighted, stage[slot], preferred_element_type=F32)

    for e in range(n_exp):
        for blk in range(1, tm // R):
            @pl.when(cnt_ref[t * n_exp + e] > blk * R)
            def _(e=e, blk=blk):
                cp = pltpu.make_async_copy(_rows_at(ys_hbm, base_ref[t * n_exp + e] + blk * R, R),
                                           extra, extra_sem)
                cp.start()
                cp.wait()
                more = jnp.where(lane + blk * R == rels[e], ws[e], 0.0).astype(BF16)
                acc_ref[...] += jnp.dot(more, extra[...], preferred_element_type=F32)

    z = alpha * x_ref[...] + (1.0 + gate_ref[...]) * acc_ref[...]
    o_ref[...] = _layer_norm(z, lg_ref[...], lb_ref[...])


def _combine(ys, base_te, off_te, cnt_te, meta, x, gate, ln_g, ln_b, *, alpha, tm):
    B, S, D = x.shape
    nt = S // tm
    n_exp = base_te.shape[0] // (B * nt)
    row = pl.BlockSpec((None, tm, D), lambda b, i, *_: (b, i, 0))
    vec = pl.BlockSpec((1, D), lambda b, i, *_: (0, 0))
    grid_spec = pltpu.PrefetchScalarGridSpec(
        num_scalar_prefetch=3,
        grid=(B, nt),
        in_specs=[pl.BlockSpec(memory_space=pl.ANY),
                  pl.BlockSpec((None, tm, LANES), lambda b, i, *_: (b, i, 0)),
                  row, pl.BlockSpec((None, 1, D), lambda b, i, *_: (b, 0, 0)), vec, vec],
        out_specs=row,
        scratch_shapes=[pltpu.VMEM((2, n_exp * DISPATCH_ROWS, D), BF16),
                        pltpu.VMEM((DISPATCH_ROWS, D), BF16),
                        pltpu.VMEM((tm, D), F32),
                        pltpu.SemaphoreType.DMA((2,)),
                        pltpu.SemaphoreType.DMA(())],
    )
    return pl.pallas_call(
        functools.partial(_combine_kernel, alpha=alpha, n_exp=n_exp),
        grid_spec=grid_spec,
        out_shape=jax.ShapeDtypeStruct((B, S, D), F32),
        compiler_params=_params(("arbitrary", "arbitrary")),
        name="moe_combine",
    )(base_te, off_te, cnt_te, ys, meta, x, gate, ln_g, ln_b)


def _moe_ffn(x, scale, shift, gate, w_router, wg, wu, wd, ln_g, ln_b, *, alpha, tm, tm_e, tf):
    B, S, D = x.shape
    T = B * S
    nt = T // tm
    n_exp = w_router.shape[1]
    wr = jnp.zeros((D, LANES), F32).at[:, :n_exp].set(w_router)
    wr_hi = wr.astype(BF16)
    wr = jnp.stack([wr_hi, (wr - wr_hi.astype(F32)).astype(BF16)])
    u, meta, counts = _router(x, scale, shift, wr, n_exp=n_exp, tm=tm)

    meta2 = meta.reshape(T, LANES)
    e0 = meta2[:, META_E0].astype(I32)
    e1 = meta2[:, META_E1].astype(I32)
    r0 = meta2[:, META_R0].astype(I32)
    r1 = meta2[:, META_R1].astype(I32)
    experts = jnp.arange(n_exp, dtype=I32)
    chosen = ((e0[:, None] == experts) | (e1[:, None] == experts)).astype(I32)
    cnt_te = chosen.reshape(nt, tm, n_exp).sum(axis=1)
    seg_te = -(-cnt_te // SEGMENT_ALIGN) * SEGMENT_ALIGN
    off_te = jnp.cumsum(cnt_te, axis=0) - cnt_te
    filled = seg_te.sum(axis=0)
    sizes = -(-(filled + DISPATCH_SLACK) // tm_e) * tm_e
    ends = jnp.cumsum(sizes)
    starts = ends - sizes
    base_te = starts[None, :] + jnp.cumsum(seg_te, axis=0) - seg_te
    shift_te = (base_te - off_te)[:, None, :]

    def rows_of(e, r):
        pick = (e[:, None] == experts).astype(I32).reshape(nt, tm, n_exp)
        return r + (pick * shift_te).sum(axis=-1).reshape(T)

    d0, d1 = rows_of(e0, r0), rows_of(e1, r1)
    bound = TOP_K * T + n_exp * ((SEGMENT_ALIGN - 1) * nt + DISPATCH_SLACK)
    n_tiles = -(-bound // tm_e) + n_exp + 1
    n_active = (ends[-1] // tm_e).astype(I32)
    tile_id = jnp.minimum(jnp.arange(n_tiles, dtype=I32), n_active - 1)
    tile_expert = jnp.sum((ends[None, :] <= (tile_id * tm_e)[:, None]).astype(I32), axis=1)
    data_end = jnp.sum((tile_expert[:, None] == experts) * (starts + filled)[None, :], axis=1)
    left = data_end - tile_id * tm_e
    tile_fill = jnp.where(left <= 0, TILE_EMPTY, jnp.where(left <= tm_e // 2, TILE_HALF, TILE_FULL)).astype(I32)
    meta_t = jnp.stack([e0, e1, d0, d1] + [jnp.zeros_like(e0)] * 4)

    xs = _dispatch(u.reshape(T, D), meta_t, base_te.reshape(-1), cnt_te.reshape(-1),
                   jnp.concatenate([starts + filled, ends[-1:]]),
                   jnp.concatenate([sizes - filled, n_tiles * tm_e - ends[-1:]]),
                   n_rows=n_tiles * tm_e, tm=tm)
    ys = _expert_ffn(xs, tile_expert, n_active.reshape(1), tile_fill, wg, wu, wd, tm=tm_e, tf=tf)
    return _combine(ys, base_te.reshape(-1), off_te.reshape(-1), cnt_te.reshape(-1),
                    meta, x, gate, ln_g, ln_b, alpha=alpha, tm=tm)


def kernel(x, c, positions, w_in, w_out, attn_norm_gain, rec_norm_gain, rec_lb_logits, ada_w, ada_b,
           ln_gain, ln_bias, ffn_w_gate, ffn_w_up, ffn_w_down, moe_router, moe_w_gate, moe_w_up,
           moe_w_down):
    B, S, D = x.shape
    depth = w_in.shape[0]
    alpha = (2 * depth) ** 0.25

    p = jax.nn.softmax(rec_lb_logits.astype(F32), axis=0)
    cum = jnp.cumsum(p, axis=0)
    lb_all = cum - cum[0:1]
    half = ATTN_HEAD_DIM // 2
    inv_freq = ROPE_THETA ** (-jnp.arange(half, dtype=F32) / half)
    inv_freq_lanes = jnp.tile(inv_freq, LANES // half).reshape(1, LANES)
    pos_f = positions.astype(F32).reshape(B, S, 1)
    n_qk = 2 * ATTN_WIDTH
    w_in_b = jnp.concatenate([_qk_lane_order(w_in[..., :n_qk]), w_in[..., n_qk:]], axis=-1).astype(BF16)
    w_out_b = w_out.astype(BF16)
    ff = ffn_w_gate.shape[2]
    ffp = -(-ff // (2 * LANES)) * (2 * LANES)
    pad_c = lambda w: jnp.pad(w.astype(BF16), ((0, 0), (0, 0), (0, ffp - ff)))
    ffn_g, ffn_u = pad_c(ffn_w_gate), pad_c(ffn_w_up)
    ffn_d = jnp.pad(ffn_w_down.astype(BF16), ((0, 0), (0, ffp - ff), (0, 0)))

    mods = _ada_modulation(c, ada_w, ada_b)
    rope_cos, rope_sin = _rope_tables(pos_f, inv_freq_lanes, tm=min(1024, S))

    def mod(layer, sub):
        m = mods[layer * 2 + sub].reshape(B, 1, 3 * D)
        return m[..., :D], m[..., D:2 * D], m[..., 2 * D:]

    vec = lambda a: a.reshape(1, -1)
    for layer in range(depth):
        shift, scale, gate = mod(layer, 0)
        proj = _in_projection(x, scale, shift, rope_cos, rope_sin, w_in_b[layer], tm=min(512, S))
        attn = _dilated_attention(proj)
        rec = _hgrn2(proj, lb_all[layer], vec(rec_norm_gain[layer]))
        x = _out_projection(attn, rec, x, gate, vec(attn_norm_gain[layer]), w_out_b[layer],
                            vec(ln_gain[layer, 0]), vec(ln_bias[layer, 0]), alpha=alpha, tm=min(512, S))
        shift, scale, gate = mod(layer, 1)
        j = layer // 2
        if layer % 2 == 0:
            x = _dense_ffn(x, scale, shift, gate, ffn_g[j], ffn_u[j], ffn_d[j],
                           vec(ln_gain[layer, 1]), vec(ln_bias[layer, 1]),
                           alpha=alpha, tm=min(512, S), tc=2 * LANES)
        else:
            x = _moe_ffn(x, scale, shift, gate, moe_router[j], moe_w_gate[j], moe_w_up[j],
                         moe_w_down[j], vec(ln_gain[layer, 1]), vec(ln_bias[layer, 1]),
                         alpha=alpha, tm=min(512, S), tm_e=min(1024, B * S), tf=512)
    return x
```

```python
import functools

import jax
import jax.numpy as jnp
from jax import lax
from jax.experimental import pallas as pl
from jax.experimental.pallas import tpu as pltpu

F32 = jnp.float32
BF16 = jnp.bfloat16
I32 = jnp.int32

LANES = 128
V7X_VMEM_LIMIT_BYTES = 56 * 1024 * 1024

ATTN_WIDTH = 512
ATTN_HEAD_DIM = 64
DILATIONS = (1, 4, 16)
ATTN_RADIUS = 64
ROPE_THETA = 10000.0
REC_WIDTH = 512
REC_DIM = 128
REC_CHUNK = 64
TOP_K = 2
LN_EPS = 1e-5
RMS_EPS = 1e-6
MASK_VALUE = -1e30
DECAY_EXP_CLAMP = 80.0
LOG2_E = 1.4426950408889634


def _params(semantics):
    return pltpu.CompilerParams(dimension_semantics=semantics,
                                vmem_limit_bytes=V7X_VMEM_LIMIT_BYTES)


def _sigmoid(x):
    return 1.0 / (1.0 + jnp.exp(-x))


def _layer_norm(z, gain, bias):
    mu = jnp.mean(z, axis=-1, keepdims=True)
    zc = z - mu
    var = jnp.mean(zc * zc, axis=-1, keepdims=True)
    return zc * lax.rsqrt(var + LN_EPS) * gain + bias


def _ada_kernel(c_ref, w_ref, b_ref, o_ref):
    c = c_ref[...]
    o_ref[...] = jnp.dot(c * _sigmoid(c), w_ref[...], precision=lax.Precision.HIGHEST,
                         preferred_element_type=F32) + b_ref[...]


def _ada_modulation(c, ada_w, ada_b):
    B, D = c.shape
    n = ada_w.shape[0] * ada_w.shape[1]
    n3 = ada_w.shape[-1]
    tn = 1024
    return pl.pallas_call(
        _ada_kernel,
        grid=(n, n3 // tn),
        in_specs=[pl.BlockSpec((B, D), lambda i, j: (0, 0)),
                  pl.BlockSpec((None, D, tn), lambda i, j: (i, 0, j)),
                  pl.BlockSpec((None, 1, tn), lambda i, j: (i, 0, j))],
        out_specs=pl.BlockSpec((None, B, tn), lambda i, j: (i, 0, j)),
        out_shape=jax.ShapeDtypeStruct((n, B, n3), F32),
        compiler_params=_params(("arbitrary", "arbitrary")),
        name="ada_modulation",
    )(c, ada_w.reshape(n, D, n3), ada_b.reshape(n, 1, n3))


ROPE_HALF = ATTN_HEAD_DIM // 2


def _qk_lane_order(w_cols):
    lead = w_cols.shape[:-1]
    w = w_cols.reshape(*lead, -1, 2, 2, ROPE_HALF)
    return jnp.swapaxes(w, -3, -2).reshape(*lead, -1)


def _rope_kernel(pos_ref, invf_ref, cos_ref, sin_ref):
    ang = pos_ref[...] * invf_ref[...]
    lane = lax.broadcasted_iota(I32, (1, LANES), 1)
    sin = jnp.sin(ang)
    cos_ref[...] = jnp.cos(ang)
    sin_ref[...] = jnp.where(lane < LANES // 2, -sin, sin)


def _rope_tables(pos_f, inv_freq_lanes, *, tm):
    B, S, _ = pos_f.shape
    tab = pl.BlockSpec((None, tm, LANES), lambda b, i: (b, i, 0))
    return pl.pallas_call(
        _rope_kernel,
        grid=(B, S // tm),
        in_specs=[pl.BlockSpec((None, tm, 1), lambda b, i: (b, i, 0)),
                  pl.BlockSpec((1, LANES), lambda b, i: (0, 0))],
        out_specs=[tab, tab],
        out_shape=[jax.ShapeDtypeStruct((B, S, LANES), F32)] * 2,
        compiler_params=_params(("arbitrary", "arbitrary")),
        name="rope_tables",
    )(pos_f, inv_freq_lanes)


def _inproj_kernel(x_ref, sc_ref, sh_ref, cos_ref, sin_ref, w_ref, o_ref, *, tn):
    u = (x_ref[...] * (1.0 + sc_ref[...]) + sh_ref[...]).astype(BF16)
    cos = cos_ref[...]
    sin_signed = sin_ref[...]
    q_scale = ATTN_HEAD_DIM ** -0.5
    for j in range(w_ref.shape[1] // tn):
        acc = jnp.dot(u, w_ref[:, j * tn:(j + 1) * tn], preferred_element_type=F32)
        if j > 0:
            o_ref[:, j * tn:(j + 1) * tn] = acc
            continue
        for kk in range(tn // LANES):
            c = acc[:, kk * LANES:(kk + 1) * LANES]
            r = c * cos + pltpu.roll(c, LANES // 2, 1) * sin_signed
            if kk * LANES < ATTN_WIDTH:
                r = r * q_scale
            o_ref[:, kk * LANES:(kk + 1) * LANES] = r


def _in_projection(x, scale, shift, rope_cos, rope_sin, w_in_bf16, *, tm):
    B, S, D = x.shape
    n_cols = w_in_bf16.shape[1]
    tn = 2 * ATTN_WIDTH
    assert S % tm == 0 and n_cols % tn == 0
    return pl.pallas_call(
        functools.partial(_inproj_kernel, tn=tn),
        grid=(B, S // tm),
        in_specs=[pl.BlockSpec((None, tm, D), lambda b, i: (b, i, 0)),
                  pl.BlockSpec((None, 1, D), lambda b, i: (b, 0, 0)),
                  pl.BlockSpec((None, 1, D), lambda b, i: (b, 0, 0)),
                  pl.BlockSpec((None, tm, LANES), lambda b, i: (b, i, 0)),
                  pl.BlockSpec((None, tm, LANES), lambda b, i: (b, i, 0)),
                  pl.BlockSpec((D, n_cols), lambda b, i: (0, 0))],
        out_specs=pl.BlockSpec((None, tm, n_cols), lambda b, i: (b, i, 0)),
        out_shape=jax.ShapeDtypeStruct((B, S, n_cols), F32),
        compiler_params=_params(("arbitrary", "arbitrary")),
        name="in_projection",
    )(x, scale, shift, rope_cos, rope_sin, w_in_bf16)


ATTN_BLOCK_GROUP = 16


def _attn_kernel(q_ref, k_ref, v_ref, o_ref, qs, ks, vs, qf, kf, vf, pm, plr, pa, nm, nl, na, bias_s,
                 *, S):
    lane = lax.broadcasted_iota(I32, (1, LANES), 1)
    head0 = lane < ATTN_HEAD_DIM
    head0_qk = (lane % ATTN_HEAD_DIM) < ROPE_HALF

    for p, d in enumerate(DILATIONS):
        L = S // d
        tq = min(128, L)
        W = min(2 * tq, L)
        nb = L // tq

        d_prev = DILATIONS[p - 1] if p else 1
        ratio, l_prev = d // d_prev, S // d_prev
        keep_f32 = 0 < p < len(DILATIONS) - 1
        for src, stage, dst in ((q_ref, qf, qs), (k_ref, kf, ks), (v_ref, vf, vs)):
            src = src if p <= 1 else stage
            for r in range(d):
                a, r_prev = divmod(r, d_prev)
                rows = pl.ds(r_prev * l_prev + a, L, stride=ratio) if ratio > 1 else pl.ds(0, L)
                val = src[rows, :]
                if keep_f32:
                    stage[r * L:(r + 1) * L, :] = val
                dst[r * L:(r + 1) * L, :] = val.astype(BF16)

        n_blocks = d * nb
        rc = lax.broadcasted_iota(I32, (tq, W), 0) - lax.broadcasted_iota(I32, (tq, W), 1)
        for which, delta in enumerate((0, ATTN_RADIUS, tq)):
            bias_s[which, pl.ds(0, tq), pl.ds(0, W)] = jnp.where(
                jnp.abs(rc + delta) <= ATTN_RADIUS, 0.0, MASK_VALUE)
        group = min(ATTN_BLOCK_GROUP, n_blocks)
        assert n_blocks % group == 0

        def block_group(gi, carry, L=L, tq=tq, W=W, nb=nb, group=group):
            q0s, k0s, scores, probs, outs, sums = [], [], [], [], [], []
            for u in range(group):
                g = gi * group + u
                r = g // nb
                n = g - r * nb
                base = r * L
                q0 = pl.multiple_of(base + n * tq, tq)
                ws = jnp.clip(n * tq - ATTN_RADIUS, 0, L - W)
                k0 = pl.multiple_of(base + ws, 16)
                qb = qs[pl.ds(q0, tq), :]
                zero = jnp.zeros_like(qb)
                q2 = jnp.concatenate([jnp.where(head0_qk, qb, zero), jnp.where(head0_qk, zero, qb)], axis=0)
                s = lax.dot_general(q2, ks[pl.ds(k0, W), :], (((1,), (1,)), ((), ())),
                                    preferred_element_type=F32)
                delta = n * tq - ws
                which = jnp.where(delta == 0, 0, jnp.where(delta == ATTN_RADIUS, 1, 2))
                bias = bias_s[which, pl.ds(0, tq), pl.ds(0, W)]
                scores.append(s + jnp.concatenate([bias, bias], axis=0))
                q0s.append(q0)
                k0s.append(k0)
            for u in range(group):
                s = scores[u]
                m = jnp.max(s, axis=1, keepdims=True)
                e = jnp.exp(s - m)
                probs.append((m, e.astype(BF16)))
                sums.append(jnp.sum(e, axis=1, keepdims=True))
            for u in range(group):
                outs.append(jnp.dot(probs[u][1], vs[pl.ds(k0s[u], W), :], preferred_element_type=F32))
            for u in range(group):
                m, l = probs[u][0], sums[u]
                rows = pl.ds(q0s[u], tq)
                out_m[rows, :] = jnp.where(head0, m[:tq], m[tq:])
                out_l[rows, :] = jnp.where(head0, l[:tq], l[tq:])
                out_a[rows, :] = jnp.where(head0, outs[u][:tq], outs[u][tq:])
            return carry

        out_m, out_l, out_a = (nm.at[p], nl.at[p], na.at[p]) if d == 1 else (pm, plr, pa)
        lax.fori_loop(0, n_blocks // group, block_group, 0)

        for r in range(d if d > 1 else 0):
            rows = pl.ds(r, L, stride=d)
            nm[p, rows, :] = pm[r * L:(r + 1) * L, :]
            nl[p, rows, :] = plr[r * L:(r + 1) * L, :]
            na[p, rows, :] = pa[r * L:(r + 1) * L, :]

    m_all = jnp.maximum(jnp.maximum(nm[0], nm[1]), nm[2])
    num = jnp.zeros((S, LANES), F32)
    den = jnp.zeros((S, LANES), F32)
    for p in range(len(DILATIONS)):
        w = jnp.exp(nm[p] - m_all)
        num = num + w * na[p]
        den = den + w * nl[p]
    o_ref[...] = (num / den).astype(o_ref.dtype)


def _dilated_attention(proj):
    B, S, _ = proj.shape
    n_pairs = ATTN_WIDTH // LANES
    assert S % (16 * DILATIONS[-1]) == 0
    blk = lambda off: pl.BlockSpec((None, S, LANES), lambda b, h: (b, 0, off + h))
    return pl.pallas_call(
        functools.partial(_attn_kernel, S=S),
        grid=(B, n_pairs),
        in_specs=[blk(0), blk(n_pairs), blk(2 * n_pairs)],
        out_specs=pl.BlockSpec((None, S, LANES), lambda b, h: (b, 0, h)),
        out_shape=jax.ShapeDtypeStruct((B, S, ATTN_WIDTH), BF16),
        scratch_shapes=[pltpu.VMEM((S, LANES), BF16)] * 3
                       + [pltpu.VMEM((S, LANES), F32)] * 6
                       + [pltpu.VMEM((len(DILATIONS), S, LANES), F32)] * 3
                       + [pltpu.VMEM((3, min(128, S), min(256, S)), F32)],
        compiler_params=_params(("arbitrary", "arbitrary")),
        name="dilated_attention",
    )(proj, proj, proj)


REC_GROUP_CHUNKS = 4
REC_GROUP_ROWS = REC_GROUP_CHUNKS * REC_CHUNK
REC_STAGE_GROUPS = 8


def _hgrn_kernel(rq_ref, zf_ref, zb_ref, ri_ref, rg_ref, lb_ref, gain_ref, o_ref,
                 q_s, vt_s, oi_s, qp_s, ut_s, dd_s, st_s, os_s, *, S):
    C, G, GR = REC_CHUNK, REC_GROUP_CHUNKS, REC_GROUP_ROWS
    nc, ng = S // C, S // GR
    sg = min(REC_STAGE_GROUPS, ng)
    assert ng % sg == 0

    row = lax.broadcasted_iota(I32, (GR, GR), 0)
    col = lax.broadcasted_iota(I32, (GR, GR), 1)
    same_chunk = (row // C) == (col // C)
    t_i = lax.broadcasted_iota(I32, (C, C), 0)
    s_i = lax.broadcasted_iota(I32, (C, C), 1)

    def prep(i, carry):
        r0 = pl.multiple_of(i * GR, GR)
        rq = rq_ref[pl.ds(r0, GR), :]
        q_s[pl.ds(r0, GR), :] = rq * _sigmoid(rq)
        v = ri_ref[pl.ds(r0, GR), :]
        for c in range(G):
            vt_s[i * G + c] = v[c * C:(c + 1) * C, :].T.astype(BF16)
        return carry

    lax.fori_loop(0, ng, prep, 0, unroll=True)

    for direction in range(2):
        fwd = direction == 0
        z_ref = zf_ref if fwd else zb_ref
        cum = jnp.where(same_chunk & ((col <= row) if fwd else (col >= row)), 1.0, 0.0).astype(BF16)
        keep = (s_i <= t_i) if fwd else (s_i >= t_i)
        mid_row = C // 2 - 1 if fwd else C // 2
        last_row = C - 1 if fwd else 0

        def phase_a(it, carry, fwd=fwd, z_ref=z_ref, cum=cum, keep=keep,
                    mid_row=mid_row, last_row=last_row, direction=direction):
            lb = lb_ref[direction:direction + 1, :]
            log_lb = jnp.log(lb)
            log_1m_lb = jnp.log1p(-lb)
            r0s, c0s, k3s, bcs, atts, stage3 = [], [], [], [], [], []
            for u in range(sg):
                i = it * sg + u
                r0 = pl.multiple_of(i * GR, GR)
                z = z_ref[pl.ds(r0, GR), :]
                e = jnp.exp(-jnp.abs(z))
                t = log_1m_lb + (jnp.minimum(z, 0.0) - jnp.log(1.0 + e))
                g = jnp.maximum(log_lb, t) + jnp.log(1.0 + jnp.exp(-jnp.abs(log_lb - t)))
                kk = (1.0 - lb) * (jnp.where(z > 0, e, 1.0) / (1.0 + e))
                g = g * LOG2_E
                g1 = g.astype(BF16)
                r1 = g - g1.astype(F32)
                g2 = r1.astype(BF16)
                g3 = (r1 - g2.astype(F32)).astype(BF16)
                bcs.append(jnp.dot(cum, g1, preferred_element_type=F32)
                           + jnp.dot(cum, g2, preferred_element_type=F32)
                           + jnp.dot(cum, g3, preferred_element_type=F32))
                k3s.append(kk.reshape(G, C, REC_DIM))
                r0s.append(r0)
                c0s.append(pl.multiple_of(i * G, G))
            for u in range(sg):
                b3 = bcs[u].reshape(G, C, REC_DIM)
                mid = b3[:, mid_row:mid_row + 1, :]
                last = b3[:, last_row:last_row + 1, :]
                e3 = b3 - mid
                q3 = q_s[pl.ds(r0s[u], GR), :].reshape(G, C, REC_DIM)
                clamp = DECAY_EXP_CLAMP * LOG2_E
                qt = (q3 * jnp.exp2(jnp.minimum(e3, clamp))).astype(BF16)
                kt = (k3s[u] * jnp.exp2(jnp.minimum(-e3, clamp))).astype(BF16)
                atts.append(jnp.einsum('gtc,gsc->gts', qt, kt, preferred_element_type=F32))
                qp_s[pl.ds(r0s[u], GR), :] = (q3 * jnp.exp2(b3)).astype(BF16).reshape(GR, REC_DIM)
                stage3.append(((k3s[u] * jnp.exp2(last - b3)).astype(BF16), jnp.exp2(last)))
            for u in range(sg):
                att = jnp.where(keep[None], atts[u], 0.0).astype(BF16)
                v3 = ri_ref[pl.ds(r0s[u], GR), :].reshape(G, C, REC_DIM).astype(BF16)
                oi = jnp.einsum('gts,gsv->gtv', att, v3, preferred_element_type=F32)
                oi_s[pl.ds(r0s[u], GR), :] = oi.reshape(GR, REC_DIM)
                kp, decay = stage3[u]
                ut_s[pl.ds(c0s[u], G)] = jnp.einsum('gvs,gsc->gvc', vt_s[pl.ds(c0s[u], G)], kp,
                                                    preferred_element_type=F32)
                dd_s[pl.ds(c0s[u], G)] = decay
            return carry

        lax.fori_loop(0, ng // sg, phase_a, 0)

        def scan(c, st, fwd=fwd):
            idx = c if fwd else nc - 1 - c
            st_s[idx] = st.astype(BF16)
            return dd_s[idx] * st + ut_s[idx]

        lax.fori_loop(0, nc, scan, jnp.zeros((REC_DIM, REC_DIM), F32))

        def phase_c(it, carry, fwd=fwd):
            r0s = [pl.multiple_of((it * sg + u) * GR, GR) for u in range(sg)]
            oos = []
            for u in range(sg):
                c0 = pl.multiple_of((it * sg + u) * G, G)
                qp3 = qp_s[pl.ds(r0s[u], GR), :].reshape(G, C, REC_DIM)
                oos.append(jnp.einsum('gtc,gvc->gtv', qp3, st_s[pl.ds(c0, G)],
                                      preferred_element_type=F32))
            for u in range(sg):
                rows = pl.ds(r0s[u], GR)
                tot = oi_s[rows, :] + oos[u].reshape(GR, REC_DIM)
                os_s[rows, :] = tot if fwd else os_s[rows, :] + tot
            return carry

        lax.fori_loop(0, ng // sg, phase_c, 0)

    o = os_s[...]
    o = o * lax.rsqrt(jnp.mean(o * o, axis=-1, keepdims=True) + RMS_EPS) * gain_ref[...]
    o_ref[...] = (o * _sigmoid(rg_ref[...])).astype(o_ref.dtype)


def _hgrn2(proj, lb, rec_gain):
    B, S, _ = proj.shape
    nh = REC_WIDTH // REC_DIM
    assert S % REC_GROUP_ROWS == 0
    col0 = 3 * ATTN_WIDTH // LANES
    blk = lambda k: pl.BlockSpec((None, S, REC_DIM), lambda b, h: (b, 0, col0 + k * nh + h))
    nc = S // REC_CHUNK
    return pl.pallas_call(
        functools.partial(_hgrn_kernel, S=S),
        grid=(B, nh),
        in_specs=[blk(0), blk(1), blk(2), blk(3), blk(4),
                  pl.BlockSpec((2, REC_DIM), lambda b, h: (0, h)),
                  pl.BlockSpec((1, REC_DIM), lambda b, h: (0, h))],
        out_specs=pl.BlockSpec((None, S, REC_DIM), lambda b, h: (b, 0, h)),
        out_shape=jax.ShapeDtypeStruct((B, S, REC_WIDTH), BF16),
        scratch_shapes=[pltpu.VMEM((S, REC_DIM), F32),
                        pltpu.VMEM((nc, REC_DIM, REC_CHUNK), BF16),
                        pltpu.VMEM((S, REC_DIM), F32),
                        pltpu.VMEM((S, REC_DIM), BF16),
                        pltpu.VMEM((nc, REC_DIM, REC_DIM), F32),
                        pltpu.VMEM((nc, 1, REC_DIM), F32),
                        pltpu.VMEM((nc, REC_DIM, REC_DIM), BF16),
                        pltpu.VMEM((S, REC_DIM), F32)],
        compiler_params=_params(("arbitrary", "arbitrary")),
        name="hgrn2",
    )(proj, proj, proj, proj, proj, lb, rec_gain)


def _outproj_kernel(a_ref, r_ref, x_ref, gate_ref, ag_ref, w_ref, lg_ref, lb_ref, o_ref, *, alpha):
    a = a_ref[...].astype(F32)
    an = a * lax.rsqrt(jnp.mean(a * a, axis=-1, keepdims=True) + RMS_EPS) * ag_ref[...]
    y = (jnp.dot(an.astype(BF16), w_ref[:ATTN_WIDTH, :], preferred_element_type=F32)
         + jnp.dot(r_ref[...], w_ref[ATTN_WIDTH:, :], preferred_element_type=F32))
    z = alpha * x_ref[...] + (1.0 + gate_ref[...]) * y
    o_ref[...] = _layer_norm(z, lg_ref[...], lb_ref[...])


def _out_projection(attn, rec, x, gate, attn_gain, w_out_bf16, ln_g, ln_b, *, alpha, tm):
    B, S, D = x.shape
    row = lambda w: pl.BlockSpec((None, tm, w), lambda b, i: (b, i, 0))
    vec = lambda w: pl.BlockSpec((1, w), lambda b, i: (0, 0))
    return pl.pallas_call(
        functools.partial(_outproj_kernel, alpha=alpha),
        grid=(B, S // tm),
        in_specs=[row(ATTN_WIDTH), row(REC_WIDTH), row(D),
                  pl.BlockSpec((None, 1, D), lambda b, i: (b, 0, 0)),
                  vec(ATTN_WIDTH),
                  pl.BlockSpec((ATTN_WIDTH + REC_WIDTH, D), lambda b, i: (0, 0)),
                  vec(D), vec(D)],
        out_specs=row(D),
        out_shape=jax.ShapeDtypeStruct((B, S, D), F32),
        compiler_params=_params(("arbitrary", "arbitrary")),
        name="out_projection",
    )(attn, rec, x, gate, attn_gain, w_out_bf16, ln_g, ln_b)


def _ffn_kernel(x_ref, sc_ref, sh_ref, gate_ref, wg_ref, wu_ref, wd_ref, lg_ref, lb_ref, o_ref,
                *, alpha, tc):
    x = x_ref[...]
    u = (x * (1.0 + sc_ref[...]) + sh_ref[...]).astype(BF16)
    acc = jnp.zeros(x.shape, F32)
    for c in range(wg_ref.shape[1] // tc):
        cols = slice(c * tc, (c + 1) * tc)
        g = jnp.dot(u, wg_ref[:, cols], preferred_element_type=F32)
        up = jnp.dot(u, wu_ref[:, cols], preferred_element_type=F32)
        h = (g * _sigmoid(g) * up).astype(BF16)
        acc = acc + jnp.dot(h, wd_ref[cols, :], preferred_element_type=F32)
    z = alpha * x + (1.0 + gate_ref[...]) * acc
    o_ref[...] = _layer_norm(z, lg_ref[...], lb_ref[...])


def _dense_ffn(x, scale, shift, gate, wg, wu, wd, ln_g, ln_b, *, alpha, tm, tc):
    B, S, D = x.shape
    ffp = wg.shape[1]
    assert ffp % tc == 0
    row = pl.BlockSpec((None, tm, D), lambda b, i: (b, i, 0))
    mod = pl.BlockSpec((None, 1, D), lambda b, i: (b, 0, 0))
    vec = pl.BlockSpec((1, D), lambda b, i: (0, 0))
    return pl.pallas_call(
        functools.partial(_ffn_kernel, alpha=alpha, tc=tc),
        grid=(B, S // tm),
        in_specs=[row, mod, mod, mod,
                  pl.BlockSpec((D, ffp), lambda b, i: (0, 0)),
                  pl.BlockSpec((D, ffp), lambda b, i: (0, 0)),
                  pl.BlockSpec((ffp, D), lambda b, i: (0, 0)),
                  vec, vec],
        out_specs=row,
        out_shape=jax.ShapeDtypeStruct((B, S, D), F32),
        compiler_params=_params(("arbitrary", "arbitrary")),
        name="dense_ffn",
    )(x, scale, shift, gate, wg, wu, wd, ln_g, ln_b)


META_E0, META_E1, META_W0, META_W1, META_R0, META_R1 = range(6)
SEGMENT_ALIGN = 16
DISPATCH_ROWS = 256
DISPATCH_HALF = DISPATCH_ROWS // 2
DISPATCH_SLACK = DISPATCH_HALF


def _router_kernel(x_ref, sc_ref, sh_ref, wr_ref, u_ref, meta_ref, cnt_ref, carry_ref, *, n_exp):
    tm = x_ref.shape[0]

    @pl.when((pl.program_id(0) == 0) & (pl.program_id(1) == 0))
    def _():
        carry_ref[...] = jnp.zeros_like(carry_ref)

    u = x_ref[...] * (1.0 + sc_ref[...]) + sh_ref[...]
    u_ref[...] = u.astype(BF16)
    u_hi = u.astype(BF16)
    u_lo = (u - u_hi.astype(F32)).astype(BF16)
    logits = (jnp.dot(u_hi, wr_ref[0], preferred_element_type=F32)
              + jnp.dot(u_lo, wr_ref[0], preferred_element_type=F32)
              + jnp.dot(u_hi, wr_ref[1], preferred_element_type=F32))
    lane = lax.broadcasted_iota(I32, (tm, LANES), 1)
    neg = -jnp.inf
    l1 = jnp.where(lane < n_exp, logits, neg)
    m1 = jnp.max(l1, axis=1, keepdims=True)
    i1 = jnp.min(jnp.where(l1 == m1, lane, LANES), axis=1, keepdims=True)
    l2 = jnp.where(lane == i1, neg, l1)
    m2 = jnp.max(l2, axis=1, keepdims=True)
    i2 = jnp.min(jnp.where(l2 == m2, lane, LANES), axis=1, keepdims=True)
    e = jnp.exp(m2 - m1)
    w1 = 1.0 / (1.0 + e)
    w2 = e * w1
    sel = jnp.where((lane == i1) | (lane == i2), 1.0, 0.0)
    before = (lax.broadcasted_iota(I32, (tm, tm), 1) < lax.broadcasted_iota(I32, (tm, tm), 0))
    ranks = jnp.dot(jnp.where(before, 1.0, 0.0).astype(BF16), sel.astype(BF16),
                    preferred_element_type=F32) + carry_ref[...]
    r1 = jnp.sum(jnp.where(lane == i1, ranks, 0.0), axis=1, keepdims=True)
    r2 = jnp.sum(jnp.where(lane == i2, ranks, 0.0), axis=1, keepdims=True)
    carry_ref[...] = carry_ref[...] + jnp.sum(sel, axis=0, keepdims=True)
    cnt_ref[...] = carry_ref[...]
    meta = jnp.zeros((tm, LANES), F32)
    for k, val in ((META_E0, i1.astype(F32)), (META_E1, i2.astype(F32)), (META_W0, w1),
                   (META_W1, w2), (META_R0, r1), (META_R1, r2)):
        meta = jnp.where(lane == k, val, meta)
    meta_ref[...] = meta


def _router(x, scale, shift, w_router_lanes, *, n_exp, tm):
    B, S, D = x.shape
    nt = S // tm
    row = pl.BlockSpec((None, tm, D), lambda b, i: (b, i, 0))
    mod = pl.BlockSpec((None, 1, D), lambda b, i: (b, 0, 0))
    return pl.pallas_call(
        functools.partial(_router_kernel, n_exp=n_exp),
        grid=(B, nt),
        in_specs=[row, mod, mod, pl.BlockSpec((2, D, LANES), lambda b, i: (0, 0, 0))],
        out_specs=[row,
                   pl.BlockSpec((None, tm, LANES), lambda b, i: (b, i, 0)),
                   pl.BlockSpec((1, LANES), lambda b, i: (0, 0))],
        out_shape=[jax.ShapeDtypeStruct((B, S, D), BF16),
                   jax.ShapeDtypeStruct((B, S, LANES), F32),
                   jax.ShapeDtypeStruct((1, LANES), F32)],
        scratch_shapes=[pltpu.VMEM((1, LANES), F32)],
        compiler_params=_params(("arbitrary", "arbitrary")),
        name="moe_router",
    )(x, scale, shift, w_router_lanes)


def _rows_at(hbm, first_row, n_rows):
    return hbm.at[pl.ds(pl.multiple_of(first_row, SEGMENT_ALIGN), n_rows)]


def _dispatch_kernel(base_ref, cnt_ref, padlo_ref, padn_ref, u_ref, meta_ref, xs_hbm,
                     stage, extra, zeros_buf, sems, extra_sem, *, n_exp):
    t = pl.program_id(0)
    last = pl.num_programs(0) - 1
    tm = u_ref.shape[0]
    R, H = DISPATCH_ROWS, DISPATCH_HALF
    slot = t % 2
    u = u_ref[...]
    e0, e1 = meta_ref[0:1, :], meta_ref[1:2, :]
    d0, d1 = meta_ref[2:3, :], meta_ref[3:4, :]
    srow = lax.broadcasted_iota(I32, (R, tm), 0)

    def main_copies(step, buf, fn):
        for e in range(n_exp):
            base = base_ref[step * n_exp + e]
            fn(pltpu.make_async_copy(stage.at[buf, pl.ds(e * R, H)], _rows_at(xs_hbm, base, H),
                                     sems.at[buf]))

            @pl.when(cnt_ref[step * n_exp + e] > H)
            def _(e=e, base=base):
                fn(pltpu.make_async_copy(stage.at[buf, pl.ds(e * R + H, H)],
                                         _rows_at(xs_hbm, base + H, H), sems.at[buf]))

    rels = []
    for e in range(n_exp):
        rel = jnp.where(e0 == e, d0, jnp.where(e1 == e, d1, -1))
        rels.append(jnp.where(rel >= 0, rel - base_ref[t * n_exp + e], -1))
    onehot = jnp.concatenate([jnp.where(srow == rel, 1.0, 0.0) for rel in rels], axis=0).astype(BF16)
    stage[slot] = jnp.dot(onehot, u, preferred_element_type=F32).astype(BF16)

    @pl.when(t > 0)
    def _():
        main_copies(t - 1, 1 - slot, lambda cp: cp.wait())
    main_copies(t, slot, lambda cp: cp.start())

    for e in range(n_exp):
        for blk in range(1, tm // R):
            @pl.when(cnt_ref[t * n_exp + e] > blk * R)
            def _(e=e, blk=blk):
                more = jnp.where(srow + blk * R == rels[e], 1.0, 0.0).astype(BF16)
                extra[...] = jnp.dot(more, u, preferred_element_type=F32).astype(BF16)
                cp = pltpu.make_async_copy(extra, _rows_at(xs_hbm, base_ref[t * n_exp + e] + blk * R, R),
                                           extra_sem)
                cp.start()
                cp.wait()

    @pl.when(t == last)
    def _():
        main_copies(t, slot, lambda cp: cp.wait())
        zeros_buf[...] = jnp.zeros_like(zeros_buf)
        pieces = [SEGMENT_ALIGN << b for b in range((R // SEGMENT_ALIGN).bit_length() - 1)]

        def pad_copies(fn):
            for p in range(padlo_ref.shape[0]):
                lo, n = padlo_ref[p], padn_ref[p]
                whole = n // R

                def body(k, carry, lo=lo):
                    fn(pltpu.make_async_copy(zeros_buf, _rows_at(xs_hbm, lo + k * R, R), extra_sem))
                    return carry
                lax.fori_loop(0, whole, body, 0)
                rem = n - whole * R
                for sz in pieces:
                    off = lo + whole * R + (rem // (2 * sz)) * (2 * sz)
                    pl.when((rem // sz) % 2 == 1)(functools.partial(
                        lambda off, sz: fn(pltpu.make_async_copy(
                            zeros_buf.at[pl.ds(0, sz)], _rows_at(xs_hbm, off, sz), extra_sem)), off, sz))

        pad_copies(lambda cp: cp.start())
        pad_copies(lambda cp: cp.wait())


def _dispatch(u, meta_t, base_te, cnt_te, pad_lo, pad_n, *, n_rows, tm):
    T, D = u.shape
    n_exp = pad_lo.shape[0] - 1
    assert tm % DISPATCH_ROWS == 0
    grid_spec = pltpu.PrefetchScalarGridSpec(
        num_scalar_prefetch=4,
        grid=(T // tm,),
        in_specs=[pl.BlockSpec((tm, D), lambda t, *_: (t, 0)),
                  pl.BlockSpec((meta_t.shape[0], tm), lambda t, *_: (0, t))],
        out_specs=pl.BlockSpec(memory_space=pl.ANY),
        scratch_shapes=[pltpu.VMEM((2, n_exp * DISPATCH_ROWS, D), BF16),
                        pltpu.VMEM((DISPATCH_ROWS, D), BF16),
                        pltpu.VMEM((DISPATCH_ROWS, D), BF16),
                        pltpu.SemaphoreType.DMA((2,)),
                        pltpu.SemaphoreType.DMA(())],
    )
    return pl.pallas_call(
        functools.partial(_dispatch_kernel, n_exp=n_exp),
        grid_spec=grid_spec,
        out_shape=jax.ShapeDtypeStruct((n_rows, D), BF16),
        compiler_params=_params(("arbitrary",)),
        name="moe_dispatch",
    )(base_te, cnt_te, pad_lo, pad_n, u, meta_t)


TILE_FULL, TILE_HALF, TILE_EMPTY = 0, 1, 2


def _expert_kernel(te_ref, na_ref, fill_ref, xs_ref, wg_ref, wu_ref, wd_ref, o_ref, acc_ref):
    i = pl.program_id(0)
    j = pl.program_id(1)
    last = pl.num_programs(1) - 1
    active = i < na_ref[0]
    fill = fill_ref[i]
    tm = acc_ref.shape[0]

    @pl.when(active & (j == 0))
    def _():
        acc_ref[...] = jnp.zeros_like(acc_ref)

    def swiglu(rows):
        xb = xs_ref[rows, :]
        g = jnp.dot(xb, wg_ref[...].astype(BF16), preferred_element_type=F32)
        up = jnp.dot(xb, wu_ref[...].astype(BF16), preferred_element_type=F32)
        h = (g * _sigmoid(g) * up).astype(BF16)
        acc_ref[rows, :] += jnp.dot(h, wd_ref[...].astype(BF16), preferred_element_type=F32)

    pl.when(active & (fill == TILE_FULL))(functools.partial(swiglu, pl.ds(0, tm)))
    pl.when(active & (fill == TILE_HALF))(functools.partial(swiglu, pl.ds(0, tm // 2)))

    @pl.when(active & (j == last))
    def _():
        o_ref[...] = acc_ref[...].astype(BF16)

    @pl.when(jnp.logical_not(active) & (j == last))
    def _():
        o_ref[...] = jnp.zeros_like(o_ref)


def _expert_ffn(xs, tile_expert, n_active, tile_fill, wg, wu, wd, *, tm, tf):
    n_rows, D = xs.shape
    ff = wg.shape[2]
    assert n_rows % tm == 0 and ff % tf == 0
    nj = ff // tf

    def jj(i, j, na):
        return jnp.where(i < na[0], j, nj - 1)

    def ii(i, na):
        return jnp.minimum(i, na[0] - 1)

    grid_spec = pltpu.PrefetchScalarGridSpec(
        num_scalar_prefetch=3,
        grid=(n_rows // tm, nj),
        in_specs=[pl.BlockSpec((tm, D), lambda i, j, te, na, tf_: (ii(i, na), 0)),
                  pl.BlockSpec((None, D, tf), lambda i, j, te, na, tf_: (te[i], 0, jj(i, j, na))),
                  pl.BlockSpec((None, D, tf), lambda i, j, te, na, tf_: (te[i], 0, jj(i, j, na))),
                  pl.BlockSpec((None, tf, D), lambda i, j, te, na, tf_: (te[i], jj(i, j, na), 0))],
        out_specs=pl.BlockSpec((tm, D), lambda i, j, te, na, tf_: (i, 0)),
        scratch_shapes=[pltpu.VMEM((tm, D), F32)],
    )
    return pl.pallas_call(
        _expert_kernel,
        grid_spec=grid_spec,
        out_shape=jax.ShapeDtypeStruct((n_rows, D), BF16),
        compiler_params=_params(("arbitrary", "arbitrary")),
        name="moe_experts",
    )(tile_expert, n_active, tile_fill, xs, wg, wu, wd)


def _combine_kernel(base_ref, off_ref, cnt_ref, ys_hbm, meta_ref, x_ref, gate_ref, lg_ref, lb_ref, o_ref,
                    stage, extra, acc_ref, sems, extra_sem, *, alpha, n_exp):
    t = pl.program_id(0) * pl.num_programs(1) + pl.program_id(1)
    n_steps = pl.num_programs(0) * pl.num_programs(1)
    tm = x_ref.shape[0]
    R = DISPATCH_ROWS
    slot = t % 2

    def main_copies(step, buf, fn):
        for e in range(n_exp):
            fn(pltpu.make_async_copy(_rows_at(ys_hbm, base_ref[step * n_exp + e], R),
                                     stage.at[buf, pl.ds(e * R, R)], sems.at[buf]))

    @pl.when(t == 0)
    def _():
        main_copies(0, 0, lambda cp: cp.start())

    @pl.when(t + 1 < n_steps)
    def _():
        main_copies(t + 1, 1 - slot, lambda cp: cp.start())

    meta = meta_ref[...]
    col = lambda k: meta[:, k:k + 1]
    e0, e1 = col(META_E0).astype(I32), col(META_E1).astype(I32)
    r0, r1 = col(META_R0).astype(I32), col(META_R1).astype(I32)
    w0, w1 = col(META_W0), col(META_W1)
    lane = lax.broadcasted_iota(I32, (tm, R), 1)

    rels, ws, picks = [], [], []
    for e in range(n_exp):
        rel = jnp.where(e0 == e, r0, jnp.where(e1 == e, r1, -1))
        rels.append(jnp.where(rel >= 0, rel - off_ref[t * n_exp + e], -1))
        ws.append(jnp.where(e0 == e, w0, jnp.where(e1 == e, w1, 0.0)))
        picks.append(jnp.where(lane == rels[e], ws[e], 0.0))
    weighted = jnp.concatenate(picks, axis=1).astype(BF16)
    main_copies(t, slot, lambda cp: cp.wait())
    acc_ref[...] = jnp.dot(weighted, stage[slot], preferred_element_type=F32)

    for e in range(n_exp):
        for blk in range(1, tm // R):
            @pl.when(cnt_ref[t * n_exp + e] > blk * R)
            def _(e=e, blk=blk):
                cp = pltpu.make_async_copy(_rows_at(ys_hbm, base_ref[t * n_exp + e] + blk * R, R),
                                           extra, extra_sem)
                cp.start()
                cp.wait()
                more = jnp.where(lane + blk * R == rels[e], ws[e], 0.0).astype(BF16)
                acc_ref[...] += jnp.dot(more, extra[...], preferred_element_type=F32)

    z = alpha * x_ref[...] + (1.0 + gate_ref[...]) * acc_ref[...]
    o_ref[...] = _layer_norm(z, lg_ref[...], lb_ref[...])


def _combine(ys, base_te, off_te, cnt_te, meta, x, gate, ln_g, ln_b, *, alpha, tm):
    B, S, D = x.shape
    nt = S // tm
    n_exp = base_te.shape[0] // (B * nt)
    row = pl.BlockSpec((None, tm, D), lambda b, i, *_: (b, i, 0))
    vec = pl.BlockSpec((1, D), lambda b, i, *_: (0, 0))
    grid_spec = pltpu.PrefetchScalarGridSpec(
        num_scalar_prefetch=3,
        grid=(B, nt),
        in_specs=[pl.BlockSpec(memory_space=pl.ANY),
                  pl.BlockSpec((None, tm, LANES), lambda b, i, *_: (b, i, 0)),
                  row, pl.BlockSpec((None, 1, D), lambda b, i, *_: (b, 0, 0)), vec, vec],
        out_specs=row,
        scratch_shapes=[pltpu.VMEM((2, n_exp * DISPATCH_ROWS, D), BF16),
                        pltpu.VMEM((DISPATCH_ROWS, D), BF16),
                        pltpu.VMEM((tm, D), F32),
                        pltpu.SemaphoreType.DMA((2,)),
                        pltpu.SemaphoreType.DMA(())],
    )
    return pl.pallas_call(
        functools.partial(_combine_kernel, alpha=alpha, n_exp=n_exp),
        grid_spec=grid_spec,
        out_shape=jax.ShapeDtypeStruct((B, S, D), F32),
        compiler_params=_params(("arbitrary", "arbitrary")),
        name="moe_combine",
    )(base_te, off_te, cnt_te, ys, meta, x, gate, ln_g, ln_b)


def _moe_ffn(x, scale, shift, gate, w_router, wg, wu, wd, ln_g, ln_b, *, alpha, tm, tm_e, tf):
    B, S, D = x.shape
    T = B * S
    nt = T // tm
    n_exp = w_router.shape[1]
    wr = jnp.zeros((D, LANES), F32).at[:, :n_exp].set(w_router)
    wr_hi = wr.astype(BF16)
    wr = jnp.stack([wr_hi, (wr - wr_hi.astype(F32)).astype(BF16)])
    u, meta, counts = _router(x, scale, shift, wr, n_exp=n_exp, tm=tm)

    meta2 = meta.reshape(T, LANES)
    e0 = meta2[:, META_E0].astype(I32)
    e1 = meta2[:, META_E1].astype(I32)
    r0 = meta2[:, META_R0].astype(I32)
    r1 = meta2[:, META_R1].astype(I32)
    experts = jnp.arange(n_exp, dtype=I32)
    chosen = ((e0[:, None] == experts) | (e1[:, None] == experts)).astype(I32)
    cnt_te = chosen.reshape(nt, tm, n_exp).sum(axis=1)
    seg_te = -(-cnt_te // SEGMENT_ALIGN) * SEGMENT_ALIGN
    off_te = jnp.cumsum(cnt_te, axis=0) - cnt_te
    filled = seg_te.sum(axis=0)
    sizes = -(-(filled + DISPATCH_SLACK) // tm_e) * tm_e
    ends = jnp.cumsum(sizes)
    starts = ends - sizes
    base_te = starts[None, :] + jnp.cumsum(seg_te, axis=0) - seg_te
    shift_te = (base_te - off_te)[:, None, :]

    def rows_of(e, r):
        pick = (e[:, None] == experts).astype(I32).reshape(nt, tm, n_exp)
        return r + (pick * shift_te).sum(axis=-1).reshape(T)

    d0, d1 = rows_of(e0, r0), rows_of(e1, r1)
    bound = TOP_K * T + n_exp * ((SEGMENT_ALIGN - 1) * nt + DISPATCH_SLACK)
    n_tiles = -(-bound // tm_e) + n_exp + 1
    n_active = (ends[-1] // tm_e).astype(I32)
    tile_id = jnp.minimum(jnp.arange(n_tiles, dtype=I32), n_active - 1)
    tile_expert = jnp.sum((ends[None, :] <= (tile_id * tm_e)[:, None]).astype(I32), axis=1)
    data_end = jnp.sum((tile_expert[:, None] == experts) * (starts + filled)[None, :], axis=1)
    left = data_end - tile_id * tm_e
    tile_fill = jnp.where(left <= 0, TILE_EMPTY, jnp.where(left <= tm_e // 2, TILE_HALF, TILE_FULL)).astype(I32)
    meta_t = jnp.stack([e0, e1, d0, d1] + [jnp.zeros_like(e0)] * 4)

    xs = _dispatch(u.reshape(T, D), meta_t, base_te.reshape(-1), cnt_te.reshape(-1),
                   jnp.concatenate([starts + filled, ends[-1:]]),
                   jnp.concatenate([sizes - filled, n_tiles * tm_e - ends[-1:]]),
                   n_rows=n_tiles * tm_e, tm=tm)
    ys = _expert_ffn(xs, tile_expert, n_active.reshape(1), tile_fill, wg, wu, wd, tm=tm_e, tf=tf)
    return _combine(ys, base_te.reshape(-1), off_te.reshape(-1), cnt_te.reshape(-1),
                    meta, x, gate, ln_g, ln_b, alpha=alpha, tm=tm)


def kernel(x, c, positions, w_in, w_out, attn_norm_gain, rec_norm_gain, rec_lb_logits, ada_w, ada_b,
           ln_gain, ln_bias, ffn_w_gate, ffn_w_up, ffn_w_down, moe_router, moe_w_gate, moe_w_up,
           moe_w_down):
    B, S, D = x.shape
    depth = w_in.shape[0]
    alpha = (2 * depth) ** 0.25

    p = jax.nn.softmax(rec_lb_logits.astype(F32), axis=0)
    cum = jnp.cumsum(p, axis=0)
    lb_all = cum - cum[0:1]
    half = ATTN_HEAD_DIM // 2
    inv_freq = ROPE_THETA ** (-jnp.arange(half, dtype=F32) / half)
    inv_freq_lanes = jnp.tile(inv_freq, LANES // half).reshape(1, LANES)
    pos_f = positions.astype(F32).reshape(B, S, 1)
    n_qk = 2 * ATTN_WIDTH
    w_in_b = jnp.concatenate([_qk_lane_order(w_in[..., :n_qk]), w_in[..., n_qk:]], axis=-1).astype(BF16)
    w_out_b = w_out.astype(BF16)
    ff = ffn_w_gate.shape[2]
    ffp = -(-ff // (2 * LANES)) * (2 * LANES)
    pad_c = lambda w: jnp.pad(w.astype(BF16), ((0, 0), (0, 0), (0, ffp - ff)))
    ffn_g, ffn_u = pad_c(ffn_w_gate), pad_c(ffn_w_up)
    ffn_d = jnp.pad(ffn_w_down.astype(BF16), ((0, 0), (0, ffp - ff), (0, 0)))

    mods = _ada_modulation(c, ada_w, ada_b)
    rope_cos, rope_sin = _rope_tables(pos_f, inv_freq_lanes, tm=min(1024, S))

    def mod(layer, sub):
        m = mods[layer * 2 + sub].reshape(B, 1, 3 * D)
        return m[..., :D], m[..., D:2 * D], m[..., 2 * D:]

    vec = lambda a: a.reshape(1, -1)
    for layer in range(depth):
        shift, scale, gate = mod(layer, 0)
        proj = _in_projection(x, scale, shift, rope_cos, rope_sin, w_in_b[layer], tm=min(512, S))
        attn = _dilated_attention(proj)
        rec = _hgrn2(proj, lb_all[layer], vec(rec_norm_gain[layer]))
        x = _out_projection(attn, rec, x, gate, vec(attn_norm_gain[layer]), w_out_b[layer],
                            vec(ln_gain[layer, 0]), vec(ln_bias[layer, 0]), alpha=alpha, tm=min(512, S))
        shift, scale, gate = mod(layer, 1)
        j = layer // 2
        if layer % 2 == 0:
            x = _dense_ffn(x, scale, shift, gate, ffn_g[j], ffn_u[j], ffn_d[j],
                           vec(ln_gain[layer, 1]), vec(ln_bias[layer, 1]),
                           alpha=alpha, tm=min(512, S), tc=2 * LANES)
        else:
            x = _moe_ffn(x, scale, shift, gate, moe_router[j], moe_w_gate[j], moe_w_up[j],
                         moe_w_down[j], vec(ln_gain[layer, 1]), vec(ln_bias[layer, 1]),
                         alpha=alpha, tm=min(512, S), tm_e=min(1024, B * S), tf=512)
    return x
```

```python
import functools
from typing import NamedTuple

import jax
import jax.numpy as jnp
from jax import lax
from jax.experimental import pallas as pl
from jax.experimental.pallas import tpu as pltpu

F32 = jnp.float32
BF16 = jnp.bfloat16
I32 = jnp.int32

LANES = 128
V7X_VMEM_LIMIT_BYTES = 56 * 1024 * 1024

ATTN_WIDTH = 512
ATTN_HEAD_DIM = 64
DILATIONS = (1, 4, 16)
ATTN_RADIUS = 64
ROPE_THETA = 10000.0
REC_WIDTH = 512
REC_DIM = 128
REC_CHUNK = 64
TOP_K = 2
LN_EPS = 1e-5
RMS_EPS = 1e-6
MASK_VALUE = -1e30
DECAY_EXP_CLAMP = 80.0
LOG2_E = 1.4426950408889634


def _params(semantics):
    return pltpu.CompilerParams(dimension_semantics=semantics,
                                vmem_limit_bytes=V7X_VMEM_LIMIT_BYTES)


def _sigmoid(x):
    return 1.0 / (1.0 + jnp.exp(-x))


def _layer_norm(z, gain, bias):
    mu = jnp.mean(z, axis=-1, keepdims=True)
    zc = z - mu
    var = jnp.mean(zc * zc, axis=-1, keepdims=True)
    return zc * lax.rsqrt(var + LN_EPS) * gain + bias


def _ada_kernel(c_ref, w_ref, b_ref, o_ref):
    c = c_ref[...]
    o_ref[...] = jnp.dot(c * _sigmoid(c), w_ref[...], precision=lax.Precision.HIGHEST,
                         preferred_element_type=F32) + b_ref[...]


def _ada_modulation(c, ada_w, ada_b):
    B, D = c.shape
    n = ada_w.shape[0] * ada_w.shape[1]
    n3 = ada_w.shape[-1]
    tn = D
    return pl.pallas_call(
        _ada_kernel,
        grid=(n, n3 // tn),
        in_specs=[pl.BlockSpec((B, D), lambda i, j: (0, 0)),
                  pl.BlockSpec((None, D, tn), lambda i, j: (i, 0, j)),
                  pl.BlockSpec((None, 1, tn), lambda i, j: (i, 0, j))],
        out_specs=pl.BlockSpec((None, None, B, tn), lambda i, j: (i, j, 0, 0)),
        out_shape=jax.ShapeDtypeStruct((n, n3 // tn, B, tn), F32),
        compiler_params=_params(("arbitrary", "arbitrary")),
        name="ada_modulation",
    )(c, ada_w.reshape(n, D, n3), ada_b.reshape(n, 1, n3))


ROPE_HALF = ATTN_HEAD_DIM // 2


def _qk_lane_order(w_cols):
    lead = w_cols.shape[:-1]
    w = w_cols.reshape(*lead, -1, 2, 2, ROPE_HALF)
    return jnp.swapaxes(w, -3, -2).reshape(*lead, -1)


def _rope_kernel(pos_ref, invf_ref, cos_ref, sin_ref):
    ang = pos_ref[...] * invf_ref[...]
    lane = lax.broadcasted_iota(I32, (1, LANES), 1)
    sin = jnp.sin(ang)
    cos_ref[...] = jnp.cos(ang)
    sin_ref[...] = jnp.where(lane < LANES // 2, -sin, sin)


def _rope_tables(pos_f, inv_freq_lanes, *, tm):
    B, S, _ = pos_f.shape
    tab = pl.BlockSpec((None, tm, LANES), lambda b, i: (b, i, 0))
    return pl.pallas_call(
        _rope_kernel,
        grid=(B, S // tm),
        in_specs=[pl.BlockSpec((None, tm, 1), lambda b, i: (b, i, 0)),
                  pl.BlockSpec((1, LANES), lambda b, i: (0, 0))],
        out_specs=[tab, tab],
        out_shape=[jax.ShapeDtypeStruct((B, S, LANES), F32)] * 2,
        compiler_params=_params(("arbitrary", "arbitrary")),
        name="rope_tables",
    )(pos_f, inv_freq_lanes)


def _inproj_kernel(x_ref, sc_ref, sh_ref, cos_ref, sin_ref, w_ref, o_ref, *, tn):
    u = (x_ref[...] * (1.0 + sc_ref[...]) + sh_ref[...]).astype(BF16)
    cos = cos_ref[...]
    sin_signed = sin_ref[...]
    q_scale = ATTN_HEAD_DIM ** -0.5
    for j in range(w_ref.shape[1] // tn):
        acc = jnp.dot(u, w_ref[:, j * tn:(j + 1) * tn], preferred_element_type=F32)
        if j > 0:
            o_ref[:, j * tn:(j + 1) * tn] = acc
            continue
        for kk in range(tn // LANES):
            c = acc[:, kk * LANES:(kk + 1) * LANES]
            r = c * cos + pltpu.roll(c, LANES // 2, 1) * sin_signed
            if kk * LANES < ATTN_WIDTH:
                r = r * q_scale
            o_ref[:, kk * LANES:(kk + 1) * LANES] = r


def _in_projection(x, scale, shift, rope_cos, rope_sin, w_in_bf16, *, tm):
    B, S, D = x.shape
    n_cols = w_in_bf16.shape[1]
    tn = 2 * ATTN_WIDTH
    assert S % tm == 0 and n_cols % tn == 0
    return pl.pallas_call(
        functools.partial(_inproj_kernel, tn=tn),
        grid=(B, S // tm),
        in_specs=[pl.BlockSpec((None, tm, D), lambda b, i: (b, i, 0)),
                  pl.BlockSpec((None, 1, D), lambda b, i: (b, 0, 0)),
                  pl.BlockSpec((None, 1, D), lambda b, i: (b, 0, 0)),
                  pl.BlockSpec((None, tm, LANES), lambda b, i: (b, i, 0)),
                  pl.BlockSpec((None, tm, LANES), lambda b, i: (b, i, 0)),
                  pl.BlockSpec((D, n_cols), lambda b, i: (0, 0))],
        out_specs=pl.BlockSpec((None, tm, n_cols), lambda b, i: (b, i, 0)),
        out_shape=jax.ShapeDtypeStruct((B, S, n_cols), F32),
        compiler_params=_params(("arbitrary", "arbitrary")),
        name="in_projection",
    )(x, scale, shift, rope_cos, rope_sin, w_in_bf16)


ATTN_BLOCK_GROUP = 16


def _attn_kernel(q_ref, k_ref, v_ref, o_ref, qs, ks, vs, qf, kf, vf, pm, plr, pa, nm, nl, na, bias_s,
                 *, S):
    lane = lax.broadcasted_iota(I32, (1, LANES), 1)
    head0 = lane < ATTN_HEAD_DIM
    head0_qk = (lane % ATTN_HEAD_DIM) < ROPE_HALF

    for p, d in enumerate(DILATIONS):
        L = S // d
        tq = min(128, L)
        W = min(2 * tq, L)
        nb = L // tq

        d_prev = DILATIONS[p - 1] if p else 1
        ratio, l_prev = d // d_prev, S // d_prev
        keep_f32 = 0 < p < len(DILATIONS) - 1
        for src, stage, dst in ((q_ref, qf, qs), (k_ref, kf, ks), (v_ref, vf, vs)):
            src = src if p <= 1 else stage
            for r in range(d):
                a, r_prev = divmod(r, d_prev)
                rows = pl.ds(r_prev * l_prev + a, L, stride=ratio) if ratio > 1 else pl.ds(0, L)
                val = src[rows, :]
                if keep_f32:
                    stage[r * L:(r + 1) * L, :] = val
                dst[r * L:(r + 1) * L, :] = val.astype(BF16)

        n_blocks = d * nb
        rc = lax.broadcasted_iota(I32, (tq, W), 0) - lax.broadcasted_iota(I32, (tq, W), 1)
        for which, delta in enumerate((0, ATTN_RADIUS, tq)):
            bias_s[which, pl.ds(0, tq), pl.ds(0, W)] = jnp.where(
                jnp.abs(rc + delta) <= ATTN_RADIUS, 0.0, MASK_VALUE)
        group = min(ATTN_BLOCK_GROUP, n_blocks)
        assert n_blocks % group == 0

        def block_group(gi, carry, L=L, tq=tq, W=W, nb=nb, group=group):
            q0s, k0s, scores, probs, outs, sums = [], [], [], [], [], []
            for u in range(group):
                g = gi * group + u
                r = g // nb
                n = g - r * nb
                base = r * L
                q0 = pl.multiple_of(base + n * tq, tq)
                ws = jnp.clip(n * tq - ATTN_RADIUS, 0, L - W)
                k0 = pl.multiple_of(base + ws, 16)
                qb = qs[pl.ds(q0, tq), :]
                zero = jnp.zeros_like(qb)
                q2 = jnp.concatenate([jnp.where(head0_qk, qb, zero), jnp.where(head0_qk, zero, qb)], axis=0)
                s = lax.dot_general(q2, ks[pl.ds(k0, W), :], (((1,), (1,)), ((), ())),
                                    preferred_element_type=F32)
                delta = n * tq - ws
                which = jnp.where(delta == 0, 0, jnp.where(delta == ATTN_RADIUS, 1, 2))
                bias = bias_s[which, pl.ds(0, tq), pl.ds(0, W)]
                scores.append(s + jnp.concatenate([bias, bias], axis=0))
                q0s.append(q0)
                k0s.append(k0)
            for u in range(group):
                s = scores[u]
                m = jnp.max(s, axis=1, keepdims=True)
                e = jnp.exp(s - m)
                probs.append((m, e.astype(BF16)))
                sums.append(jnp.sum(e, axis=1, keepdims=True))
            for u in range(group):
                outs.append(jnp.dot(probs[u][1], vs[pl.ds(k0s[u], W), :], preferred_element_type=F32))
            for u in range(group):
                m, l = probs[u][0], sums[u]
                rows = pl.ds(q0s[u], tq)
                out_m[rows, :] = jnp.where(head0, m[:tq], m[tq:])
                out_l[rows, :] = jnp.where(head0, l[:tq], l[tq:])
                out_a[rows, :] = jnp.where(head0, outs[u][:tq], outs[u][tq:])
            return carry

        out_m, out_l, out_a = (nm.at[p], nl.at[p], na.at[p]) if d == 1 else (pm, plr, pa)
        lax.fori_loop(0, n_blocks // group, block_group, 0)

        for r in range(d if d > 1 else 0):
            rows = pl.ds(r, L, stride=d)
            nm[p, rows, :] = pm[r * L:(r + 1) * L, :]
            nl[p, rows, :] = plr[r * L:(r + 1) * L, :]
            na[p, rows, :] = pa[r * L:(r + 1) * L, :]

    m_all = jnp.maximum(jnp.maximum(nm[0], nm[1]), nm[2])
    num = jnp.zeros((S, LANES), F32)
    den = jnp.zeros((S, LANES), F32)
    for p in range(len(DILATIONS)):
        w = jnp.exp(nm[p] - m_all)
        num = num + w * na[p]
        den = den + w * nl[p]
    o_ref[...] = (num / den).astype(o_ref.dtype)


def _dilated_attention(proj):
    B, S, _ = proj.shape
    n_pairs = ATTN_WIDTH // LANES
    assert S % (16 * DILATIONS[-1]) == 0
    blk = lambda off: pl.BlockSpec((None, S, LANES), lambda b, h: (b, 0, off + h))
    return pl.pallas_call(
        functools.partial(_attn_kernel, S=S),
        grid=(B, n_pairs),
        in_specs=[blk(0), blk(n_pairs), blk(2 * n_pairs)],
        out_specs=pl.BlockSpec((None, S, LANES), lambda b, h: (b, 0, h)),
        out_shape=jax.ShapeDtypeStruct((B, S, ATTN_WIDTH), BF16),
        scratch_shapes=[pltpu.VMEM((S, LANES), BF16)] * 3
                       + [pltpu.VMEM((S, LANES), F32)] * 6
                       + [pltpu.VMEM((len(DILATIONS), S, LANES), F32)] * 3
                       + [pltpu.VMEM((3, min(128, S), min(256, S)), F32)],
        compiler_params=_params(("arbitrary", "arbitrary")),
        name="dilated_attention",
    )(proj, proj, proj)


REC_GROUP_CHUNKS = 4
REC_GROUP_ROWS = REC_GROUP_CHUNKS * REC_CHUNK
REC_STAGE_GROUPS = 8


def _hgrn_kernel(rq_ref, zf_ref, zb_ref, ri_ref, rg_ref, lb_ref, gain_ref, o_ref,
                 q_s, vt_s, oi_s, qp_s, ut_s, dd_s, st_s, os_s, *, S):
    C, G, GR = REC_CHUNK, REC_GROUP_CHUNKS, REC_GROUP_ROWS
    nc, ng = S // C, S // GR
    sg = min(REC_STAGE_GROUPS, ng)
    assert ng % sg == 0

    row = lax.broadcasted_iota(I32, (GR, GR), 0)
    col = lax.broadcasted_iota(I32, (GR, GR), 1)
    same_chunk = (row // C) == (col // C)
    t_i = lax.broadcasted_iota(I32, (C, C), 0)
    s_i = lax.broadcasted_iota(I32, (C, C), 1)

    def prep(i, carry):
        r0 = pl.multiple_of(i * GR, GR)
        rq = rq_ref[pl.ds(r0, GR), :]
        q_s[pl.ds(r0, GR), :] = rq * _sigmoid(rq)
        v = ri_ref[pl.ds(r0, GR), :]
        for c in range(G):
            vt_s[i * G + c] = v[c * C:(c + 1) * C, :].T.astype(BF16)
        return carry

    lax.fori_loop(0, ng, prep, 0, unroll=True)

    for direction in range(2):
        fwd = direction == 0
        z_ref = zf_ref if fwd else zb_ref
        cum = jnp.where(same_chunk & ((col <= row) if fwd else (col >= row)), 1.0, 0.0).astype(BF16)
        keep = (s_i <= t_i) if fwd else (s_i >= t_i)
        mid_row = C // 2 - 1 if fwd else C // 2
        last_row = C - 1 if fwd else 0

        def phase_a(it, carry, fwd=fwd, z_ref=z_ref, cum=cum, keep=keep,
                    mid_row=mid_row, last_row=last_row, direction=direction):
            lb = lb_ref[direction:direction + 1, :]
            log_lb = jnp.log(lb)
            log_1m_lb = jnp.log1p(-lb)
            r0s, c0s, k3s, bcs, atts, stage3 = [], [], [], [], [], []
            for u in range(sg):
                i = it * sg + u
                r0 = pl.multiple_of(i * GR, GR)
                z = z_ref[pl.ds(r0, GR), :]
                e = jnp.exp(-jnp.abs(z))
                t = log_1m_lb + (jnp.minimum(z, 0.0) - jnp.log(1.0 + e))
                g = jnp.maximum(log_lb, t) + jnp.log(1.0 + jnp.exp(-jnp.abs(log_lb - t)))
                kk = (1.0 - lb) * (jnp.where(z > 0, e, 1.0) / (1.0 + e))
                g = g * LOG2_E
                g1 = g.astype(BF16)
                r1 = g - g1.astype(F32)
                g2 = r1.astype(BF16)
                g3 = (r1 - g2.astype(F32)).astype(BF16)
                bcs.append(jnp.dot(cum, g1, preferred_element_type=F32)
                           + jnp.dot(cum, g2, preferred_element_type=F32)
                           + jnp.dot(cum, g3, preferred_element_type=F32))
                k3s.append(kk.reshape(G, C, REC_DIM))
                r0s.append(r0)
                c0s.append(pl.multiple_of(i * G, G))
            for u in range(sg):
                b3 = bcs[u].reshape(G, C, REC_DIM)
                mid = b3[:, mid_row:mid_row + 1, :]
                last = b3[:, last_row:last_row + 1, :]
                e3 = b3 - mid
                q3 = q_s[pl.ds(r0s[u], GR), :].reshape(G, C, REC_DIM)
                clamp = DECAY_EXP_CLAMP * LOG2_E
                qt = (q3 * jnp.exp2(jnp.minimum(e3, clamp))).astype(BF16)
                kt = (k3s[u] * jnp.exp2(jnp.minimum(-e3, clamp))).astype(BF16)
                atts.append(jnp.einsum('gtc,gsc->gts', qt, kt, preferred_element_type=F32))
                qp_s[pl.ds(r0s[u], GR), :] = (q3 * jnp.exp2(b3)).astype(BF16).reshape(GR, REC_DIM)
                stage3.append(((k3s[u] * jnp.exp2(last - b3)).astype(BF16), jnp.exp2(last)))
            for u in range(sg):
                att = jnp.where(keep[None], atts[u], 0.0).astype(BF16)
                v3 = ri_ref[pl.ds(r0s[u], GR), :].reshape(G, C, REC_DIM).astype(BF16)
                oi = jnp.einsum('gts,gsv->gtv', att, v3, preferred_element_type=F32)
                oi_s[pl.ds(r0s[u], GR), :] = oi.reshape(GR, REC_DIM)
                kp, decay = stage3[u]
                ut_s[pl.ds(c0s[u], G)] = jnp.einsum('gvs,gsc->gvc', vt_s[pl.ds(c0s[u], G)], kp,
                                                    preferred_element_type=F32)
                dd_s[pl.ds(c0s[u], G)] = decay
            return carry

        lax.fori_loop(0, ng // sg, phase_a, 0)

        def scan(c, st, fwd=fwd):
            idx = c if fwd else nc - 1 - c
            st_s[idx] = st.astype(BF16)
            return dd_s[idx] * st + ut_s[idx]

        lax.fori_loop(0, nc, scan, jnp.zeros((REC_DIM, REC_DIM), F32), unroll=True)

        def phase_c(it, carry, fwd=fwd):
            r0s = [pl.multiple_of((it * sg + u) * GR, GR) for u in range(sg)]
            oos = []
            for u in range(sg):
                c0 = pl.multiple_of((it * sg + u) * G, G)
                qp3 = qp_s[pl.ds(r0s[u], GR), :].reshape(G, C, REC_DIM)
                oos.append(jnp.einsum('gtc,gvc->gtv', qp3, st_s[pl.ds(c0, G)],
                                      preferred_element_type=F32))
            for u in range(sg):
                rows = pl.ds(r0s[u], GR)
                tot = oi_s[rows, :] + oos[u].reshape(GR, REC_DIM)
                os_s[rows, :] = tot if fwd else os_s[rows, :] + tot
            return carry

        lax.fori_loop(0, ng // sg, phase_c, 0)

    o = os_s[...]
    o = o * lax.rsqrt(jnp.mean(o * o, axis=-1, keepdims=True) + RMS_EPS) * gain_ref[...]
    o_ref[...] = (o * _sigmoid(rg_ref[...])).astype(o_ref.dtype)


def _hgrn2(proj, lb, rec_gain):
    B, S, _ = proj.shape
    nh = REC_WIDTH // REC_DIM
    assert S % REC_GROUP_ROWS == 0
    col0 = 3 * ATTN_WIDTH // LANES
    blk = lambda k: pl.BlockSpec((None, S, REC_DIM), lambda b, h: (b, 0, col0 + k * nh + h))
    nc = S // REC_CHUNK
    return pl.pallas_call(
        functools.partial(_hgrn_kernel, S=S),
        grid=(B, nh),
        in_specs=[blk(0), blk(1), blk(2), blk(3), blk(4),
                  pl.BlockSpec((2, REC_DIM), lambda b, h: (0, h)),
                  pl.BlockSpec((1, REC_DIM), lambda b, h: (0, h))],
        out_specs=pl.BlockSpec((None, S, REC_DIM), lambda b, h: (b, 0, h)),
        out_shape=jax.ShapeDtypeStruct((B, S, REC_WIDTH), BF16),
        scratch_shapes=[pltpu.VMEM((S, REC_DIM), F32),
                        pltpu.VMEM((nc, REC_DIM, REC_CHUNK), BF16),
                        pltpu.VMEM((S, REC_DIM), F32),
                        pltpu.VMEM((S, REC_DIM), BF16),
                        pltpu.VMEM((nc, REC_DIM, REC_DIM), F32),
                        pltpu.VMEM((nc, 1, REC_DIM), F32),
                        pltpu.VMEM((nc, REC_DIM, REC_DIM), BF16),
                        pltpu.VMEM((S, REC_DIM), F32)],
        compiler_params=_params(("arbitrary", "arbitrary")),
        name="hgrn2",
    )(proj, proj, proj, proj, proj, lb, rec_gain)


def _outproj_kernel(a_ref, r_ref, x_ref, gate_ref, ag_ref, w_ref, lg_ref, lb_ref, o_ref, *, alpha):
    a = a_ref[...].astype(F32)
    an = a * lax.rsqrt(jnp.mean(a * a, axis=-1, keepdims=True) + RMS_EPS) * ag_ref[...]
    y = (jnp.dot(an.astype(BF16), w_ref[:ATTN_WIDTH, :], preferred_element_type=F32)
         + jnp.dot(r_ref[...], w_ref[ATTN_WIDTH:, :], preferred_element_type=F32))
    z = alpha * x_ref[...] + (1.0 + gate_ref[...]) * y
    o_ref[...] = _layer_norm(z, lg_ref[...], lb_ref[...])


def _out_projection(attn, rec, x, gate, attn_gain, w_out_bf16, ln_g, ln_b, *, alpha, tm):
    B, S, D = x.shape
    row = lambda w: pl.BlockSpec((None, tm, w), lambda b, i: (b, i, 0))
    vec = lambda w: pl.BlockSpec((1, w), lambda b, i: (0, 0))
    return pl.pallas_call(
        functools.partial(_outproj_kernel, alpha=alpha),
        grid=(B, S // tm),
        in_specs=[row(ATTN_WIDTH), row(REC_WIDTH), row(D),
                  pl.BlockSpec((None, 1, D), lambda b, i: (b, 0, 0)),
                  vec(ATTN_WIDTH),
                  pl.BlockSpec((ATTN_WIDTH + REC_WIDTH, D), lambda b, i: (0, 0)),
                  vec(D), vec(D)],
        out_specs=row(D),
        out_shape=jax.ShapeDtypeStruct((B, S, D), F32),
        compiler_params=_params(("arbitrary", "arbitrary")),
        name="out_projection",
    )(attn, rec, x, gate, attn_gain, w_out_bf16, ln_g, ln_b)


def _ffn_kernel(x_ref, sc_ref, sh_ref, gate_ref, wg_ref, wu_ref, wd_ref, lg_ref, lb_ref, o_ref,
                *, alpha, tc):
    x = x_ref[...]
    u = (x * (1.0 + sc_ref[...]) + sh_ref[...]).astype(BF16)
    acc = jnp.zeros(x.shape, F32)
    for c in range(wg_ref.shape[1] // tc):
        cols = slice(c * tc, (c + 1) * tc)
        g = jnp.dot(u, wg_ref[:, cols], preferred_element_type=F32)
        up = jnp.dot(u, wu_ref[:, cols], preferred_element_type=F32)
        h = (g * _sigmoid(g) * up).astype(BF16)
        acc = acc + jnp.dot(h, wd_ref[cols, :], preferred_element_type=F32)
    z = alpha * x + (1.0 + gate_ref[...]) * acc
    o_ref[...] = _layer_norm(z, lg_ref[...], lb_ref[...])


def _dense_ffn(x, scale, shift, gate, wg, wu, wd, ln_g, ln_b, *, alpha, tm, tc):
    B, S, D = x.shape
    ffp = wg.shape[1]
    assert ffp % tc == 0
    row = pl.BlockSpec((None, tm, D), lambda b, i: (b, i, 0))
    mod = pl.BlockSpec((None, 1, D), lambda b, i: (b, 0, 0))
    vec = pl.BlockSpec((1, D), lambda b, i: (0, 0))
    return pl.pallas_call(
        functools.partial(_ffn_kernel, alpha=alpha, tc=tc),
        grid=(B, S // tm),
        in_specs=[row, mod, mod, mod,
                  pl.BlockSpec((D, ffp), lambda b, i: (0, 0)),
                  pl.BlockSpec((D, ffp), lambda b, i: (0, 0)),
                  pl.BlockSpec((ffp, D), lambda b, i: (0, 0)),
                  vec, vec],
        out_specs=row,
        out_shape=jax.ShapeDtypeStruct((B, S, D), F32),
        compiler_params=_params(("arbitrary", "arbitrary")),
        name="dense_ffn",
    )(x, scale, shift, gate, wg, wu, wd, ln_g, ln_b)


META_E0, META_E1, META_W0, META_W1, META_R0, META_R1 = range(6)
SEGMENT_ALIGN = 16
DISPATCH_ROWS = 256
DISPATCH_HALF = DISPATCH_ROWS // 2
DISPATCH_SLACK = DISPATCH_HALF


def _router_kernel(x_ref, sc_ref, sh_ref, wr_ref, u_ref, meta_ref, carry_ref, *, n_exp):
    tm = x_ref.shape[0]

    @pl.when((pl.program_id(0) == 0) & (pl.program_id(1) == 0))
    def _():
        carry_ref[...] = jnp.zeros_like(carry_ref)

    u = x_ref[...] * (1.0 + sc_ref[...]) + sh_ref[...]
    u_ref[...] = u.astype(BF16)
    u_hi = u.astype(BF16)
    u_lo = (u - u_hi.astype(F32)).astype(BF16)
    logits = (jnp.dot(u_hi, wr_ref[0], preferred_element_type=F32)
              + jnp.dot(u_lo, wr_ref[0], preferred_element_type=F32)
              + jnp.dot(u_hi, wr_ref[1], preferred_element_type=F32))
    lane = lax.broadcasted_iota(I32, (tm, LANES), 1)
    neg = -jnp.inf
    l1 = jnp.where(lane < n_exp, logits, neg)
    m1 = jnp.max(l1, axis=1, keepdims=True)
    i1 = jnp.min(jnp.where(l1 == m1, lane, LANES), axis=1, keepdims=True)
    l2 = jnp.where(lane == i1, neg, l1)
    m2 = jnp.max(l2, axis=1, keepdims=True)
    i2 = jnp.min(jnp.where(l2 == m2, lane, LANES), axis=1, keepdims=True)
    e = jnp.exp(m2 - m1)
    w1 = 1.0 / (1.0 + e)
    w2 = e * w1
    sel = jnp.where((lane == i1) | (lane == i2), 1.0, 0.0)
    before = (lax.broadcasted_iota(I32, (tm, tm), 1) < lax.broadcasted_iota(I32, (tm, tm), 0))
    ranks = jnp.dot(jnp.where(before, 1.0, 0.0).astype(BF16), sel.astype(BF16),
                    preferred_element_type=F32) + carry_ref[...]
    r1 = jnp.sum(jnp.where(lane == i1, ranks, 0.0), axis=1, keepdims=True)
    r2 = jnp.sum(jnp.where(lane == i2, ranks, 0.0), axis=1, keepdims=True)
    carry_ref[...] = carry_ref[...] + jnp.sum(sel, axis=0, keepdims=True)
    meta = jnp.zeros((tm, LANES), F32)
    for k, val in ((META_E0, i1.astype(F32)), (META_E1, i2.astype(F32)), (META_W0, w1),
                   (META_W1, w2), (META_R0, r1), (META_R1, r2)):
        meta = jnp.where(lane == k, val, meta)
    meta_ref[...] = meta


def _router(x, scale, shift, w_router_lanes, *, n_exp, tm):
    B, S, D = x.shape
    nt = S // tm
    row = pl.BlockSpec((None, tm, D), lambda b, i: (b, i, 0))
    mod = pl.BlockSpec((None, 1, D), lambda b, i: (b, 0, 0))
    return pl.pallas_call(
        functools.partial(_router_kernel, n_exp=n_exp),
        grid=(B, nt),
        in_specs=[row, mod, mod, pl.BlockSpec((2, D, LANES), lambda b, i: (0, 0, 0))],
        out_specs=[row, pl.BlockSpec((None, tm, LANES), lambda b, i: (b, i, 0))],
        out_shape=[jax.ShapeDtypeStruct((B, S, D), BF16),
                   jax.ShapeDtypeStruct((B, S, LANES), F32)],
        scratch_shapes=[pltpu.VMEM((1, LANES), F32)],
        compiler_params=_params(("arbitrary", "arbitrary")),
        name="moe_router",
    )(x, scale, shift, w_router_lanes)


def _rows_at(hbm, first_row, n_rows):
    return hbm.at[pl.ds(pl.multiple_of(first_row, SEGMENT_ALIGN), n_rows)]


def _dispatch_kernel(base_ref, cnt_ref, padlo_ref, padn_ref, u_ref, meta_ref, xs_hbm,
                     stage, extra, zeros_buf, sems, extra_sem, *, n_exp):
    t = pl.program_id(0)
    last = pl.num_programs(0) - 1
    tm = u_ref.shape[0]
    R, H = DISPATCH_ROWS, DISPATCH_HALF
    slot = t % 2
    u = u_ref[...]
    e0, e1 = meta_ref[0:1, :], meta_ref[1:2, :]
    d0, d1 = meta_ref[2:3, :], meta_ref[3:4, :]
    srow = lax.broadcasted_iota(I32, (R, tm), 0)

    def main_copies(step, buf, fn):
        for e in range(n_exp):
            base = base_ref[step * n_exp + e]
            fn(pltpu.make_async_copy(stage.at[buf, pl.ds(e * R, H)], _rows_at(xs_hbm, base, H),
                                     sems.at[buf]))

            @pl.when(cnt_ref[step * n_exp + e] > H)
            def _(e=e, base=base):
                fn(pltpu.make_async_copy(stage.at[buf, pl.ds(e * R + H, H)],
                                         _rows_at(xs_hbm, base + H, H), sems.at[buf]))

    rels = []
    for e in range(n_exp):
        rel = jnp.where(e0 == e, d0, jnp.where(e1 == e, d1, -1))
        rels.append(jnp.where(rel >= 0, rel - base_ref[t * n_exp + e], -1))
    onehot = jnp.concatenate([jnp.where(srow == rel, 1.0, 0.0) for rel in rels], axis=0).astype(BF16)
    stage[slot] = jnp.dot(onehot, u, preferred_element_type=F32).astype(BF16)

    @pl.when(t > 0)
    def _():
        main_copies(t - 1, 1 - slot, lambda cp: cp.wait())
    main_copies(t, slot, lambda cp: cp.start())

    for e in range(n_exp):
        for blk in range(1, tm // R):
            @pl.when(cnt_ref[t * n_exp + e] > blk * R)
            def _(e=e, blk=blk):
                more = jnp.where(srow + blk * R == rels[e], 1.0, 0.0).astype(BF16)
                extra[...] = jnp.dot(more, u, preferred_element_type=F32).astype(BF16)
                cp = pltpu.make_async_copy(extra, _rows_at(xs_hbm, base_ref[t * n_exp + e] + blk * R, R),
                                           extra_sem)
                cp.start()
                cp.wait()

    @pl.when(t == last)
    def _():
        main_copies(t, slot, lambda cp: cp.wait())
        zeros_buf[...] = jnp.zeros_like(zeros_buf)
        pieces = [SEGMENT_ALIGN << b for b in range((R // SEGMENT_ALIGN).bit_length() - 1)]

        def pad_copies(fn):
            for p in range(padlo_ref.shape[0]):
                lo, n = padlo_ref[p], padn_ref[p]
                whole = n // R

                def body(k, carry, lo=lo):
                    fn(pltpu.make_async_copy(zeros_buf, _rows_at(xs_hbm, lo + k * R, R), extra_sem))
                    return carry
                lax.fori_loop(0, whole, body, 0)
                rem = n - whole * R
                for sz in pieces:
                    off = lo + whole * R + (rem // (2 * sz)) * (2 * sz)
                    pl.when((rem // sz) % 2 == 1)(functools.partial(
                        lambda off, sz: fn(pltpu.make_async_copy(
                            zeros_buf.at[pl.ds(0, sz)], _rows_at(xs_hbm, off, sz), extra_sem)), off, sz))

        pad_copies(lambda cp: cp.start())
        pad_copies(lambda cp: cp.wait())


def _dispatch(u, meta_t, base_te, cnt_te, pad_lo, pad_n, *, n_rows, tm):
    T, D = u.shape
    n_exp = pad_lo.shape[0] - 1
    assert tm % DISPATCH_ROWS == 0
    grid_spec = pltpu.PrefetchScalarGridSpec(
        num_scalar_prefetch=4,
        grid=(T // tm,),
        in_specs=[pl.BlockSpec((tm, D), lambda t, *_: (t, 0)),
                  pl.BlockSpec((meta_t.shape[0], tm), lambda t, *_: (0, t))],
        out_specs=pl.BlockSpec(memory_space=pl.ANY),
        scratch_shapes=[pltpu.VMEM((2, n_exp * DISPATCH_ROWS, D), BF16),
                        pltpu.VMEM((DISPATCH_ROWS, D), BF16),
                        pltpu.VMEM((DISPATCH_ROWS, D), BF16),
                        pltpu.SemaphoreType.DMA((2,)),
                        pltpu.SemaphoreType.DMA(())],
    )
    return pl.pallas_call(
        functools.partial(_dispatch_kernel, n_exp=n_exp),
        grid_spec=grid_spec,
        out_shape=jax.ShapeDtypeStruct((n_rows, D), BF16),
        compiler_params=_params(("arbitrary",)),
        name="moe_dispatch",
    )(base_te, cnt_te, pad_lo, pad_n, u, meta_t)


TILE_FULL, TILE_HALF, TILE_EMPTY = 0, 1, 2


def _expert_kernel(te_ref, na_ref, fill_ref, xs_ref, wg_ref, wu_ref, wd_ref, o_ref, acc_ref):
    i = pl.program_id(0)
    j = pl.program_id(1)
    last = pl.num_programs(1) - 1
    active = i < na_ref[0]
    fill = fill_ref[i]
    tm = acc_ref.shape[0]

    @pl.when(active & (j == 0))
    def _():
        acc_ref[...] = jnp.zeros_like(acc_ref)

    def swiglu(rows):
        xb = xs_ref[rows, :]
        g = jnp.dot(xb, wg_ref[...].astype(BF16), preferred_element_type=F32)
        up = jnp.dot(xb, wu_ref[...].astype(BF16), preferred_element_type=F32)
        h = (g * _sigmoid(g) * up).astype(BF16)
        acc_ref[rows, :] += jnp.dot(h, wd_ref[...].astype(BF16), preferred_element_type=F32)

    pl.when(active & (fill == TILE_FULL))(functools.partial(swiglu, pl.ds(0, tm)))
    pl.when(active & (fill == TILE_HALF))(functools.partial(swiglu, pl.ds(0, tm // 2)))

    @pl.when(active & (j == last))
    def _():
        o_ref[...] = acc_ref[...].astype(BF16)

    @pl.when(jnp.logical_not(active) & (j == last))
    def _():
        o_ref[...] = jnp.zeros_like(o_ref)


def _expert_ffn(xs, tile_expert, n_active, tile_fill, wg, wu, wd, *, tm, tf):
    n_rows, D = xs.shape
    ff = wg.shape[2]
    assert n_rows % tm == 0 and ff % tf == 0
    nj = ff // tf

    def jj(i, j, na):
        return jnp.where(i < na[0], j, nj - 1)

    def ii(i, na):
        return jnp.minimum(i, na[0] - 1)

    grid_spec = pltpu.PrefetchScalarGridSpec(
        num_scalar_prefetch=3,
        grid=(n_rows // tm, nj),
        in_specs=[pl.BlockSpec((tm, D), lambda i, j, te, na, tf_: (ii(i, na), 0)),
                  pl.BlockSpec((None, D, tf), lambda i, j, te, na, tf_: (te[i], 0, jj(i, j, na))),
                  pl.BlockSpec((None, D, tf), lambda i, j, te, na, tf_: (te[i], 0, jj(i, j, na))),
                  pl.BlockSpec((None, tf, D), lambda i, j, te, na, tf_: (te[i], jj(i, j, na), 0))],
        out_specs=pl.BlockSpec((tm, D), lambda i, j, te, na, tf_: (i, 0)),
        scratch_shapes=[pltpu.VMEM((tm, D), F32)],
    )
    return pl.pallas_call(
        _expert_kernel,
        grid_spec=grid_spec,
        out_shape=jax.ShapeDtypeStruct((n_rows, D), BF16),
        compiler_params=_params(("arbitrary", "arbitrary")),
        name="moe_experts",
    )(tile_expert, n_active, tile_fill, xs, wg, wu, wd)


def _combine_kernel(base_ref, off_ref, cnt_ref, ys_hbm, meta_ref, x_ref, gate_ref, lg_ref, lb_ref, o_ref,
                    stage, extra, acc_ref, sems, extra_sem, *, alpha, n_exp):
    t = pl.program_id(0) * pl.num_programs(1) + pl.program_id(1)
    n_steps = pl.num_programs(0) * pl.num_programs(1)
    tm = x_ref.shape[0]
    R = DISPATCH_ROWS
    slot = t % 2

    def main_copies(step, buf, fn):
        for e in range(n_exp):
            fn(pltpu.make_async_copy(_rows_at(ys_hbm, base_ref[step * n_exp + e], R),
                                     stage.at[buf, pl.ds(e * R, R)], sems.at[buf]))

    @pl.when(t == 0)
    def _():
        main_copies(0, 0, lambda cp: cp.start())

    @pl.when(t + 1 < n_steps)
    def _():
        main_copies(t + 1, 1 - slot, lambda cp: cp.start())

    meta = meta_ref[...]
    col = lambda k: meta[:, k:k + 1]
    e0, e1 = col(META_E0).astype(I32), col(META_E1).astype(I32)
    r0, r1 = col(META_R0).astype(I32), col(META_R1).astype(I32)
    w0, w1 = col(META_W0), col(META_W1)
    lane = lax.broadcasted_iota(I32, (tm, R), 1)

    rels, ws, picks = [], [], []
    for e in range(n_exp):
        rel = jnp.where(e0 == e, r0, jnp.where(e1 == e, r1, -1))
        rels.append(jnp.where(rel >= 0, rel - off_ref[t * n_exp + e], -1))
        ws.append(jnp.where(e0 == e, w0, jnp.where(e1 == e, w1, 0.0)))
        picks.append(jnp.where(lane == rels[e], ws[e], 0.0))
    weighted = jnp.concatenate(picks, axis=1).astype(BF16)
    main_copies(t, slot, lambda cp: cp.wait())
    acc_ref[...] = jnp.dot(weighted, stage[slot], preferred_element_type=F32)

    for e in range(n_exp):
        for blk in range(1, tm // R):
            @pl.when(cnt_ref[t * n_exp + e] > blk * R)
            def _(e=e, blk=blk):
                cp = pltpu.make_async_copy(_rows_at(ys_hbm, base_ref[t * n_exp + e] + blk * R, R),
                                           extra, extra_sem)
                cp.start()
                cp.wait()
                more = jnp.where(lane + blk * R == rels[e], ws[e], 0.0).astype(BF16)
                acc_ref[...] += jnp.dot(more, extra[...], preferred_element_type=F32)

    z = alpha * x_ref[...] + (1.0 + gate_ref[...]) * acc_ref[...]
    o_ref[...] = _layer_norm(z, lg_ref[...], lb_ref[...])


def _combine(ys, base_te, off_te, cnt_te, meta, x, gate, ln_g, ln_b, *, alpha, tm):
    B, S, D = x.shape
    nt = S // tm
    n_exp = base_te.shape[0] // (B * nt)
    row = pl.BlockSpec((None, tm, D), lambda b, i, *_: (b, i, 0))
    vec = pl.BlockSpec((1, D), lambda b, i, *_: (0, 0))
    grid_spec = pltpu.PrefetchScalarGridSpec(
        num_scalar_prefetch=3,
        grid=(B, nt),
        in_specs=[pl.BlockSpec(memory_space=pl.ANY),
                  pl.BlockSpec((None, tm, LANES), lambda b, i, *_: (b, i, 0)),
                  row, pl.BlockSpec((None, 1, D), lambda b, i, *_: (b, 0, 0)), vec, vec],
        out_specs=row,
        scratch_shapes=[pltpu.VMEM((2, n_exp * DISPATCH_ROWS, D), BF16),
                        pltpu.VMEM((DISPATCH_ROWS, D), BF16),
                        pltpu.VMEM((tm, D), F32),
                        pltpu.SemaphoreType.DMA((2,)),
                        pltpu.SemaphoreType.DMA(())],
    )
    return pl.pallas_call(
        functools.partial(_combine_kernel, alpha=alpha, n_exp=n_exp),
        grid_spec=grid_spec,
        out_shape=jax.ShapeDtypeStruct((B, S, D), F32),
        compiler_params=_params(("arbitrary", "arbitrary")),
        name="moe_combine",
    )(base_te, off_te, cnt_te, ys, meta, x, gate, ln_g, ln_b)


def _moe_ffn(x, scale, shift, gate, w_router, wg, wu, wd, ln_g, ln_b, *, alpha, tm, tm_e, tf):
    B, S, D = x.shape
    T = B * S
    nt = T // tm
    n_exp = w_router.shape[1]
    wr = jnp.zeros((D, LANES), F32).at[:, :n_exp].set(w_router)
    wr_hi = wr.astype(BF16)
    wr = jnp.stack([wr_hi, (wr - wr_hi.astype(F32)).astype(BF16)])
    u, meta = _router(x, scale, shift, wr, n_exp=n_exp, tm=tm)

    meta2 = meta.reshape(T, LANES)
    e0 = meta2[:, META_E0].astype(I32)
    e1 = meta2[:, META_E1].astype(I32)
    r0 = meta2[:, META_R0].astype(I32)
    r1 = meta2[:, META_R1].astype(I32)
    experts = jnp.arange(n_exp, dtype=I32)
    chosen = ((e0[:, None] == experts) | (e1[:, None] == experts)).astype(I32)
    cnt_te = chosen.reshape(nt, tm, n_exp).sum(axis=1)
    seg_te = -(-cnt_te // SEGMENT_ALIGN) * SEGMENT_ALIGN
    off_te = jnp.cumsum(cnt_te, axis=0) - cnt_te
    filled = seg_te.sum(axis=0)
    sizes = -(-(filled + DISPATCH_SLACK) // tm_e) * tm_e
    ends = jnp.cumsum(sizes)
    starts = ends - sizes
    base_te = starts[None, :] + jnp.cumsum(seg_te, axis=0) - seg_te
    shift_te = (base_te - off_te)[:, None, :]

    def rows_of(e, r):
        pick = (e[:, None] == experts).astype(I32).reshape(nt, tm, n_exp)
        return r + (pick * shift_te).sum(axis=-1).reshape(T)

    d0, d1 = rows_of(e0, r0), rows_of(e1, r1)
    bound = TOP_K * T + n_exp * ((SEGMENT_ALIGN - 1) * nt + DISPATCH_SLACK)
    n_tiles = -(-bound // tm_e) + n_exp + 1
    n_active = (ends[-1] // tm_e).astype(I32)
    tile_id = jnp.minimum(jnp.arange(n_tiles, dtype=I32), n_active - 1)
    tile_expert = jnp.sum((ends[None, :] <= (tile_id * tm_e)[:, None]).astype(I32), axis=1)
    data_end = jnp.sum((tile_expert[:, None] == experts) * (starts + filled)[None, :], axis=1)
    left = data_end - tile_id * tm_e
    tile_fill = jnp.where(left <= 0, TILE_EMPTY, jnp.where(left <= tm_e // 2, TILE_HALF, TILE_FULL)).astype(I32)
    meta_t = jnp.stack([e0, e1, d0, d1] + [jnp.zeros_like(e0)] * 4)

    xs = _dispatch(u.reshape(T, D), meta_t, base_te.reshape(-1), cnt_te.reshape(-1),
                   jnp.concatenate([starts + filled, ends[-1:]]),
                   jnp.concatenate([sizes - filled, n_tiles * tm_e - ends[-1:]]),
                   n_rows=n_tiles * tm_e, tm=tm)
    ys = _expert_ffn(xs, tile_expert, n_active.reshape(1), tile_fill, wg, wu, wd, tm=tm_e, tf=tf)
    return _combine(ys, base_te.reshape(-1), off_te.reshape(-1), cnt_te.reshape(-1),
                    meta, x, gate, ln_g, ln_b, alpha=alpha, tm=tm)


class _Tiles(NamedTuple):
    token_rows: int
    rope_rows: int
    ffn_chunk: int
    expert_rows: int
    expert_ff: int


def _tile_plan(batch, seq):
    return _Tiles(token_rows=min(512, seq), rope_rows=min(1024, seq), ffn_chunk=2 * LANES,
                  expert_rows=min(1024, batch * seq), expert_ff=512)


def kernel(x, c, positions, w_in, w_out, attn_norm_gain, rec_norm_gain, rec_lb_logits, ada_w, ada_b,
           ln_gain, ln_bias, ffn_w_gate, ffn_w_up, ffn_w_down, moe_router, moe_w_gate, moe_w_up,
           moe_w_down):
    B, S, D = x.shape
    depth = w_in.shape[0]
    alpha = (2 * depth) ** 0.25

    p = jax.nn.softmax(rec_lb_logits.astype(F32), axis=0)
    cum = jnp.cumsum(p, axis=0)
    lb_all = cum - cum[0:1]
    half = ATTN_HEAD_DIM // 2
    inv_freq = ROPE_THETA ** (-jnp.arange(half, dtype=F32) / half)
    inv_freq_lanes = jnp.tile(inv_freq, LANES // half).reshape(1, LANES)
    pos_f = positions.astype(F32).reshape(B, S, 1)
    n_qk = 2 * ATTN_WIDTH
    w_in_b = jnp.concatenate([_qk_lane_order(w_in[..., :n_qk]), w_in[..., n_qk:]], axis=-1).astype(BF16)
    w_out_b = w_out.astype(BF16)
    ff = ffn_w_gate.shape[2]
    ffp = -(-ff // (2 * LANES)) * (2 * LANES)
    pad_c = lambda w: jnp.pad(w.astype(BF16), ((0, 0), (0, 0), (0, ffp - ff)))
    ffn_g, ffn_u = pad_c(ffn_w_gate), pad_c(ffn_w_up)
    ffn_d = jnp.pad(ffn_w_down.astype(BF16), ((0, 0), (0, ffp - ff), (0, 0)))

    mods = _ada_modulation(c, ada_w, ada_b)
    tiles = _tile_plan(B, S)
    rope_cos, rope_sin = _rope_tables(pos_f, inv_freq_lanes, tm=tiles.rope_rows)

    def mod(layer, sub):
        m = mods[layer * 2 + sub].reshape(3, B, 1, D)
        return m[0], m[1], m[2]

    vec = lambda a: a.reshape(1, -1)
    for layer in range(depth):
        shift, scale, gate = mod(layer, 0)
        proj = _in_projection(x, scale, shift, rope_cos, rope_sin, w_in_b[layer], tm=tiles.token_rows)
        attn = _dilated_attention(proj)
        rec = _hgrn2(proj, lb_all[layer], vec(rec_norm_gain[layer]))
        x = _out_projection(attn, rec, x, gate, vec(attn_norm_gain[layer]), w_out_b[layer],
                            vec(ln_gain[layer, 0]), vec(ln_bias[layer, 0]), alpha=alpha,
                            tm=tiles.token_rows)
        shift, scale, gate = mod(layer, 1)
        j = layer // 2
        if layer % 2 == 0:
            x = _dense_ffn(x, scale, shift, gate, ffn_g[j], ffn_u[j], ffn_d[j],
                           vec(ln_gain[layer, 1]), vec(ln_bias[layer, 1]),
                           alpha=alpha, tm=tiles.token_rows, tc=tiles.ffn_chunk)
        else:
            x = _moe_ffn(x, scale, shift, gate, moe_router[j], moe_w_gate[j], moe_w_up[j],
                         moe_w_down[j], vec(ln_gain[layer, 1]), vec(ln_bias[layer, 1]),
                         alpha=alpha, tm=tiles.token_rows, tm_e=tiles.expert_rows, tf=tiles.expert_ff)
    return x
```

```python
import functools
from typing import NamedTuple

import jax
import jax.numpy as jnp
from jax import lax
from jax.experimental import pallas as pl
from jax.experimental.pallas import tpu as pltpu

F32 = jnp.float32
BF16 = jnp.bfloat16
I32 = jnp.int32

LANES = 128
V7X_VMEM_LIMIT_BYTES = 56 * 1024 * 1024

ATTN_WIDTH = 512
ATTN_HEAD_DIM = 64
DILATIONS = (1, 4, 16)
ATTN_RADIUS = 64
ROPE_THETA = 10000.0
REC_WIDTH = 512
REC_DIM = 128
REC_CHUNK = 64
TOP_K = 2
LN_EPS = 1e-5
RMS_EPS = 1e-6
MASK_VALUE = -1e30
DECAY_EXP_CLAMP = 80.0
LOG2_E = 1.4426950408889634


def _params(semantics):
    return pltpu.CompilerParams(dimension_semantics=semantics,
                                vmem_limit_bytes=V7X_VMEM_LIMIT_BYTES)


def _sigmoid(x):
    return 1.0 / (1.0 + jnp.exp(-x))


def _layer_norm(z, gain, bias):
    mu = jnp.mean(z, axis=-1, keepdims=True)
    zc = z - mu
    var = jnp.mean(zc * zc, axis=-1, keepdims=True)
    return zc * lax.rsqrt(var + LN_EPS) * gain + bias


def _ada_kernel(c_ref, w_ref, b_ref, o_ref):
    c = c_ref[...]
    o_ref[...] = jnp.dot(c * _sigmoid(c), w_ref[...], precision=lax.Precision.HIGHEST,
                         preferred_element_type=F32) + b_ref[...]


def _ada_modulation(c, ada_w, ada_b):
    B, D = c.shape
    n = ada_w.shape[0] * ada_w.shape[1]
    n3 = ada_w.shape[-1]
    tn = D
    return pl.pallas_call(
        _ada_kernel,
        grid=(n, n3 // tn),
        in_specs=[pl.BlockSpec((B, D), lambda i, j: (0, 0)),
                  pl.BlockSpec((None, D, tn), lambda i, j: (i, 0, j)),
                  pl.BlockSpec((None, 1, tn), lambda i, j: (i, 0, j))],
        out_specs=pl.BlockSpec((None, None, B, tn), lambda i, j: (i, j, 0, 0)),
        out_shape=jax.ShapeDtypeStruct((n, n3 // tn, B, tn), F32),
        compiler_params=_params(("arbitrary", "arbitrary")),
        name="ada_modulation",
    )(c, ada_w.reshape(n, D, n3), ada_b.reshape(n, 1, n3))


ROPE_HALF = ATTN_HEAD_DIM // 2


def _qk_lane_order(w_cols):
    lead = w_cols.shape[:-1]
    w = w_cols.reshape(*lead, -1, 2, 2, ROPE_HALF)
    return jnp.swapaxes(w, -3, -2).reshape(*lead, -1)


def _rope_kernel(pos_ref, invf_ref, cos_ref, sin_ref):
    ang = pos_ref[...] * invf_ref[...]
    lane = lax.broadcasted_iota(I32, (1, LANES), 1)
    sin = jnp.sin(ang)
    cos_ref[...] = jnp.cos(ang)
    sin_ref[...] = jnp.where(lane < LANES // 2, -sin, sin)


def _rope_tables(pos_f, inv_freq_lanes, *, tm):
    B, S, _ = pos_f.shape
    tab = pl.BlockSpec((None, tm, LANES), lambda b, i: (b, i, 0))
    return pl.pallas_call(
        _rope_kernel,
        grid=(B, S // tm),
        in_specs=[pl.BlockSpec((None, tm, 1), lambda b, i: (b, i, 0)),
                  pl.BlockSpec((1, LANES), lambda b, i: (0, 0))],
        out_specs=[tab, tab],
        out_shape=[jax.ShapeDtypeStruct((B, S, LANES), F32)] * 2,
        compiler_params=_params(("arbitrary", "arbitrary")),
        name="rope_tables",
    )(pos_f, inv_freq_lanes)


def _inproj_kernel(x_ref, sc_ref, sh_ref, cos_ref, sin_ref, w_ref, o_ref, *, tn):
    u = (x_ref[...] * (1.0 + sc_ref[...]) + sh_ref[...]).astype(BF16)
    cos = cos_ref[...]
    sin_signed = sin_ref[...]
    q_scale = ATTN_HEAD_DIM ** -0.5
    for j in range(w_ref.shape[1] // tn):
        acc = jnp.dot(u, w_ref[:, j * tn:(j + 1) * tn], preferred_element_type=F32)
        if j > 0:
            o_ref[:, j * tn:(j + 1) * tn] = acc
            continue
        for kk in range(tn // LANES):
            c = acc[:, kk * LANES:(kk + 1) * LANES]
            r = c * cos + pltpu.roll(c, LANES // 2, 1) * sin_signed
            if kk * LANES < ATTN_WIDTH:
                r = r * q_scale
            o_ref[:, kk * LANES:(kk + 1) * LANES] = r


def _in_projection(x, scale, shift, rope_cos, rope_sin, w_in_bf16, layer, *, tm):
    B, S, D = x.shape
    n_cols = w_in_bf16.shape[2]
    tn = 2 * ATTN_WIDTH
    assert S % tm == 0 and n_cols % tn == 0
    return pl.pallas_call(
        functools.partial(_inproj_kernel, tn=tn),
        grid=(B, S // tm),
        in_specs=[pl.BlockSpec((None, tm, D), lambda b, i: (b, i, 0)),
                  pl.BlockSpec((None, 1, D), lambda b, i: (b, 0, 0)),
                  pl.BlockSpec((None, 1, D), lambda b, i: (b, 0, 0)),
                  pl.BlockSpec((None, tm, LANES), lambda b, i: (b, i, 0)),
                  pl.BlockSpec((None, tm, LANES), lambda b, i: (b, i, 0)),
                  pl.BlockSpec((None, D, n_cols), lambda b, i: (layer, 0, 0))],
        out_specs=pl.BlockSpec((None, tm, n_cols), lambda b, i: (b, i, 0)),
        out_shape=jax.ShapeDtypeStruct((B, S, n_cols), F32),
        compiler_params=_params(("arbitrary", "arbitrary")),
        name="in_projection",
    )(x, scale, shift, rope_cos, rope_sin, w_in_bf16)


ATTN_BLOCK_GROUP = 16


def _attn_kernel(q_ref, k_ref, v_ref, o_ref, qs, ks, vs, qf, kf, vf, pm, plr, pa, nm, nl, na, bias_s,
                 *, S):
    lane = lax.broadcasted_iota(I32, (1, LANES), 1)
    head0 = lane < ATTN_HEAD_DIM
    head0_qk = (lane % ATTN_HEAD_DIM) < ROPE_HALF

    for p, d in enumerate(DILATIONS):
        L = S // d
        tq = min(128, L)
        W = min(2 * tq, L)
        nb = L // tq

        d_prev = DILATIONS[p - 1] if p else 1
        ratio, l_prev = d // d_prev, S // d_prev
        keep_f32 = 0 < p < len(DILATIONS) - 1
        for src, stage, dst in ((q_ref, qf, qs), (k_ref, kf, ks), (v_ref, vf, vs)):
            src = src if p <= 1 else stage
            for r in range(d):
                a, r_prev = divmod(r, d_prev)
                rows = pl.ds(r_prev * l_prev + a, L, stride=ratio) if ratio > 1 else pl.ds(0, L)
                val = src[rows, :]
                if keep_f32:
                    stage[r * L:(r + 1) * L, :] = val
                dst[r * L:(r + 1) * L, :] = val.astype(BF16)

        n_blocks = d * nb
        rc = lax.broadcasted_iota(I32, (tq, W), 0) - lax.broadcasted_iota(I32, (tq, W), 1)
        for which, delta in enumerate((0, ATTN_RADIUS, tq)):
            bias_s[which, pl.ds(0, tq), pl.ds(0, W)] = jnp.where(
                jnp.abs(rc + delta) <= ATTN_RADIUS, 0.0, MASK_VALUE)
        group = min(ATTN_BLOCK_GROUP, n_blocks)
        assert n_blocks % group == 0

        def block_group(gi, carry, L=L, tq=tq, W=W, nb=nb, group=group):
            q0s, k0s, scores, probs, outs, sums = [], [], [], [], [], []
            for u in range(group):
                g = gi * group + u
                r = g // nb
                n = g - r * nb
                base = r * L
                q0 = pl.multiple_of(base + n * tq, tq)
                ws = jnp.clip(n * tq - ATTN_RADIUS, 0, L - W)
                k0 = pl.multiple_of(base + ws, 16)
                qb = qs[pl.ds(q0, tq), :]
                zero = jnp.zeros_like(qb)
                q2 = jnp.concatenate([jnp.where(head0_qk, qb, zero), jnp.where(head0_qk, zero, qb)], axis=0)
                s = lax.dot_general(q2, ks[pl.ds(k0, W), :], (((1,), (1,)), ((), ())),
                                    preferred_element_type=F32)
                delta = n * tq - ws
                which = jnp.where(delta == 0, 0, jnp.where(delta == ATTN_RADIUS, 1, 2))
                bias = bias_s[which, pl.ds(0, tq), pl.ds(0, W)]
                scores.append(s + jnp.concatenate([bias, bias], axis=0))
                q0s.append(q0)
                k0s.append(k0)
            for u in range(group):
                s = scores[u]
                m = jnp.max(s, axis=1, keepdims=True)
                e = jnp.exp(s - m)
                probs.append((m, e.astype(BF16)))
                sums.append(jnp.sum(e, axis=1, keepdims=True))
            for u in range(group):
                outs.append(jnp.dot(probs[u][1], vs[pl.ds(k0s[u], W), :], preferred_element_type=F32))
            for u in range(group):
                m, l = probs[u][0], sums[u]
                rows = pl.ds(q0s[u], tq)
                out_m[rows, :] = jnp.where(head0, m[:tq], m[tq:])
                out_l[rows, :] = jnp.where(head0, l[:tq], l[tq:])
                out_a[rows, :] = jnp.where(head0, outs[u][:tq], outs[u][tq:])
            return carry

        out_m, out_l, out_a = (nm.at[p], nl.at[p], na.at[p]) if d == 1 else (pm, plr, pa)
        lax.fori_loop(0, n_blocks // group, block_group, 0)

        for r in range(d if d > 1 else 0):
            rows = pl.ds(r, L, stride=d)
            nm[p, rows, :] = pm[r * L:(r + 1) * L, :]
            nl[p, rows, :] = plr[r * L:(r + 1) * L, :]
            na[p, rows, :] = pa[r * L:(r + 1) * L, :]

    m_all = jnp.maximum(jnp.maximum(nm[0], nm[1]), nm[2])
    num = jnp.zeros((S, LANES), F32)
    den = jnp.zeros((S, LANES), F32)
    for p in range(len(DILATIONS)):
        w = jnp.exp(nm[p] - m_all)
        num = num + w * na[p]
        den = den + w * nl[p]
    o_ref[...] = (num / den).astype(o_ref.dtype)


def _dilated_attention(proj):
    B, S, _ = proj.shape
    n_pairs = ATTN_WIDTH // LANES
    assert S % (16 * DILATIONS[-1]) == 0
    blk = lambda off: pl.BlockSpec((None, S, LANES), lambda b, h: (b, 0, off + h))
    return pl.pallas_call(
        functools.partial(_attn_kernel, S=S),
        grid=(B, n_pairs),
        in_specs=[blk(0), blk(n_pairs), blk(2 * n_pairs)],
        out_specs=pl.BlockSpec((None, S, LANES), lambda b, h: (b, 0, h)),
        out_shape=jax.ShapeDtypeStruct((B, S, ATTN_WIDTH), BF16),
        scratch_shapes=[pltpu.VMEM((S, LANES), BF16)] * 3
                       + [pltpu.VMEM((S, LANES), F32)] * 6
                       + [pltpu.VMEM((len(DILATIONS), S, LANES), F32)] * 3
                       + [pltpu.VMEM((3, min(128, S), min(256, S)), F32)],
        compiler_params=_params(("arbitrary", "arbitrary")),
        name="dilated_attention",
    )(proj, proj, proj)


REC_GROUP_CHUNKS = 4
REC_GROUP_ROWS = REC_GROUP_CHUNKS * REC_CHUNK
REC_STAGE_GROUPS = 8


def _hgrn_kernel(rq_ref, zf_ref, zb_ref, ri_ref, rg_ref, lb_ref, gain_ref, o_ref,
                 q_s, vt_s, oi_s, qp_s, ut_s, dd_s, st_s, os_s, *, S):
    C, G, GR = REC_CHUNK, REC_GROUP_CHUNKS, REC_GROUP_ROWS
    nc, ng = S // C, S // GR
    sg = min(REC_STAGE_GROUPS, ng)
    assert ng % sg == 0

    row = lax.broadcasted_iota(I32, (GR, GR), 0)
    col = lax.broadcasted_iota(I32, (GR, GR), 1)
    same_chunk = (row // C) == (col // C)
    t_i = lax.broadcasted_iota(I32, (C, C), 0)
    s_i = lax.broadcasted_iota(I32, (C, C), 1)

    def prep(i, carry):
        r0 = pl.multiple_of(i * GR, GR)
        rq = rq_ref[pl.ds(r0, GR), :]
        q_s[pl.ds(r0, GR), :] = rq * _sigmoid(rq)
        v = ri_ref[pl.ds(r0, GR), :]
        for c in range(G):
            vt_s[i * G + c] = v[c * C:(c + 1) * C, :].T.astype(BF16)
        return carry

    lax.fori_loop(0, ng, prep, 0, unroll=True)

    for direction in range(2):
        fwd = direction == 0
        z_ref = zf_ref if fwd else zb_ref
        cum = jnp.where(same_chunk & ((col <= row) if fwd else (col >= row)), 1.0, 0.0).astype(BF16)
        keep = (s_i <= t_i) if fwd else (s_i >= t_i)
        mid_row = C // 2 - 1 if fwd else C // 2
        last_row = C - 1 if fwd else 0

        def phase_a(it, carry, fwd=fwd, z_ref=z_ref, cum=cum, keep=keep,
                    mid_row=mid_row, last_row=last_row, direction=direction):
            lb = lb_ref[direction:direction + 1, :]
            log_lb = jnp.log(lb)
            log_1m_lb = jnp.log1p(-lb)
            r0s, c0s, k3s, bcs, atts, stage3 = [], [], [], [], [], []
            for u in range(sg):
                i = it * sg + u
                r0 = pl.multiple_of(i * GR, GR)
                z = z_ref[pl.ds(r0, GR), :]
                e = jnp.exp(-jnp.abs(z))
                t = log_1m_lb + (jnp.minimum(z, 0.0) - jnp.log(1.0 + e))
                g = jnp.maximum(log_lb, t) + jnp.log(1.0 + jnp.exp(-jnp.abs(log_lb - t)))
                kk = (1.0 - lb) * (jnp.where(z > 0, e, 1.0) / (1.0 + e))
                g = g * LOG2_E
                g1 = g.astype(BF16)
                r1 = g - g1.astype(F32)
                g2 = r1.astype(BF16)
                g3 = (r1 - g2.astype(F32)).astype(BF16)
                bcs.append(jnp.dot(cum, g1, preferred_element_type=F32)
                           + jnp.dot(cum, g2, preferred_element_type=F32)
                           + jnp.dot(cum, g3, preferred_element_type=F32))
                k3s.append(kk.reshape(G, C, REC_DIM))
                r0s.append(r0)
                c0s.append(pl.multiple_of(i * G, G))
            for u in range(sg):
                b3 = bcs[u].reshape(G, C, REC_DIM)
                mid = b3[:, mid_row:mid_row + 1, :]
                last = b3[:, last_row:last_row + 1, :]
                e3 = b3 - mid
                q3 = q_s[pl.ds(r0s[u], GR), :].reshape(G, C, REC_DIM)
                clamp = DECAY_EXP_CLAMP * LOG2_E
                qt = (q3 * jnp.exp2(jnp.minimum(e3, clamp))).astype(BF16)
                kt = (k3s[u] * jnp.exp2(jnp.minimum(-e3, clamp))).astype(BF16)
                atts.append(jnp.einsum('gtc,gsc->gts', qt, kt, preferred_element_type=F32))
                qp_s[pl.ds(r0s[u], GR), :] = (q3 * jnp.exp2(b3)).astype(BF16).reshape(GR, REC_DIM)
                stage3.append(((k3s[u] * jnp.exp2(last - b3)).astype(BF16), jnp.exp2(last)))
            for u in range(sg):
                att = jnp.where(keep[None], atts[u], 0.0).astype(BF16)
                v3 = ri_ref[pl.ds(r0s[u], GR), :].reshape(G, C, REC_DIM).astype(BF16)
                oi = jnp.einsum('gts,gsv->gtv', att, v3, preferred_element_type=F32)
                oi_s[pl.ds(r0s[u], GR), :] = oi.reshape(GR, REC_DIM)
                kp, decay = stage3[u]
                ut_s[pl.ds(c0s[u], G)] = jnp.einsum('gvs,gsc->gvc', vt_s[pl.ds(c0s[u], G)], kp,
                                                    preferred_element_type=F32)
                dd_s[pl.ds(c0s[u], G)] = decay
            return carry

        lax.fori_loop(0, ng // sg, phase_a, 0)

        def scan(c, st, fwd=fwd):
            idx = c if fwd else nc - 1 - c
            st_s[idx] = st.astype(BF16)
            return dd_s[idx] * st + ut_s[idx]

        lax.fori_loop(0, nc, scan, jnp.zeros((REC_DIM, REC_DIM), F32), unroll=True)

        def phase_c(it, carry, fwd=fwd):
            r0s = [pl.multiple_of((it * sg + u) * GR, GR) for u in range(sg)]
            oos = []
            for u in range(sg):
                c0 = pl.multiple_of((it * sg + u) * G, G)
                qp3 = qp_s[pl.ds(r0s[u], GR), :].reshape(G, C, REC_DIM)
                oos.append(jnp.einsum('gtc,gvc->gtv', qp3, st_s[pl.ds(c0, G)],
                                      preferred_element_type=F32))
            for u in range(sg):
                rows = pl.ds(r0s[u], GR)
                tot = oi_s[rows, :] + oos[u].reshape(GR, REC_DIM)
                os_s[rows, :] = tot if fwd else os_s[rows, :] + tot
            return carry

        lax.fori_loop(0, ng // sg, phase_c, 0)

    o = os_s[...]
    o = o * lax.rsqrt(jnp.mean(o * o, axis=-1, keepdims=True) + RMS_EPS) * gain_ref[...]
    o_ref[...] = (o * _sigmoid(rg_ref[...])).astype(o_ref.dtype)


def _hgrn2(proj, lb, rec_gain):
    B, S, _ = proj.shape
    nh = REC_WIDTH // REC_DIM
    assert S % REC_GROUP_ROWS == 0
    col0 = 3 * ATTN_WIDTH // LANES
    blk = lambda k: pl.BlockSpec((None, S, REC_DIM), lambda b, h: (b, 0, col0 + k * nh + h))
    nc = S // REC_CHUNK
    return pl.pallas_call(
        functools.partial(_hgrn_kernel, S=S),
        grid=(B, nh),
        in_specs=[blk(0), blk(1), blk(2), blk(3), blk(4),
                  pl.BlockSpec((2, REC_DIM), lambda b, h: (0, h)),
                  pl.BlockSpec((1, REC_DIM), lambda b, h: (0, h))],
        out_specs=pl.BlockSpec((None, S, REC_DIM), lambda b, h: (b, 0, h)),
        out_shape=jax.ShapeDtypeStruct((B, S, REC_WIDTH), BF16),
        scratch_shapes=[pltpu.VMEM((S, REC_DIM), F32),
                        pltpu.VMEM((nc, REC_DIM, REC_CHUNK), BF16),
                        pltpu.VMEM((S, REC_DIM), F32),
                        pltpu.VMEM((S, REC_DIM), BF16),
                        pltpu.VMEM((nc, REC_DIM, REC_DIM), F32),
                        pltpu.VMEM((nc, 1, REC_DIM), F32),
                        pltpu.VMEM((nc, REC_DIM, REC_DIM), BF16),
                        pltpu.VMEM((S, REC_DIM), F32)],
        compiler_params=_params(("arbitrary", "arbitrary")),
        name="hgrn2",
    )(proj, proj, proj, proj, proj, lb, rec_gain)


def _outproj_kernel(a_ref, r_ref, x_ref, gate_ref, ag_ref, w_ref, lg_ref, lb_ref, o_ref, *, alpha):
    a = a_ref[...].astype(F32)
    an = a * lax.rsqrt(jnp.mean(a * a, axis=-1, keepdims=True) + RMS_EPS) * ag_ref[...]
    y = (jnp.dot(an.astype(BF16), w_ref[:ATTN_WIDTH, :], preferred_element_type=F32)
         + jnp.dot(r_ref[...], w_ref[ATTN_WIDTH:, :], preferred_element_type=F32))
    z = alpha * x_ref[...] + (1.0 + gate_ref[...]) * y
    o_ref[...] = _layer_norm(z, lg_ref[...], lb_ref[...])


def _out_projection(attn, rec, x, gate, attn_gain, w_out_bf16, layer, ln_g, ln_b, *, alpha, tm):
    B, S, D = x.shape
    row = lambda w: pl.BlockSpec((None, tm, w), lambda b, i: (b, i, 0))
    vec = lambda w: pl.BlockSpec((1, w), lambda b, i: (0, 0))
    return pl.pallas_call(
        functools.partial(_outproj_kernel, alpha=alpha),
        grid=(B, S // tm),
        in_specs=[row(ATTN_WIDTH), row(REC_WIDTH), row(D),
                  pl.BlockSpec((None, 1, D), lambda b, i: (b, 0, 0)),
                  vec(ATTN_WIDTH),
                  pl.BlockSpec((None, ATTN_WIDTH + REC_WIDTH, D), lambda b, i: (layer, 0, 0)),
                  vec(D), vec(D)],
        out_specs=row(D),
        out_shape=jax.ShapeDtypeStruct((B, S, D), F32),
        compiler_params=_params(("arbitrary", "arbitrary")),
        name="out_projection",
    )(attn, rec, x, gate, attn_gain, w_out_bf16, ln_g, ln_b)


def _ffn_kernel(x_ref, sc_ref, sh_ref, gate_ref, wg_ref, wu_ref, wd_ref, lg_ref, lb_ref, o_ref,
                *, alpha, tc):
    x = x_ref[...]
    u = (x * (1.0 + sc_ref[...]) + sh_ref[...]).astype(BF16)
    acc = jnp.zeros(x.shape, F32)
    for c in range(wg_ref.shape[1] // tc):
        cols = slice(c * tc, (c + 1) * tc)
        g = jnp.dot(u, wg_ref[:, cols], preferred_element_type=F32)
        up = jnp.dot(u, wu_ref[:, cols], preferred_element_type=F32)
        h = (g * _sigmoid(g) * up).astype(BF16)
        acc = acc + jnp.dot(h, wd_ref[cols, :], preferred_element_type=F32)
    z = alpha * x + (1.0 + gate_ref[...]) * acc
    o_ref[...] = _layer_norm(z, lg_ref[...], lb_ref[...])


def _dense_ffn(x, scale, shift, gate, wg, wu, wd, ln_g, ln_b, *, alpha, tm, tc):
    B, S, D = x.shape
    ffp = wg.shape[1]
    assert ffp % tc == 0
    row = pl.BlockSpec((None, tm, D), lambda b, i: (b, i, 0))
    mod = pl.BlockSpec((None, 1, D), lambda b, i: (b, 0, 0))
    vec = pl.BlockSpec((1, D), lambda b, i: (0, 0))
    return pl.pallas_call(
        functools.partial(_ffn_kernel, alpha=alpha, tc=tc),
        grid=(B, S // tm),
        in_specs=[row, mod, mod, mod,
                  pl.BlockSpec((D, ffp), lambda b, i: (0, 0)),
                  pl.BlockSpec((D, ffp), lambda b, i: (0, 0)),
                  pl.BlockSpec((ffp, D), lambda b, i: (0, 0)),
                  vec, vec],
        out_specs=row,
        out_shape=jax.ShapeDtypeStruct((B, S, D), F32),
        compiler_params=_params(("arbitrary", "arbitrary")),
        name="dense_ffn",
    )(x, scale, shift, gate, wg, wu, wd, ln_g, ln_b)


META_E0, META_E1, META_W0, META_W1, META_R0, META_R1 = range(6)
SEGMENT_ALIGN = 16
DISPATCH_ROWS = 256
DISPATCH_HALF = DISPATCH_ROWS // 2
DISPATCH_SLACK = DISPATCH_HALF


def _router_kernel(x_ref, sc_ref, sh_ref, wr_ref, u_ref, meta_ref, carry_ref, *, n_exp):
    tm = x_ref.shape[0]

    @pl.when((pl.program_id(0) == 0) & (pl.program_id(1) == 0))
    def _():
        carry_ref[...] = jnp.zeros_like(carry_ref)

    u = x_ref[...] * (1.0 + sc_ref[...]) + sh_ref[...]
    u_ref[...] = u.astype(BF16)
    u_hi = u.astype(BF16)
    u_lo = (u - u_hi.astype(F32)).astype(BF16)
    logits = (jnp.dot(u_hi, wr_ref[0], preferred_element_type=F32)
              + jnp.dot(u_lo, wr_ref[0], preferred_element_type=F32)
              + jnp.dot(u_hi, wr_ref[1], preferred_element_type=F32))
    lane = lax.broadcasted_iota(I32, (tm, LANES), 1)
    neg = -jnp.inf
    l1 = jnp.where(lane < n_exp, logits, neg)
    m1 = jnp.max(l1, axis=1, keepdims=True)
    i1 = jnp.min(jnp.where(l1 == m1, lane, LANES), axis=1, keepdims=True)
    l2 = jnp.where(lane == i1, neg, l1)
    m2 = jnp.max(l2, axis=1, keepdims=True)
    i2 = jnp.min(jnp.where(l2 == m2, lane, LANES), axis=1, keepdims=True)
    e = jnp.exp(m2 - m1)
    w1 = 1.0 / (1.0 + e)
    w2 = e * w1
    sel = jnp.where((lane == i1) | (lane == i2), 1.0, 0.0)
    before = (lax.broadcasted_iota(I32, (tm, tm), 1) < lax.broadcasted_iota(I32, (tm, tm), 0))
    ranks = jnp.dot(jnp.where(before, 1.0, 0.0).astype(BF16), sel.astype(BF16),
                    preferred_element_type=F32) + carry_ref[...]
    r1 = jnp.sum(jnp.where(lane == i1, ranks, 0.0), axis=1, keepdims=True)
    r2 = jnp.sum(jnp.where(lane == i2, ranks, 0.0), axis=1, keepdims=True)
    carry_ref[...] = carry_ref[...] + jnp.sum(sel, axis=0, keepdims=True)
    meta = jnp.zeros((tm, LANES), F32)
    for k, val in ((META_E0, i1.astype(F32)), (META_E1, i2.astype(F32)), (META_W0, w1),
                   (META_W1, w2), (META_R0, r1), (META_R1, r2)):
        meta = jnp.where(lane == k, val, meta)
    meta_ref[...] = meta


def _router(x, scale, shift, w_router_lanes, *, n_exp, tm):
    B, S, D = x.shape
    nt = S // tm
    row = pl.BlockSpec((None, tm, D), lambda b, i: (b, i, 0))
    mod = pl.BlockSpec((None, 1, D), lambda b, i: (b, 0, 0))
    return pl.pallas_call(
        functools.partial(_router_kernel, n_exp=n_exp),
        grid=(B, nt),
        in_specs=[row, mod, mod, pl.BlockSpec((2, D, LANES), lambda b, i: (0, 0, 0))],
        out_specs=[row, pl.BlockSpec((None, tm, LANES), lambda b, i: (b, i, 0))],
        out_shape=[jax.ShapeDtypeStruct((B, S, D), BF16),
                   jax.ShapeDtypeStruct((B, S, LANES), F32)],
        scratch_shapes=[pltpu.VMEM((1, LANES), F32)],
        compiler_params=_params(("arbitrary", "arbitrary")),
        name="moe_router",
    )(x, scale, shift, w_router_lanes)


def _rows_at(hbm, first_row, n_rows):
    return hbm.at[pl.ds(pl.multiple_of(first_row, SEGMENT_ALIGN), n_rows)]


def _dispatch_kernel(base_ref, cnt_ref, padlo_ref, padn_ref, u_ref, meta_ref, xs_hbm,
                     stage, extra, zeros_buf, sems, extra_sem, *, n_exp):
    t = pl.program_id(0)
    last = pl.num_programs(0) - 1
    tm = u_ref.shape[0]
    R, H = DISPATCH_ROWS, DISPATCH_HALF
    slot = t % 2
    u = u_ref[...]
    e0, e1 = meta_ref[0:1, :], meta_ref[1:2, :]
    d0, d1 = meta_ref[2:3, :], meta_ref[3:4, :]
    srow = lax.broadcasted_iota(I32, (R, tm), 0)

    def main_copies(step, buf, fn):
        for e in range(n_exp):
            base = base_ref[step * n_exp + e]
            fn(pltpu.make_async_copy(stage.at[buf, pl.ds(e * R, H)], _rows_at(xs_hbm, base, H),
                                     sems.at[buf]))

            @pl.when(cnt_ref[step * n_exp + e] > H)
            def _(e=e, base=base):
                fn(pltpu.make_async_copy(stage.at[buf, pl.ds(e * R + H, H)],
                                         _rows_at(xs_hbm, base + H, H), sems.at[buf]))

    rels = []
    for e in range(n_exp):
        rel = jnp.where(e0 == e, d0, jnp.where(e1 == e, d1, -1))
        rels.append(jnp.where(rel >= 0, rel - base_ref[t * n_exp + e], -1))
    onehot = jnp.concatenate([jnp.where(srow == rel, 1.0, 0.0) for rel in rels], axis=0).astype(BF16)
    stage[slot] = jnp.dot(onehot, u, preferred_element_type=F32).astype(BF16)

    @pl.when(t > 0)
    def _():
        main_copies(t - 1, 1 - slot, lambda cp: cp.wait())
    main_copies(t, slot, lambda cp: cp.start())

    for e in range(n_exp):
        for blk in range(1, tm // R):
            @pl.when(cnt_ref[t * n_exp + e] > blk * R)
            def _(e=e, blk=blk):
                more = jnp.where(srow + blk * R == rels[e], 1.0, 0.0).astype(BF16)
                extra[...] = jnp.dot(more, u, preferred_element_type=F32).astype(BF16)
                cp = pltpu.make_async_copy(extra, _rows_at(xs_hbm, base_ref[t * n_exp + e] + blk * R, R),
                                           extra_sem)
                cp.start()
                cp.wait()

    @pl.when(t == last)
    def _():
        main_copies(t, slot, lambda cp: cp.wait())
        zeros_buf[...] = jnp.zeros_like(zeros_buf)
        pieces = [SEGMENT_ALIGN << b for b in range((R // SEGMENT_ALIGN).bit_length() - 1)]

        def pad_copies(fn):
            for p in range(padlo_ref.shape[0]):
                lo, n = padlo_ref[p], padn_ref[p]
                whole = n // R

                def body(k, carry, lo=lo):
                    fn(pltpu.make_async_copy(zeros_buf, _rows_at(xs_hbm, lo + k * R, R), extra_sem))
                    return carry
                lax.fori_loop(0, whole, body, 0)
                rem = n - whole * R
                for sz in pieces:
                    off = lo + whole * R + (rem // (2 * sz)) * (2 * sz)
                    pl.when((rem // sz) % 2 == 1)(functools.partial(
                        lambda off, sz: fn(pltpu.make_async_copy(
                            zeros_buf.at[pl.ds(0, sz)], _rows_at(xs_hbm, off, sz), extra_sem)), off, sz))

        pad_copies(lambda cp: cp.start())
        pad_copies(lambda cp: cp.wait())


def _dispatch(u, meta_t, base_te, cnt_te, pad_lo, pad_n, *, n_rows, tm):
    T, D = u.shape
    n_exp = pad_lo.shape[0] - 1
    assert tm % DISPATCH_ROWS == 0
    grid_spec = pltpu.PrefetchScalarGridSpec(
        num_scalar_prefetch=4,
        grid=(T // tm,),
        in_specs=[pl.BlockSpec((tm, D), lambda t, *_: (t, 0)),
                  pl.BlockSpec((meta_t.shape[0], tm), lambda t, *_: (0, t))],
        out_specs=pl.BlockSpec(memory_space=pl.ANY),
        scratch_shapes=[pltpu.VMEM((2, n_exp * DISPATCH_ROWS, D), BF16),
                        pltpu.VMEM((DISPATCH_ROWS, D), BF16),
                        pltpu.VMEM((DISPATCH_ROWS, D), BF16),
                        pltpu.SemaphoreType.DMA((2,)),
                        pltpu.SemaphoreType.DMA(())],
    )
    return pl.pallas_call(
        functools.partial(_dispatch_kernel, n_exp=n_exp),
        grid_spec=grid_spec,
        out_shape=jax.ShapeDtypeStruct((n_rows, D), BF16),
        compiler_params=_params(("arbitrary",)),
        name="moe_dispatch",
    )(base_te, cnt_te, pad_lo, pad_n, u, meta_t)


TILE_FULL, TILE_HALF, TILE_EMPTY = 0, 1, 2


def _expert_kernel(te_ref, na_ref, fill_ref, xs_ref, wg_ref, wu_ref, wd_ref, o_ref, acc_ref):
    i = pl.program_id(0)
    j = pl.program_id(1)
    last = pl.num_programs(1) - 1
    active = i < na_ref[0]
    fill = fill_ref[i]
    tm = acc_ref.shape[0]

    @pl.when(active & (j == 0))
    def _():
        acc_ref[...] = jnp.zeros_like(acc_ref)

    def swiglu(rows):
        xb = xs_ref[rows, :]
        g = jnp.dot(xb, wg_ref[...].astype(BF16), preferred_element_type=F32)
        up = jnp.dot(xb, wu_ref[...].astype(BF16), preferred_element_type=F32)
        h = (g * _sigmoid(g) * up).astype(BF16)
        acc_ref[rows, :] += jnp.dot(h, wd_ref[...].astype(BF16), preferred_element_type=F32)

    pl.when(active & (fill == TILE_FULL))(functools.partial(swiglu, pl.ds(0, tm)))
    pl.when(active & (fill == TILE_HALF))(functools.partial(swiglu, pl.ds(0, tm // 2)))

    @pl.when(active & (j == last))
    def _():
        o_ref[...] = acc_ref[...].astype(BF16)

    @pl.when(jnp.logical_not(active) & (j == last))
    def _():
        o_ref[...] = jnp.zeros_like(o_ref)


def _expert_ffn(xs, tile_expert, n_active, tile_fill, wg, wu, wd, *, tm, tf):
    n_rows, D = xs.shape
    ff = wg.shape[2]
    assert n_rows % tm == 0 and ff % tf == 0
    nj = ff // tf

    def jj(i, j, na):
        return jnp.where(i < na[0], j, nj - 1)

    def ii(i, na):
        return jnp.minimum(i, na[0] - 1)

    grid_spec = pltpu.PrefetchScalarGridSpec(
        num_scalar_prefetch=3,
        grid=(n_rows // tm, nj),
        in_specs=[pl.BlockSpec((tm, D), lambda i, j, te, na, tf_: (ii(i, na), 0)),
                  pl.BlockSpec((None, D, tf), lambda i, j, te, na, tf_: (te[i], 0, jj(i, j, na))),
                  pl.BlockSpec((None, D, tf), lambda i, j, te, na, tf_: (te[i], 0, jj(i, j, na))),
                  pl.BlockSpec((None, tf, D), lambda i, j, te, na, tf_: (te[i], jj(i, j, na), 0))],
        out_specs=pl.BlockSpec((tm, D), lambda i, j, te, na, tf_: (i, 0)),
        scratch_shapes=[pltpu.VMEM((tm, D), F32)],
    )
    return pl.pallas_call(
        _expert_kernel,
        grid_spec=grid_spec,
        out_shape=jax.ShapeDtypeStruct((n_rows, D), BF16),
        compiler_params=_params(("arbitrary", "arbitrary")),
        name="moe_experts",
    )(tile_expert, n_active, tile_fill, xs, wg, wu, wd)


def _combine_kernel(base_ref, off_ref, cnt_ref, ys_hbm, meta_ref, x_ref, gate_ref, lg_ref, lb_ref, o_ref,
                    stage, extra, acc_ref, sems, extra_sem, *, alpha, n_exp):
    t = pl.program_id(0) * pl.num_programs(1) + pl.program_id(1)
    n_steps = pl.num_programs(0) * pl.num_programs(1)
    tm = x_ref.shape[0]
    R = DISPATCH_ROWS
    slot = t % 2

    def main_copies(step, buf, fn):
        for e in range(n_exp):
            fn(pltpu.make_async_copy(_rows_at(ys_hbm, base_ref[step * n_exp + e], R),
                                     stage.at[buf, pl.ds(e * R, R)], sems.at[buf]))

    @pl.when(t == 0)
    def _():
        main_copies(0, 0, lambda cp: cp.start())

    @pl.when(t + 1 < n_steps)
    def _():
        main_copies(t + 1, 1 - slot, lambda cp: cp.start())

    meta = meta_ref[...]
    col = lambda k: meta[:, k:k + 1]
    e0, e1 = col(META_E0).astype(I32), col(META_E1).astype(I32)
    r0, r1 = col(META_R0).astype(I32), col(META_R1).astype(I32)
    w0, w1 = col(META_W0), col(META_W1)
    lane = lax.broadcasted_iota(I32, (tm, R), 1)

    rels, ws, picks = [], [], []
    for e in range(n_exp):
        rel = jnp.where(e0 == e, r0, jnp.where(e1 == e, r1, -1))
        rels.append(jnp.where(rel >= 0, rel - off_ref[t * n_exp + e], -1))
        ws.append(jnp.where(e0 == e, w0, jnp.where(e1 == e, w1, 0.0)))
        picks.append(jnp.where(lane == rels[e], ws[e], 0.0))
    weighted = jnp.concatenate(picks, axis=1).astype(BF16)
    main_copies(t, slot, lambda cp: cp.wait())
    acc_ref[...] = jnp.dot(weighted, stage[slot], preferred_element_type=F32)

    for e in range(n_exp):
        for blk in range(1, tm // R):
            @pl.when(cnt_ref[t * n_exp + e] > blk * R)
            def _(e=e, blk=blk):
                cp = pltpu.make_async_copy(_rows_at(ys_hbm, base_ref[t * n_exp + e] + blk * R, R),
                                           extra, extra_sem)
                cp.start()
                cp.wait()
                more = jnp.where(lane + blk * R == rels[e], ws[e], 0.0).astype(BF16)
                acc_ref[...] += jnp.dot(more, extra[...], preferred_element_type=F32)

    z = alpha * x_ref[...] + (1.0 + gate_ref[...]) * acc_ref[...]
    o_ref[...] = _layer_norm(z, lg_ref[...], lb_ref[...])


def _combine(ys, base_te, off_te, cnt_te, meta, x, gate, ln_g, ln_b, *, alpha, tm):
    B, S, D = x.shape
    nt = S // tm
    n_exp = base_te.shape[0] // (B * nt)
    row = pl.BlockSpec((None, tm, D), lambda b, i, *_: (b, i, 0))
    vec = pl.BlockSpec((1, D), lambda b, i, *_: (0, 0))
    grid_spec = pltpu.PrefetchScalarGridSpec(
        num_scalar_prefetch=3,
        grid=(B, nt),
        in_specs=[pl.BlockSpec(memory_space=pl.ANY),
                  pl.BlockSpec((None, tm, LANES), lambda b, i, *_: (b, i, 0)),
                  row, pl.BlockSpec((None, 1, D), lambda b, i, *_: (b, 0, 0)), vec, vec],
        out_specs=row,
        scratch_shapes=[pltpu.VMEM((2, n_exp * DISPATCH_ROWS, D), BF16),
                        pltpu.VMEM((DISPATCH_ROWS, D), BF16),
                        pltpu.VMEM((tm, D), F32),
                        pltpu.SemaphoreType.DMA((2,)),
                        pltpu.SemaphoreType.DMA(())],
    )
    return pl.pallas_call(
        functools.partial(_combine_kernel, alpha=alpha, n_exp=n_exp),
        grid_spec=grid_spec,
        out_shape=jax.ShapeDtypeStruct((B, S, D), F32),
        compiler_params=_params(("arbitrary", "arbitrary")),
        name="moe_combine",
    )(base_te, off_te, cnt_te, ys, meta, x, gate, ln_g, ln_b)


def _moe_ffn(x, scale, shift, gate, w_router, wg, wu, wd, ln_g, ln_b, *, alpha, tm, tm_e, tf):
    B, S, D = x.shape
    T = B * S
    nt = T // tm
    n_exp = w_router.shape[1]
    wr = jnp.zeros((D, LANES), F32).at[:, :n_exp].set(w_router)
    wr_hi = wr.astype(BF16)
    wr = jnp.stack([wr_hi, (wr - wr_hi.astype(F32)).astype(BF16)])
    u, meta = _router(x, scale, shift, wr, n_exp=n_exp, tm=tm)

    meta2 = meta.reshape(T, LANES)
    e0 = meta2[:, META_E0].astype(I32)
    e1 = meta2[:, META_E1].astype(I32)
    r0 = meta2[:, META_R0].astype(I32)
    r1 = meta2[:, META_R1].astype(I32)
    experts = jnp.arange(n_exp, dtype=I32)
    chosen = ((e0[:, None] == experts) | (e1[:, None] == experts)).astype(I32)
    cnt_te = chosen.reshape(nt, tm, n_exp).sum(axis=1)
    seg_te = -(-cnt_te // SEGMENT_ALIGN) * SEGMENT_ALIGN
    off_te = jnp.cumsum(cnt_te, axis=0) - cnt_te
    filled = seg_te.sum(axis=0)
    sizes = -(-(filled + DISPATCH_SLACK) // tm_e) * tm_e
    ends = jnp.cumsum(sizes)
    starts = ends - sizes
    base_te = starts[None, :] + jnp.cumsum(seg_te, axis=0) - seg_te
    shift_te = (base_te - off_te)[:, None, :]

    def rows_of(e, r):
        pick = (e[:, None] == experts).astype(I32).reshape(nt, tm, n_exp)
        return r + (pick * shift_te).sum(axis=-1).reshape(T)

    d0, d1 = rows_of(e0, r0), rows_of(e1, r1)
    bound = TOP_K * T + n_exp * ((SEGMENT_ALIGN - 1) * nt + DISPATCH_SLACK)
    n_tiles = -(-bound // tm_e) + n_exp + 1
    n_active = (ends[-1] // tm_e).astype(I32)
    tile_id = jnp.minimum(jnp.arange(n_tiles, dtype=I32), n_active - 1)
    tile_expert = jnp.sum((ends[None, :] <= (tile_id * tm_e)[:, None]).astype(I32), axis=1)
    data_end = jnp.sum((tile_expert[:, None] == experts) * (starts + filled)[None, :], axis=1)
    left = data_end - tile_id * tm_e
    tile_fill = jnp.where(left <= 0, TILE_EMPTY, jnp.where(left <= tm_e // 2, TILE_HALF, TILE_FULL)).astype(I32)
    meta_t = jnp.stack([e0, e1, d0, d1] + [jnp.zeros_like(e0)] * 4)

    xs = _dispatch(u.reshape(T, D), meta_t, base_te.reshape(-1), cnt_te.reshape(-1),
                   jnp.concatenate([starts + filled, ends[-1:]]),
                   jnp.concatenate([sizes - filled, n_tiles * tm_e - ends[-1:]]),
                   n_rows=n_tiles * tm_e, tm=tm)
    ys = _expert_ffn(xs, tile_expert, n_active.reshape(1), tile_fill, wg, wu, wd, tm=tm_e, tf=tf)
    return _combine(ys, base_te.reshape(-1), off_te.reshape(-1), cnt_te.reshape(-1),
                    meta, x, gate, ln_g, ln_b, alpha=alpha, tm=tm)


class _Tiles(NamedTuple):
    token_rows: int
    rope_rows: int
    ffn_chunk: int
    expert_rows: int
    expert_ff: int


def _tile_plan(batch, seq):
    return _Tiles(token_rows=min(512, seq), rope_rows=min(1024, seq), ffn_chunk=2 * LANES,
                  expert_rows=min(1024, batch * seq), expert_ff=512)


def kernel(x, c, positions, w_in, w_out, attn_norm_gain, rec_norm_gain, rec_lb_logits, ada_w, ada_b,
           ln_gain, ln_bias, ffn_w_gate, ffn_w_up, ffn_w_down, moe_router, moe_w_gate, moe_w_up,
           moe_w_down):
    B, S, D = x.shape
    depth = w_in.shape[0]
    alpha = (2 * depth) ** 0.25

    p = jax.nn.softmax(rec_lb_logits.astype(F32), axis=0)
    cum = jnp.cumsum(p, axis=0)
    lb_all = cum - cum[0:1]
    half = ATTN_HEAD_DIM // 2
    inv_freq = ROPE_THETA ** (-jnp.arange(half, dtype=F32) / half)
    inv_freq_lanes = jnp.tile(inv_freq, LANES // half).reshape(1, LANES)
    pos_f = positions.astype(F32).reshape(B, S, 1)
    n_qk = 2 * ATTN_WIDTH
    w_in_b = jnp.concatenate([_qk_lane_order(w_in[..., :n_qk]), w_in[..., n_qk:]], axis=-1).astype(BF16)
    w_out_b = w_out.astype(BF16)
    ff = ffn_w_gate.shape[2]
    ffp = -(-ff // (2 * LANES)) * (2 * LANES)
    pad_c = lambda w: jnp.pad(w.astype(BF16), ((0, 0), (0, 0), (0, ffp - ff)))
    ffn_g, ffn_u = pad_c(ffn_w_gate), pad_c(ffn_w_up)
    ffn_d = jnp.pad(ffn_w_down.astype(BF16), ((0, 0), (0, ffp - ff), (0, 0)))

    mods = _ada_modulation(c, ada_w, ada_b)
    tiles = _tile_plan(B, S)
    rope_cos, rope_sin = _rope_tables(pos_f, inv_freq_lanes, tm=tiles.rope_rows)

    def mod(layer, sub):
        m = mods[layer * 2 + sub].reshape(3, B, 1, D)
        return m[0], m[1], m[2]

    vec = lambda a: a.reshape(1, -1)
    for layer in range(depth):
        shift, scale, gate = mod(layer, 0)
        proj = _in_projection(x, scale, shift, rope_cos, rope_sin, w_in_b, layer, tm=tiles.token_rows)
        attn = _dilated_attention(proj)
        rec = _hgrn2(proj, lb_all[layer], vec(rec_norm_gain[layer]))
        x = _out_projection(attn, rec, x, gate, vec(attn_norm_gain[layer]), w_out_b, layer,
                            vec(ln_gain[layer, 0]), vec(ln_bias[layer, 0]), alpha=alpha,
                            tm=tiles.token_rows)
        shift, scale, gate = mod(layer, 1)
        j = layer // 2
        if layer % 2 == 0:
            x = _dense_ffn(x, scale, shift, gate, ffn_g[j], ffn_u[j], ffn_d[j],
                           vec(ln_gain[layer, 1]), vec(ln_bias[layer, 1]),
                           alpha=alpha, tm=tiles.token_rows, tc=tiles.ffn_chunk)
        else:
            x = _moe_ffn(x, scale, shift, gate, moe_router[j], moe_w_gate[j], moe_w_up[j],
                         moe_w_down[j], vec(ln_gain[layer, 1]), vec(ln_bias[layer, 1]),
                         alpha=alpha, tm=tiles.token_rows, tm_e=tiles.expert_rows, tf=tiles.expert_ff)
    return x
```

```python
import functools
from typing import NamedTuple

import jax
import jax.numpy as jnp
from jax import lax
from jax.experimental import pallas as pl
from jax.experimental.pallas import tpu as pltpu

F32 = jnp.float32
BF16 = jnp.bfloat16
I32 = jnp.int32

LANES = 128
V7X_VMEM_LIMIT_BYTES = 56 * 1024 * 1024

ATTN_WIDTH = 512
ATTN_HEAD_DIM = 64
DILATIONS = (1, 4, 16)
ATTN_RADIUS = 64
ROPE_THETA = 10000.0
REC_WIDTH = 512
REC_DIM = 128
REC_CHUNK = 64
TOP_K = 2
LN_EPS = 1e-5
RMS_EPS = 1e-6
MASK_VALUE = -1e30
DECAY_EXP_CLAMP = 80.0
LOG2_E = 1.4426950408889634


def _params(semantics):
    return pltpu.CompilerParams(dimension_semantics=semantics,
                                vmem_limit_bytes=V7X_VMEM_LIMIT_BYTES)


def _sigmoid(x):
    return 1.0 / (1.0 + jnp.exp(-x))


def _layer_norm(z, gain, bias):
    mu = jnp.mean(z, axis=-1, keepdims=True)
    zc = z - mu
    var = jnp.mean(zc * zc, axis=-1, keepdims=True)
    return zc * lax.rsqrt(var + LN_EPS) * gain + bias


def _ada_kernel(c_ref, w_ref, b_ref, o_ref):
    c = c_ref[...]
    o_ref[...] = jnp.dot(c * _sigmoid(c), w_ref[...], precision=lax.Precision.HIGHEST,
                         preferred_element_type=F32) + b_ref[...]


def _ada_modulation(c, ada_w, ada_b):
    B, D = c.shape
    n = ada_w.shape[0] * ada_w.shape[1]
    n3 = ada_w.shape[-1]
    tn = D
    return pl.pallas_call(
        _ada_kernel,
        grid=(n, n3 // tn),
        in_specs=[pl.BlockSpec((B, D), lambda i, j: (0, 0)),
                  pl.BlockSpec((None, D, tn), lambda i, j: (i, 0, j)),
                  pl.BlockSpec((None, 1, tn), lambda i, j: (i, 0, j))],
        out_specs=pl.BlockSpec((None, None, B, tn), lambda i, j: (i, j, 0, 0)),
        out_shape=jax.ShapeDtypeStruct((n, n3 // tn, B, tn), F32),
        compiler_params=_params(("arbitrary", "arbitrary")),
        name="ada_modulation",
    )(c, ada_w.reshape(n, D, n3), ada_b.reshape(n, 1, n3))


ROPE_HALF = ATTN_HEAD_DIM // 2


def _qk_lane_order(w_cols):
    lead = w_cols.shape[:-1]
    w = w_cols.reshape(*lead, -1, 2, 2, ROPE_HALF)
    return jnp.swapaxes(w, -3, -2).reshape(*lead, -1)


def _rope_kernel(pos_ref, invf_ref, cos_ref, sin_ref):
    ang = pos_ref[...] * invf_ref[...]
    lane = lax.broadcasted_iota(I32, (1, LANES), 1)
    sin = jnp.sin(ang)
    cos_ref[...] = jnp.cos(ang)
    sin_ref[...] = jnp.where(lane < LANES // 2, -sin, sin)


def _rope_tables(pos_f, inv_freq_lanes, *, tm):
    B, S, _ = pos_f.shape
    tab = pl.BlockSpec((None, tm, LANES), lambda b, i: (b, i, 0))
    return pl.pallas_call(
        _rope_kernel,
        grid=(B, S // tm),
        in_specs=[pl.BlockSpec((None, tm, 1), lambda b, i: (b, i, 0)),
                  pl.BlockSpec((1, LANES), lambda b, i: (0, 0))],
        out_specs=[tab, tab],
        out_shape=[jax.ShapeDtypeStruct((B, S, LANES), F32)] * 2,
        compiler_params=_params(("arbitrary", "arbitrary")),
        name="rope_tables",
    )(pos_f, inv_freq_lanes)


def _inproj_kernel(x_ref, sc_ref, sh_ref, cos_ref, sin_ref, w_ref, o_ref, *, tn):
    u = (x_ref[...] * (1.0 + sc_ref[...]) + sh_ref[...]).astype(BF16)
    cos = cos_ref[...]
    sin_signed = sin_ref[...]
    q_scale = ATTN_HEAD_DIM ** -0.5
    for j in range(w_ref.shape[1] // tn):
        acc = jnp.dot(u, w_ref[:, j * tn:(j + 1) * tn], preferred_element_type=F32)
        if j > 0:
            o_ref[:, j * tn:(j + 1) * tn] = acc
            continue
        for kk in range(tn // LANES):
            c = acc[:, kk * LANES:(kk + 1) * LANES]
            r = c * cos + pltpu.roll(c, LANES // 2, 1) * sin_signed
            if kk * LANES < ATTN_WIDTH:
                r = r * q_scale
            o_ref[:, kk * LANES:(kk + 1) * LANES] = r


def _in_projection(x, scale, shift, rope_cos, rope_sin, w_in_bf16, layer, *, tm):
    B, S, D = x.shape
    n_cols = w_in_bf16.shape[2]
    tn = 2 * ATTN_WIDTH
    assert S % tm == 0 and n_cols % tn == 0
    return pl.pallas_call(
        functools.partial(_inproj_kernel, tn=tn),
        grid=(B, S // tm),
        in_specs=[pl.BlockSpec((None, tm, D), lambda b, i: (b, i, 0)),
                  pl.BlockSpec((None, 1, D), lambda b, i: (b, 0, 0)),
                  pl.BlockSpec((None, 1, D), lambda b, i: (b, 0, 0)),
                  pl.BlockSpec((None, tm, LANES), lambda b, i: (b, i, 0)),
                  pl.BlockSpec((None, tm, LANES), lambda b, i: (b, i, 0)),
                  pl.BlockSpec((None, D, n_cols), lambda b, i: (layer, 0, 0))],
        out_specs=pl.BlockSpec((None, tm, n_cols), lambda b, i: (b, i, 0)),
        out_shape=jax.ShapeDtypeStruct((B, S, n_cols), F32),
        compiler_params=_params(("arbitrary", "arbitrary")),
        name="in_projection",
    )(x, scale, shift, rope_cos, rope_sin, w_in_bf16)


ATTN_BLOCK_GROUP = 16


def _attn_kernel(q_ref, k_ref, v_ref, o_ref, qs, ks, vs, qf, kf, vf, pm, plr, pa, nm, nl, na, bias_s,
                 *, S):
    lane = lax.broadcasted_iota(I32, (1, LANES), 1)
    head0 = lane < ATTN_HEAD_DIM
    head0_qk = (lane % ATTN_HEAD_DIM) < ROPE_HALF

    for p, d in enumerate(DILATIONS):
        L = S // d
        tq = min(128, L)
        W = min(2 * tq, L)
        nb = L // tq

        d_prev = DILATIONS[p - 1] if p else 1
        ratio, l_prev = d // d_prev, S // d_prev
        keep_f32 = 0 < p < len(DILATIONS) - 1
        for src, stage, dst in ((q_ref, qf, qs), (k_ref, kf, ks), (v_ref, vf, vs)):
            src = src if p <= 1 else stage
            for r in range(d):
                a, r_prev = divmod(r, d_prev)
                rows = pl.ds(r_prev * l_prev + a, L, stride=ratio) if ratio > 1 else pl.ds(0, L)
                val = src[rows, :]
                if keep_f32:
                    stage[r * L:(r + 1) * L, :] = val
                dst[r * L:(r + 1) * L, :] = val.astype(BF16)

        n_blocks = d * nb
        rc = lax.broadcasted_iota(I32, (tq, W), 0) - lax.broadcasted_iota(I32, (tq, W), 1)
        for which, delta in enumerate((0, ATTN_RADIUS, tq)):
            bias_s[which, pl.ds(0, tq), pl.ds(0, W)] = jnp.where(
                jnp.abs(rc + delta) <= ATTN_RADIUS, 0.0, MASK_VALUE)
        group = min(ATTN_BLOCK_GROUP, n_blocks)
        assert n_blocks % group == 0

        def block_group(gi, carry, L=L, tq=tq, W=W, nb=nb, group=group):
            q0s, k0s, scores, probs, outs, sums = [], [], [], [], [], []
            for u in range(group):
                g = gi * group + u
                r = g // nb
                n = g - r * nb
                base = r * L
                q0 = pl.multiple_of(base + n * tq, tq)
                ws = jnp.clip(n * tq - ATTN_RADIUS, 0, L - W)
                k0 = pl.multiple_of(base + ws, 16)
                qb = qs[pl.ds(q0, tq), :]
                zero = jnp.zeros_like(qb)
                q2 = jnp.concatenate([jnp.where(head0_qk, qb, zero), jnp.where(head0_qk, zero, qb)], axis=0)
                s = lax.dot_general(q2, ks[pl.ds(k0, W), :], (((1,), (1,)), ((), ())),
                                    preferred_element_type=F32)
                delta = n * tq - ws
                which = jnp.where(delta == 0, 0, jnp.where(delta == ATTN_RADIUS, 1, 2))
                bias = bias_s[which, pl.ds(0, tq), pl.ds(0, W)]
                scores.append(s + jnp.concatenate([bias, bias], axis=0))
                q0s.append(q0)
                k0s.append(k0)
            for u in range(group):
                s = scores[u]
                m = jnp.max(s, axis=1, keepdims=True)
                e = jnp.exp(s - m)
                probs.append((m, e.astype(BF16)))
                sums.append(jnp.sum(e, axis=1, keepdims=True))
            for u in range(group):
                outs.append(jnp.dot(probs[u][1], vs[pl.ds(k0s[u], W), :], preferred_element_type=F32))
            for u in range(group):
                m, l = probs[u][0], sums[u]
                rows = pl.ds(q0s[u], tq)
                out_m[rows, :] = jnp.where(head0, m[:tq], m[tq:])
                out_l[rows, :] = jnp.where(head0, l[:tq], l[tq:])
                out_a[rows, :] = jnp.where(head0, outs[u][:tq], outs[u][tq:])
            return carry

        out_m, out_l, out_a = (nm.at[p], nl.at[p], na.at[p]) if d == 1 else (pm, plr, pa)
        lax.fori_loop(0, n_blocks // group, block_group, 0)

        for r in range(d if d > 1 else 0):
            rows = pl.ds(r, L, stride=d)
            nm[p, rows, :] = pm[r * L:(r + 1) * L, :]
            nl[p, rows, :] = plr[r * L:(r + 1) * L, :]
            na[p, rows, :] = pa[r * L:(r + 1) * L, :]

    m_all = jnp.maximum(jnp.maximum(nm[0], nm[1]), nm[2])
    num = jnp.zeros((S, LANES), F32)
    den = jnp.zeros((S, LANES), F32)
    for p in range(len(DILATIONS)):
        w = jnp.exp(nm[p] - m_all)
        num = num + w * na[p]
        den = den + w * nl[p]
    o_ref[...] = (num / den).astype(o_ref.dtype)


def _dilated_attention(proj):
    B, S, _ = proj.shape
    n_pairs = ATTN_WIDTH // LANES
    assert S % (16 * DILATIONS[-1]) == 0
    blk = lambda off: pl.BlockSpec((None, S, LANES), lambda b, h: (b, 0, off + h))
    return pl.pallas_call(
        functools.partial(_attn_kernel, S=S),
        grid=(B, n_pairs),
        in_specs=[blk(0), blk(n_pairs), blk(2 * n_pairs)],
        out_specs=pl.BlockSpec((None, S, LANES), lambda b, h: (b, 0, h)),
        out_shape=jax.ShapeDtypeStruct((B, S, ATTN_WIDTH), BF16),
        scratch_shapes=[pltpu.VMEM((S, LANES), BF16)] * 3
                       + [pltpu.VMEM((S, LANES), F32)] * 6
                       + [pltpu.VMEM((len(DILATIONS), S, LANES), F32)] * 3
                       + [pltpu.VMEM((3, min(128, S), min(256, S)), F32)],
        compiler_params=_params(("arbitrary", "arbitrary")),
        name="dilated_attention",
    )(proj, proj, proj)


REC_GROUP_CHUNKS = 4
REC_GROUP_ROWS = REC_GROUP_CHUNKS * REC_CHUNK
REC_STAGE_GROUPS = 8


def _hgrn_kernel(rq_ref, zf_ref, zb_ref, ri_ref, rg_ref, lb_ref, gain_ref, o_ref,
                 q_s, vt_s, oi_s, qp_s, ut_s, dd_s, st_s, os_s, *, S):
    C, G, GR = REC_CHUNK, REC_GROUP_CHUNKS, REC_GROUP_ROWS
    nc, ng = S // C, S // GR
    sg = min(REC_STAGE_GROUPS, ng)
    assert ng % sg == 0

    row = lax.broadcasted_iota(I32, (GR, GR), 0)
    col = lax.broadcasted_iota(I32, (GR, GR), 1)
    same_chunk = (row // C) == (col // C)
    t_i = lax.broadcasted_iota(I32, (C, C), 0)
    s_i = lax.broadcasted_iota(I32, (C, C), 1)

    def prep(i, carry):
        r0 = pl.multiple_of(i * GR, GR)
        rq = rq_ref[pl.ds(r0, GR), :]
        q_s[pl.ds(r0, GR), :] = rq * _sigmoid(rq)
        v = ri_ref[pl.ds(r0, GR), :]
        for c in range(G):
            vt_s[i * G + c] = v[c * C:(c + 1) * C, :].T.astype(BF16)
        return carry

    lax.fori_loop(0, ng, prep, 0, unroll=True)

    for direction in range(2):
        fwd = direction == 0
        z_ref = zf_ref if fwd else zb_ref
        cum = jnp.where(same_chunk & ((col <= row) if fwd else (col >= row)), 1.0, 0.0).astype(BF16)
        keep = (s_i <= t_i) if fwd else (s_i >= t_i)
        mid_row = C // 2 - 1 if fwd else C // 2
        last_row = C - 1 if fwd else 0

        def phase_a(it, carry, fwd=fwd, z_ref=z_ref, cum=cum, keep=keep,
                    mid_row=mid_row, last_row=last_row, direction=direction):
            lb = lb_ref[direction:direction + 1, :]
            log_lb = jnp.log(lb)
            log_1m_lb = jnp.log1p(-lb)
            r0s, c0s, k3s, bcs, atts, stage3 = [], [], [], [], [], []
            for u in range(sg):
                i = it * sg + u
                r0 = pl.multiple_of(i * GR, GR)
                z = z_ref[pl.ds(r0, GR), :]
                e = jnp.exp(-jnp.abs(z))
                t = log_1m_lb + (jnp.minimum(z, 0.0) - jnp.log(1.0 + e))
                g = jnp.maximum(log_lb, t) + jnp.log(1.0 + jnp.exp(-jnp.abs(log_lb - t)))
                kk = (1.0 - lb) * (jnp.where(z > 0, e, 1.0) / (1.0 + e))
                g = g * LOG2_E
                g1 = g.astype(BF16)
                r1 = g - g1.astype(F32)
                g2 = r1.astype(BF16)
                g3 = (r1 - g2.astype(F32)).astype(BF16)
                bcs.append(jnp.dot(cum, g1, preferred_element_type=F32)
                           + jnp.dot(cum, g2, preferred_element_type=F32)
                           + jnp.dot(cum, g3, preferred_element_type=F32))
                k3s.append(kk.reshape(G, C, REC_DIM))
                r0s.append(r0)
                c0s.append(pl.multiple_of(i * G, G))
            for u in range(sg):
                b3 = bcs[u].reshape(G, C, REC_DIM)
                mid = b3[:, mid_row:mid_row + 1, :]
                last = b3[:, last_row:last_row + 1, :]
                e3 = b3 - mid
                q3 = q_s[pl.ds(r0s[u], GR), :].reshape(G, C, REC_DIM)
                clamp = DECAY_EXP_CLAMP * LOG2_E
                qt = (q3 * jnp.exp2(jnp.minimum(e3, clamp))).astype(BF16)
                kt = (k3s[u] * jnp.exp2(jnp.minimum(-e3, clamp))).astype(BF16)
                atts.append(jnp.einsum('gtc,gsc->gts', qt, kt, preferred_element_type=F32))
                qp_s[pl.ds(r0s[u], GR), :] = (q3 * jnp.exp2(b3)).astype(BF16).reshape(GR, REC_DIM)
                stage3.append(((k3s[u] * jnp.exp2(last - b3)).astype(BF16), jnp.exp2(last)))
            for u in range(sg):
                att = jnp.where(keep[None], atts[u], 0.0).astype(BF16)
                v3 = ri_ref[pl.ds(r0s[u], GR), :].reshape(G, C, REC_DIM).astype(BF16)
                oi = jnp.einsum('gts,gsv->gtv', att, v3, preferred_element_type=F32)
                oi_s[pl.ds(r0s[u], GR), :] = oi.reshape(GR, REC_DIM)
                kp, decay = stage3[u]
                ut_s[pl.ds(c0s[u], G)] = jnp.einsum('gvs,gsc->gvc', vt_s[pl.ds(c0s[u], G)], kp,
                                                    preferred_element_type=F32)
                dd_s[pl.ds(c0s[u], G)] = decay
            return carry

        lax.fori_loop(0, ng // sg, phase_a, 0)

        def scan(c, st, fwd=fwd):
            idx = c if fwd else nc - 1 - c
            st_s[idx] = st.astype(BF16)
            return dd_s[idx] * st + ut_s[idx]

        lax.fori_loop(0, nc, scan, jnp.zeros((REC_DIM, REC_DIM), F32), unroll=True)

        def phase_c(it, carry, fwd=fwd):
            r0s = [pl.multiple_of((it * sg + u) * GR, GR) for u in range(sg)]
            oos = []
            for u in range(sg):
                c0 = pl.multiple_of((it * sg + u) * G, G)
                qp3 = qp_s[pl.ds(r0s[u], GR), :].reshape(G, C, REC_DIM)
                oos.append(jnp.einsum('gtc,gvc->gtv', qp3, st_s[pl.ds(c0, G)],
                                      preferred_element_type=F32))
            for u in range(sg):
                rows = pl.ds(r0s[u], GR)
                tot = oi_s[rows, :] + oos[u].reshape(GR, REC_DIM)
                os_s[rows, :] = tot if fwd else os_s[rows, :] + tot
            return carry

        lax.fori_loop(0, ng // sg, phase_c, 0)

    o = os_s[...]
    o = o * lax.rsqrt(jnp.mean(o * o, axis=-1, keepdims=True) + RMS_EPS) * gain_ref[...]
    o_ref[...] = (o * _sigmoid(rg_ref[...])).astype(o_ref.dtype)


def _hgrn2(proj, lb, rec_gain):
    B, S, _ = proj.shape
    nh = REC_WIDTH // REC_DIM
    assert S % REC_GROUP_ROWS == 0
    col0 = 3 * ATTN_WIDTH // LANES
    blk = lambda k: pl.BlockSpec((None, S, REC_DIM), lambda b, h: (b, 0, col0 + k * nh + h))
    nc = S // REC_CHUNK
    return pl.pallas_call(
        functools.partial(_hgrn_kernel, S=S),
        grid=(B, nh),
        in_specs=[blk(0), blk(1), blk(2), blk(3), blk(4),
                  pl.BlockSpec((2, REC_DIM), lambda b, h: (0, h)),
                  pl.BlockSpec((1, REC_DIM), lambda b, h: (0, h))],
        out_specs=pl.BlockSpec((None, S, REC_DIM), lambda b, h: (b, 0, h)),
        out_shape=jax.ShapeDtypeStruct((B, S, REC_WIDTH), BF16),
        scratch_shapes=[pltpu.VMEM((S, REC_DIM), F32),
                        pltpu.VMEM((nc, REC_DIM, REC_CHUNK), BF16),
                        pltpu.VMEM((S, REC_DIM), F32),
                        pltpu.VMEM((S, REC_DIM), BF16),
                        pltpu.VMEM((nc, REC_DIM, REC_DIM), F32),
                        pltpu.VMEM((nc, 1, REC_DIM), F32),
                        pltpu.VMEM((nc, REC_DIM, REC_DIM), BF16),
                        pltpu.VMEM((S, REC_DIM), F32)],
        compiler_params=_params(("arbitrary", "arbitrary")),
        name="hgrn2",
    )(proj, proj, proj, proj, proj, lb, rec_gain)


def _outproj_kernel(a_ref, r_ref, x_ref, gate_ref, ag_ref, w_ref, lg_ref, lb_ref, o_ref, *, alpha):
    a = a_ref[...].astype(F32)
    an = a * lax.rsqrt(jnp.mean(a * a, axis=-1, keepdims=True) + RMS_EPS) * ag_ref[...]
    y = (jnp.dot(an.astype(BF16), w_ref[:ATTN_WIDTH, :], preferred_element_type=F32)
         + jnp.dot(r_ref[...], w_ref[ATTN_WIDTH:, :], preferred_element_type=F32))
    z = alpha * x_ref[...] + (1.0 + gate_ref[...]) * y
    o_ref[...] = _layer_norm(z, lg_ref[...], lb_ref[...])


def _out_projection(attn, rec, x, gate, attn_gain, w_out_bf16, layer, ln_g, ln_b, *, alpha, tm):
    B, S, D = x.shape
    row = lambda w: pl.BlockSpec((None, tm, w), lambda b, i: (b, i, 0))
    vec = lambda w: pl.BlockSpec((1, w), lambda b, i: (0, 0))
    return pl.pallas_call(
        functools.partial(_outproj_kernel, alpha=alpha),
        grid=(B, S // tm),
        in_specs=[row(ATTN_WIDTH), row(REC_WIDTH), row(D),
                  pl.BlockSpec((None, 1, D), lambda b, i: (b, 0, 0)),
                  vec(ATTN_WIDTH),
                  pl.BlockSpec((None, ATTN_WIDTH + REC_WIDTH, D), lambda b, i: (layer, 0, 0)),
                  vec(D), vec(D)],
        out_specs=row(D),
        out_shape=jax.ShapeDtypeStruct((B, S, D), F32),
        compiler_params=_params(("arbitrary", "arbitrary")),
        name="out_projection",
    )(attn, rec, x, gate, attn_gain, w_out_bf16, ln_g, ln_b)


def _ffn_kernel(x_ref, sc_ref, sh_ref, gate_ref, wg_ref, wu_ref, wd_ref, lg_ref, lb_ref, o_ref,
                *, alpha, tc):
    x = x_ref[...]
    u = (x * (1.0 + sc_ref[...]) + sh_ref[...]).astype(BF16)
    acc = jnp.zeros(x.shape, F32)
    for c in range(wg_ref.shape[1] // tc):
        cols = slice(c * tc, (c + 1) * tc)
        g = jnp.dot(u, wg_ref[:, cols], preferred_element_type=F32)
        up = jnp.dot(u, wu_ref[:, cols], preferred_element_type=F32)
        h = (g * _sigmoid(g) * up).astype(BF16)
        acc = acc + jnp.dot(h, wd_ref[cols, :], preferred_element_type=F32)
    z = alpha * x + (1.0 + gate_ref[...]) * acc
    o_ref[...] = _layer_norm(z, lg_ref[...], lb_ref[...])


def _dense_ffn(x, scale, shift, gate, wg, wu, wd, ln_g, ln_b, *, alpha, tm, tc):
    B, S, D = x.shape
    ffp = wg.shape[1]
    assert ffp % tc == 0
    row = pl.BlockSpec((None, tm, D), lambda b, i: (b, i, 0))
    mod = pl.BlockSpec((None, 1, D), lambda b, i: (b, 0, 0))
    vec = pl.BlockSpec((1, D), lambda b, i: (0, 0))
    return pl.pallas_call(
        functools.partial(_ffn_kernel, alpha=alpha, tc=tc),
        grid=(B, S // tm),
        in_specs=[row, mod, mod, mod,
                  pl.BlockSpec((D, ffp), lambda b, i: (0, 0)),
                  pl.BlockSpec((D, ffp), lambda b, i: (0, 0)),
                  pl.BlockSpec((ffp, D), lambda b, i: (0, 0)),
                  vec, vec],
        out_specs=row,
        out_shape=jax.ShapeDtypeStruct((B, S, D), F32),
        compiler_params=_params(("arbitrary", "arbitrary")),
        name="dense_ffn",
    )(x, scale, shift, gate, wg, wu, wd, ln_g, ln_b)


META_E0, META_E1, META_W0, META_W1, META_R0, META_R1 = range(6)
SEGMENT_ALIGN = 16
DISPATCH_ROWS = 192
DISPATCH_HALF = DISPATCH_ROWS // 2
DISPATCH_SLACK = DISPATCH_ROWS


def _router_kernel(x_ref, sc_ref, sh_ref, wr_ref, u_ref, meta_ref, carry_ref, *, n_exp):
    tm = x_ref.shape[0]

    @pl.when((pl.program_id(0) == 0) & (pl.program_id(1) == 0))
    def _():
        carry_ref[...] = jnp.zeros_like(carry_ref)

    u = x_ref[...] * (1.0 + sc_ref[...]) + sh_ref[...]
    u_ref[...] = u.astype(BF16)
    u_hi = u.astype(BF16)
    u_lo = (u - u_hi.astype(F32)).astype(BF16)
    logits = (jnp.dot(u_hi, wr_ref[0], preferred_element_type=F32)
              + jnp.dot(u_lo, wr_ref[0], preferred_element_type=F32)
              + jnp.dot(u_hi, wr_ref[1], preferred_element_type=F32))
    lane = lax.broadcasted_iota(I32, (tm, LANES), 1)
    neg = -jnp.inf
    l1 = jnp.where(lane < n_exp, logits, neg)
    m1 = jnp.max(l1, axis=1, keepdims=True)
    i1 = jnp.min(jnp.where(l1 == m1, lane, LANES), axis=1, keepdims=True)
    l2 = jnp.where(lane == i1, neg, l1)
    m2 = jnp.max(l2, axis=1, keepdims=True)
    i2 = jnp.min(jnp.where(l2 == m2, lane, LANES), axis=1, keepdims=True)
    e = jnp.exp(m2 - m1)
    w1 = 1.0 / (1.0 + e)
    w2 = e * w1
    sel = jnp.where((lane == i1) | (lane == i2), 1.0, 0.0)
    before = (lax.broadcasted_iota(I32, (tm, tm), 1) < lax.broadcasted_iota(I32, (tm, tm), 0))
    ranks = jnp.dot(jnp.where(before, 1.0, 0.0).astype(BF16), sel.astype(BF16),
                    preferred_element_type=F32) + carry_ref[...]
    r1 = jnp.sum(jnp.where(lane == i1, ranks, 0.0), axis=1, keepdims=True)
    r2 = jnp.sum(jnp.where(lane == i2, ranks, 0.0), axis=1, keepdims=True)
    carry_ref[...] = carry_ref[...] + jnp.sum(sel, axis=0, keepdims=True)
    meta = jnp.zeros((tm, LANES), F32)
    for k, val in ((META_E0, i1.astype(F32)), (META_E1, i2.astype(F32)), (META_W0, w1),
                   (META_W1, w2), (META_R0, r1), (META_R1, r2)):
        meta = jnp.where(lane == k, val, meta)
    meta_ref[...] = meta


def _router(x, scale, shift, w_router_lanes, *, n_exp, tm):
    B, S, D = x.shape
    nt = S // tm
    row = pl.BlockSpec((None, tm, D), lambda b, i: (b, i, 0))
    mod = pl.BlockSpec((None, 1, D), lambda b, i: (b, 0, 0))
    return pl.pallas_call(
        functools.partial(_router_kernel, n_exp=n_exp),
        grid=(B, nt),
        in_specs=[row, mod, mod, pl.BlockSpec((2, D, LANES), lambda b, i: (0, 0, 0))],
        out_specs=[row, pl.BlockSpec((None, tm, LANES), lambda b, i: (b, i, 0))],
        out_shape=[jax.ShapeDtypeStruct((B, S, D), BF16),
                   jax.ShapeDtypeStruct((B, S, LANES), F32)],
        scratch_shapes=[pltpu.VMEM((1, LANES), F32)],
        compiler_params=_params(("arbitrary", "arbitrary")),
        name="moe_router",
    )(x, scale, shift, w_router_lanes)


def _rows_at(hbm, first_row, n_rows):
    return hbm.at[pl.ds(pl.multiple_of(first_row, SEGMENT_ALIGN), n_rows)]


def _dispatch_kernel(base_ref, cnt_ref, padlo_ref, padn_ref, u_ref, meta_ref, xs_hbm,
                     stage, extra, zeros_buf, sems, extra_sem, *, n_exp):
    t = pl.program_id(0)
    last = pl.num_programs(0) - 1
    tm = u_ref.shape[0]
    R, H = DISPATCH_ROWS, DISPATCH_HALF
    slot = t % 2
    u = u_ref[...]
    e0, e1 = meta_ref[0:1, :], meta_ref[1:2, :]
    d0, d1 = meta_ref[2:3, :], meta_ref[3:4, :]
    srow = lax.broadcasted_iota(I32, (R, tm), 0)

    def main_copies(step, buf, fn):
        for e in range(n_exp):
            base = base_ref[step * n_exp + e]
            fn(pltpu.make_async_copy(stage.at[buf, pl.ds(e * R, H)], _rows_at(xs_hbm, base, H),
                                     sems.at[buf]))

            @pl.when(cnt_ref[step * n_exp + e] > H)
            def _(e=e, base=base):
                fn(pltpu.make_async_copy(stage.at[buf, pl.ds(e * R + H, H)],
                                         _rows_at(xs_hbm, base + H, H), sems.at[buf]))

    rels = []
    for e in range(n_exp):
        rel = jnp.where(e0 == e, d0, jnp.where(e1 == e, d1, -1))
        rels.append(jnp.where(rel >= 0, rel - base_ref[t * n_exp + e], -1))
    onehot = jnp.concatenate([jnp.where(srow == rel, 1.0, 0.0) for rel in rels], axis=0).astype(BF16)
    stage[slot] = jnp.dot(onehot, u, preferred_element_type=F32).astype(BF16)

    @pl.when(t > 0)
    def _():
        main_copies(t - 1, 1 - slot, lambda cp: cp.wait())
    main_copies(t, slot, lambda cp: cp.start())

    for e in range(n_exp):
        for blk in range(1, -(-tm // R)):
            @pl.when(cnt_ref[t * n_exp + e] > blk * R)
            def _(e=e, blk=blk):
                more = jnp.where(srow + blk * R == rels[e], 1.0, 0.0).astype(BF16)
                extra[...] = jnp.dot(more, u, preferred_element_type=F32).astype(BF16)
                cp = pltpu.make_async_copy(extra, _rows_at(xs_hbm, base_ref[t * n_exp + e] + blk * R, R),
                                           extra_sem)
                cp.start()
                cp.wait()

    @pl.when(t == last)
    def _():
        main_copies(t, slot, lambda cp: cp.wait())
        zeros_buf[...] = jnp.zeros_like(zeros_buf)
        pieces = [SEGMENT_ALIGN << b for b in range(((R - 1) // SEGMENT_ALIGN).bit_length())]

        def pad_copies(fn):
            for p in range(padlo_ref.shape[0]):
                lo, n = padlo_ref[p], padn_ref[p]
                whole = n // R

                def body(k, carry, lo=lo):
                    fn(pltpu.make_async_copy(zeros_buf, _rows_at(xs_hbm, lo + k * R, R), extra_sem))
                    return carry
                lax.fori_loop(0, whole, body, 0)
                rem = n - whole * R
                for sz in pieces:
                    off = lo + whole * R + (rem // (2 * sz)) * (2 * sz)
                    pl.when((rem // sz) % 2 == 1)(functools.partial(
                        lambda off, sz: fn(pltpu.make_async_copy(
                            zeros_buf.at[pl.ds(0, sz)], _rows_at(xs_hbm, off, sz), extra_sem)), off, sz))

        pad_copies(lambda cp: cp.start())
        pad_copies(lambda cp: cp.wait())


def _dispatch(u, meta_t, base_te, cnt_te, pad_lo, pad_n, *, n_rows, tm):
    T, D = u.shape
    n_exp = pad_lo.shape[0] - 1
    grid_spec = pltpu.PrefetchScalarGridSpec(
        num_scalar_prefetch=4,
        grid=(T // tm,),
        in_specs=[pl.BlockSpec((tm, D), lambda t, *_: (t, 0)),
                  pl.BlockSpec((meta_t.shape[0], tm), lambda t, *_: (0, t))],
        out_specs=pl.BlockSpec(memory_space=pl.ANY),
        scratch_shapes=[pltpu.VMEM((2, n_exp * DISPATCH_ROWS, D), BF16),
                        pltpu.VMEM((DISPATCH_ROWS, D), BF16),
                        pltpu.VMEM((DISPATCH_ROWS, D), BF16),
                        pltpu.SemaphoreType.DMA((2,)),
                        pltpu.SemaphoreType.DMA(())],
    )
    return pl.pallas_call(
        functools.partial(_dispatch_kernel, n_exp=n_exp),
        grid_spec=grid_spec,
        out_shape=jax.ShapeDtypeStruct((n_rows, D), BF16),
        compiler_params=_params(("arbitrary",)),
        name="moe_dispatch",
    )(base_te, cnt_te, pad_lo, pad_n, u, meta_t)


TILE_FULL, TILE_HALF, TILE_EMPTY = 0, 1, 2


def _expert_kernel(te_ref, na_ref, fill_ref, xs_ref, wg_ref, wu_ref, wd_ref, o_ref, acc_ref):
    i = pl.program_id(0)
    j = pl.program_id(1)
    last = pl.num_programs(1) - 1
    active = i < na_ref[0]
    fill = fill_ref[i]
    tm = acc_ref.shape[0]

    @pl.when(active & (j == 0))
    def _():
        acc_ref[...] = jnp.zeros_like(acc_ref)

    def swiglu(rows):
        xb = xs_ref[rows, :]
        g = jnp.dot(xb, wg_ref[...].astype(BF16), preferred_element_type=F32)
        up = jnp.dot(xb, wu_ref[...].astype(BF16), preferred_element_type=F32)
        h = (g * _sigmoid(g) * up).astype(BF16)
        acc_ref[rows, :] += jnp.dot(h, wd_ref[...].astype(BF16), preferred_element_type=F32)

    pl.when(active & (fill == TILE_FULL))(functools.partial(swiglu, pl.ds(0, tm)))
    pl.when(active & (fill == TILE_HALF))(functools.partial(swiglu, pl.ds(0, tm // 2)))

    @pl.when(active & (j == last))
    def _():
        o_ref[...] = acc_ref[...].astype(BF16)

    @pl.when(jnp.logical_not(active) & (j == last))
    def _():
        o_ref[...] = jnp.zeros_like(o_ref)


def _expert_ffn(xs, tile_expert, n_active, tile_fill, wg, wu, wd, *, tm, tf):
    n_rows, D = xs.shape
    ff = wg.shape[2]
    assert n_rows % tm == 0 and ff % tf == 0
    nj = ff // tf

    def jj(i, j, na):
        return jnp.where(i < na[0], j, nj - 1)

    def ii(i, na):
        return jnp.minimum(i, na[0] - 1)

    grid_spec = pltpu.PrefetchScalarGridSpec(
        num_scalar_prefetch=3,
        grid=(n_rows // tm, nj),
        in_specs=[pl.BlockSpec((tm, D), lambda i, j, te, na, tf_: (ii(i, na), 0)),
                  pl.BlockSpec((None, D, tf), lambda i, j, te, na, tf_: (te[i], 0, jj(i, j, na))),
                  pl.BlockSpec((None, D, tf), lambda i, j, te, na, tf_: (te[i], 0, jj(i, j, na))),
                  pl.BlockSpec((None, tf, D), lambda i, j, te, na, tf_: (te[i], jj(i, j, na), 0))],
        out_specs=pl.BlockSpec((tm, D), lambda i, j, te, na, tf_: (i, 0)),
        scratch_shapes=[pltpu.VMEM((tm, D), F32)],
    )
    return pl.pallas_call(
        _expert_kernel,
        grid_spec=grid_spec,
        out_shape=jax.ShapeDtypeStruct((n_rows, D), BF16),
        compiler_params=_params(("arbitrary", "arbitrary")),
        name="moe_experts",
    )(tile_expert, n_active, tile_fill, xs, wg, wu, wd)


def _combine_kernel(base_ref, off_ref, cnt_ref, ys_hbm, meta_ref, x_ref, gate_ref, lg_ref, lb_ref, o_ref,
                    stage, extra, acc_ref, sems, extra_sem, *, alpha, n_exp):
    t = pl.program_id(0) * pl.num_programs(1) + pl.program_id(1)
    n_steps = pl.num_programs(0) * pl.num_programs(1)
    tm = x_ref.shape[0]
    R = DISPATCH_ROWS
    slot = t % 2

    def main_copies(step, buf, fn):
        for e in range(n_exp):
            fn(pltpu.make_async_copy(_rows_at(ys_hbm, base_ref[step * n_exp + e], R),
                                     stage.at[buf, pl.ds(e * R, R)], sems.at[buf]))

    @pl.when(t == 0)
    def _():
        main_copies(0, 0, lambda cp: cp.start())

    @pl.when(t + 1 < n_steps)
    def _():
        main_copies(t + 1, 1 - slot, lambda cp: cp.start())

    meta = meta_ref[...]
    col = lambda k: meta[:, k:k + 1]
    e0, e1 = col(META_E0).astype(I32), col(META_E1).astype(I32)
    r0, r1 = col(META_R0).astype(I32), col(META_R1).astype(I32)
    w0, w1 = col(META_W0), col(META_W1)
    lane = lax.broadcasted_iota(I32, (tm, R), 1)

    rels, ws, picks = [], [], []
    for e in range(n_exp):
        rel = jnp.where(e0 == e, r0, jnp.where(e1 == e, r1, -1))
        rels.append(jnp.where(rel >= 0, rel - off_ref[t * n_exp + e], -1))
        ws.append(jnp.where(e0 == e, w0, jnp.where(e1 == e, w1, 0.0)))
        picks.append(jnp.where(lane == rels[e], ws[e], 0.0))
    weighted = jnp.concatenate(picks, axis=1).astype(BF16)
    main_copies(t, slot, lambda cp: cp.wait())
    acc_ref[...] = jnp.dot(weighted, stage[slot], preferred_element_type=F32)

    for e in range(n_exp):
        for blk in range(1, -(-tm // R)):
            @pl.when(cnt_ref[t * n_exp + e] > blk * R)
            def _(e=e, blk=blk):
                cp = pltpu.make_async_copy(_rows_at(ys_hbm, base_ref[t * n_exp + e] + blk * R, R),
                                           extra, extra_sem)
                cp.start()
                cp.wait()
                more = jnp.where(lane + blk * R == rels[e], ws[e], 0.0).astype(BF16)
                acc_ref[...] += jnp.dot(more, extra[...], preferred_element_type=F32)

    z = alpha * x_ref[...] + (1.0 + gate_ref[...]) * acc_ref[...]
    o_ref[...] = _layer_norm(z, lg_ref[...], lb_ref[...])


def _combine(ys, base_te, off_te, cnt_te, meta, x, gate, ln_g, ln_b, *, alpha, tm):
    B, S, D = x.shape
    nt = S // tm
    n_exp = base_te.shape[0] // (B * nt)
    row = pl.BlockSpec((None, tm, D), lambda b, i, *_: (b, i, 0))
    vec = pl.BlockSpec((1, D), lambda b, i, *_: (0, 0))
    grid_spec = pltpu.PrefetchScalarGridSpec(
        num_scalar_prefetch=3,
        grid=(B, nt),
        in_specs=[pl.BlockSpec(memory_space=pl.ANY),
                  pl.BlockSpec((None, tm, LANES), lambda b, i, *_: (b, i, 0)),
                  row, pl.BlockSpec((None, 1, D), lambda b, i, *_: (b, 0, 0)), vec, vec],
        out_specs=row,
        scratch_shapes=[pltpu.VMEM((2, n_exp * DISPATCH_ROWS, D), BF16),
                        pltpu.VMEM((DISPATCH_ROWS, D), BF16),
                        pltpu.VMEM((tm, D), F32),
                        pltpu.SemaphoreType.DMA((2,)),
                        pltpu.SemaphoreType.DMA(())],
    )
    return pl.pallas_call(
        functools.partial(_combine_kernel, alpha=alpha, n_exp=n_exp),
        grid_spec=grid_spec,
        out_shape=jax.ShapeDtypeStruct((B, S, D), F32),
        compiler_params=_params(("arbitrary", "arbitrary")),
        name="moe_combine",
    )(base_te, off_te, cnt_te, ys, meta, x, gate, ln_g, ln_b)


def _moe_ffn(x, scale, shift, gate, w_router, wg, wu, wd, ln_g, ln_b, *, alpha, tm, tm_e, tf):
    B, S, D = x.shape
    T = B * S
    nt = T // tm
    n_exp = w_router.shape[1]
    wr = jnp.zeros((D, LANES), F32).at[:, :n_exp].set(w_router)
    wr_hi = wr.astype(BF16)
    wr = jnp.stack([wr_hi, (wr - wr_hi.astype(F32)).astype(BF16)])
    u, meta = _router(x, scale, shift, wr, n_exp=n_exp, tm=tm)

    meta2 = meta.reshape(T, LANES)
    e0 = meta2[:, META_E0].astype(I32)
    e1 = meta2[:, META_E1].astype(I32)
    r0 = meta2[:, META_R0].astype(I32)
    r1 = meta2[:, META_R1].astype(I32)
    experts = jnp.arange(n_exp, dtype=I32)
    chosen = ((e0[:, None] == experts) | (e1[:, None] == experts)).astype(I32)
    cnt_te = chosen.reshape(nt, tm, n_exp).sum(axis=1)
    seg_te = -(-cnt_te // SEGMENT_ALIGN) * SEGMENT_ALIGN
    off_te = jnp.cumsum(cnt_te, axis=0) - cnt_te
    filled = seg_te.sum(axis=0)
    sizes = -(-(filled + DISPATCH_SLACK) // tm_e) * tm_e
    ends = jnp.cumsum(sizes)
    starts = ends - sizes
    base_te = starts[None, :] + jnp.cumsum(seg_te, axis=0) - seg_te
    shift_te = (base_te - off_te)[:, None, :]

    def rows_of(e, r):
        pick = (e[:, None] == experts).astype(I32).reshape(nt, tm, n_exp)
        return r + (pick * shift_te).sum(axis=-1).reshape(T)

    d0, d1 = rows_of(e0, r0), rows_of(e1, r1)
    bound = TOP_K * T + n_exp * ((SEGMENT_ALIGN - 1) * nt + DISPATCH_SLACK)
    n_tiles = -(-bound // tm_e) + n_exp + 1
    n_active = (ends[-1] // tm_e).astype(I32)
    tile_id = jnp.minimum(jnp.arange(n_tiles, dtype=I32), n_active - 1)
    tile_expert = jnp.sum((ends[None, :] <= (tile_id * tm_e)[:, None]).astype(I32), axis=1)
    data_end = jnp.sum((tile_expert[:, None] == experts) * (starts + filled)[None, :], axis=1)
    left = data_end - tile_id * tm_e
    tile_fill = jnp.where(left <= 0, TILE_EMPTY, jnp.where(left <= tm_e // 2, TILE_HALF, TILE_FULL)).astype(I32)
    meta_t = jnp.stack([e0, e1, d0, d1] + [jnp.zeros_like(e0)] * 4)

    xs = _dispatch(u.reshape(T, D), meta_t, base_te.reshape(-1), cnt_te.reshape(-1),
                   jnp.concatenate([starts + filled, ends[-1:]]),
                   jnp.concatenate([sizes - filled, n_tiles * tm_e - ends[-1:]]),
                   n_rows=n_tiles * tm_e, tm=tm)
    ys = _expert_ffn(xs, tile_expert, n_active.reshape(1), tile_fill, wg, wu, wd, tm=tm_e, tf=tf)
    return _combine(ys, base_te.reshape(-1), off_te.reshape(-1), cnt_te.reshape(-1),
                    meta, x, gate, ln_g, ln_b, alpha=alpha, tm=tm)


class _Tiles(NamedTuple):
    token_rows: int
    rope_rows: int
    ffn_chunk: int
    expert_rows: int
    expert_ff: int


def _tile_plan(batch, seq):
    return _Tiles(token_rows=min(512, seq), rope_rows=min(1024, seq), ffn_chunk=2 * LANES,
                  expert_rows=min(1024, batch * seq), expert_ff=512)


def kernel(x, c, positions, w_in, w_out, attn_norm_gain, rec_norm_gain, rec_lb_logits, ada_w, ada_b,
           ln_gain, ln_bias, ffn_w_gate, ffn_w_up, ffn_w_down, moe_router, moe_w_gate, moe_w_up,
           moe_w_down):
    B, S, D = x.shape
    depth = w_in.shape[0]
    alpha = (2 * depth) ** 0.25

    p = jax.nn.softmax(rec_lb_logits.astype(F32), axis=0)
    cum = jnp.cumsum(p, axis=0)
    lb_all = cum - cum[0:1]
    half = ATTN_HEAD_DIM // 2
    inv_freq = ROPE_THETA ** (-jnp.arange(half, dtype=F32) / half)
    inv_freq_lanes = jnp.tile(inv_freq, LANES // half).reshape(1, LANES)
    pos_f = positions.astype(F32).reshape(B, S, 1)
    n_qk = 2 * ATTN_WIDTH
    w_in_b = jnp.concatenate([_qk_lane_order(w_in[..., :n_qk]), w_in[..., n_qk:]], axis=-1).astype(BF16)
    w_out_b = w_out.astype(BF16)
    ff = ffn_w_gate.shape[2]
    ffp = -(-ff // (2 * LANES)) * (2 * LANES)
    pad_c = lambda w: jnp.pad(w.astype(BF16), ((0, 0), (0, 0), (0, ffp - ff)))
    ffn_g, ffn_u = pad_c(ffn_w_gate), pad_c(ffn_w_up)
    ffn_d = jnp.pad(ffn_w_down.astype(BF16), ((0, 0), (0, ffp - ff), (0, 0)))

    mods = _ada_modulation(c, ada_w, ada_b)
    tiles = _tile_plan(B, S)
    rope_cos, rope_sin = _rope_tables(pos_f, inv_freq_lanes, tm=tiles.rope_rows)

    def mod(layer, sub):
        m = mods[layer * 2 + sub].reshape(3, B, 1, D)
        return m[0], m[1], m[2]

    vec = lambda a: a.reshape(1, -1)
    for layer in range(depth):
        shift, scale, gate = mod(layer, 0)
        proj = _in_projection(x, scale, shift, rope_cos, rope_sin, w_in_b, layer, tm=tiles.token_rows)
        attn = _dilated_attention(proj)
        rec = _hgrn2(proj, lb_all[layer], vec(rec_norm_gain[layer]))
        x = _out_projection(attn, rec, x, gate, vec(attn_norm_gain[layer]), w_out_b, layer,
                            vec(ln_gain[layer, 0]), vec(ln_bias[layer, 0]), alpha=alpha,
                            tm=tiles.token_rows)
        shift, scale, gate = mod(layer, 1)
        j = layer // 2
        if layer % 2 == 0:
            x = _dense_ffn(x, scale, shift, gate, ffn_g[j], ffn_u[j], ffn_d[j],
                           vec(ln_gain[layer, 1]), vec(ln_bias[layer, 1]),
                           alpha=alpha, tm=tiles.token_rows, tc=tiles.ffn_chunk)
        else:
            x = _moe_ffn(x, scale, shift, gate, moe_router[j], moe_w_gate[j], moe_w_up[j],
                         moe_w_down[j], vec(ln_gain[layer, 1]), vec(ln_bias[layer, 1]),
                         alpha=alpha, tm=tiles.token_rows, tm_e=tiles.expert_rows, tf=tiles.expert_ff)
    return x
```

```python
import functools
from typing import NamedTuple

import jax
import jax.numpy as jnp
from jax import lax
from jax.experimental import pallas as pl
from jax.experimental.pallas import tpu as pltpu

F32 = jnp.float32
BF16 = jnp.bfloat16
I32 = jnp.int32

LANES = 128
V7X_VMEM_LIMIT_BYTES = 56 * 1024 * 1024

ATTN_WIDTH = 512
ATTN_HEAD_DIM = 64
DILATIONS = (1, 4, 16)
ATTN_RADIUS = 64
ROPE_THETA = 10000.0
REC_WIDTH = 512
REC_DIM = 128
REC_CHUNK = 64
TOP_K = 2
LN_EPS = 1e-5
RMS_EPS = 1e-6
MASK_VALUE = -1e30
DECAY_EXP_CLAMP = 80.0
LOG2_E = 1.4426950408889634


def _params(semantics):
    return pltpu.CompilerParams(dimension_semantics=semantics,
                                vmem_limit_bytes=V7X_VMEM_LIMIT_BYTES)


def _sigmoid(x):
    return 1.0 / (1.0 + jnp.exp(-x))


def _layer_norm(z, gain, bias):
    mu = jnp.mean(z, axis=-1, keepdims=True)
    zc = z - mu
    var = jnp.mean(zc * zc, axis=-1, keepdims=True)
    return zc * lax.rsqrt(var + LN_EPS) * gain + bias


def _ada_kernel(c_ref, w_ref, b_ref, o_ref):
    c = c_ref[...]
    o_ref[...] = jnp.dot(c * _sigmoid(c), w_ref[...], precision=lax.Precision.HIGHEST,
                         preferred_element_type=F32) + b_ref[...]


def _ada_modulation(c, ada_w, ada_b):
    B, D = c.shape
    n = ada_w.shape[0] * ada_w.shape[1]
    n3 = ada_w.shape[-1]
    tn = D
    return pl.pallas_call(
        _ada_kernel,
        grid=(n, n3 // tn),
        in_specs=[pl.BlockSpec((B, D), lambda i, j: (0, 0)),
                  pl.BlockSpec((None, D, tn), lambda i, j: (i, 0, j)),
                  pl.BlockSpec((None, 1, tn), lambda i, j: (i, 0, j))],
        out_specs=pl.BlockSpec((None, None, B, tn), lambda i, j: (i, j, 0, 0)),
        out_shape=jax.ShapeDtypeStruct((n, n3 // tn, B, tn), F32),
        compiler_params=_params(("arbitrary", "arbitrary")),
        name="ada_modulation",
    )(c, ada_w.reshape(n, D, n3), ada_b.reshape(n, 1, n3))


ROPE_HALF = ATTN_HEAD_DIM // 2


def _qk_lane_order(w_cols):
    lead = w_cols.shape[:-1]
    w = w_cols.reshape(*lead, -1, 2, 2, ROPE_HALF)
    return jnp.swapaxes(w, -3, -2).reshape(*lead, -1)


def _rope_kernel(pos_ref, invf_ref, cos_ref, sin_ref):
    ang = pos_ref[...] * invf_ref[...]
    lane = lax.broadcasted_iota(I32, (1, LANES), 1)
    sin = jnp.sin(ang)
    cos_ref[...] = jnp.cos(ang)
    sin_ref[...] = jnp.where(lane < LANES // 2, -sin, sin)


def _rope_tables(pos_f, inv_freq_lanes, *, tm):
    B, S, _ = pos_f.shape
    tab = pl.BlockSpec((None, tm, LANES), lambda b, i: (b, i, 0))
    return pl.pallas_call(
        _rope_kernel,
        grid=(B, S // tm),
        in_specs=[pl.BlockSpec((None, tm, 1), lambda b, i: (b, i, 0)),
                  pl.BlockSpec((1, LANES), lambda b, i: (0, 0))],
        out_specs=[tab, tab],
        out_shape=[jax.ShapeDtypeStruct((B, S, LANES), F32)] * 2,
        compiler_params=_params(("arbitrary", "arbitrary")),
        name="rope_tables",
    )(pos_f, inv_freq_lanes)


def _inproj_kernel(x_ref, sc_ref, sh_ref, cos_ref, sin_ref, w_ref, o_ref, *, tn):
    u = (x_ref[...] * (1.0 + sc_ref[...]) + sh_ref[...]).astype(BF16)
    cos = cos_ref[...]
    sin_signed = sin_ref[...]
    q_scale = ATTN_HEAD_DIM ** -0.5
    for j in range(w_ref.shape[1] // tn):
        acc = jnp.dot(u, w_ref[:, j * tn:(j + 1) * tn], preferred_element_type=F32)
        if j > 0:
            o_ref[:, j * tn:(j + 1) * tn] = acc
            continue
        for kk in range(tn // LANES):
            c = acc[:, kk * LANES:(kk + 1) * LANES]
            r = c * cos + pltpu.roll(c, LANES // 2, 1) * sin_signed
            if kk * LANES < ATTN_WIDTH:
                r = r * q_scale
            o_ref[:, kk * LANES:(kk + 1) * LANES] = r


def _in_projection(x, scale, shift, rope_cos, rope_sin, w_in_bf16, layer, *, tm):
    B, S, D = x.shape
    n_cols = w_in_bf16.shape[2]
    tn = 2 * ATTN_WIDTH
    assert S % tm == 0 and n_cols % tn == 0
    return pl.pallas_call(
        functools.partial(_inproj_kernel, tn=tn),
        grid=(B, S // tm),
        in_specs=[pl.BlockSpec((None, tm, D), lambda b, i: (b, i, 0)),
                  pl.BlockSpec((None, 1, D), lambda b, i: (b, 0, 0)),
                  pl.BlockSpec((None, 1, D), lambda b, i: (b, 0, 0)),
                  pl.BlockSpec((None, tm, LANES), lambda b, i: (b, i, 0)),
                  pl.BlockSpec((None, tm, LANES), lambda b, i: (b, i, 0)),
                  pl.BlockSpec((None, D, n_cols), lambda b, i: (layer, 0, 0))],
        out_specs=pl.BlockSpec((None, tm, n_cols), lambda b, i: (b, i, 0)),
        out_shape=jax.ShapeDtypeStruct((B, S, n_cols), F32),
        compiler_params=_params(("arbitrary", "arbitrary")),
        name="in_projection",
    )(x, scale, shift, rope_cos, rope_sin, w_in_bf16)


ATTN_BLOCK_GROUP = 16


def _attn_kernel(q_ref, k_ref, v_ref, o_ref, qs, ks, vs, qf, kf, vf, pm, plr, pa, nm, nl, na, bias_s,
                 *, S):
    lane = lax.broadcasted_iota(I32, (1, LANES), 1)
    head0 = lane < ATTN_HEAD_DIM
    head0_qk = (lane % ATTN_HEAD_DIM) < ROPE_HALF

    for p, d in enumerate(DILATIONS):
        L = S // d
        tq = min(128, L)
        W = min(2 * tq, L)
        nb = L // tq

        d_prev = DILATIONS[p - 1] if p else 1
        ratio, l_prev = d // d_prev, S // d_prev
        keep_f32 = 0 < p < len(DILATIONS) - 1
        for src, stage, dst in ((q_ref, qf, qs), (k_ref, kf, ks), (v_ref, vf, vs)):
            src = src if p <= 1 else stage
            for r in range(d):
                a, r_prev = divmod(r, d_prev)
                rows = pl.ds(r_prev * l_prev + a, L, stride=ratio) if ratio > 1 else pl.ds(0, L)
                val = src[rows, :]
                if keep_f32:
                    stage[r * L:(r + 1) * L, :] = val
                dst[r * L:(r + 1) * L, :] = val.astype(BF16)

        n_blocks = d * nb
        rc = lax.broadcasted_iota(I32, (tq, W), 0) - lax.broadcasted_iota(I32, (tq, W), 1)
        for which, delta in enumerate((0, ATTN_RADIUS, tq)):
            bias_s[which, pl.ds(0, tq), pl.ds(0, W)] = jnp.where(
                jnp.abs(rc + delta) <= ATTN_RADIUS, 0.0, MASK_VALUE)
        group = min(ATTN_BLOCK_GROUP, n_blocks)
        assert n_blocks % group == 0

        def block_group(gi, carry, L=L, tq=tq, W=W, nb=nb, group=group):
            q0s, k0s, scores, probs, outs, sums = [], [], [], [], [], []
            for u in range(group):
                g = gi * group + u
                r = g // nb
                n = g - r * nb
                base = r * L
                q0 = pl.multiple_of(base + n * tq, tq)
                ws = jnp.clip(n * tq - ATTN_RADIUS, 0, L - W)
                k0 = pl.multiple_of(base + ws, 16)
                qb = qs[pl.ds(q0, tq), :]
                zero = jnp.zeros_like(qb)
                q2 = jnp.concatenate([jnp.where(head0_qk, qb, zero), jnp.where(head0_qk, zero, qb)], axis=0)
                s = lax.dot_general(q2, ks[pl.ds(k0, W), :], (((1,), (1,)), ((), ())),
                                    preferred_element_type=F32)
                delta = n * tq - ws
                which = jnp.where(delta == 0, 0, jnp.where(delta == ATTN_RADIUS, 1, 2))
                bias = bias_s[which, pl.ds(0, tq), pl.ds(0, W)]
                scores.append(s + jnp.concatenate([bias, bias], axis=0))
                q0s.append(q0)
                k0s.append(k0)
            for u in range(group):
                s = scores[u]
                m = jnp.max(s, axis=1, keepdims=True)
                e = jnp.exp(s - m)
                probs.append((m, e.astype(BF16)))
                sums.append(jnp.sum(e, axis=1, keepdims=True))
            for u in range(group):
                outs.append(jnp.dot(probs[u][1], vs[pl.ds(k0s[u], W), :], preferred_element_type=F32))
            for u in range(group):
                m, l = probs[u][0], sums[u]
                rows = pl.ds(q0s[u], tq)
                out_m[rows, :] = jnp.where(head0, m[:tq], m[tq:])
                out_l[rows, :] = jnp.where(head0, l[:tq], l[tq:])
                out_a[rows, :] = jnp.where(head0, outs[u][:tq], outs[u][tq:])
            return carry

        out_m, out_l, out_a = (nm.at[p], nl.at[p], na.at[p]) if d == 1 else (pm, plr, pa)
        lax.fori_loop(0, n_blocks // group, block_group, 0)

        for r in range(d if d > 1 else 0):
            rows = pl.ds(r, L, stride=d)
            nm[p, rows, :] = pm[r * L:(r + 1) * L, :]
            nl[p, rows, :] = plr[r * L:(r + 1) * L, :]
            na[p, rows, :] = pa[r * L:(r + 1) * L, :]

    m_all = jnp.maximum(jnp.maximum(nm[0], nm[1]), nm[2])
    num = jnp.zeros((S, LANES), F32)
    den = jnp.zeros((S, LANES), F32)
    for p in range(len(DILATIONS)):
        w = jnp.exp(nm[p] - m_all)
        num = num + w * na[p]
        den = den + w * nl[p]
    o_ref[...] = (num / den).astype(o_ref.dtype)


def _dilated_attention(proj):
    B, S, _ = proj.shape
    n_pairs = ATTN_WIDTH // LANES
    assert S % (16 * DILATIONS[-1]) == 0
    blk = lambda off: pl.BlockSpec((None, S, LANES), lambda b, h: (b, 0, off + h))
    return pl.pallas_call(
        functools.partial(_attn_kernel, S=S),
        grid=(B, n_pairs),
        in_specs=[blk(0), blk(n_pairs), blk(2 * n_pairs)],
        out_specs=pl.BlockSpec((None, S, LANES), lambda b, h: (b, 0, h)),
        out_shape=jax.ShapeDtypeStruct((B, S, ATTN_WIDTH), BF16),
        scratch_shapes=[pltpu.VMEM((S, LANES), BF16)] * 3
                       + [pltpu.VMEM((S, LANES), F32)] * 6
                       + [pltpu.VMEM((len(DILATIONS), S, LANES), F32)] * 3
                       + [pltpu.VMEM((3, min(128, S), min(256, S)), F32)],
        compiler_params=_params(("arbitrary", "arbitrary")),
        name="dilated_attention",
    )(proj, proj, proj)


REC_GROUP_CHUNKS = 4
REC_GROUP_ROWS = REC_GROUP_CHUNKS * REC_CHUNK
REC_STAGE_GROUPS = 8


def _hgrn_kernel(rq_ref, zf_ref, zb_ref, ri_ref, rg_ref, lb_ref, gain_ref, o_ref,
                 q_s, vt_s, oi_s, qp_s, ut_s, dd_s, st_s, os_s, *, S):
    C, G, GR = REC_CHUNK, REC_GROUP_CHUNKS, REC_GROUP_ROWS
    nc, ng = S // C, S // GR
    sg = min(REC_STAGE_GROUPS, ng)
    assert ng % sg == 0

    row = lax.broadcasted_iota(I32, (GR, GR), 0)
    col = lax.broadcasted_iota(I32, (GR, GR), 1)
    same_chunk = (row // C) == (col // C)
    t_i = lax.broadcasted_iota(I32, (C, C), 0)
    s_i = lax.broadcasted_iota(I32, (C, C), 1)

    def prep(i, carry):
        r0 = pl.multiple_of(i * GR, GR)
        rq = rq_ref[pl.ds(r0, GR), :]
        q_s[pl.ds(r0, GR), :] = rq * _sigmoid(rq)
        v = ri_ref[pl.ds(r0, GR), :]
        for c in range(G):
            vt_s[i * G + c] = v[c * C:(c + 1) * C, :].T.astype(BF16)
        return carry

    lax.fori_loop(0, ng, prep, 0, unroll=True)

    for direction in range(2):
        fwd = direction == 0
        z_ref = zf_ref if fwd else zb_ref
        cum = jnp.where(same_chunk & ((col <= row) if fwd else (col >= row)), 1.0, 0.0).astype(BF16)
        keep = (s_i <= t_i) if fwd else (s_i >= t_i)
        mid_row = C // 2 - 1 if fwd else C // 2
        last_row = C - 1 if fwd else 0

        def phase_a(it, carry, fwd=fwd, z_ref=z_ref, cum=cum, keep=keep,
                    mid_row=mid_row, last_row=last_row, direction=direction):
            lb = lb_ref[direction:direction + 1, :]
            log_lb = jnp.log(lb)
            log_1m_lb = jnp.log1p(-lb)
            r0s, c0s, k3s, bcs, atts, stage3 = [], [], [], [], [], []
            for u in range(sg):
                i = it * sg + u
                r0 = pl.multiple_of(i * GR, GR)
                z = z_ref[pl.ds(r0, GR), :]
                e = jnp.exp(-jnp.abs(z))
                t = log_1m_lb + (jnp.minimum(z, 0.0) - jnp.log(1.0 + e))
                g = jnp.maximum(log_lb, t) + jnp.log(1.0 + jnp.exp(-jnp.abs(log_lb - t)))
                kk = (1.0 - lb) * (jnp.where(z > 0, e, 1.0) / (1.0 + e))
                g = g * LOG2_E
                g1 = g.astype(BF16)
                r1 = g - g1.astype(F32)
                g2 = r1.astype(BF16)
                g3 = (r1 - g2.astype(F32)).astype(BF16)
                bcs.append(jnp.dot(cum, g1, preferred_element_type=F32)
                           + jnp.dot(cum, g2, preferred_element_type=F32)
                           + jnp.dot(cum, g3, preferred_element_type=F32))
                k3s.append(kk.reshape(G, C, REC_DIM))
                r0s.append(r0)
                c0s.append(pl.multiple_of(i * G, G))
            for u in range(sg):
                b3 = bcs[u].reshape(G, C, REC_DIM)
                mid = b3[:, mid_row:mid_row + 1, :]
                last = b3[:, last_row:last_row + 1, :]
                e3 = b3 - mid
                q3 = q_s[pl.ds(r0s[u], GR), :].reshape(G, C, REC_DIM)
                clamp = DECAY_EXP_CLAMP * LOG2_E
                qt = (q3 * jnp.exp2(jnp.minimum(e3, clamp))).astype(BF16)
                kt = (k3s[u] * jnp.exp2(jnp.minimum(-e3, clamp))).astype(BF16)
                atts.append(jnp.einsum('gtc,gsc->gts', qt, kt, preferred_element_type=F32))
                qp_s[pl.ds(r0s[u], GR), :] = (q3 * jnp.exp2(b3)).astype(BF16).reshape(GR, REC_DIM)
                stage3.append(((k3s[u] * jnp.exp2(last - b3)).astype(BF16), jnp.exp2(last)))
            for u in range(sg):
                att = jnp.where(keep[None], atts[u], 0.0).astype(BF16)
                v3 = ri_ref[pl.ds(r0s[u], GR), :].reshape(G, C, REC_DIM).astype(BF16)
                oi = jnp.einsum('gts,gsv->gtv', att, v3, preferred_element_type=F32)
                oi_s[pl.ds(r0s[u], GR), :] = oi.reshape(GR, REC_DIM)
                kp, decay = stage3[u]
                ut_s[pl.ds(c0s[u], G)] = jnp.einsum('gvs,gsc->gvc', vt_s[pl.ds(c0s[u], G)], kp,
                                                    preferred_element_type=F32)
                dd_s[pl.ds(c0s[u], G)] = decay
            return carry

        lax.fori_loop(0, ng // sg, phase_a, 0)

        def scan(c, st, fwd=fwd):
            idx = c if fwd else nc - 1 - c
            st_s[idx] = st.astype(BF16)
            return dd_s[idx] * st + ut_s[idx]

        lax.fori_loop(0, nc, scan, jnp.zeros((REC_DIM, REC_DIM), F32), unroll=True)

        def phase_c(it, carry, fwd=fwd):
            r0s = [pl.multiple_of((it * sg + u) * GR, GR) for u in range(sg)]
            oos = []
            for u in range(sg):
                c0 = pl.multiple_of((it * sg + u) * G, G)
                qp3 = qp_s[pl.ds(r0s[u], GR), :].reshape(G, C, REC_DIM)
                oos.append(jnp.einsum('gtc,gvc->gtv', qp3, st_s[pl.ds(c0, G)],
                                      preferred_element_type=F32))
            for u in range(sg):
                rows = pl.ds(r0s[u], GR)
                tot = oi_s[rows, :] + oos[u].reshape(GR, REC_DIM)
                os_s[rows, :] = tot if fwd else os_s[rows, :] + tot
            return carry

        lax.fori_loop(0, ng // sg, phase_c, 0)

    o = os_s[...]
    o = o * lax.rsqrt(jnp.mean(o * o, axis=-1, keepdims=True) + RMS_EPS) * gain_ref[...]
    o_ref[...] = (o * _sigmoid(rg_ref[...])).astype(o_ref.dtype)


def _hgrn2(proj, lb, rec_gain):
    B, S, _ = proj.shape
    nh = REC_WIDTH // REC_DIM
    assert S % REC_GROUP_ROWS == 0
    col0 = 3 * ATTN_WIDTH // LANES
    blk = lambda k: pl.BlockSpec((None, S, REC_DIM), lambda b, h: (b, 0, col0 + k * nh + h))
    nc = S // REC_CHUNK
    return pl.pallas_call(
        functools.partial(_hgrn_kernel, S=S),
        grid=(B, nh),
        in_specs=[blk(0), blk(1), blk(2), blk(3), blk(4),
                  pl.BlockSpec((2, REC_DIM), lambda b, h: (0, h)),
                  pl.BlockSpec((1, REC_DIM), lambda b, h: (0, h))],
        out_specs=pl.BlockSpec((None, S, REC_DIM), lambda b, h: (b, 0, h)),
        out_shape=jax.ShapeDtypeStruct((B, S, REC_WIDTH), BF16),
        scratch_shapes=[pltpu.VMEM((S, REC_DIM), F32),
                        pltpu.VMEM((nc, REC_DIM, REC_CHUNK), BF16),
                        pltpu.VMEM((S, REC_DIM), F32),
                        pltpu.VMEM((S, REC_DIM), BF16),
                        pltpu.VMEM((nc, REC_DIM, REC_DIM), F32),
                        pltpu.VMEM((nc, 1, REC_DIM), F32),
                        pltpu.VMEM((nc, REC_DIM, REC_DIM), BF16),
                        pltpu.VMEM((S, REC_DIM), F32)],
        compiler_params=_params(("arbitrary", "arbitrary")),
        name="hgrn2",
    )(proj, proj, proj, proj, proj, lb, rec_gain)


def _outproj_kernel(a_ref, r_ref, x_ref, gate_ref, ag_ref, w_ref, lg_ref, lb_ref, o_ref, *, alpha):
    a = a_ref[...].astype(F32)
    an = a * lax.rsqrt(jnp.mean(a * a, axis=-1, keepdims=True) + RMS_EPS) * ag_ref[...]
    y = (jnp.dot(an.astype(BF16), w_ref[:ATTN_WIDTH, :], preferred_element_type=F32)
         + jnp.dot(r_ref[...], w_ref[ATTN_WIDTH:, :], preferred_element_type=F32))
    z = alpha * x_ref[...] + (1.0 + gate_ref[...]) * y
    o_ref[...] = _layer_norm(z, lg_ref[...], lb_ref[...])


def _out_projection(attn, rec, x, gate, attn_gain, w_out_bf16, layer, ln_g, ln_b, *, alpha, tm):
    B, S, D = x.shape
    row = lambda w: pl.BlockSpec((None, tm, w), lambda b, i: (b, i, 0))
    vec = lambda w: pl.BlockSpec((1, w), lambda b, i: (0, 0))
    return pl.pallas_call(
        functools.partial(_outproj_kernel, alpha=alpha),
        grid=(B, S // tm),
        in_specs=[row(ATTN_WIDTH), row(REC_WIDTH), row(D),
                  pl.BlockSpec((None, 1, D), lambda b, i: (b, 0, 0)),
                  vec(ATTN_WIDTH),
                  pl.BlockSpec((None, ATTN_WIDTH + REC_WIDTH, D), lambda b, i: (layer, 0, 0)),
                  vec(D), vec(D)],
        out_specs=row(D),
        out_shape=jax.ShapeDtypeStruct((B, S, D), F32),
        compiler_params=_params(("arbitrary", "arbitrary")),
        name="out_projection",
    )(attn, rec, x, gate, attn_gain, w_out_bf16, ln_g, ln_b)


def _ffn_kernel(x_ref, sc_ref, sh_ref, gate_ref, wg_ref, wu_ref, wd_ref, lg_ref, lb_ref, o_ref,
                *, alpha, tc):
    x = x_ref[...]
    u = (x * (1.0 + sc_ref[...]) + sh_ref[...]).astype(BF16)
    acc = jnp.zeros(x.shape, F32)
    for c in range(wg_ref.shape[1] // tc):
        cols = slice(c * tc, (c + 1) * tc)
        g = jnp.dot(u, wg_ref[:, cols], preferred_element_type=F32)
        up = jnp.dot(u, wu_ref[:, cols], preferred_element_type=F32)
        h = (g * _sigmoid(g) * up).astype(BF16)
        acc = acc + jnp.dot(h, wd_ref[cols, :], preferred_element_type=F32)
    z = alpha * x + (1.0 + gate_ref[...]) * acc
    o_ref[...] = _layer_norm(z, lg_ref[...], lb_ref[...])


def _dense_ffn(x, scale, shift, gate, wg, wu, wd, ln_g, ln_b, *, alpha, tm, tc):
    B, S, D = x.shape
    ffp = wg.shape[1]
    assert ffp % tc == 0
    row = pl.BlockSpec((None, tm, D), lambda b, i: (b, i, 0))
    mod = pl.BlockSpec((None, 1, D), lambda b, i: (b, 0, 0))
    vec = pl.BlockSpec((1, D), lambda b, i: (0, 0))
    return pl.pallas_call(
        functools.partial(_ffn_kernel, alpha=alpha, tc=tc),
        grid=(B, S // tm),
        in_specs=[row, mod, mod, mod,
                  pl.BlockSpec((D, ffp), lambda b, i: (0, 0)),
                  pl.BlockSpec((D, ffp), lambda b, i: (0, 0)),
                  pl.BlockSpec((ffp, D), lambda b, i: (0, 0)),
                  vec, vec],
        out_specs=row,
        out_shape=jax.ShapeDtypeStruct((B, S, D), F32),
        compiler_params=_params(("arbitrary", "arbitrary")),
        name="dense_ffn",
    )(x, scale, shift, gate, wg, wu, wd, ln_g, ln_b)


META_E0, META_E1, META_W0, META_W1, META_R0, META_R1 = range(6)
SEGMENT_ALIGN = 16
DISPATCH_ROWS = 256
DISPATCH_HALF = DISPATCH_ROWS // 2
DISPATCH_SLACK = DISPATCH_ROWS


def _router_kernel(x_ref, sc_ref, sh_ref, wr_ref, u_ref, meta_ref, carry_ref, *, n_exp):
    tm = x_ref.shape[0]

    @pl.when((pl.program_id(0) == 0) & (pl.program_id(1) == 0))
    def _():
        carry_ref[...] = jnp.zeros_like(carry_ref)

    u = x_ref[...] * (1.0 + sc_ref[...]) + sh_ref[...]
    u_ref[...] = u.astype(BF16)
    u_hi = u.astype(BF16)
    u_lo = (u - u_hi.astype(F32)).astype(BF16)
    logits = (jnp.dot(u_hi, wr_ref[0], preferred_element_type=F32)
              + jnp.dot(u_lo, wr_ref[0], preferred_element_type=F32)
              + jnp.dot(u_hi, wr_ref[1], preferred_element_type=F32))
    lane = lax.broadcasted_iota(I32, (tm, LANES), 1)
    neg = -jnp.inf
    l1 = jnp.where(lane < n_exp, logits, neg)
    m1 = jnp.max(l1, axis=1, keepdims=True)
    i1 = jnp.min(jnp.where(l1 == m1, lane, LANES), axis=1, keepdims=True)
    l2 = jnp.where(lane == i1, neg, l1)
    m2 = jnp.max(l2, axis=1, keepdims=True)
    i2 = jnp.min(jnp.where(l2 == m2, lane, LANES), axis=1, keepdims=True)
    e = jnp.exp(m2 - m1)
    w1 = 1.0 / (1.0 + e)
    w2 = e * w1
    sel = jnp.where((lane == i1) | (lane == i2), 1.0, 0.0)
    before = (lax.broadcasted_iota(I32, (tm, tm), 1) < lax.broadcasted_iota(I32, (tm, tm), 0))
    ranks = jnp.dot(jnp.where(before, 1.0, 0.0).astype(BF16), sel.astype(BF16),
                    preferred_element_type=F32) + carry_ref[...]
    r1 = jnp.sum(jnp.where(lane == i1, ranks, 0.0), axis=1, keepdims=True)
    r2 = jnp.sum(jnp.where(lane == i2, ranks, 0.0), axis=1, keepdims=True)
    carry_ref[...] = carry_ref[...] + jnp.sum(sel, axis=0, keepdims=True)
    meta = jnp.zeros((tm, LANES), F32)
    for k, val in ((META_E0, i1.astype(F32)), (META_E1, i2.astype(F32)), (META_W0, w1),
                   (META_W1, w2), (META_R0, r1), (META_R1, r2)):
        meta = jnp.where(lane == k, val, meta)
    meta_ref[...] = meta


def _router(x, scale, shift, w_router_lanes, *, n_exp, tm):
    B, S, D = x.shape
    nt = S // tm
    row = pl.BlockSpec((None, tm, D), lambda b, i: (b, i, 0))
    mod = pl.BlockSpec((None, 1, D), lambda b, i: (b, 0, 0))
    return pl.pallas_call(
        functools.partial(_router_kernel, n_exp=n_exp),
        grid=(B, nt),
        in_specs=[row, mod, mod, pl.BlockSpec((2, D, LANES), lambda b, i: (0, 0, 0))],
        out_specs=[row, pl.BlockSpec((None, tm, LANES), lambda b, i: (b, i, 0))],
        out_shape=[jax.ShapeDtypeStruct((B, S, D), BF16),
                   jax.ShapeDtypeStruct((B, S, LANES), F32)],
        scratch_shapes=[pltpu.VMEM((1, LANES), F32)],
        compiler_params=_params(("arbitrary", "arbitrary")),
        name="moe_router",
    )(x, scale, shift, w_router_lanes)


def _rows_at(hbm, first_row, n_rows):
    return hbm.at[pl.ds(pl.multiple_of(first_row, SEGMENT_ALIGN), n_rows)]


def _dispatch_kernel(base_ref, cnt_ref, padlo_ref, padn_ref, u_ref, meta_ref, xs_hbm,
                     stage, extra, zeros_buf, sems, extra_sem, *, n_exp):
    t = pl.program_id(0)
    last = pl.num_programs(0) - 1
    tm = u_ref.shape[0]
    R, H = DISPATCH_ROWS, DISPATCH_HALF
    slot = t % 2
    u = u_ref[...]
    e0, e1 = meta_ref[0:1, :], meta_ref[1:2, :]
    d0, d1 = meta_ref[2:3, :], meta_ref[3:4, :]
    srow = lax.broadcasted_iota(I32, (R, tm), 0)

    def main_copies(step, buf, fn):
        for e in range(n_exp):
            base = base_ref[step * n_exp + e]
            fn(pltpu.make_async_copy(stage.at[buf, pl.ds(e * R, H)], _rows_at(xs_hbm, base, H),
                                     sems.at[buf]))

            @pl.when(cnt_ref[step * n_exp + e] > H)
            def _(e=e, base=base):
                fn(pltpu.make_async_copy(stage.at[buf, pl.ds(e * R + H, H)],
                                         _rows_at(xs_hbm, base + H, H), sems.at[buf]))

    rels = []
    for e in range(n_exp):
        rel = jnp.where(e0 == e, d0, jnp.where(e1 == e, d1, -1))
        rels.append(jnp.where(rel >= 0, rel - base_ref[t * n_exp + e], -1))
    onehot = jnp.concatenate([jnp.where(srow == rel, 1.0, 0.0) for rel in rels], axis=0).astype(BF16)
    stage[slot] = jnp.dot(onehot, u, preferred_element_type=F32).astype(BF16)

    @pl.when(t > 0)
    def _():
        main_copies(t - 1, 1 - slot, lambda cp: cp.wait())
    main_copies(t, slot, lambda cp: cp.start())

    for e in range(n_exp):
        for blk in range(1, -(-tm // R)):
            @pl.when(cnt_ref[t * n_exp + e] > blk * R)
            def _(e=e, blk=blk):
                more = jnp.where(srow + blk * R == rels[e], 1.0, 0.0).astype(BF16)
                extra[...] = jnp.dot(more, u, preferred_element_type=F32).astype(BF16)
                cp = pltpu.make_async_copy(extra, _rows_at(xs_hbm, base_ref[t * n_exp + e] + blk * R, R),
                                           extra_sem)
                cp.start()
                cp.wait()

    @pl.when(t == last)
    def _():
        main_copies(t, slot, lambda cp: cp.wait())
        zeros_buf[...] = jnp.zeros_like(zeros_buf)
        pieces = [SEGMENT_ALIGN << b for b in range(((R - 1) // SEGMENT_ALIGN).bit_length())]

        def pad_copies(fn):
            for p in range(padlo_ref.shape[0]):
                lo, n = padlo_ref[p], padn_ref[p]
                whole = n // R

                def body(k, carry, lo=lo):
                    fn(pltpu.make_async_copy(zeros_buf, _rows_at(xs_hbm, lo + k * R, R), extra_sem))
                    return carry
                lax.fori_loop(0, whole, body, 0)
                rem = n - whole * R
                for sz in pieces:
                    off = lo + whole * R + (rem // (2 * sz)) * (2 * sz)
                    pl.when((rem // sz) % 2 == 1)(functools.partial(
                        lambda off, sz: fn(pltpu.make_async_copy(
                            zeros_buf.at[pl.ds(0, sz)], _rows_at(xs_hbm, off, sz), extra_sem)), off, sz))

        pad_copies(lambda cp: cp.start())
        pad_copies(lambda cp: cp.wait())


def _dispatch(u, meta_t, base_te, cnt_te, pad_lo, pad_n, *, n_rows, tm):
    T, D = u.shape
    n_exp = pad_lo.shape[0] - 1
    grid_spec = pltpu.PrefetchScalarGridSpec(
        num_scalar_prefetch=4,
        grid=(T // tm,),
        in_specs=[pl.BlockSpec((tm, D), lambda t, *_: (t, 0)),
                  pl.BlockSpec((meta_t.shape[0], tm), lambda t, *_: (0, t))],
        out_specs=pl.BlockSpec(memory_space=pl.ANY),
        scratch_shapes=[pltpu.VMEM((2, n_exp * DISPATCH_ROWS, D), BF16),
                        pltpu.VMEM((DISPATCH_ROWS, D), BF16),
                        pltpu.VMEM((DISPATCH_ROWS, D), BF16),
                        pltpu.SemaphoreType.DMA((2,)),
                        pltpu.SemaphoreType.DMA(())],
    )
    return pl.pallas_call(
        functools.partial(_dispatch_kernel, n_exp=n_exp),
        grid_spec=grid_spec,
        out_shape=jax.ShapeDtypeStruct((n_rows, D), BF16),
        compiler_params=_params(("arbitrary",)),
        name="moe_dispatch",
    )(base_te, cnt_te, pad_lo, pad_n, u, meta_t)


TILE_FULL, TILE_HALF, TILE_EMPTY = 0, 1, 2


def _expert_kernel(te_ref, na_ref, fill_ref, xs_ref, wg_ref, wu_ref, wd_ref, o_ref, acc_ref):
    i = pl.program_id(0)
    j = pl.program_id(1)
    last = pl.num_programs(1) - 1
    active = i < na_ref[0]
    fill = fill_ref[i]
    tm = acc_ref.shape[0]

    @pl.when(active & (j == 0))
    def _():
        acc_ref[...] = jnp.zeros_like(acc_ref)

    def swiglu(rows):
        xb = xs_ref[rows, :]
        g = jnp.dot(xb, wg_ref[...].astype(BF16), preferred_element_type=F32)
        up = jnp.dot(xb, wu_ref[...].astype(BF16), preferred_element_type=F32)
        h = (g * _sigmoid(g) * up).astype(BF16)
        acc_ref[rows, :] += jnp.dot(h, wd_ref[...].astype(BF16), preferred_element_type=F32)

    pl.when(active & (fill == TILE_FULL))(functools.partial(swiglu, pl.ds(0, tm)))
    pl.when(active & (fill == TILE_HALF))(functools.partial(swiglu, pl.ds(0, tm // 2)))

    @pl.when(active & (j == last))
    def _():
        o_ref[...] = acc_ref[...].astype(BF16)

    @pl.when(jnp.logical_not(active) & (j == last))
    def _():
        o_ref[...] = jnp.zeros_like(o_ref)


def _expert_ffn(xs, tile_expert, n_active, tile_fill, wg, wu, wd, *, tm, tf):
    n_rows, D = xs.shape
    ff = wg.shape[2]
    assert n_rows % tm == 0 and ff % tf == 0
    nj = ff // tf

    def jj(i, j, na):
        return jnp.where(i < na[0], j, nj - 1)

    def ii(i, na):
        return jnp.minimum(i, na[0] - 1)

    grid_spec = pltpu.PrefetchScalarGridSpec(
        num_scalar_prefetch=3,
        grid=(n_rows // tm, nj),
        in_specs=[pl.BlockSpec((tm, D), lambda i, j, te, na, tf_: (ii(i, na), 0)),
                  pl.BlockSpec((None, D, tf), lambda i, j, te, na, tf_: (te[i], 0, jj(i, j, na))),
                  pl.BlockSpec((None, D, tf), lambda i, j, te, na, tf_: (te[i], 0, jj(i, j, na))),
                  pl.BlockSpec((None, tf, D), lambda i, j, te, na, tf_: (te[i], jj(i, j, na), 0))],
        out_specs=pl.BlockSpec((tm, D), lambda i, j, te, na, tf_: (i, 0)),
        scratch_shapes=[pltpu.VMEM((tm, D), F32)],
    )
    return pl.pallas_call(
        _expert_kernel,
        grid_spec=grid_spec,
        out_shape=jax.ShapeDtypeStruct((n_rows, D), BF16),
        compiler_params=_params(("arbitrary", "arbitrary")),
        name="moe_experts",
    )(tile_expert, n_active, tile_fill, xs, wg, wu, wd)


def _combine_kernel(base_ref, off_ref, cnt_ref, ys_hbm, meta_ref, x_ref, gate_ref, lg_ref, lb_ref, o_ref,
                    stage, extra, acc_ref, sems, extra_sem, *, alpha, n_exp):
    t = pl.program_id(0) * pl.num_programs(1) + pl.program_id(1)
    n_steps = pl.num_programs(0) * pl.num_programs(1)
    tm = x_ref.shape[0]
    R = DISPATCH_ROWS
    slot = t % 2

    def main_copies(step, buf, fn):
        for e in range(n_exp):
            fn(pltpu.make_async_copy(_rows_at(ys_hbm, base_ref[step * n_exp + e], R),
                                     stage.at[buf, pl.ds(e * R, R)], sems.at[buf]))

    @pl.when(t == 0)
    def _():
        main_copies(0, 0, lambda cp: cp.start())

    @pl.when(t + 1 < n_steps)
    def _():
        main_copies(t + 1, 1 - slot, lambda cp: cp.start())

    meta = meta_ref[...]
    col = lambda k: meta[:, k:k + 1]
    e0, e1 = col(META_E0).astype(I32), col(META_E1).astype(I32)
    r0, r1 = col(META_R0).astype(I32), col(META_R1).astype(I32)
    w0, w1 = col(META_W0), col(META_W1)
    lane = lax.broadcasted_iota(I32, (tm, R), 1)

    rels, ws, picks = [], [], []
    for e in range(n_exp):
        rel = jnp.where(e0 == e, r0, jnp.where(e1 == e, r1, -1))
        rels.append(jnp.where(rel >= 0, rel - off_ref[t * n_exp + e], -1))
        ws.append(jnp.where(e0 == e, w0, jnp.where(e1 == e, w1, 0.0)))
        picks.append(jnp.where(lane == rels[e], ws[e], 0.0))
    weighted = jnp.concatenate(picks, axis=1).astype(BF16)

    acc_ref[...] = jnp.zeros_like(acc_ref)
    for e in range(n_exp):
        for blk in range(1, -(-tm // R)):
            @pl.when(cnt_ref[t * n_exp + e] > blk * R)
            def _(e=e, blk=blk):
                cp = pltpu.make_async_copy(_rows_at(ys_hbm, base_ref[t * n_exp + e] + blk * R, R),
                                           extra, extra_sem)
                cp.start()
                cp.wait()
                more = jnp.where(lane + blk * R == rels[e], ws[e], 0.0).astype(BF16)
                acc_ref[...] += jnp.dot(more, extra[...], preferred_element_type=F32)

    main_copies(t, slot, lambda cp: cp.wait())
    y = acc_ref[...] + jnp.dot(weighted, stage[slot], preferred_element_type=F32)
    z = alpha * x_ref[...] + (1.0 + gate_ref[...]) * y
    o_ref[...] = _layer_norm(z, lg_ref[...], lb_ref[...])


def _combine(ys, base_te, off_te, cnt_te, meta, x, gate, ln_g, ln_b, *, alpha, tm):
    B, S, D = x.shape
    nt = S // tm
    n_exp = base_te.shape[0] // (B * nt)
    row = pl.BlockSpec((None, tm, D), lambda b, i, *_: (b, i, 0))
    vec = pl.BlockSpec((1, D), lambda b, i, *_: (0, 0))
    grid_spec = pltpu.PrefetchScalarGridSpec(
        num_scalar_prefetch=3,
        grid=(B, nt),
        in_specs=[pl.BlockSpec(memory_space=pl.ANY),
                  pl.BlockSpec((None, tm, LANES), lambda b, i, *_: (b, i, 0)),
                  row, pl.BlockSpec((None, 1, D), lambda b, i, *_: (b, 0, 0)), vec, vec],
        out_specs=row,
        scratch_shapes=[pltpu.VMEM((2, n_exp * DISPATCH_ROWS, D), BF16),
                        pltpu.VMEM((DISPATCH_ROWS, D), BF16),
                        pltpu.VMEM((tm, D), F32),
                        pltpu.SemaphoreType.DMA((2,)),
                        pltpu.SemaphoreType.DMA(())],
    )
    return pl.pallas_call(
        functools.partial(_combine_kernel, alpha=alpha, n_exp=n_exp),
        grid_spec=grid_spec,
        out_shape=jax.ShapeDtypeStruct((B, S, D), F32),
        compiler_params=_params(("arbitrary", "arbitrary")),
        name="moe_combine",
    )(base_te, off_te, cnt_te, ys, meta, x, gate, ln_g, ln_b)


def _moe_ffn(x, scale, shift, gate, w_router, wg, wu, wd, ln_g, ln_b, *, alpha, tm, tm_e, tf):
    B, S, D = x.shape
    T = B * S
    nt = T // tm
    n_exp = w_router.shape[1]
    wr = jnp.zeros((D, LANES), F32).at[:, :n_exp].set(w_router)
    wr_hi = wr.astype(BF16)
    wr = jnp.stack([wr_hi, (wr - wr_hi.astype(F32)).astype(BF16)])
    u, meta = _router(x, scale, shift, wr, n_exp=n_exp, tm=tm)

    meta2 = meta.reshape(T, LANES)
    e0 = meta2[:, META_E0].astype(I32)
    e1 = meta2[:, META_E1].astype(I32)
    r0 = meta2[:, META_R0].astype(I32)
    r1 = meta2[:, META_R1].astype(I32)
    experts = jnp.arange(n_exp, dtype=I32)
    chosen = ((e0[:, None] == experts) | (e1[:, None] == experts)).astype(I32)
    cnt_te = chosen.reshape(nt, tm, n_exp).sum(axis=1)
    seg_te = -(-cnt_te // SEGMENT_ALIGN) * SEGMENT_ALIGN
    off_te = jnp.cumsum(cnt_te, axis=0) - cnt_te
    filled = seg_te.sum(axis=0)
    sizes = -(-(filled + DISPATCH_SLACK) // tm_e) * tm_e
    ends = jnp.cumsum(sizes)
    starts = ends - sizes
    base_te = starts[None, :] + jnp.cumsum(seg_te, axis=0) - seg_te
    shift_te = (base_te - off_te)[:, None, :]

    def rows_of(e, r):
        pick = (e[:, None] == experts).astype(I32).reshape(nt, tm, n_exp)
        return r + (pick * shift_te).sum(axis=-1).reshape(T)

    d0, d1 = rows_of(e0, r0), rows_of(e1, r1)
    bound = TOP_K * T + n_exp * ((SEGMENT_ALIGN - 1) * nt + DISPATCH_SLACK)
    n_tiles = -(-bound // tm_e) + n_exp + 1
    n_active = (ends[-1] // tm_e).astype(I32)
    tile_id = jnp.minimum(jnp.arange(n_tiles, dtype=I32), n_active - 1)
    tile_expert = jnp.sum((ends[None, :] <= (tile_id * tm_e)[:, None]).astype(I32), axis=1)
    data_end = jnp.sum((tile_expert[:, None] == experts) * (starts + filled)[None, :], axis=1)
    left = data_end - tile_id * tm_e
    tile_fill = jnp.where(left <= 0, TILE_EMPTY, jnp.where(left <= tm_e // 2, TILE_HALF, TILE_FULL)).astype(I32)
    meta_t = jnp.stack([e0, e1, d0, d1] + [jnp.zeros_like(e0)] * 4)

    xs = _dispatch(u.reshape(T, D), meta_t, base_te.reshape(-1), cnt_te.reshape(-1),
                   jnp.concatenate([starts + filled, ends[-1:]]),
                   jnp.concatenate([sizes - filled, n_tiles * tm_e - ends[-1:]]),
                   n_rows=n_tiles * tm_e, tm=tm)
    ys = _expert_ffn(xs, tile_expert, n_active.reshape(1), tile_fill, wg, wu, wd, tm=tm_e, tf=tf)
    return _combine(ys, base_te.reshape(-1), off_te.reshape(-1), cnt_te.reshape(-1),
                    meta, x, gate, ln_g, ln_b, alpha=alpha, tm=tm)


class _Tiles(NamedTuple):
    token_rows: int
    rope_rows: int
    ffn_chunk: int
    expert_rows: int
    expert_ff: int


def _tile_plan(batch, seq):
    return _Tiles(token_rows=min(512, seq), rope_rows=min(1024, seq), ffn_chunk=2 * LANES,
                  expert_rows=min(1024, batch * seq), expert_ff=512)


def kernel(x, c, positions, w_in, w_out, attn_norm_gain, rec_norm_gain, rec_lb_logits, ada_w, ada_b,
           ln_gain, ln_bias, ffn_w_gate, ffn_w_up, ffn_w_down, moe_router, moe_w_gate, moe_w_up,
           moe_w_down):
    B, S, D = x.shape
    depth = w_in.shape[0]
    alpha = (2 * depth) ** 0.25

    p = jax.nn.softmax(rec_lb_logits.astype(F32), axis=0)
    cum = jnp.cumsum(p, axis=0)
    lb_all = cum - cum[0:1]
    half = ATTN_HEAD_DIM // 2
    inv_freq = ROPE_THETA ** (-jnp.arange(half, dtype=F32) / half)
    inv_freq_lanes = jnp.tile(inv_freq, LANES // half).reshape(1, LANES)
    pos_f = positions.astype(F32).reshape(B, S, 1)
    n_qk = 2 * ATTN_WIDTH
    w_in_b = jnp.concatenate([_qk_lane_order(w_in[..., :n_qk]), w_in[..., n_qk:]], axis=-1).astype(BF16)
    w_out_b = w_out.astype(BF16)
    ff = ffn_w_gate.shape[2]
    ffp = -(-ff // (2 * LANES)) * (2 * LANES)
    pad_c = lambda w: jnp.pad(w.astype(BF16), ((0, 0), (0, 0), (0, ffp - ff)))
    ffn_g, ffn_u = pad_c(ffn_w_gate), pad_c(ffn_w_up)
    ffn_d = jnp.pad(ffn_w_down.astype(BF16), ((0, 0), (0, ffp - ff), (0, 0)))

    mods = _ada_modulation(c, ada_w, ada_b)
    tiles = _tile_plan(B, S)
    rope_cos, rope_sin = _rope_tables(pos_f, inv_freq_lanes, tm=tiles.rope_rows)

    def mod(layer, sub):
        m = mods[layer * 2 + sub].reshape(3, B, 1, D)
        return m[0], m[1], m[2]

    vec = lambda a: a.reshape(1, -1)
    for layer in range(depth):
        shift, scale, gate = mod(layer, 0)
        proj = _in_projection(x, scale, shift, rope_cos, rope_sin, w_in_b, layer, tm=tiles.token_rows)
        attn = _dilated_attention(proj)
        rec = _hgrn2(proj, lb_all[layer], vec(rec_norm_gain[layer]))
        x = _out_projection(attn, rec, x, gate, vec(attn_norm_gain[layer]), w_out_b, layer,
                            vec(ln_gain[layer, 0]), vec(ln_bias[layer, 0]), alpha=alpha,
                            tm=tiles.token_rows)
        shift, scale, gate = mod(layer, 1)
        j = layer // 2
        if layer % 2 == 0:
            x = _dense_ffn(x, scale, shift, gate, ffn_g[j], ffn_u[j], ffn_d[j],
                           vec(ln_gain[layer, 1]), vec(ln_bias[layer, 1]),
                           alpha=alpha, tm=tiles.token_rows, tc=tiles.ffn_chunk)
        else:
            x = _moe_ffn(x, scale, shift, gate, moe_router[j], moe_w_gate[j], moe_w_up[j],
                         moe_w_down[j], vec(ln_gain[layer, 1]), vec(ln_bias[layer, 1]),
                         alpha=alpha, tm=tiles.token_rows, tm_e=tiles.expert_rows, tf=tiles.expert_ff)
    return x
```

```python
import functools
from typing import NamedTuple

import jax
import jax.numpy as jnp
from jax import lax
from jax.experimental import pallas as pl
from jax.experimental.pallas import tpu as pltpu

F32 = jnp.float32
BF16 = jnp.bfloat16
I32 = jnp.int32

LANES = 128
V7X_VMEM_LIMIT_BYTES = 56 * 1024 * 1024

ATTN_WIDTH = 512
ATTN_HEAD_DIM = 64
DILATIONS = (1, 4, 16)
ATTN_RADIUS = 64
ROPE_THETA = 10000.0
REC_WIDTH = 512
REC_DIM = 128
REC_CHUNK = 64
TOP_K = 2
LN_EPS = 1e-5
RMS_EPS = 1e-6
MASK_VALUE = -1e30
DECAY_EXP_CLAMP = 80.0
LOG2_E = 1.4426950408889634


def _params(semantics):
    return pltpu.CompilerParams(dimension_semantics=semantics,
                                vmem_limit_bytes=V7X_VMEM_LIMIT_BYTES)


def _sigmoid(x):
    return 1.0 / (1.0 + jnp.exp(-x))


def _layer_norm(z, gain, bias):
    mu = jnp.mean(z, axis=-1, keepdims=True)
    zc = z - mu
    var = jnp.mean(zc * zc, axis=-1, keepdims=True)
    return zc * lax.rsqrt(var + LN_EPS) * gain + bias


def _ada_kernel(c_ref, w_ref, b_ref, o_ref):
    c = c_ref[...]
    o_ref[...] = jnp.dot(c * _sigmoid(c), w_ref[...], precision=lax.Precision.HIGHEST,
                         preferred_element_type=F32) + b_ref[...]


def _ada_modulation(c, ada_w, ada_b):
    B, D = c.shape
    n = ada_w.shape[0] * ada_w.shape[1]
    n3 = ada_w.shape[-1]
    tn = D
    return pl.pallas_call(
        _ada_kernel,
        grid=(n, n3 // tn),
        in_specs=[pl.BlockSpec((B, D), lambda i, j: (0, 0)),
                  pl.BlockSpec((None, D, tn), lambda i, j: (i, 0, j)),
                  pl.BlockSpec((None, 1, tn), lambda i, j: (i, 0, j))],
        out_specs=pl.BlockSpec((None, None, B, tn), lambda i, j: (i, j, 0, 0)),
        out_shape=jax.ShapeDtypeStruct((n, n3 // tn, B, tn), F32),
        compiler_params=_params(("arbitrary", "arbitrary")),
        name="ada_modulation",
    )(c, ada_w.reshape(n, D, n3), ada_b.reshape(n, 1, n3))


ROPE_HALF = ATTN_HEAD_DIM // 2


def _qk_lane_order(w_cols):
    lead = w_cols.shape[:-1]
    w = w_cols.reshape(*lead, -1, 2, 2, ROPE_HALF)
    return jnp.swapaxes(w, -3, -2).reshape(*lead, -1)


def _rope_kernel(pos_ref, invf_ref, cos_ref, sin_ref):
    ang = pos_ref[...] * invf_ref[...]
    lane = lax.broadcasted_iota(I32, (1, LANES), 1)
    sin = jnp.sin(ang)
    cos_ref[...] = jnp.cos(ang)
    sin_ref[...] = jnp.where(lane < LANES // 2, -sin, sin)


def _rope_tables(pos_f, inv_freq_lanes, *, tm):
    B, S, _ = pos_f.shape
    tab = pl.BlockSpec((None, tm, LANES), lambda b, i: (b, i, 0))
    return pl.pallas_call(
        _rope_kernel,
        grid=(B, S // tm),
        in_specs=[pl.BlockSpec((None, tm, 1), lambda b, i: (b, i, 0)),
                  pl.BlockSpec((1, LANES), lambda b, i: (0, 0))],
        out_specs=[tab, tab],
        out_shape=[jax.ShapeDtypeStruct((B, S, LANES), F32)] * 2,
        compiler_params=_params(("arbitrary", "arbitrary")),
        name="rope_tables",
    )(pos_f, inv_freq_lanes)


def _inproj_kernel(x_ref, sc_ref, sh_ref, cos_ref, sin_ref, w_ref, o_ref, *, tn):
    u = (x_ref[...] * (1.0 + sc_ref[...]) + sh_ref[...]).astype(BF16)
    cos = cos_ref[...]
    sin_signed = sin_ref[...]
    q_scale = ATTN_HEAD_DIM ** -0.5
    for j in range(w_ref.shape[1] // tn):
        acc = jnp.dot(u, w_ref[:, j * tn:(j + 1) * tn], preferred_element_type=F32)
        if j > 0:
            o_ref[:, j * tn:(j + 1) * tn] = acc
            continue
        for kk in range(tn // LANES):
            c = acc[:, kk * LANES:(kk + 1) * LANES]
            r = c * cos + pltpu.roll(c, LANES // 2, 1) * sin_signed
            if kk * LANES < ATTN_WIDTH:
                r = r * q_scale
            o_ref[:, kk * LANES:(kk + 1) * LANES] = r


def _in_projection(x, scale, shift, rope_cos, rope_sin, w_in_bf16, layer, *, tm):
    B, S, D = x.shape
    n_cols = w_in_bf16.shape[2]
    tn = 2 * ATTN_WIDTH
    assert S % tm == 0 and n_cols % tn == 0
    return pl.pallas_call(
        functools.partial(_inproj_kernel, tn=tn),
        grid=(B, S // tm),
        in_specs=[pl.BlockSpec((None, tm, D), lambda b, i: (b, i, 0)),
                  pl.BlockSpec((None, 1, D), lambda b, i: (b, 0, 0)),
                  pl.BlockSpec((None, 1, D), lambda b, i: (b, 0, 0)),
                  pl.BlockSpec((None, tm, LANES), lambda b, i: (b, i, 0)),
                  pl.BlockSpec((None, tm, LANES), lambda b, i: (b, i, 0)),
                  pl.BlockSpec((None, D, n_cols), lambda b, i: (layer, 0, 0))],
        out_specs=pl.BlockSpec((None, tm, n_cols), lambda b, i: (b, i, 0)),
        out_shape=jax.ShapeDtypeStruct((B, S, n_cols), F32),
        compiler_params=_params(("arbitrary", "arbitrary")),
        name="in_projection",
    )(x, scale, shift, rope_cos, rope_sin, w_in_bf16)


ATTN_BLOCK_GROUP = 16


def _attn_kernel(q_ref, k_ref, v_ref, o_ref, qs, ks, vs, qf, kf, vf, pm, plr, pa, nm, nl, na, bias_s,
                 *, S):
    lane = lax.broadcasted_iota(I32, (1, LANES), 1)
    head0 = lane < ATTN_HEAD_DIM
    head0_qk = (lane % ATTN_HEAD_DIM) < ROPE_HALF

    for p, d in enumerate(DILATIONS):
        L = S // d
        tq = min(128, L)
        W = min(2 * tq, L)
        nb = L // tq

        d_prev = DILATIONS[p - 1] if p else 1
        ratio, l_prev = d // d_prev, S // d_prev
        keep_f32 = 0 < p < len(DILATIONS) - 1
        for src, stage, dst in ((q_ref, qf, qs), (k_ref, kf, ks), (v_ref, vf, vs)):
            src = src if p <= 1 else stage
            for r in range(d):
                a, r_prev = divmod(r, d_prev)
                rows = pl.ds(r_prev * l_prev + a, L, stride=ratio) if ratio > 1 else pl.ds(0, L)
                val = src[rows, :]
                if keep_f32:
                    stage[r * L:(r + 1) * L, :] = val
                dst[r * L:(r + 1) * L, :] = val.astype(BF16)

        n_blocks = d * nb
        rc = lax.broadcasted_iota(I32, (tq, W), 0) - lax.broadcasted_iota(I32, (tq, W), 1)
        for which, delta in enumerate((0, ATTN_RADIUS, tq)):
            bias_s[which, pl.ds(0, tq), pl.ds(0, W)] = jnp.where(
                jnp.abs(rc + delta) <= ATTN_RADIUS, 0.0, MASK_VALUE)
        group = min(ATTN_BLOCK_GROUP, n_blocks)
        assert n_blocks % group == 0

        def block_group(gi, carry, L=L, tq=tq, W=W, nb=nb, group=group):
            q0s, k0s, scores, probs, sums = {}, {}, {}, {}, {}

            def score_stage(u):
                g = gi * group + u
                r = g // nb
                n = g - r * nb
                base = r * L
                q0 = pl.multiple_of(base + n * tq, tq)
                ws = jnp.clip(n * tq - ATTN_RADIUS, 0, L - W)
                k0 = pl.multiple_of(base + ws, 16)
                qb = qs[pl.ds(q0, tq), :]
                zero = jnp.zeros_like(qb)
                q2 = jnp.concatenate([jnp.where(head0_qk, qb, zero), jnp.where(head0_qk, zero, qb)], axis=0)
                s = lax.dot_general(q2, ks[pl.ds(k0, W), :], (((1,), (1,)), ((), ())),
                                    preferred_element_type=F32)
                delta = n * tq - ws
                which = jnp.where(delta == 0, 0, jnp.where(delta == ATTN_RADIUS, 1, 2))
                bias = bias_s[which, pl.ds(0, tq), pl.ds(0, W)]
                scores[u] = s + jnp.concatenate([bias, bias], axis=0)
                q0s[u], k0s[u] = q0, k0

            def softmax_stage(u):
                s = scores.pop(u)
                m = jnp.max(s, axis=1, keepdims=True)
                e = jnp.exp(s - m)
                probs[u] = (m, e.astype(BF16))
                sums[u] = jnp.sum(e, axis=1, keepdims=True)

            def value_stage(u):
                m, e = probs.pop(u)
                l = sums.pop(u)
                out = jnp.dot(e, vs[pl.ds(k0s[u], W), :], preferred_element_type=F32)
                rows = pl.ds(q0s[u], tq)
                out_m[rows, :] = jnp.where(head0, m[:tq], m[tq:])
                out_l[rows, :] = jnp.where(head0, l[:tq], l[tq:])
                out_a[rows, :] = jnp.where(head0, out[:tq], out[tq:])

            for step in range(group + 2):
                if step < group:
                    score_stage(step)
                if 1 <= step <= group:
                    softmax_stage(step - 1)
                if step >= 2:
                    value_stage(step - 2)
            return carry

        out_m, out_l, out_a = (nm.at[p], nl.at[p], na.at[p]) if d == 1 else (pm, plr, pa)
        lax.fori_loop(0, n_blocks // group, block_group, 0)

        for r in range(d if d > 1 else 0):
            rows = pl.ds(r, L, stride=d)
            nm[p, rows, :] = pm[r * L:(r + 1) * L, :]
            nl[p, rows, :] = plr[r * L:(r + 1) * L, :]
            na[p, rows, :] = pa[r * L:(r + 1) * L, :]

    m_all = jnp.maximum(jnp.maximum(nm[0], nm[1]), nm[2])
    num = jnp.zeros((S, LANES), F32)
    den = jnp.zeros((S, LANES), F32)
    for p in range(len(DILATIONS)):
        w = jnp.exp(nm[p] - m_all)
        num = num + w * na[p]
        den = den + w * nl[p]
    o_ref[...] = (num / den).astype(o_ref.dtype)


def _dilated_attention(proj):
    B, S, _ = proj.shape
    n_pairs = ATTN_WIDTH // LANES
    assert S % (16 * DILATIONS[-1]) == 0
    blk = lambda off: pl.BlockSpec((None, S, LANES), lambda b, h: (b, 0, off + h))
    return pl.pallas_call(
        functools.partial(_attn_kernel, S=S),
        grid=(B, n_pairs),
        in_specs=[blk(0), blk(n_pairs), blk(2 * n_pairs)],
        out_specs=pl.BlockSpec((None, S, LANES), lambda b, h: (b, 0, h)),
        out_shape=jax.ShapeDtypeStruct((B, S, ATTN_WIDTH), BF16),
        scratch_shapes=[pltpu.VMEM((S, LANES), BF16)] * 3
                       + [pltpu.VMEM((S, LANES), F32)] * 6
                       + [pltpu.VMEM((len(DILATIONS), S, LANES), F32)] * 3
                       + [pltpu.VMEM((3, min(128, S), min(256, S)), F32)],
        compiler_params=_params(("arbitrary", "arbitrary")),
        name="dilated_attention",
    )(proj, proj, proj)


REC_GROUP_CHUNKS = 4
REC_GROUP_ROWS = REC_GROUP_CHUNKS * REC_CHUNK
REC_STAGE_GROUPS = 8


def _hgrn_kernel(rq_ref, zf_ref, zb_ref, ri_ref, rg_ref, lb_ref, gain_ref, o_ref,
                 q_s, vt_s, oi_s, qp_s, ut_s, dd_s, st_s, os_s, *, S):
    C, G, GR = REC_CHUNK, REC_GROUP_CHUNKS, REC_GROUP_ROWS
    nc, ng = S // C, S // GR
    sg = min(REC_STAGE_GROUPS, ng)
    assert ng % sg == 0

    row = lax.broadcasted_iota(I32, (GR, GR), 0)
    col = lax.broadcasted_iota(I32, (GR, GR), 1)
    same_chunk = (row // C) == (col // C)
    t_i = lax.broadcasted_iota(I32, (C, C), 0)
    s_i = lax.broadcasted_iota(I32, (C, C), 1)

    def prep(i, carry):
        r0 = pl.multiple_of(i * GR, GR)
        rq = rq_ref[pl.ds(r0, GR), :]
        q_s[pl.ds(r0, GR), :] = rq * _sigmoid(rq)
        v = ri_ref[pl.ds(r0, GR), :]
        for c in range(G):
            vt_s[i * G + c] = v[c * C:(c + 1) * C, :].T.astype(BF16)
        return carry

    lax.fori_loop(0, ng, prep, 0, unroll=True)

    for direction in range(2):
        fwd = direction == 0
        z_ref = zf_ref if fwd else zb_ref
        cum = jnp.where(same_chunk & ((col <= row) if fwd else (col >= row)), 1.0, 0.0).astype(BF16)
        keep = (s_i <= t_i) if fwd else (s_i >= t_i)
        mid_row = C // 2 - 1 if fwd else C // 2
        last_row = C - 1 if fwd else 0

        def phase_a(it, carry, fwd=fwd, z_ref=z_ref, cum=cum, keep=keep,
                    mid_row=mid_row, last_row=last_row, direction=direction):
            lb = lb_ref[direction:direction + 1, :]
            log_lb = jnp.log(lb)
            log_1m_lb = jnp.log1p(-lb)
            r0s, c0s, k3s, bcs, atts, stage3 = [], [], [], [], [], []
            for u in range(sg):
                i = it * sg + u
                r0 = pl.multiple_of(i * GR, GR)
                z = z_ref[pl.ds(r0, GR), :]
                e = jnp.exp(-jnp.abs(z))
                t = log_1m_lb + (jnp.minimum(z, 0.0) - jnp.log(1.0 + e))
                g = jnp.maximum(log_lb, t) + jnp.log(1.0 + jnp.exp(-jnp.abs(log_lb - t)))
                kk = (1.0 - lb) * (jnp.where(z > 0, e, 1.0) / (1.0 + e))
                g = g * LOG2_E
                g1 = g.astype(BF16)
                r1 = g - g1.astype(F32)
                g2 = r1.astype(BF16)
                g3 = (r1 - g2.astype(F32)).astype(BF16)
                bcs.append(jnp.dot(cum, g1, preferred_element_type=F32)
                           + jnp.dot(cum, g2, preferred_element_type=F32)
                           + jnp.dot(cum, g3, preferred_element_type=F32))
                k3s.append(kk.reshape(G, C, REC_DIM))
                r0s.append(r0)
                c0s.append(pl.multiple_of(i * G, G))
            for u in range(sg):
                b3 = bcs[u].reshape(G, C, REC_DIM)
                mid = b3[:, mid_row:mid_row + 1, :]
                last = b3[:, last_row:last_row + 1, :]
                e3 = b3 - mid
                q3 = q_s[pl.ds(r0s[u], GR), :].reshape(G, C, REC_DIM)
                clamp = DECAY_EXP_CLAMP * LOG2_E
                qt = (q3 * jnp.exp2(jnp.minimum(e3, clamp))).astype(BF16)
                kt = (k3s[u] * jnp.exp2(jnp.minimum(-e3, clamp))).astype(BF16)
                atts.append(jnp.einsum('gtc,gsc->gts', qt, kt, preferred_element_type=F32))
                qp_s[pl.ds(r0s[u], GR), :] = (q3 * jnp.exp2(b3)).astype(BF16).reshape(GR, REC_DIM)
                stage3.append(((k3s[u] * jnp.exp2(last - b3)).astype(BF16), jnp.exp2(last)))
            for u in range(sg):
                att = jnp.where(keep[None], atts[u], 0.0).astype(BF16)
                v3 = ri_ref[pl.ds(r0s[u], GR), :].reshape(G, C, REC_DIM).astype(BF16)
                oi = jnp.einsum('gts,gsv->gtv', att, v3, preferred_element_type=F32)
                oi_s[pl.ds(r0s[u], GR), :] = oi.reshape(GR, REC_DIM)
                kp, decay = stage3[u]
                ut_s[pl.ds(c0s[u], G)] = jnp.einsum('gvs,gsc->gvc', vt_s[pl.ds(c0s[u], G)], kp,
                                                    preferred_element_type=F32)
                dd_s[pl.ds(c0s[u], G)] = decay
            return carry

        lax.fori_loop(0, ng // sg, phase_a, 0)

        def scan(c, st, fwd=fwd):
            idx = c if fwd else nc - 1 - c
            st_s[idx] = st.astype(BF16)
            return dd_s[idx] * st + ut_s[idx]

        lax.fori_loop(0, nc, scan, jnp.zeros((REC_DIM, REC_DIM), F32), unroll=True)

        def phase_c(it, carry, fwd=fwd):
            r0s = [pl.multiple_of((it * sg + u) * GR, GR) for u in range(sg)]
            oos = []
            for u in range(sg):
                c0 = pl.multiple_of((it * sg + u) * G, G)
                qp3 = qp_s[pl.ds(r0s[u], GR), :].reshape(G, C, REC_DIM)
                oos.append(jnp.einsum('gtc,gvc->gtv', qp3, st_s[pl.ds(c0, G)],
                                      preferred_element_type=F32))
            for u in range(sg):
                rows = pl.ds(r0s[u], GR)
                tot = oi_s[rows, :] + oos[u].reshape(GR, REC_DIM)
                os_s[rows, :] = tot if fwd else os_s[rows, :] + tot
            return carry

        lax.fori_loop(0, ng // sg, phase_c, 0)

    o = os_s[...]
    o = o * lax.rsqrt(jnp.mean(o * o, axis=-1, keepdims=True) + RMS_EPS) * gain_ref[...]
    o_ref[...] = (o * _sigmoid(rg_ref[...])).astype(o_ref.dtype)


def _hgrn2(proj, lb, rec_gain):
    B, S, _ = proj.shape
    nh = REC_WIDTH // REC_DIM
    assert S % REC_GROUP_ROWS == 0
    col0 = 3 * ATTN_WIDTH // LANES
    blk = lambda k: pl.BlockSpec((None, S, REC_DIM), lambda b, h: (b, 0, col0 + k * nh + h))
    nc = S // REC_CHUNK
    return pl.pallas_call(
        functools.partial(_hgrn_kernel, S=S),
        grid=(B, nh),
        in_specs=[blk(0), blk(1), blk(2), blk(3), blk(4),
                  pl.BlockSpec((2, REC_DIM), lambda b, h: (0, h)),
                  pl.BlockSpec((1, REC_DIM), lambda b, h: (0, h))],
        out_specs=pl.BlockSpec((None, S, REC_DIM), lambda b, h: (b, 0, h)),
        out_shape=jax.ShapeDtypeStruct((B, S, REC_WIDTH), BF16),
        scratch_shapes=[pltpu.VMEM((S, REC_DIM), F32),
                        pltpu.VMEM((nc, REC_DIM, REC_CHUNK), BF16),
                        pltpu.VMEM((S, REC_DIM), F32),
                        pltpu.VMEM((S, REC_DIM), BF16),
                        pltpu.VMEM((nc, REC_DIM, REC_DIM), F32),
                        pltpu.VMEM((nc, 1, REC_DIM), F32),
                        pltpu.VMEM((nc, REC_DIM, REC_DIM), BF16),
                        pltpu.VMEM((S, REC_DIM), F32)],
        compiler_params=_params(("arbitrary", "arbitrary")),
        name="hgrn2",
    )(proj, proj, proj, proj, proj, lb, rec_gain)


def _outproj_kernel(a_ref, r_ref, x_ref, gate_ref, ag_ref, w_ref, lg_ref, lb_ref, o_ref, *, alpha):
    a = a_ref[...].astype(F32)
    an = a * lax.rsqrt(jnp.mean(a * a, axis=-1, keepdims=True) + RMS_EPS) * ag_ref[...]
    y = (jnp.dot(an.astype(BF16), w_ref[:ATTN_WIDTH, :], preferred_element_type=F32)
         + jnp.dot(r_ref[...], w_ref[ATTN_WIDTH:, :], preferred_element_type=F32))
    z = alpha * x_ref[...] + (1.0 + gate_ref[...]) * y
    o_ref[...] = _layer_norm(z, lg_ref[...], lb_ref[...])


def _out_projection(attn, rec, x, gate, attn_gain, w_out_bf16, layer, ln_g, ln_b, *, alpha, tm):
    B, S, D = x.shape
    row = lambda w: pl.BlockSpec((None, tm, w), lambda b, i: (b, i, 0))
    vec = lambda w: pl.BlockSpec((1, w), lambda b, i: (0, 0))
    return pl.pallas_call(
        functools.partial(_outproj_kernel, alpha=alpha),
        grid=(B, S // tm),
        in_specs=[row(ATTN_WIDTH), row(REC_WIDTH), row(D),
                  pl.BlockSpec((None, 1, D), lambda b, i: (b, 0, 0)),
                  vec(ATTN_WIDTH),
                  pl.BlockSpec((None, ATTN_WIDTH + REC_WIDTH, D), lambda b, i: (layer, 0, 0)),
                  vec(D), vec(D)],
        out_specs=row(D),
        out_shape=jax.ShapeDtypeStruct((B, S, D), F32),
        compiler_params=_params(("arbitrary", "arbitrary")),
        name="out_projection",
    )(attn, rec, x, gate, attn_gain, w_out_bf16, ln_g, ln_b)


def _ffn_kernel(x_ref, sc_ref, sh_ref, gate_ref, wg_ref, wu_ref, wd_ref, lg_ref, lb_ref, o_ref,
                *, alpha, tc):
    x = x_ref[...]
    u = (x * (1.0 + sc_ref[...]) + sh_ref[...]).astype(BF16)
    acc = jnp.zeros(x.shape, F32)
    for c in range(wg_ref.shape[1] // tc):
        cols = slice(c * tc, (c + 1) * tc)
        g = jnp.dot(u, wg_ref[:, cols], preferred_element_type=F32)
        up = jnp.dot(u, wu_ref[:, cols], preferred_element_type=F32)
        h = (g * _sigmoid(g) * up).astype(BF16)
        acc = acc + jnp.dot(h, wd_ref[cols, :], preferred_element_type=F32)
    z = alpha * x + (1.0 + gate_ref[...]) * acc
    o_ref[...] = _layer_norm(z, lg_ref[...], lb_ref[...])


def _dense_ffn(x, scale, shift, gate, wg, wu, wd, ln_g, ln_b, *, alpha, tm, tc):
    B, S, D = x.shape
    ffp = wg.shape[1]
    assert ffp % tc == 0
    row = pl.BlockSpec((None, tm, D), lambda b, i: (b, i, 0))
    mod = pl.BlockSpec((None, 1, D), lambda b, i: (b, 0, 0))
    vec = pl.BlockSpec((1, D), lambda b, i: (0, 0))
    return pl.pallas_call(
        functools.partial(_ffn_kernel, alpha=alpha, tc=tc),
        grid=(B, S // tm),
        in_specs=[row, mod, mod, mod,
                  pl.BlockSpec((D, ffp), lambda b, i: (0, 0)),
                  pl.BlockSpec((D, ffp), lambda b, i: (0, 0)),
                  pl.BlockSpec((ffp, D), lambda b, i: (0, 0)),
                  vec, vec],
        out_specs=row,
        out_shape=jax.ShapeDtypeStruct((B, S, D), F32),
        compiler_params=_params(("arbitrary", "arbitrary")),
        name="dense_ffn",
    )(x, scale, shift, gate, wg, wu, wd, ln_g, ln_b)


META_E0, META_E1, META_W0, META_W1, META_R0, META_R1 = range(6)
SEGMENT_ALIGN = 16
DISPATCH_ROWS = 256
DISPATCH_HALF = DISPATCH_ROWS // 2
DISPATCH_SLACK = DISPATCH_ROWS


def _router_kernel(x_ref, sc_ref, sh_ref, wr_ref, u_ref, meta_ref, carry_ref, *, n_exp):
    tm = x_ref.shape[0]

    @pl.when((pl.program_id(0) == 0) & (pl.program_id(1) == 0))
    def _():
        carry_ref[...] = jnp.zeros_like(carry_ref)

    u = x_ref[...] * (1.0 + sc_ref[...]) + sh_ref[...]
    u_ref[...] = u.astype(BF16)
    u_hi = u.astype(BF16)
    u_lo = (u - u_hi.astype(F32)).astype(BF16)
    logits = (jnp.dot(u_hi, wr_ref[0], preferred_element_type=F32)
              + jnp.dot(u_lo, wr_ref[0], preferred_element_type=F32)
              + jnp.dot(u_hi, wr_ref[1], preferred_element_type=F32))
    lane = lax.broadcasted_iota(I32, (tm, LANES), 1)
    neg = -jnp.inf
    l1 = jnp.where(lane < n_exp, logits, neg)
    m1 = jnp.max(l1, axis=1, keepdims=True)
    i1 = jnp.min(jnp.where(l1 == m1, lane, LANES), axis=1, keepdims=True)
    l2 = jnp.where(lane == i1, neg, l1)
    m2 = jnp.max(l2, axis=1, keepdims=True)
    i2 = jnp.min(jnp.where(l2 == m2, lane, LANES), axis=1, keepdims=True)
    e = jnp.exp(m2 - m1)
    w1 = 1.0 / (1.0 + e)
    w2 = e * w1
    sel = jnp.where((lane == i1) | (lane == i2), 1.0, 0.0)
    before = (lax.broadcasted_iota(I32, (tm, tm), 1) < lax.broadcasted_iota(I32, (tm, tm), 0))
    ranks = jnp.dot(jnp.where(before, 1.0, 0.0).astype(BF16), sel.astype(BF16),
                    preferred_element_type=F32) + carry_ref[...]
    r1 = jnp.sum(jnp.where(lane == i1, ranks, 0.0), axis=1, keepdims=True)
    r2 = jnp.sum(jnp.where(lane == i2, ranks, 0.0), axis=1, keepdims=True)
    carry_ref[...] = carry_ref[...] + jnp.sum(sel, axis=0, keepdims=True)
    meta = jnp.zeros((tm, LANES), F32)
    for k, val in ((META_E0, i1.astype(F32)), (META_E1, i2.astype(F32)), (META_W0, w1),
                   (META_W1, w2), (META_R0, r1), (META_R1, r2)):
        meta = jnp.where(lane == k, val, meta)
    meta_ref[...] = meta


def _router(x, scale, shift, w_router_lanes, *, n_exp, tm):
    B, S, D = x.shape
    nt = S // tm
    row = pl.BlockSpec((None, tm, D), lambda b, i: (b, i, 0))
    mod = pl.BlockSpec((None, 1, D), lambda b, i: (b, 0, 0))
    return pl.pallas_call(
        functools.partial(_router_kernel, n_exp=n_exp),
        grid=(B, nt),
        in_specs=[row, mod, mod, pl.BlockSpec((2, D, LANES), lambda b, i: (0, 0, 0))],
        out_specs=[row, pl.BlockSpec((None, tm, LANES), lambda b, i: (b, i, 0))],
        out_shape=[jax.ShapeDtypeStruct((B, S, D), BF16),
                   jax.ShapeDtypeStruct((B, S, LANES), F32)],
        scratch_shapes=[pltpu.VMEM((1, LANES), F32)],
        compiler_params=_params(("arbitrary", "arbitrary")),
        name="moe_router",
    )(x, scale, shift, w_router_lanes)


def _rows_at(hbm, first_row, n_rows):
    return hbm.at[pl.ds(pl.multiple_of(first_row, SEGMENT_ALIGN), n_rows)]


def _dispatch_kernel(base_ref, cnt_ref, padlo_ref, padn_ref, u_ref, meta_ref, xs_hbm,
                     stage, extra, zeros_buf, sems, extra_sem, *, n_exp):
    t = pl.program_id(0)
    last = pl.num_programs(0) - 1
    tm = u_ref.shape[0]
    R, H = DISPATCH_ROWS, DISPATCH_HALF
    slot = t % 2
    u = u_ref[...]
    e0, e1 = meta_ref[0:1, :], meta_ref[1:2, :]
    d0, d1 = meta_ref[2:3, :], meta_ref[3:4, :]
    srow = lax.broadcasted_iota(I32, (R, tm), 0)

    def main_copies(step, buf, fn):
        for e in range(n_exp):
            base = base_ref[step * n_exp + e]
            fn(pltpu.make_async_copy(stage.at[buf, pl.ds(e * R, H)], _rows_at(xs_hbm, base, H),
                                     sems.at[buf]))

            @pl.when(cnt_ref[step * n_exp + e] > H)
            def _(e=e, base=base):
                fn(pltpu.make_async_copy(stage.at[buf, pl.ds(e * R + H, H)],
                                         _rows_at(xs_hbm, base + H, H), sems.at[buf]))

    rels = []
    for e in range(n_exp):
        rel = jnp.where(e0 == e, d0, jnp.where(e1 == e, d1, -1))
        rels.append(jnp.where(rel >= 0, rel - base_ref[t * n_exp + e], -1))
    onehot = jnp.concatenate([jnp.where(srow == rel, 1.0, 0.0) for rel in rels], axis=0).astype(BF16)
    stage[slot] = jnp.dot(onehot, u, preferred_element_type=F32).astype(BF16)

    @pl.when(t > 0)
    def _():
        main_copies(t - 1, 1 - slot, lambda cp: cp.wait())
    main_copies(t, slot, lambda cp: cp.start())

    for e in range(n_exp):
        for blk in range(1, -(-tm // R)):
            @pl.when(cnt_ref[t * n_exp + e] > blk * R)
            def _(e=e, blk=blk):
                more = jnp.where(srow + blk * R == rels[e], 1.0, 0.0).astype(BF16)
                extra[...] = jnp.dot(more, u, preferred_element_type=F32).astype(BF16)
                cp = pltpu.make_async_copy(extra, _rows_at(xs_hbm, base_ref[t * n_exp + e] + blk * R, R),
                                           extra_sem)
                cp.start()
                cp.wait()

    @pl.when(t == last)
    def _():
        main_copies(t, slot, lambda cp: cp.wait())
        zeros_buf[...] = jnp.zeros_like(zeros_buf)
        pieces = [SEGMENT_ALIGN << b for b in range(((R - 1) // SEGMENT_ALIGN).bit_length())]

        def pad_copies(fn):
            for p in range(padlo_ref.shape[0]):
                lo, n = padlo_ref[p], padn_ref[p]
                whole = n // R

                def body(k, carry, lo=lo):
                    fn(pltpu.make_async_copy(zeros_buf, _rows_at(xs_hbm, lo + k * R, R), extra_sem))
                    return carry
                lax.fori_loop(0, whole, body, 0)
                rem = n - whole * R
                for sz in pieces:
                    off = lo + whole * R + (rem // (2 * sz)) * (2 * sz)
                    pl.when((rem // sz) % 2 == 1)(functools.partial(
                        lambda off, sz: fn(pltpu.make_async_copy(
                            zeros_buf.at[pl.ds(0, sz)], _rows_at(xs_hbm, off, sz), extra_sem)), off, sz))

        pad_copies(lambda cp: cp.start())
        pad_copies(lambda cp: cp.wait())


def _dispatch(u, meta_t, base_te, cnt_te, pad_lo, pad_n, *, n_rows, tm):
    T, D = u.shape
    n_exp = pad_lo.shape[0] - 1
    grid_spec = pltpu.PrefetchScalarGridSpec(
        num_scalar_prefetch=4,
        grid=(T // tm,),
        in_specs=[pl.BlockSpec((tm, D), lambda t, *_: (t, 0)),
                  pl.BlockSpec((meta_t.shape[0], tm), lambda t, *_: (0, t))],
        out_specs=pl.BlockSpec(memory_space=pl.ANY),
        scratch_shapes=[pltpu.VMEM((2, n_exp * DISPATCH_ROWS, D), BF16),
                        pltpu.VMEM((DISPATCH_ROWS, D), BF16),
                        pltpu.VMEM((DISPATCH_ROWS, D), BF16),
                        pltpu.SemaphoreType.DMA((2,)),
                        pltpu.SemaphoreType.DMA(())],
    )
    return pl.pallas_call(
        functools.partial(_dispatch_kernel, n_exp=n_exp),
        grid_spec=grid_spec,
        out_shape=jax.ShapeDtypeStruct((n_rows, D), BF16),
        compiler_params=_params(("arbitrary",)),
        name="moe_dispatch",
    )(base_te, cnt_te, pad_lo, pad_n, u, meta_t)


TILE_FULL, TILE_HALF, TILE_EMPTY = 0, 1, 2


def _expert_kernel(te_ref, na_ref, fill_ref, xs_ref, wg_ref, wu_ref, wd_ref, o_ref, acc_ref):
    i = pl.program_id(0)
    j = pl.program_id(1)
    last = pl.num_programs(1) - 1
    active = i < na_ref[0]
    fill = fill_ref[i]
    tm = acc_ref.shape[0]

    @pl.when(active & (j == 0))
    def _():
        acc_ref[...] = jnp.zeros_like(acc_ref)

    def swiglu(rows):
        xb = xs_ref[rows, :]
        g = jnp.dot(xb, wg_ref[...].astype(BF16), preferred_element_type=F32)
        up = jnp.dot(xb, wu_ref[...].astype(BF16), preferred_element_type=F32)
        h = (g * _sigmoid(g) * up).astype(BF16)
        acc_ref[rows, :] += jnp.dot(h, wd_ref[...].astype(BF16), preferred_element_type=F32)

    pl.when(active & (fill == TILE_FULL))(functools.partial(swiglu, pl.ds(0, tm)))
    pl.when(active & (fill == TILE_HALF))(functools.partial(swiglu, pl.ds(0, tm // 2)))

    @pl.when(active & (j == last))
    def _():
        o_ref[...] = acc_ref[...].astype(BF16)

    @pl.when(jnp.logical_not(active) & (j == last))
    def _():
        o_ref[...] = jnp.zeros_like(o_ref)


def _expert_ffn(xs, tile_expert, n_active, tile_fill, wg, wu, wd, *, tm, tf):
    n_rows, D = xs.shape
    ff = wg.shape[2]
    assert n_rows % tm == 0 and ff % tf == 0
    nj = ff // tf

    def jj(i, j, na):
        return jnp.where(i < na[0], j, nj - 1)

    def ii(i, na):
        return jnp.minimum(i, na[0] - 1)

    grid_spec = pltpu.PrefetchScalarGridSpec(
        num_scalar_prefetch=3,
        grid=(n_rows // tm, nj),
        in_specs=[pl.BlockSpec((tm, D), lambda i, j, te, na, tf_: (ii(i, na), 0)),
                  pl.BlockSpec((None, D, tf), lambda i, j, te, na, tf_: (te[i], 0, jj(i, j, na))),
                  pl.BlockSpec((None, D, tf), lambda i, j, te, na, tf_: (te[i], 0, jj(i, j, na))),
                  pl.BlockSpec((None, tf, D), lambda i, j, te, na, tf_: (te[i], jj(i, j, na), 0))],
        out_specs=pl.BlockSpec((tm, D), lambda i, j, te, na, tf_: (i, 0)),
        scratch_shapes=[pltpu.VMEM((tm, D), F32)],
    )
    return pl.pallas_call(
        _expert_kernel,
        grid_spec=grid_spec,
        out_shape=jax.ShapeDtypeStruct((n_rows, D), BF16),
        compiler_params=_params(("arbitrary", "arbitrary")),
        name="moe_experts",
    )(tile_expert, n_active, tile_fill, xs, wg, wu, wd)


def _combine_kernel(base_ref, off_ref, cnt_ref, ys_hbm, meta_ref, x_ref, gate_ref, lg_ref, lb_ref, o_ref,
                    stage, extra, acc_ref, sems, extra_sem, *, alpha, n_exp):
    t = pl.program_id(0) * pl.num_programs(1) + pl.program_id(1)
    n_steps = pl.num_programs(0) * pl.num_programs(1)
    tm = x_ref.shape[0]
    R = DISPATCH_ROWS
    slot = t % 2

    def main_copies(step, buf, fn):
        for e in range(n_exp):
            fn(pltpu.make_async_copy(_rows_at(ys_hbm, base_ref[step * n_exp + e], R),
                                     stage.at[buf, pl.ds(e * R, R)], sems.at[buf]))

    @pl.when(t == 0)
    def _():
        main_copies(0, 0, lambda cp: cp.start())

    @pl.when(t + 1 < n_steps)
    def _():
        main_copies(t + 1, 1 - slot, lambda cp: cp.start())

    meta = meta_ref[...]
    col = lambda k: meta[:, k:k + 1]
    e0, e1 = col(META_E0).astype(I32), col(META_E1).astype(I32)
    r0, r1 = col(META_R0).astype(I32), col(META_R1).astype(I32)
    w0, w1 = col(META_W0), col(META_W1)
    lane = lax.broadcasted_iota(I32, (tm, R), 1)

    rels, ws, picks = [], [], []
    for e in range(n_exp):
        rel = jnp.where(e0 == e, r0, jnp.where(e1 == e, r1, -1))
        rels.append(jnp.where(rel >= 0, rel - off_ref[t * n_exp + e], -1))
        ws.append(jnp.where(e0 == e, w0, jnp.where(e1 == e, w1, 0.0)))
        picks.append(jnp.where(lane == rels[e], ws[e], 0.0))
    weighted = jnp.concatenate(picks, axis=1).astype(BF16)

    acc_ref[...] = jnp.zeros_like(acc_ref)
    for e in range(n_exp):
        for blk in range(1, -(-tm // R)):
            @pl.when(cnt_ref[t * n_exp + e] > blk * R)
            def _(e=e, blk=blk):
                cp = pltpu.make_async_copy(_rows_at(ys_hbm, base_ref[t * n_exp + e] + blk * R, R),
                                           extra, extra_sem)
                cp.start()
                cp.wait()
                more = jnp.where(lane + blk * R == rels[e], ws[e], 0.0).astype(BF16)
                acc_ref[...] += jnp.dot(more, extra[...], preferred_element_type=F32)

    main_copies(t, slot, lambda cp: cp.wait())
    y = acc_ref[...] + jnp.dot(weighted, stage[slot], preferred_element_type=F32)
    z = alpha * x_ref[...] + (1.0 + gate_ref[...]) * y
    o_ref[...] = _layer_norm(z, lg_ref[...], lb_ref[...])


def _combine(ys, base_te, off_te, cnt_te, meta, x, gate, ln_g, ln_b, *, alpha, tm):
    B, S, D = x.shape
    nt = S // tm
    n_exp = base_te.shape[0] // (B * nt)
    row = pl.BlockSpec((None, tm, D), lambda b, i, *_: (b, i, 0))
    vec = pl.BlockSpec((1, D), lambda b, i, *_: (0, 0))
    grid_spec = pltpu.PrefetchScalarGridSpec(
        num_scalar_prefetch=3,
        grid=(B, nt),
        in_specs=[pl.BlockSpec(memory_space=pl.ANY),
                  pl.BlockSpec((None, tm, LANES), lambda b, i, *_: (b, i, 0)),
                  row, pl.BlockSpec((None, 1, D), lambda b, i, *_: (b, 0, 0)), vec, vec],
        out_specs=row,
        scratch_shapes=[pltpu.VMEM((2, n_exp * DISPATCH_ROWS, D), BF16),
                        pltpu.VMEM((DISPATCH_ROWS, D), BF16),
                        pltpu.VMEM((tm, D), F32),
                        pltpu.SemaphoreType.DMA((2,)),
                        pltpu.SemaphoreType.DMA(())],
    )
    return pl.pallas_call(
        functools.partial(_combine_kernel, alpha=alpha, n_exp=n_exp),
        grid_spec=grid_spec,
        out_shape=jax.ShapeDtypeStruct((B, S, D), F32),
        compiler_params=_params(("arbitrary", "arbitrary")),
        name="moe_combine",
    )(base_te, off_te, cnt_te, ys, meta, x, gate, ln_g, ln_b)


def _moe_ffn(x, scale, shift, gate, w_router, wg, wu, wd, ln_g, ln_b, *, alpha, tm, tm_e, tf):
    B, S, D = x.shape
    T = B * S
    nt = T // tm
    n_exp = w_router.shape[1]
    wr = jnp.zeros((D, LANES), F32).at[:, :n_exp].set(w_router)
    wr_hi = wr.astype(BF16)
    wr = jnp.stack([wr_hi, (wr - wr_hi.astype(F32)).astype(BF16)])
    u, meta = _router(x, scale, shift, wr, n_exp=n_exp, tm=tm)

    meta2 = meta.reshape(T, LANES)
    e0 = meta2[:, META_E0].astype(I32)
    e1 = meta2[:, META_E1].astype(I32)
    r0 = meta2[:, META_R0].astype(I32)
    r1 = meta2[:, META_R1].astype(I32)
    experts = jnp.arange(n_exp, dtype=I32)
    chosen = ((e0[:, None] == experts) | (e1[:, None] == experts)).astype(I32)
    cnt_te = chosen.reshape(nt, tm, n_exp).sum(axis=1)
    seg_te = -(-cnt_te // SEGMENT_ALIGN) * SEGMENT_ALIGN
    off_te = jnp.cumsum(cnt_te, axis=0) - cnt_te
    filled = seg_te.sum(axis=0)
    sizes = -(-(filled + DISPATCH_SLACK) // tm_e) * tm_e
    ends = jnp.cumsum(sizes)
    starts = ends - sizes
    base_te = starts[None, :] + jnp.cumsum(seg_te, axis=0) - seg_te
    shift_te = (base_te - off_te)[:, None, :]

    def rows_of(e, r):
        pick = (e[:, None] == experts).astype(I32).reshape(nt, tm, n_exp)
        return r + (pick * shift_te).sum(axis=-1).reshape(T)

    d0, d1 = rows_of(e0, r0), rows_of(e1, r1)
    bound = TOP_K * T + n_exp * ((SEGMENT_ALIGN - 1) * nt + DISPATCH_SLACK)
    n_tiles = -(-bound // tm_e) + n_exp + 1
    n_active = (ends[-1] // tm_e).astype(I32)
    tile_id = jnp.minimum(jnp.arange(n_tiles, dtype=I32), n_active - 1)
    tile_expert = jnp.sum((ends[None, :] <= (tile_id * tm_e)[:, None]).astype(I32), axis=1)
    data_end = jnp.sum((tile_expert[:, None] == experts) * (starts + filled)[None, :], axis=1)
    left = data_end - tile_id * tm_e
    tile_fill = jnp.where(left <= 0, TILE_EMPTY, jnp.where(left <= tm_e // 2, TILE_HALF, TILE_FULL)).astype(I32)
    meta_t = jnp.stack([e0, e1, d0, d1] + [jnp.zeros_like(e0)] * 4)

    xs = _dispatch(u.reshape(T, D), meta_t, base_te.reshape(-1), cnt_te.reshape(-1),
                   jnp.concatenate([starts + filled, ends[-1:]]),
                   jnp.concatenate([sizes - filled, n_tiles * tm_e - ends[-1:]]),
                   n_rows=n_tiles * tm_e, tm=tm)
    ys = _expert_ffn(xs, tile_expert, n_active.reshape(1), tile_fill, wg, wu, wd, tm=tm_e, tf=tf)
    return _combine(ys, base_te.reshape(-1), off_te.reshape(-1), cnt_te.reshape(-1),
                    meta, x, gate, ln_g, ln_b, alpha=alpha, tm=tm)


class _Tiles(NamedTuple):
    token_rows: int
    rope_rows: int
    ffn_chunk: int
    expert_rows: int
    expert_ff: int


def _tile_plan(batch, seq):
    return _Tiles(token_rows=min(512, seq), rope_rows=min(1024, seq), ffn_chunk=2 * LANES,
                  expert_rows=min(1024, batch * seq), expert_ff=512)


def kernel(x, c, positions, w_in, w_out, attn_norm_gain, rec_norm_gain, rec_lb_logits, ada_w, ada_b,
           ln_gain, ln_bias, ffn_w_gate, ffn_w_up, ffn_w_down, moe_router, moe_w_gate, moe_w_up,
           moe_w_down):
    B, S, D = x.shape
    depth = w_in.shape[0]
    alpha = (2 * depth) ** 0.25

    p = jax.nn.softmax(rec_lb_logits.astype(F32), axis=0)
    cum = jnp.cumsum(p, axis=0)
    lb_all = cum - cum[0:1]
    half = ATTN_HEAD_DIM // 2
    inv_freq = ROPE_THETA ** (-jnp.arange(half, dtype=F32) / half)
    inv_freq_lanes = jnp.tile(inv_freq, LANES // half).reshape(1, LANES)
    pos_f = positions.astype(F32).reshape(B, S, 1)
    n_qk = 2 * ATTN_WIDTH
    w_in_b = jnp.concatenate([_qk_lane_order(w_in[..., :n_qk]), w_in[..., n_qk:]], axis=-1).astype(BF16)
    w_out_b = w_out.astype(BF16)
    ff = ffn_w_gate.shape[2]
    ffp = -(-ff // (2 * LANES)) * (2 * LANES)
    pad_c = lambda w: jnp.pad(w.astype(BF16), ((0, 0), (0, 0), (0, ffp - ff)))
    ffn_g, ffn_u = pad_c(ffn_w_gate), pad_c(ffn_w_up)
    ffn_d = jnp.pad(ffn_w_down.astype(BF16), ((0, 0), (0, ffp - ff), (0, 0)))

    mods = _ada_modulation(c, ada_w, ada_b)
    tiles = _tile_plan(B, S)
    rope_cos, rope_sin = _rope_tables(pos_f, inv_freq_lanes, tm=tiles.rope_rows)

    def mod(layer, sub):
        m = mods[layer * 2 + sub].reshape(3, B, 1, D)
        return m[0], m[1], m[2]

    vec = lambda a: a.reshape(1, -1)
    for layer in range(depth):
        shift, scale, gate = mod(layer, 0)
        proj = _in_projection(x, scale, shift, rope_cos, rope_sin, w_in_b, layer, tm=tiles.token_rows)
        attn = _dilated_attention(proj)
        rec = _hgrn2(proj, lb_all[layer], vec(rec_norm_gain[layer]))
        x = _out_projection(attn, rec, x, gate, vec(attn_norm_gain[layer]), w_out_b, layer,
                            vec(ln_gain[layer, 0]), vec(ln_bias[layer, 0]), alpha=alpha,
                            tm=tiles.token_rows)
        shift, scale, gate = mod(layer, 1)
        j = layer // 2
        if layer % 2 == 0:
            x = _dense_ffn(x, scale, shift, gate, ffn_g[j], ffn_u[j], ffn_d[j],
                           vec(ln_gain[layer, 1]), vec(ln_bias[layer, 1]),
                           alpha=alpha, tm=tiles.token_rows, tc=tiles.ffn_chunk)
        else:
            x = _moe_ffn(x, scale, shift, gate, moe_router[j], moe_w_gate[j], moe_w_up[j],
                         moe_w_down[j], vec(ln_gain[layer, 1]), vec(ln_bias[layer, 1]),
                         alpha=alpha, tm=tiles.token_rows, tm_e=tiles.expert_rows, tf=tiles.expert_ff)
    return x
```

```python
import functools
from typing import NamedTuple

import jax
import jax.numpy as jnp
from jax import lax
from jax.experimental import pallas as pl
from jax.experimental.pallas import tpu as pltpu

F32 = jnp.float32
BF16 = jnp.bfloat16
I32 = jnp.int32

LANES = 128
V7X_VMEM_LIMIT_BYTES = 56 * 1024 * 1024

ATTN_WIDTH = 512
ATTN_HEAD_DIM = 64
DILATIONS = (1, 4, 16)
ATTN_RADIUS = 64
ROPE_THETA = 10000.0
REC_WIDTH = 512
REC_DIM = 128
REC_CHUNK = 64
TOP_K = 2
LN_EPS = 1e-5
RMS_EPS = 1e-6
MASK_VALUE = -1e30
DECAY_EXP_CLAMP = 80.0
LOG2_E = 1.4426950408889634


def _params(semantics):
    return pltpu.CompilerParams(dimension_semantics=semantics,
                                vmem_limit_bytes=V7X_VMEM_LIMIT_BYTES)


def _sigmoid(x):
    return 1.0 / (1.0 + jnp.exp(-x))


def _layer_norm(z, gain, bias):
    mu = jnp.mean(z, axis=-1, keepdims=True)
    zc = z - mu
    var = jnp.mean(zc * zc, axis=-1, keepdims=True)
    return zc * lax.rsqrt(var + LN_EPS) * gain + bias


def _ada_kernel(c_ref, w_ref, b_ref, o_ref):
    c = c_ref[...]
    o_ref[...] = jnp.dot(c * _sigmoid(c), w_ref[...], precision=lax.Precision.HIGHEST,
                         preferred_element_type=F32) + b_ref[...]


def _ada_modulation(c, ada_w, ada_b):
    B, D = c.shape
    n = ada_w.shape[0] * ada_w.shape[1]
    n3 = ada_w.shape[-1]
    tn = D
    return pl.pallas_call(
        _ada_kernel,
        grid=(n, n3 // tn),
        in_specs=[pl.BlockSpec((B, D), lambda i, j: (0, 0)),
                  pl.BlockSpec((None, D, tn), lambda i, j: (i, 0, j)),
                  pl.BlockSpec((None, 1, tn), lambda i, j: (i, 0, j))],
        out_specs=pl.BlockSpec((None, None, B, tn), lambda i, j: (i, j, 0, 0)),
        out_shape=jax.ShapeDtypeStruct((n, n3 // tn, B, tn), F32),
        compiler_params=_params(("arbitrary", "arbitrary")),
        name="ada_modulation",
    )(c, ada_w.reshape(n, D, n3), ada_b.reshape(n, 1, n3))


ROPE_HALF = ATTN_HEAD_DIM // 2


def _qk_lane_order(w_cols):
    lead = w_cols.shape[:-1]
    w = w_cols.reshape(*lead, -1, 2, 2, ROPE_HALF)
    return jnp.swapaxes(w, -3, -2).reshape(*lead, -1)


def _rope_kernel(pos_ref, invf_ref, cos_ref, sin_ref):
    ang = pos_ref[...] * invf_ref[...]
    lane = lax.broadcasted_iota(I32, (1, LANES), 1)
    sin = jnp.sin(ang)
    cos_ref[...] = jnp.cos(ang)
    sin_ref[...] = jnp.where(lane < LANES // 2, -sin, sin)


def _rope_tables(pos_f, inv_freq_lanes, *, tm):
    B, S, _ = pos_f.shape
    tab = pl.BlockSpec((None, tm, LANES), lambda b, i: (b, i, 0))
    return pl.pallas_call(
        _rope_kernel,
        grid=(B, S // tm),
        in_specs=[pl.BlockSpec((None, tm, 1), lambda b, i: (b, i, 0)),
                  pl.BlockSpec((1, LANES), lambda b, i: (0, 0))],
        out_specs=[tab, tab],
        out_shape=[jax.ShapeDtypeStruct((B, S, LANES), F32)] * 2,
        compiler_params=_params(("arbitrary", "arbitrary")),
        name="rope_tables",
    )(pos_f, inv_freq_lanes)


def _inproj_kernel(x_ref, sc_ref, sh_ref, cos_ref, sin_ref, w_ref, o_ref, *, tn):
    u = (x_ref[...] * (1.0 + sc_ref[...]) + sh_ref[...]).astype(BF16)
    cos = cos_ref[...]
    sin_signed = sin_ref[...]
    q_scale = ATTN_HEAD_DIM ** -0.5
    for j in range(w_ref.shape[1] // tn):
        acc = jnp.dot(u, w_ref[:, j * tn:(j + 1) * tn], preferred_element_type=F32)
        if j > 0:
            o_ref[:, j * tn:(j + 1) * tn] = acc
            continue
        for kk in range(tn // LANES):
            c = acc[:, kk * LANES:(kk + 1) * LANES]
            r = c * cos + pltpu.roll(c, LANES // 2, 1) * sin_signed
            if kk * LANES < ATTN_WIDTH:
                r = r * q_scale
            o_ref[:, kk * LANES:(kk + 1) * LANES] = r


def _in_projection(x, scale, shift, rope_cos, rope_sin, w_in_bf16, layer, *, tm):
    B, S, D = x.shape
    n_cols = w_in_bf16.shape[2]
    tn = 2 * ATTN_WIDTH
    assert S % tm == 0 and n_cols % tn == 0
    return pl.pallas_call(
        functools.partial(_inproj_kernel, tn=tn),
        grid=(B, S // tm),
        in_specs=[pl.BlockSpec((None, tm, D), lambda b, i: (b, i, 0)),
                  pl.BlockSpec((None, 1, D), lambda b, i: (b, 0, 0)),
                  pl.BlockSpec((None, 1, D), lambda b, i: (b, 0, 0)),
                  pl.BlockSpec((None, tm, LANES), lambda b, i: (b, i, 0)),
                  pl.BlockSpec((None, tm, LANES), lambda b, i: (b, i, 0)),
                  pl.BlockSpec((None, D, n_cols), lambda b, i: (layer, 0, 0))],
        out_specs=pl.BlockSpec((None, tm, n_cols), lambda b, i: (b, i, 0)),
        out_shape=jax.ShapeDtypeStruct((B, S, n_cols), F32),
        compiler_params=_params(("arbitrary", "arbitrary")),
        name="in_projection",
    )(x, scale, shift, rope_cos, rope_sin, w_in_bf16)


def _attn_kernel(q_ref, k_ref, v_ref, o_ref, qs, ks, vs, qf, kf, vf, nm, nl, na, bias_s, *, S):
    lane = lax.broadcasted_iota(I32, (1, LANES), 1)
    head0 = lane < ATTN_HEAD_DIM
    head0_qk = (lane % ATTN_HEAD_DIM) < ROPE_HALF

    for p, d in enumerate(DILATIONS):
        L = S // d
        tq = min(128, L)
        W = min(2 * tq, L)
        nb = L // tq

        d_prev = DILATIONS[p - 1] if p else 1
        ratio, l_prev = d // d_prev, S // d_prev
        keep_f32 = 0 < p < len(DILATIONS) - 1
        for src, stage, dst in ((q_ref, qf, qs), (k_ref, kf, ks), (v_ref, vf, vs)):
            src = src if p <= 1 else stage
            for r in range(d):
                a, r_prev = divmod(r, d_prev)
                rows = pl.ds(r_prev * l_prev + a, L, stride=ratio) if ratio > 1 else pl.ds(0, L)
                val = src[rows, :]
                if keep_f32:
                    stage[r * L:(r + 1) * L, :] = val
                dst[r * L:(r + 1) * L, :] = val.astype(BF16)

        n_blocks = d * nb
        rc = lax.broadcasted_iota(I32, (tq, W), 0) - lax.broadcasted_iota(I32, (tq, W), 1)
        for which, delta in enumerate((0, ATTN_RADIUS, tq)):
            bias_s[which, pl.ds(0, tq), pl.ds(0, W)] = jnp.where(
                jnp.abs(rc + delta) <= ATTN_RADIUS, 0.0, MASK_VALUE)
        spans, scores, probs, sums = {}, {}, {}, {}

        def score_stage(g, L=L, tq=tq, W=W, nb=nb):
            r, n = divmod(g, nb)
            q0 = r * L + n * tq
            ws = min(max(n * tq - ATTN_RADIUS, 0), L - W)
            qb = qs[q0:q0 + tq, :]
            zero = jnp.zeros_like(qb)
            q2 = jnp.concatenate([jnp.where(head0_qk, qb, zero), jnp.where(head0_qk, zero, qb)], axis=0)
            s = lax.dot_general(q2, ks[r * L + ws:r * L + ws + W, :], (((1,), (1,)), ((), ())),
                                preferred_element_type=F32)
            which = {0: 0, ATTN_RADIUS: 1}.get(n * tq - ws, 2)
            bias = bias_s[which, 0:tq, 0:W]
            scores[g] = s + jnp.concatenate([bias, bias], axis=0)
            rows = pl.ds(r + d * n * tq, tq, stride=d) if d > 1 else pl.ds(n * tq, tq)
            spans[g] = (rows, slice(r * L + ws, r * L + ws + W))

        def softmax_stage(g):
            s = scores.pop(g)
            m = jnp.max(s, axis=1, keepdims=True)
            e = jnp.exp(s - m)
            probs[g] = (m, e.astype(BF16))
            sums[g] = jnp.sum(e, axis=1, keepdims=True)

        def value_stage(g, tq=tq):
            m, e = probs.pop(g)
            l = sums.pop(g)
            rows, keys = spans.pop(g)
            out = jnp.dot(e, vs[keys, :], preferred_element_type=F32)
            nm[p, rows, :] = jnp.where(head0, m[:tq], m[tq:])
            nl[p, rows, :] = jnp.where(head0, l[:tq], l[tq:])
            na[p, rows, :] = jnp.where(head0, out[:tq], out[tq:])

        for step in range(n_blocks + 2):
            if step < n_blocks:
                score_stage(step)
            if 1 <= step <= n_blocks:
                softmax_stage(step - 1)
            if step >= 2:
                value_stage(step - 2)

    m_all = jnp.maximum(jnp.maximum(nm[0], nm[1]), nm[2])
    num = jnp.zeros((S, LANES), F32)
    den = jnp.zeros((S, LANES), F32)
    for p in range(len(DILATIONS)):
        w = jnp.exp(nm[p] - m_all)
        num = num + w * na[p]
        den = den + w * nl[p]
    o_ref[...] = (num / den).astype(o_ref.dtype)


def _dilated_attention(proj):
    B, S, _ = proj.shape
    n_pairs = ATTN_WIDTH // LANES
    assert S % (16 * DILATIONS[-1]) == 0
    blk = lambda off: pl.BlockSpec((None, S, LANES), lambda b, h: (b, 0, off + h))
    return pl.pallas_call(
        functools.partial(_attn_kernel, S=S),
        grid=(B, n_pairs),
        in_specs=[blk(0), blk(n_pairs), blk(2 * n_pairs)],
        out_specs=pl.BlockSpec((None, S, LANES), lambda b, h: (b, 0, h)),
        out_shape=jax.ShapeDtypeStruct((B, S, ATTN_WIDTH), BF16),
        scratch_shapes=[pltpu.VMEM((S, LANES), BF16)] * 3
                       + [pltpu.VMEM((S, LANES), F32)] * 3
                       + [pltpu.VMEM((len(DILATIONS), S, LANES), F32)] * 3
                       + [pltpu.VMEM((3, min(128, S), min(256, S)), F32)],
        compiler_params=_params(("arbitrary", "arbitrary")),
        name="dilated_attention",
    )(proj, proj, proj)


REC_GROUP_CHUNKS = 4
REC_GROUP_ROWS = REC_GROUP_CHUNKS * REC_CHUNK
REC_STAGE_GROUPS = 8


def _hgrn_kernel(rq_ref, zf_ref, zb_ref, ri_ref, rg_ref, lb_ref, gain_ref, o_ref,
                 q_s, vt_s, oi_s, qp_s, ut_s, dd_s, st_s, os_s, *, S):
    C, G, GR = REC_CHUNK, REC_GROUP_CHUNKS, REC_GROUP_ROWS
    nc, ng = S // C, S // GR
    sg = min(REC_STAGE_GROUPS, ng)
    assert ng % sg == 0

    row = lax.broadcasted_iota(I32, (GR, GR), 0)
    col = lax.broadcasted_iota(I32, (GR, GR), 1)
    same_chunk = (row // C) == (col // C)
    t_i = lax.broadcasted_iota(I32, (C, C), 0)
    s_i = lax.broadcasted_iota(I32, (C, C), 1)

    def prep(i, carry):
        r0 = pl.multiple_of(i * GR, GR)
        rq = rq_ref[pl.ds(r0, GR), :]
        q_s[pl.ds(r0, GR), :] = rq * _sigmoid(rq)
        v = ri_ref[pl.ds(r0, GR), :]
        for c in range(G):
            vt_s[i * G + c] = v[c * C:(c + 1) * C, :].T.astype(BF16)
        return carry

    lax.fori_loop(0, ng, prep, 0, unroll=True)

    for direction in range(2):
        fwd = direction == 0
        z_ref = zf_ref if fwd else zb_ref
        cum = jnp.where(same_chunk & ((col <= row) if fwd else (col >= row)), 1.0, 0.0).astype(BF16)
        keep = (s_i <= t_i) if fwd else (s_i >= t_i)
        mid_row = C // 2 - 1 if fwd else C // 2
        last_row = C - 1 if fwd else 0

        def phase_a(it, carry, fwd=fwd, z_ref=z_ref, cum=cum, keep=keep,
                    mid_row=mid_row, last_row=last_row, direction=direction):
            lb = lb_ref[direction:direction + 1, :]
            log_lb = jnp.log(lb)
            log_1m_lb = jnp.log1p(-lb)
            r0s, c0s, k3s, bcs, atts, stage3 = [], [], [], [], [], []
            for u in range(sg):
                i = it * sg + u
                r0 = pl.multiple_of(i * GR, GR)
                z = z_ref[pl.ds(r0, GR), :]
                e = jnp.exp(-jnp.abs(z))
                t = log_1m_lb + (jnp.minimum(z, 0.0) - jnp.log(1.0 + e))
                g = jnp.maximum(log_lb, t) + jnp.log(1.0 + jnp.exp(-jnp.abs(log_lb - t)))
                kk = (1.0 - lb) * (jnp.where(z > 0, e, 1.0) / (1.0 + e))
                g = g * LOG2_E
                g1 = g.astype(BF16)
                r1 = g - g1.astype(F32)
                g2 = r1.astype(BF16)
                g3 = (r1 - g2.astype(F32)).astype(BF16)
                bcs.append(jnp.dot(cum, g1, preferred_element_type=F32)
                           + jnp.dot(cum, g2, preferred_element_type=F32)
                           + jnp.dot(cum, g3, preferred_element_type=F32))
                k3s.append(kk.reshape(G, C, REC_DIM))
                r0s.append(r0)
                c0s.append(pl.multiple_of(i * G, G))
            for u in range(sg):
                b3 = bcs[u].reshape(G, C, REC_DIM)
                mid = b3[:, mid_row:mid_row + 1, :]
                last = b3[:, last_row:last_row + 1, :]
                e3 = b3 - mid
                q3 = q_s[pl.ds(r0s[u], GR), :].reshape(G, C, REC_DIM)
                clamp = DECAY_EXP_CLAMP * LOG2_E
                qt = (q3 * jnp.exp2(jnp.minimum(e3, clamp))).astype(BF16)
                kt = (k3s[u] * jnp.exp2(jnp.minimum(-e3, clamp))).astype(BF16)
                atts.append(jnp.einsum('gtc,gsc->gts', qt, kt, preferred_element_type=F32))
                qp_s[pl.ds(r0s[u], GR), :] = (q3 * jnp.exp2(b3)).astype(BF16).reshape(GR, REC_DIM)
                stage3.append(((k3s[u] * jnp.exp2(last - b3)).astype(BF16), jnp.exp2(last)))
            for u in range(sg):
                att = jnp.where(keep[None], atts[u], 0.0).astype(BF16)
                v3 = ri_ref[pl.ds(r0s[u], GR), :].reshape(G, C, REC_DIM).astype(BF16)
                oi = jnp.einsum('gts,gsv->gtv', att, v3, preferred_element_type=F32)
                oi_s[pl.ds(r0s[u], GR), :] = oi.reshape(GR, REC_DIM)
                kp, decay = stage3[u]
                ut_s[pl.ds(c0s[u], G)] = jnp.einsum('gvs,gsc->gvc', vt_s[pl.ds(c0s[u], G)], kp,
                                                    preferred_element_type=F32)
                dd_s[pl.ds(c0s[u], G)] = decay
            return carry

        lax.fori_loop(0, ng // sg, phase_a, 0)

        def scan(c, st, fwd=fwd):
            idx = c if fwd else nc - 1 - c
            st_s[idx] = st.astype(BF16)
            return dd_s[idx] * st + ut_s[idx]

        lax.fori_loop(0, nc, scan, jnp.zeros((REC_DIM, REC_DIM), F32), unroll=True)

        def phase_c(it, carry, fwd=fwd):
            r0s = [pl.multiple_of((it * sg + u) * GR, GR) for u in range(sg)]
            oos = []
            for u in range(sg):
                c0 = pl.multiple_of((it * sg + u) * G, G)
                qp3 = qp_s[pl.ds(r0s[u], GR), :].reshape(G, C, REC_DIM)
                oos.append(jnp.einsum('gtc,gvc->gtv', qp3, st_s[pl.ds(c0, G)],
                                      preferred_element_type=F32))
            for u in range(sg):
                rows = pl.ds(r0s[u], GR)
                tot = oi_s[rows, :] + oos[u].reshape(GR, REC_DIM)
                os_s[rows, :] = tot if fwd else os_s[rows, :] + tot
            return carry

        lax.fori_loop(0, ng // sg, phase_c, 0)

    o = os_s[...]
    o = o * lax.rsqrt(jnp.mean(o * o, axis=-1, keepdims=True) + RMS_EPS) * gain_ref[...]
    o_ref[...] = (o * _sigmoid(rg_ref[...])).astype(o_ref.dtype)


def _hgrn2(proj, lb, rec_gain):
    B, S, _ = proj.shape
    nh = REC_WIDTH // REC_DIM
    assert S % REC_GROUP_ROWS == 0
    col0 = 3 * ATTN_WIDTH // LANES
    blk = lambda k: pl.BlockSpec((None, S, REC_DIM), lambda b, h: (b, 0, col0 + k * nh + h))
    nc = S // REC_CHUNK
    return pl.pallas_call(
        functools.partial(_hgrn_kernel, S=S),
        grid=(B, nh),
        in_specs=[blk(0), blk(1), blk(2), blk(3), blk(4),
                  pl.BlockSpec((2, REC_DIM), lambda b, h: (0, h)),
                  pl.BlockSpec((1, REC_DIM), lambda b, h: (0, h))],
        out_specs=pl.BlockSpec((None, S, REC_DIM), lambda b, h: (b, 0, h)),
        out_shape=jax.ShapeDtypeStruct((B, S, REC_WIDTH), BF16),
        scratch_shapes=[pltpu.VMEM((S, REC_DIM), F32),
                        pltpu.VMEM((nc, REC_DIM, REC_CHUNK), BF16),
                        pltpu.VMEM((S, REC_DIM), F32),
                        pltpu.VMEM((S, REC_DIM), BF16),
                        pltpu.VMEM((nc, REC_DIM, REC_DIM), F32),
                        pltpu.VMEM((nc, 1, REC_DIM), F32),
                        pltpu.VMEM((nc, REC_DIM, REC_DIM), BF16),
                        pltpu.VMEM((S, REC_DIM), F32)],
        compiler_params=_params(("arbitrary", "arbitrary")),
        name="hgrn2",
    )(proj, proj, proj, proj, proj, lb, rec_gain)


def _outproj_kernel(a_ref, r_ref, x_ref, gate_ref, ag_ref, w_ref, lg_ref, lb_ref, o_ref, *, alpha):
    a = a_ref[...].astype(F32)
    an = a * lax.rsqrt(jnp.mean(a * a, axis=-1, keepdims=True) + RMS_EPS) * ag_ref[...]
    y = (jnp.dot(an.astype(BF16), w_ref[:ATTN_WIDTH, :], preferred_element_type=F32)
         + jnp.dot(r_ref[...], w_ref[ATTN_WIDTH:, :], preferred_element_type=F32))
    z = alpha * x_ref[...] + (1.0 + gate_ref[...]) * y
    o_ref[...] = _layer_norm(z, lg_ref[...], lb_ref[...])


def _out_projection(attn, rec, x, gate, attn_gain, w_out_bf16, layer, ln_g, ln_b, *, alpha, tm):
    B, S, D = x.shape
    row = lambda w: pl.BlockSpec((None, tm, w), lambda b, i: (b, i, 0))
    vec = lambda w: pl.BlockSpec((1, w), lambda b, i: (0, 0))
    return pl.pallas_call(
        functools.partial(_outproj_kernel, alpha=alpha),
        grid=(B, S // tm),
        in_specs=[row(ATTN_WIDTH), row(REC_WIDTH), row(D),
                  pl.BlockSpec((None, 1, D), lambda b, i: (b, 0, 0)),
                  vec(ATTN_WIDTH),
                  pl.BlockSpec((None, ATTN_WIDTH + REC_WIDTH, D), lambda b, i: (layer, 0, 0)),
                  vec(D), vec(D)],
        out_specs=row(D),
        out_shape=jax.ShapeDtypeStruct((B, S, D), F32),
        compiler_params=_params(("arbitrary", "arbitrary")),
        name="out_projection",
    )(attn, rec, x, gate, attn_gain, w_out_bf16, ln_g, ln_b)


def _ffn_kernel(x_ref, sc_ref, sh_ref, gate_ref, wg_ref, wu_ref, wd_ref, lg_ref, lb_ref, o_ref,
                *, alpha, tc):
    x = x_ref[...]
    u = (x * (1.0 + sc_ref[...]) + sh_ref[...]).astype(BF16)
    acc = jnp.zeros(x.shape, F32)
    for c in range(wg_ref.shape[1] // tc):
        cols = slice(c * tc, (c + 1) * tc)
        g = jnp.dot(u, wg_ref[:, cols], preferred_element_type=F32)
        up = jnp.dot(u, wu_ref[:, cols], preferred_element_type=F32)
        h = (g * _sigmoid(g) * up).astype(BF16)
        acc = acc + jnp.dot(h, wd_ref[cols, :], preferred_element_type=F32)
    z = alpha * x + (1.0 + gate_ref[...]) * acc
    o_ref[...] = _layer_norm(z, lg_ref[...], lb_ref[...])


def _dense_ffn(x, scale, shift, gate, wg, wu, wd, ln_g, ln_b, *, alpha, tm, tc):
    B, S, D = x.shape
    ffp = wg.shape[1]
    assert ffp % tc == 0
    row = pl.BlockSpec((None, tm, D), lambda b, i: (b, i, 0))
    mod = pl.BlockSpec((None, 1, D), lambda b, i: (b, 0, 0))
    vec = pl.BlockSpec((1, D), lambda b, i: (0, 0))
    return pl.pallas_call(
        functools.partial(_ffn_kernel, alpha=alpha, tc=tc),
        grid=(B, S // tm),
        in_specs=[row, mod, mod, mod,
                  pl.BlockSpec((D, ffp), lambda b, i: (0, 0)),
                  pl.BlockSpec((D, ffp), lambda b, i: (0, 0)),
                  pl.BlockSpec((ffp, D), lambda b, i: (0, 0)),
                  vec, vec],
        out_specs=row,
        out_shape=jax.ShapeDtypeStruct((B, S, D), F32),
        compiler_params=_params(("arbitrary", "arbitrary")),
        name="dense_ffn",
    )(x, scale, shift, gate, wg, wu, wd, ln_g, ln_b)


META_E0, META_E1, META_W0, META_W1, META_R0, META_R1 = range(6)
SEGMENT_ALIGN = 16
DISPATCH_ROWS = 256
DISPATCH_HALF = DISPATCH_ROWS // 2
DISPATCH_SLACK = DISPATCH_ROWS


def _router_kernel(x_ref, sc_ref, sh_ref, wr_ref, u_ref, meta_ref, carry_ref, *, n_exp):
    tm = x_ref.shape[0]

    @pl.when((pl.program_id(0) == 0) & (pl.program_id(1) == 0))
    def _():
        carry_ref[...] = jnp.zeros_like(carry_ref)

    u = x_ref[...] * (1.0 + sc_ref[...]) + sh_ref[...]
    u_ref[...] = u.astype(BF16)
    u_hi = u.astype(BF16)
    u_lo = (u - u_hi.astype(F32)).astype(BF16)
    logits = (jnp.dot(u_hi, wr_ref[0], preferred_element_type=F32)
              + jnp.dot(u_lo, wr_ref[0], preferred_element_type=F32)
              + jnp.dot(u_hi, wr_ref[1], preferred_element_type=F32))
    lane = lax.broadcasted_iota(I32, (tm, LANES), 1)
    neg = -jnp.inf
    l1 = jnp.where(lane < n_exp, logits, neg)
    m1 = jnp.max(l1, axis=1, keepdims=True)
    i1 = jnp.min(jnp.where(l1 == m1, lane, LANES), axis=1, keepdims=True)
    l2 = jnp.where(lane == i1, neg, l1)
    m2 = jnp.max(l2, axis=1, keepdims=True)
    i2 = jnp.min(jnp.where(l2 == m2, lane, LANES), axis=1, keepdims=True)
    e = jnp.exp(m2 - m1)
    w1 = 1.0 / (1.0 + e)
    w2 = e * w1
    sel = jnp.where((lane == i1) | (lane == i2), 1.0, 0.0)
    before = (lax.broadcasted_iota(I32, (tm, tm), 1) < lax.broadcasted_iota(I32, (tm, tm), 0))
    ranks = jnp.dot(jnp.where(before, 1.0, 0.0).astype(BF16), sel.astype(BF16),
                    preferred_element_type=F32) + carry_ref[...]
    r1 = jnp.sum(jnp.where(lane == i1, ranks, 0.0), axis=1, keepdims=True)
    r2 = jnp.sum(jnp.where(lane == i2, ranks, 0.0), axis=1, keepdims=True)
    carry_ref[...] = carry_ref[...] + jnp.sum(sel, axis=0, keepdims=True)
    meta = jnp.zeros((tm, LANES), F32)
    for k, val in ((META_E0, i1.astype(F32)), (META_E1, i2.astype(F32)), (META_W0, w1),
                   (META_W1, w2), (META_R0, r1), (META_R1, r2)):
        meta = jnp.where(lane == k, val, meta)
    meta_ref[...] = meta


def _router(x, scale, shift, w_router_lanes, *, n_exp, tm):
    B, S, D = x.shape
    nt = S // tm
    row = pl.BlockSpec((None, tm, D), lambda b, i: (b, i, 0))
    mod = pl.BlockSpec((None, 1, D), lambda b, i: (b, 0, 0))
    return pl.pallas_call(
        functools.partial(_router_kernel, n_exp=n_exp),
        grid=(B, nt),
        in_specs=[row, mod, mod, pl.BlockSpec((2, D, LANES), lambda b, i: (0, 0, 0))],
        out_specs=[row, pl.BlockSpec((None, tm, LANES), lambda b, i: (b, i, 0))],
        out_shape=[jax.ShapeDtypeStruct((B, S, D), BF16),
                   jax.ShapeDtypeStruct((B, S, LANES), F32)],
        scratch_shapes=[pltpu.VMEM((1, LANES), F32)],
        compiler_params=_params(("arbitrary", "arbitrary")),
        name="moe_router",
    )(x, scale, shift, w_router_lanes)


def _rows_at(hbm, first_row, n_rows):
    return hbm.at[pl.ds(pl.multiple_of(first_row, SEGMENT_ALIGN), n_rows)]


def _dispatch_kernel(base_ref, cnt_ref, padlo_ref, padn_ref, u_ref, meta_ref, xs_hbm,
                     stage, extra, zeros_buf, sems, extra_sem, *, n_exp):
    t = pl.program_id(0)
    last = pl.num_programs(0) - 1
    tm = u_ref.shape[0]
    R, H = DISPATCH_ROWS, DISPATCH_HALF
    slot = t % 2
    u = u_ref[...]
    e0, e1 = meta_ref[0:1, :], meta_ref[1:2, :]
    d0, d1 = meta_ref[2:3, :], meta_ref[3:4, :]
    srow = lax.broadcasted_iota(I32, (R, tm), 0)

    def main_copies(step, buf, fn):
        for e in range(n_exp):
            base = base_ref[step * n_exp + e]
            fn(pltpu.make_async_copy(stage.at[buf, pl.ds(e * R, H)], _rows_at(xs_hbm, base, H),
                                     sems.at[buf]))

            @pl.when(cnt_ref[step * n_exp + e] > H)
            def _(e=e, base=base):
                fn(pltpu.make_async_copy(stage.at[buf, pl.ds(e * R + H, H)],
                                         _rows_at(xs_hbm, base + H, H), sems.at[buf]))

    rels = []
    for e in range(n_exp):
        rel = jnp.where(e0 == e, d0, jnp.where(e1 == e, d1, -1))
        rels.append(jnp.where(rel >= 0, rel - base_ref[t * n_exp + e], -1))
    onehot = jnp.concatenate([jnp.where(srow == rel, 1.0, 0.0) for rel in rels], axis=0).astype(BF16)
    stage[slot] = jnp.dot(onehot, u, preferred_element_type=F32).astype(BF16)

    @pl.when(t > 0)
    def _():
        main_copies(t - 1, 1 - slot, lambda cp: cp.wait())
    main_copies(t, slot, lambda cp: cp.start())

    for e in range(n_exp):
        for blk in range(1, -(-tm // R)):
            @pl.when(cnt_ref[t * n_exp + e] > blk * R)
            def _(e=e, blk=blk):
                more = jnp.where(srow + blk * R == rels[e], 1.0, 0.0).astype(BF16)
                extra[...] = jnp.dot(more, u, preferred_element_type=F32).astype(BF16)
                cp = pltpu.make_async_copy(extra, _rows_at(xs_hbm, base_ref[t * n_exp + e] + blk * R, R),
                                           extra_sem)
                cp.start()
                cp.wait()

    @pl.when(t == last)
    def _():
        main_copies(t, slot, lambda cp: cp.wait())
        zeros_buf[...] = jnp.zeros_like(zeros_buf)
        pieces = [SEGMENT_ALIGN << b for b in range(((R - 1) // SEGMENT_ALIGN).bit_length())]

        def pad_copies(fn):
            for p in range(padlo_ref.shape[0]):
                lo, n = padlo_ref[p], padn_ref[p]
                whole = n // R

                def body(k, carry, lo=lo):
                    fn(pltpu.make_async_copy(zeros_buf, _rows_at(xs_hbm, lo + k * R, R), extra_sem))
                    return carry
                lax.fori_loop(0, whole, body, 0)
                rem = n - whole * R
                for sz in pieces:
                    off = lo + whole * R + (rem // (2 * sz)) * (2 * sz)
                    pl.when((rem // sz) % 2 == 1)(functools.partial(
                        lambda off, sz: fn(pltpu.make_async_copy(
                            zeros_buf.at[pl.ds(0, sz)], _rows_at(xs_hbm, off, sz), extra_sem)), off, sz))

        pad_copies(lambda cp: cp.start())
        pad_copies(lambda cp: cp.wait())


def _dispatch(u, meta_t, base_te, cnt_te, pad_lo, pad_n, *, n_rows, tm):
    T, D = u.shape
    n_exp = pad_lo.shape[0] - 1
    grid_spec = pltpu.PrefetchScalarGridSpec(
        num_scalar_prefetch=4,
        grid=(T // tm,),
        in_specs=[pl.BlockSpec((tm, D), lambda t, *_: (t, 0)),
                  pl.BlockSpec((meta_t.shape[0], tm), lambda t, *_: (0, t))],
        out_specs=pl.BlockSpec(memory_space=pl.ANY),
        scratch_shapes=[pltpu.VMEM((2, n_exp * DISPATCH_ROWS, D), BF16),
                        pltpu.VMEM((DISPATCH_ROWS, D), BF16),
                        pltpu.VMEM((DISPATCH_ROWS, D), BF16),
                        pltpu.SemaphoreType.DMA((2,)),
                        pltpu.SemaphoreType.DMA(())],
    )
    return pl.pallas_call(
        functools.partial(_dispatch_kernel, n_exp=n_exp),
        grid_spec=grid_spec,
        out_shape=jax.ShapeDtypeStruct((n_rows, D), BF16),
        compiler_params=_params(("arbitrary",)),
        name="moe_dispatch",
    )(base_te, cnt_te, pad_lo, pad_n, u, meta_t)


TILE_FULL, TILE_HALF, TILE_EMPTY = 0, 1, 2


def _expert_kernel(te_ref, na_ref, fill_ref, xs_ref, wg_ref, wu_ref, wd_ref, o_ref, acc_ref):
    i = pl.program_id(0)
    j = pl.program_id(1)
    last = pl.num_programs(1) - 1
    active = i < na_ref[0]
    fill = fill_ref[i]
    tm = acc_ref.shape[0]

    @pl.when(active & (j == 0))
    def _():
        acc_ref[...] = jnp.zeros_like(acc_ref)

    def swiglu(rows):
        xb = xs_ref[rows, :]
        g = jnp.dot(xb, wg_ref[...].astype(BF16), preferred_element_type=F32)
        up = jnp.dot(xb, wu_ref[...].astype(BF16), preferred_element_type=F32)
        h = (g * _sigmoid(g) * up).astype(BF16)
        acc_ref[rows, :] += jnp.dot(h, wd_ref[...].astype(BF16), preferred_element_type=F32)

    pl.when(active & (fill == TILE_FULL))(functools.partial(swiglu, pl.ds(0, tm)))
    pl.when(active & (fill == TILE_HALF))(functools.partial(swiglu, pl.ds(0, tm // 2)))

    @pl.when(active & (j == last))
    def _():
        o_ref[...] = acc_ref[...].astype(BF16)

    @pl.when(jnp.logical_not(active) & (j == last))
    def _():
        o_ref[...] = jnp.zeros_like(o_ref)


def _expert_ffn(xs, tile_expert, n_active, tile_fill, wg, wu, wd, *, tm, tf):
    n_rows, D = xs.shape
    ff = wg.shape[2]
    assert n_rows % tm == 0 and ff % tf == 0
    nj = ff // tf

    def jj(i, j, na):
        return jnp.where(i < na[0], j, nj - 1)

    def ii(i, na):
        return jnp.minimum(i, na[0] - 1)

    grid_spec = pltpu.PrefetchScalarGridSpec(
        num_scalar_prefetch=3,
        grid=(n_rows // tm, nj),
        in_specs=[pl.BlockSpec((tm, D), lambda i, j, te, na, tf_: (ii(i, na), 0)),
                  pl.BlockSpec((None, D, tf), lambda i, j, te, na, tf_: (te[i], 0, jj(i, j, na))),
                  pl.BlockSpec((None, D, tf), lambda i, j, te, na, tf_: (te[i], 0, jj(i, j, na))),
                  pl.BlockSpec((None, tf, D), lambda i, j, te, na, tf_: (te[i], jj(i, j, na), 0))],
        out_specs=pl.BlockSpec((tm, D), lambda i, j, te, na, tf_: (i, 0)),
        scratch_shapes=[pltpu.VMEM((tm, D), F32)],
    )
    return pl.pallas_call(
        _expert_kernel,
        grid_spec=grid_spec,
        out_shape=jax.ShapeDtypeStruct((n_rows, D), BF16),
        compiler_params=_params(("arbitrary", "arbitrary")),
        name="moe_experts",
    )(tile_expert, n_active, tile_fill, xs, wg, wu, wd)


def _combine_kernel(base_ref, off_ref, cnt_ref, ys_hbm, meta_ref, x_ref, gate_ref, lg_ref, lb_ref, o_ref,
                    stage, extra, acc_ref, sems, extra_sem, *, alpha, n_exp):
    t = pl.program_id(0) * pl.num_programs(1) + pl.program_id(1)
    n_steps = pl.num_programs(0) * pl.num_programs(1)
    tm = x_ref.shape[0]
    R = DISPATCH_ROWS
    slot = t % 2

    def main_copies(step, buf, fn):
        for e in range(n_exp):
            fn(pltpu.make_async_copy(_rows_at(ys_hbm, base_ref[step * n_exp + e], R),
                                     stage.at[buf, pl.ds(e * R, R)], sems.at[buf]))

    @pl.when(t == 0)
    def _():
        main_copies(0, 0, lambda cp: cp.start())

    @pl.when(t + 1 < n_steps)
    def _():
        main_copies(t + 1, 1 - slot, lambda cp: cp.start())

    meta = meta_ref[...]
    col = lambda k: meta[:, k:k + 1]
    e0, e1 = col(META_E0).astype(I32), col(META_E1).astype(I32)
    r0, r1 = col(META_R0).astype(I32), col(META_R1).astype(I32)
    w0, w1 = col(META_W0), col(META_W1)
    lane = lax.broadcasted_iota(I32, (tm, R), 1)

    rels, ws, picks = [], [], []
    for e in range(n_exp):
        rel = jnp.where(e0 == e, r0, jnp.where(e1 == e, r1, -1))
        rels.append(jnp.where(rel >= 0, rel - off_ref[t * n_exp + e], -1))
        ws.append(jnp.where(e0 == e, w0, jnp.where(e1 == e, w1, 0.0)))
        picks.append(jnp.where(lane == rels[e], ws[e], 0.0))
    weighted = jnp.concatenate(picks, axis=1).astype(BF16)

    acc_ref[...] = jnp.zeros_like(acc_ref)
    for e in range(n_exp):
        for blk in range(1, -(-tm // R)):
            @pl.when(cnt_ref[t * n_exp + e] > blk * R)
            def _(e=e, blk=blk):
                cp = pltpu.make_async_copy(_rows_at(ys_hbm, base_ref[t * n_exp + e] + blk * R, R),
                                           extra, extra_sem)
                cp.start()
                cp.wait()
                more = jnp.where(lane + blk * R == rels[e], ws[e], 0.0).astype(BF16)
                acc_ref[...] += jnp.dot(more, extra[...], preferred_element_type=F32)

    main_copies(t, slot, lambda cp: cp.wait())
    y = acc_ref[...] + jnp.dot(weighted, stage[slot], preferred_element_type=F32)
    z = alpha * x_ref[...] + (1.0 + gate_ref[...]) * y
    o_ref[...] = _layer_norm(z, lg_ref[...], lb_ref[...])


def _combine(ys, base_te, off_te, cnt_te, meta, x, gate, ln_g, ln_b, *, alpha, tm):
    B, S, D = x.shape
    nt = S // tm
    n_exp = base_te.shape[0] // (B * nt)
    row = pl.BlockSpec((None, tm, D), lambda b, i, *_: (b, i, 0))
    vec = pl.BlockSpec((1, D), lambda b, i, *_: (0, 0))
    grid_spec = pltpu.PrefetchScalarGridSpec(
        num_scalar_prefetch=3,
        grid=(B, nt),
        in_specs=[pl.BlockSpec(memory_space=pl.ANY),
                  pl.BlockSpec((None, tm, LANES), lambda b, i, *_: (b, i, 0)),
                  row, pl.BlockSpec((None, 1, D), lambda b, i, *_: (b, 0, 0)), vec, vec],
        out_specs=row,
        scratch_shapes=[pltpu.VMEM((2, n_exp * DISPATCH_ROWS, D), BF16),
                        pltpu.VMEM((DISPATCH_ROWS, D), BF16),
                        pltpu.VMEM((tm, D), F32),
                        pltpu.SemaphoreType.DMA((2,)),
                        pltpu.SemaphoreType.DMA(())],
    )
    return pl.pallas_call(
        functools.partial(_combine_kernel, alpha=alpha, n_exp=n_exp),
        grid_spec=grid_spec,
        out_shape=jax.ShapeDtypeStruct((B, S, D), F32),
        compiler_params=_params(("arbitrary", "arbitrary")),
        name="moe_combine",
    )(base_te, off_te, cnt_te, ys, meta, x, gate, ln_g, ln_b)


def _moe_ffn(x, scale, shift, gate, w_router, wg, wu, wd, ln_g, ln_b, *, alpha, tm, tm_e, tf):
    B, S, D = x.shape
    T = B * S
    nt = T // tm
    n_exp = w_router.shape[1]
    wr = jnp.zeros((D, LANES), F32).at[:, :n_exp].set(w_router)
    wr_hi = wr.astype(BF16)
    wr = jnp.stack([wr_hi, (wr - wr_hi.astype(F32)).astype(BF16)])
    u, meta = _router(x, scale, shift, wr, n_exp=n_exp, tm=tm)

    meta2 = meta.reshape(T, LANES)
    e0 = meta2[:, META_E0].astype(I32)
    e1 = meta2[:, META_E1].astype(I32)
    r0 = meta2[:, META_R0].astype(I32)
    r1 = meta2[:, META_R1].astype(I32)
    experts = jnp.arange(n_exp, dtype=I32)
    chosen = ((e0[:, None] == experts) | (e1[:, None] == experts)).astype(I32)
    cnt_te = chosen.reshape(nt, tm, n_exp).sum(axis=1)
    seg_te = -(-cnt_te // SEGMENT_ALIGN) * SEGMENT_ALIGN
    off_te = jnp.cumsum(cnt_te, axis=0) - cnt_te
    filled = seg_te.sum(axis=0)
    sizes = -(-(filled + DISPATCH_SLACK) // tm_e) * tm_e
    ends = jnp.cumsum(sizes)
    starts = ends - sizes
    base_te = starts[None, :] + jnp.cumsum(seg_te, axis=0) - seg_te
    shift_te = (base_te - off_te)[:, None, :]

    def rows_of(e, r):
        pick = (e[:, None] == experts).astype(I32).reshape(nt, tm, n_exp)
        return r + (pick * shift_te).sum(axis=-1).reshape(T)

    d0, d1 = rows_of(e0, r0), rows_of(e1, r1)
    bound = TOP_K * T + n_exp * ((SEGMENT_ALIGN - 1) * nt + DISPATCH_SLACK)
    n_tiles = -(-bound // tm_e) + n_exp + 1
    n_active = (ends[-1] // tm_e).astype(I32)
    tile_id = jnp.minimum(jnp.arange(n_tiles, dtype=I32), n_active - 1)
    tile_expert = jnp.sum((ends[None, :] <= (tile_id * tm_e)[:, None]).astype(I32), axis=1)
    data_end = jnp.sum((tile_expert[:, None] == experts) * (starts + filled)[None, :], axis=1)
    left = data_end - tile_id * tm_e
    tile_fill = jnp.where(left <= 0, TILE_EMPTY, jnp.where(left <= tm_e // 2, TILE_HALF, TILE_FULL)).astype(I32)
    meta_t = jnp.stack([e0, e1, d0, d1] + [jnp.zeros_like(e0)] * 4)

    xs = _dispatch(u.reshape(T, D), meta_t, base_te.reshape(-1), cnt_te.reshape(-1),
                   jnp.concatenate([starts + filled, ends[-1:]]),
                   jnp.concatenate([sizes - filled, n_tiles * tm_e - ends[-1:]]),
                   n_rows=n_tiles * tm_e, tm=tm)
    ys = _expert_ffn(xs, tile_expert, n_active.reshape(1), tile_fill, wg, wu, wd, tm=tm_e, tf=tf)
    return _combine(ys, base_te.reshape(-1), off_te.reshape(-1), cnt_te.reshape(-1),
                    meta, x, gate, ln_g, ln_b, alpha=alpha, tm=tm)


class _Tiles(NamedTuple):
    token_rows: int
    rope_rows: int
    ffn_chunk: int
    expert_rows: int
    expert_ff: int


def _tile_plan(batch, seq):
    return _Tiles(token_rows=min(512, seq), rope_rows=min(1024, seq), ffn_chunk=2 * LANES,
                  expert_rows=min(1024, batch * seq), expert_ff=512)


def kernel(x, c, positions, w_in, w_out, attn_norm_gain, rec_norm_gain, rec_lb_logits, ada_w, ada_b,
           ln_gain, ln_bias, ffn_w_gate, ffn_w_up, ffn_w_down, moe_router, moe_w_gate, moe_w_up,
           moe_w_down):
    B, S, D = x.shape
    depth = w_in.shape[0]
    alpha = (2 * depth) ** 0.25

    p = jax.nn.softmax(rec_lb_logits.astype(F32), axis=0)
    cum = jnp.cumsum(p, axis=0)
    lb_all = cum - cum[0:1]
    half = ATTN_HEAD_DIM // 2
    inv_freq = ROPE_THETA ** (-jnp.arange(half, dtype=F32) / half)
    inv_freq_lanes = jnp.tile(inv_freq, LANES // half).reshape(1, LANES)
    pos_f = positions.astype(F32).reshape(B, S, 1)
    n_qk = 2 * ATTN_WIDTH
    w_in_b = jnp.concatenate([_qk_lane_order(w_in[..., :n_qk]), w_in[..., n_qk:]], axis=-1).astype(BF16)
    w_out_b = w_out.astype(BF16)
    ff = ffn_w_gate.shape[2]
    ffp = -(-ff // (2 * LANES)) * (2 * LANES)
    pad_c = lambda w: jnp.pad(w.astype(BF16), ((0, 0), (0, 0), (0, ffp - ff)))
    ffn_g, ffn_u = pad_c(ffn_w_gate), pad_c(ffn_w_up)
    ffn_d = jnp.pad(ffn_w_down.astype(BF16), ((0, 0), (0, ffp - ff), (0, 0)))

    mods = _ada_modulation(c, ada_w, ada_b)
    tiles = _tile_plan(B, S)
    rope_cos, rope_sin = _rope_tables(pos_f, inv_freq_lanes, tm=tiles.rope_rows)

    def mod(layer, sub):
        m = mods[layer * 2 + sub].reshape(3, B, 1, D)
        return m[0], m[1], m[2]

    vec = lambda a: a.reshape(1, -1)
    for layer in range(depth):
        shift, scale, gate = mod(layer, 0)
        proj = _in_projection(x, scale, shift, rope_cos, rope_sin, w_in_b, layer, tm=tiles.token_rows)
        attn = _dilated_attention(proj)
        rec = _hgrn2(proj, lb_all[layer], vec(rec_norm_gain[layer]))
        x = _out_projection(attn, rec, x, gate, vec(attn_norm_gain[layer]), w_out_b, layer,
                            vec(ln_gain[layer, 0]), vec(ln_bias[layer, 0]), alpha=alpha,
                            tm=tiles.token_rows)
        shift, scale, gate = mod(layer, 1)
        j = layer // 2
        if layer % 2 == 0:
            x = _dense_ffn(x, scale, shift, gate, ffn_g[j], ffn_u[j], ffn_d[j],
                           vec(ln_gain[layer, 1]), vec(ln_bias[layer, 1]),
                           alpha=alpha, tm=tiles.token_rows, tc=tiles.ffn_chunk)
        else:
            x = _moe_ffn(x, scale, shift, gate, moe_router[j], moe_w_gate[j], moe_w_up[j],
                         moe_w_down[j], vec(ln_gain[layer, 1]), vec(ln_bias[layer, 1]),
                         alpha=alpha, tm=tiles.token_rows, tm_e=tiles.expert_rows, tf=tiles.expert_ff)
    return x
```

```python
import functools
from typing import NamedTuple

import jax
import jax.numpy as jnp
from jax import lax
from jax.experimental import pallas as pl
from jax.experimental.pallas import tpu as pltpu

F32 = jnp.float32
BF16 = jnp.bfloat16
I32 = jnp.int32

LANES = 128
V7X_VMEM_LIMIT_BYTES = 56 * 1024 * 1024

ATTN_WIDTH = 512
ATTN_HEAD_DIM = 64
DILATIONS = (1, 4, 16)
ATTN_RADIUS = 64
ROPE_THETA = 10000.0
REC_WIDTH = 512
REC_DIM = 128
REC_CHUNK = 64
TOP_K = 2
LN_EPS = 1e-5
RMS_EPS = 1e-6
MASK_VALUE = -1e30
DECAY_EXP_CLAMP = 80.0
LOG2_E = 1.4426950408889634


def _params(semantics):
    return pltpu.CompilerParams(dimension_semantics=semantics,
                                vmem_limit_bytes=V7X_VMEM_LIMIT_BYTES)


def _sigmoid(x):
    return 1.0 / (1.0 + jnp.exp(-x))


def _layer_norm(z, gain, bias):
    mu = jnp.mean(z, axis=-1, keepdims=True)
    zc = z - mu
    var = jnp.mean(zc * zc, axis=-1, keepdims=True)
    return zc * lax.rsqrt(var + LN_EPS) * gain + bias


def _ada_kernel(c_ref, w_ref, b_ref, o_ref):
    c = c_ref[...]
    o_ref[...] = jnp.dot(c * _sigmoid(c), w_ref[...], precision=lax.Precision.HIGHEST,
                         preferred_element_type=F32) + b_ref[...]


def _ada_modulation(c, ada_w, ada_b):
    B, D = c.shape
    n = ada_w.shape[0] * ada_w.shape[1]
    n3 = ada_w.shape[-1]
    tn = D
    return pl.pallas_call(
        _ada_kernel,
        grid=(n, n3 // tn),
        in_specs=[pl.BlockSpec((B, D), lambda i, j: (0, 0)),
                  pl.BlockSpec((None, D, tn), lambda i, j: (i, 0, j)),
                  pl.BlockSpec((None, 1, tn), lambda i, j: (i, 0, j))],
        out_specs=pl.BlockSpec((None, None, B, tn), lambda i, j: (i, j, 0, 0)),
        out_shape=jax.ShapeDtypeStruct((n, n3 // tn, B, tn), F32),
        compiler_params=_params(("arbitrary", "arbitrary")),
        name="ada_modulation",
    )(c, ada_w.reshape(n, D, n3), ada_b.reshape(n, 1, n3))


ROPE_HALF = ATTN_HEAD_DIM // 2


def _qk_lane_order(w_cols):
    lead = w_cols.shape[:-1]
    w = w_cols.reshape(*lead, -1, 2, 2, ROPE_HALF)
    return jnp.swapaxes(w, -3, -2).reshape(*lead, -1)


def _rope_kernel(pos_ref, invf_ref, cos_ref, sin_ref):
    ang = pos_ref[...] * invf_ref[...]
    lane = lax.broadcasted_iota(I32, (1, LANES), 1)
    sin = jnp.sin(ang)
    cos_ref[...] = jnp.cos(ang)
    sin_ref[...] = jnp.where(lane < LANES // 2, -sin, sin)


def _rope_tables(pos_f, inv_freq_lanes, *, tm):
    B, S, _ = pos_f.shape
    tab = pl.BlockSpec((None, tm, LANES), lambda b, i: (b, i, 0))
    return pl.pallas_call(
        _rope_kernel,
        grid=(B, S // tm),
        in_specs=[pl.BlockSpec((None, tm, 1), lambda b, i: (b, i, 0)),
                  pl.BlockSpec((1, LANES), lambda b, i: (0, 0))],
        out_specs=[tab, tab],
        out_shape=[jax.ShapeDtypeStruct((B, S, LANES), F32)] * 2,
        compiler_params=_params(("arbitrary", "arbitrary")),
        name="rope_tables",
    )(pos_f, inv_freq_lanes)


def _inproj_kernel(x_ref, sc_ref, sh_ref, cos_ref, sin_ref, w_ref, o_ref, *, tn):
    u = (x_ref[...] * (1.0 + sc_ref[...]) + sh_ref[...]).astype(BF16)
    cos = cos_ref[...]
    sin_signed = sin_ref[...]
    q_scale = ATTN_HEAD_DIM ** -0.5
    for j in range(w_ref.shape[1] // tn):
        acc = jnp.dot(u, w_ref[:, j * tn:(j + 1) * tn], preferred_element_type=F32)
        if j > 0:
            o_ref[:, j * tn:(j + 1) * tn] = acc
            continue
        for kk in range(tn // LANES):
            c = acc[:, kk * LANES:(kk + 1) * LANES]
            r = c * cos + pltpu.roll(c, LANES // 2, 1) * sin_signed
            if kk * LANES < ATTN_WIDTH:
                r = r * q_scale
            o_ref[:, kk * LANES:(kk + 1) * LANES] = r


def _in_projection(x, scale, shift, rope_cos, rope_sin, w_in_bf16, layer, *, tm):
    B, S, D = x.shape
    n_cols = w_in_bf16.shape[2]
    tn = 2 * ATTN_WIDTH
    assert S % tm == 0 and n_cols % tn == 0
    return pl.pallas_call(
        functools.partial(_inproj_kernel, tn=tn),
        grid=(B, S // tm),
        in_specs=[pl.BlockSpec((None, tm, D), lambda b, i: (b, i, 0)),
                  pl.BlockSpec((None, 1, D), lambda b, i: (b, 0, 0)),
                  pl.BlockSpec((None, 1, D), lambda b, i: (b, 0, 0)),
                  pl.BlockSpec((None, tm, LANES), lambda b, i: (b, i, 0)),
                  pl.BlockSpec((None, tm, LANES), lambda b, i: (b, i, 0)),
                  pl.BlockSpec((None, D, n_cols), lambda b, i: (layer, 0, 0))],
        out_specs=pl.BlockSpec((None, tm, n_cols), lambda b, i: (b, i, 0)),
        out_shape=jax.ShapeDtypeStruct((B, S, n_cols), F32),
        compiler_params=_params(("arbitrary", "arbitrary")),
        name="in_projection",
    )(x, scale, shift, rope_cos, rope_sin, w_in_bf16)


def _attn_kernel(q_ref, k_ref, v_ref, o_ref, qs, ks, vs, qf, kf, vf, nm, nl, na, bias_s, *, S):
    lane = lax.broadcasted_iota(I32, (1, LANES), 1)
    head0 = lane < ATTN_HEAD_DIM
    head0_qk = (lane % ATTN_HEAD_DIM) < ROPE_HALF

    for p, d in enumerate(DILATIONS):
        L = S // d
        tq = min(128, L)
        W = min(2 * tq, L)
        nb = L // tq

        d_prev = DILATIONS[p - 1] if p else 1
        ratio, l_prev = d // d_prev, S // d_prev
        keep_f32 = 0 < p < len(DILATIONS) - 1
        for src, stage, dst in ((q_ref, qf, qs), (k_ref, kf, ks), (v_ref, vf, vs)):
            src = src if p <= 1 else stage
            for r in range(d):
                a, r_prev = divmod(r, d_prev)
                rows = pl.ds(r_prev * l_prev + a, L, stride=ratio) if ratio > 1 else pl.ds(0, L)
                val = src[rows, :]
                if keep_f32:
                    stage[r * L:(r + 1) * L, :] = val
                dst[r * L:(r + 1) * L, :] = val.astype(BF16)

        n_blocks = d * nb
        rc = lax.broadcasted_iota(I32, (tq, W), 0) - lax.broadcasted_iota(I32, (tq, W), 1)
        for which, delta in enumerate((0, ATTN_RADIUS, tq)):
            bias_s[which, pl.ds(0, tq), pl.ds(0, W)] = jnp.where(
                jnp.abs(rc + delta) <= ATTN_RADIUS, 0.0, MASK_VALUE)
        spans, scores, probs, sums = {}, {}, {}, {}

        def score_stage(g, L=L, tq=tq, W=W, nb=nb):
            r, n = divmod(g, nb)
            q0 = r * L + n * tq
            ws = min(max(n * tq - ATTN_RADIUS, 0), L - W)
            qb = qs[q0:q0 + tq, :]
            zero = jnp.zeros_like(qb)
            q2 = jnp.concatenate([jnp.where(head0_qk, qb, zero), jnp.where(head0_qk, zero, qb)], axis=0)
            s = lax.dot_general(q2, ks[r * L + ws:r * L + ws + W, :], (((1,), (1,)), ((), ())),
                                preferred_element_type=F32)
            which = {0: 0, ATTN_RADIUS: 1}.get(n * tq - ws, 2)
            bias = bias_s[which, 0:tq, 0:W]
            scores[g] = s + jnp.concatenate([bias, bias], axis=0)
            rows = pl.ds(r + d * n * tq, tq, stride=d) if d > 1 else pl.ds(n * tq, tq)
            spans[g] = (rows, slice(r * L + ws, r * L + ws + W))

        def softmax_stage(g):
            s = scores.pop(g)
            m = jnp.max(s, axis=1, keepdims=True)
            e = jnp.exp(s - m)
            probs[g] = (m, e.astype(BF16))
            sums[g] = jnp.sum(e, axis=1, keepdims=True)

        def value_stage(g, tq=tq):
            m, e = probs.pop(g)
            l = sums.pop(g)
            rows, keys = spans.pop(g)
            out = jnp.dot(e, vs[keys, :], preferred_element_type=F32)
            nm[p, rows, :] = jnp.where(head0, m[:tq], m[tq:])
            nl[p, rows, :] = jnp.where(head0, l[:tq], l[tq:])
            na[p, rows, :] = jnp.where(head0, out[:tq], out[tq:])

        for step in range(n_blocks + 2):
            if step < n_blocks:
                score_stage(step)
            if 1 <= step <= n_blocks:
                softmax_stage(step - 1)
            if step >= 2:
                value_stage(step - 2)

    m_all = jnp.maximum(jnp.maximum(nm[0], nm[1]), nm[2])
    num = jnp.zeros((S, LANES), F32)
    den = jnp.zeros((S, LANES), F32)
    for p in range(len(DILATIONS)):
        w = jnp.exp(nm[p] - m_all)
        num = num + w * na[p]
        den = den + w * nl[p]
    o_ref[...] = (num / den).astype(o_ref.dtype)


def _dilated_attention(proj):
    B, S, _ = proj.shape
    n_pairs = ATTN_WIDTH // LANES
    assert S % (16 * DILATIONS[-1]) == 0
    blk = lambda off: pl.BlockSpec((None, S, LANES), lambda b, h: (b, 0, off + h))
    return pl.pallas_call(
        functools.partial(_attn_kernel, S=S),
        grid=(B, n_pairs),
        in_specs=[blk(0), blk(n_pairs), blk(2 * n_pairs)],
        out_specs=pl.BlockSpec((None, S, LANES), lambda b, h: (b, 0, h)),
        out_shape=jax.ShapeDtypeStruct((B, S, ATTN_WIDTH), BF16),
        scratch_shapes=[pltpu.VMEM((S, LANES), BF16)] * 3
                       + [pltpu.VMEM((S, LANES), F32)] * 3
                       + [pltpu.VMEM((len(DILATIONS), S, LANES), F32)] * 3
                       + [pltpu.VMEM((3, min(128, S), min(256, S)), F32)],
        compiler_params=_params(("arbitrary", "arbitrary")),
        name="dilated_attention",
    )(proj, proj, proj)


REC_GROUP_CHUNKS = 4
REC_GROUP_ROWS = REC_GROUP_CHUNKS * REC_CHUNK
REC_STAGE_GROUPS = 8


def _hgrn_kernel(rq_ref, zf_ref, zb_ref, ri_ref, rg_ref, lb_ref, gain_ref, o_ref,
                 q_s, vt_s, oi_s, qp_s, ut_s, dd_s, st_s, os_s, *, S):
    C, G, GR = REC_CHUNK, REC_GROUP_CHUNKS, REC_GROUP_ROWS
    nc, ng = S // C, S // GR
    sg = min(REC_STAGE_GROUPS, ng)
    assert ng % sg == 0

    row = lax.broadcasted_iota(I32, (GR, GR), 0)
    col = lax.broadcasted_iota(I32, (GR, GR), 1)
    same_chunk = (row // C) == (col // C)
    t_i = lax.broadcasted_iota(I32, (C, C), 0)
    s_i = lax.broadcasted_iota(I32, (C, C), 1)

    def prep(i, carry):
        r0 = pl.multiple_of(i * GR, GR)
        rq = rq_ref[pl.ds(r0, GR), :]
        q_s[pl.ds(r0, GR), :] = rq * _sigmoid(rq)
        v = ri_ref[pl.ds(r0, GR), :]
        for c in range(G):
            vt_s[i * G + c] = v[c * C:(c + 1) * C, :].T.astype(BF16)
        return carry

    lax.fori_loop(0, ng, prep, 0, unroll=True)

    for direction in range(2):
        fwd = direction == 0
        z_ref = zf_ref if fwd else zb_ref
        cum = jnp.where(same_chunk & ((col <= row) if fwd else (col >= row)), 1.0, 0.0).astype(BF16)
        keep = (s_i <= t_i) if fwd else (s_i >= t_i)
        mid_row = C // 2 - 1 if fwd else C // 2
        last_row = C - 1 if fwd else 0

        def phase_a(it, carry, fwd=fwd, z_ref=z_ref, cum=cum, keep=keep,
                    mid_row=mid_row, last_row=last_row, direction=direction):
            lb = lb_ref[direction:direction + 1, :]
            log_lb = jnp.log(lb)
            log_1m_lb = jnp.log1p(-lb)
            r0s, c0s, k3s, bcs, atts, stage3 = [], [], [], [], [], []
            for u in range(sg):
                i = it * sg + u
                r0 = pl.multiple_of(i * GR, GR)
                z = z_ref[pl.ds(r0, GR), :]
                e = jnp.exp(-jnp.abs(z))
                t = log_1m_lb + (jnp.minimum(z, 0.0) - jnp.log(1.0 + e))
                g = jnp.maximum(log_lb, t) + jnp.log(1.0 + jnp.exp(-jnp.abs(log_lb - t)))
                kk = (1.0 - lb) * (jnp.where(z > 0, e, 1.0) / (1.0 + e))
                g = g * LOG2_E
                g1 = g.astype(BF16)
                r1 = g - g1.astype(F32)
                g2 = r1.astype(BF16)
                g3 = (r1 - g2.astype(F32)).astype(BF16)
                bcs.append(jnp.dot(cum, g1, preferred_element_type=F32)
                           + jnp.dot(cum, g2, preferred_element_type=F32)
                           + jnp.dot(cum, g3, preferred_element_type=F32))
                k3s.append(kk.reshape(G, C, REC_DIM))
                r0s.append(r0)
                c0s.append(pl.multiple_of(i * G, G))
            for u in range(sg):
                b3 = bcs[u].reshape(G, C, REC_DIM)
                mid = b3[:, mid_row:mid_row + 1, :]
                last = b3[:, last_row:last_row + 1, :]
                e3 = b3 - mid
                q3 = q_s[pl.ds(r0s[u], GR), :].reshape(G, C, REC_DIM)
                clamp = DECAY_EXP_CLAMP * LOG2_E
                qt = (q3 * jnp.exp2(jnp.minimum(e3, clamp))).astype(BF16)
                kt = (k3s[u] * jnp.exp2(jnp.minimum(-e3, clamp))).astype(BF16)
                atts.append(jnp.einsum('gtc,gsc->gts', qt, kt, preferred_element_type=F32))
                qp_s[pl.ds(r0s[u], GR), :] = (q3 * jnp.exp2(b3)).astype(BF16).reshape(GR, REC_DIM)
                stage3.append(((k3s[u] * jnp.exp2(last - b3)).astype(BF16), jnp.exp2(last)))
            for u in range(sg):
                att = jnp.where(keep[None], atts[u], 0.0).astype(BF16)
                v3 = ri_ref[pl.ds(r0s[u], GR), :].reshape(G, C, REC_DIM).astype(BF16)
                oi = jnp.einsum('gts,gsv->gtv', att, v3, preferred_element_type=F32)
                oi_s[pl.ds(r0s[u], GR), :] = oi.reshape(GR, REC_DIM)
                kp, decay = stage3[u]
                ut_s[pl.ds(c0s[u], G)] = jnp.einsum('gvs,gsc->gvc', vt_s[pl.ds(c0s[u], G)], kp,
                                                    preferred_element_type=F32)
                dd_s[pl.ds(c0s[u], G)] = decay
            return carry

        lax.fori_loop(0, ng // sg, phase_a, 0)

        def scan(c, st, fwd=fwd):
            idx = c if fwd else nc - 1 - c
            st_s[idx] = st.astype(BF16)
            return dd_s[idx] * st + ut_s[idx]

        lax.fori_loop(0, nc, scan, jnp.zeros((REC_DIM, REC_DIM), F32), unroll=True)

        def phase_c(it, carry, fwd=fwd):
            r0s = [pl.multiple_of((it * sg + u) * GR, GR) for u in range(sg)]
            oos = []
            for u in range(sg):
                c0 = pl.multiple_of((it * sg + u) * G, G)
                qp3 = qp_s[pl.ds(r0s[u], GR), :].reshape(G, C, REC_DIM)
                oos.append(jnp.einsum('gtc,gvc->gtv', qp3, st_s[pl.ds(c0, G)],
                                      preferred_element_type=F32))
            for u in range(sg):
                rows = pl.ds(r0s[u], GR)
                tot = oi_s[rows, :] + oos[u].reshape(GR, REC_DIM)
                os_s[rows, :] = tot if fwd else os_s[rows, :] + tot
            return carry

        lax.fori_loop(0, ng // sg, phase_c, 0)

    o = os_s[...]
    o = o * lax.rsqrt(jnp.mean(o * o, axis=-1, keepdims=True) + RMS_EPS) * gain_ref[...]
    o_ref[...] = (o * _sigmoid(rg_ref[...])).astype(o_ref.dtype)


def _hgrn2(proj, lb, rec_gain):
    B, S, _ = proj.shape
    nh = REC_WIDTH // REC_DIM
    assert S % REC_GROUP_ROWS == 0
    col0 = 3 * ATTN_WIDTH // LANES
    blk = lambda k: pl.BlockSpec((None, S, REC_DIM), lambda b, h: (b, 0, col0 + k * nh + h))
    nc = S // REC_CHUNK
    return pl.pallas_call(
        functools.partial(_hgrn_kernel, S=S),
        grid=(B, nh),
        in_specs=[blk(0), blk(1), blk(2), blk(3), blk(4),
                  pl.BlockSpec((2, REC_DIM), lambda b, h: (0, h)),
                  pl.BlockSpec((1, REC_DIM), lambda b, h: (0, h))],
        out_specs=pl.BlockSpec((None, S, REC_DIM), lambda b, h: (b, 0, h)),
        out_shape=jax.ShapeDtypeStruct((B, S, REC_WIDTH), BF16),
        scratch_shapes=[pltpu.VMEM((S, REC_DIM), F32),
                        pltpu.VMEM((nc, REC_DIM, REC_CHUNK), BF16),
                        pltpu.VMEM((S, REC_DIM), F32),
                        pltpu.VMEM((S, REC_DIM), BF16),
                        pltpu.VMEM((nc, REC_DIM, REC_DIM), F32),
                        pltpu.VMEM((nc, 1, REC_DIM), F32),
                        pltpu.VMEM((nc, REC_DIM, REC_DIM), BF16),
                        pltpu.VMEM((S, REC_DIM), F32)],
        compiler_params=_params(("arbitrary", "arbitrary")),
        name="hgrn2",
    )(proj, proj, proj, proj, proj, lb, rec_gain)


def _outproj_kernel(a_ref, r_ref, x_ref, gate_ref, ag_ref, w_ref, lg_ref, lb_ref, o_ref, *, alpha):
    a = a_ref[...].astype(F32)
    an = a * lax.rsqrt(jnp.mean(a * a, axis=-1, keepdims=True) + RMS_EPS) * ag_ref[...]
    y = (jnp.dot(an.astype(BF16), w_ref[:ATTN_WIDTH, :], preferred_element_type=F32)
         + jnp.dot(r_ref[...], w_ref[ATTN_WIDTH:, :], preferred_element_type=F32))
    z = alpha * x_ref[...] + (1.0 + gate_ref[...]) * y
    o_ref[...] = _layer_norm(z, lg_ref[...], lb_ref[...])


def _out_projection(attn, rec, x, gate, attn_gain, w_out_bf16, layer, ln_g, ln_b, *, alpha, tm):
    B, S, D = x.shape
    row = lambda w: pl.BlockSpec((None, tm, w), lambda b, i: (b, i, 0))
    vec = lambda w: pl.BlockSpec((1, w), lambda b, i: (0, 0))
    return pl.pallas_call(
        functools.partial(_outproj_kernel, alpha=alpha),
        grid=(B, S // tm),
        in_specs=[row(ATTN_WIDTH), row(REC_WIDTH), row(D),
                  pl.BlockSpec((None, 1, D), lambda b, i: (b, 0, 0)),
                  vec(ATTN_WIDTH),
                  pl.BlockSpec((None, ATTN_WIDTH + REC_WIDTH, D), lambda b, i: (layer, 0, 0)),
                  vec(D), vec(D)],
        out_specs=row(D),
        out_shape=jax.ShapeDtypeStruct((B, S, D), F32),
        compiler_params=_params(("arbitrary", "arbitrary")),
        name="out_projection",
    )(attn, rec, x, gate, attn_gain, w_out_bf16, ln_g, ln_b)


def _ffn_kernel(x_ref, sc_ref, sh_ref, gate_ref, wg_ref, wu_ref, wd_ref, lg_ref, lb_ref, o_ref,
                *, alpha, tc):
    x = x_ref[...]
    u = (x * (1.0 + sc_ref[...]) + sh_ref[...]).astype(BF16)
    acc = jnp.zeros(x.shape, F32)
    for c in range(wg_ref.shape[1] // tc):
        cols = slice(c * tc, (c + 1) * tc)
        g = jnp.dot(u, wg_ref[:, cols], preferred_element_type=F32)
        up = jnp.dot(u, wu_ref[:, cols], preferred_element_type=F32)
        h = (g * _sigmoid(g) * up).astype(BF16)
        acc = acc + jnp.dot(h, wd_ref[cols, :], preferred_element_type=F32)
    z = alpha * x + (1.0 + gate_ref[...]) * acc
    o_ref[...] = _layer_norm(z, lg_ref[...], lb_ref[...])


def _dense_ffn(x, scale, shift, gate, wg, wu, wd, ln_g, ln_b, *, alpha, tm, tc):
    B, S, D = x.shape
    ffp = wg.shape[1]
    assert ffp % tc == 0
    row = pl.BlockSpec((None, tm, D), lambda b, i: (b, i, 0))
    mod = pl.BlockSpec((None, 1, D), lambda b, i: (b, 0, 0))
    vec = pl.BlockSpec((1, D), lambda b, i: (0, 0))
    return pl.pallas_call(
        functools.partial(_ffn_kernel, alpha=alpha, tc=tc),
        grid=(B, S // tm),
        in_specs=[row, mod, mod, mod,
                  pl.BlockSpec((D, ffp), lambda b, i: (0, 0)),
                  pl.BlockSpec((D, ffp), lambda b, i: (0, 0)),
                  pl.BlockSpec((ffp, D), lambda b, i: (0, 0)),
                  vec, vec],
        out_specs=row,
        out_shape=jax.ShapeDtypeStruct((B, S, D), F32),
        compiler_params=_params(("arbitrary", "arbitrary")),
        name="dense_ffn",
    )(x, scale, shift, gate, wg, wu, wd, ln_g, ln_b)


META_E0, META_E1, META_W0, META_W1, META_R0, META_R1 = range(6)
SEGMENT_ALIGN = 16
DISPATCH_ROWS = 256
DISPATCH_HALF = DISPATCH_ROWS // 2
DISPATCH_SLACK = DISPATCH_ROWS


def _router_kernel(x_ref, sc_ref, sh_ref, wr_ref, u_ref, meta_ref, carry_ref, *, n_exp):
    tm = x_ref.shape[0]

    @pl.when((pl.program_id(0) == 0) & (pl.program_id(1) == 0))
    def _():
        carry_ref[...] = jnp.zeros_like(carry_ref)

    u = x_ref[...] * (1.0 + sc_ref[...]) + sh_ref[...]
    u_ref[...] = u.astype(BF16)
    u_hi = u.astype(BF16)
    u_lo = (u - u_hi.astype(F32)).astype(BF16)
    logits = (jnp.dot(u_hi, wr_ref[0], preferred_element_type=F32)
              + jnp.dot(u_lo, wr_ref[0], preferred_element_type=F32)
              + jnp.dot(u_hi, wr_ref[1], preferred_element_type=F32))
    lane = lax.broadcasted_iota(I32, (tm, LANES), 1)
    neg = -jnp.inf
    l1 = jnp.where(lane < n_exp, logits, neg)
    m1 = jnp.max(l1, axis=1, keepdims=True)
    i1 = jnp.min(jnp.where(l1 == m1, lane, LANES), axis=1, keepdims=True)
    l2 = jnp.where(lane == i1, neg, l1)
    m2 = jnp.max(l2, axis=1, keepdims=True)
    i2 = jnp.min(jnp.where(l2 == m2, lane, LANES), axis=1, keepdims=True)
    e = jnp.exp(m2 - m1)
    w1 = 1.0 / (1.0 + e)
    w2 = e * w1
    sel = jnp.where((lane == i1) | (lane == i2), 1.0, 0.0)
    before = (lax.broadcasted_iota(I32, (tm, tm), 1) < lax.broadcasted_iota(I32, (tm, tm), 0))
    ranks = jnp.dot(jnp.where(before, 1.0, 0.0).astype(BF16), sel.astype(BF16),
                    preferred_element_type=F32) + carry_ref[...]
    r1 = jnp.sum(jnp.where(lane == i1, ranks, 0.0), axis=1, keepdims=True)
    r2 = jnp.sum(jnp.where(lane == i2, ranks, 0.0), axis=1, keepdims=True)
    carry_ref[...] = carry_ref[...] + jnp.sum(sel, axis=0, keepdims=True)
    meta = jnp.zeros((tm, LANES), F32)
    for k, val in ((META_E0, i1.astype(F32)), (META_E1, i2.astype(F32)), (META_W0, w1),
                   (META_W1, w2), (META_R0, r1), (META_R1, r2)):
        meta = jnp.where(lane == k, val, meta)
    meta_ref[...] = meta


def _router(x, scale, shift, w_router_lanes, *, n_exp, tm):
    B, S, D = x.shape
    nt = S // tm
    row = pl.BlockSpec((None, tm, D), lambda b, i: (b, i, 0))
    mod = pl.BlockSpec((None, 1, D), lambda b, i: (b, 0, 0))
    return pl.pallas_call(
        functools.partial(_router_kernel, n_exp=n_exp),
        grid=(B, nt),
        in_specs=[row, mod, mod, pl.BlockSpec((2, D, LANES), lambda b, i: (0, 0, 0))],
        out_specs=[row, pl.BlockSpec((None, tm, LANES), lambda b, i: (b, i, 0))],
        out_shape=[jax.ShapeDtypeStruct((B, S, D), BF16),
                   jax.ShapeDtypeStruct((B, S, LANES), F32)],
        scratch_shapes=[pltpu.VMEM((1, LANES), F32)],
        compiler_params=_params(("arbitrary", "arbitrary")),
        name="moe_router",
    )(x, scale, shift, w_router_lanes)


def _rows_at(hbm, first_row, n_rows):
    return hbm.at[pl.ds(pl.multiple_of(first_row, SEGMENT_ALIGN), n_rows)]


def _dispatch_kernel(base_ref, cnt_ref, padlo_ref, padn_ref, u_ref, meta_ref, xs_hbm,
                     stage, extra, zeros_buf, sems, extra_sem, *, n_exp):
    t = pl.program_id(0)
    last = pl.num_programs(0) - 1
    tm = u_ref.shape[0]
    R, H = DISPATCH_ROWS, DISPATCH_HALF
    slot = t % 2
    u = u_ref[...]
    e0, e1 = meta_ref[0:1, :], meta_ref[1:2, :]
    d0, d1 = meta_ref[2:3, :], meta_ref[3:4, :]
    srow = lax.broadcasted_iota(I32, (R, tm), 0)

    def main_copies(step, buf, fn):
        for e in range(n_exp):
            base = base_ref[step * n_exp + e]
            fn(pltpu.make_async_copy(stage.at[buf, pl.ds(e * R, H)], _rows_at(xs_hbm, base, H),
                                     sems.at[buf]))

            @pl.when(cnt_ref[step * n_exp + e] > H)
            def _(e=e, base=base):
                fn(pltpu.make_async_copy(stage.at[buf, pl.ds(e * R + H, H)],
                                         _rows_at(xs_hbm, base + H, H), sems.at[buf]))

    rels = []
    for e in range(n_exp):
        rel = jnp.where(e0 == e, d0, jnp.where(e1 == e, d1, -1))
        rels.append(jnp.where(rel >= 0, rel - base_ref[t * n_exp + e], -1))
    onehot = jnp.concatenate([jnp.where(srow == rel, 1.0, 0.0) for rel in rels], axis=0).astype(BF16)
    stage[slot] = jnp.dot(onehot, u, preferred_element_type=F32).astype(BF16)

    @pl.when(t > 0)
    def _():
        main_copies(t - 1, 1 - slot, lambda cp: cp.wait())
    main_copies(t, slot, lambda cp: cp.start())

    for e in range(n_exp):
        for blk in range(1, -(-tm // R)):
            @pl.when(cnt_ref[t * n_exp + e] > blk * R)
            def _(e=e, blk=blk):
                more = jnp.where(srow + blk * R == rels[e], 1.0, 0.0).astype(BF16)
                extra[...] = jnp.dot(more, u, preferred_element_type=F32).astype(BF16)
                cp = pltpu.make_async_copy(extra, _rows_at(xs_hbm, base_ref[t * n_exp + e] + blk * R, R),
                                           extra_sem)
                cp.start()
                cp.wait()

    @pl.when(t == last)
    def _():
        main_copies(t, slot, lambda cp: cp.wait())
        zeros_buf[...] = jnp.zeros_like(zeros_buf)
        pieces = [SEGMENT_ALIGN << b for b in range(((R - 1) // SEGMENT_ALIGN).bit_length())]

        def pad_copies(fn):
            for p in range(padlo_ref.shape[0]):
                lo, n = padlo_ref[p], padn_ref[p]
                whole = n // R

                def body(k, carry, lo=lo):
                    fn(pltpu.make_async_copy(zeros_buf, _rows_at(xs_hbm, lo + k * R, R), extra_sem))
                    return carry
                lax.fori_loop(0, whole, body, 0)
                rem = n - whole * R
                for sz in pieces:
                    off = lo + whole * R + (rem // (2 * sz)) * (2 * sz)
                    pl.when((rem // sz) % 2 == 1)(functools.partial(
                        lambda off, sz: fn(pltpu.make_async_copy(
                            zeros_buf.at[pl.ds(0, sz)], _rows_at(xs_hbm, off, sz), extra_sem)), off, sz))

        pad_copies(lambda cp: cp.start())
        pad_copies(lambda cp: cp.wait())


def _dispatch(u, meta_t, base_te, cnt_te, pad_lo, pad_n, *, n_rows, tm):
    T, D = u.shape
    n_exp = pad_lo.shape[0] - 1
    grid_spec = pltpu.PrefetchScalarGridSpec(
        num_scalar_prefetch=4,
        grid=(T // tm,),
        in_specs=[pl.BlockSpec((tm, D), lambda t, *_: (t, 0)),
                  pl.BlockSpec((meta_t.shape[0], tm), lambda t, *_: (0, t))],
        out_specs=pl.BlockSpec(memory_space=pl.ANY),
        scratch_shapes=[pltpu.VMEM((2, n_exp * DISPATCH_ROWS, D), BF16),
                        pltpu.VMEM((DISPATCH_ROWS, D), BF16),
                        pltpu.VMEM((DISPATCH_ROWS, D), BF16),
                        pltpu.SemaphoreType.DMA((2,)),
                        pltpu.SemaphoreType.DMA(())],
    )
    return pl.pallas_call(
        functools.partial(_dispatch_kernel, n_exp=n_exp),
        grid_spec=grid_spec,
        out_shape=jax.ShapeDtypeStruct((n_rows, D), BF16),
        compiler_params=_params(("arbitrary",)),
        name="moe_dispatch",
    )(base_te, cnt_te, pad_lo, pad_n, u, meta_t)


TILE_FULL, TILE_HALF, TILE_EMPTY = 0, 1, 2


def _expert_kernel(te_ref, na_ref, fill_ref, xs_ref, wg_ref, wu_ref, wd_ref, o_ref, acc_ref):
    i = pl.program_id(0)
    j = pl.program_id(1)
    last = pl.num_programs(1) - 1
    active = i < na_ref[0]
    fill = fill_ref[i]
    tm = acc_ref.shape[0]

    @pl.when(active & (j == 0))
    def _():
        acc_ref[...] = jnp.zeros_like(acc_ref)

    def swiglu(rows):
        xb = xs_ref[rows, :]
        g = jnp.dot(xb, wg_ref[...].astype(BF16), preferred_element_type=F32)
        up = jnp.dot(xb, wu_ref[...].astype(BF16), preferred_element_type=F32)
        h = (g * _sigmoid(g) * up).astype(BF16)
        acc_ref[rows, :] += jnp.dot(h, wd_ref[...].astype(BF16), preferred_element_type=F32)

    pl.when(active & (fill == TILE_FULL))(functools.partial(swiglu, pl.ds(0, tm)))
    pl.when(active & (fill == TILE_HALF))(functools.partial(swiglu, pl.ds(0, tm // 2)))

    @pl.when(active & (j == last))
    def _():
        o_ref[...] = acc_ref[...].astype(BF16)

    @pl.when(jnp.logical_not(active) & (j == last))
    def _():
        o_ref[...] = jnp.zeros_like(o_ref)


def _expert_ffn(xs, tile_expert, n_active, tile_fill, wg, wu, wd, *, tm, tf):
    n_rows, D = xs.shape
    ff = wg.shape[2]
    assert n_rows % tm == 0 and ff % tf == 0
    nj = ff // tf

    def jj(i, j, na):
        return jnp.where(i < na[0], j, nj - 1)

    def ii(i, na):
        return jnp.minimum(i, na[0] - 1)

    grid_spec = pltpu.PrefetchScalarGridSpec(
        num_scalar_prefetch=3,
        grid=(n_rows // tm, nj),
        in_specs=[pl.BlockSpec((tm, D), lambda i, j, te, na, tf_: (ii(i, na), 0)),
                  pl.BlockSpec((None, D, tf), lambda i, j, te, na, tf_: (te[i], 0, jj(i, j, na))),
                  pl.BlockSpec((None, D, tf), lambda i, j, te, na, tf_: (te[i], 0, jj(i, j, na))),
                  pl.BlockSpec((None, tf, D), lambda i, j, te, na, tf_: (te[i], jj(i, j, na), 0))],
        out_specs=pl.BlockSpec((tm, D), lambda i, j, te, na, tf_: (i, 0)),
        scratch_shapes=[pltpu.VMEM((tm, D), F32)],
    )
    return pl.pallas_call(
        _expert_kernel,
        grid_spec=grid_spec,
        out_shape=jax.ShapeDtypeStruct((n_rows, D), BF16),
        compiler_params=_params(("arbitrary", "arbitrary")),
        name="moe_experts",
    )(tile_expert, n_active, tile_fill, xs, wg, wu, wd)


def _combine_kernel(base_ref, off_ref, cnt_ref, ys_hbm, meta_ref, x_ref, gate_ref, lg_ref, lb_ref, o_ref,
                    stage, extra, acc_ref, sems, extra_sem, *, alpha, n_exp):
    t = pl.program_id(0) * pl.num_programs(1) + pl.program_id(1)
    n_steps = pl.num_programs(0) * pl.num_programs(1)
    tm = x_ref.shape[0]
    R = DISPATCH_ROWS
    slot = t % 2

    def main_copies(step, buf, fn):
        for e in range(n_exp):
            fn(pltpu.make_async_copy(_rows_at(ys_hbm, base_ref[step * n_exp + e], R),
                                     stage.at[buf, pl.ds(e * R, R)], sems.at[buf]))

    @pl.when(t == 0)
    def _():
        main_copies(0, 0, lambda cp: cp.start())

    @pl.when(t + 1 < n_steps)
    def _():
        main_copies(t + 1, 1 - slot, lambda cp: cp.start())

    meta = meta_ref[...]
    col = lambda k: meta[:, k:k + 1]
    e0, e1 = col(META_E0).astype(I32), col(META_E1).astype(I32)
    r0, r1 = col(META_R0).astype(I32), col(META_R1).astype(I32)
    w0, w1 = col(META_W0), col(META_W1)
    lane = lax.broadcasted_iota(I32, (tm, R), 1)

    rels, ws, picks = [], [], []
    for e in range(n_exp):
        rel = jnp.where(e0 == e, r0, jnp.where(e1 == e, r1, -1))
        rels.append(jnp.where(rel >= 0, rel - off_ref[t * n_exp + e], -1))
        ws.append(jnp.where(e0 == e, w0, jnp.where(e1 == e, w1, 0.0)))
        picks.append(jnp.where(lane == rels[e], ws[e], 0.0))
    weighted = jnp.concatenate(picks, axis=1).astype(BF16)

    acc_ref[...] = jnp.zeros_like(acc_ref)
    for e in range(n_exp):
        for blk in range(1, -(-tm // R)):
            @pl.when(cnt_ref[t * n_exp + e] > blk * R)
            def _(e=e, blk=blk):
                cp = pltpu.make_async_copy(_rows_at(ys_hbm, base_ref[t * n_exp + e] + blk * R, R),
                                           extra, extra_sem)
                cp.start()
                cp.wait()
                more = jnp.where(lane + blk * R == rels[e], ws[e], 0.0).astype(BF16)
                acc_ref[...] += jnp.dot(more, extra[...], preferred_element_type=F32)

    main_copies(t, slot, lambda cp: cp.wait())
    y = acc_ref[...] + jnp.dot(weighted, stage[slot], preferred_element_type=F32)
    z = alpha * x_ref[...] + (1.0 + gate_ref[...]) * y
    o_ref[...] = _layer_norm(z, lg_ref[...], lb_ref[...])


def _combine(ys, base_te, off_te, cnt_te, meta, x, gate, ln_g, ln_b, *, alpha, tm):
    B, S, D = x.shape
    nt = S // tm
    n_exp = base_te.shape[0] // (B * nt)
    row = pl.BlockSpec((None, tm, D), lambda b, i, *_: (b, i, 0))
    vec = pl.BlockSpec((1, D), lambda b, i, *_: (0, 0))
    grid_spec = pltpu.PrefetchScalarGridSpec(
        num_scalar_prefetch=3,
        grid=(B, nt),
        in_specs=[pl.BlockSpec(memory_space=pl.ANY),
                  pl.BlockSpec((None, tm, LANES), lambda b, i, *_: (b, i, 0)),
                  row, pl.BlockSpec((None, 1, D), lambda b, i, *_: (b, 0, 0)), vec, vec],
        out_specs=row,
        scratch_shapes=[pltpu.VMEM((2, n_exp * DISPATCH_ROWS, D), BF16),
                        pltpu.VMEM((DISPATCH_ROWS, D), BF16),
                        pltpu.VMEM((tm, D), F32),
                        pltpu.SemaphoreType.DMA((2,)),
                        pltpu.SemaphoreType.DMA(())],
    )
    return pl.pallas_call(
        functools.partial(_combine_kernel, alpha=alpha, n_exp=n_exp),
        grid_spec=grid_spec,
        out_shape=jax.ShapeDtypeStruct((B, S, D), F32),
        compiler_params=_params(("arbitrary", "arbitrary")),
        name="moe_combine",
    )(base_te, off_te, cnt_te, ys, meta, x, gate, ln_g, ln_b)


def _moe_ffn(x, scale, shift, gate, w_router, wg, wu, wd, ln_g, ln_b, *, alpha, tm, tm_e, tf):
    B, S, D = x.shape
    T = B * S
    nt = T // tm
    n_exp = w_router.shape[1]
    wr = jnp.zeros((D, LANES), F32).at[:, :n_exp].set(w_router)
    wr_hi = wr.astype(BF16)
    wr = jnp.stack([wr_hi, (wr - wr_hi.astype(F32)).astype(BF16)])
    u, meta = _router(x, scale, shift, wr, n_exp=n_exp, tm=tm)

    meta2 = meta.reshape(T, LANES)
    e0 = meta2[:, META_E0].astype(I32)
    e1 = meta2[:, META_E1].astype(I32)
    r0 = meta2[:, META_R0].astype(I32)
    r1 = meta2[:, META_R1].astype(I32)
    experts = jnp.arange(n_exp, dtype=I32)
    chosen = ((e0[:, None] == experts) | (e1[:, None] == experts)).astype(I32)
    cnt_te = chosen.reshape(nt, tm, n_exp).sum(axis=1)
    seg_te = -(-cnt_te // SEGMENT_ALIGN) * SEGMENT_ALIGN
    off_te = jnp.cumsum(cnt_te, axis=0) - cnt_te
    filled = seg_te.sum(axis=0)
    sizes = -(-(filled + DISPATCH_SLACK) // tm_e) * tm_e
    ends = jnp.cumsum(sizes)
    starts = ends - sizes
    base_te = starts[None, :] + jnp.cumsum(seg_te, axis=0) - seg_te
    shift_te = (base_te - off_te)[:, None, :]

    def rows_of(e, r):
        pick = (e[:, None] == experts).astype(I32).reshape(nt, tm, n_exp)
        return r + (pick * shift_te).sum(axis=-1).reshape(T)

    d0, d1 = rows_of(e0, r0), rows_of(e1, r1)
    bound = TOP_K * T + n_exp * ((SEGMENT_ALIGN - 1) * nt + DISPATCH_SLACK)
    n_tiles = -(-bound // tm_e) + n_exp + 1
    n_active = (ends[-1] // tm_e).astype(I32)
    tile_id = jnp.minimum(jnp.arange(n_tiles, dtype=I32), n_active - 1)
    tile_expert = jnp.sum((ends[None, :] <= (tile_id * tm_e)[:, None]).astype(I32), axis=1)
    data_end = jnp.sum((tile_expert[:, None] == experts) * (starts + filled)[None, :], axis=1)
    left = data_end - tile_id * tm_e
    tile_fill = jnp.where(left <= 0, TILE_EMPTY, jnp.where(left <= tm_e // 2, TILE_HALF, TILE_FULL)).astype(I32)
    meta_t = jnp.stack([e0, e1, d0, d1] + [jnp.zeros_like(e0)] * 4)

    xs = _dispatch(u.reshape(T, D), meta_t, base_te.reshape(-1), cnt_te.reshape(-1),
                   jnp.concatenate([starts + filled, ends[-1:]]),
                   jnp.concatenate([sizes - filled, n_tiles * tm_e - ends[-1:]]),
                   n_rows=n_tiles * tm_e, tm=tm)
    ys = _expert_ffn(xs, tile_expert, n_active.reshape(1), tile_fill, wg, wu, wd, tm=tm_e, tf=tf)
    return _combine(ys, base_te.reshape(-1), off_te.reshape(-1), cnt_te.reshape(-1),
                    meta, x, gate, ln_g, ln_b, alpha=alpha, tm=tm)


class _Tiles(NamedTuple):
    token_rows: int
    light_rows: int
    ffn_chunk: int
    expert_rows: int
    expert_ff: int


def _tile_plan(batch, seq):
    return _Tiles(token_rows=min(512, seq), light_rows=min(1024, seq), ffn_chunk=2 * LANES,
                  expert_rows=min(1024, batch * seq), expert_ff=512)


def kernel(x, c, positions, w_in, w_out, attn_norm_gain, rec_norm_gain, rec_lb_logits, ada_w, ada_b,
           ln_gain, ln_bias, ffn_w_gate, ffn_w_up, ffn_w_down, moe_router, moe_w_gate, moe_w_up,
           moe_w_down):
    B, S, D = x.shape
    depth = w_in.shape[0]
    alpha = (2 * depth) ** 0.25

    p = jax.nn.softmax(rec_lb_logits.astype(F32), axis=0)
    cum = jnp.cumsum(p, axis=0)
    lb_all = cum - cum[0:1]
    half = ATTN_HEAD_DIM // 2
    inv_freq = ROPE_THETA ** (-jnp.arange(half, dtype=F32) / half)
    inv_freq_lanes = jnp.tile(inv_freq, LANES // half).reshape(1, LANES)
    pos_f = positions.astype(F32).reshape(B, S, 1)
    n_qk = 2 * ATTN_WIDTH
    w_in_b = jnp.concatenate([_qk_lane_order(w_in[..., :n_qk]), w_in[..., n_qk:]], axis=-1).astype(BF16)
    w_out_b = w_out.astype(BF16)
    ff = ffn_w_gate.shape[2]
    ffp = -(-ff // (2 * LANES)) * (2 * LANES)
    pad_c = lambda w: jnp.pad(w.astype(BF16), ((0, 0), (0, 0), (0, ffp - ff)))
    ffn_g, ffn_u = pad_c(ffn_w_gate), pad_c(ffn_w_up)
    ffn_d = jnp.pad(ffn_w_down.astype(BF16), ((0, 0), (0, ffp - ff), (0, 0)))

    mods = _ada_modulation(c, ada_w, ada_b)
    tiles = _tile_plan(B, S)
    rope_cos, rope_sin = _rope_tables(pos_f, inv_freq_lanes, tm=tiles.light_rows)

    def mod(layer, sub):
        m = mods[layer * 2 + sub].reshape(3, B, 1, D)
        return m[0], m[1], m[2]

    vec = lambda a: a.reshape(1, -1)
    for layer in range(depth):
        shift, scale, gate = mod(layer, 0)
        proj = _in_projection(x, scale, shift, rope_cos, rope_sin, w_in_b, layer, tm=tiles.token_rows)
        attn = _dilated_attention(proj)
        rec = _hgrn2(proj, lb_all[layer], vec(rec_norm_gain[layer]))
        x = _out_projection(attn, rec, x, gate, vec(attn_norm_gain[layer]), w_out_b, layer,
                            vec(ln_gain[layer, 0]), vec(ln_bias[layer, 0]), alpha=alpha,
                            tm=tiles.light_rows)
        shift, scale, gate = mod(layer, 1)
        j = layer // 2
        if layer % 2 == 0:
            x = _dense_ffn(x, scale, shift, gate, ffn_g[j], ffn_u[j], ffn_d[j],
                           vec(ln_gain[layer, 1]), vec(ln_bias[layer, 1]),
                           alpha=alpha, tm=tiles.token_rows, tc=tiles.ffn_chunk)
        else:
            x = _moe_ffn(x, scale, shift, gate, moe_router[j], moe_w_gate[j], moe_w_up[j],
                         moe_w_down[j], vec(ln_gain[layer, 1]), vec(ln_bias[layer, 1]),
                         alpha=alpha, tm=tiles.token_rows, tm_e=tiles.expert_rows, tf=tiles.expert_ff)
    return x
```

```python
import functools
from typing import NamedTuple

import jax
import jax.numpy as jnp
from jax import lax
from jax.experimental import pallas as pl
from jax.experimental.pallas import tpu as pltpu

F32 = jnp.float32
BF16 = jnp.bfloat16
I32 = jnp.int32

LANES = 128
V7X_VMEM_LIMIT_BYTES = 56 * 1024 * 1024

ATTN_WIDTH = 512
ATTN_HEAD_DIM = 64
DILATIONS = (1, 4, 16)
ATTN_RADIUS = 64
ROPE_THETA = 10000.0
REC_WIDTH = 512
REC_DIM = 128
REC_CHUNK = 64
TOP_K = 2
LN_EPS = 1e-5
RMS_EPS = 1e-6
MASK_VALUE = -1e30
DECAY_EXP_CLAMP = 80.0
LOG2_E = 1.4426950408889634


def _params(semantics):
    return pltpu.CompilerParams(dimension_semantics=semantics,
                                vmem_limit_bytes=V7X_VMEM_LIMIT_BYTES)


def _sigmoid(x):
    return 1.0 / (1.0 + jnp.exp(-x))


def _layer_norm(z, gain, bias):
    mu = jnp.mean(z, axis=-1, keepdims=True)
    zc = z - mu
    var = jnp.mean(zc * zc, axis=-1, keepdims=True)
    return zc * lax.rsqrt(var + LN_EPS) * gain + bias


def _ada_kernel(c_ref, w_ref, b_ref, o_ref):
    c = c_ref[...]
    o_ref[...] = jnp.dot(c * _sigmoid(c), w_ref[...], precision=lax.Precision.HIGHEST,
                         preferred_element_type=F32) + b_ref[...]


def _ada_modulation(c, ada_w, ada_b):
    B, D = c.shape
    n = ada_w.shape[0] * ada_w.shape[1]
    n3 = ada_w.shape[-1]
    tn = D
    return pl.pallas_call(
        _ada_kernel,
        grid=(n, n3 // tn),
        in_specs=[pl.BlockSpec((B, D), lambda i, j: (0, 0)),
                  pl.BlockSpec((None, D, tn), lambda i, j: (i, 0, j)),
                  pl.BlockSpec((None, 1, tn), lambda i, j: (i, 0, j))],
        out_specs=pl.BlockSpec((None, None, B, tn), lambda i, j: (i, j, 0, 0)),
        out_shape=jax.ShapeDtypeStruct((n, n3 // tn, B, tn), F32),
        compiler_params=_params(("arbitrary", "arbitrary")),
        name="ada_modulation",
    )(c, ada_w.reshape(n, D, n3), ada_b.reshape(n, 1, n3))


ROPE_HALF = ATTN_HEAD_DIM // 2


def _qk_lane_order(w_cols):
    lead = w_cols.shape[:-1]
    w = w_cols.reshape(*lead, -1, 2, 2, ROPE_HALF)
    return jnp.swapaxes(w, -3, -2).reshape(*lead, -1)


def _rope_kernel(pos_ref, invf_ref, cos_ref, sin_ref):
    ang = pos_ref[...] * invf_ref[...]
    lane = lax.broadcasted_iota(I32, (1, LANES), 1)
    sin = jnp.sin(ang)
    cos_ref[...] = jnp.cos(ang)
    sin_ref[...] = jnp.where(lane < LANES // 2, -sin, sin)


def _rope_tables(pos_f, inv_freq_lanes, *, tm):
    B, S, _ = pos_f.shape
    tab = pl.BlockSpec((None, tm, LANES), lambda b, i: (b, i, 0))
    return pl.pallas_call(
        _rope_kernel,
        grid=(B, S // tm),
        in_specs=[pl.BlockSpec((None, tm, 1), lambda b, i: (b, i, 0)),
                  pl.BlockSpec((1, LANES), lambda b, i: (0, 0))],
        out_specs=[tab, tab],
        out_shape=[jax.ShapeDtypeStruct((B, S, LANES), F32)] * 2,
        compiler_params=_params(("arbitrary", "arbitrary")),
        name="rope_tables",
    )(pos_f, inv_freq_lanes)


def _inproj_kernel(x_ref, sc_ref, sh_ref, cos_ref, sin_ref, w_ref, o_ref, *, tn):
    u = (x_ref[...] * (1.0 + sc_ref[...]) + sh_ref[...]).astype(BF16)
    cos = cos_ref[...]
    sin_signed = sin_ref[...]
    q_scale = ATTN_HEAD_DIM ** -0.5
    for j in range(w_ref.shape[1] // tn):
        acc = jnp.dot(u, w_ref[:, j * tn:(j + 1) * tn], preferred_element_type=F32)
        if j > 0:
            o_ref[:, j * tn:(j + 1) * tn] = acc
            continue
        for kk in range(tn // LANES):
            c = acc[:, kk * LANES:(kk + 1) * LANES]
            r = c * cos + pltpu.roll(c, LANES // 2, 1) * sin_signed
            if kk * LANES < ATTN_WIDTH:
                r = r * q_scale
            o_ref[:, kk * LANES:(kk + 1) * LANES] = r


def _in_projection(x, scale, shift, rope_cos, rope_sin, w_in_bf16, layer, *, tm):
    B, S, D = x.shape
    n_cols = w_in_bf16.shape[2]
    tn = 2 * ATTN_WIDTH
    assert S % tm == 0 and n_cols % tn == 0
    return pl.pallas_call(
        functools.partial(_inproj_kernel, tn=tn),
        grid=(B, S // tm),
        in_specs=[pl.BlockSpec((None, tm, D), lambda b, i: (b, i, 0)),
                  pl.BlockSpec((None, 1, D), lambda b, i: (b, 0, 0)),
                  pl.BlockSpec((None, 1, D), lambda b, i: (b, 0, 0)),
                  pl.BlockSpec((None, tm, LANES), lambda b, i: (b, i, 0)),
                  pl.BlockSpec((None, tm, LANES), lambda b, i: (b, i, 0)),
                  pl.BlockSpec((None, D, n_cols), lambda b, i: (layer, 0, 0))],
        out_specs=pl.BlockSpec((None, tm, n_cols), lambda b, i: (b, i, 0)),
        out_shape=jax.ShapeDtypeStruct((B, S, n_cols), F32),
        compiler_params=_params(("arbitrary", "arbitrary")),
        name="in_projection",
    )(x, scale, shift, rope_cos, rope_sin, w_in_bf16)


def _attn_kernel(q_ref, k_ref, v_ref, o_ref, qs, ks, vs, qf, kf, vf, nm, nl, na, bias_s, *, S):
    lane = lax.broadcasted_iota(I32, (1, LANES), 1)
    head0 = lane < ATTN_HEAD_DIM
    head0_qk = (lane % ATTN_HEAD_DIM) < ROPE_HALF

    for p, d in enumerate(DILATIONS):
        L = S // d
        tq = min(128, L)
        W = min(2 * tq, L)
        nb = L // tq

        d_prev = DILATIONS[p - 1] if p else 1
        ratio, l_prev = d // d_prev, S // d_prev
        keep_f32 = 0 < p < len(DILATIONS) - 1
        for src, stage, dst in ((q_ref, qf, qs), (k_ref, kf, ks), (v_ref, vf, vs)):
            src = src if p <= 1 else stage
            for r in range(d):
                a, r_prev = divmod(r, d_prev)
                rows = pl.ds(r_prev * l_prev + a, L, stride=ratio) if ratio > 1 else pl.ds(0, L)
                val = src[rows, :]
                if keep_f32:
                    stage[r * L:(r + 1) * L, :] = val
                dst[r * L:(r + 1) * L, :] = val.astype(BF16)

        n_blocks = d * nb
        rc = lax.broadcasted_iota(I32, (tq, W), 0) - lax.broadcasted_iota(I32, (tq, W), 1)
        for which, delta in enumerate((0, ATTN_RADIUS, tq)):
            bias_s[which, pl.ds(0, tq), pl.ds(0, W)] = jnp.where(
                jnp.abs(rc + delta) <= ATTN_RADIUS, 0.0, MASK_VALUE)
        spans, scores, probs, sums = {}, {}, {}, {}

        def score_stage(g, L=L, tq=tq, W=W, nb=nb):
            r, n = divmod(g, nb)
            q0 = r * L + n * tq
            ws = min(max(n * tq - ATTN_RADIUS, 0), L - W)
            qb = qs[q0:q0 + tq, :]
            zero = jnp.zeros_like(qb)
            q2 = jnp.concatenate([jnp.where(head0_qk, qb, zero), jnp.where(head0_qk, zero, qb)], axis=0)
            s = lax.dot_general(q2, ks[r * L + ws:r * L + ws + W, :], (((1,), (1,)), ((), ())),
                                preferred_element_type=F32)
            which = {0: 0, ATTN_RADIUS: 1}.get(n * tq - ws, 2)
            bias = bias_s[which, 0:tq, 0:W]
            scores[g] = s + jnp.concatenate([bias, bias], axis=0)
            rows = pl.ds(r + d * n * tq, tq, stride=d) if d > 1 else pl.ds(n * tq, tq)
            spans[g] = (rows, slice(r * L + ws, r * L + ws + W))

        def softmax_stage(g):
            s = scores.pop(g)
            m = jnp.max(s, axis=1, keepdims=True)
            e = jnp.exp(s - m)
            probs[g] = (m, e.astype(BF16))
            sums[g] = jnp.sum(e, axis=1, keepdims=True)

        def value_stage(g, tq=tq):
            m, e = probs.pop(g)
            l = sums.pop(g)
            rows, keys = spans.pop(g)
            out = jnp.dot(e, vs[keys, :], preferred_element_type=F32)
            nm[p, rows, :] = jnp.where(head0, m[:tq], m[tq:])
            nl[p, rows, :] = jnp.where(head0, l[:tq], l[tq:])
            na[p, rows, :] = jnp.where(head0, out[:tq], out[tq:])

        for step in range(n_blocks + 2):
            if step < n_blocks:
                score_stage(step)
            if 1 <= step <= n_blocks:
                softmax_stage(step - 1)
            if step >= 2:
                value_stage(step - 2)

    m_all = jnp.maximum(jnp.maximum(nm[0], nm[1]), nm[2])
    num = jnp.zeros((S, LANES), F32)
    den = jnp.zeros((S, LANES), F32)
    for p in range(len(DILATIONS)):
        w = jnp.exp(nm[p] - m_all)
        num = num + w * na[p]
        den = den + w * nl[p]
    o_ref[...] = (num / den).astype(o_ref.dtype)


def _dilated_attention(proj):
    B, S, _ = proj.shape
    n_pairs = ATTN_WIDTH // LANES
    assert S % (16 * DILATIONS[-1]) == 0
    blk = lambda off: pl.BlockSpec((None, S, LANES), lambda b, h: (b, 0, off + h))
    return pl.pallas_call(
        functools.partial(_attn_kernel, S=S),
        grid=(B, n_pairs),
        in_specs=[blk(0), blk(n_pairs), blk(2 * n_pairs)],
        out_specs=pl.BlockSpec((None, S, LANES), lambda b, h: (b, 0, h)),
        out_shape=jax.ShapeDtypeStruct((B, S, ATTN_WIDTH), BF16),
        scratch_shapes=[pltpu.VMEM((S, LANES), BF16)] * 3
                       + [pltpu.VMEM((S, LANES), F32)] * 3
                       + [pltpu.VMEM((len(DILATIONS), S, LANES), F32)] * 3
                       + [pltpu.VMEM((3, min(128, S), min(256, S)), F32)],
        compiler_params=_params(("arbitrary", "arbitrary")),
        name="dilated_attention",
    )(proj, proj, proj)


REC_GROUP_CHUNKS = 4
REC_GROUP_ROWS = REC_GROUP_CHUNKS * REC_CHUNK
REC_STAGE_GROUPS = 8


def _hgrn_kernel(rq_ref, zf_ref, zb_ref, ri_ref, rg_ref, lb_ref, gain_ref, o_ref,
                 q_s, vt_s, oi_s, qp_s, ut_s, dd_s, st_s, os_s, *, S):
    C, G, GR = REC_CHUNK, REC_GROUP_CHUNKS, REC_GROUP_ROWS
    nc, ng = S // C, S // GR
    sg = min(REC_STAGE_GROUPS, ng)
    assert ng % sg == 0

    row = lax.broadcasted_iota(I32, (GR, GR), 0)
    col = lax.broadcasted_iota(I32, (GR, GR), 1)
    same_chunk = (row // C) == (col // C)
    t_i = lax.broadcasted_iota(I32, (C, C), 0)
    s_i = lax.broadcasted_iota(I32, (C, C), 1)

    def prep(i, carry):
        r0 = pl.multiple_of(i * GR, GR)
        rq = rq_ref[pl.ds(r0, GR), :]
        q_s[pl.ds(r0, GR), :] = rq * _sigmoid(rq)
        v = ri_ref[pl.ds(r0, GR), :]
        for c in range(G):
            vt_s[i * G + c] = v[c * C:(c + 1) * C, :].T.astype(BF16)
        return carry

    lax.fori_loop(0, ng, prep, 0, unroll=True)

    for direction in range(2):
        fwd = direction == 0
        z_ref = zf_ref if fwd else zb_ref
        cum = jnp.where(same_chunk & ((col <= row) if fwd else (col >= row)), 1.0, 0.0).astype(BF16)
        keep = (s_i <= t_i) if fwd else (s_i >= t_i)
        mid_row = C // 2 - 1 if fwd else C // 2
        last_row = C - 1 if fwd else 0

        def phase_a(it, carry, fwd=fwd, z_ref=z_ref, cum=cum, keep=keep,
                    mid_row=mid_row, last_row=last_row, direction=direction):
            lb = lb_ref[direction:direction + 1, :]
            log_lb = jnp.log(lb)
            log_1m_lb = jnp.log1p(-lb)
            r0s, c0s, k3s, bcs, atts, stage3 = [], [], [], [], [], []
            for u in range(sg):
                i = it * sg + u
                r0 = pl.multiple_of(i * GR, GR)
                z = z_ref[pl.ds(r0, GR), :]
                e = jnp.exp(-jnp.abs(z))
                t = log_1m_lb + (jnp.minimum(z, 0.0) - jnp.log(1.0 + e))
                g = jnp.maximum(log_lb, t) + jnp.log(1.0 + jnp.exp(-jnp.abs(log_lb - t)))
                kk = (1.0 - lb) * (jnp.where(z > 0, e, 1.0) / (1.0 + e))
                g = g * LOG2_E
                g1 = g.astype(BF16)
                r1 = g - g1.astype(F32)
                g2 = r1.astype(BF16)
                g3 = (r1 - g2.astype(F32)).astype(BF16)
                bcs.append(jnp.dot(cum, g1, preferred_element_type=F32)
                           + jnp.dot(cum, g2, preferred_element_type=F32)
                           + jnp.dot(cum, g3, preferred_element_type=F32))
                k3s.append(kk.reshape(G, C, REC_DIM))
                r0s.append(r0)
                c0s.append(pl.multiple_of(i * G, G))
            for u in range(sg):
                b3 = bcs[u].reshape(G, C, REC_DIM)
                mid = b3[:, mid_row:mid_row + 1, :]
                last = b3[:, last_row:last_row + 1, :]
                e3 = b3 - mid
                q3 = q_s[pl.ds(r0s[u], GR), :].reshape(G, C, REC_DIM)
                clamp = DECAY_EXP_CLAMP * LOG2_E
                qt = (q3 * jnp.exp2(jnp.minimum(e3, clamp))).astype(BF16)
                kt = (k3s[u] * jnp.exp2(jnp.minimum(-e3, clamp))).astype(BF16)
                atts.append(jnp.einsum('gtc,gsc->gts', qt, kt, preferred_element_type=F32))
                qp_s[pl.ds(r0s[u], GR), :] = (q3 * jnp.exp2(b3)).astype(BF16).reshape(GR, REC_DIM)
                stage3.append(((k3s[u] * jnp.exp2(last - b3)).astype(BF16), jnp.exp2(last)))
            for u in range(sg):
                att = jnp.where(keep[None], atts[u], 0.0).astype(BF16)
                v3 = ri_ref[pl.ds(r0s[u], GR), :].reshape(G, C, REC_DIM).astype(BF16)
                oi = jnp.einsum('gts,gsv->gtv', att, v3, preferred_element_type=F32)
                oi_s[pl.ds(r0s[u], GR), :] = oi.reshape(GR, REC_DIM)
                kp, decay = stage3[u]
                ut_s[pl.ds(c0s[u], G)] = jnp.einsum('gvs,gsc->gvc', vt_s[pl.ds(c0s[u], G)], kp,
                                                    preferred_element_type=F32)
                dd_s[pl.ds(c0s[u], G)] = decay
            return carry

        lax.fori_loop(0, ng // sg, phase_a, 0)

        def scan(c, st, fwd=fwd):
            idx = c if fwd else nc - 1 - c
            st_s[idx] = st.astype(BF16)
            return dd_s[idx] * st + ut_s[idx]

        lax.fori_loop(0, nc, scan, jnp.zeros((REC_DIM, REC_DIM), F32), unroll=True)

        def phase_c(it, carry, fwd=fwd):
            r0s = [pl.multiple_of((it * sg + u) * GR, GR) for u in range(sg)]
            oos = []
            for u in range(sg):
                c0 = pl.multiple_of((it * sg + u) * G, G)
                qp3 = qp_s[pl.ds(r0s[u], GR), :].reshape(G, C, REC_DIM)
                oos.append(jnp.einsum('gtc,gvc->gtv', qp3, st_s[pl.ds(c0, G)],
                                      preferred_element_type=F32))
            for u in range(sg):
                rows = pl.ds(r0s[u], GR)
                tot = oi_s[rows, :] + oos[u].reshape(GR, REC_DIM)
                os_s[rows, :] = tot if fwd else os_s[rows, :] + tot
            return carry

        lax.fori_loop(0, ng // sg, phase_c, 0)

    o = os_s[...]
    o = o * lax.rsqrt(jnp.mean(o * o, axis=-1, keepdims=True) + RMS_EPS) * gain_ref[...]
    o_ref[...] = (o * _sigmoid(rg_ref[...])).astype(o_ref.dtype)


def _hgrn2(proj, lb, rec_gain):
    B, S, _ = proj.shape
    nh = REC_WIDTH // REC_DIM
    assert S % REC_GROUP_ROWS == 0
    col0 = 3 * ATTN_WIDTH // LANES
    blk = lambda k: pl.BlockSpec((None, S, REC_DIM), lambda b, h: (b, 0, col0 + k * nh + h))
    nc = S // REC_CHUNK
    return pl.pallas_call(
        functools.partial(_hgrn_kernel, S=S),
        grid=(B, nh),
        in_specs=[blk(0), blk(1), blk(2), blk(3), blk(4),
                  pl.BlockSpec((2, REC_DIM), lambda b, h: (0, h)),
                  pl.BlockSpec((1, REC_DIM), lambda b, h: (0, h))],
        out_specs=pl.BlockSpec((None, S, REC_DIM), lambda b, h: (b, 0, h)),
        out_shape=jax.ShapeDtypeStruct((B, S, REC_WIDTH), BF16),
        scratch_shapes=[pltpu.VMEM((S, REC_DIM), F32),
                        pltpu.VMEM((nc, REC_DIM, REC_CHUNK), BF16),
                        pltpu.VMEM((S, REC_DIM), F32),
                        pltpu.VMEM((S, REC_DIM), BF16),
                        pltpu.VMEM((nc, REC_DIM, REC_DIM), F32),
                        pltpu.VMEM((nc, 1, REC_DIM), F32),
                        pltpu.VMEM((nc, REC_DIM, REC_DIM), BF16),
                        pltpu.VMEM((S, REC_DIM), F32)],
        compiler_params=_params(("arbitrary", "arbitrary")),
        name="hgrn2",
    )(proj, proj, proj, proj, proj, lb, rec_gain)


def _outproj_kernel(a_ref, r_ref, x_ref, gate_ref, ag_ref, w_ref, lg_ref, lb_ref, o_ref, *, alpha):
    a = a_ref[...].astype(F32)
    an = a * lax.rsqrt(jnp.mean(a * a, axis=-1, keepdims=True) + RMS_EPS) * ag_ref[...]
    y = (jnp.dot(an.astype(BF16), w_ref[:ATTN_WIDTH, :], preferred_element_type=F32)
         + jnp.dot(r_ref[...], w_ref[ATTN_WIDTH:, :], preferred_element_type=F32))
    z = alpha * x_ref[...] + (1.0 + gate_ref[...]) * y
    o_ref[...] = _layer_norm(z, lg_ref[...], lb_ref[...])


def _out_projection(attn, rec, x, gate, attn_gain, w_out_bf16, layer, ln_g, ln_b, *, alpha, tm):
    B, S, D = x.shape
    row = lambda w: pl.BlockSpec((None, tm, w), lambda b, i: (b, i, 0))
    vec = lambda w: pl.BlockSpec((1, w), lambda b, i: (0, 0))
    return pl.pallas_call(
        functools.partial(_outproj_kernel, alpha=alpha),
        grid=(B, S // tm),
        in_specs=[row(ATTN_WIDTH), row(REC_WIDTH), row(D),
                  pl.BlockSpec((None, 1, D), lambda b, i: (b, 0, 0)),
                  vec(ATTN_WIDTH),
                  pl.BlockSpec((None, ATTN_WIDTH + REC_WIDTH, D), lambda b, i: (layer, 0, 0)),
                  vec(D), vec(D)],
        out_specs=row(D),
        out_shape=jax.ShapeDtypeStruct((B, S, D), F32),
        compiler_params=_params(("arbitrary", "arbitrary")),
        name="out_projection",
    )(attn, rec, x, gate, attn_gain, w_out_bf16, ln_g, ln_b)


def _ffn_kernel(x_ref, sc_ref, sh_ref, gate_ref, wg_ref, wu_ref, wd_ref, lg_ref, lb_ref, o_ref,
                *, alpha, tc):
    x = x_ref[...]
    u = (x * (1.0 + sc_ref[...]) + sh_ref[...]).astype(BF16)
    acc = jnp.zeros(x.shape, F32)
    for c in range(wg_ref.shape[1] // tc):
        cols = slice(c * tc, (c + 1) * tc)
        g = jnp.dot(u, wg_ref[:, cols], preferred_element_type=F32)
        up = jnp.dot(u, wu_ref[:, cols], preferred_element_type=F32)
        h = (g * _sigmoid(g) * up).astype(BF16)
        acc = acc + jnp.dot(h, wd_ref[cols, :], preferred_element_type=F32)
    z = alpha * x + (1.0 + gate_ref[...]) * acc
    o_ref[...] = _layer_norm(z, lg_ref[...], lb_ref[...])


def _dense_ffn(x, scale, shift, gate, wg, wu, wd, ln_g, ln_b, *, alpha, tm, tc):
    B, S, D = x.shape
    ffp = wg.shape[1]
    assert ffp % tc == 0
    row = pl.BlockSpec((None, tm, D), lambda b, i: (b, i, 0))
    mod = pl.BlockSpec((None, 1, D), lambda b, i: (b, 0, 0))
    vec = pl.BlockSpec((1, D), lambda b, i: (0, 0))
    return pl.pallas_call(
        functools.partial(_ffn_kernel, alpha=alpha, tc=tc),
        grid=(B, S // tm),
        in_specs=[row, mod, mod, mod,
                  pl.BlockSpec((D, ffp), lambda b, i: (0, 0)),
                  pl.BlockSpec((D, ffp), lambda b, i: (0, 0)),
                  pl.BlockSpec((ffp, D), lambda b, i: (0, 0)),
                  vec, vec],
        out_specs=row,
        out_shape=jax.ShapeDtypeStruct((B, S, D), F32),
        compiler_params=_params(("arbitrary", "arbitrary")),
        name="dense_ffn",
    )(x, scale, shift, gate, wg, wu, wd, ln_g, ln_b)


META_E0, META_E1, META_W0, META_W1, META_R0, META_R1 = range(6)
SEGMENT_ALIGN = 16
DISPATCH_ROWS = 256
DISPATCH_HALF = DISPATCH_ROWS // 2
DISPATCH_SLACK = DISPATCH_ROWS


def _router_kernel(x_ref, sc_ref, sh_ref, wr_ref, u_ref, meta_ref, carry_ref, *, n_exp):
    tm = x_ref.shape[0]

    @pl.when((pl.program_id(0) == 0) & (pl.program_id(1) == 0))
    def _():
        carry_ref[...] = jnp.zeros_like(carry_ref)

    u = x_ref[...] * (1.0 + sc_ref[...]) + sh_ref[...]
    u_ref[...] = u.astype(BF16)
    u_hi = u.astype(BF16)
    u_lo = (u - u_hi.astype(F32)).astype(BF16)
    logits = (jnp.dot(u_hi, wr_ref[0], preferred_element_type=F32)
              + jnp.dot(u_lo, wr_ref[0], preferred_element_type=F32)
              + jnp.dot(u_hi, wr_ref[1], preferred_element_type=F32))
    lane = lax.broadcasted_iota(I32, (tm, LANES), 1)
    neg = -jnp.inf
    l1 = jnp.where(lane < n_exp, logits, neg)
    m1 = jnp.max(l1, axis=1, keepdims=True)
    i1 = jnp.min(jnp.where(l1 == m1, lane, LANES), axis=1, keepdims=True)
    l2 = jnp.where(lane == i1, neg, l1)
    m2 = jnp.max(l2, axis=1, keepdims=True)
    i2 = jnp.min(jnp.where(l2 == m2, lane, LANES), axis=1, keepdims=True)
    e = jnp.exp(m2 - m1)
    w1 = 1.0 / (1.0 + e)
    w2 = e * w1
    sel = jnp.where((lane == i1) | (lane == i2), 1.0, 0.0)
    before = (lax.broadcasted_iota(I32, (tm, tm), 1) < lax.broadcasted_iota(I32, (tm, tm), 0))
    ranks = jnp.dot(jnp.where(before, 1.0, 0.0).astype(BF16), sel.astype(BF16),
                    preferred_element_type=F32) + carry_ref[...]
    r1 = jnp.sum(jnp.where(lane == i1, ranks, 0.0), axis=1, keepdims=True)
    r2 = jnp.sum(jnp.where(lane == i2, ranks, 0.0), axis=1, keepdims=True)
    carry_ref[...] = carry_ref[...] + jnp.sum(sel, axis=0, keepdims=True)
    meta = jnp.zeros((tm, LANES), F32)
    for k, val in ((META_E0, i1.astype(F32)), (META_E1, i2.astype(F32)), (META_W0, w1),
                   (META_W1, w2), (META_R0, r1), (META_R1, r2)):
        meta = jnp.where(lane == k, val, meta)
    meta_ref[...] = meta


def _router(x, scale, shift, w_router_lanes, *, n_exp, tm):
    B, S, D = x.shape
    nt = S // tm
    row = pl.BlockSpec((None, tm, D), lambda b, i: (b, i, 0))
    mod = pl.BlockSpec((None, 1, D), lambda b, i: (b, 0, 0))
    return pl.pallas_call(
        functools.partial(_router_kernel, n_exp=n_exp),
        grid=(B, nt),
        in_specs=[row, mod, mod, pl.BlockSpec((2, D, LANES), lambda b, i: (0, 0, 0))],
        out_specs=[row, pl.BlockSpec((None, tm, LANES), lambda b, i: (b, i, 0))],
        out_shape=[jax.ShapeDtypeStruct((B, S, D), BF16),
                   jax.ShapeDtypeStruct((B, S, LANES), F32)],
        scratch_shapes=[pltpu.VMEM((1, LANES), F32)],
        compiler_params=_params(("arbitrary", "arbitrary")),
        name="moe_router",
    )(x, scale, shift, w_router_lanes)


def _rows_at(hbm, first_row, n_rows):
    return hbm.at[pl.ds(pl.multiple_of(first_row, SEGMENT_ALIGN), n_rows)]


def _dispatch_kernel(base_ref, cnt_ref, padlo_ref, padn_ref, u_ref, meta_ref, xs_hbm,
                     stage, extra, zeros_buf, sems, extra_sem, *, n_exp):
    t = pl.program_id(0)
    last = pl.num_programs(0) - 1
    tm = u_ref.shape[0]
    R, H = DISPATCH_ROWS, DISPATCH_HALF
    slot = t % 2
    u = u_ref[...]
    e0, e1 = meta_ref[0:1, :], meta_ref[1:2, :]
    d0, d1 = meta_ref[2:3, :], meta_ref[3:4, :]
    srow = lax.broadcasted_iota(I32, (R, tm), 0)

    def main_copies(step, buf, fn):
        for e in range(n_exp):
            base = base_ref[step * n_exp + e]
            fn(pltpu.make_async_copy(stage.at[buf, pl.ds(e * R, H)], _rows_at(xs_hbm, base, H),
                                     sems.at[buf]))

            @pl.when(cnt_ref[step * n_exp + e] > H)
            def _(e=e, base=base):
                fn(pltpu.make_async_copy(stage.at[buf, pl.ds(e * R + H, H)],
                                         _rows_at(xs_hbm, base + H, H), sems.at[buf]))

    rels = []
    for e in range(n_exp):
        rel = jnp.where(e0 == e, d0, jnp.where(e1 == e, d1, -1))
        rels.append(jnp.where(rel >= 0, rel - base_ref[t * n_exp + e], -1))
    onehot = jnp.concatenate([jnp.where(srow == rel, 1.0, 0.0) for rel in rels], axis=0).astype(BF16)
    stage[slot] = jnp.dot(onehot, u, preferred_element_type=F32).astype(BF16)

    @pl.when(t > 0)
    def _():
        main_copies(t - 1, 1 - slot, lambda cp: cp.wait())
    main_copies(t, slot, lambda cp: cp.start())

    for e in range(n_exp):
        for blk in range(1, -(-tm // R)):
            @pl.when(cnt_ref[t * n_exp + e] > blk * R)
            def _(e=e, blk=blk):
                more = jnp.where(srow + blk * R == rels[e], 1.0, 0.0).astype(BF16)
                extra[...] = jnp.dot(more, u, preferred_element_type=F32).astype(BF16)
                cp = pltpu.make_async_copy(extra, _rows_at(xs_hbm, base_ref[t * n_exp + e] + blk * R, R),
                                           extra_sem)
                cp.start()
                cp.wait()

    @pl.when(t == last)
    def _():
        main_copies(t, slot, lambda cp: cp.wait())
        zeros_buf[...] = jnp.zeros_like(zeros_buf)
        pieces = [SEGMENT_ALIGN << b for b in range(((R - 1) // SEGMENT_ALIGN).bit_length())]

        def pad_copies(fn):
            for p in range(padlo_ref.shape[0]):
                lo, n = padlo_ref[p], padn_ref[p]
                whole = n // R

                def body(k, carry, lo=lo):
                    fn(pltpu.make_async_copy(zeros_buf, _rows_at(xs_hbm, lo + k * R, R), extra_sem))
                    return carry
                lax.fori_loop(0, whole, body, 0)
                rem = n - whole * R
                for sz in pieces:
                    off = lo + whole * R + (rem // (2 * sz)) * (2 * sz)
                    pl.when((rem // sz) % 2 == 1)(functools.partial(
                        lambda off, sz: fn(pltpu.make_async_copy(
                            zeros_buf.at[pl.ds(0, sz)], _rows_at(xs_hbm, off, sz), extra_sem)), off, sz))

        pad_copies(lambda cp: cp.start())
        pad_copies(lambda cp: cp.wait())


def _dispatch(u, meta_t, base_te, cnt_te, pad_lo, pad_n, *, n_rows, tm):
    T, D = u.shape
    n_exp = pad_lo.shape[0] - 1
    grid_spec = pltpu.PrefetchScalarGridSpec(
        num_scalar_prefetch=4,
        grid=(T // tm,),
        in_specs=[pl.BlockSpec((tm, D), lambda t, *_: (t, 0)),
                  pl.BlockSpec((meta_t.shape[0], tm), lambda t, *_: (0, t))],
        out_specs=pl.BlockSpec(memory_space=pl.ANY),
        scratch_shapes=[pltpu.VMEM((2, n_exp * DISPATCH_ROWS, D), BF16),
                        pltpu.VMEM((DISPATCH_ROWS, D), BF16),
                        pltpu.VMEM((DISPATCH_ROWS, D), BF16),
                        pltpu.SemaphoreType.DMA((2,)),
                        pltpu.SemaphoreType.DMA(())],
    )
    return pl.pallas_call(
        functools.partial(_dispatch_kernel, n_exp=n_exp),
        grid_spec=grid_spec,
        out_shape=jax.ShapeDtypeStruct((n_rows, D), BF16),
        compiler_params=_params(("arbitrary",)),
        name="moe_dispatch",
    )(base_te, cnt_te, pad_lo, pad_n, u, meta_t)


TILE_FULL, TILE_HALF, TILE_EMPTY = 0, 1, 2


def _expert_kernel(te_ref, na_ref, fill_ref, xs_ref, wg_ref, wu_ref, wd_ref, o_ref, acc_ref):
    i = pl.program_id(0)
    j = pl.program_id(1)
    last = pl.num_programs(1) - 1
    active = i < na_ref[0]
    fill = fill_ref[i]
    tm = acc_ref.shape[0]

    @pl.when(active & (j == 0))
    def _():
        acc_ref[...] = jnp.zeros_like(acc_ref)

    def swiglu(rows):
        xb = xs_ref[rows, :]
        g = jnp.dot(xb, wg_ref[...].astype(BF16), preferred_element_type=F32)
        up = jnp.dot(xb, wu_ref[...].astype(BF16), preferred_element_type=F32)
        h = (g * _sigmoid(g) * up).astype(BF16)
        acc_ref[rows, :] += jnp.dot(h, wd_ref[...].astype(BF16), preferred_element_type=F32)

    pl.when(active & (fill == TILE_FULL))(functools.partial(swiglu, pl.ds(0, tm)))
    pl.when(active & (fill == TILE_HALF))(functools.partial(swiglu, pl.ds(0, tm // 2)))

    @pl.when(active & (j == last))
    def _():
        o_ref[...] = acc_ref[...].astype(BF16)

    @pl.when(jnp.logical_not(active) & (j == last))
    def _():
        o_ref[...] = jnp.zeros_like(o_ref)


def _expert_ffn(xs, tile_expert, n_active, tile_fill, wg, wu, wd, *, tm, tf):
    n_rows, D = xs.shape
    ff = wg.shape[2]
    assert n_rows % tm == 0 and ff % tf == 0
    nj = ff // tf

    def jj(i, j, na):
        return jnp.where(i < na[0], j, nj - 1)

    def ii(i, na):
        return jnp.minimum(i, na[0] - 1)

    grid_spec = pltpu.PrefetchScalarGridSpec(
        num_scalar_prefetch=3,
        grid=(n_rows // tm, nj),
        in_specs=[pl.BlockSpec((tm, D), lambda i, j, te, na, tf_: (ii(i, na), 0)),
                  pl.BlockSpec((None, D, tf), lambda i, j, te, na, tf_: (te[i], 0, jj(i, j, na))),
                  pl.BlockSpec((None, D, tf), lambda i, j, te, na, tf_: (te[i], 0, jj(i, j, na))),
                  pl.BlockSpec((None, tf, D), lambda i, j, te, na, tf_: (te[i], jj(i, j, na), 0))],
        out_specs=pl.BlockSpec((tm, D), lambda i, j, te, na, tf_: (i, 0)),
        scratch_shapes=[pltpu.VMEM((tm, D), F32)],
    )
    return pl.pallas_call(
        _expert_kernel,
        grid_spec=grid_spec,
        out_shape=jax.ShapeDtypeStruct((n_rows, D), BF16),
        compiler_params=_params(("arbitrary", "arbitrary")),
        name="moe_experts",
    )(tile_expert, n_active, tile_fill, xs, wg, wu, wd)


def _combine_kernel(base_ref, off_ref, cnt_ref, ys_hbm, meta_ref, x_ref, gate_ref, lg_ref, lb_ref, o_ref,
                    stage, extra, acc_ref, sems, extra_sem, *, alpha, n_exp):
    t = pl.program_id(0) * pl.num_programs(1) + pl.program_id(1)
    n_steps = pl.num_programs(0) * pl.num_programs(1)
    tm = x_ref.shape[0]
    R = DISPATCH_ROWS
    slot = t % 2

    def main_copies(step, buf, fn):
        for e in range(n_exp):
            fn(pltpu.make_async_copy(_rows_at(ys_hbm, base_ref[step * n_exp + e], R),
                                     stage.at[buf, pl.ds(e * R, R)], sems.at[buf]))

    @pl.when(t == 0)
    def _():
        main_copies(0, 0, lambda cp: cp.start())

    @pl.when(t + 1 < n_steps)
    def _():
        main_copies(t + 1, 1 - slot, lambda cp: cp.start())

    meta = meta_ref[...]
    col = lambda k: meta[:, k:k + 1]
    e0, e1 = col(META_E0).astype(I32), col(META_E1).astype(I32)
    r0, r1 = col(META_R0).astype(I32), col(META_R1).astype(I32)
    w0, w1 = col(META_W0), col(META_W1)
    lane = lax.broadcasted_iota(I32, (tm, R), 1)

    rels, ws, picks = [], [], []
    for e in range(n_exp):
        rel = jnp.where(e0 == e, r0, jnp.where(e1 == e, r1, -1))
        rels.append(jnp.where(rel >= 0, rel - off_ref[t * n_exp + e], -1))
        ws.append(jnp.where(e0 == e, w0, jnp.where(e1 == e, w1, 0.0)))
        picks.append(jnp.where(lane == rels[e], ws[e], 0.0))
    weighted = jnp.concatenate(picks, axis=1).astype(BF16)

    acc_ref[...] = jnp.zeros_like(acc_ref)
    for e in range(n_exp):
        for blk in range(1, -(-tm // R)):
            @pl.when(cnt_ref[t * n_exp + e] > blk * R)
            def _(e=e, blk=blk):
                cp = pltpu.make_async_copy(_rows_at(ys_hbm, base_ref[t * n_exp + e] + blk * R, R),
                                           extra, extra_sem)
                cp.start()
                cp.wait()
                more = jnp.where(lane + blk * R == rels[e], ws[e], 0.0).astype(BF16)
                acc_ref[...] += jnp.dot(more, extra[...], preferred_element_type=F32)

    main_copies(t, slot, lambda cp: cp.wait())
    y = acc_ref[...] + jnp.dot(weighted, stage[slot], preferred_element_type=F32)
    z = alpha * x_ref[...] + (1.0 + gate_ref[...]) * y
    o_ref[...] = _layer_norm(z, lg_ref[...], lb_ref[...])


def _combine(ys, base_te, off_te, cnt_te, meta, x, gate, ln_g, ln_b, *, alpha, tm):
    B, S, D = x.shape
    nt = S // tm
    n_exp = base_te.shape[0] // (B * nt)
    row = pl.BlockSpec((None, tm, D), lambda b, i, *_: (b, i, 0))
    vec = pl.BlockSpec((1, D), lambda b, i, *_: (0, 0))
    grid_spec = pltpu.PrefetchScalarGridSpec(
        num_scalar_prefetch=3,
        grid=(B, nt),
        in_specs=[pl.BlockSpec(memory_space=pl.ANY),
                  pl.BlockSpec((None, tm, LANES), lambda b, i, *_: (b, i, 0)),
                  row, pl.BlockSpec((None, 1, D), lambda b, i, *_: (b, 0, 0)), vec, vec],
        out_specs=row,
        scratch_shapes=[pltpu.VMEM((2, n_exp * DISPATCH_ROWS, D), BF16),
                        pltpu.VMEM((DISPATCH_ROWS, D), BF16),
                        pltpu.VMEM((tm, D), F32),
                        pltpu.SemaphoreType.DMA((2,)),
                        pltpu.SemaphoreType.DMA(())],
    )
    return pl.pallas_call(
        functools.partial(_combine_kernel, alpha=alpha, n_exp=n_exp),
        grid_spec=grid_spec,
        out_shape=jax.ShapeDtypeStruct((B, S, D), F32),
        compiler_params=_params(("arbitrary", "arbitrary")),
        name="moe_combine",
    )(base_te, off_te, cnt_te, ys, meta, x, gate, ln_g, ln_b)


def _moe_ffn(x, scale, shift, gate, w_router, wg, wu, wd, ln_g, ln_b, *, alpha, tm, tm_e, tf):
    B, S, D = x.shape
    T = B * S
    nt = T // tm
    n_exp = w_router.shape[1]
    wr = jnp.zeros((D, LANES), F32).at[:, :n_exp].set(w_router)
    wr_hi = wr.astype(BF16)
    wr = jnp.stack([wr_hi, (wr - wr_hi.astype(F32)).astype(BF16)])
    u, meta = _router(x, scale, shift, wr, n_exp=n_exp, tm=tm)

    meta2 = meta.reshape(T, LANES)
    e0 = meta2[:, META_E0].astype(I32)
    e1 = meta2[:, META_E1].astype(I32)
    r0 = meta2[:, META_R0].astype(I32)
    r1 = meta2[:, META_R1].astype(I32)
    experts = jnp.arange(n_exp, dtype=I32)
    chosen = ((e0[:, None] == experts) | (e1[:, None] == experts)).astype(I32)
    cnt_te = chosen.reshape(nt, tm, n_exp).sum(axis=1)
    seg_te = -(-cnt_te // SEGMENT_ALIGN) * SEGMENT_ALIGN
    off_te = jnp.cumsum(cnt_te, axis=0) - cnt_te
    filled = seg_te.sum(axis=0)
    sizes = -(-(filled + DISPATCH_SLACK) // tm_e) * tm_e
    ends = jnp.cumsum(sizes)
    starts = ends - sizes
    base_te = starts[None, :] + jnp.cumsum(seg_te, axis=0) - seg_te
    shift_te = (base_te - off_te)[:, None, :]

    def rows_of(e, r):
        pick = (e[:, None] == experts).astype(I32).reshape(nt, tm, n_exp)
        return r + (pick * shift_te).sum(axis=-1).reshape(T)

    d0, d1 = rows_of(e0, r0), rows_of(e1, r1)
    bound = TOP_K * T + n_exp * ((SEGMENT_ALIGN - 1) * nt + DISPATCH_SLACK)
    n_tiles = -(-bound // tm_e) + n_exp + 1
    n_active = (ends[-1] // tm_e).astype(I32)
    tile_id = jnp.minimum(jnp.arange(n_tiles, dtype=I32), n_active - 1)
    tile_expert = jnp.sum((ends[None, :] <= (tile_id * tm_e)[:, None]).astype(I32), axis=1)
    data_end = jnp.sum((tile_expert[:, None] == experts) * (starts + filled)[None, :], axis=1)
    left = data_end - tile_id * tm_e
    tile_fill = jnp.where(left <= 0, TILE_EMPTY, jnp.where(left <= tm_e // 2, TILE_HALF, TILE_FULL)).astype(I32)
    meta_t = jnp.stack([e0, e1, d0, d1] + [jnp.zeros_like(e0)] * 4)

    xs = _dispatch(u.reshape(T, D), meta_t, base_te.reshape(-1), cnt_te.reshape(-1),
                   jnp.concatenate([starts + filled, ends[-1:]]),
                   jnp.concatenate([sizes - filled, n_tiles * tm_e - ends[-1:]]),
                   n_rows=n_tiles * tm_e, tm=tm)
    ys = _expert_ffn(xs, tile_expert, n_active.reshape(1), tile_fill, wg, wu, wd, tm=tm_e, tf=tf)
    return _combine(ys, base_te.reshape(-1), off_te.reshape(-1), cnt_te.reshape(-1),
                    meta, x, gate, ln_g, ln_b, alpha=alpha, tm=tm)


class _Tiles(NamedTuple):
    token_rows: int
    light_rows: int
    ffn_chunk: int
    expert_rows: int
    expert_ff: int


def _tile_plan(batch, seq):
    return _Tiles(token_rows=min(512, seq), light_rows=min(2048, seq), ffn_chunk=2 * LANES,
                  expert_rows=min(1024, batch * seq), expert_ff=512)


def kernel(x, c, positions, w_in, w_out, attn_norm_gain, rec_norm_gain, rec_lb_logits, ada_w, ada_b,
           ln_gain, ln_bias, ffn_w_gate, ffn_w_up, ffn_w_down, moe_router, moe_w_gate, moe_w_up,
           moe_w_down):
    B, S, D = x.shape
    depth = w_in.shape[0]
    alpha = (2 * depth) ** 0.25

    p = jax.nn.softmax(rec_lb_logits.astype(F32), axis=0)
    cum = jnp.cumsum(p, axis=0)
    lb_all = cum - cum[0:1]
    half = ATTN_HEAD_DIM // 2
    inv_freq = ROPE_THETA ** (-jnp.arange(half, dtype=F32) / half)
    inv_freq_lanes = jnp.tile(inv_freq, LANES // half).reshape(1, LANES)
    pos_f = positions.astype(F32).reshape(B, S, 1)
    n_qk = 2 * ATTN_WIDTH
    w_in_b = jnp.concatenate([_qk_lane_order(w_in[..., :n_qk]), w_in[..., n_qk:]], axis=-1).astype(BF16)
    w_out_b = w_out.astype(BF16)
    ff = ffn_w_gate.shape[2]
    ffp = -(-ff // (2 * LANES)) * (2 * LANES)
    pad_c = lambda w: jnp.pad(w.astype(BF16), ((0, 0), (0, 0), (0, ffp - ff)))
    ffn_g, ffn_u = pad_c(ffn_w_gate), pad_c(ffn_w_up)
    ffn_d = jnp.pad(ffn_w_down.astype(BF16), ((0, 0), (0, ffp - ff), (0, 0)))

    mods = _ada_modulation(c, ada_w, ada_b)
    tiles = _tile_plan(B, S)
    rope_cos, rope_sin = _rope_tables(pos_f, inv_freq_lanes, tm=tiles.light_rows)

    def mod(layer, sub):
        m = mods[layer * 2 + sub].reshape(3, B, 1, D)
        return m[0], m[1], m[2]

    vec = lambda a: a.reshape(1, -1)
    for layer in range(depth):
        shift, scale, gate = mod(layer, 0)
        proj = _in_projection(x, scale, shift, rope_cos, rope_sin, w_in_b, layer, tm=tiles.token_rows)
        attn = _dilated_attention(proj)
        rec = _hgrn2(proj, lb_all[layer], vec(rec_norm_gain[layer]))
        x = _out_projection(attn, rec, x, gate, vec(attn_norm_gain[layer]), w_out_b, layer,
                            vec(ln_gain[layer, 0]), vec(ln_bias[layer, 0]), alpha=alpha,
                            tm=tiles.light_rows)
        shift, scale, gate = mod(layer, 1)
        j = layer // 2
        if layer % 2 == 0:
            x = _dense_ffn(x, scale, shift, gate, ffn_g[j], ffn_u[j], ffn_d[j],
                           vec(ln_gain[layer, 1]), vec(ln_bias[layer, 1]),
                           alpha=alpha, tm=tiles.token_rows, tc=tiles.ffn_chunk)
        else:
            x = _moe_ffn(x, scale, shift, gate, moe_router[j], moe_w_gate[j], moe_w_up[j],
                         moe_w_down[j], vec(ln_gain[layer, 1]), vec(ln_bias[layer, 1]),
                         alpha=alpha, tm=tiles.token_rows, tm_e=tiles.expert_rows, tf=tiles.expert_ff)
    return x
```
